```python
import jax, jax.numpy as jnp
from jax import lax
import numpy as np

D_MODEL = 1024
BATCH = 8
SEQ = 4096
DEPTH = 1

N_META = 16
LRU_WIDTH = 1024
LRU_HEADS = 8
LRU_BLOCK = LRU_WIDTH // LRU_HEADS
CONV_WIDTH = 4
LRU_C = 8.0
RET_HEADS = 8
RET_QK_DIM = 64
RET_V_DIM = 128
RET_QK_WIDTH = RET_HEADS * RET_QK_DIM
RET_WIDTH = RET_HEADS * RET_V_DIM
CHUNK = 128
ROPE_BASE = 10000.0
MIX_WIDTH = LRU_WIDTH + RET_WIDTH
SPLIT_SIZES = (LRU_WIDTH, LRU_WIDTH, RET_QK_WIDTH, RET_QK_WIDTH, RET_WIDTH, RET_WIDTH)
IN_WIDTH = sum(SPLIT_SIZES)
EPS = 1e-6

kernel_name = 'hymba_style_rglru_retention_hybrid'


def _rmsnorm(x, g):
    xf = x.astype(jnp.float32)
    y = xf * lax.rsqrt(jnp.mean(xf * xf, axis=-1, keepdims=True) + EPS)
    return (y * g.astype(jnp.float32)).astype(x.dtype)


def _causal_conv(x, w, b):
    T = x.shape[1]
    xp = jnp.pad(x, ((0, 0), (CONV_WIDTH - 1, 0), (0, 0)))
    y = b
    for k in range(CONV_WIDTH):
        y = y + xp[:, k:k + T] * w[k]
    return y


def _block_diag(x, w, b):
    B, T, _ = x.shape
    xh = x.reshape(B, T, LRU_HEADS, LRU_BLOCK)
    return jnp.einsum('bthi,hij->bthj', xh, w).reshape(B, T, LRU_WIDTH) + b


def _rg_lru(x, w_rg, b_rg, w_ig, b_ig, lam):
    r = jax.nn.sigmoid(_block_diag(x, w_rg, b_rg).astype(jnp.float32))
    i = jax.nn.sigmoid(_block_diag(x, w_ig, b_ig).astype(jnp.float32))
    log_a = -LRU_C * r * jax.nn.softplus(-lam.astype(jnp.float32))
    a = jnp.exp(log_a)
    beta = jnp.sqrt(-jnp.expm1(2.0 * log_a))
    u = beta * i * x.astype(jnp.float32)

    def combine(lhs, rhs):
        a1, b1 = lhs
        a2, b2 = rhs
        return a1 * a2, a2 * b1 + b2

    _, h = lax.associative_scan(combine, (a, u), axis=1)
    return h.astype(x.dtype)


def _rotary(t, pos):
    half = RET_QK_DIM // 2
    inv = ROPE_BASE ** (-jnp.arange(half, dtype=jnp.float32) / half)
    ang = pos.astype(jnp.float32)[:, None] * inv[None, :]
    cos = jnp.cos(ang)[None, :, None, :]
    sin = jnp.sin(ang)[None, :, None, :]
    t1, t2 = t[..., :half], t[..., half:]
    return jnp.concatenate([t1 * cos - t2 * sin, t1 * sin + t2 * cos], axis=-1)


def _retention(q, k, v):
    B, T, H, _ = q.shape
    pad = (-N_META) % CHUNK
    widths = ((0, 0), (pad, 0), (0, 0), (0, 0))
    q, k, v = jnp.pad(q, widths), jnp.pad(k, widths), jnp.pad(v, widths)
    n = (T + pad) // CHUNK
    q = q.reshape(B, n, CHUNK, H, RET_QK_DIM)
    k = k.reshape(B, n, CHUNK, H, RET_QK_DIM)
    v = v.reshape(B, n, CHUNK, H, RET_V_DIM)
    log_g = jnp.log1p(-jnp.exp2(-5.0 - jnp.arange(RET_HEADS, dtype=jnp.float32)))
    idx = jnp.arange(CHUNK, dtype=jnp.float32)
    diff = idx[:, None] - idx[None, :]
    dmask = jnp.where(diff[None] >= 0.0,
                      jnp.exp(jnp.maximum(diff, 0.0)[None] * log_g[:, None, None]), 0.0)
    s = jnp.einsum('bnchd,bnmhd->bnhcm', q, k) * dmask
    inner = jnp.einsum('bnhcm,bnmhe->bnche', s, v)
    k_dec = k * jnp.exp((CHUNK - 1.0 - idx)[:, None] * log_g[None, :])[:, :, None]
    kv = jnp.einsum('bnchd,bnche->bnhde', k_dec, v)
    g_chunk = jnp.exp(CHUNK * log_g)[None, :, None, None]

    def step(state, kv_n):
        return g_chunk * state + kv_n, state

    init = jnp.zeros((B, H, RET_QK_DIM, RET_V_DIM), jnp.float32)
    _, r_prev = lax.scan(step, init, jnp.moveaxis(kv, 1, 0))
    r_prev = jnp.moveaxis(r_prev, 0, 1)
    q_dec = q * jnp.exp((idx + 1.0)[:, None] * log_g[None, :])[:, :, None]
    cross = jnp.einsum('bnchd,bnhde->bnche', q_dec, r_prev)
    o = (inner + cross).reshape(B, n * CHUNK, H, RET_V_DIM)
    return o[:, pad:]


def _head_norm(o, g):
    mu = jnp.mean(o, axis=-1, keepdims=True)
    oc = o - mu
    var = jnp.mean(oc * oc, axis=-1, keepdims=True)
    return oc * lax.rsqrt(var + EPS) * g.astype(jnp.float32).reshape(RET_HEADS, RET_V_DIM)


def _fwd_setup_inputs(seed: int = 0) -> dict:
    key = jax.random.key(seed)
    ks = jax.random.split(key, 16)
    f32 = jnp.float32
    x = jax.random.normal(ks[0], (BATCH, SEQ, D_MODEL), f32)
    meta_tokens = jax.random.normal(ks[1], (N_META, D_MODEL), f32)
    norm_gain = 1.0 + 0.01 * jax.random.normal(ks[2], (DEPTH, D_MODEL), f32)
    w_in = jax.random.normal(ks[3], (DEPTH, D_MODEL, IN_WIDTH), f32) * D_MODEL ** -0.5
    conv_w = jax.random.normal(ks[4], (DEPTH, CONV_WIDTH, LRU_WIDTH), f32) * CONV_WIDTH ** -0.5
    conv_b = 0.01 * jax.random.normal(ks[5], (DEPTH, LRU_WIDTH), f32)
    w_rg = jax.random.normal(ks[6], (DEPTH, LRU_HEADS, LRU_BLOCK, LRU_BLOCK), f32) * LRU_BLOCK ** -0.5
    b_rg = 0.01 * jax.random.normal(ks[7], (DEPTH, LRU_WIDTH), f32)
    w_ig = jax.random.normal(ks[8], (DEPTH, LRU_HEADS, LRU_BLOCK, LRU_BLOCK), f32) * LRU_BLOCK ** -0.5
    b_ig = 0.01 * jax.random.normal(ks[9], (DEPTH, LRU_WIDTH), f32)
    ac = jax.random.uniform(ks[10], (DEPTH, LRU_WIDTH), f32, minval=0.9, maxval=0.999)
    a = ac ** (1.0 / LRU_C)
    lru_lambda = jnp.log(a) - jnp.log1p(-a)
    ret_norm_gain = 1.0 + 0.01 * jax.random.normal(ks[11], (DEPTH, RET_WIDTH), f32)
    w_out = jax.random.normal(ks[12], (DEPTH, MIX_WIDTH, D_MODEL), f32) * MIX_WIDTH ** -0.5
    final_norm_gain = 1.0 + 0.01 * jax.random.normal(ks[13], (D_MODEL,), f32)
    return {'x': x, 'meta_tokens': meta_tokens, 'norm_gain': norm_gain, 'w_in': w_in,
            'conv_w': conv_w, 'conv_b': conv_b, 'w_rg': w_rg, 'b_rg': b_rg,
            'w_ig': w_ig, 'b_ig': b_ig, 'lru_lambda': lru_lambda,
            'ret_norm_gain': ret_norm_gain, 'w_out': w_out, 'final_norm_gain': final_norm_gain}


def _fwd_reference(x, meta_tokens, norm_gain, w_in, conv_w, conv_b, w_rg, b_rg, w_ig, b_ig,
              lru_lambda, ret_norm_gain, w_out, final_norm_gain):
    B = x.shape[0]
    meta = jnp.broadcast_to(meta_tokens.astype(x.dtype)[None], (B, N_META, D_MODEL))
    h = jnp.concatenate([meta, x], axis=1)
    T = h.shape[1]
    pos = jnp.arange(T)
    split_idx = np.cumsum(SPLIT_SIZES)[:-1].tolist()
    for l in range(DEPTH):
        u = _rmsnorm(h, norm_gain[l])
        proj = jnp.einsum('btd,de->bte', u, w_in[l])
        lru_x, lru_gate, q, k, v, ret_gate = jnp.split(proj, split_idx, axis=-1)
        xc = _causal_conv(lru_x, conv_w[l], conv_b[l])
        y_lru = _rg_lru(xc, w_rg[l], b_rg[l], w_ig[l], b_ig[l], lru_lambda[l]) * jax.nn.silu(lru_gate)
        qh = _rotary(q.reshape(B, T, RET_HEADS, RET_QK_DIM).astype(jnp.float32), pos)
        kh = _rotary(k.reshape(B, T, RET_HEADS, RET_QK_DIM).astype(jnp.float32), pos) * RET_QK_DIM ** -0.5
        vh = v.reshape(B, T, RET_HEADS, RET_V_DIM).astype(jnp.float32)
        o = _head_norm(_retention(qh, kh, vh), ret_norm_gain[l])
        y_ret = o.reshape(B, T, RET_WIDTH).astype(x.dtype) * jax.nn.silu(ret_gate)
        y = jnp.concatenate([y_lru, y_ret], axis=-1)
        h = h + jnp.einsum('bte,ed->btd', y, w_out[l])
    return _rmsnorm(h, final_norm_gain)[:, N_META:]


import jax as _jax
import jax.numpy as _jnp

TWIN_FORMAT = 'train_step'
FWD_PARAMS = ['x', 'meta_tokens', 'norm_gain', 'w_in', 'conv_w', 'conv_b', 'w_rg', 'b_rg', 'w_ig', 'b_ig', 'lru_lambda', 'ret_norm_gain', 'w_out', 'final_norm_gain']
TWIN_WEIGHTS = ['meta_tokens', 'norm_gain', 'w_in', 'conv_w', 'conv_b', 'w_rg', 'b_rg', 'w_ig', 'b_ig', 'lru_lambda', 'ret_norm_gain', 'w_out', 'final_norm_gain']
TWIN_DIFF_INPUT = 'x'
TWIN_INPUTS = ['x', 'meta_tokens', 'norm_gain', 'w_in', 'conv_w', 'conv_b', 'w_rg', 'b_rg', 'w_ig', 'b_ig', 'lru_lambda', 'ret_norm_gain', 'w_out', 'final_norm_gain', 'loss_target', 'm_meta_tokens', 'm_norm_gain', 'm_w_in', 'm_conv_w', 'm_conv_b', 'm_w_rg', 'm_b_rg', 'm_w_ig', 'm_b_ig', 'm_lru_lambda', 'm_ret_norm_gain', 'm_w_out', 'm_final_norm_gain', 'v_meta_tokens', 'v_norm_gain', 'v_w_in', 'v_conv_w', 'v_conv_b', 'v_w_rg', 'v_b_rg', 'v_w_ig', 'v_b_ig', 'v_lru_lambda', 'v_ret_norm_gain', 'v_w_out', 'v_final_norm_gain']
TWIN_OUTPUTS = ['loss', 'grad_x', 'grad_meta_tokens', 'grad_norm_gain', 'grad_w_in', 'grad_conv_w', 'grad_conv_b', 'grad_w_rg', 'grad_b_rg', 'grad_w_ig', 'grad_b_ig', 'grad_lru_lambda', 'grad_ret_norm_gain', 'grad_w_out', 'grad_final_norm_gain', 'delta_meta_tokens', 'delta_norm_gain', 'delta_w_in', 'delta_conv_w', 'delta_conv_b', 'delta_w_rg', 'delta_b_rg', 'delta_w_ig', 'delta_b_ig', 'delta_lru_lambda', 'delta_ret_norm_gain', 'delta_w_out', 'delta_final_norm_gain', 'new_m_meta_tokens', 'new_m_norm_gain', 'new_m_w_in', 'new_m_conv_w', 'new_m_conv_b', 'new_m_w_rg', 'new_m_b_rg', 'new_m_w_ig', 'new_m_b_ig', 'new_m_lru_lambda', 'new_m_ret_norm_gain', 'new_m_w_out', 'new_m_final_norm_gain', 'new_v_meta_tokens', 'new_v_norm_gain', 'new_v_w_in', 'new_v_conv_w', 'new_v_conv_b', 'new_v_w_rg', 'new_v_b_rg', 'new_v_w_ig', 'new_v_b_ig', 'new_v_lru_lambda', 'new_v_ret_norm_gain', 'new_v_w_out', 'new_v_final_norm_gain']
TWIN_LEAF_KINDS = {'loss': 'loss', 'grad_x': 'grad_x', 'grad_meta_tokens': 'grad_w', 'grad_norm_gain': 'grad_w', 'grad_w_in': 'grad_w', 'grad_conv_w': 'grad_w', 'grad_conv_b': 'grad_w', 'grad_w_rg': 'grad_w', 'grad_b_rg': 'grad_w', 'grad_w_ig': 'grad_w', 'grad_b_ig': 'grad_w', 'grad_lru_lambda': 'grad_w', 'grad_ret_norm_gain': 'grad_w', 'grad_w_out': 'grad_w', 'grad_final_norm_gain': 'grad_w', 'delta_meta_tokens': 'delta_w', 'delta_norm_gain': 'delta_w', 'delta_w_in': 'delta_w', 'delta_conv_w': 'delta_w', 'delta_conv_b': 'delta_w', 'delta_w_rg': 'delta_w', 'delta_b_rg': 'delta_w', 'delta_w_ig': 'delta_w', 'delta_b_ig': 'delta_w', 'delta_lru_lambda': 'delta_w', 'delta_ret_norm_gain': 'delta_w', 'delta_w_out': 'delta_w', 'delta_final_norm_gain': 'delta_w', 'new_m_meta_tokens': 'new_m', 'new_m_norm_gain': 'new_m', 'new_m_w_in': 'new_m', 'new_m_conv_w': 'new_m', 'new_m_conv_b': 'new_m', 'new_m_w_rg': 'new_m', 'new_m_b_rg': 'new_m', 'new_m_w_ig': 'new_m', 'new_m_b_ig': 'new_m', 'new_m_lru_lambda': 'new_m', 'new_m_ret_norm_gain': 'new_m', 'new_m_w_out': 'new_m', 'new_m_final_norm_gain': 'new_m', 'new_v_meta_tokens': 'new_v', 'new_v_norm_gain': 'new_v', 'new_v_w_in': 'new_v', 'new_v_conv_w': 'new_v', 'new_v_conv_b': 'new_v', 'new_v_w_rg': 'new_v', 'new_v_b_rg': 'new_v', 'new_v_w_ig': 'new_v', 'new_v_b_ig': 'new_v', 'new_v_lru_lambda': 'new_v', 'new_v_ret_norm_gain': 'new_v', 'new_v_w_out': 'new_v', 'new_v_final_norm_gain': 'new_v'}


def _forward(args):
    return _fwd_reference(*[args[k] for k in FWD_PARAMS])


def _output_shape():
    def fwd():
        inp = _fwd_setup_inputs(0)
        return _fwd_reference(*[inp[k] for k in FWD_PARAMS])
    out = _jax.eval_shape(fwd)
    return out.shape, out.dtype

N_MICROBATCH = 1
ADAM_LR = 0.001
ADAM_B1 = 0.9
ADAM_B2 = 0.999
ADAM_EPS = 1e-08
ADAM_WD = 0.01
ADAM_STEP = 10
PER_EXAMPLE_BATCH_AXIS = {'x': 0, 'loss_target': 0}
SHARED_INPUTS = []
_WEIGHT_DTYPES = {'meta_tokens': _jnp.float32, 'norm_gain': _jnp.float32, 'w_in': _jnp.float32, 'conv_w': _jnp.float32, 'conv_b': _jnp.float32, 'w_rg': _jnp.float32, 'b_rg': _jnp.float32, 'w_ig': _jnp.float32, 'b_ig': _jnp.float32, 'lru_lambda': _jnp.float32, 'ret_norm_gain': _jnp.float32, 'w_out': _jnp.float32, 'final_norm_gain': _jnp.float32}
MOMENT_SCALE = {'meta_tokens': 8.655584e-03, 'norm_gain': 1.715865e-01, 'w_in': 6.625777e-02, 'conv_w': 4.204841e-02, 'conv_b': 4.756717e-01, 'w_rg': 1.312675e-02, 'b_rg': 1.009955e-02, 'w_ig': 2.314484e-02, 'b_ig': 1.400110e-02, 'lru_lambda': 2.035606e-02, 'ret_norm_gain': 6.715678e-02, 'w_out': 7.913576e-02, 'final_norm_gain': 3.199253e+01}


def _to_microbatches(a, axis):
    t = _jnp.moveaxis(a, axis, 0)
    t = t.reshape((N_MICROBATCH, t.shape[0] // N_MICROBATCH) + t.shape[1:])
    return _jnp.moveaxis(t, 1, axis + 1)


def setup_inputs(seed: int = 0) -> dict:
    inp = _fwd_setup_inputs(seed)
    key = _jax.random.fold_in(_jax.random.key(seed), 7919)
    shape, _ = _output_shape()
    out = dict(inp)
    out["loss_target"] = _jax.random.normal(_jax.random.fold_in(key, 0), shape, _jnp.float32)
    for i, name in enumerate(TWIN_WEIGHTS):
        w = inp[name].astype(_jnp.float32)
        if MOMENT_SCALE is None:
            s = _jnp.sqrt(_jnp.mean(_jnp.square(w)) + 1e-30)
        else:
            s = MOMENT_SCALE[name]
        km, kv = _jax.random.split(_jax.random.fold_in(key, i + 1))
        out[name] = w
        out["m_" + name] = s * _jax.random.normal(km, w.shape, _jnp.float32)
        out["v_" + name] = (s * s) * _jax.random.uniform(kv, w.shape, _jnp.float32, 0.5, 1.5)
    if N_MICROBATCH > 1:
        for name, axis in PER_EXAMPLE_BATCH_AXIS.items():
            out[name] = _to_microbatches(out[name], axis)
    return {'x': out['x'], 'meta_tokens': out['meta_tokens'], 'norm_gain': out['norm_gain'], 'w_in': out['w_in'], 'conv_w': out['conv_w'], 'conv_b': out['conv_b'], 'w_rg': out['w_rg'], 'b_rg': out['b_rg'], 'w_ig': out['w_ig'], 'b_ig': out['b_ig'], 'lru_lambda': out['lru_lambda'], 'ret_norm_gain': out['ret_norm_gain'], 'w_out': out['w_out'], 'final_norm_gain': out['final_norm_gain'], 'loss_target': out['loss_target'], 'm_meta_tokens': out['m_meta_tokens'], 'm_norm_gain': out['m_norm_gain'], 'm_w_in': out['m_w_in'], 'm_conv_w': out['m_conv_w'], 'm_conv_b': out['m_conv_b'], 'm_w_rg': out['m_w_rg'], 'm_b_rg': out['m_b_rg'], 'm_w_ig': out['m_w_ig'], 'm_b_ig': out['m_b_ig'], 'm_lru_lambda': out['m_lru_lambda'], 'm_ret_norm_gain': out['m_ret_norm_gain'], 'm_w_out': out['m_w_out'], 'm_final_norm_gain': out['m_final_norm_gain'], 'v_meta_tokens': out['v_meta_tokens'], 'v_norm_gain': out['v_norm_gain'], 'v_w_in': out['v_w_in'], 'v_conv_w': out['v_conv_w'], 'v_conv_b': out['v_conv_b'], 'v_w_rg': out['v_w_rg'], 'v_b_rg': out['v_b_rg'], 'v_w_ig': out['v_w_ig'], 'v_b_ig': out['v_b_ig'], 'v_lru_lambda': out['v_lru_lambda'], 'v_ret_norm_gain': out['v_ret_norm_gain'], 'v_w_out': out['v_w_out'], 'v_final_norm_gain': out['v_final_norm_gain']}


def _loss(weights, diff, rest, loss_target):
    with _jax.named_scope("forward"):
        args = {**rest, TWIN_DIFF_INPUT: diff, **{k: w.astype(_WEIGHT_DTYPES[k]) for k, w in weights.items()}}
        y = _forward(args)
    with _jax.named_scope("loss_head"):
        err = _jnp.square(y.astype(_jnp.float32) - loss_target)
        return 0.5 * _jnp.sum(_jnp.mean(err, axis=-1)) if err.ndim else 0.5 * err


def _adamw(w, g, m, v):
    m = ADAM_B1 * m + (1.0 - ADAM_B1) * g
    v = ADAM_B2 * v + (1.0 - ADAM_B2) * _jnp.square(g)
    m_hat = m / (1.0 - ADAM_B1 ** ADAM_STEP)
    v_hat = v / (1.0 - ADAM_B2 ** ADAM_STEP)
    delta = -ADAM_LR * (m_hat / (_jnp.sqrt(v_hat) + ADAM_EPS) + ADAM_WD * w)
    return delta, m, v


def reference(x, meta_tokens, norm_gain, w_in, conv_w, conv_b, w_rg, b_rg, w_ig, b_ig, lru_lambda, ret_norm_gain, w_out, final_norm_gain, loss_target, m_meta_tokens, m_norm_gain, m_w_in, m_conv_w, m_conv_b, m_w_rg, m_b_rg, m_w_ig, m_b_ig, m_lru_lambda, m_ret_norm_gain, m_w_out, m_final_norm_gain, v_meta_tokens, v_norm_gain, v_w_in, v_conv_w, v_conv_b, v_w_rg, v_b_rg, v_w_ig, v_b_ig, v_lru_lambda, v_ret_norm_gain, v_w_out, v_final_norm_gain):
    given = dict(x=x, meta_tokens=meta_tokens, norm_gain=norm_gain, w_in=w_in, conv_w=conv_w, conv_b=conv_b, w_rg=w_rg, b_rg=b_rg, w_ig=w_ig, b_ig=b_ig, lru_lambda=lru_lambda, ret_norm_gain=ret_norm_gain, w_out=w_out, final_norm_gain=final_norm_gain, loss_target=loss_target, m_meta_tokens=m_meta_tokens, m_norm_gain=m_norm_gain, m_w_in=m_w_in, m_conv_w=m_conv_w, m_conv_b=m_conv_b, m_w_rg=m_w_rg, m_b_rg=m_b_rg, m_w_ig=m_w_ig, m_b_ig=m_b_ig, m_lru_lambda=m_lru_lambda, m_ret_norm_gain=m_ret_norm_gain, m_w_out=m_w_out, m_final_norm_gain=m_final_norm_gain, v_meta_tokens=v_meta_tokens, v_norm_gain=v_norm_gain, v_w_in=v_w_in, v_conv_w=v_conv_w, v_conv_b=v_conv_b, v_w_rg=v_w_rg, v_b_rg=v_b_rg, v_w_ig=v_w_ig, v_b_ig=v_b_ig, v_lru_lambda=v_lru_lambda, v_ret_norm_gain=v_ret_norm_gain, v_w_out=v_w_out, v_final_norm_gain=v_final_norm_gain)
    weights = {n: given[n] for n in TWIN_WEIGHTS}
    shared = {n: given[n] for n in SHARED_INPUTS}
    per_example = {n: given[n] for n in ['x']}
    grad_fn = _jax.value_and_grad(_loss, argnums=(0, 1))

    def one_microbatch(ex, loss_target):
        ex = dict(ex)
        diff = ex.pop(TWIN_DIFF_INPUT)
        return grad_fn(weights, diff, {**shared, **ex}, loss_target)

    if N_MICROBATCH == 1:
        loss, (grad_w, grad_x) = one_microbatch(per_example, given["loss_target"])
    else:
        def body(carry, xs):
            loss_sum, grad_sum = carry
            l_k, (gw_k, gx_k) = one_microbatch(xs[0], xs[1])
            with _jax.named_scope("update"):
                return (loss_sum + l_k, _jax.tree.map(_jnp.add, grad_sum, gw_k)), gx_k

        init = (_jnp.zeros((), _jnp.float32), _jax.tree.map(_jnp.zeros_like, weights))
        (loss, grad_w), grad_x = _jax.lax.scan(body, init, (per_example, given["loss_target"]))
    with _jax.named_scope("update"):
        delta_w, new_m, new_v = {}, {}, {}
        for n in TWIN_WEIGHTS:
            delta_w[n], new_m[n], new_v[n] = _adamw(weights[n], grad_w[n], given["m_" + n], given["v_" + n])
    return (loss, grad_x, *[grad_w[n] for n in TWIN_WEIGHTS], *[delta_w[n] for n in TWIN_WEIGHTS],
            *[new_m[n] for n in TWIN_WEIGHTS], *[new_v[n] for n in TWIN_WEIGHTS])
```

```python
import functools

import jax
import jax.numpy as jnp
from jax import lax
from jax.experimental import pallas as pl
from jax.experimental.pallas import tpu as pltpu

F32 = jnp.float32
BF16 = jnp.bfloat16

N_META = 16
CHUNK = 128
PAD_ROWS = CHUNK - N_META
HEADS = 8
QK_DIM = 64
LANES = 128
SUBLANES = 8
LRU_C = 8.0
EPS = 1e-6
ROPE_BASE = 10000.0
ADAM_LR = 0.001
ADAM_B1 = 0.9
ADAM_B2 = 0.999
ADAM_EPS = 1e-08
ADAM_WD = 0.01
ADAM_STEP = 10
N_CHIPS = 4
N_DEV = 8
TOKEN_TILE = 384
VMEM_LIMIT = 58 * 1024 * 1024
MESH = pl.DeviceIdType.MESH

VMEM_SPEC = pl.BlockSpec(memory_space=pltpu.VMEM)
SMEM_SPEC = pl.BlockSpec(memory_space=pltpu.SMEM)
ANY_SPEC = pl.BlockSpec(memory_space=pl.ANY)

ROW_WR, ROW_WI, ROW_META, ROW_CONV, ROW_VEC, UNIT_ROWS = 0, 128, 256, 272, 276, 288
N_VEC = 7


def _dot(a, b):
    return jnp.dot(a, b, preferred_element_type=F32)


def _dot_nt(a, b):
    return lax.dot_general(a, b, (((1,), (1,)), ((), ())), preferred_element_type=F32)


def _dot_tn(a, b):
    return lax.dot_general(a, b, (((0,), (0,)), ((), ())), preferred_element_type=F32)


def _sigmoid(x):
    return 1.0 / (1.0 + jnp.exp(-x))


def _shift_down(x, prev8, s):
    rolled = pltpu.roll(x, s, 0)
    rows = lax.broadcasted_iota(jnp.int32, (SUBLANES, x.shape[1]), 0)
    top = jnp.where(rows < s, pltpu.roll(prev8, s, 0), rolled[0:SUBLANES])
    return jnp.concatenate([top, rolled[SUBLANES:]], axis=0)


def _shift_up(x, next8, s):
    n = x.shape[0]
    rolled = pltpu.roll(x, n - s, 0)
    rows = lax.broadcasted_iota(jnp.int32, (SUBLANES, x.shape[1]), 0)
    bot = jnp.where(rows >= SUBLANES - s, pltpu.roll(next8, SUBLANES - s, 0), rolled[n - SUBLANES:n])
    return jnp.concatenate([rolled[:n - SUBLANES], bot], axis=0)


def _rot_partner(t):
    w = t.shape[1]
    lane = lax.broadcasted_iota(jnp.int32, t.shape, 1)
    first = (lane % QK_DIM) < (QK_DIM // 2)
    return jnp.where(first, pltpu.roll(t, w - QK_DIM // 2, 1), pltpu.roll(t, QK_DIM // 2, 1))


def _tile_lanes(t, reps):
    return jnp.concatenate([t] * reps, axis=1)


def _gather_weights(w_in, w_out, small):
    r_in, c_in = w_in.shape
    r_out, c_out = w_out.shape
    h_in, h_out = r_in // 2, r_out // 2

    def body(win_ref, wout_ref, small_ref, wg_ref, wo_ref, sg_ref, send_sems, recv_sems):
        x, y, c = lax.axis_index("x"), lax.axis_index("y"), lax.axis_index("c")
        j = 2 * x + y
        sibling = (x, y, 1 - c)
        chips = [(1 - x, y), (x, 1 - y), (1 - x, 1 - y)]

        wg_ref[j] = win_ref[...].astype(BF16)
        wo_ref[j] = wout_ref[...].astype(BF16)
        sg_ref[j] = small_ref[...]

        def half_in(jj, cc):
            return wg_ref.at[jj, pl.ds(cc * h_in, h_in), :]

        def half_out(jj, cc):
            return wo_ref.at[jj, pl.ds(cc * h_out, h_out), :]

        def copy(k, ref, to):
            return pltpu.make_async_remote_copy(src_ref=ref, dst_ref=ref, send_sem=send_sems.at[k],
                                                recv_sem=recv_sems.at[k], device_id=to, device_id_type=MESH)

        first = []
        for k, (cx, cy) in enumerate(chips):
            first.append(copy(k, half_in(j, c), (cx, cy, c)))
            first.append(copy(3 + k, half_out(j, c), (cx, cy, c)))
            first.append(copy(6 + k, sg_ref.at[j], (cx, cy, c)))
        for cp in first:
            cp.start()
        passed = []
        for k, (cx, cy) in enumerate(chips):
            jk = 2 * cx + cy
            copy(k, half_in(jk, c), sibling).wait_recv()
            fwd = copy(9 + k, half_in(jk, c), sibling)
            fwd.start()
            passed.append(fwd)
            copy(3 + k, half_out(jk, c), sibling).wait_recv()
            fwd = copy(12 + k, half_out(jk, c), sibling)
            fwd.start()
            passed.append(fwd)
        for k, (cx, cy) in enumerate(chips):
            jk = 2 * cx + cy
            copy(9 + k, half_in(jk, 1 - c), sibling).wait_recv()
            copy(12 + k, half_out(jk, 1 - c), sibling).wait_recv()
            copy(6 + k, sg_ref.at[jk], sibling).wait_recv()
        for cp in first + passed:
            cp.wait_send()

    return pl.pallas_call(
        body,
        name="gather_weights",
        out_shape=(jax.ShapeDtypeStruct((N_CHIPS, r_in, c_in), BF16),
                   jax.ShapeDtypeStruct((N_CHIPS, r_out, c_out), BF16),
                   jax.ShapeDtypeStruct((N_CHIPS,) + small.shape, F32)),
        in_specs=[VMEM_SPEC, VMEM_SPEC, VMEM_SPEC],
        out_specs=(VMEM_SPEC, VMEM_SPEC, VMEM_SPEC),
        scratch_shapes=[pltpu.SemaphoreType.DMA((15,)), pltpu.SemaphoreType.DMA((15,))],
        compiler_params=pltpu.CompilerParams(vmem_limit_bytes=VMEM_LIMIT),
    )(w_in, w_out, small)


def _proj_segments(d_lru, d_qk, d_ret, chunk_w):
    widths = [d_lru, d_lru, d_qk, d_qk, d_ret, d_ret]
    segs, col = [], 0
    for w in widths:
        parts, off = [], 0
        while off < w:
            jj, inner = divmod(col + off, chunk_w)
            take = min(w - off, chunk_w - inner)
            parts.append((jj, inner, off, take))
            off += take
        segs.append(parts)
        col += w
    return segs


def _in_proj(x2, meta, gain, wg, cos_t, sin_t, tm, d_lru, d_qk, d_ret):
    s_len, d = x2.shape
    tp = s_len + CHUNK
    nt, nb = tp // tm, tm // CHUNK
    segs = _proj_segments(d_lru, d_qk, d_ret, wg.shape[2])
    widths = [d_lru, d_lru, d_qk, d_qk, d_ret, d_ret]

    def body(*refs):
        xb = refs[:nb]
        meta_ref, g_ref, w_ref, cos_ref, sin_ref = refs[nb:nb + 5]
        hp_ref = refs[nb + 5]
        outs = refs[nb + 6:]
        i = pl.program_id(0)
        blocks = [r[...] for r in xb]
        head = jnp.concatenate([jnp.zeros((PAD_ROWS, d), F32), meta_ref[...]], axis=0)
        blocks[0] = jnp.where(i == 0, head, blocks[0])
        h = jnp.concatenate(blocks, axis=0)
        hp_ref[...] = h
        rinv = lax.rsqrt(jnp.mean(h * h, axis=-1, keepdims=True) + EPS)
        u = ((h * rinv) * g_ref[...]).astype(BF16)
        for out_ref, parts in zip(outs, segs):
            for jj, inner, off, take in parts:
                out_ref[:, off:off + take] = _dot(u, w_ref[jj, :, inner:inner + take])
        cos = _tile_lanes(cos_ref[...], d_qk // LANES)
        sin = _tile_lanes(sin_ref[...], d_qk // LANES)
        q = outs[2][...]
        outs[2][...] = q * cos + _rot_partner(q) * sin
        k = outs[3][...]
        outs[3][...] = (k * cos + _rot_partner(k) * sin) * (QK_DIM ** -0.5)

    x_specs = [pl.BlockSpec((CHUNK, d), functools.partial(lambda i, b: (jnp.maximum(i * nb + b - 1, 0), 0), b=b))
               for b in range(nb)]
    tile = lambda w: pl.BlockSpec((tm, w), lambda i: (i, 0))
    return pl.pallas_call(
        body,
        name="in_proj",
        grid=(nt,),
        in_specs=x_specs + [pl.BlockSpec(meta.shape, lambda i: (0, 0)),
                            pl.BlockSpec(gain.shape, lambda i: (0, 0)),
                            pl.BlockSpec(wg.shape, lambda i: (0, 0, 0)),
                            tile(LANES), tile(LANES)],
        out_specs=[tile(d)] + [tile(w) for w in widths],
        out_shape=[jax.ShapeDtypeStruct((tp, d), F32)] + [jax.ShapeDtypeStruct((tp, w), F32) for w in widths],
        compiler_params=pltpu.CompilerParams(dimension_semantics=("arbitrary",), vmem_limit_bytes=VMEM_LIMIT),
    )(*([x2] * nb), meta, gain, wg, cos_t, sin_t)


def _lru_front(lx, prev8, cw_ref, cb_ref, wr_ref, br_ref, wi_ref, bi_ref, lam_ref):
    x1 = _shift_down(lx, prev8, 1)
    x2 = _shift_down(lx, prev8, 2)
    x3 = _shift_down(lx, prev8, 3)
    xc = cb_ref[...] + x3 * cw_ref[0:1, :]
    xc = xc + x2 * cw_ref[1:2, :]
    xc = xc + x1 * cw_ref[2:3, :]
    xc = xc + lx * cw_ref[3:4, :]
    n_heads = wr_ref.shape[0]
    pre_r, pre_i = [], []
    for hd in range(n_heads):
        xh = xc[:, hd * LANES:(hd + 1) * LANES].astype(BF16)
        pre_r.append(_dot(xh, wr_ref[hd].astype(BF16)))
        pre_i.append(_dot(xh, wi_ref[hd].astype(BF16)))
    r = _sigmoid(jnp.concatenate(pre_r, axis=1) + br_ref[...])
    ig = _sigmoid(jnp.concatenate(pre_i, axis=1) + bi_ref[...])
    z = -lam_ref[...]
    e = jnp.exp(-jnp.abs(z))
    e1 = 1.0 + e
    log1p_e = jnp.where(e1 == 1.0, e, jnp.log(e1) * (e / (e1 - 1.0)))
    sp = jnp.maximum(z, 0.0) + log1p_e
    log_a = (-LRU_C * r) * sp
    a = jnp.exp(log_a)
    zz = -2.0 * log_a
    series = zz * (1.0 - zz * (0.5 - zz * (1.0 / 6.0 - zz * (1.0 / 24.0))))
    beta = jnp.sqrt(jnp.where(zz < 0.03125, series, 1.0 - jnp.exp(-zz)))
    return dict(x1=x1, x2=x2, x3=x3, xc=xc, r=r, ig=ig, sp=sp, a=a, beta=beta)


def _lru_fwd(lx, lg, cw, cb, wr, br, wi, bi, lam, tm):
    tp, w = lx.shape
    nt = tp // tm
    per8 = tm // SUBLANES

    def body(lx_ref, lxp_ref, lg_ref, cw_ref, cb_ref, wr_ref, br_ref, wi_ref, bi_ref, lam_ref,
             hl_ref, y_ref, a_s, u_s, carry):
        i = pl.program_id(0)

        @pl.when(i == 0)
        def _():
            carry[...] = jnp.zeros_like(carry)

        lxv = lx_ref[...]
        prev8 = jnp.where(i == 0, 0.0, lxp_ref[...])
        f = _lru_front(lxv, prev8, cw_ref, cb_ref, wr_ref, br_ref, wi_ref, bi_ref, lam_ref)
        row = lax.broadcasted_iota(jnp.int32, (tm, 1), 0) + i * tm
        a_s[...] = f["a"]
        u_s[...] = jnp.where(row >= PAD_ROWS, f["beta"] * f["ig"] * f["xc"], 0.0)

        def step(t, h):
            h = a_s[pl.ds(t, 1), :] * h + u_s[pl.ds(t, 1), :]
            hl_ref[pl.ds(t, 1), :] = h
            return h

        carry[0:1, :] = lax.fori_loop(0, tm, step, carry[0:1, :], unroll=8)
        g = lg_ref[...]
        y_ref[...] = (hl_ref[...] * (g * _sigmoid(g))).astype(BF16)

    tile = pl.BlockSpec((tm, w), lambda i: (i, 0))
    prev = pl.BlockSpec((SUBLANES, w), lambda i: (jnp.maximum(i * per8 - 1, 0), 0))
    vec = pl.BlockSpec((1, w), lambda i: (0, 0))
    mat = pl.BlockSpec(wr.shape, lambda i: (0, 0, 0))
    return pl.pallas_call(
        body,
        name="lru_fwd",
        grid=(nt,),
        in_specs=[tile, prev, tile, pl.BlockSpec(cw.shape, lambda i: (0, 0)), vec, mat, vec, mat, vec, vec],
        out_specs=[tile, tile],
        out_shape=[jax.ShapeDtypeStruct((tp, w), F32), jax.ShapeDtypeStruct((tp, w), BF16)],
        scratch_shapes=[pltpu.VMEM((tm, w), F32), pltpu.VMEM((tm, w), F32), pltpu.VMEM((SUBLANES, w), F32)],
        compiler_params=pltpu.CompilerParams(dimension_semantics=("arbitrary",), vmem_limit_bytes=VMEM_LIMIT),
    )(lx, lx, lg, cw, cb, wr, br, wi, bi, lam)


def _ret_tables():
    log_g = jnp.log1p(-jnp.exp2(-5.0 - jnp.arange(HEADS, dtype=F32)))
    idx = jnp.arange(CHUNK, dtype=F32)
    diff = idx[:, None] - idx[None, :]
    dmask = jnp.where(diff[None] >= 0.0, jnp.exp(jnp.maximum(diff, 0.0)[None] * log_g[:, None, None]), 0.0)
    kdec = jnp.repeat(jnp.exp((CHUNK - 1.0 - idx)[:, None] * log_g[None, :]), QK_DIM, axis=1)
    qdec = jnp.repeat(jnp.exp((idx + 1.0)[:, None] * log_g[None, :]), QK_DIM, axis=1)
    g_chunk = jnp.exp(CHUNK * log_g)
    g_rows = jnp.repeat(g_chunk, QK_DIM).reshape(HEADS // 2, 2 * QK_DIM, 1)
    g_state = jnp.broadcast_to(g_rows, (HEADS // 2, 2 * QK_DIM, 2 * LANES))
    r_head = jnp.arange(2 * QK_DIM)[:, None] // QK_DIM
    c_head = jnp.arange(2 * LANES)[None, :] // LANES
    block_diag = (r_head == c_head).astype(F32)
    return dmask, qdec, kdec, g_state, block_diag


def _head_norm(o_h):
    mu = jnp.mean(o_h, axis=-1, keepdims=True)
    oc = o_h - mu
    var = jnp.mean(oc * oc, axis=-1, keepdims=True)
    rstd = lax.rsqrt(var + EPS)
    return oc * rstd, rstd


def _ret_fwd(q, k, v, rg, gain, tables):
    tp, d_qk = q.shape
    d_ret = v.shape[1]
    n_ch = tp // CHUNK
    n_pairs = HEADS // 2
    dmask, qdec, kdec, g_state, block_diag = tables

    def body(q_ref, k_ref, v_ref, rg_ref, gain_ref, dm_ref, qd_ref, kd_ref, gs_ref, bd_ref,
             o_ref, y_ref, rp_ref, state):
        n = pl.program_id(0)

        @pl.when(n == 0)
        def _():
            state[...] = jnp.zeros_like(state)

        lane = lax.broadcasted_iota(jnp.int32, (CHUNK, LANES), 1)
        for p in range(n_pairs):
            qs = slice(p * LANES, (p + 1) * LANES)
            vs = slice(p * 2 * LANES, (p + 1) * 2 * LANES)
            qp, kp = q_ref[:, qs], k_ref[:, qs]
            vb = v_ref[:, vs].astype(BF16)
            kb = kp.astype(BF16)
            qd = (qp * qd_ref[:, qs]).astype(BF16)
            kd = (kp * kd_ref[:, qs]).astype(BF16)
            st = state[p]
            st_b = st.astype(BF16)
            rp_ref[0, p] = st_b
            cross = _dot(qd, st_b)
            for e in range(2):
                hd = 2 * p + e
                hs = slice(hd * LANES, (hd + 1) * LANES)
                qm = jnp.where((lane // QK_DIM) == e, qp, 0.0).astype(BF16)
                s = _dot_nt(qm, kb) * dm_ref[hd]
                o_h = _dot(s.astype(BF16), vb[:, e * LANES:(e + 1) * LANES]) + cross[:, e * LANES:(e + 1) * LANES]
                o_ref[:, hs] = o_h
                xhat, _ = _head_norm(o_h)
                g = rg_ref[:, hs]
                y_ref[:, hs] = ((xhat * gain_ref[:, hs]) * (g * _sigmoid(g))).astype(BF16)
            state[p] = gs_ref[p] * st + bd_ref[...] * _dot_tn(kd, vb)

    ch = lambda w: pl.BlockSpec((CHUNK, w), lambda n: (n, 0))
    const2 = lambda a: pl.BlockSpec(a.shape, lambda n: (0, 0))
    const3 = lambda a: pl.BlockSpec(a.shape, lambda n: (0, 0, 0))
    return pl.pallas_call(
        body,
        name="ret_fwd",
        grid=(n_ch,),
        in_specs=[ch(d_qk), ch(d_qk), ch(d_ret), ch(d_ret), const2(gain), const3(dmask), const2(qdec), const2(kdec),
                  const3(g_state), const2(block_diag)],
        out_specs=[ch(d_ret), ch(d_ret),
                   pl.BlockSpec((1, n_pairs, 2 * QK_DIM, 2 * LANES), lambda n: (n, 0, 0, 0))],
        out_shape=[jax.ShapeDtypeStruct((tp, d_ret), F32), jax.ShapeDtypeStruct((tp, d_ret), BF16),
                   jax.ShapeDtypeStruct((n_ch, n_pairs, 2 * QK_DIM, 2 * LANES), BF16)],
        scratch_shapes=[pltpu.VMEM((n_pairs, 2 * QK_DIM, 2 * LANES), F32)],
        compiler_params=pltpu.CompilerParams(dimension_semantics=("arbitrary",), vmem_limit_bytes=VMEM_LIMIT),
    )(q, k, v, rg, gain, dmask, qdec, kdec, g_state, block_diag)


def _out_proj_loss(y_lru, y_ret, hp, tgt, wo, gain_f, tm):
    tp, d = hp.shape
    w_lru = y_lru.shape[1]
    w_mix = wo.shape[0]
    nt, nb = tp // tm, tm // CHUNK

    def body(*refs):
        yl_ref, yr_ref, hp_ref = refs[:3]
        tb = refs[3:3 + nb]
        wo_ref, gf_ref = refs[3 + nb:5 + nb]
        dh2_ref, dyl_ref, dyr_ref, dwo_ref, dgf_ref, loss_ref = refs[5 + nb:]
        i = pl.program_id(0)

        @pl.when(i == 0)
        def _():
            dwo_ref[...] = jnp.zeros_like(dwo_ref)
            dgf_ref[...] = jnp.zeros_like(dgf_ref)
            loss_ref[...] = jnp.zeros_like(loss_ref)

        yl, yr = yl_ref[...], yr_ref[...]
        h2 = hp_ref[...] + _dot(yl, wo_ref[0:w_lru, :]) + _dot(yr, wo_ref[w_lru:w_mix, :])
        rinv = lax.rsqrt(jnp.mean(h2 * h2, axis=-1, keepdims=True) + EPS)
        nrm = h2 * rinv
        gf = gf_ref[...]
        tgt_v = jnp.concatenate([r[...] for r in tb], axis=0)
        row = lax.broadcasted_iota(jnp.int32, (tm, 1), 0) + i * tm
        err = jnp.where(row >= CHUNK, nrm * gf - tgt_v, 0.0)
        loss_ref[...] += 0.5 * jnp.sum(jnp.mean(err * err, axis=-1, keepdims=True))
        dout = err * (1.0 / d)
        dgf_ref[...] += jnp.sum(dout * nrm, axis=0, keepdims=True)
        dn = dout * gf
        dh2 = rinv * (dn - nrm * jnp.mean(dn * nrm, axis=-1, keepdims=True))
        dh2_ref[...] = dh2
        dh2b = dh2.astype(BF16)
        dyl_ref[...] = _dot_nt(dh2b, wo_ref[0:w_lru, :])
        dyr_ref[...] = _dot_nt(dh2b, wo_ref[w_lru:w_mix, :])
        dwo_ref[0:w_lru, :] += _dot_tn(yl, dh2b)
        dwo_ref[w_lru:w_mix, :] += _dot_tn(yr, dh2b)

    tile = lambda w: pl.BlockSpec((tm, w), lambda i: (i, 0))
    t_specs = [pl.BlockSpec((CHUNK, d), functools.partial(lambda i, b: (jnp.maximum(i * nb + b - 1, 0), 0), b=b))
               for b in range(nb)]
    return pl.pallas_call(
        body,
        name="out_proj_loss",
        grid=(nt,),
        in_specs=[tile(w_lru), tile(w_mix - w_lru), tile(d)] + t_specs +
                 [pl.BlockSpec(wo.shape, lambda i: (0, 0)), pl.BlockSpec(gain_f.shape, lambda i: (0, 0))],
        out_specs=[tile(d), tile(w_lru), tile(w_mix - w_lru), pl.BlockSpec(wo.shape, lambda i: (0, 0)),
                   pl.BlockSpec((1, d), lambda i: (0, 0)), pl.BlockSpec((SUBLANES, LANES), lambda i: (0, 0))],
        out_shape=[jax.ShapeDtypeStruct((tp, d), F32), jax.ShapeDtypeStruct((tp, w_lru), F32),
                   jax.ShapeDtypeStruct((tp, w_mix - w_lru), F32), jax.ShapeDtypeStruct(wo.shape, F32),
                   jax.ShapeDtypeStruct((1, d), F32), jax.ShapeDtypeStruct((SUBLANES, LANES), F32)],
        compiler_params=pltpu.CompilerParams(dimension_semantics=("arbitrary",), vmem_limit_bytes=VMEM_LIMIT),
    )(y_lru, y_ret, hp, *([tgt] * nb), wo, gain_f)


def _ret_bwd(q, k, v, rg, o, rprev, dy, gain, cos_t, sin_t, tables):
    tp, d_qk = q.shape
    d_ret = v.shape[1]
    n_ch = tp // CHUNK
    n_pairs = HEADS // 2
    dmask, qdec, kdec, g_state, block_diag = tables

    def body(q_ref, k_ref, v_ref, rg_ref, o_ref, rp_ref, dy_ref, gain_ref, cos_ref, sin_ref,
             dm_ref, qd_ref, kd_ref, gs_ref, bd_ref, dq_ref, dk_ref, dv_ref, drg_ref, dgain_ref, dstate):
        n = pl.program_id(0)

        @pl.when(n == 0)
        def _():
            dstate[...] = jnp.zeros_like(dstate)
            dgain_ref[...] = jnp.zeros_like(dgain_ref)

        lane = lax.broadcasted_iota(jnp.int32, (CHUNK, LANES), 1)
        dq_parts, dk_parts = [], []
        for p in range(n_pairs):
            qs = slice(p * LANES, (p + 1) * LANES)
            vs = slice(p * 2 * LANES, (p + 1) * 2 * LANES)
            do_parts = []
            for e in range(2):
                hd = 2 * p + e
                hs = slice(hd * LANES, (hd + 1) * LANES)
                xhat, rstd = _head_norm(o_ref[:, hs])
                g = rg_ref[:, hs]
                sg = _sigmoid(g)
                dyh = dy_ref[:, hs]
                gn = gain_ref[:, hs]
                d_on = dyh * (g * sg)
                drg_ref[:, hs] = (dyh * (xhat * gn) * (sg * (1.0 + g * (1.0 - sg)))).astype(BF16)
                dgain_ref[:, hs] += jnp.sum(d_on * xhat, axis=0, keepdims=True)
                dxh = d_on * gn
                do_parts.append(rstd * (dxh - jnp.mean(dxh, axis=-1, keepdims=True)
                                        - xhat * jnp.mean(dxh * xhat, axis=-1, keepdims=True)))
            do_b = jnp.concatenate(do_parts, axis=1).astype(BF16)
            qp, kp = q_ref[:, qs], k_ref[:, qs]
            vb = v_ref[:, vs].astype(BF16)
            kb = kp.astype(BF16)
            qd = (qp * qd_ref[:, qs]).astype(BF16)
            kd = (kp * kd_ref[:, qs]).astype(BF16)
            dst = dstate[p]
            dst_b = dst.astype(BF16)
            dqp = _dot_nt(do_b, rp_ref[0, p]) * qd_ref[:, qs]
            dkp = _dot_nt(vb, dst_b) * kd_ref[:, qs]
            dvp = _dot(kd, dst_b)
            dv_parts = []
            for e in range(2):
                hd = 2 * p + e
                es = slice(e * LANES, (e + 1) * LANES)
                mine = (lane // QK_DIM) == e
                qm = jnp.where(mine, qp, 0.0).astype(BF16)
                km = jnp.where(mine, kp, 0.0).astype(BF16)
                dm = dm_ref[hd]
                s = (_dot_nt(qm, kb) * dm).astype(BF16)
                ds = (_dot_nt(do_b[:, es], vb[:, es]) * dm).astype(BF16)
                dv_parts.append(dvp[:, es] + _dot_tn(s, do_b[:, es]))
                dqp = dqp + _dot(ds, km)
                dkp = dkp + _dot_tn(ds, qm)
            dv_ref[:, vs] = jnp.concatenate(dv_parts, axis=1).astype(BF16)
            dstate[p] = gs_ref[p] * dst + bd_ref[...] * _dot_tn(qd, do_b)
            dq_parts.append(dqp)
            dk_parts.append(dkp)
        cos = _tile_lanes(cos_ref[...], d_qk // LANES)
        sin = _tile_lanes(sin_ref[...], d_qk // LANES)
        dq = jnp.concatenate(dq_parts, axis=1)
        dk = jnp.concatenate(dk_parts, axis=1) * (QK_DIM ** -0.5)
        dq_ref[...] = (dq * cos + _rot_partner(dq * sin)).astype(BF16)
        dk_ref[...] = (dk * cos + _rot_partner(dk * sin)).astype(BF16)

    last = n_ch - 1
    ch = lambda w: pl.BlockSpec((CHUNK, w), lambda n: (last - n, 0))
    const2 = lambda a: pl.BlockSpec(a.shape, lambda n: (0, 0))
    const3 = lambda a: pl.BlockSpec(a.shape, lambda n: (0, 0, 0))
    return pl.pallas_call(
        body,
        name="ret_bwd",
        grid=(n_ch,),
        in_specs=[ch(d_qk), ch(d_qk), ch(d_ret), ch(d_ret), ch(d_ret),
                  pl.BlockSpec((1, n_pairs, 2 * QK_DIM, 2 * LANES), lambda n: (last - n, 0, 0, 0)),
                  ch(d_ret), const2(gain), ch(LANES), ch(LANES),
                  const3(dmask), const2(qdec), const2(kdec), const3(g_state), const2(block_diag)],
        out_specs=[ch(d_qk), ch(d_qk), ch(d_ret), ch(d_ret), pl.BlockSpec((1, d_ret), lambda n: (0, 0))],
        out_shape=[jax.ShapeDtypeStruct((tp, d_qk), BF16), jax.ShapeDtypeStruct((tp, d_qk), BF16),
                   jax.ShapeDtypeStruct((tp, d_ret), BF16), jax.ShapeDtypeStruct((tp, d_ret), BF16),
                   jax.ShapeDtypeStruct((1, d_ret), F32)],
        scratch_shapes=[pltpu.VMEM((n_pairs, 2 * QK_DIM, 2 * LANES), F32)],
        compiler_params=pltpu.CompilerParams(dimension_semantics=("arbitrary",), vmem_limit_bytes=VMEM_LIMIT),
    )(q, k, v, rg, o, rprev, dy, gain, cos_t, sin_t, dmask, qdec, kdec, g_state, block_diag)


def _lru_bwd(lx, lg, hl, dy, cw, cb, wr, br, wi, bi, lam, tm):
    tp, w = lx.shape
    nt = tp // tm
    per8 = tm // SUBLANES
    n_heads = wr.shape[0]

    def body(lx_ref, lxp_ref, lg_ref, hl_ref, hlp_ref, dy_ref, cw_ref, cb_ref, wr_ref, br_ref, wi_ref, bi_ref, lam_ref,
             dlx_ref, dlg_ref, dcw_ref, dcb_ref, dwr_ref, dbr_ref, dwi_ref, dbi_ref, dlam_ref,
             a_s, g_s, dh_s, carry, dxc_next):
        i = pl.program_id(0)
        first_tile = i == nt - 1

        @pl.when(i == 0)
        def _():
            carry[...] = jnp.zeros_like(carry)
            dxc_next[...] = jnp.zeros_like(dxc_next)
            for r in (dcw_ref, dcb_ref, dwr_ref, dbr_ref, dwi_ref, dbi_ref, dlam_ref):
                r[...] = jnp.zeros_like(r)

        lxv = lx_ref[...]
        prev8 = jnp.where(first_tile, 0.0, lxp_ref[...])
        f = _lru_front(lxv, prev8, cw_ref, cb_ref, wr_ref, br_ref, wi_ref, bi_ref, lam_ref)
        a, beta, r, ig, xc = f["a"], f["beta"], f["r"], f["ig"], f["xc"]
        g = lg_ref[...]
        sg = _sigmoid(g)
        dyv = dy_ref[...]
        hlv = hl_ref[...]
        dlg_ref[...] = (dyv * hlv * (sg * (1.0 + g * (1.0 - sg)))).astype(BF16)
        a_s[...] = a
        g_s[...] = dyv * (g * sg)

        def step(s, cn):
            t = tm - 1 - s
            dh = g_s[pl.ds(t, 1), :] + cn
            dh_s[pl.ds(t, 1), :] = dh
            return a_s[pl.ds(t, 1), :] * dh

        carry[0:1, :] = lax.fori_loop(0, tm, step, carry[0:1, :], unroll=8)
        dh = dh_s[...]
        hprev = _shift_down(hlv, jnp.where(first_tile, 0.0, hlp_ref[...]), 1)
        row = lax.broadcasted_iota(jnp.int32, (tm, 1), 0) + (nt - 1 - i) * tm
        du = jnp.where(row >= PAD_ROWS, dh, 0.0)
        da = dh * hprev
        dbeta = du * ig * xc
        d_ig = du * beta * xc
        dxc = du * beta * ig
        dloga = da * a - dbeta * (a * a / beta)
        lam_v = lam_ref[...]
        dlam_ref[...] += jnp.sum(dloga * r, axis=0, keepdims=True) * (LRU_C * _sigmoid(-lam_v))
        dpr = (dloga * (-LRU_C * f["sp"])) * r * (1.0 - r)
        dpi = d_ig * ig * (1.0 - ig)
        dbr_ref[...] += jnp.sum(dpr, axis=0, keepdims=True)
        dbi_ref[...] += jnp.sum(dpi, axis=0, keepdims=True)
        dxc_parts = []
        for hd in range(n_heads):
            hs = slice(hd * LANES, (hd + 1) * LANES)
            xh = xc[:, hs].astype(BF16)
            dprh = dpr[:, hs].astype(BF16)
            dpih = dpi[:, hs].astype(BF16)
            dwr_ref[hd] += _dot_tn(xh, dprh)
            dwi_ref[hd] += _dot_tn(xh, dpih)
            dxc_parts.append(_dot_nt(dprh, wr_ref[hd].astype(BF16)) + _dot_nt(dpih, wi_ref[hd].astype(BF16)))
        dxc = dxc + jnp.concatenate(dxc_parts, axis=1)
        nxt = dxc_next[...]
        dlx = dxc * cw_ref[3:4, :]
        dlx = dlx + _shift_up(dxc, nxt, 1) * cw_ref[2:3, :]
        dlx = dlx + _shift_up(dxc, nxt, 2) * cw_ref[1:2, :]
        dlx = dlx + _shift_up(dxc, nxt, 3) * cw_ref[0:1, :]
        dlx_ref[...] = dlx.astype(BF16)
        dxc_next[...] = dxc[0:SUBLANES]
        dcb_ref[...] += jnp.sum(dxc, axis=0, keepdims=True)
        dcw_ref[0:1, :] += jnp.sum(dxc * f["x3"], axis=0, keepdims=True)
        dcw_ref[1:2, :] += jnp.sum(dxc * f["x2"], axis=0, keepdims=True)
        dcw_ref[2:3, :] += jnp.sum(dxc * f["x1"], axis=0, keepdims=True)
        dcw_ref[3:4, :] += jnp.sum(dxc * lxv, axis=0, keepdims=True)

    last = nt - 1
    tile = pl.BlockSpec((tm, w), lambda i: (last - i, 0))
    prev = pl.BlockSpec((SUBLANES, w), lambda i: (jnp.maximum((last - i) * per8 - 1, 0), 0))
    vec = pl.BlockSpec((1, w), lambda i: (0, 0))
    mat = pl.BlockSpec(wr.shape, lambda i: (0, 0, 0))
    cwb = pl.BlockSpec(cw.shape, lambda i: (0, 0))
    return pl.pallas_call(
        body,
        name="lru_bwd",
        grid=(nt,),
        in_specs=[tile, prev, tile, tile, prev, tile, cwb, vec, mat, vec, mat, vec, vec],
        out_specs=[tile, tile, cwb, vec, mat, vec, mat, vec, vec],
        out_shape=[jax.ShapeDtypeStruct((tp, w), BF16), jax.ShapeDtypeStruct((tp, w), BF16),
                   jax.ShapeDtypeStruct(cw.shape, F32), jax.ShapeDtypeStruct((1, w), F32),
                   jax.ShapeDtypeStruct(wr.shape, F32), jax.ShapeDtypeStruct((1, w), F32),
                   jax.ShapeDtypeStruct(wr.shape, F32), jax.ShapeDtypeStruct((1, w), F32),
                   jax.ShapeDtypeStruct((1, w), F32)],
        scratch_shapes=[pltpu.VMEM((tm, w), F32), pltpu.VMEM((tm, w), F32), pltpu.VMEM((tm, w), F32),
                        pltpu.VMEM((SUBLANES, w), F32), pltpu.VMEM((SUBLANES, w), F32)],
        compiler_params=pltpu.CompilerParams(dimension_semantics=("arbitrary",), vmem_limit_bytes=VMEM_LIMIT),
    )(lx, lx, lg, hl, hl, dy, cw, cb, wr, br, wi, bi, lam)


def _in_proj_bwd(dparts, hp, dh2, gain, wg, s_len, tm):
    tp, d = hp.shape
    nt = tp // tm
    widths = [p.shape[1] for p in dparts]
    segs = _proj_segments(widths[0], widths[2], widths[4], wg.shape[2])

    def body(*refs):
        dp = refs[:6]
        hp_ref, dh2_ref, g_ref, w_ref = refs[6:10]
        gx_ref, dmeta_ref, dg_ref, dwg_ref = refs[10:14]
        acc, stage, sem = refs[14:]
        i = pl.program_id(0)

        @pl.when(i == 0)
        def _():
            acc[...] = jnp.zeros_like(acc)
            dg_ref[...] = jnp.zeros_like(dg_ref)

        h = hp_ref[...]
        rinv = lax.rsqrt(jnp.mean(h * h, axis=-1, keepdims=True) + EPS)
        nrm = h * rinv
        gv = g_ref[...]
        u = (nrm * gv).astype(BF16)
        du = jnp.zeros((tm, d), F32)
        for p_ref, parts in zip(dp, segs):
            for jj, inner, off, take in parts:
                seg = p_ref[:, off:off + take]
                du = du + _dot_nt(seg, w_ref[jj, :, inner:inner + take])
                acc[jj, :, inner:inner + take] += _dot_tn(u, seg)
        dg_ref[...] += jnp.sum(du * nrm, axis=0, keepdims=True)
        dn = du * gv
        dh = dh2_ref[...] + rinv * (dn - nrm * jnp.mean(dn * nrm, axis=-1, keepdims=True))
        stage[...] = dh

        @pl.when(i == 0)
        def _():
            dmeta_ref[...] = dh[PAD_ROWS:CHUNK]
            cp = pltpu.make_async_copy(stage.at[pl.ds(CHUNK, tm - CHUNK), :], gx_ref.at[pl.ds(0, tm - CHUNK), :], sem)
            cp.start()
            cp.wait()

        @pl.when(i > 0)
        def _():
            start = pl.multiple_of(i * tm - CHUNK, CHUNK)
            cp = pltpu.make_async_copy(stage, gx_ref.at[pl.ds(start, tm), :], sem)
            cp.start()
            cp.wait()

        @pl.when(i == nt - 1)
        def _():
            cp = pltpu.make_async_copy(acc, dwg_ref, sem)
            cp.start()
            cp.wait()

    tile = lambda w: pl.BlockSpec((tm, w), lambda i: (i, 0))
    return pl.pallas_call(
        body,
        name="in_proj_bwd",
        grid=(nt,),
        in_specs=[tile(w) for w in widths] + [tile(d), tile(d), pl.BlockSpec(gain.shape, lambda i: (0, 0)),
                                              pl.BlockSpec(wg.shape, lambda i: (0, 0, 0))],
        out_specs=[ANY_SPEC, pl.BlockSpec((N_META, d), lambda i: (0, 0)), pl.BlockSpec((1, d), lambda i: (0, 0)),
                   ANY_SPEC],
        out_shape=[jax.ShapeDtypeStruct((s_len, d), F32), jax.ShapeDtypeStruct((N_META, d), F32),
                   jax.ShapeDtypeStruct((1, d), F32), jax.ShapeDtypeStruct(wg.shape, F32)],
        scratch_shapes=[pltpu.VMEM(wg.shape, F32), pltpu.VMEM((tm, d), F32), pltpu.SemaphoreType.DMA],
        compiler_params=pltpu.CompilerParams(dimension_semantics=("arbitrary",), vmem_limit_bytes=VMEM_LIMIT),
    )(*dparts, hp, dh2, gain, wg)


def _pair_exchange(bufs):
    n = len(bufs)

    def body(*refs):
        src, dst = refs[:n], refs[n:2 * n]
        send_sems, recv_sems = refs[2 * n:]
        x, y, c = lax.axis_index("x"), lax.axis_index("y"), lax.axis_index("c")
        copies = []
        for b in range(n):
            for jj in range(N_CHIPS):
                copies.append(pltpu.make_async_remote_copy(
                    src_ref=src[b].at[2 * jj + 1 - c], dst_ref=dst[b].at[jj],
                    send_sem=send_sems.at[b * N_CHIPS + jj], recv_sem=recv_sems.at[b * N_CHIPS + jj],
                    device_id=(x, y, 1 - c), device_id_type=MESH))
        for cp in copies:
            cp.start()
        for cp in copies:
            cp.wait()

    return pl.pallas_call(
        body,
        name="grad_pair_exchange",
        in_specs=[ANY_SPEC] * n,
        out_specs=[ANY_SPEC] * n,
        out_shape=[jax.ShapeDtypeStruct((N_CHIPS,) + b.shape[1:], b.dtype) for b in bufs],
        scratch_shapes=[pltpu.SemaphoreType.DMA((n * N_CHIPS,)), pltpu.SemaphoreType.DMA((n * N_CHIPS,))],
    )(*bufs)


def _pair_sum(buf, recv, c_arr, tr, name):
    _, rows, cols = buf.shape

    def body(c_ref, mine_ref, got_ref, out_ref):
        out_ref[...] = (mine_ref[...] + got_ref[...]).astype(BF16)

    grid_spec = pltpu.PrefetchScalarGridSpec(
        num_scalar_prefetch=1,
        grid=(N_CHIPS, rows // tr),
        in_specs=[pl.BlockSpec((1, tr, cols), lambda jj, r, c_ref: (2 * jj + c_ref[0], r, 0)),
                  pl.BlockSpec((1, tr, cols), lambda jj, r, c_ref: (jj, r, 0))],
        out_specs=pl.BlockSpec((1, tr, cols), lambda jj, r, c_ref: (jj, r, 0)),
    )
    return pl.pallas_call(
        body,
        name=name,
        grid_spec=grid_spec,
        out_shape=jax.ShapeDtypeStruct((N_CHIPS, rows, cols), BF16),
    )(c_arr, buf, recv)


def _chip_exchange(bufs):
    n = len(bufs)

    def body(*refs):
        src, dst = refs[:n], refs[n:2 * n]
        send_sems, recv_sems, local_sems = refs[2 * n:]
        x, y, c = lax.axis_index("x"), lax.axis_index("y"), lax.axis_index("c")
        j = 2 * x + y
        chips = [(1 - x, y), (x, 1 - y), (1 - x, 1 - y)]
        local = [pltpu.make_async_copy(src[b].at[j], dst[b].at[j], local_sems.at[b]) for b in range(n)]
        for cp in local:
            cp.start()
        copies = []
        for b in range(n):
            for k, (cx, cy) in enumerate(chips):
                copies.append(pltpu.make_async_remote_copy(
                    src_ref=src[b].at[2 * cx + cy], dst_ref=dst[b].at[j],
                    send_sem=send_sems.at[b * 3 + k], recv_sem=recv_sems.at[b * 3 + k],
                    device_id=(cx, cy, c), device_id_type=MESH))
        for cp in copies:
            cp.start()
        for cp in copies:
            cp.wait()
        for cp in local:
            cp.wait()

    return pl.pallas_call(
        body,
        name="grad_chip_exchange",
        in_specs=[ANY_SPEC] * n,
        out_specs=[ANY_SPEC] * n,
        out_shape=[jax.ShapeDtypeStruct(b.shape, b.dtype) for b in bufs],
        scratch_shapes=[pltpu.SemaphoreType.DMA((n * 3,)), pltpu.SemaphoreType.DMA((n * 3,)),
                        pltpu.SemaphoreType.DMA((n,))],
    )(*bufs)


def _chip_sum(buf, tr, name):
    _, rows, cols = buf.shape

    def body(in_ref, out_ref):
        acc = in_ref[0].astype(F32) + in_ref[1].astype(F32)
        acc = acc + in_ref[2].astype(F32)
        out_ref[...] = acc + in_ref[3].astype(F32)

    return pl.pallas_call(
        body,
        name=name,
        grid=(rows // tr,),
        in_specs=[pl.BlockSpec((N_CHIPS, tr, cols), lambda r: (0, r, 0))],
        out_specs=pl.BlockSpec((tr, cols), lambda r: (r, 0)),
        out_shape=jax.ShapeDtypeStruct((rows, cols), F32),
    )(buf)


def _finish_exchange(f_in, f_out, f_small):
    def body(fin_ref, fout_ref, fs_ref, oin_ref, oout_ref, os_ref, send_sems, recv_sems, local_sems):
        x, y, c = lax.axis_index("x"), lax.axis_index("y"), lax.axis_index("c")
        j = 2 * x + y
        me = 2 * j + c
        sibling = (x, y, 1 - c)
        chips = [(1 - x, y), (x, 1 - y), (1 - x, 1 - y)]
        local = [pltpu.make_async_copy(fin_ref, oin_ref.at[c], local_sems.at[0]),
                 pltpu.make_async_copy(fout_ref, oout_ref.at[c], local_sems.at[1]),
                 pltpu.make_async_copy(fs_ref, os_ref.at[me], local_sems.at[2])]
        for cp in local:
            cp.start()

        def copy(k, src, dst, to):
            return pltpu.make_async_remote_copy(src_ref=src, dst_ref=dst, send_sem=send_sems.at[k],
                                                recv_sem=recv_sems.at[k], device_id=to, device_id_type=MESH)

        first = [copy(0, fin_ref, oin_ref.at[c], sibling), copy(1, fout_ref, oout_ref.at[c], sibling),
                 copy(2, fs_ref, os_ref.at[me], sibling)]
        first += [copy(3 + k, fs_ref, os_ref.at[me], (cx, cy, c)) for k, (cx, cy) in enumerate(chips)]
        for cp in first:
            cp.start()
        passed = []
        for k, (cx, cy) in enumerate(chips):
            unit = 2 * (2 * cx + cy) + c
            copy(3 + k, fs_ref, os_ref.at[unit], sibling).wait_recv()
            fwd = copy(6 + k, os_ref.at[unit], os_ref.at[unit], sibling)
            fwd.start()
            passed.append(fwd)
        copy(0, fin_ref, oin_ref.at[1 - c], sibling).wait_recv()
        copy(1, fout_ref, oout_ref.at[1 - c], sibling).wait_recv()
        copy(2, fs_ref, os_ref.at[2 * j + 1 - c], sibling).wait_recv()
        for k, (cx, cy) in enumerate(chips):
            unit = 2 * (2 * cx + cy) + 1 - c
            copy(6 + k, fs_ref, os_ref.at[unit], sibling).wait_recv()
        for cp in first + passed:
            cp.wait_send()
        for cp in local:
            cp.wait()

    return pl.pallas_call(
        body,
        name="grad_finish_exchange",
        in_specs=[ANY_SPEC] * 3,
        out_specs=[ANY_SPEC] * 3,
        out_shape=[jax.ShapeDtypeStruct((2,) + f_in.shape, F32), jax.ShapeDtypeStruct((2,) + f_out.shape, F32),
                   jax.ShapeDtypeStruct((N_DEV,) + f_small.shape, F32)],
        scratch_shapes=[pltpu.SemaphoreType.DMA((9,)), pltpu.SemaphoreType.DMA((9,)), pltpu.SemaphoreType.DMA((3,))],
    )(f_in, f_out, f_small)


def _adamw_math(w, g, m, v):
    m = ADAM_B1 * m + (1.0 - ADAM_B1) * g
    v = ADAM_B2 * v + (1.0 - ADAM_B2) * (g * g)
    m_hat = m / (1.0 - ADAM_B1 ** ADAM_STEP)
    v_hat = v / (1.0 - ADAM_B2 ** ADAM_STEP)
    delta = -ADAM_LR * (m_hat / (jnp.sqrt(v_hat) + ADAM_EPS) + ADAM_WD * w)
    return delta, m, v


def _adamw_big(w, g, m, v, tr, name):
    rows, cols = w.shape

    def body(w_ref, g_ref, m_ref, v_ref, d_ref, mo_ref, vo_ref):
        d_ref[...], mo_ref[...], vo_ref[...] = _adamw_math(w_ref[...], g_ref[...], m_ref[...], v_ref[...])

    spec = pl.BlockSpec((tr, cols), lambda r: (r, 0))
    return pl.pallas_call(
        body,
        name=name,
        grid=(rows // tr,),
        in_specs=[spec] * 4,
        out_specs=[spec] * 3,
        out_shape=[jax.ShapeDtypeStruct(w.shape, F32)] * 3,
    )(w, g, m, v)


def _adamw_small(j_arr, packed, params):
    names = list(params)
    n = len(names)
    vec_names = ["norm_gain", "conv_b", "b_rg", "b_ig", "lru_lambda", "ret_norm_gain", "final_norm_gain"]

    def body(j_ref, pk_ref, *refs):
        ins = refs[:3 * n]
        outs = refs[3 * n:]
        j = j_ref[0]

        def shard(row, rows):
            return jnp.concatenate([pk_ref[2 * j, row:row + rows, :], pk_ref[2 * j + 1, row:row + rows, :]], axis=1)

        for idx, name in enumerate(names):
            if name == "w_rg":
                g = pk_ref[:, ROW_WR:ROW_WR + LANES, :]
            elif name == "w_ig":
                g = pk_ref[:, ROW_WI:ROW_WI + LANES, :]
            elif name == "meta_tokens":
                g = shard(ROW_META, N_META)
            elif name == "conv_w":
                g = shard(ROW_CONV, 4)
            else:
                row = ROW_VEC + vec_names.index(name)
                g = jnp.concatenate([pk_ref[u, row:row + 1, :] for u in range(N_DEV)], axis=1)
            w_ref, m_ref, v_ref = ins[3 * idx:3 * idx + 3]
            delta, m, v = _adamw_math(w_ref[...], g, m_ref[...], v_ref[...])
            g_ref, d_ref, mo_ref, vo_ref = outs[4 * idx:4 * idx + 4]
            g_ref[...], d_ref[...], mo_ref[...], vo_ref[...] = g, delta, m, v

    flat_in, out_shape = [], []
    for name in names:
        w, m, v = params[name]
        flat_in += [w, m, v]
        out_shape += [jax.ShapeDtypeStruct(w.shape, F32)] * 4
    res = pl.pallas_call(
        body,
        name="adamw_small",
        in_specs=[SMEM_SPEC, VMEM_SPEC] + [VMEM_SPEC] * (3 * n),
        out_specs=[VMEM_SPEC] * (4 * n),
        out_shape=out_shape,
    )(j_arr, packed, *flat_in)
    return {name: tuple(res[4 * idx:4 * idx + 4]) for idx, name in enumerate(names)}


def _units(a):
    rows = a.shape[0]
    return jnp.transpose(a.reshape(rows, N_DEV, LANES), (1, 0, 2))


def kernel(x, meta_tokens, norm_gain, w_in, conv_w, conv_b, w_rg, b_rg, w_ig, b_ig, lru_lambda, ret_norm_gain, w_out, final_norm_gain, loss_target, m_meta_tokens, m_norm_gain, m_w_in, m_conv_w, m_conv_b, m_w_rg, m_b_rg, m_w_ig, m_b_ig, m_lru_lambda, m_ret_norm_gain, m_w_out, m_final_norm_gain, v_meta_tokens, v_norm_gain, v_w_in, v_conv_w, v_conv_b, v_w_rg, v_b_rg, v_w_ig, v_b_ig, v_lru_lambda, v_ret_norm_gain, v_w_out, v_final_norm_gain):
    s_len, d = x.shape[1], x.shape[2]
    d_lru = w_rg.shape[1] * w_rg.shape[2]
    d_ret = ret_norm_gain.shape[1]
    d_qk = HEADS * QK_DIM
    tp = s_len + CHUNK
    tm = TOKEN_TILE
    assert tp % tm == 0 and d_lru == HEADS * LANES and d_ret == HEADS * LANES
    ax, ay, ac = lax.axis_index("x"), lax.axis_index("y"), lax.axis_index("c")
    c_arr = jnp.reshape(ac, (1,)).astype(jnp.int32)
    j_arr = jnp.reshape(2 * ax + ay, (1,)).astype(jnp.int32)

    small = jnp.concatenate([meta_tokens, conv_w[0], jnp.zeros((4, meta_tokens.shape[1]), F32)], axis=0)
    wg, wo4, sg = _gather_weights(w_in[0], w_out[0], small)
    wo = wo4.reshape(N_CHIPS * wo4.shape[1], wo4.shape[2])
    cols = sg.shape[2]
    meta_full = jnp.transpose(sg[:, :N_META, :], (1, 0, 2)).reshape(N_META, N_CHIPS * cols)
    cw_full = jnp.transpose(sg[:, N_META:N_META + 4, :], (1, 0, 2)).reshape(4, N_CHIPS * cols)
    cw8 = jnp.concatenate([cw_full, jnp.zeros((4, cw_full.shape[1]), F32)], axis=0)

    half = QK_DIM // 2
    inv = ROPE_BASE ** (-jnp.arange(half, dtype=F32) / half)
    pos = (jnp.arange(tp) - PAD_ROWS).astype(F32)
    ang = pos[:, None] * inv[None, :]
    cos_t = jnp.tile(jnp.cos(ang), (1, LANES // half))
    sign = jnp.where((jnp.arange(LANES) % QK_DIM) < half, -1.0, 1.0).astype(F32)
    sin_t = jnp.tile(jnp.sin(ang), (1, LANES // half)) * sign[None, :]
    tables = _ret_tables()
    gain_f = final_norm_gain.reshape(1, d)

    hp, lx, lg, q, k, v, rg = _in_proj(x[0], meta_full, norm_gain, wg, cos_t, sin_t, tm, d_lru, d_qk, d_ret)
    hl, y_lru = _lru_fwd(lx, lg, cw8, conv_b, w_rg[0], b_rg, w_ig[0], b_ig, lru_lambda, tm)
    o, y_ret, rprev = _ret_fwd(q, k, v, rg, ret_norm_gain, tables)
    dh2, dy_lru, dy_ret, dwo, dgf, loss_acc = _out_proj_loss(y_lru, y_ret, hp, loss_target[0], wo, gain_f, tm)
    loss = lax.psum(loss_acc[0, 0], ("x", "y", "c"))

    dq, dk, dv, drg, dgain = _ret_bwd(q, k, v, rg, o, rprev, dy_ret, ret_norm_gain, cos_t, sin_t, tables)
    dlx, dlg, dcw, dcb, dwr, dbr, dwi, dbi, dlam = _lru_bwd(lx, lg, hl, dy_lru, cw8, conv_b, w_rg[0], b_rg,
                                                            w_ig[0], b_ig, lru_lambda, tm)
    grad_x, dmeta, dg1, dwg = _in_proj_bwd([dlx, dlg, dq, dk, dv, drg], hp, dh2, norm_gain, wg, s_len, tm)

    g_in = dwg.reshape(N_DEV, dwg.shape[1] // 2, dwg.shape[2])
    g_out = dwo.reshape(N_DEV, dwo.shape[0] // N_DEV, dwo.shape[1])
    vecs = [dg1, dcb, dbr, dbi, dlam, dgain, dgf]
    g_small = jnp.concatenate([dwr, dwi, _units(dmeta), _units(dcw[0:4])] + [_units(a) for a in vecs]
                              + [jnp.zeros((N_DEV, UNIT_ROWS - ROW_VEC - N_VEC, LANES), F32)], axis=1)
    r_in, r_out, r_small = _pair_exchange([g_in, g_out, g_small])
    q_in = _pair_sum(g_in, r_in, c_arr, 128, "grad_pair_sum_in")
    q_out = _pair_sum(g_out, r_out, c_arr, 128, "grad_pair_sum_out")
    q_small = _pair_sum(g_small, r_small, c_arr, UNIT_ROWS, "grad_pair_sum_small")
    e_in, e_out, e_small = _chip_exchange([q_in, q_out, q_small])
    f_in = _chip_sum(e_in, 128, "grad_chip_sum_in")
    f_out = _chip_sum(e_out, 128, "grad_chip_sum_out")
    f_small = _chip_sum(e_small, UNIT_ROWS, "grad_chip_sum_small")
    o_in, o_out, o_small = _finish_exchange(f_in, f_out, f_small)
    grad_w_in = o_in.reshape(w_in.shape[1], w_in.shape[2])
    grad_w_out = o_out.reshape(w_out.shape[1], w_out.shape[2])

    d_w_in, nm_w_in, nv_w_in = _adamw_big(w_in[0], grad_w_in, m_w_in[0], v_w_in[0], 256, "adamw_w_in")
    d_w_out, nm_w_out, nv_w_out = _adamw_big(w_out[0], grad_w_out, m_w_out[0], v_w_out[0], 256, "adamw_w_out")
    small_params = {
        "meta_tokens": (meta_tokens, m_meta_tokens, v_meta_tokens),
        "norm_gain": (norm_gain, m_norm_gain, v_norm_gain),
        "conv_w": (conv_w[0], m_conv_w[0], v_conv_w[0]),
        "conv_b": (conv_b, m_conv_b, v_conv_b),
        "w_rg": (w_rg[0], m_w_rg[0], v_w_rg[0]),
        "b_rg": (b_rg, m_b_rg, v_b_rg),
        "w_ig": (w_ig[0], m_w_ig[0], v_w_ig[0]),
        "b_ig": (b_ig, m_b_ig, v_b_ig),
        "lru_lambda": (lru_lambda, m_lru_lambda, v_lru_lambda),
        "ret_norm_gain": (ret_norm_gain, m_ret_norm_gain, v_ret_norm_gain),
        "final_norm_gain": (gain_f, m_final_norm_gain.reshape(1, d), v_final_norm_gain.reshape(1, d)),
    }
    res = _adamw_small(j_arr, o_small, small_params)
    res["w_in"] = (grad_w_in, d_w_in, nm_w_in, nv_w_in)
    res["w_out"] = (grad_w_out, d_w_out, nm_w_out, nv_w_out)

    order = ["meta_tokens", "norm_gain", "w_in", "conv_w", "conv_b", "w_rg", "b_rg", "w_ig", "b_ig", "lru_lambda",
             "ret_norm_gain", "w_out", "final_norm_gain"]
    shapes = {"w_in": w_in.shape, "conv_w": conv_w.shape, "w_rg": w_rg.shape, "w_ig": w_ig.shape,
              "w_out": w_out.shape, "final_norm_gain": final_norm_gain.shape}
    outs = [loss, grad_x.reshape(x.shape)]
    for kind in range(4):
        for name in order:
            a = res[name][kind]
            outs.append(a.reshape(shapes[name]) if name in shapes else a)
    return tuple(outs)
```

```python
import functools

import jax
import jax.numpy as jnp
from jax import lax
from jax.experimental import pallas as pl
from jax.experimental.pallas import tpu as pltpu

F32 = jnp.float32
BF16 = jnp.bfloat16

N_META = 16
CHUNK = 128
PAD_ROWS = CHUNK - N_META
HEADS = 8
QK_DIM = 64
LANES = 128
SUBLANES = 8
LRU_C = 8.0
EPS = 1e-6
ROPE_BASE = 10000.0
ADAM_LR = 0.001
ADAM_B1 = 0.9
ADAM_B2 = 0.999
ADAM_EPS = 1e-08
ADAM_WD = 0.01
ADAM_STEP = 10
N_CHIPS = 4
N_DEV = 8
TOKEN_TILE = 384
VMEM_LIMIT = 58 * 1024 * 1024
MESH = pl.DeviceIdType.MESH

VMEM_SPEC = pl.BlockSpec(memory_space=pltpu.VMEM)
SMEM_SPEC = pl.BlockSpec(memory_space=pltpu.SMEM)
ANY_SPEC = pl.BlockSpec(memory_space=pl.ANY)

ROW_WR, ROW_WI, ROW_META, ROW_CONV, ROW_VEC, UNIT_ROWS = 0, 128, 256, 272, 276, 288
N_VEC = 7


def _dot(a, b):
    return jnp.dot(a, b, preferred_element_type=F32)


def _dot_nt(a, b):
    return lax.dot_general(a, b, (((1,), (1,)), ((), ())), preferred_element_type=F32)


def _dot_tn(a, b):
    return lax.dot_general(a, b, (((0,), (0,)), ((), ())), preferred_element_type=F32)


def _sigmoid(x):
    return 1.0 / (1.0 + jnp.exp(-x))


def _shift_down(x, prev8, s):
    rolled = pltpu.roll(x, s, 0)
    rows = lax.broadcasted_iota(jnp.int32, (SUBLANES, x.shape[1]), 0)
    top = jnp.where(rows < s, pltpu.roll(prev8, s, 0), rolled[0:SUBLANES])
    return jnp.concatenate([top, rolled[SUBLANES:]], axis=0)


def _shift_up(x, next8, s):
    n = x.shape[0]
    rolled = pltpu.roll(x, n - s, 0)
    rows = lax.broadcasted_iota(jnp.int32, (SUBLANES, x.shape[1]), 0)
    bot = jnp.where(rows >= SUBLANES - s, pltpu.roll(next8, SUBLANES - s, 0), rolled[n - SUBLANES:n])
    return jnp.concatenate([rolled[:n - SUBLANES], bot], axis=0)


def _rot_partner(t):
    w = t.shape[1]
    lane = lax.broadcasted_iota(jnp.int32, t.shape, 1)
    first = (lane % QK_DIM) < (QK_DIM // 2)
    return jnp.where(first, pltpu.roll(t, w - QK_DIM // 2, 1), pltpu.roll(t, QK_DIM // 2, 1))


def _tile_lanes(t, reps):
    return jnp.concatenate([t] * reps, axis=1)


def _gather_weights(w_in, w_out, small):
    r_in, c_in = w_in.shape
    r_out, c_out = w_out.shape
    h_in, h_out = r_in // 2, r_out // 2

    def body(win_ref, wout_ref, small_ref, wg_ref, wo_ref, sg_ref, send_sems, recv_sems):
        x, y, c = lax.axis_index("x"), lax.axis_index("y"), lax.axis_index("c")
        j = 2 * x + y
        sibling = (x, y, 1 - c)
        chips = [(1 - x, y), (x, 1 - y), (1 - x, 1 - y)]

        wg_ref[j] = win_ref[...].astype(BF16)
        wo_ref[j] = wout_ref[...].astype(BF16)
        sg_ref[j] = small_ref[...]

        def half_in(jj, cc):
            return wg_ref.at[jj, pl.ds(cc * h_in, h_in), :]

        def half_out(jj, cc):
            return wo_ref.at[jj, pl.ds(cc * h_out, h_out), :]

        def copy(k, ref, to):
            return pltpu.make_async_remote_copy(src_ref=ref, dst_ref=ref, send_sem=send_sems.at[k],
                                                recv_sem=recv_sems.at[k], device_id=to, device_id_type=MESH)

        first = []
        for k, (cx, cy) in enumerate(chips):
            first.append(copy(k, half_in(j, c), (cx, cy, c)))
            first.append(copy(3 + k, half_out(j, c), (cx, cy, c)))
            first.append(copy(6 + k, sg_ref.at[j], (cx, cy, c)))
        for cp in first:
            cp.start()
        passed = []
        for k, (cx, cy) in enumerate(chips):
            jk = 2 * cx + cy
            copy(k, half_in(jk, c), sibling).wait_recv()
            fwd = copy(9 + k, half_in(jk, c), sibling)
            fwd.start()
            passed.append(fwd)
            copy(3 + k, half_out(jk, c), sibling).wait_recv()
            fwd = copy(12 + k, half_out(jk, c), sibling)
            fwd.start()
            passed.append(fwd)
        for k, (cx, cy) in enumerate(chips):
            jk = 2 * cx + cy
            copy(9 + k, half_in(jk, 1 - c), sibling).wait_recv()
            copy(12 + k, half_out(jk, 1 - c), sibling).wait_recv()
            copy(6 + k, sg_ref.at[jk], sibling).wait_recv()
        for cp in first + passed:
            cp.wait_send()

    return pl.pallas_call(
        body,
        name="gather_weights",
        out_shape=(jax.ShapeDtypeStruct((N_CHIPS, r_in, c_in), BF16),
                   jax.ShapeDtypeStruct((N_CHIPS, r_out, c_out), BF16),
                   jax.ShapeDtypeStruct((N_CHIPS,) + small.shape, F32)),
        in_specs=[VMEM_SPEC, VMEM_SPEC, VMEM_SPEC],
        out_specs=(VMEM_SPEC, VMEM_SPEC, VMEM_SPEC),
        scratch_shapes=[pltpu.SemaphoreType.DMA((15,)), pltpu.SemaphoreType.DMA((15,))],
        compiler_params=pltpu.CompilerParams(vmem_limit_bytes=VMEM_LIMIT),
    )(w_in, w_out, small)


def _proj_segments(d_lru, d_qk, d_ret, chunk_w):
    widths = [d_lru, d_lru, d_qk, d_qk, d_ret, d_ret]
    segs, col = [], 0
    for w in widths:
        parts, off = [], 0
        while off < w:
            jj, inner = divmod(col + off, chunk_w)
            take = min(w - off, chunk_w - inner)
            parts.append((jj, inner, off, take))
            off += take
        segs.append(parts)
        col += w
    return segs


def _in_proj(x2, meta, gain, wg, cos_t, sin_t, tm, d_lru, d_qk, d_ret):
    s_len, d = x2.shape
    tp = s_len + CHUNK
    nt, nb = tp // tm, tm // CHUNK
    segs = _proj_segments(d_lru, d_qk, d_ret, wg.shape[2])
    widths = [d_lru, d_lru, d_qk, d_qk, d_ret, d_ret]

    def body(*refs):
        xb = refs[:nb]
        meta_ref, g_ref, w_ref, cos_ref, sin_ref = refs[nb:nb + 5]
        hp_ref = refs[nb + 5]
        outs = refs[nb + 6:]
        i = pl.program_id(0)
        blocks = [r[...] for r in xb]
        head = jnp.concatenate([jnp.zeros((PAD_ROWS, d), F32), meta_ref[...]], axis=0)
        blocks[0] = jnp.where(i == 0, head, blocks[0])
        h = jnp.concatenate(blocks, axis=0)
        hp_ref[...] = h
        rinv = lax.rsqrt(jnp.mean(h * h, axis=-1, keepdims=True) + EPS)
        u = ((h * rinv) * g_ref[...]).astype(BF16)
        for out_ref, parts in zip(outs, segs):
            for jj, inner, off, take in parts:
                out_ref[:, off:off + take] = _dot(u, w_ref[jj, :, inner:inner + take])
        cos = _tile_lanes(cos_ref[...], d_qk // LANES)
        sin = _tile_lanes(sin_ref[...], d_qk // LANES)
        q = outs[2][...]
        outs[2][...] = q * cos + _rot_partner(q) * sin
        k = outs[3][...]
        outs[3][...] = (k * cos + _rot_partner(k) * sin) * (QK_DIM ** -0.5)

    x_specs = [pl.BlockSpec((CHUNK, d), functools.partial(lambda i, b: (jnp.maximum(i * nb + b - 1, 0), 0), b=b))
               for b in range(nb)]
    tile = lambda w: pl.BlockSpec((tm, w), lambda i: (i, 0))
    return pl.pallas_call(
        body,
        name="in_proj",
        grid=(nt,),
        in_specs=x_specs + [pl.BlockSpec(meta.shape, lambda i: (0, 0)),
                            pl.BlockSpec(gain.shape, lambda i: (0, 0)),
                            pl.BlockSpec(wg.shape, lambda i: (0, 0, 0)),
                            tile(LANES), tile(LANES)],
        out_specs=[tile(d)] + [tile(w) for w in widths],
        out_shape=[jax.ShapeDtypeStruct((tp, d), F32)] + [jax.ShapeDtypeStruct((tp, w), F32) for w in widths],
        compiler_params=pltpu.CompilerParams(dimension_semantics=("arbitrary",), vmem_limit_bytes=VMEM_LIMIT),
    )(*([x2] * nb), meta, gain, wg, cos_t, sin_t)


def _lru_front(lx, prev8, cw_ref, cb_ref, wr_ref, br_ref, wi_ref, bi_ref, lam_ref):
    x1 = _shift_down(lx, prev8, 1)
    x2 = _shift_down(lx, prev8, 2)
    x3 = _shift_down(lx, prev8, 3)
    xc = cb_ref[...] + x3 * cw_ref[0:1, :]
    xc = xc + x2 * cw_ref[1:2, :]
    xc = xc + x1 * cw_ref[2:3, :]
    xc = xc + lx * cw_ref[3:4, :]
    n_heads = wr_ref.shape[0]
    pre_r, pre_i = [], []
    for hd in range(n_heads):
        xh = xc[:, hd * LANES:(hd + 1) * LANES].astype(BF16)
        pre_r.append(_dot(xh, wr_ref[hd].astype(BF16)))
        pre_i.append(_dot(xh, wi_ref[hd].astype(BF16)))
    r = _sigmoid(jnp.concatenate(pre_r, axis=1) + br_ref[...])
    ig = _sigmoid(jnp.concatenate(pre_i, axis=1) + bi_ref[...])
    z = -lam_ref[...]
    e = jnp.exp(-jnp.abs(z))
    e1 = 1.0 + e
    log1p_e = jnp.where(e1 == 1.0, e, jnp.log(e1) * (e / (e1 - 1.0)))
    sp = jnp.maximum(z, 0.0) + log1p_e
    log_a = (-LRU_C * r) * sp
    a = jnp.exp(log_a)
    zz = -2.0 * log_a
    series = zz * (1.0 - zz * (0.5 - zz * (1.0 / 6.0)))
    beta2 = jnp.maximum(jnp.where(zz < 0.015625, series, 1.0 - a * a), 1e-37)
    beta = beta2 * lax.rsqrt(beta2)
    return dict(x1=x1, x2=x2, x3=x3, xc=xc, r=r, ig=ig, sp=sp, a=a, beta=beta)


def _lru_fwd(lx, lg, cw, cb, wr, br, wi, bi, lam, tm):
    tp, w = lx.shape
    nt = tp // tm
    per8 = tm // SUBLANES

    def body(lx_ref, lxp_ref, lg_ref, cw_ref, cb_ref, wr_ref, br_ref, wi_ref, bi_ref, lam_ref,
             hl_ref, y_ref, a_s, u_s, carry):
        i = pl.program_id(0)

        @pl.when(i == 0)
        def _():
            carry[...] = jnp.zeros_like(carry)

        lxv = lx_ref[...]
        prev8 = jnp.where(i == 0, 0.0, lxp_ref[...])
        f = _lru_front(lxv, prev8, cw_ref, cb_ref, wr_ref, br_ref, wi_ref, bi_ref, lam_ref)
        row = lax.broadcasted_iota(jnp.int32, (tm, 1), 0) + i * tm
        a_s[...] = f["a"]
        u_s[...] = jnp.where(row >= PAD_ROWS, f["beta"] * f["ig"] * f["xc"], 0.0)

        def step(t, h):
            h = a_s[pl.ds(t, 1), :] * h + u_s[pl.ds(t, 1), :]
            hl_ref[pl.ds(t, 1), :] = h
            return h

        carry[0:1, :] = lax.fori_loop(0, tm, step, carry[0:1, :], unroll=8)
        g = lg_ref[...]
        y_ref[...] = (hl_ref[...] * (g * _sigmoid(g))).astype(BF16)

    tile = pl.BlockSpec((tm, w), lambda i: (i, 0))
    prev = pl.BlockSpec((SUBLANES, w), lambda i: (jnp.maximum(i * per8 - 1, 0), 0))
    vec = pl.BlockSpec((1, w), lambda i: (0, 0))
    mat = pl.BlockSpec(wr.shape, lambda i: (0, 0, 0))
    return pl.pallas_call(
        body,
        name="lru_fwd",
        grid=(nt,),
        in_specs=[tile, prev, tile, pl.BlockSpec(cw.shape, lambda i: (0, 0)), vec, mat, vec, mat, vec, vec],
        out_specs=[tile, tile],
        out_shape=[jax.ShapeDtypeStruct((tp, w), F32), jax.ShapeDtypeStruct((tp, w), BF16)],
        scratch_shapes=[pltpu.VMEM((tm, w), F32), pltpu.VMEM((tm, w), F32), pltpu.VMEM((SUBLANES, w), F32)],
        compiler_params=pltpu.CompilerParams(dimension_semantics=("arbitrary",), vmem_limit_bytes=VMEM_LIMIT),
    )(lx, lx, lg, cw, cb, wr, br, wi, bi, lam)


def _ret_tables():
    log_g = jnp.log1p(-jnp.exp2(-5.0 - jnp.arange(HEADS, dtype=F32)))
    idx = jnp.arange(CHUNK, dtype=F32)
    diff = idx[:, None] - idx[None, :]
    dmask = jnp.where(diff[None] >= 0.0, jnp.exp(jnp.maximum(diff, 0.0)[None] * log_g[:, None, None]), 0.0)
    kdec = jnp.repeat(jnp.exp((CHUNK - 1.0 - idx)[:, None] * log_g[None, :]), QK_DIM, axis=1)
    qdec = jnp.repeat(jnp.exp((idx + 1.0)[:, None] * log_g[None, :]), QK_DIM, axis=1)
    g_chunk = jnp.exp(CHUNK * log_g)
    g_rows = jnp.repeat(g_chunk, QK_DIM).reshape(HEADS // 2, 2 * QK_DIM, 1)
    g_state = jnp.broadcast_to(g_rows, (HEADS // 2, 2 * QK_DIM, 2 * LANES))
    r_head = jnp.arange(2 * QK_DIM)[:, None] // QK_DIM
    c_head = jnp.arange(2 * LANES)[None, :] // LANES
    block_diag = (r_head == c_head).astype(F32)
    return dmask, qdec, kdec, g_state, block_diag


def _head_norm(o_h):
    mu = jnp.mean(o_h, axis=-1, keepdims=True)
    oc = o_h - mu
    var = jnp.mean(oc * oc, axis=-1, keepdims=True)
    rstd = lax.rsqrt(var + EPS)
    return oc * rstd, rstd


def _ret_fwd(q, k, v, rg, gain, tables):
    tp, d_qk = q.shape
    d_ret = v.shape[1]
    n_ch = tp // CHUNK
    n_pairs = HEADS // 2
    dmask, qdec, kdec, g_state, block_diag = tables

    def body(q_ref, k_ref, v_ref, rg_ref, gain_ref, dm_ref, qd_ref, kd_ref, gs_ref, bd_ref,
             o_ref, y_ref, rp_ref, state):
        n = pl.program_id(0)

        @pl.when(n == 0)
        def _():
            state[...] = jnp.zeros_like(state)

        lane = lax.broadcasted_iota(jnp.int32, (CHUNK, LANES), 1)
        for p in range(n_pairs):
            qs = slice(p * LANES, (p + 1) * LANES)
            vs = slice(p * 2 * LANES, (p + 1) * 2 * LANES)
            qp, kp = q_ref[:, qs], k_ref[:, qs]
            vb = v_ref[:, vs].astype(BF16)
            kb = kp.astype(BF16)
            qd = (qp * qd_ref[:, qs]).astype(BF16)
            kd = (kp * kd_ref[:, qs]).astype(BF16)
            st = state[p]
            st_b = st.astype(BF16)
            rp_ref[0, p] = st_b
            cross = _dot(qd, st_b)
            for e in range(2):
                hd = 2 * p + e
                hs = slice(hd * LANES, (hd + 1) * LANES)
                qm = jnp.where((lane // QK_DIM) == e, qp, 0.0).astype(BF16)
                s = _dot_nt(qm, kb) * dm_ref[hd]
                o_h = _dot(s.astype(BF16), vb[:, e * LANES:(e + 1) * LANES]) + cross[:, e * LANES:(e + 1) * LANES]
                o_ref[:, hs] = o_h
                xhat, _ = _head_norm(o_h)
                g = rg_ref[:, hs]
                y_ref[:, hs] = ((xhat * gain_ref[:, hs]) * (g * _sigmoid(g))).astype(BF16)
            state[p] = gs_ref[p] * st + bd_ref[...] * _dot_tn(kd, vb)

    ch = lambda w: pl.BlockSpec((CHUNK, w), lambda n: (n, 0))
    const2 = lambda a: pl.BlockSpec(a.shape, lambda n: (0, 0))
    const3 = lambda a: pl.BlockSpec(a.shape, lambda n: (0, 0, 0))
    return pl.pallas_call(
        body,
        name="ret_fwd",
        grid=(n_ch,),
        in_specs=[ch(d_qk), ch(d_qk), ch(d_ret), ch(d_ret), const2(gain), const3(dmask), const2(qdec), const2(kdec),
                  const3(g_state), const2(block_diag)],
        out_specs=[ch(d_ret), ch(d_ret),
                   pl.BlockSpec((1, n_pairs, 2 * QK_DIM, 2 * LANES), lambda n: (n, 0, 0, 0))],
        out_shape=[jax.ShapeDtypeStruct((tp, d_ret), F32), jax.ShapeDtypeStruct((tp, d_ret), BF16),
                   jax.ShapeDtypeStruct((n_ch, n_pairs, 2 * QK_DIM, 2 * LANES), BF16)],
        scratch_shapes=[pltpu.VMEM((n_pairs, 2 * QK_DIM, 2 * LANES), F32)],
        compiler_params=pltpu.CompilerParams(dimension_semantics=("arbitrary",), vmem_limit_bytes=VMEM_LIMIT),
    )(q, k, v, rg, gain, dmask, qdec, kdec, g_state, block_diag)


def _out_proj_loss(y_lru, y_ret, hp, tgt, wo, gain_f, tm):
    tp, d = hp.shape
    w_lru = y_lru.shape[1]
    w_mix = wo.shape[0]
    nt, nb = tp // tm, tm // CHUNK

    def body(*refs):
        yl_ref, yr_ref, hp_ref = refs[:3]
        tb = refs[3:3 + nb]
        wo_ref, gf_ref = refs[3 + nb:5 + nb]
        dh2_ref, dyl_ref, dyr_ref, dwo_ref, dgf_ref, loss_ref = refs[5 + nb:]
        i = pl.program_id(0)

        @pl.when(i == 0)
        def _():
            dwo_ref[...] = jnp.zeros_like(dwo_ref)
            dgf_ref[...] = jnp.zeros_like(dgf_ref)
            loss_ref[...] = jnp.zeros_like(loss_ref)

        yl, yr = yl_ref[...], yr_ref[...]
        h2 = hp_ref[...] + _dot(yl, wo_ref[0:w_lru, :]) + _dot(yr, wo_ref[w_lru:w_mix, :])
        rinv = lax.rsqrt(jnp.mean(h2 * h2, axis=-1, keepdims=True) + EPS)
        nrm = h2 * rinv
        gf = gf_ref[...]
        tgt_v = jnp.concatenate([r[...] for r in tb], axis=0)
        row = lax.broadcasted_iota(jnp.int32, (tm, 1), 0) + i * tm
        err = jnp.where(row >= CHUNK, nrm * gf - tgt_v, 0.0)
        loss_ref[...] += 0.5 * jnp.sum(jnp.mean(err * err, axis=-1, keepdims=True))
        dout = err * (1.0 / d)
        dgf_ref[...] += jnp.sum(dout * nrm, axis=0, keepdims=True)
        dn = dout * gf
        dh2 = rinv * (dn - nrm * jnp.mean(dn * nrm, axis=-1, keepdims=True))
        dh2_ref[...] = dh2
        dh2b = dh2.astype(BF16)
        dyl_ref[...] = _dot_nt(dh2b, wo_ref[0:w_lru, :])
        dyr_ref[...] = _dot_nt(dh2b, wo_ref[w_lru:w_mix, :])
        dwo_ref[0:w_lru, :] += _dot_tn(yl, dh2b)
        dwo_ref[w_lru:w_mix, :] += _dot_tn(yr, dh2b)

    tile = lambda w: pl.BlockSpec((tm, w), lambda i: (i, 0))
    t_specs = [pl.BlockSpec((CHUNK, d), functools.partial(lambda i, b: (jnp.maximum(i * nb + b - 1, 0), 0), b=b))
               for b in range(nb)]
    return pl.pallas_call(
        body,
        name="out_proj_loss",
        grid=(nt,),
        in_specs=[tile(w_lru), tile(w_mix - w_lru), tile(d)] + t_specs +
                 [pl.BlockSpec(wo.shape, lambda i: (0, 0)), pl.BlockSpec(gain_f.shape, lambda i: (0, 0))],
        out_specs=[tile(d), tile(w_lru), tile(w_mix - w_lru), pl.BlockSpec(wo.shape, lambda i: (0, 0)),
                   pl.BlockSpec((1, d), lambda i: (0, 0)), pl.BlockSpec((SUBLANES, LANES), lambda i: (0, 0))],
        out_shape=[jax.ShapeDtypeStruct((tp, d), F32), jax.ShapeDtypeStruct((tp, w_lru), F32),
                   jax.ShapeDtypeStruct((tp, w_mix - w_lru), F32), jax.ShapeDtypeStruct(wo.shape, F32),
                   jax.ShapeDtypeStruct((1, d), F32), jax.ShapeDtypeStruct((SUBLANES, LANES), F32)],
        compiler_params=pltpu.CompilerParams(dimension_semantics=("arbitrary",), vmem_limit_bytes=VMEM_LIMIT),
    )(y_lru, y_ret, hp, *([tgt] * nb), wo, gain_f)


def _ret_bwd(q, k, v, rg, o, rprev, dy, gain, cos_t, sin_t, tables):
    tp, d_qk = q.shape
    d_ret = v.shape[1]
    n_ch = tp // CHUNK
    n_pairs = HEADS // 2
    dmask, qdec, kdec, g_state, block_diag = tables

    def body(q_ref, k_ref, v_ref, rg_ref, o_ref, rp_ref, dy_ref, gain_ref, cos_ref, sin_ref,
             dm_ref, qd_ref, kd_ref, gs_ref, bd_ref, dq_ref, dk_ref, dv_ref, drg_ref, dgain_ref, dstate):
        n = pl.program_id(0)

        @pl.when(n == 0)
        def _():
            dstate[...] = jnp.zeros_like(dstate)
            dgain_ref[...] = jnp.zeros_like(dgain_ref)

        lane = lax.broadcasted_iota(jnp.int32, (CHUNK, LANES), 1)
        dq_parts, dk_parts = [], []
        for p in range(n_pairs):
            qs = slice(p * LANES, (p + 1) * LANES)
            vs = slice(p * 2 * LANES, (p + 1) * 2 * LANES)
            do_parts = []
            for e in range(2):
                hd = 2 * p + e
                hs = slice(hd * LANES, (hd + 1) * LANES)
                xhat, rstd = _head_norm(o_ref[:, hs])
                g = rg_ref[:, hs]
                sg = _sigmoid(g)
                dyh = dy_ref[:, hs]
                gn = gain_ref[:, hs]
                d_on = dyh * (g * sg)
                drg_ref[:, hs] = (dyh * (xhat * gn) * (sg * (1.0 + g * (1.0 - sg)))).astype(BF16)
                dgain_ref[:, hs] += jnp.sum(d_on * xhat, axis=0, keepdims=True)
                dxh = d_on * gn
                do_parts.append(rstd * (dxh - jnp.mean(dxh, axis=-1, keepdims=True)
                                        - xhat * jnp.mean(dxh * xhat, axis=-1, keepdims=True)))
            do_b = jnp.concatenate(do_parts, axis=1).astype(BF16)
            qp, kp = q_ref[:, qs], k_ref[:, qs]
            vb = v_ref[:, vs].astype(BF16)
            kb = kp.astype(BF16)
            qd = (qp * qd_ref[:, qs]).astype(BF16)
            kd = (kp * kd_ref[:, qs]).astype(BF16)
            dst = dstate[p]
            dst_b = dst.astype(BF16)
            dqp = _dot_nt(do_b, rp_ref[0, p]) * qd_ref[:, qs]
            dkp = _dot_nt(vb, dst_b) * kd_ref[:, qs]
            dvp = _dot(kd, dst_b)
            dv_parts = []
            for e in range(2):
                hd = 2 * p + e
                es = slice(e * LANES, (e + 1) * LANES)
                mine = (lane // QK_DIM) == e
                qm = jnp.where(mine, qp, 0.0).astype(BF16)
                km = jnp.where(mine, kp, 0.0).astype(BF16)
                dm = dm_ref[hd]
                s = (_dot_nt(qm, kb) * dm).astype(BF16)
                ds = (_dot_nt(do_b[:, es], vb[:, es]) * dm).astype(BF16)
                dv_parts.append(dvp[:, es] + _dot_tn(s, do_b[:, es]))
                dqp = dqp + _dot(ds, km)
                dkp = dkp + _dot_tn(ds, qm)
            dv_ref[:, vs] = jnp.concatenate(dv_parts, axis=1).astype(BF16)
            dstate[p] = gs_ref[p] * dst + bd_ref[...] * _dot_tn(qd, do_b)
            dq_parts.append(dqp)
            dk_parts.append(dkp)
        cos = _tile_lanes(cos_ref[...], d_qk // LANES)
        sin = _tile_lanes(sin_ref[...], d_qk // LANES)
        dq = jnp.concatenate(dq_parts, axis=1)
        dk = jnp.concatenate(dk_parts, axis=1) * (QK_DIM ** -0.5)
        dq_ref[...] = (dq * cos + _rot_partner(dq * sin)).astype(BF16)
        dk_ref[...] = (dk * cos + _rot_partner(dk * sin)).astype(BF16)

    last = n_ch - 1
    ch = lambda w: pl.BlockSpec((CHUNK, w), lambda n: (last - n, 0))
    const2 = lambda a: pl.BlockSpec(a.shape, lambda n: (0, 0))
    const3 = lambda a: pl.BlockSpec(a.shape, lambda n: (0, 0, 0))
    return pl.pallas_call(
        body,
        name="ret_bwd",
        grid=(n_ch,),
        in_specs=[ch(d_qk), ch(d_qk), ch(d_ret), ch(d_ret), ch(d_ret),
                  pl.BlockSpec((1, n_pairs, 2 * QK_DIM, 2 * LANES), lambda n: (last - n, 0, 0, 0)),
                  ch(d_ret), const2(gain), ch(LANES), ch(LANES),
                  const3(dmask), const2(qdec), const2(kdec), const3(g_state), const2(block_diag)],
        out_specs=[ch(d_qk), ch(d_qk), ch(d_ret), ch(d_ret), pl.BlockSpec((1, d_ret), lambda n: (0, 0))],
        out_shape=[jax.ShapeDtypeStruct((tp, d_qk), BF16), jax.ShapeDtypeStruct((tp, d_qk), BF16),
                   jax.ShapeDtypeStruct((tp, d_ret), BF16), jax.ShapeDtypeStruct((tp, d_ret), BF16),
                   jax.ShapeDtypeStruct((1, d_ret), F32)],
        scratch_shapes=[pltpu.VMEM((n_pairs, 2 * QK_DIM, 2 * LANES), F32)],
        compiler_params=pltpu.CompilerParams(dimension_semantics=("arbitrary",), vmem_limit_bytes=VMEM_LIMIT),
    )(q, k, v, rg, o, rprev, dy, gain, cos_t, sin_t, dmask, qdec, kdec, g_state, block_diag)


def _lru_bwd(lx, lg, hl, dy, cw, cb, wr, br, wi, bi, lam, tm):
    tp, w = lx.shape
    nt = tp // tm
    per8 = tm // SUBLANES
    n_heads = wr.shape[0]

    def body(lx_ref, lxp_ref, lg_ref, hl_ref, hlp_ref, dy_ref, cw_ref, cb_ref, wr_ref, br_ref, wi_ref, bi_ref, lam_ref,
             dlx_ref, dlg_ref, dcw_ref, dcb_ref, dwr_ref, dbr_ref, dwi_ref, dbi_ref, dlam_ref,
             a_s, g_s, dh_s, carry, dxc_next):
        i = pl.program_id(0)
        first_tile = i == nt - 1

        @pl.when(i == 0)
        def _():
            carry[...] = jnp.zeros_like(carry)
            dxc_next[...] = jnp.zeros_like(dxc_next)
            for r in (dcw_ref, dcb_ref, dwr_ref, dbr_ref, dwi_ref, dbi_ref, dlam_ref):
                r[...] = jnp.zeros_like(r)

        lxv = lx_ref[...]
        prev8 = jnp.where(first_tile, 0.0, lxp_ref[...])
        f = _lru_front(lxv, prev8, cw_ref, cb_ref, wr_ref, br_ref, wi_ref, bi_ref, lam_ref)
        a, beta, r, ig, xc = f["a"], f["beta"], f["r"], f["ig"], f["xc"]
        g = lg_ref[...]
        sg = _sigmoid(g)
        dyv = dy_ref[...]
        hlv = hl_ref[...]
        dlg_ref[...] = (dyv * hlv * (sg * (1.0 + g * (1.0 - sg)))).astype(BF16)
        a_s[...] = a
        g_s[...] = dyv * (g * sg)

        def step(s, cn):
            t = tm - 1 - s
            dh = g_s[pl.ds(t, 1), :] + cn
            dh_s[pl.ds(t, 1), :] = dh
            return a_s[pl.ds(t, 1), :] * dh

        carry[0:1, :] = lax.fori_loop(0, tm, step, carry[0:1, :], unroll=8)
        dh = dh_s[...]
        hprev = _shift_down(hlv, jnp.where(first_tile, 0.0, hlp_ref[...]), 1)
        row = lax.broadcasted_iota(jnp.int32, (tm, 1), 0) + (nt - 1 - i) * tm
        du = jnp.where(row >= PAD_ROWS, dh, 0.0)
        da = dh * hprev
        dbeta = du * ig * xc
        d_ig = du * beta * xc
        dxc = du * beta * ig
        dloga = da * a - dbeta * (a * a / beta)
        lam_v = lam_ref[...]
        dlam_ref[...] += jnp.sum(dloga * r, axis=0, keepdims=True) * (LRU_C * _sigmoid(-lam_v))
        dpr = (dloga * (-LRU_C * f["sp"])) * r * (1.0 - r)
        dpi = d_ig * ig * (1.0 - ig)
        dbr_ref[...] += jnp.sum(dpr, axis=0, keepdims=True)
        dbi_ref[...] += jnp.sum(dpi, axis=0, keepdims=True)
        dxc_parts = []
        for hd in range(n_heads):
            hs = slice(hd * LANES, (hd + 1) * LANES)
            xh = xc[:, hs].astype(BF16)
            dprh = dpr[:, hs].astype(BF16)
            dpih = dpi[:, hs].astype(BF16)
            dwr_ref[hd] += _dot_tn(xh, dprh)
            dwi_ref[hd] += _dot_tn(xh, dpih)
            dxc_parts.append(_dot_nt(dprh, wr_ref[hd].astype(BF16)) + _dot_nt(dpih, wi_ref[hd].astype(BF16)))
        dxc = dxc + jnp.concatenate(dxc_parts, axis=1)
        nxt = dxc_next[...]
        dlx = dxc * cw_ref[3:4, :]
        dlx = dlx + _shift_up(dxc, nxt, 1) * cw_ref[2:3, :]
        dlx = dlx + _shift_up(dxc, nxt, 2) * cw_ref[1:2, :]
        dlx = dlx + _shift_up(dxc, nxt, 3) * cw_ref[0:1, :]
        dlx_ref[...] = dlx.astype(BF16)
        dxc_next[...] = dxc[0:SUBLANES]
        dcb_ref[...] += jnp.sum(dxc, axis=0, keepdims=True)
        dcw_ref[0:1, :] += jnp.sum(dxc * f["x3"], axis=0, keepdims=True)
        dcw_ref[1:2, :] += jnp.sum(dxc * f["x2"], axis=0, keepdims=True)
        dcw_ref[2:3, :] += jnp.sum(dxc * f["x1"], axis=0, keepdims=True)
        dcw_ref[3:4, :] += jnp.sum(dxc * lxv, axis=0, keepdims=True)

    last = nt - 1
    tile = pl.BlockSpec((tm, w), lambda i: (last - i, 0))
    prev = pl.BlockSpec((SUBLANES, w), lambda i: (jnp.maximum((last - i) * per8 - 1, 0), 0))
    vec = pl.BlockSpec((1, w), lambda i: (0, 0))
    mat = pl.BlockSpec(wr.shape, lambda i: (0, 0, 0))
    cwb = pl.BlockSpec(cw.shape, lambda i: (0, 0))
    return pl.pallas_call(
        body,
        name="lru_bwd",
        grid=(nt,),
        in_specs=[tile, prev, tile, tile, prev, tile, cwb, vec, mat, vec, mat, vec, vec],
        out_specs=[tile, tile, cwb, vec, mat, vec, mat, vec, vec],
        out_shape=[jax.ShapeDtypeStruct((tp, w), BF16), jax.ShapeDtypeStruct((tp, w), BF16),
                   jax.ShapeDtypeStruct(cw.shape, F32), jax.ShapeDtypeStruct((1, w), F32),
                   jax.ShapeDtypeStruct(wr.shape, F32), jax.ShapeDtypeStruct((1, w), F32),
                   jax.ShapeDtypeStruct(wr.shape, F32), jax.ShapeDtypeStruct((1, w), F32),
                   jax.ShapeDtypeStruct((1, w), F32)],
        scratch_shapes=[pltpu.VMEM((tm, w), F32), pltpu.VMEM((tm, w), F32), pltpu.VMEM((tm, w), F32),
                        pltpu.VMEM((SUBLANES, w), F32), pltpu.VMEM((SUBLANES, w), F32)],
        compiler_params=pltpu.CompilerParams(dimension_semantics=("arbitrary",), vmem_limit_bytes=VMEM_LIMIT),
    )(lx, lx, lg, hl, hl, dy, cw, cb, wr, br, wi, bi, lam)


def _in_proj_bwd(dparts, hp, dh2, gain, wg, s_len, tm):
    tp, d = hp.shape
    nt = tp // tm
    widths = [p.shape[1] for p in dparts]
    segs = _proj_segments(widths[0], widths[2], widths[4], wg.shape[2])

    def body(*refs):
        dp = refs[:6]
        hp_ref, dh2_ref, g_ref, w_ref = refs[6:10]
        gx_ref, dmeta_ref, dg_ref, dwg_ref = refs[10:14]
        acc, stage, sem = refs[14:]
        i = pl.program_id(0)

        @pl.when(i == 0)
        def _():
            acc[...] = jnp.zeros_like(acc)
            dg_ref[...] = jnp.zeros_like(dg_ref)

        h = hp_ref[...]
        rinv = lax.rsqrt(jnp.mean(h * h, axis=-1, keepdims=True) + EPS)
        nrm = h * rinv
        gv = g_ref[...]
        u = (nrm * gv).astype(BF16)
        du = jnp.zeros((tm, d), F32)
        for p_ref, parts in zip(dp, segs):
            for jj, inner, off, take in parts:
                seg = p_ref[:, off:off + take]
                du = du + _dot_nt(seg, w_ref[jj, :, inner:inner + take])
                acc[jj, :, inner:inner + take] += _dot_tn(u, seg)
        dg_ref[...] += jnp.sum(du * nrm, axis=0, keepdims=True)
        dn = du * gv
        dh = dh2_ref[...] + rinv * (dn - nrm * jnp.mean(dn * nrm, axis=-1, keepdims=True))
        stage[...] = dh

        @pl.when(i == 0)
        def _():
            dmeta_ref[...] = dh[PAD_ROWS:CHUNK]
            cp = pltpu.make_async_copy(stage.at[pl.ds(CHUNK, tm - CHUNK), :], gx_ref.at[pl.ds(0, tm - CHUNK), :], sem)
            cp.start()
            cp.wait()

        @pl.when(i > 0)
        def _():
            start = pl.multiple_of(i * tm - CHUNK, CHUNK)
            cp = pltpu.make_async_copy(stage, gx_ref.at[pl.ds(start, tm), :], sem)
            cp.start()
            cp.wait()

        @pl.when(i == nt - 1)
        def _():
            cp = pltpu.make_async_copy(acc, dwg_ref, sem)
            cp.start()
            cp.wait()

    tile = lambda w: pl.BlockSpec((tm, w), lambda i: (i, 0))
    return pl.pallas_call(
        body,
        name="in_proj_bwd",
        grid=(nt,),
        in_specs=[tile(w) for w in widths] + [tile(d), tile(d), pl.BlockSpec(gain.shape, lambda i: (0, 0)),
                                              pl.BlockSpec(wg.shape, lambda i: (0, 0, 0))],
        out_specs=[ANY_SPEC, pl.BlockSpec((N_META, d), lambda i: (0, 0)), pl.BlockSpec((1, d), lambda i: (0, 0)),
                   ANY_SPEC],
        out_shape=[jax.ShapeDtypeStruct((s_len, d), F32), jax.ShapeDtypeStruct((N_META, d), F32),
                   jax.ShapeDtypeStruct((1, d), F32), jax.ShapeDtypeStruct(wg.shape, F32)],
        scratch_shapes=[pltpu.VMEM(wg.shape, F32), pltpu.VMEM((tm, d), F32), pltpu.SemaphoreType.DMA],
        compiler_params=pltpu.CompilerParams(dimension_semantics=("arbitrary",), vmem_limit_bytes=VMEM_LIMIT),
    )(*dparts, hp, dh2, gain, wg)


def _pair_exchange(bufs):
    n = len(bufs)

    def body(*refs):
        src, dst = refs[:n], refs[n:2 * n]
        send_sems, recv_sems = refs[2 * n:]
        x, y, c = lax.axis_index("x"), lax.axis_index("y"), lax.axis_index("c")
        copies = []
        for b in range(n):
            for jj in range(N_CHIPS):
                copies.append(pltpu.make_async_remote_copy(
                    src_ref=src[b].at[2 * jj + 1 - c], dst_ref=dst[b].at[jj],
                    send_sem=send_sems.at[b * N_CHIPS + jj], recv_sem=recv_sems.at[b * N_CHIPS + jj],
                    device_id=(x, y, 1 - c), device_id_type=MESH))
        for cp in copies:
            cp.start()
        for cp in copies:
            cp.wait()

    return pl.pallas_call(
        body,
        name="grad_pair_exchange",
        in_specs=[ANY_SPEC] * n,
        out_specs=[ANY_SPEC] * n,
        out_shape=[jax.ShapeDtypeStruct((N_CHIPS,) + b.shape[1:], b.dtype) for b in bufs],
        scratch_shapes=[pltpu.SemaphoreType.DMA((n * N_CHIPS,)), pltpu.SemaphoreType.DMA((n * N_CHIPS,))],
    )(*bufs)


def _pair_sum(buf, recv, c_arr, tr, name):
    _, rows, cols = buf.shape

    def body(c_ref, mine_ref, got_ref, out_ref):
        out_ref[...] = (mine_ref[...] + got_ref[...]).astype(BF16)

    grid_spec = pltpu.PrefetchScalarGridSpec(
        num_scalar_prefetch=1,
        grid=(N_CHIPS, rows // tr),
        in_specs=[pl.BlockSpec((1, tr, cols), lambda jj, r, c_ref: (2 * jj + c_ref[0], r, 0)),
                  pl.BlockSpec((1, tr, cols), lambda jj, r, c_ref: (jj, r, 0))],
        out_specs=pl.BlockSpec((1, tr, cols), lambda jj, r, c_ref: (jj, r, 0)),
    )
    return pl.pallas_call(
        body,
        name=name,
        grid_spec=grid_spec,
        out_shape=jax.ShapeDtypeStruct((N_CHIPS, rows, cols), BF16),
    )(c_arr, buf, recv)


def _chip_exchange(bufs):
    n = len(bufs)

    def body(*refs):
        src, dst = refs[:n], refs[n:2 * n]
        send_sems, recv_sems = refs[2 * n:]
        x, y, c = lax.axis_index("x"), lax.axis_index("y"), lax.axis_index("c")
        j = 2 * x + y
        chips = [(1 - x, y), (x, 1 - y), (1 - x, 1 - y)]
        copies = []
        for b in range(n):
            for k, (cx, cy) in enumerate(chips):
                copies.append(pltpu.make_async_remote_copy(
                    src_ref=src[b].at[2 * cx + cy], dst_ref=dst[b].at[j],
                    send_sem=send_sems.at[b * 3 + k], recv_sem=recv_sems.at[b * 3 + k],
                    device_id=(cx, cy, c), device_id_type=MESH))
        for cp in copies:
            cp.start()
        for cp in copies:
            cp.wait()

    return pl.pallas_call(
        body,
        name="grad_chip_exchange",
        in_specs=[ANY_SPEC] * n,
        out_specs=[ANY_SPEC] * n,
        out_shape=[jax.ShapeDtypeStruct(b.shape, b.dtype) for b in bufs],
        scratch_shapes=[pltpu.SemaphoreType.DMA((n * 3,)), pltpu.SemaphoreType.DMA((n * 3,))],
    )(*bufs)


def _chip_sum(mine, got, j_arr, tr, name):
    _, rows, cols = got.shape

    def body(j_ref, mine_ref, got_ref, out_ref):
        j = j_ref[0]
        acc = None
        for jj in range(N_CHIPS):
            term = jnp.where(j == jj, mine_ref[0], got_ref[jj]).astype(F32)
            acc = term if acc is None else acc + term
        out_ref[...] = acc

    grid_spec = pltpu.PrefetchScalarGridSpec(
        num_scalar_prefetch=1,
        grid=(rows // tr,),
        in_specs=[pl.BlockSpec((1, tr, cols), lambda r, j_ref: (j_ref[0], r, 0)),
                  pl.BlockSpec((N_CHIPS, tr, cols), lambda r, j_ref: (0, r, 0))],
        out_specs=pl.BlockSpec((tr, cols), lambda r, j_ref: (r, 0)),
    )
    return pl.pallas_call(
        body,
        name=name,
        grid_spec=grid_spec,
        out_shape=jax.ShapeDtypeStruct((rows, cols), F32),
    )(j_arr, mine, got)


def _finish_exchange(f_in, f_out, f_small):
    def body(fin_ref, fout_ref, fs_ref, rin_ref, rout_ref, os_ref, send_sems, recv_sems, local_sem):
        x, y, c = lax.axis_index("x"), lax.axis_index("y"), lax.axis_index("c")
        j = 2 * x + y
        me = 2 * j + c
        sibling = (x, y, 1 - c)
        chips = [(1 - x, y), (x, 1 - y), (1 - x, 1 - y)]
        local = pltpu.make_async_copy(fs_ref, os_ref.at[me], local_sem)
        local.start()

        def copy(k, src, dst, to):
            return pltpu.make_async_remote_copy(src_ref=src, dst_ref=dst, send_sem=send_sems.at[k],
                                                recv_sem=recv_sems.at[k], device_id=to, device_id_type=MESH)

        first = [copy(0, fin_ref, rin_ref, sibling), copy(1, fout_ref, rout_ref, sibling),
                 copy(2, fs_ref, os_ref.at[me], sibling)]
        first += [copy(3 + k, fs_ref, os_ref.at[me], (cx, cy, c)) for k, (cx, cy) in enumerate(chips)]
        for cp in first:
            cp.start()
        passed = []
        for k, (cx, cy) in enumerate(chips):
            unit = 2 * (2 * cx + cy) + c
            copy(3 + k, fs_ref, os_ref.at[unit], sibling).wait_recv()
            fwd = copy(6 + k, os_ref.at[unit], os_ref.at[unit], sibling)
            fwd.start()
            passed.append(fwd)
        copy(0, fin_ref, rin_ref, sibling).wait_recv()
        copy(1, fout_ref, rout_ref, sibling).wait_recv()
        copy(2, fs_ref, os_ref.at[2 * j + 1 - c], sibling).wait_recv()
        for k, (cx, cy) in enumerate(chips):
            unit = 2 * (2 * cx + cy) + 1 - c
            copy(6 + k, fs_ref, os_ref.at[unit], sibling).wait_recv()
        for cp in first + passed:
            cp.wait_send()
        local.wait()

    return pl.pallas_call(
        body,
        name="grad_finish_exchange",
        in_specs=[ANY_SPEC] * 3,
        out_specs=[ANY_SPEC] * 3,
        out_shape=[jax.ShapeDtypeStruct(f_in.shape, F32), jax.ShapeDtypeStruct(f_out.shape, F32),
                   jax.ShapeDtypeStruct((N_DEV,) + f_small.shape, F32)],
        scratch_shapes=[pltpu.SemaphoreType.DMA((9,)), pltpu.SemaphoreType.DMA((9,)), pltpu.SemaphoreType.DMA],
    )(f_in, f_out, f_small)


def _adamw_math(w, g, m, v):
    m = ADAM_B1 * m + (1.0 - ADAM_B1) * g
    v = ADAM_B2 * v + (1.0 - ADAM_B2) * (g * g)
    m_hat = m / (1.0 - ADAM_B1 ** ADAM_STEP)
    v_hat = v / (1.0 - ADAM_B2 ** ADAM_STEP)
    delta = -ADAM_LR * (m_hat / (jnp.sqrt(v_hat) + ADAM_EPS) + ADAM_WD * w)
    return delta, m, v


def _adamw_big(w, g_mine, g_sib, m, v, c_arr, tr, name):
    rows, cols = w.shape
    half = rows // 2
    per = half // tr

    def body(c_ref, w_ref, gm_ref, gs_ref, m_ref, v_ref, g_ref, d_ref, mo_ref, vo_ref):
        g = jnp.where(pl.program_id(0) == c_ref[0], gm_ref[...], gs_ref[...])
        g_ref[...] = g
        d_ref[...], mo_ref[...], vo_ref[...] = _adamw_math(w_ref[...], g, m_ref[...], v_ref[...])

    full = pl.BlockSpec((tr, cols), lambda h, r, c_ref: (h * per + r, 0))
    unit = pl.BlockSpec((tr, cols), lambda h, r, c_ref: (r, 0))
    grid_spec = pltpu.PrefetchScalarGridSpec(
        num_scalar_prefetch=1,
        grid=(2, per),
        in_specs=[full, unit, unit, full, full],
        out_specs=[full] * 4,
    )
    return pl.pallas_call(
        body,
        name=name,
        grid_spec=grid_spec,
        out_shape=[jax.ShapeDtypeStruct(w.shape, F32)] * 4,
    )(c_arr, w, g_mine, g_sib, m, v)


def _adamw_small(j_arr, packed, params):
    names = list(params)
    n = len(names)
    vec_names = ["norm_gain", "conv_b", "b_rg", "b_ig", "lru_lambda", "ret_norm_gain", "final_norm_gain"]

    def body(j_ref, pk_ref, *refs):
        ins = refs[:3 * n]
        outs = refs[3 * n:]
        j = j_ref[0]

        def shard(row, rows):
            return jnp.concatenate([pk_ref[2 * j, row:row + rows, :], pk_ref[2 * j + 1, row:row + rows, :]], axis=1)

        for idx, name in enumerate(names):
            if name == "w_rg":
                g = pk_ref[:, ROW_WR:ROW_WR + LANES, :]
            elif name == "w_ig":
                g = pk_ref[:, ROW_WI:ROW_WI + LANES, :]
            elif name == "meta_tokens":
                g = shard(ROW_META, N_META)
            elif name == "conv_w":
                g = shard(ROW_CONV, 4)
            else:
                row = ROW_VEC + vec_names.index(name)
                g = jnp.concatenate([pk_ref[u, row:row + 1, :] for u in range(N_DEV)], axis=1)
            w_ref, m_ref, v_ref = ins[3 * idx:3 * idx + 3]
            delta, m, v = _adamw_math(w_ref[...], g, m_ref[...], v_ref[...])
            g_ref, d_ref, mo_ref, vo_ref = outs[4 * idx:4 * idx + 4]
            g_ref[...], d_ref[...], mo_ref[...], vo_ref[...] = g, delta, m, v

    flat_in, out_shape = [], []
    for name in names:
        w, m, v = params[name]
        flat_in += [w, m, v]
        out_shape += [jax.ShapeDtypeStruct(w.shape, F32)] * 4
    res = pl.pallas_call(
        body,
        name="adamw_small",
        in_specs=[SMEM_SPEC, VMEM_SPEC] + [VMEM_SPEC] * (3 * n),
        out_specs=[VMEM_SPEC] * (4 * n),
        out_shape=out_shape,
    )(j_arr, packed, *flat_in)
    return {name: tuple(res[4 * idx:4 * idx + 4]) for idx, name in enumerate(names)}


def _units(a):
    rows = a.shape[0]
    return jnp.transpose(a.reshape(rows, N_DEV, LANES), (1, 0, 2))


def kernel(x, meta_tokens, norm_gain, w_in, conv_w, conv_b, w_rg, b_rg, w_ig, b_ig, lru_lambda, ret_norm_gain, w_out, final_norm_gain, loss_target, m_meta_tokens, m_norm_gain, m_w_in, m_conv_w, m_conv_b, m_w_rg, m_b_rg, m_w_ig, m_b_ig, m_lru_lambda, m_ret_norm_gain, m_w_out, m_final_norm_gain, v_meta_tokens, v_norm_gain, v_w_in, v_conv_w, v_conv_b, v_w_rg, v_b_rg, v_w_ig, v_b_ig, v_lru_lambda, v_ret_norm_gain, v_w_out, v_final_norm_gain):
    s_len, d = x.shape[1], x.shape[2]
    d_lru = w_rg.shape[1] * w_rg.shape[2]
    d_ret = ret_norm_gain.shape[1]
    d_qk = HEADS * QK_DIM
    tp = s_len + CHUNK
    tm = TOKEN_TILE
    assert tp % tm == 0 and d_lru == HEADS * LANES and d_ret == HEADS * LANES
    ax, ay, ac = lax.axis_index("x"), lax.axis_index("y"), lax.axis_index("c")
    c_arr = jnp.reshape(ac, (1,)).astype(jnp.int32)
    j_arr = jnp.reshape(2 * ax + ay, (1,)).astype(jnp.int32)

    small = jnp.concatenate([meta_tokens, conv_w[0], jnp.zeros((4, meta_tokens.shape[1]), F32)], axis=0)
    wg, wo4, sg = _gather_weights(w_in[0], w_out[0], small)
    wo = wo4.reshape(N_CHIPS * wo4.shape[1], wo4.shape[2])
    cols = sg.shape[2]
    meta_full = jnp.transpose(sg[:, :N_META, :], (1, 0, 2)).reshape(N_META, N_CHIPS * cols)
    cw_full = jnp.transpose(sg[:, N_META:N_META + 4, :], (1, 0, 2)).reshape(4, N_CHIPS * cols)
    cw8 = jnp.concatenate([cw_full, jnp.zeros((4, cw_full.shape[1]), F32)], axis=0)

    half = QK_DIM // 2
    inv = ROPE_BASE ** (-jnp.arange(half, dtype=F32) / half)
    pos = (jnp.arange(tp) - PAD_ROWS).astype(F32)
    ang = pos[:, None] * inv[None, :]
    cos_t = jnp.tile(jnp.cos(ang), (1, LANES // half))
    sign = jnp.where((jnp.arange(LANES) % QK_DIM) < half, -1.0, 1.0).astype(F32)
    sin_t = jnp.tile(jnp.sin(ang), (1, LANES // half)) * sign[None, :]
    tables = _ret_tables()
    gain_f = final_norm_gain.reshape(1, d)

    hp, lx, lg, q, k, v, rg = _in_proj(x[0], meta_full, norm_gain, wg, cos_t, sin_t, tm, d_lru, d_qk, d_ret)
    hl, y_lru = _lru_fwd(lx, lg, cw8, conv_b, w_rg[0], b_rg, w_ig[0], b_ig, lru_lambda, tm)
    o, y_ret, rprev = _ret_fwd(q, k, v, rg, ret_norm_gain, tables)
    dh2, dy_lru, dy_ret, dwo, dgf, loss_acc = _out_proj_loss(y_lru, y_ret, hp, loss_target[0], wo, gain_f, tm)
    loss = lax.psum(loss_acc[0, 0], ("x", "y", "c"))

    dq, dk, dv, drg, dgain = _ret_bwd(q, k, v, rg, o, rprev, dy_ret, ret_norm_gain, cos_t, sin_t, tables)
    dlx, dlg, dcw, dcb, dwr, dbr, dwi, dbi, dlam = _lru_bwd(lx, lg, hl, dy_lru, cw8, conv_b, w_rg[0], b_rg,
                                                            w_ig[0], b_ig, lru_lambda, tm)
    grad_x, dmeta, dg1, dwg = _in_proj_bwd([dlx, dlg, dq, dk, dv, drg], hp, dh2, norm_gain, wg, s_len, tm)

    g_in = dwg.reshape(N_DEV, dwg.shape[1] // 2, dwg.shape[2])
    g_out = dwo.reshape(N_DEV, dwo.shape[0] // N_DEV, dwo.shape[1])
    vecs = [dg1, dcb, dbr, dbi, dlam, dgain, dgf]
    g_small = jnp.concatenate([dwr, dwi, _units(dmeta), _units(dcw[0:4])] + [_units(a) for a in vecs]
                              + [jnp.zeros((N_DEV, UNIT_ROWS - ROW_VEC - N_VEC, LANES), F32)], axis=1)
    r_in, r_out, r_small = _pair_exchange([g_in, g_out, g_small])
    q_in = _pair_sum(g_in, r_in, c_arr, 128, "grad_pair_sum_in")
    q_out = _pair_sum(g_out, r_out, c_arr, 128, "grad_pair_sum_out")
    q_small = _pair_sum(g_small, r_small, c_arr, UNIT_ROWS, "grad_pair_sum_small")
    e_in, e_out, e_small = _chip_exchange([q_in, q_out, q_small])
    f_in = _chip_sum(q_in, e_in, j_arr, 128, "grad_chip_sum_in")
    f_out = _chip_sum(q_out, e_out, j_arr, 128, "grad_chip_sum_out")
    f_small = _chip_sum(q_small, e_small, j_arr, UNIT_ROWS, "grad_chip_sum_small")
    s_in, s_out, o_small = _finish_exchange(f_in, f_out, f_small)

    res_in = _adamw_big(w_in[0], f_in, s_in, m_w_in[0], v_w_in[0], c_arr, 256, "adamw_w_in")
    res_out = _adamw_big(w_out[0], f_out, s_out, m_w_out[0], v_w_out[0], c_arr, 256, "adamw_w_out")
    small_params = {
        "meta_tokens": (meta_tokens, m_meta_tokens, v_meta_tokens),
        "norm_gain": (norm_gain, m_norm_gain, v_norm_gain),
        "conv_w": (conv_w[0], m_conv_w[0], v_conv_w[0]),
        "conv_b": (conv_b, m_conv_b, v_conv_b),
        "w_rg": (w_rg[0], m_w_rg[0], v_w_rg[0]),
        "b_rg": (b_rg, m_b_rg, v_b_rg),
        "w_ig": (w_ig[0], m_w_ig[0], v_w_ig[0]),
        "b_ig": (b_ig, m_b_ig, v_b_ig),
        "lru_lambda": (lru_lambda, m_lru_lambda, v_lru_lambda),
        "ret_norm_gain": (ret_norm_gain, m_ret_norm_gain, v_ret_norm_gain),
        "final_norm_gain": (gain_f, m_final_norm_gain.reshape(1, d), v_final_norm_gain.reshape(1, d)),
    }
    res = _adamw_small(j_arr, o_small, small_params)
    res["w_in"] = tuple(res_in)
    res["w_out"] = tuple(res_out)

    order = ["meta_tokens", "norm_gain", "w_in", "conv_w", "conv_b", "w_rg", "b_rg", "w_ig", "b_ig", "lru_lambda",
             "ret_norm_gain", "w_out", "final_norm_gain"]
    shapes = {"w_in": w_in.shape, "conv_w": conv_w.shape, "w_rg": w_rg.shape, "w_ig": w_ig.shape,
              "w_out": w_out.shape, "final_norm_gain": final_norm_gain.shape}
    outs = [loss, grad_x.reshape(x.shape)]
    for kind in range(4):
        for name in order:
            a = res[name][kind]
            outs.append(a.reshape(shapes[name]) if name in shapes else a)
    return tuple(outs)
```

```python
import functools

import jax
import jax.numpy as jnp
from jax import lax
from jax.experimental import pallas as pl
from jax.experimental.pallas import tpu as pltpu

F32 = jnp.float32
BF16 = jnp.bfloat16

N_META = 16
CHUNK = 128
PAD_ROWS = CHUNK - N_META
HEADS = 8
QK_DIM = 64
LANES = 128
SUBLANES = 8
LRU_C = 8.0
EPS = 1e-6
ROPE_BASE = 10000.0
ADAM_LR = 0.001
ADAM_B1 = 0.9
ADAM_B2 = 0.999
ADAM_EPS = 1e-08
ADAM_WD = 0.01
ADAM_STEP = 10
N_CHIPS = 4
N_DEV = 8
TOKEN_TILE = 384
VMEM_LIMIT = 58 * 1024 * 1024
MESH = pl.DeviceIdType.MESH

VMEM_SPEC = pl.BlockSpec(memory_space=pltpu.VMEM)
SMEM_SPEC = pl.BlockSpec(memory_space=pltpu.SMEM)
ANY_SPEC = pl.BlockSpec(memory_space=pl.ANY)

ROW_WR, ROW_WI, ROW_META, ROW_CONV, ROW_VEC, UNIT_ROWS = 0, 128, 256, 272, 276, 288
N_VEC = 7
ROW_LOSS = ROW_VEC + N_VEC


def _dot(a, b):
    return jnp.dot(a, b, preferred_element_type=F32)


def _dot_nt(a, b):
    return lax.dot_general(a, b, (((1,), (1,)), ((), ())), preferred_element_type=F32)


def _dot_tn(a, b):
    return lax.dot_general(a, b, (((0,), (0,)), ((), ())), preferred_element_type=F32)


def _sigmoid(x):
    return 0.5 * jnp.tanh(0.5 * x) + 0.5


def _shift_down(x, prev8, s):
    rolled = pltpu.roll(x, s, 0)
    rows = lax.broadcasted_iota(jnp.int32, (SUBLANES, x.shape[1]), 0)
    top = jnp.where(rows < s, pltpu.roll(prev8, s, 0), rolled[0:SUBLANES])
    return jnp.concatenate([top, rolled[SUBLANES:]], axis=0)


def _shift_up(x, next8, s):
    n = x.shape[0]
    rolled = pltpu.roll(x, n - s, 0)
    rows = lax.broadcasted_iota(jnp.int32, (SUBLANES, x.shape[1]), 0)
    bot = jnp.where(rows >= SUBLANES - s, pltpu.roll(next8, SUBLANES - s, 0), rolled[n - SUBLANES:n])
    return jnp.concatenate([rolled[:n - SUBLANES], bot], axis=0)


def _rot_partner(t):
    w = t.shape[1]
    lane = lax.broadcasted_iota(jnp.int32, t.shape, 1)
    first = (lane % QK_DIM) < (QK_DIM // 2)
    return jnp.where(first, pltpu.roll(t, w - QK_DIM // 2, 1), pltpu.roll(t, QK_DIM // 2, 1))


def _tile_lanes(t, reps):
    return jnp.concatenate([t] * reps, axis=1)


def _gather_weights(w_in, w_out, small):
    r_in, c_in = w_in.shape
    r_out, c_out = w_out.shape
    h_in, h_out = r_in // 2, r_out // 2

    def body(win_ref, wout_ref, small_ref, wg_ref, wo_ref, sg_ref, send_sems, recv_sems):
        x, y, c = lax.axis_index("x"), lax.axis_index("y"), lax.axis_index("c")
        j = 2 * x + y
        sibling = (x, y, 1 - c)
        chips = [(1 - x, y), (x, 1 - y), (1 - x, 1 - y)]

        wg_ref[j] = win_ref[...].astype(BF16)
        wo_ref[j] = wout_ref[...].astype(BF16)
        sg_ref[j] = small_ref[...]

        def half_in(jj, cc):
            return wg_ref.at[jj, pl.ds(cc * h_in, h_in), :]

        def half_out(jj, cc):
            return wo_ref.at[jj, pl.ds(cc * h_out, h_out), :]

        def copy(k, ref, to):
            return pltpu.make_async_remote_copy(src_ref=ref, dst_ref=ref, send_sem=send_sems.at[k],
                                                recv_sem=recv_sems.at[k], device_id=to, device_id_type=MESH)

        first = []
        for k, (cx, cy) in enumerate(chips):
            first.append(copy(k, half_in(j, c), (cx, cy, c)))
            first.append(copy(3 + k, half_out(j, c), (cx, cy, c)))
            first.append(copy(6 + k, sg_ref.at[j], (cx, cy, c)))
        for cp in first:
            cp.start()
        passed = []
        for k, (cx, cy) in enumerate(chips):
            jk = 2 * cx + cy
            copy(k, half_in(jk, c), sibling).wait_recv()
            fwd = copy(9 + k, half_in(jk, c), sibling)
            fwd.start()
            passed.append(fwd)
            copy(3 + k, half_out(jk, c), sibling).wait_recv()
            fwd = copy(12 + k, half_out(jk, c), sibling)
            fwd.start()
            passed.append(fwd)
        for k, (cx, cy) in enumerate(chips):
            jk = 2 * cx + cy
            copy(9 + k, half_in(jk, 1 - c), sibling).wait_recv()
            copy(12 + k, half_out(jk, 1 - c), sibling).wait_recv()
            copy(6 + k, sg_ref.at[jk], sibling).wait_recv()
        for cp in first + passed:
            cp.wait_send()

    return pl.pallas_call(
        body,
        name="gather_weights",
        out_shape=(jax.ShapeDtypeStruct((N_CHIPS, r_in, c_in), BF16),
                   jax.ShapeDtypeStruct((N_CHIPS, r_out, c_out), BF16),
                   jax.ShapeDtypeStruct((N_CHIPS,) + small.shape, F32)),
        in_specs=[VMEM_SPEC, VMEM_SPEC, VMEM_SPEC],
        out_specs=(VMEM_SPEC, VMEM_SPEC, VMEM_SPEC),
        scratch_shapes=[pltpu.SemaphoreType.DMA((15,)), pltpu.SemaphoreType.DMA((15,))],
        compiler_params=pltpu.CompilerParams(vmem_limit_bytes=VMEM_LIMIT),
    )(w_in, w_out, small)


def _proj_segments(d_lru, d_qk, d_ret, chunk_w):
    widths = [d_lru, d_lru, d_qk, d_qk, d_ret, d_ret]
    segs, col = [], 0
    for w in widths:
        parts, off = [], 0
        while off < w:
            jj, inner = divmod(col + off, chunk_w)
            take = min(w - off, chunk_w - inner)
            parts.append((jj, inner, off, take))
            off += take
        segs.append(parts)
        col += w
    return segs


def _in_proj(x2, meta, gain, wg, cos_t, sin_t, tm, d_lru, d_qk, d_ret):
    s_len, d = x2.shape
    tp = s_len + CHUNK
    nt, nb = tp // tm, tm // CHUNK
    segs = _proj_segments(d_lru, d_qk, d_ret, wg.shape[2])
    widths = [d_lru, d_lru, d_qk, d_qk, d_ret, d_ret]

    def body(*refs):
        xb = refs[:nb]
        meta_ref, g_ref, w_ref, cos_ref, sin_ref = refs[nb:nb + 5]
        hp_ref = refs[nb + 5]
        outs = refs[nb + 6:]
        i = pl.program_id(0)
        blocks = [r[...] for r in xb]
        head = jnp.concatenate([jnp.zeros((PAD_ROWS, d), F32), meta_ref[...]], axis=0)
        blocks[0] = jnp.where(i == 0, head, blocks[0])
        h = jnp.concatenate(blocks, axis=0)
        hp_ref[...] = h
        rinv = lax.rsqrt(jnp.mean(h * h, axis=-1, keepdims=True) + EPS)
        u = ((h * rinv) * g_ref[...]).astype(BF16)
        for out_ref, parts in zip(outs, segs):
            for jj, inner, off, take in parts:
                out_ref[:, off:off + take] = _dot(u, w_ref[jj, :, inner:inner + take])
        cos = _tile_lanes(cos_ref[...], d_qk // LANES)
        sin = _tile_lanes(sin_ref[...], d_qk // LANES)
        q = outs[2][...]
        outs[2][...] = q * cos + _rot_partner(q) * sin
        k = outs[3][...]
        outs[3][...] = (k * cos + _rot_partner(k) * sin) * (QK_DIM ** -0.5)

    x_specs = [pl.BlockSpec((CHUNK, d), functools.partial(lambda i, b: (jnp.maximum(i * nb + b - 1, 0), 0), b=b))
               for b in range(nb)]
    tile = lambda w: pl.BlockSpec((tm, w), lambda i: (i, 0))
    return pl.pallas_call(
        body,
        name="in_proj",
        grid=(nt,),
        in_specs=x_specs + [pl.BlockSpec(meta.shape, lambda i: (0, 0)),
                            pl.BlockSpec(gain.shape, lambda i: (0, 0)),
                            pl.BlockSpec(wg.shape, lambda i: (0, 0, 0)),
                            tile(LANES), tile(LANES)],
        out_specs=[tile(d)] + [tile(w) for w in widths],
        out_shape=[jax.ShapeDtypeStruct((tp, d), F32)] + [jax.ShapeDtypeStruct((tp, w), F32) for w in widths],
        compiler_params=pltpu.CompilerParams(dimension_semantics=("arbitrary",), vmem_limit_bytes=VMEM_LIMIT),
    )(*([x2] * nb), meta, gain, wg, cos_t, sin_t)


def _softplus_neg(lam):
    z = -lam
    e = jnp.exp(-jnp.abs(z))
    e1 = 1.0 + e
    log1p_e = jnp.where(e1 == 1.0, e, jnp.log(e1) * (e / (e1 - 1.0)))
    return jnp.maximum(z, 0.0) + log1p_e


def _lru_fwd(lx, lg, cw, cb, wr, br, wi, bi, lam, tm):
    tp, w = lx.shape
    nt = tp // tm
    per8 = tm // SUBLANES
    n_heads = wr.shape[0]

    def body(lx_ref, lxp_ref, lg_ref, cw_ref, cb_ref, wr_ref, br_ref, wi_ref, bi_ref, lam_ref,
             hl_ref, y_ref, xc_ref, r_ref, ig_ref, a_ref, beta_ref, w4_ref, u_s, carry):
        i = pl.program_id(0)

        @pl.when(i == 0)
        def _():
            carry[...] = jnp.zeros_like(carry)

        lxv = lx_ref[...]
        prev8 = jnp.where(i == 0, 0.0, lxp_ref[...])
        xc = cb_ref[...] + _shift_down(lxv, prev8, 3) * cw_ref[0:1, :]
        xc = xc + _shift_down(lxv, prev8, 2) * cw_ref[1:2, :]
        xc = xc + _shift_down(lxv, prev8, 1) * cw_ref[2:3, :]
        xc = xc + lxv * cw_ref[3:4, :]
        xc_ref[...] = xc
        pre_r, pre_i = [], []
        for hd in range(n_heads):
            xh = xc[:, hd * LANES:(hd + 1) * LANES].astype(BF16)
            pre_r.append(_dot(xh, wr_ref[hd].astype(BF16)))
            pre_i.append(_dot(xh, wi_ref[hd].astype(BF16)))
        r = _sigmoid(jnp.concatenate(pre_r, axis=1) + br_ref[...])
        ig = _sigmoid(jnp.concatenate(pre_i, axis=1) + bi_ref[...])
        r_ref[...] = r
        ig_ref[...] = ig
        log_a = (-LRU_C * r) * _softplus_neg(lam_ref[...])
        a = jnp.exp(log_a)
        a_ref[...] = a
        zz = -2.0 * log_a
        series = zz * (1.0 - zz * (0.5 - zz * (1.0 / 6.0)))
        a2 = a * a
        beta2 = jnp.maximum(jnp.where(zz < 0.015625, series, 1.0 - a2), 1e-37)
        rsb = lax.rsqrt(beta2)
        beta = beta2 * rsb
        beta_ref[...] = beta
        w4_ref[...] = a2 * rsb
        row = lax.broadcasted_iota(jnp.int32, (tm, 1), 0) + i * tm
        u_s[...] = jnp.where(row >= PAD_ROWS, beta * ig * xc, 0.0)

        def step(t, h):
            h = a_ref[pl.ds(t, 1), :] * h + u_s[pl.ds(t, 1), :]
            hl_ref[pl.ds(t, 1), :] = h
            return h

        carry[0:1, :] = lax.fori_loop(0, tm, step, carry[0:1, :], unroll=8)
        g = lg_ref[...]
        y_ref[...] = (hl_ref[...] * (g * _sigmoid(g))).astype(BF16)

    tile = pl.BlockSpec((tm, w), lambda i: (i, 0))
    prev = pl.BlockSpec((SUBLANES, w), lambda i: (jnp.maximum(i * per8 - 1, 0), 0))
    vec = pl.BlockSpec((1, w), lambda i: (0, 0))
    mat = pl.BlockSpec(wr.shape, lambda i: (0, 0, 0))
    f32_out = jax.ShapeDtypeStruct((tp, w), F32)
    return pl.pallas_call(
        body,
        name="lru_fwd",
        grid=(nt,),
        in_specs=[tile, prev, tile, pl.BlockSpec(cw.shape, lambda i: (0, 0)), vec, mat, vec, mat, vec, vec],
        out_specs=[tile] * 8,
        out_shape=[f32_out, jax.ShapeDtypeStruct((tp, w), BF16)] + [f32_out] * 6,
        scratch_shapes=[pltpu.VMEM((tm, w), F32), pltpu.VMEM((SUBLANES, w), F32)],
        compiler_params=pltpu.CompilerParams(dimension_semantics=("arbitrary",), vmem_limit_bytes=VMEM_LIMIT),
    )(lx, lx, lg, cw, cb, wr, br, wi, bi, lam)


def _ret_tables():
    log_g = jnp.log1p(-jnp.exp2(-5.0 - jnp.arange(HEADS, dtype=F32)))
    idx = jnp.arange(CHUNK, dtype=F32)
    diff = idx[:, None] - idx[None, :]
    dmask = jnp.where(diff[None] >= 0.0, jnp.exp(jnp.maximum(diff, 0.0)[None] * log_g[:, None, None]), 0.0)
    kdec = jnp.repeat(jnp.exp((CHUNK - 1.0 - idx)[:, None] * log_g[None, :]), QK_DIM, axis=1)
    qdec = jnp.repeat(jnp.exp((idx + 1.0)[:, None] * log_g[None, :]), QK_DIM, axis=1)
    g_chunk = jnp.exp(CHUNK * log_g)
    g_rows = jnp.repeat(g_chunk, QK_DIM).reshape(HEADS // 2, 2 * QK_DIM, 1)
    g_state = jnp.broadcast_to(g_rows, (HEADS // 2, 2 * QK_DIM, 2 * LANES))
    r_head = jnp.arange(2 * QK_DIM)[:, None] // QK_DIM
    c_head = jnp.arange(2 * LANES)[None, :] // LANES
    block_diag = (r_head == c_head).astype(F32)
    return dmask, qdec, kdec, g_state, block_diag


def _head_norm(o_h):
    mu = jnp.mean(o_h, axis=-1, keepdims=True)
    oc = o_h - mu
    var = jnp.mean(oc * oc, axis=-1, keepdims=True)
    rstd = lax.rsqrt(var + EPS)
    return oc * rstd, rstd


def _ret_fwd(q, k, v, rg, gain, tables):
    tp, d_qk = q.shape
    d_ret = v.shape[1]
    n_ch = tp // CHUNK
    n_pairs = HEADS // 2
    dmask, qdec, kdec, g_state, block_diag = tables

    def body(q_ref, k_ref, v_ref, rg_ref, gain_ref, dm_ref, qd_ref, kd_ref, gs_ref, bd_ref,
             o_ref, y_ref, rp_ref, state):
        n = pl.program_id(0)

        @pl.when(n == 0)
        def _():
            state[...] = jnp.zeros_like(state)

        lane = lax.broadcasted_iota(jnp.int32, (CHUNK, LANES), 1)
        for p in range(n_pairs):
            qs = slice(p * LANES, (p + 1) * LANES)
            vs = slice(p * 2 * LANES, (p + 1) * 2 * LANES)
            qp, kp = q_ref[:, qs], k_ref[:, qs]
            vb = v_ref[:, vs].astype(BF16)
            kb = kp.astype(BF16)
            qd = (qp * qd_ref[:, qs]).astype(BF16)
            kd = (kp * kd_ref[:, qs]).astype(BF16)
            st = state[p]
            st_b = st.astype(BF16)
            rp_ref[0, p] = st_b
            cross = _dot(qd, st_b)
            for e in range(2):
                hd = 2 * p + e
                hs = slice(hd * LANES, (hd + 1) * LANES)
                qm = jnp.where((lane // QK_DIM) == e, qp, 0.0).astype(BF16)
                s = _dot_nt(qm, kb) * dm_ref[hd]
                o_h = _dot(s.astype(BF16), vb[:, e * LANES:(e + 1) * LANES]) + cross[:, e * LANES:(e + 1) * LANES]
                o_ref[:, hs] = o_h
                xhat, _ = _head_norm(o_h)
                g = rg_ref[:, hs]
                y_ref[:, hs] = ((xhat * gain_ref[:, hs]) * (g * _sigmoid(g))).astype(BF16)
            state[p] = gs_ref[p] * st + bd_ref[...] * _dot_tn(kd, vb)

    ch = lambda w: pl.BlockSpec((CHUNK, w), lambda n: (n, 0))
    const2 = lambda a: pl.BlockSpec(a.shape, lambda n: (0, 0))
    const3 = lambda a: pl.BlockSpec(a.shape, lambda n: (0, 0, 0))
    return pl.pallas_call(
        body,
        name="ret_fwd",
        grid=(n_ch,),
        in_specs=[ch(d_qk), ch(d_qk), ch(d_ret), ch(d_ret), const2(gain), const3(dmask), const2(qdec), const2(kdec),
                  const3(g_state), const2(block_diag)],
        out_specs=[ch(d_ret), ch(d_ret),
                   pl.BlockSpec((1, n_pairs, 2 * QK_DIM, 2 * LANES), lambda n: (n, 0, 0, 0))],
        out_shape=[jax.ShapeDtypeStruct((tp, d_ret), F32), jax.ShapeDtypeStruct((tp, d_ret), BF16),
                   jax.ShapeDtypeStruct((n_ch, n_pairs, 2 * QK_DIM, 2 * LANES), BF16)],
        scratch_shapes=[pltpu.VMEM((n_pairs, 2 * QK_DIM, 2 * LANES), F32)],
        compiler_params=pltpu.CompilerParams(dimension_semantics=("arbitrary",), vmem_limit_bytes=VMEM_LIMIT),
    )(q, k, v, rg, gain, dmask, qdec, kdec, g_state, block_diag)


def _out_proj_loss(y_lru, y_ret, hp, tgt, wo, gain_f, tm):
    tp, d = hp.shape
    w_lru = y_lru.shape[1]
    w_mix = wo.shape[0]
    nt, nb = tp // tm, tm // CHUNK

    def body(*refs):
        yl_ref, yr_ref, hp_ref = refs[:3]
        tb = refs[3:3 + nb]
        wo_ref, gf_ref = refs[3 + nb:5 + nb]
        dh2_ref, dyl_ref, dyr_ref, dwo_ref, dgf_ref, loss_ref = refs[5 + nb:]
        i = pl.program_id(0)

        @pl.when(i == 0)
        def _():
            dwo_ref[...] = jnp.zeros_like(dwo_ref)
            dgf_ref[...] = jnp.zeros_like(dgf_ref)
            loss_ref[...] = jnp.zeros_like(loss_ref)

        yl, yr = yl_ref[...], yr_ref[...]
        h2 = hp_ref[...] + _dot(yl, wo_ref[0:w_lru, :]) + _dot(yr, wo_ref[w_lru:w_mix, :])
        rinv = lax.rsqrt(jnp.mean(h2 * h2, axis=-1, keepdims=True) + EPS)
        nrm = h2 * rinv
        gf = gf_ref[...]
        tgt_v = jnp.concatenate([r[...] for r in tb], axis=0)
        row = lax.broadcasted_iota(jnp.int32, (tm, 1), 0) + i * tm
        err = jnp.where(row >= CHUNK, nrm * gf - tgt_v, 0.0)
        loss_ref[...] += 0.5 * jnp.sum(jnp.mean(err * err, axis=-1, keepdims=True))
        dout = err * (1.0 / d)
        dgf_ref[...] += jnp.sum(dout * nrm, axis=0, keepdims=True)
        dn = dout * gf
        dh2 = rinv * (dn - nrm * jnp.mean(dn * nrm, axis=-1, keepdims=True))
        dh2_ref[...] = dh2
        dh2b = dh2.astype(BF16)
        dyl_ref[...] = _dot_nt(dh2b, wo_ref[0:w_lru, :])
        dyr_ref[...] = _dot_nt(dh2b, wo_ref[w_lru:w_mix, :])
        dwo_ref[0:w_lru, :] += _dot_tn(yl, dh2b)
        dwo_ref[w_lru:w_mix, :] += _dot_tn(yr, dh2b)

    tile = lambda w: pl.BlockSpec((tm, w), lambda i: (i, 0))
    t_specs = [pl.BlockSpec((CHUNK, d), functools.partial(lambda i, b: (jnp.maximum(i * nb + b - 1, 0), 0), b=b))
               for b in range(nb)]
    return pl.pallas_call(
        body,
        name="out_proj_loss",
        grid=(nt,),
        in_specs=[tile(w_lru), tile(w_mix - w_lru), tile(d)] + t_specs +
                 [pl.BlockSpec(wo.shape, lambda i: (0, 0)), pl.BlockSpec(gain_f.shape, lambda i: (0, 0))],
        out_specs=[tile(d), tile(w_lru), tile(w_mix - w_lru), pl.BlockSpec(wo.shape, lambda i: (0, 0)),
                   pl.BlockSpec((1, d), lambda i: (0, 0)), pl.BlockSpec((SUBLANES, LANES), lambda i: (0, 0))],
        out_shape=[jax.ShapeDtypeStruct((tp, d), F32), jax.ShapeDtypeStruct((tp, w_lru), F32),
                   jax.ShapeDtypeStruct((tp, w_mix - w_lru), F32), jax.ShapeDtypeStruct(wo.shape, F32),
                   jax.ShapeDtypeStruct((1, d), F32), jax.ShapeDtypeStruct((SUBLANES, LANES), F32)],
        compiler_params=pltpu.CompilerParams(dimension_semantics=("arbitrary",), vmem_limit_bytes=VMEM_LIMIT),
    )(y_lru, y_ret, hp, *([tgt] * nb), wo, gain_f)


def _ret_bwd(q, k, v, rg, o, rprev, dy, gain, cos_t, sin_t, tables):
    tp, d_qk = q.shape
    d_ret = v.shape[1]
    n_ch = tp // CHUNK
    n_pairs = HEADS // 2
    dmask, qdec, kdec, g_state, block_diag = tables

    def body(q_ref, k_ref, v_ref, rg_ref, o_ref, rp_ref, dy_ref, gain_ref, cos_ref, sin_ref,
             dm_ref, qd_ref, kd_ref, gs_ref, bd_ref, dq_ref, dk_ref, dv_ref, drg_ref, dgain_ref, dstate):
        n = pl.program_id(0)

        @pl.when(n == 0)
        def _():
            dstate[...] = jnp.zeros_like(dstate)
            dgain_ref[...] = jnp.zeros_like(dgain_ref)

        lane = lax.broadcasted_iota(jnp.int32, (CHUNK, LANES), 1)
        dq_parts, dk_parts = [], []
        for p in range(n_pairs):
            qs = slice(p * LANES, (p + 1) * LANES)
            vs = slice(p * 2 * LANES, (p + 1) * 2 * LANES)
            do_parts = []
            for e in range(2):
                hd = 2 * p + e
                hs = slice(hd * LANES, (hd + 1) * LANES)
                xhat, rstd = _head_norm(o_ref[:, hs])
                g = rg_ref[:, hs]
                sg = _sigmoid(g)
                dyh = dy_ref[:, hs]
                gn = gain_ref[:, hs]
                d_on = dyh * (g * sg)
                drg_ref[:, hs] = (dyh * (xhat * gn) * (sg * (1.0 + g * (1.0 - sg)))).astype(BF16)
                dgain_ref[:, hs] += jnp.sum(d_on * xhat, axis=0, keepdims=True)
                dxh = d_on * gn
                do_parts.append(rstd * (dxh - jnp.mean(dxh, axis=-1, keepdims=True)
                                        - xhat * jnp.mean(dxh * xhat, axis=-1, keepdims=True)))
            do_b = jnp.concatenate(do_parts, axis=1).astype(BF16)
            qp, kp = q_ref[:, qs], k_ref[:, qs]
            vb = v_ref[:, vs].astype(BF16)
            kb = kp.astype(BF16)
            qd = (qp * qd_ref[:, qs]).astype(BF16)
            kd = (kp * kd_ref[:, qs]).astype(BF16)
            dst = dstate[p]
            dst_b = dst.astype(BF16)
            dqp = _dot_nt(do_b, rp_ref[0, p]) * qd_ref[:, qs]
            dkp = _dot_nt(vb, dst_b) * kd_ref[:, qs]
            dvp = _dot(kd, dst_b)
            dv_parts = []
            for e in range(2):
                hd = 2 * p + e
                es = slice(e * LANES, (e + 1) * LANES)
                mine = (lane // QK_DIM) == e
                qm = jnp.where(mine, qp, 0.0).astype(BF16)
                km = jnp.where(mine, kp, 0.0).astype(BF16)
                dm = dm_ref[hd]
                s = (_dot_nt(qm, kb) * dm).astype(BF16)
                ds = (_dot_nt(do_b[:, es], vb[:, es]) * dm).astype(BF16)
                dv_parts.append(dvp[:, es] + _dot_tn(s, do_b[:, es]))
                dqp = dqp + _dot(ds, km)
                dkp = dkp + _dot_tn(ds, qm)
            dv_ref[:, vs] = jnp.concatenate(dv_parts, axis=1).astype(BF16)
            dstate[p] = gs_ref[p] * dst + bd_ref[...] * _dot_tn(qd, do_b)
            dq_parts.append(dqp)
            dk_parts.append(dkp)
        cos = _tile_lanes(cos_ref[...], d_qk // LANES)
        sin = _tile_lanes(sin_ref[...], d_qk // LANES)
        dq = jnp.concatenate(dq_parts, axis=1)
        dk = jnp.concatenate(dk_parts, axis=1) * (QK_DIM ** -0.5)
        dq_ref[...] = (dq * cos + _rot_partner(dq * sin)).astype(BF16)
        dk_ref[...] = (dk * cos + _rot_partner(dk * sin)).astype(BF16)

    last = n_ch - 1
    ch = lambda w: pl.BlockSpec((CHUNK, w), lambda n: (last - n, 0))
    const2 = lambda a: pl.BlockSpec(a.shape, lambda n: (0, 0))
    const3 = lambda a: pl.BlockSpec(a.shape, lambda n: (0, 0, 0))
    return pl.pallas_call(
        body,
        name="ret_bwd",
        grid=(n_ch,),
        in_specs=[ch(d_qk), ch(d_qk), ch(d_ret), ch(d_ret), ch(d_ret),
                  pl.BlockSpec((1, n_pairs, 2 * QK_DIM, 2 * LANES), lambda n: (last - n, 0, 0, 0)),
                  ch(d_ret), const2(gain), ch(LANES), ch(LANES),
                  const3(dmask), const2(qdec), const2(kdec), const3(g_state), const2(block_diag)],
        out_specs=[ch(d_qk), ch(d_qk), ch(d_ret), ch(d_ret), pl.BlockSpec((1, d_ret), lambda n: (0, 0))],
        out_shape=[jax.ShapeDtypeStruct((tp, d_qk), BF16), jax.ShapeDtypeStruct((tp, d_qk), BF16),
                   jax.ShapeDtypeStruct((tp, d_ret), BF16), jax.ShapeDtypeStruct((tp, d_ret), BF16),
                   jax.ShapeDtypeStruct((1, d_ret), F32)],
        scratch_shapes=[pltpu.VMEM((n_pairs, 2 * QK_DIM, 2 * LANES), F32)],
        compiler_params=pltpu.CompilerParams(dimension_semantics=("arbitrary",), vmem_limit_bytes=VMEM_LIMIT),
    )(q, k, v, rg, o, rprev, dy, gain, cos_t, sin_t, dmask, qdec, kdec, g_state, block_diag)


def _lru_bwd(lx, lg, hl, dy, saved, cw, wr, wi, lam, tm):
    tp, w = lx.shape
    nt = tp // tm
    per8 = tm // SUBLANES
    n_heads = wr.shape[0]

    def body(lx_ref, lg_ref, hl_ref, hlp_ref, dy_ref, xc_ref, r_ref, ig_ref, a_ref, beta_ref, w4_ref,
             cw_ref, wr_ref, wi_ref, lam_ref,
             dlx_ref, dlg_ref, dcw_ref, dcb_ref, dwr_ref, dbr_ref, dwi_ref, dbi_ref, dlam_ref,
             g_s, dh_s, carry, dxc_next):
        i = pl.program_id(0)
        first_tile = i == nt - 1

        @pl.when(i == 0)
        def _():
            carry[...] = jnp.zeros_like(carry)
            dxc_next[...] = jnp.zeros_like(dxc_next)
            for r in (dcw_ref, dcb_ref, dwr_ref, dbr_ref, dwi_ref, dbi_ref, dlam_ref):
                r[...] = jnp.zeros_like(r)

        lxv = lx_ref[...]
        a, beta, r, ig, xc = a_ref[...], beta_ref[...], r_ref[...], ig_ref[...], xc_ref[...]
        g = lg_ref[...]
        sg = _sigmoid(g)
        dyv = dy_ref[...]
        hlv = hl_ref[...]
        dlg_ref[...] = (dyv * hlv * (sg * (1.0 + g * (1.0 - sg)))).astype(BF16)
        g_s[...] = dyv * (g * sg)

        def step(s, cn):
            t = tm - 1 - s
            dh = g_s[pl.ds(t, 1), :] + cn
            dh_s[pl.ds(t, 1), :] = dh
            return a_ref[pl.ds(t, 1), :] * dh

        carry[0:1, :] = lax.fori_loop(0, tm, step, carry[0:1, :], unroll=8)
        dh = dh_s[...]
        hprev = _shift_down(hlv, jnp.where(first_tile, 0.0, hlp_ref[...]), 1)
        row = lax.broadcasted_iota(jnp.int32, (tm, 1), 0) + (nt - 1 - i) * tm
        du = jnp.where(row >= PAD_ROWS, dh, 0.0)
        da = dh * hprev
        dbeta = du * ig * xc
        d_ig = du * beta * xc
        dxc = du * beta * ig
        dloga = da * a - dbeta * w4_ref[...]
        lam_v = lam_ref[...]
        dlam_ref[...] += jnp.sum(dloga * r, axis=0, keepdims=True) * (LRU_C * _sigmoid(-lam_v))
        dpr = (dloga * (-LRU_C * _softplus_neg(lam_v))) * r * (1.0 - r)
        dpi = d_ig * ig * (1.0 - ig)
        dbr_ref[...] += jnp.sum(dpr, axis=0, keepdims=True)
        dbi_ref[...] += jnp.sum(dpi, axis=0, keepdims=True)
        dxc_parts = []
        for hd in range(n_heads):
            hs = slice(hd * LANES, (hd + 1) * LANES)
            xh = xc[:, hs].astype(BF16)
            dprh = dpr[:, hs].astype(BF16)
            dpih = dpi[:, hs].astype(BF16)
            dwr_ref[hd] += _dot_tn(xh, dprh)
            dwi_ref[hd] += _dot_tn(xh, dpih)
            dxc_parts.append(_dot_nt(dprh, wr_ref[hd].astype(BF16)) + _dot_nt(dpih, wi_ref[hd].astype(BF16)))
        dxc = dxc + jnp.concatenate(dxc_parts, axis=1)
        nxt = dxc_next[...]
        up1, up2, up3 = _shift_up(dxc, nxt, 1), _shift_up(dxc, nxt, 2), _shift_up(dxc, nxt, 3)
        dlx = dxc * cw_ref[3:4, :]
        dlx = dlx + up1 * cw_ref[2:3, :]
        dlx = dlx + up2 * cw_ref[1:2, :]
        dlx = dlx + up3 * cw_ref[0:1, :]
        dlx_ref[...] = dlx.astype(BF16)
        dxc_next[...] = dxc[0:SUBLANES]
        dcb_ref[...] += jnp.sum(dxc, axis=0, keepdims=True)
        dcw_ref[0:1, :] += jnp.sum(up3 * lxv, axis=0, keepdims=True)
        dcw_ref[1:2, :] += jnp.sum(up2 * lxv, axis=0, keepdims=True)
        dcw_ref[2:3, :] += jnp.sum(up1 * lxv, axis=0, keepdims=True)
        dcw_ref[3:4, :] += jnp.sum(dxc * lxv, axis=0, keepdims=True)

    last = nt - 1
    tile = pl.BlockSpec((tm, w), lambda i: (last - i, 0))
    prev = pl.BlockSpec((SUBLANES, w), lambda i: (jnp.maximum((last - i) * per8 - 1, 0), 0))
    vec = pl.BlockSpec((1, w), lambda i: (0, 0))
    mat = pl.BlockSpec(wr.shape, lambda i: (0, 0, 0))
    cwb = pl.BlockSpec(cw.shape, lambda i: (0, 0))
    return pl.pallas_call(
        body,
        name="lru_bwd",
        grid=(nt,),
        in_specs=[tile, tile, tile, prev, tile] + [tile] * 6 + [cwb, mat, mat, vec],
        out_specs=[tile, tile, cwb, vec, mat, vec, mat, vec, vec],
        out_shape=[jax.ShapeDtypeStruct((tp, w), BF16), jax.ShapeDtypeStruct((tp, w), BF16),
                   jax.ShapeDtypeStruct(cw.shape, F32), jax.ShapeDtypeStruct((1, w), F32),
                   jax.ShapeDtypeStruct(wr.shape, F32), jax.ShapeDtypeStruct((1, w), F32),
                   jax.ShapeDtypeStruct(wr.shape, F32), jax.ShapeDtypeStruct((1, w), F32),
                   jax.ShapeDtypeStruct((1, w), F32)],
        scratch_shapes=[pltpu.VMEM((tm, w), F32), pltpu.VMEM((tm, w), F32),
                        pltpu.VMEM((SUBLANES, w), F32), pltpu.VMEM((SUBLANES, w), F32)],
        compiler_params=pltpu.CompilerParams(dimension_semantics=("arbitrary",), vmem_limit_bytes=VMEM_LIMIT),
    )(lx, lg, hl, hl, dy, *saved, cw, wr, wi, lam)


def _in_proj_bwd(dparts, hp, dh2, gain, wg, s_len, tm):
    tp, d = hp.shape
    nt = tp // tm
    widths = [p.shape[1] for p in dparts]
    segs = _proj_segments(widths[0], widths[2], widths[4], wg.shape[2])

    def body(*refs):
        dp = refs[:6]
        hp_ref, dh2_ref, g_ref, w_ref = refs[6:10]
        gx_ref, dmeta_ref, dg_ref, dwg_ref = refs[10:14]
        acc, stage, sem = refs[14:]
        i = pl.program_id(0)

        @pl.when(i == 0)
        def _():
            acc[...] = jnp.zeros_like(acc)
            dg_ref[...] = jnp.zeros_like(dg_ref)

        h = hp_ref[...]
        rinv = lax.rsqrt(jnp.mean(h * h, axis=-1, keepdims=True) + EPS)
        nrm = h * rinv
        gv = g_ref[...]
        u = (nrm * gv).astype(BF16)
        du = jnp.zeros((tm, d), F32)
        for p_ref, parts in zip(dp, segs):
            for jj, inner, off, take in parts:
                seg = p_ref[:, off:off + take]
                du = du + _dot_nt(seg, w_ref[jj, :, inner:inner + take])
                acc[jj, :, inner:inner + take] += _dot_tn(u, seg)
        dg_ref[...] += jnp.sum(du * nrm, axis=0, keepdims=True)
        dn = du * gv
        dh = dh2_ref[...] + rinv * (dn - nrm * jnp.mean(dn * nrm, axis=-1, keepdims=True))
        stage[...] = dh

        @pl.when(i == 0)
        def _():
            dmeta_ref[...] = dh[PAD_ROWS:CHUNK]
            cp = pltpu.make_async_copy(stage.at[pl.ds(CHUNK, tm - CHUNK), :], gx_ref.at[pl.ds(0, tm - CHUNK), :], sem)
            cp.start()
            cp.wait()

        @pl.when(i > 0)
        def _():
            start = pl.multiple_of(i * tm - CHUNK, CHUNK)
            cp = pltpu.make_async_copy(stage, gx_ref.at[pl.ds(start, tm), :], sem)
            cp.start()
            cp.wait()

        @pl.when(i == nt - 1)
        def _():
            cp = pltpu.make_async_copy(acc, dwg_ref, sem)
            cp.start()
            cp.wait()

    tile = lambda w: pl.BlockSpec((tm, w), lambda i: (i, 0))
    return pl.pallas_call(
        body,
        name="in_proj_bwd",
        grid=(nt,),
        in_specs=[tile(w) for w in widths] + [tile(d), tile(d), pl.BlockSpec(gain.shape, lambda i: (0, 0)),
                                              pl.BlockSpec(wg.shape, lambda i: (0, 0, 0))],
        out_specs=[ANY_SPEC, pl.BlockSpec((N_META, d), lambda i: (0, 0)), pl.BlockSpec((1, d), lambda i: (0, 0)),
                   ANY_SPEC],
        out_shape=[jax.ShapeDtypeStruct((s_len, d), F32), jax.ShapeDtypeStruct((N_META, d), F32),
                   jax.ShapeDtypeStruct((1, d), F32), jax.ShapeDtypeStruct(wg.shape, F32)],
        scratch_shapes=[pltpu.VMEM(wg.shape, F32), pltpu.VMEM((tm, d), F32), pltpu.SemaphoreType.DMA],
        compiler_params=pltpu.CompilerParams(dimension_semantics=("arbitrary",), vmem_limit_bytes=VMEM_LIMIT),
    )(*dparts, hp, dh2, gain, wg)


def _pair_exchange(bufs):
    n = len(bufs)

    def body(*refs):
        src, dst = refs[:n], refs[n:2 * n]
        send_sems, recv_sems = refs[2 * n:]
        x, y, c = lax.axis_index("x"), lax.axis_index("y"), lax.axis_index("c")
        copies = []
        for b in range(n):
            for jj in range(N_CHIPS):
                copies.append(pltpu.make_async_remote_copy(
                    src_ref=src[b].at[2 * jj + 1 - c], dst_ref=dst[b].at[jj],
                    send_sem=send_sems.at[b * N_CHIPS + jj], recv_sem=recv_sems.at[b * N_CHIPS + jj],
                    device_id=(x, y, 1 - c), device_id_type=MESH))
        for cp in copies:
            cp.start()
        for cp in copies:
            cp.wait()

    return pl.pallas_call(
        body,
        name="grad_pair_exchange",
        in_specs=[ANY_SPEC] * n,
        out_specs=[ANY_SPEC] * n,
        out_shape=[jax.ShapeDtypeStruct((N_CHIPS,) + b.shape[1:], b.dtype) for b in bufs],
        scratch_shapes=[pltpu.SemaphoreType.DMA((n * N_CHIPS,)), pltpu.SemaphoreType.DMA((n * N_CHIPS,))],
    )(*bufs)


def _pair_sum(buf, recv, c_arr, tr, name):
    _, rows, cols = buf.shape

    def body(c_ref, mine_ref, got_ref, out_ref):
        out_ref[...] = (mine_ref[...] + got_ref[...]).astype(BF16)

    grid_spec = pltpu.PrefetchScalarGridSpec(
        num_scalar_prefetch=1,
        grid=(N_CHIPS, rows // tr),
        in_specs=[pl.BlockSpec((1, tr, cols), lambda jj, r, c_ref: (2 * jj + c_ref[0], r, 0)),
                  pl.BlockSpec((1, tr, cols), lambda jj, r, c_ref: (jj, r, 0))],
        out_specs=pl.BlockSpec((1, tr, cols), lambda jj, r, c_ref: (jj, r, 0)),
    )
    return pl.pallas_call(
        body,
        name=name,
        grid_spec=grid_spec,
        out_shape=jax.ShapeDtypeStruct((N_CHIPS, rows, cols), BF16),
    )(c_arr, buf, recv)


def _chip_exchange(bufs):
    n = len(bufs)

    def body(*refs):
        src, dst = refs[:n], refs[n:2 * n]
        send_sems, recv_sems = refs[2 * n:]
        x, y, c = lax.axis_index("x"), lax.axis_index("y"), lax.axis_index("c")
        j = 2 * x + y
        chips = [(1 - x, y), (x, 1 - y), (1 - x, 1 - y)]
        copies = []
        for b in range(n):
            for k, (cx, cy) in enumerate(chips):
                copies.append(pltpu.make_async_remote_copy(
                    src_ref=src[b].at[2 * cx + cy], dst_ref=dst[b].at[j],
                    send_sem=send_sems.at[b * 3 + k], recv_sem=recv_sems.at[b * 3 + k],
                    device_id=(cx, cy, c), device_id_type=MESH))
        for cp in copies:
            cp.start()
        for cp in copies:
            cp.wait()

    return pl.pallas_call(
        body,
        name="grad_chip_exchange",
        in_specs=[ANY_SPEC] * n,
        out_specs=[ANY_SPEC] * n,
        out_shape=[jax.ShapeDtypeStruct(b.shape, b.dtype) for b in bufs],
        scratch_shapes=[pltpu.SemaphoreType.DMA((n * 3,)), pltpu.SemaphoreType.DMA((n * 3,))],
    )(*bufs)


def _chip_sum(mine, got, j_arr, tr, name, loss_part=None):
    _, rows, cols = got.shape
    extra = [] if loss_part is None else [loss_part]

    def body(j_ref, mine_ref, got_ref, *rest):
        out_ref = rest[-1]
        j = j_ref[0]
        acc = None
        for jj in range(N_CHIPS):
            term = jnp.where(j == jj, mine_ref[0], got_ref[jj]).astype(F32)
            acc = term if acc is None else acc + term
        out_ref[...] = acc
        if loss_part is not None:
            out_ref[ROW_LOSS:ROW_LOSS + 1, :] = rest[0][0:1, :]

    grid_spec = pltpu.PrefetchScalarGridSpec(
        num_scalar_prefetch=1,
        grid=(rows // tr,),
        in_specs=[pl.BlockSpec((1, tr, cols), lambda r, j_ref: (j_ref[0], r, 0)),
                  pl.BlockSpec((N_CHIPS, tr, cols), lambda r, j_ref: (0, r, 0))] +
                 [pl.BlockSpec(e.shape, lambda r, j_ref: (0, 0)) for e in extra],
        out_specs=pl.BlockSpec((tr, cols), lambda r, j_ref: (r, 0)),
    )
    return pl.pallas_call(
        body,
        name=name,
        grid_spec=grid_spec,
        out_shape=jax.ShapeDtypeStruct((rows, cols), F32),
    )(j_arr, mine, got, *extra)


def _finish_exchange(f_in, f_out, f_small):
    def body(fin_ref, fout_ref, fs_ref, rin_ref, rout_ref, os_ref, send_sems, recv_sems, local_sem):
        x, y, c = lax.axis_index("x"), lax.axis_index("y"), lax.axis_index("c")
        j = 2 * x + y
        me = 2 * j + c
        sibling = (x, y, 1 - c)
        chips = [(1 - x, y), (x, 1 - y), (1 - x, 1 - y)]
        local = pltpu.make_async_copy(fs_ref, os_ref.at[me], local_sem)
        local.start()

        def copy(k, src, dst, to):
            return pltpu.make_async_remote_copy(src_ref=src, dst_ref=dst, send_sem=send_sems.at[k],
                                                recv_sem=recv_sems.at[k], device_id=to, device_id_type=MESH)

        first = [copy(0, fin_ref, rin_ref, sibling), copy(1, fout_ref, rout_ref, sibling),
                 copy(2, fs_ref, os_ref.at[me], sibling)]
        first += [copy(3 + k, fs_ref, os_ref.at[me], (cx, cy, c)) for k, (cx, cy) in enumerate(chips)]
        for cp in first:
            cp.start()
        passed = []
        for k, (cx, cy) in enumerate(chips):
            unit = 2 * (2 * cx + cy) + c
            copy(3 + k, fs_ref, os_ref.at[unit], sibling).wait_recv()
            fwd = copy(6 + k, os_ref.at[unit], os_ref.at[unit], sibling)
            fwd.start()
            passed.append(fwd)
        copy(0, fin_ref, rin_ref, sibling).wait_recv()
        copy(1, fout_ref, rout_ref, sibling).wait_recv()
        copy(2, fs_ref, os_ref.at[2 * j + 1 - c], sibling).wait_recv()
        for k, (cx, cy) in enumerate(chips):
            unit = 2 * (2 * cx + cy) + 1 - c
            copy(6 + k, fs_ref, os_ref.at[unit], sibling).wait_recv()
        for cp in first + passed:
            cp.wait_send()
        local.wait()

    return pl.pallas_call(
        body,
        name="grad_finish_exchange",
        in_specs=[ANY_SPEC] * 3,
        out_specs=[ANY_SPEC] * 3,
        out_shape=[jax.ShapeDtypeStruct(f_in.shape, F32), jax.ShapeDtypeStruct(f_out.shape, F32),
                   jax.ShapeDtypeStruct((N_DEV,) + f_small.shape, F32)],
        scratch_shapes=[pltpu.SemaphoreType.DMA((9,)), pltpu.SemaphoreType.DMA((9,)), pltpu.SemaphoreType.DMA],
    )(f_in, f_out, f_small)


def _adamw_math(w, g, m, v):
    m = ADAM_B1 * m + (1.0 - ADAM_B1) * g
    v = ADAM_B2 * v + (1.0 - ADAM_B2) * (g * g)
    m_hat = m / (1.0 - ADAM_B1 ** ADAM_STEP)
    v_hat = v / (1.0 - ADAM_B2 ** ADAM_STEP)
    delta = -ADAM_LR * (m_hat / (jnp.sqrt(v_hat) + ADAM_EPS) + ADAM_WD * w)
    return delta, m, v


def _adamw_big(w, g_mine, g_sib, m, v, c_arr, tr, name):
    rows, cols = w.shape
    half = rows // 2
    per = half // tr

    def body(c_ref, w_ref, gm_ref, gs_ref, m_ref, v_ref, g_ref, d_ref, mo_ref, vo_ref):
        g = jnp.where(pl.program_id(0) == c_ref[0], gm_ref[...], gs_ref[...])
        g_ref[...] = g
        d_ref[...], mo_ref[...], vo_ref[...] = _adamw_math(w_ref[...], g, m_ref[...], v_ref[...])

    full = pl.BlockSpec((tr, cols), lambda h, r, c_ref: (h * per + r, 0))
    unit = pl.BlockSpec((tr, cols), lambda h, r, c_ref: (r, 0))
    grid_spec = pltpu.PrefetchScalarGridSpec(
        num_scalar_prefetch=1,
        grid=(2, per),
        in_specs=[full, unit, unit, full, full],
        out_specs=[full] * 4,
    )
    return pl.pallas_call(
        body,
        name=name,
        grid_spec=grid_spec,
        out_shape=[jax.ShapeDtypeStruct(w.shape, F32)] * 4,
    )(c_arr, w, g_mine, g_sib, m, v)


def _adamw_small(j_arr, packed, params):
    names = list(params)
    n = len(names)
    vec_names = ["norm_gain", "conv_b", "b_rg", "b_ig", "lru_lambda", "ret_norm_gain", "final_norm_gain"]

    def body(j_ref, pk_ref, *refs):
        ins = refs[:3 * n]
        outs = refs[3 * n:]
        j = j_ref[0]

        def shard(row, rows):
            return jnp.concatenate([pk_ref[2 * j, row:row + rows, :], pk_ref[2 * j + 1, row:row + rows, :]], axis=1)

        for idx, name in enumerate(names):
            if name == "w_rg":
                g = pk_ref[:, ROW_WR:ROW_WR + LANES, :]
            elif name == "w_ig":
                g = pk_ref[:, ROW_WI:ROW_WI + LANES, :]
            elif name == "meta_tokens":
                g = shard(ROW_META, N_META)
            elif name == "conv_w":
                g = shard(ROW_CONV, 4)
            else:
                row = ROW_VEC + vec_names.index(name)
                g = jnp.concatenate([pk_ref[u, row:row + 1, :] for u in range(N_DEV)], axis=1)
            w_ref, m_ref, v_ref = ins[3 * idx:3 * idx + 3]
            delta, m, v = _adamw_math(w_ref[...], g, m_ref[...], v_ref[...])
            g_ref, d_ref, mo_ref, vo_ref = outs[4 * idx:4 * idx + 4]
            g_ref[...], d_ref[...], mo_ref[...], vo_ref[...] = g, delta, m, v
        total = pk_ref[0, ROW_LOSS:ROW_LOSS + 1, :]
        for u in range(1, N_DEV):
            total = total + pk_ref[u, ROW_LOSS:ROW_LOSS + 1, :]
        outs[4 * n][...] = jnp.broadcast_to(total, (SUBLANES, LANES))

    flat_in, out_shape = [], []
    for name in names:
        w, m, v = params[name]
        flat_in += [w, m, v]
        out_shape += [jax.ShapeDtypeStruct(w.shape, F32)] * 4
    out_shape.append(jax.ShapeDtypeStruct((SUBLANES, LANES), F32))
    res = pl.pallas_call(
        body,
        name="adamw_small",
        in_specs=[SMEM_SPEC, VMEM_SPEC] + [VMEM_SPEC] * (3 * n),
        out_specs=[VMEM_SPEC] * (4 * n + 1),
        out_shape=out_shape,
    )(j_arr, packed, *flat_in)
    return {name: tuple(res[4 * idx:4 * idx + 4]) for idx, name in enumerate(names)}, res[4 * n][0, 0]


def _units(a):
    rows = a.shape[0]
    return jnp.transpose(a.reshape(rows, N_DEV, LANES), (1, 0, 2))


def kernel(x, meta_tokens, norm_gain, w_in, conv_w, conv_b, w_rg, b_rg, w_ig, b_ig, lru_lambda, ret_norm_gain, w_out, final_norm_gain, loss_target, m_meta_tokens, m_norm_gain, m_w_in, m_conv_w, m_conv_b, m_w_rg, m_b_rg, m_w_ig, m_b_ig, m_lru_lambda, m_ret_norm_gain, m_w_out, m_final_norm_gain, v_meta_tokens, v_norm_gain, v_w_in, v_conv_w, v_conv_b, v_w_rg, v_b_rg, v_w_ig, v_b_ig, v_lru_lambda, v_ret_norm_gain, v_w_out, v_final_norm_gain):
    s_len, d = x.shape[1], x.shape[2]
    d_lru = w_rg.shape[1] * w_rg.shape[2]
    d_ret = ret_norm_gain.shape[1]
    d_qk = HEADS * QK_DIM
    tp = s_len + CHUNK
    tm = TOKEN_TILE
    assert tp % tm == 0 and d_lru == HEADS * LANES and d_ret == HEADS * LANES
    ax, ay, ac = lax.axis_index("x"), lax.axis_index("y"), lax.axis_index("c")
    c_arr = jnp.reshape(ac, (1,)).astype(jnp.int32)
    j_arr = jnp.reshape(2 * ax + ay, (1,)).astype(jnp.int32)

    small = jnp.concatenate([meta_tokens, conv_w[0], jnp.zeros((4, meta_tokens.shape[1]), F32)], axis=0)
    wg, wo4, sg = _gather_weights(w_in[0], w_out[0], small)
    wo = wo4.reshape(N_CHIPS * wo4.shape[1], wo4.shape[2])
    cols = sg.shape[2]
    meta_full = jnp.transpose(sg[:, :N_META, :], (1, 0, 2)).reshape(N_META, N_CHIPS * cols)
    cw_full = jnp.transpose(sg[:, N_META:N_META + 4, :], (1, 0, 2)).reshape(4, N_CHIPS * cols)
    cw8 = jnp.concatenate([cw_full, jnp.zeros((4, cw_full.shape[1]), F32)], axis=0)

    half = QK_DIM // 2
    inv = ROPE_BASE ** (-jnp.arange(half, dtype=F32) / half)
    pos = (jnp.arange(tp) - PAD_ROWS).astype(F32)
    ang = pos[:, None] * inv[None, :]
    cos_t = jnp.tile(jnp.cos(ang), (1, LANES // half))
    sign = jnp.where((jnp.arange(LANES) % QK_DIM) < half, -1.0, 1.0).astype(F32)
    sin_t = jnp.tile(jnp.sin(ang), (1, LANES // half)) * sign[None, :]
    tables = _ret_tables()
    gain_f = final_norm_gain.reshape(1, d)

    hp, lx, lg, q, k, v, rg = _in_proj(x[0], meta_full, norm_gain, wg, cos_t, sin_t, tm, d_lru, d_qk, d_ret)
    hl, y_lru, *lru_saved = _lru_fwd(lx, lg, cw8, conv_b, w_rg[0], b_rg, w_ig[0], b_ig, lru_lambda, tm)
    o, y_ret, rprev = _ret_fwd(q, k, v, rg, ret_norm_gain, tables)
    dh2, dy_lru, dy_ret, dwo, dgf, loss_acc = _out_proj_loss(y_lru, y_ret, hp, loss_target[0], wo, gain_f, tm)

    dq, dk, dv, drg, dgain = _ret_bwd(q, k, v, rg, o, rprev, dy_ret, ret_norm_gain, cos_t, sin_t, tables)
    dlx, dlg, dcw, dcb, dwr, dbr, dwi, dbi, dlam = _lru_bwd(lx, lg, hl, dy_lru, lru_saved, cw8, w_rg[0], w_ig[0],
                                                            lru_lambda, tm)
    grad_x, dmeta, dg1, dwg = _in_proj_bwd([dlx, dlg, dq, dk, dv, drg], hp, dh2, norm_gain, wg, s_len, tm)

    g_in = dwg.reshape(N_DEV, dwg.shape[1] // 2, dwg.shape[2])
    g_out = dwo.reshape(N_DEV, dwo.shape[0] // N_DEV, dwo.shape[1])
    vecs = [dg1, dcb, dbr, dbi, dlam, dgain, dgf]
    g_small = jnp.concatenate([dwr, dwi, _units(dmeta), _units(dcw[0:4])] + [_units(a) for a in vecs]
                              + [jnp.zeros((N_DEV, UNIT_ROWS - ROW_VEC - N_VEC, LANES), F32)], axis=1)
    r_in, r_out, r_small = _pair_exchange([g_in, g_out, g_small])
    q_in = _pair_sum(g_in, r_in, c_arr, 128, "grad_pair_sum_in")
    q_out = _pair_sum(g_out, r_out, c_arr, 128, "grad_pair_sum_out")
    q_small = _pair_sum(g_small, r_small, c_arr, UNIT_ROWS, "grad_pair_sum_small")
    e_in, e_out, e_small = _chip_exchange([q_in, q_out, q_small])
    f_in = _chip_sum(q_in, e_in, j_arr, 128, "grad_chip_sum_in")
    f_out = _chip_sum(q_out, e_out, j_arr, 128, "grad_chip_sum_out")
    f_small = _chip_sum(q_small, e_small, j_arr, UNIT_ROWS, "grad_chip_sum_small", loss_part=loss_acc)
    s_in, s_out, o_small = _finish_exchange(f_in, f_out, f_small)

    res_in = _adamw_big(w_in[0], f_in, s_in, m_w_in[0], v_w_in[0], c_arr, 256, "adamw_w_in")
    res_out = _adamw_big(w_out[0], f_out, s_out, m_w_out[0], v_w_out[0], c_arr, 256, "adamw_w_out")
    small_params = {
        "meta_tokens": (meta_tokens, m_meta_tokens, v_meta_tokens),
        "norm_gain": (norm_gain, m_norm_gain, v_norm_gain),
        "conv_w": (conv_w[0], m_conv_w[0], v_conv_w[0]),
        "conv_b": (conv_b, m_conv_b, v_conv_b),
        "w_rg": (w_rg[0], m_w_rg[0], v_w_rg[0]),
        "b_rg": (b_rg, m_b_rg, v_b_rg),
        "w_ig": (w_ig[0], m_w_ig[0], v_w_ig[0]),
        "b_ig": (b_ig, m_b_ig, v_b_ig),
        "lru_lambda": (lru_lambda, m_lru_lambda, v_lru_lambda),
        "ret_norm_gain": (ret_norm_gain, m_ret_norm_gain, v_ret_norm_gain),
        "final_norm_gain": (gain_f, m_final_norm_gain.reshape(1, d), v_final_norm_gain.reshape(1, d)),
    }
    res, loss = _adamw_small(j_arr, o_small, small_params)
    res["w_in"] = tuple(res_in)
    res["w_out"] = tuple(res_out)

    order = ["meta_tokens", "norm_gain", "w_in", "conv_w", "conv_b", "w_rg", "b_rg", "w_ig", "b_ig", "lru_lambda",
             "ret_norm_gain", "w_out", "final_norm_gain"]
    shapes = {"w_in": w_in.shape, "conv_w": conv_w.shape, "w_rg": w_rg.shape, "w_ig": w_ig.shape,
              "w_out": w_out.shape, "final_norm_gain": final_norm_gain.shape}
    outs = [loss, grad_x.reshape(x.shape)]
    for kind in range(4):
        for name in order:
            a = res[name][kind]
            outs.append(a.reshape(shapes[name]) if name in shapes else a)
    return tuple(outs)
```

```python
import functools

import jax
import jax.numpy as jnp
from jax import lax
from jax.experimental import pallas as pl
from jax.experimental.pallas import tpu as pltpu

F32 = jnp.float32
BF16 = jnp.bfloat16

N_META = 16
CHUNK = 128
PAD_ROWS = CHUNK - N_META
HEADS = 8
QK_DIM = 64
LANES = 128
SUBLANES = 8
LRU_C = 8.0
EPS = 1e-6
ROPE_BASE = 10000.0
ADAM_LR = 0.001
ADAM_B1 = 0.9
ADAM_B2 = 0.999
ADAM_EPS = 1e-08
ADAM_WD = 0.01
ADAM_STEP = 10
N_CHIPS = 4
N_DEV = 8
TOKEN_TILE = 384
VMEM_LIMIT = 58 * 1024 * 1024
MESH = pl.DeviceIdType.MESH

VMEM_SPEC = pl.BlockSpec(memory_space=pltpu.VMEM)
SMEM_SPEC = pl.BlockSpec(memory_space=pltpu.SMEM)
ANY_SPEC = pl.BlockSpec(memory_space=pl.ANY)

ROW_WR, ROW_WI, ROW_META, ROW_CONV, ROW_VEC, UNIT_ROWS = 0, 128, 256, 272, 276, 288
N_VEC = 7
ROW_LOSS = ROW_VEC + N_VEC
TAIL_ROWS = 24


def _dot(a, b):
    return jnp.dot(a, b, preferred_element_type=F32)


def _dot_nt(a, b):
    return lax.dot_general(a, b, (((1,), (1,)), ((), ())), preferred_element_type=F32)


def _dot_tn(a, b):
    return lax.dot_general(a, b, (((0,), (0,)), ((), ())), preferred_element_type=F32)


def _sigmoid(x):
    return 0.5 * jnp.tanh(0.5 * x) + 0.5


def _shift_down(x, prev8, s):
    rolled = pltpu.roll(x, s, 0)
    rows = lax.broadcasted_iota(jnp.int32, (SUBLANES, x.shape[1]), 0)
    top = jnp.where(rows < s, pltpu.roll(prev8, s, 0), rolled[0:SUBLANES])
    return jnp.concatenate([top, rolled[SUBLANES:]], axis=0)


def _shift_up(x, next8, s):
    n = x.shape[0]
    rolled = pltpu.roll(x, n - s, 0)
    rows = lax.broadcasted_iota(jnp.int32, (SUBLANES, x.shape[1]), 0)
    bot = jnp.where(rows >= SUBLANES - s, pltpu.roll(next8, SUBLANES - s, 0), rolled[n - SUBLANES:n])
    return jnp.concatenate([rolled[:n - SUBLANES], bot], axis=0)


def _rot_partner(t):
    w = t.shape[1]
    lane = lax.broadcasted_iota(jnp.int32, t.shape, 1)
    first = (lane % QK_DIM) < (QK_DIM // 2)
    return jnp.where(first, pltpu.roll(t, w - QK_DIM // 2, 1), pltpu.roll(t, QK_DIM // 2, 1))


def _tile_lanes(t, reps):
    return jnp.concatenate([t] * reps, axis=1)


class _Ride:
    def __init__(self, srcs, dst_shapes, n_copies, make):
        self.srcs, self.dst_shapes, self.n_copies, self.make = list(srcs), list(dst_shapes), n_copies, make


def _join_rides(a, b):
    def make(src, dst, send_sems, recv_sems, base):
        na, da = len(a.srcs), len(a.dst_shapes)
        return (a.make(src[:na], dst[:da], send_sems, recv_sems, base)
                + b.make(src[na:], dst[da:], send_sems, recv_sems, base + a.n_copies))

    return _Ride(a.srcs + b.srcs, a.dst_shapes + b.dst_shapes, a.n_copies + b.n_copies, make)


def _position():
    x, y, c = lax.axis_index("x"), lax.axis_index("y"), lax.axis_index("c")
    return x, y, c, [(1 - x, y), (x, 1 - y), (1 - x, 1 - y)]


def _remote(src, dst, send_sems, recv_sems, k, to):
    return pltpu.make_async_remote_copy(src_ref=src, dst_ref=dst, send_sem=send_sems.at[k], recv_sem=recv_sems.at[k],
                                        device_id=to, device_id_type=MESH)


def _pair_ride(bufs):
    def make(src, dst, send_sems, recv_sems, base):
        x, y, c, _ = _position()
        return [_remote(src[b].at[2 * jj + 1 - c], dst[b].at[jj], send_sems, recv_sems, base + b * N_CHIPS + jj,
                        (x, y, 1 - c)) for b in range(len(bufs)) for jj in range(N_CHIPS)]

    shapes = [jax.ShapeDtypeStruct((N_CHIPS,) + b.shape[1:], b.dtype) for b in bufs]
    return _Ride(bufs, shapes, N_CHIPS * len(bufs), make)


def _chip_ride(bufs):
    def make(src, dst, send_sems, recv_sems, base):
        x, y, c, chips = _position()
        return [_remote(src[b].at[2 * cx + cy], dst[b].at[2 * x + y], send_sems, recv_sems, base + b * 3 + k,
                        (cx, cy, c)) for b in range(len(bufs)) for k, (cx, cy) in enumerate(chips)]

    shapes = [jax.ShapeDtypeStruct(b.shape, b.dtype) for b in bufs]
    return _Ride(bufs, shapes, 3 * len(bufs), make)


def _sibling_ride(bufs):
    def make(src, dst, send_sems, recv_sems, base):
        x, y, c, _ = _position()
        return [_remote(src[b], dst[b], send_sems, recv_sems, base + b, (x, y, 1 - c)) for b in range(len(bufs))]

    shapes = [jax.ShapeDtypeStruct(b.shape, b.dtype) for b in bufs]
    return _Ride(bufs, shapes, len(bufs), make)


def _exchange_call(ride, name):
    n_src, n_dst = len(ride.srcs), len(ride.dst_shapes)

    def body(*refs):
        copies = ride.make(refs[:n_src], refs[n_src:n_src + n_dst], refs[-2], refs[-1], 0)
        for cp in copies:
            cp.start()
        for cp in copies:
            cp.wait()

    return pl.pallas_call(
        body,
        name=name,
        in_specs=[ANY_SPEC] * n_src,
        out_specs=[ANY_SPEC] * n_dst,
        out_shape=ride.dst_shapes,
        scratch_shapes=[pltpu.SemaphoreType.DMA((ride.n_copies,)), pltpu.SemaphoreType.DMA((ride.n_copies,))],
    )(*ride.srcs)


def _hosted_call(body, ride, n_steps, *, name, in_specs, out_specs, out_shape, scratch_shapes, args):
    params = pltpu.CompilerParams(dimension_semantics=("arbitrary",), vmem_limit_bytes=VMEM_LIMIT)
    if ride is None:
        res = pl.pallas_call(body, name=name, grid=(n_steps,), in_specs=list(in_specs), out_specs=list(out_specs),
                             out_shape=list(out_shape), scratch_shapes=list(scratch_shapes),
                             compiler_params=params)(*args)
        return list(res), []
    sizes = [len(in_specs), len(ride.srcs), len(out_specs), len(ride.dst_shapes), len(scratch_shapes), 2]

    def hosted(*refs):
        groups, pos = [], 0
        for n in sizes:
            groups.append(refs[pos:pos + n])
            pos += n
        ins, rin, outs, rout, scr, (send_sems, recv_sems) = groups
        i = pl.program_id(0)

        @pl.when(i == 0)
        def _():
            for cp in ride.make(rin, rout, send_sems, recv_sems, 0):
                cp.start()

        body(*ins, *outs, *scr)

        @pl.when(i == n_steps - 1)
        def _():
            for cp in ride.make(rin, rout, send_sems, recv_sems, 0):
                cp.wait()

    n_out = len(out_specs)
    res = pl.pallas_call(
        hosted,
        name=name,
        grid=(n_steps,),
        in_specs=list(in_specs) + [ANY_SPEC] * len(ride.srcs),
        out_specs=list(out_specs) + [ANY_SPEC] * len(ride.dst_shapes),
        out_shape=list(out_shape) + ride.dst_shapes,
        scratch_shapes=list(scratch_shapes) + [pltpu.SemaphoreType.DMA((ride.n_copies,)),
                                               pltpu.SemaphoreType.DMA((ride.n_copies,))],
        compiler_params=params,
    )(*args, *ride.srcs)
    return list(res[:n_out]), list(res[n_out:])


def _gather_weights(w_in, small):
    r_in, c_in = w_in.shape
    h_in = r_in // 2

    def body(win_ref, small_ref, wg_ref, sg_ref, send_sems, recv_sems):
        x, y, c, chips = _position()
        j = 2 * x + y
        sibling = (x, y, 1 - c)

        wg_ref[j] = win_ref[...].astype(BF16)
        sg_ref[j] = small_ref[...]

        def half_in(jj, cc):
            return wg_ref.at[jj, pl.ds(cc * h_in, h_in), :]

        def copy(k, ref, to):
            return _remote(ref, ref, send_sems, recv_sems, k, to)

        first = []
        for k, (cx, cy) in enumerate(chips):
            first.append(copy(k, half_in(j, c), (cx, cy, c)))
            first.append(copy(3 + k, sg_ref.at[j], (cx, cy, c)))
        for cp in first:
            cp.start()
        passed = []
        for k, (cx, cy) in enumerate(chips):
            jk = 2 * cx + cy
            copy(k, half_in(jk, c), sibling).wait_recv()
            fwd = copy(6 + k, half_in(jk, c), sibling)
            fwd.start()
            passed.append(fwd)
        for k, (cx, cy) in enumerate(chips):
            jk = 2 * cx + cy
            copy(6 + k, half_in(jk, 1 - c), sibling).wait_recv()
            copy(3 + k, sg_ref.at[jk], sibling).wait_recv()
        for cp in first + passed:
            cp.wait_send()

    return pl.pallas_call(
        body,
        name="gather_weights",
        out_shape=(jax.ShapeDtypeStruct((N_CHIPS, r_in, c_in), BF16),
                   jax.ShapeDtypeStruct((N_CHIPS,) + small.shape, F32)),
        in_specs=[VMEM_SPEC, VMEM_SPEC],
        out_specs=(VMEM_SPEC, VMEM_SPEC),
        scratch_shapes=[pltpu.SemaphoreType.DMA((9,)), pltpu.SemaphoreType.DMA((9,))],
        compiler_params=pltpu.CompilerParams(vmem_limit_bytes=VMEM_LIMIT),
    )(w_in, small)


def _proj_segments(d_lru, d_qk, d_ret, chunk_w):
    widths = [d_lru, d_lru, d_qk, d_qk, d_ret, d_ret]
    segs, col = [], 0
    for w in widths:
        parts, off = [], 0
        while off < w:
            jj, inner = divmod(col + off, chunk_w)
            take = min(w - off, chunk_w - inner)
            parts.append((jj, inner, off, take))
            off += take
        segs.append(parts)
        col += w
    return segs


def _in_proj(x2, meta, gain, wg, cos_t, sin_t, w_out, tm, d_lru, d_qk, d_ret):
    s_len, d = x2.shape
    tp = s_len + CHUNK
    nt, nb = tp // tm, tm // CHUNK
    segs = _proj_segments(d_lru, d_qk, d_ret, wg.shape[2])
    widths = [d_lru, d_lru, d_qk, d_qk, d_ret, d_ret]
    r_out, c_out = w_out.shape
    h_out = r_out // 2
    fwd_step = min(4, nt - 1)

    def gather_w_out(i, wout_ref, wo_ref, wob, send_sems, recv_sems, local_sem):
        x, y, c, chips = _position()
        j = 2 * x + y
        sibling = (x, y, 1 - c)

        def half(jj, cc):
            return wo_ref.at[jj, pl.ds(cc * h_out, h_out), :]

        local = pltpu.make_async_copy(wob, wo_ref.at[j], local_sem)
        first = [_remote(wob.at[pl.ds(c * h_out, h_out), :], half(j, c), send_sems, recv_sems, k, (cx, cy, c))
                 for k, (cx, cy) in enumerate(chips)]
        passed = [_remote(half(2 * cx + cy, c), half(2 * cx + cy, c), send_sems, recv_sems, 3 + k, sibling)
                  for k, (cx, cy) in enumerate(chips)]

        @pl.when(i == 0)
        def _():
            wob[...] = wout_ref[...].astype(BF16)
            local.start()
            for cp in first:
                cp.start()

        @pl.when(i == fwd_step)
        def _():
            for k, (cx, cy) in enumerate(chips):
                _remote(half(2 * cx + cy, c), half(2 * cx + cy, c), send_sems, recv_sems, k, sibling).wait_recv()
                passed[k].start()

        @pl.when(i == nt - 1)
        def _():
            for k, (cx, cy) in enumerate(chips):
                jk = 2 * cx + cy
                _remote(half(jk, 1 - c), half(jk, 1 - c), send_sems, recv_sems, 3 + k, sibling).wait_recv()
            for cp in first + passed:
                cp.wait_send()
            local.wait()

    def body(*refs):
        xb = refs[:nb]
        meta_ref, g_ref, w_ref, cos_ref, sin_ref, wout_ref = refs[nb:nb + 6]
        hp_ref = refs[nb + 6]
        outs = refs[nb + 7:nb + 13]
        wo_ref, wob, send_sems, recv_sems, local_sem = refs[nb + 13:]
        i = pl.program_id(0)
        gather_w_out(i, wout_ref, wo_ref, wob, send_sems, recv_sems, local_sem)
        blocks = [r[...] for r in xb]
        head = jnp.concatenate([jnp.zeros((PAD_ROWS, d), F32), meta_ref[...]], axis=0)
        blocks[0] = jnp.where(i == 0, head, blocks[0])
        h = jnp.concatenate(blocks, axis=0)
        hp_ref[...] = h
        rinv = lax.rsqrt(jnp.mean(h * h, axis=-1, keepdims=True) + EPS)
        u = ((h * rinv) * g_ref[...]).astype(BF16)
        for out_ref, parts in zip(outs, segs):
            for jj, inner, off, take in parts:
                out_ref[:, off:off + take] = _dot(u, w_ref[jj, :, inner:inner + take])
        cos = _tile_lanes(cos_ref[...], d_qk // LANES)
        sin = _tile_lanes(sin_ref[...], d_qk // LANES)
        q = outs[2][...]
        outs[2][...] = q * cos + _rot_partner(q) * sin
        k = outs[3][...]
        outs[3][...] = (k * cos + _rot_partner(k) * sin) * (QK_DIM ** -0.5)

    x_specs = [pl.BlockSpec((CHUNK, d), functools.partial(lambda i, b: (jnp.maximum(i * nb + b - 1, 0), 0), b=b))
               for b in range(nb)]
    tile = lambda w: pl.BlockSpec((tm, w), lambda i: (i, 0))
    return pl.pallas_call(
        body,
        name="in_proj",
        grid=(nt,),
        in_specs=x_specs + [pl.BlockSpec(meta.shape, lambda i: (0, 0)),
                            pl.BlockSpec(gain.shape, lambda i: (0, 0)),
                            pl.BlockSpec(wg.shape, lambda i: (0, 0, 0)),
                            tile(LANES), tile(LANES),
                            pl.BlockSpec(w_out.shape, lambda i: (0, 0))],
        out_specs=[tile(d)] + [tile(w) for w in widths] + [ANY_SPEC],
        out_shape=[jax.ShapeDtypeStruct((tp, d), F32)] + [jax.ShapeDtypeStruct((tp, w), F32) for w in widths]
                  + [jax.ShapeDtypeStruct((N_CHIPS, r_out, c_out), BF16)],
        scratch_shapes=[pltpu.VMEM((r_out, c_out), BF16), pltpu.SemaphoreType.DMA((6,)),
                        pltpu.SemaphoreType.DMA((6,)), pltpu.SemaphoreType.DMA],
        compiler_params=pltpu.CompilerParams(dimension_semantics=("arbitrary",), vmem_limit_bytes=VMEM_LIMIT),
    )(*([x2] * nb), meta, gain, wg, cos_t, sin_t, w_out)


def _softplus_neg(lam):
    z = -lam
    e = jnp.exp(-jnp.abs(z))
    e1 = 1.0 + e
    log1p_e = jnp.where(e1 == 1.0, e, jnp.log(e1) * (e / (e1 - 1.0)))
    return jnp.maximum(z, 0.0) + log1p_e


def _lru_fwd(lx, lg, cw, cb, wr, br, wi, bi, lam, tm):
    tp, w = lx.shape
    nt = tp // tm
    per8 = tm // SUBLANES
    n_heads = wr.shape[0]

    def body(lx_ref, lxp_ref, lg_ref, cw_ref, cb_ref, wr_ref, br_ref, wi_ref, bi_ref, lam_ref,
             hl_ref, y_ref, xc_ref, r_ref, ig_ref, a_ref, beta_ref, w4_ref, u_s, carry):
        i = pl.program_id(0)

        @pl.when(i == 0)
        def _():
            carry[...] = jnp.zeros_like(carry)

        lxv = lx_ref[...]
        prev8 = jnp.where(i == 0, 0.0, lxp_ref[...])
        xc = cb_ref[...] + _shift_down(lxv, prev8, 3) * cw_ref[0:1, :]
        xc = xc + _shift_down(lxv, prev8, 2) * cw_ref[1:2, :]
        xc = xc + _shift_down(lxv, prev8, 1) * cw_ref[2:3, :]
        xc = xc + lxv * cw_ref[3:4, :]
        xc_ref[...] = xc
        pre_r, pre_i = [], []
        for hd in range(n_heads):
            xh = xc[:, hd * LANES:(hd + 1) * LANES].astype(BF16)
            pre_r.append(_dot(xh, wr_ref[hd].astype(BF16)))
            pre_i.append(_dot(xh, wi_ref[hd].astype(BF16)))
        r = _sigmoid(jnp.concatenate(pre_r, axis=1) + br_ref[...])
        ig = _sigmoid(jnp.concatenate(pre_i, axis=1) + bi_ref[...])
        r_ref[...] = r
        ig_ref[...] = ig
        log_a = (-LRU_C * r) * _softplus_neg(lam_ref[...])
        a = jnp.exp(log_a)
        a_ref[...] = a
        zz = -2.0 * log_a
        series = zz * (1.0 - zz * (0.5 - zz * (1.0 / 6.0)))
        a2 = a * a
        beta2 = jnp.maximum(jnp.where(zz < 0.015625, series, 1.0 - a2), 1e-37)
        rsb = lax.rsqrt(beta2)
        beta = beta2 * rsb
        beta_ref[...] = beta
        w4_ref[...] = a2 * rsb
        row = lax.broadcasted_iota(jnp.int32, (tm, 1), 0) + i * tm
        u_s[...] = jnp.where(row >= PAD_ROWS, beta * ig * xc, 0.0)

        def step(t, h):
            h = a_ref[pl.ds(t, 1), :] * h + u_s[pl.ds(t, 1), :]
            hl_ref[pl.ds(t, 1), :] = h
            return h

        carry[0:1, :] = lax.fori_loop(0, tm, step, carry[0:1, :], unroll=8)
        g = lg_ref[...]
        y_ref[...] = (hl_ref[...] * (g * _sigmoid(g))).astype(BF16)

    tile = pl.BlockSpec((tm, w), lambda i: (i, 0))
    prev = pl.BlockSpec((SUBLANES, w), lambda i: (jnp.maximum(i * per8 - 1, 0), 0))
    vec = pl.BlockSpec((1, w), lambda i: (0, 0))
    mat = pl.BlockSpec(wr.shape, lambda i: (0, 0, 0))
    f32_out = jax.ShapeDtypeStruct((tp, w), F32)
    return pl.pallas_call(
        body,
        name="lru_fwd",
        grid=(nt,),
        in_specs=[tile, prev, tile, pl.BlockSpec(cw.shape, lambda i: (0, 0)), vec, mat, vec, mat, vec, vec],
        out_specs=[tile] * 8,
        out_shape=[f32_out, jax.ShapeDtypeStruct((tp, w), BF16)] + [f32_out] * 6,
        scratch_shapes=[pltpu.VMEM((tm, w), F32), pltpu.VMEM((SUBLANES, w), F32)],
        compiler_params=pltpu.CompilerParams(dimension_semantics=("arbitrary",), vmem_limit_bytes=VMEM_LIMIT),
    )(lx, lx, lg, cw, cb, wr, br, wi, bi, lam)


def _ret_tables():
    log_g = jnp.log1p(-jnp.exp2(-5.0 - jnp.arange(HEADS, dtype=F32)))
    idx = jnp.arange(CHUNK, dtype=F32)
    diff = idx[:, None] - idx[None, :]
    dmask = jnp.where(diff[None] >= 0.0, jnp.exp(jnp.maximum(diff, 0.0)[None] * log_g[:, None, None]), 0.0)
    kdec = jnp.repeat(jnp.exp((CHUNK - 1.0 - idx)[:, None] * log_g[None, :]), QK_DIM, axis=1)
    qdec = jnp.repeat(jnp.exp((idx + 1.0)[:, None] * log_g[None, :]), QK_DIM, axis=1)
    g_chunk = jnp.exp(CHUNK * log_g)
    g_rows = jnp.repeat(g_chunk, QK_DIM).reshape(HEADS // 2, 2 * QK_DIM, 1)
    g_state = jnp.broadcast_to(g_rows, (HEADS // 2, 2 * QK_DIM, 2 * LANES))
    r_head = jnp.arange(2 * QK_DIM)[:, None] // QK_DIM
    c_head = jnp.arange(2 * LANES)[None, :] // LANES
    block_diag = (r_head == c_head).astype(F32)
    return dmask, qdec, kdec, g_state, block_diag


def _head_norm(o_h):
    mu = jnp.mean(o_h, axis=-1, keepdims=True)
    oc = o_h - mu
    var = jnp.mean(oc * oc, axis=-1, keepdims=True)
    rstd = lax.rsqrt(var + EPS)
    return oc * rstd, rstd


def _ret_fwd(q, k, v, rg, gain, tables):
    tp, d_qk = q.shape
    d_ret = v.shape[1]
    n_ch = tp // CHUNK
    n_pairs = HEADS // 2
    dmask, qdec, kdec, g_state, block_diag = tables

    def body(q_ref, k_ref, v_ref, rg_ref, gain_ref, dm_ref, qd_ref, kd_ref, gs_ref, bd_ref,
             o_ref, y_ref, rp_ref, state):
        n = pl.program_id(0)

        @pl.when(n == 0)
        def _():
            state[...] = jnp.zeros_like(state)

        lane = lax.broadcasted_iota(jnp.int32, (CHUNK, LANES), 1)
        for p in range(n_pairs):
            qs = slice(p * LANES, (p + 1) * LANES)
            vs = slice(p * 2 * LANES, (p + 1) * 2 * LANES)
            qp, kp = q_ref[:, qs], k_ref[:, qs]
            vb = v_ref[:, vs].astype(BF16)
            kb = kp.astype(BF16)
            qd = (qp * qd_ref[:, qs]).astype(BF16)
            kd = (kp * kd_ref[:, qs]).astype(BF16)
            st = state[p]
            st_b = st.astype(BF16)
            rp_ref[0, p] = st_b
            cross = _dot(qd, st_b)
            for e in range(2):
                hd = 2 * p + e
                hs = slice(hd * LANES, (hd + 1) * LANES)
                qm = jnp.where((lane // QK_DIM) == e, qp, 0.0).astype(BF16)
                s = _dot_nt(qm, kb) * dm_ref[hd]
                o_h = _dot(s.astype(BF16), vb[:, e * LANES:(e + 1) * LANES]) + cross[:, e * LANES:(e + 1) * LANES]
                o_ref[:, hs] = o_h
                xhat, _ = _head_norm(o_h)
                g = rg_ref[:, hs]
                y_ref[:, hs] = ((xhat * gain_ref[:, hs]) * (g * _sigmoid(g))).astype(BF16)
            state[p] = gs_ref[p] * st + bd_ref[...] * _dot_tn(kd, vb)

    ch = lambda w: pl.BlockSpec((CHUNK, w), lambda n: (n, 0))
    const2 = lambda a: pl.BlockSpec(a.shape, lambda n: (0, 0))
    const3 = lambda a: pl.BlockSpec(a.shape, lambda n: (0, 0, 0))
    return pl.pallas_call(
        body,
        name="ret_fwd",
        grid=(n_ch,),
        in_specs=[ch(d_qk), ch(d_qk), ch(d_ret), ch(d_ret), const2(gain), const3(dmask), const2(qdec), const2(kdec),
                  const3(g_state), const2(block_diag)],
        out_specs=[ch(d_ret), ch(d_ret),
                   pl.BlockSpec((1, n_pairs, 2 * QK_DIM, 2 * LANES), lambda n: (n, 0, 0, 0))],
        out_shape=[jax.ShapeDtypeStruct((tp, d_ret), F32), jax.ShapeDtypeStruct((tp, d_ret), BF16),
                   jax.ShapeDtypeStruct((n_ch, n_pairs, 2 * QK_DIM, 2 * LANES), BF16)],
        scratch_shapes=[pltpu.VMEM((n_pairs, 2 * QK_DIM, 2 * LANES), F32)],
        compiler_params=pltpu.CompilerParams(dimension_semantics=("arbitrary",), vmem_limit_bytes=VMEM_LIMIT),
    )(q, k, v, rg, gain, dmask, qdec, kdec, g_state, block_diag)


def _out_proj_loss(y_lru, y_ret, hp, tgt, wo, gain_f, tm):
    tp, d = hp.shape
    w_lru = y_lru.shape[1]
    w_mix = wo.shape[0]
    nt, nb = tp // tm, tm // CHUNK

    def body(*refs):
        yl_ref, yr_ref, hp_ref = refs[:3]
        tb = refs[3:3 + nb]
        wo_ref, gf_ref = refs[3 + nb:5 + nb]
        dh2_ref, dyl_ref, dyr_ref, dwo_ref, dgf_ref, loss_ref = refs[5 + nb:]
        i = pl.program_id(0)

        @pl.when(i == 0)
        def _():
            dwo_ref[...] = jnp.zeros_like(dwo_ref)
            dgf_ref[...] = jnp.zeros_like(dgf_ref)
            loss_ref[...] = jnp.zeros_like(loss_ref)

        yl, yr = yl_ref[...], yr_ref[...]
        h2 = hp_ref[...] + _dot(yl, wo_ref[0:w_lru, :]) + _dot(yr, wo_ref[w_lru:w_mix, :])
        rinv = lax.rsqrt(jnp.mean(h2 * h2, axis=-1, keepdims=True) + EPS)
        nrm = h2 * rinv
        gf = gf_ref[...]
        tgt_v = jnp.concatenate([r[...] for r in tb], axis=0)
        row = lax.broadcasted_iota(jnp.int32, (tm, 1), 0) + i * tm
        err = jnp.where(row >= CHUNK, nrm * gf - tgt_v, 0.0)
        loss_ref[...] += 0.5 * jnp.sum(jnp.mean(err * err, axis=-1, keepdims=True))
        dout = err * (1.0 / d)
        dgf_ref[...] += jnp.sum(dout * nrm, axis=0, keepdims=True)
        dn = dout * gf
        dh2 = rinv * (dn - nrm * jnp.mean(dn * nrm, axis=-1, keepdims=True))
        dh2_ref[...] = dh2
        dh2b = dh2.astype(BF16)
        dyl_ref[...] = _dot_nt(dh2b, wo_ref[0:w_lru, :])
        dyr_ref[...] = _dot_nt(dh2b, wo_ref[w_lru:w_mix, :])
        dwo_ref[0:w_lru, :] += _dot_tn(yl, dh2b)
        dwo_ref[w_lru:w_mix, :] += _dot_tn(yr, dh2b)

    tile = lambda w: pl.BlockSpec((tm, w), lambda i: (i, 0))
    t_specs = [pl.BlockSpec((CHUNK, d), functools.partial(lambda i, b: (jnp.maximum(i * nb + b - 1, 0), 0), b=b))
               for b in range(nb)]
    return pl.pallas_call(
        body,
        name="out_proj_loss",
        grid=(nt,),
        in_specs=[tile(w_lru), tile(w_mix - w_lru), tile(d)] + t_specs +
                 [pl.BlockSpec(wo.shape, lambda i: (0, 0)), pl.BlockSpec(gain_f.shape, lambda i: (0, 0))],
        out_specs=[tile(d), tile(w_lru), tile(w_mix - w_lru), pl.BlockSpec(wo.shape, lambda i: (0, 0)),
                   pl.BlockSpec((1, d), lambda i: (0, 0)), pl.BlockSpec((SUBLANES, LANES), lambda i: (0, 0))],
        out_shape=[jax.ShapeDtypeStruct((tp, d), F32), jax.ShapeDtypeStruct((tp, w_lru), F32),
                   jax.ShapeDtypeStruct((tp, w_mix - w_lru), F32), jax.ShapeDtypeStruct(wo.shape, F32),
                   jax.ShapeDtypeStruct((1, d), F32), jax.ShapeDtypeStruct((SUBLANES, LANES), F32)],
        compiler_params=pltpu.CompilerParams(dimension_semantics=("arbitrary",), vmem_limit_bytes=VMEM_LIMIT),
    )(y_lru, y_ret, hp, *([tgt] * nb), wo, gain_f)


def _ret_bwd(q, k, v, rg, o, rprev, dy, gain, cos_t, sin_t, tables, ride=None):
    tp, d_qk = q.shape
    d_ret = v.shape[1]
    n_ch = tp // CHUNK
    n_pairs = HEADS // 2
    dmask, qdec, kdec, g_state, block_diag = tables

    def body(q_ref, k_ref, v_ref, rg_ref, o_ref, rp_ref, dy_ref, gain_ref, cos_ref, sin_ref,
             dm_ref, qd_ref, kd_ref, gs_ref, bd_ref, dq_ref, dk_ref, dv_ref, drg_ref, dgain_ref, dstate):
        n = pl.program_id(0)

        @pl.when(n == 0)
        def _():
            dstate[...] = jnp.zeros_like(dstate)
            dgain_ref[...] = jnp.zeros_like(dgain_ref)

        lane = lax.broadcasted_iota(jnp.int32, (CHUNK, LANES), 1)
        dq_parts, dk_parts = [], []
        for p in range(n_pairs):
            qs = slice(p * LANES, (p + 1) * LANES)
            vs = slice(p * 2 * LANES, (p + 1) * 2 * LANES)
            do_parts = []
            for e in range(2):
                hd = 2 * p + e
                hs = slice(hd * LANES, (hd + 1) * LANES)
                xhat, rstd = _head_norm(o_ref[:, hs])
                g = rg_ref[:, hs]
                sg = _sigmoid(g)
                dyh = dy_ref[:, hs]
                gn = gain_ref[:, hs]
                d_on = dyh * (g * sg)
                drg_ref[:, hs] = (dyh * (xhat * gn) * (sg * (1.0 + g * (1.0 - sg)))).astype(BF16)
                dgain_ref[:, hs] += jnp.sum(d_on * xhat, axis=0, keepdims=True)
                dxh = d_on * gn
                do_parts.append(rstd * (dxh - jnp.mean(dxh, axis=-1, keepdims=True)
                                        - xhat * jnp.mean(dxh * xhat, axis=-1, keepdims=True)))
            do_b = jnp.concatenate(do_parts, axis=1).astype(BF16)
            qp, kp = q_ref[:, qs], k_ref[:, qs]
            vb = v_ref[:, vs].astype(BF16)
            kb = kp.astype(BF16)
            qd = (qp * qd_ref[:, qs]).astype(BF16)
            kd = (kp * kd_ref[:, qs]).astype(BF16)
            dst = dstate[p]
            dst_b = dst.astype(BF16)
            dqp = _dot_nt(do_b, rp_ref[0, p]) * qd_ref[:, qs]
            dkp = _dot_nt(vb, dst_b) * kd_ref[:, qs]
            dvp = _dot(kd, dst_b)
            dv_parts = []
            for e in range(2):
                hd = 2 * p + e
                es = slice(e * LANES, (e + 1) * LANES)
                mine = (lane // QK_DIM) == e
                qm = jnp.where(mine, qp, 0.0).astype(BF16)
                km = jnp.where(mine, kp, 0.0).astype(BF16)
                dm = dm_ref[hd]
                s = (_dot_nt(qm, kb) * dm).astype(BF16)
                ds = (_dot_nt(do_b[:, es], vb[:, es]) * dm).astype(BF16)
                dv_parts.append(dvp[:, es] + _dot_tn(s, do_b[:, es]))
                dqp = dqp + _dot(ds, km)
                dkp = dkp + _dot_tn(ds, qm)
            dv_ref[:, vs] = jnp.concatenate(dv_parts, axis=1).astype(BF16)
            dstate[p] = gs_ref[p] * dst + bd_ref[...] * _dot_tn(qd, do_b)
            dq_parts.append(dqp)
            dk_parts.append(dkp)
        cos = _tile_lanes(cos_ref[...], d_qk // LANES)
        sin = _tile_lanes(sin_ref[...], d_qk // LANES)
        dq = jnp.concatenate(dq_parts, axis=1)
        dk = jnp.concatenate(dk_parts, axis=1) * (QK_DIM ** -0.5)
        dq_ref[...] = (dq * cos + _rot_partner(dq * sin)).astype(BF16)
        dk_ref[...] = (dk * cos + _rot_partner(dk * sin)).astype(BF16)

    last = n_ch - 1
    ch = lambda w: pl.BlockSpec((CHUNK, w), lambda n: (last - n, 0))
    const2 = lambda a: pl.BlockSpec(a.shape, lambda n: (0, 0))
    const3 = lambda a: pl.BlockSpec(a.shape, lambda n: (0, 0, 0))
    return _hosted_call(
        body, ride, n_ch,
        name="ret_bwd",
        in_specs=[ch(d_qk), ch(d_qk), ch(d_ret), ch(d_ret), ch(d_ret),
                  pl.BlockSpec((1, n_pairs, 2 * QK_DIM, 2 * LANES), lambda n: (last - n, 0, 0, 0)),
                  ch(d_ret), const2(gain), ch(LANES), ch(LANES),
                  const3(dmask), const2(qdec), const2(kdec), const3(g_state), const2(block_diag)],
        out_specs=[ch(d_qk), ch(d_qk), ch(d_ret), ch(d_ret), pl.BlockSpec((1, d_ret), lambda n: (0, 0))],
        out_shape=[jax.ShapeDtypeStruct((tp, d_qk), BF16), jax.ShapeDtypeStruct((tp, d_qk), BF16),
                   jax.ShapeDtypeStruct((tp, d_ret), BF16), jax.ShapeDtypeStruct((tp, d_ret), BF16),
                   jax.ShapeDtypeStruct((1, d_ret), F32)],
        scratch_shapes=[pltpu.VMEM((n_pairs, 2 * QK_DIM, 2 * LANES), F32)],
        args=(q, k, v, rg, o, rprev, dy, gain, cos_t, sin_t, dmask, qdec, kdec, g_state, block_diag),
    )


def _lru_bwd(lx, lg, hl, dy, saved, cw, wr, wi, lam, tm, ride=None):
    tp, w = lx.shape
    nt = tp // tm
    per8 = tm // SUBLANES
    n_heads = wr.shape[0]

    def body(lx_ref, lg_ref, hl_ref, hlp_ref, dy_ref, xc_ref, r_ref, ig_ref, a_ref, beta_ref, w4_ref,
             cw_ref, wr_ref, wi_ref, lam_ref,
             dlx_ref, dlg_ref, dcw_ref, dcb_ref, dwr_ref, dbr_ref, dwi_ref, dbi_ref, dlam_ref,
             g_s, dh_s, carry, dxc_next):
        i = pl.program_id(0)
        first_tile = i == nt - 1

        @pl.when(i == 0)
        def _():
            carry[...] = jnp.zeros_like(carry)
            dxc_next[...] = jnp.zeros_like(dxc_next)
            for r in (dcw_ref, dcb_ref, dwr_ref, dbr_ref, dwi_ref, dbi_ref, dlam_ref):
                r[...] = jnp.zeros_like(r)

        lxv = lx_ref[...]
        a, beta, r, ig, xc = a_ref[...], beta_ref[...], r_ref[...], ig_ref[...], xc_ref[...]
        g = lg_ref[...]
        sg = _sigmoid(g)
        dyv = dy_ref[...]
        hlv = hl_ref[...]
        dlg_ref[...] = (dyv * hlv * (sg * (1.0 + g * (1.0 - sg)))).astype(BF16)
        g_s[...] = dyv * (g * sg)

        def step(s, cn):
            t = tm - 1 - s
            dh = g_s[pl.ds(t, 1), :] + cn
            dh_s[pl.ds(t, 1), :] = dh
            return a_ref[pl.ds(t, 1), :] * dh

        carry[0:1, :] = lax.fori_loop(0, tm, step, carry[0:1, :], unroll=8)
        dh = dh_s[...]
        hprev = _shift_down(hlv, jnp.where(first_tile, 0.0, hlp_ref[...]), 1)
        row = lax.broadcasted_iota(jnp.int32, (tm, 1), 0) + (nt - 1 - i) * tm
        du = jnp.where(row >= PAD_ROWS, dh, 0.0)
        da = dh * hprev
        dbeta = du * ig * xc
        d_ig = du * beta * xc
        dxc = du * beta * ig
        dloga = da * a - dbeta * w4_ref[...]
        lam_v = lam_ref[...]
        dlam_ref[...] += jnp.sum(dloga * r, axis=0, keepdims=True) * (LRU_C * _sigmoid(-lam_v))
        dpr = (dloga * (-LRU_C * _softplus_neg(lam_v))) * r * (1.0 - r)
        dpi = d_ig * ig * (1.0 - ig)
        dbr_ref[...] += jnp.sum(dpr, axis=0, keepdims=True)
        dbi_ref[...] += jnp.sum(dpi, axis=0, keepdims=True)
        dxc_parts = []
        for hd in range(n_heads):
            hs = slice(hd * LANES, (hd + 1) * LANES)
            xh = xc[:, hs].astype(BF16)
            dprh = dpr[:, hs].astype(BF16)
            dpih = dpi[:, hs].astype(BF16)
            dwr_ref[hd] += _dot_tn(xh, dprh)
            dwi_ref[hd] += _dot_tn(xh, dpih)
            dxc_parts.append(_dot_nt(dprh, wr_ref[hd].astype(BF16)) + _dot_nt(dpih, wi_ref[hd].astype(BF16)))
        dxc = dxc + jnp.concatenate(dxc_parts, axis=1)
        nxt = dxc_next[...]
        up1, up2, up3 = _shift_up(dxc, nxt, 1), _shift_up(dxc, nxt, 2), _shift_up(dxc, nxt, 3)
        dlx = dxc * cw_ref[3:4, :]
        dlx = dlx + up1 * cw_ref[2:3, :]
        dlx = dlx + up2 * cw_ref[1:2, :]
        dlx = dlx + up3 * cw_ref[0:1, :]
        dlx_ref[...] = dlx.astype(BF16)
        dxc_next[...] = dxc[0:SUBLANES]
        dcb_ref[...] += jnp.sum(dxc, axis=0, keepdims=True)
        dcw_ref[0:1, :] += jnp.sum(up3 * lxv, axis=0, keepdims=True)
        dcw_ref[1:2, :] += jnp.sum(up2 * lxv, axis=0, keepdims=True)
        dcw_ref[2:3, :] += jnp.sum(up1 * lxv, axis=0, keepdims=True)
        dcw_ref[3:4, :] += jnp.sum(dxc * lxv, axis=0, keepdims=True)

    last = nt - 1
    tile = pl.BlockSpec((tm, w), lambda i: (last - i, 0))
    prev = pl.BlockSpec((SUBLANES, w), lambda i: (jnp.maximum((last - i) * per8 - 1, 0), 0))
    vec = pl.BlockSpec((1, w), lambda i: (0, 0))
    mat = pl.BlockSpec(wr.shape, lambda i: (0, 0, 0))
    cwb = pl.BlockSpec(cw.shape, lambda i: (0, 0))
    return _hosted_call(
        body, ride, nt,
        name="lru_bwd",
        in_specs=[tile, tile, tile, prev, tile] + [tile] * 6 + [cwb, mat, mat, vec],
        out_specs=[tile, tile, cwb, vec, mat, vec, mat, vec, vec],
        out_shape=[jax.ShapeDtypeStruct((tp, w), BF16), jax.ShapeDtypeStruct((tp, w), BF16),
                   jax.ShapeDtypeStruct(cw.shape, F32), jax.ShapeDtypeStruct((1, w), F32),
                   jax.ShapeDtypeStruct(wr.shape, F32), jax.ShapeDtypeStruct((1, w), F32),
                   jax.ShapeDtypeStruct(wr.shape, F32), jax.ShapeDtypeStruct((1, w), F32),
                   jax.ShapeDtypeStruct((1, w), F32)],
        scratch_shapes=[pltpu.VMEM((tm, w), F32), pltpu.VMEM((tm, w), F32),
                        pltpu.VMEM((SUBLANES, w), F32), pltpu.VMEM((SUBLANES, w), F32)],
        args=(lx, lg, hl, hl, dy, *saved, cw, wr, wi, lam),
    )


def _in_proj_dw(dparts, hp, gain, wg_shape, tm, ride=None):
    tp, d = hp.shape
    nt = tp // tm
    widths = [p.shape[1] for p in dparts]
    segs = _proj_segments(widths[0], widths[2], widths[4], wg_shape[2])

    def body(*refs):
        dp = refs[:6]
        hp_ref, g_ref, dwg_ref, acc, sem = refs[6:]
        i = pl.program_id(0)

        @pl.when(i == 0)
        def _():
            acc[...] = jnp.zeros_like(acc)

        h = hp_ref[...]
        rinv = lax.rsqrt(jnp.mean(h * h, axis=-1, keepdims=True) + EPS)
        u = ((h * rinv) * g_ref[...]).astype(BF16)
        for p_ref, parts in zip(dp, segs):
            for jj, inner, off, take in parts:
                acc[jj, :, inner:inner + take] += _dot_tn(u, p_ref[:, off:off + take])

        @pl.when(i == nt - 1)
        def _():
            cp = pltpu.make_async_copy(acc, dwg_ref, sem)
            cp.start()
            cp.wait()

    tile = lambda w: pl.BlockSpec((tm, w), lambda i: (i, 0))
    outs, rides = _hosted_call(
        body, ride, nt,
        name="in_proj_dw",
        in_specs=[tile(w) for w in widths] + [tile(d), pl.BlockSpec(gain.shape, lambda i: (0, 0))],
        out_specs=[ANY_SPEC],
        out_shape=[jax.ShapeDtypeStruct(wg_shape, F32)],
        scratch_shapes=[pltpu.VMEM(wg_shape, F32), pltpu.SemaphoreType.DMA],
        args=(*dparts, hp, gain),
    )
    return outs[0], rides


def _in_proj_dx(dparts, hp, dh2, gain, wg, s_len, tm, ride=None):
    tp, d = hp.shape
    nt = tp // tm
    widths = [p.shape[1] for p in dparts]
    segs = _proj_segments(widths[0], widths[2], widths[4], wg.shape[2])

    def body(*refs):
        dp = refs[:6]
        hp_ref, dh2_ref, g_ref, w_ref = refs[6:10]
        gx_ref, dmeta_ref, dg_ref = refs[10:13]
        stage, sem = refs[13:]
        i = pl.program_id(0)

        @pl.when(i == 0)
        def _():
            dg_ref[...] = jnp.zeros_like(dg_ref)

        h = hp_ref[...]
        rinv = lax.rsqrt(jnp.mean(h * h, axis=-1, keepdims=True) + EPS)
        nrm = h * rinv
        gv = g_ref[...]
        du = jnp.zeros((tm, d), F32)
        for p_ref, parts in zip(dp, segs):
            for jj, inner, off, take in parts:
                du = du + _dot_nt(p_ref[:, off:off + take], w_ref[jj, :, inner:inner + take])
        dg_ref[...] += jnp.sum(du * nrm, axis=0, keepdims=True)
        dn = du * gv
        dh = dh2_ref[...] + rinv * (dn - nrm * jnp.mean(dn * nrm, axis=-1, keepdims=True))
        stage[...] = dh

        @pl.when(i == 0)
        def _():
            dmeta_ref[...] = dh[PAD_ROWS:CHUNK]
            cp = pltpu.make_async_copy(stage.at[pl.ds(CHUNK, tm - CHUNK), :], gx_ref.at[pl.ds(0, tm - CHUNK), :], sem)
            cp.start()
            cp.wait()

        @pl.when(i > 0)
        def _():
            start = pl.multiple_of(i * tm - CHUNK, CHUNK)
            cp = pltpu.make_async_copy(stage, gx_ref.at[pl.ds(start, tm), :], sem)
            cp.start()
            cp.wait()

    tile = lambda w: pl.BlockSpec((tm, w), lambda i: (i, 0))
    return _hosted_call(
        body, ride, nt,
        name="in_proj_dx",
        in_specs=[tile(w) for w in widths] + [tile(d), tile(d), pl.BlockSpec(gain.shape, lambda i: (0, 0)),
                                              pl.BlockSpec(wg.shape, lambda i: (0, 0, 0))],
        out_specs=[ANY_SPEC, pl.BlockSpec((N_META, d), lambda i: (0, 0)), pl.BlockSpec((1, d), lambda i: (0, 0))],
        out_shape=[jax.ShapeDtypeStruct((s_len, d), F32), jax.ShapeDtypeStruct((N_META, d), F32),
                   jax.ShapeDtypeStruct((1, d), F32)],
        scratch_shapes=[pltpu.VMEM((tm, d), F32), pltpu.SemaphoreType.DMA],
        args=(*dparts, hp, dh2, gain, wg),
    )


def _pair_sum(buf, recv, c_arr, tr, name):
    _, rows, cols = buf.shape

    def body(c_ref, mine_ref, got_ref, out_ref):
        out_ref[...] = (mine_ref[...] + got_ref[...]).astype(BF16)

    grid_spec = pltpu.PrefetchScalarGridSpec(
        num_scalar_prefetch=1,
        grid=(N_CHIPS, rows // tr),
        in_specs=[pl.BlockSpec((1, tr, cols), lambda jj, r, c_ref: (2 * jj + c_ref[0], r, 0)),
                  pl.BlockSpec((1, tr, cols), lambda jj, r, c_ref: (jj, r, 0))],
        out_specs=pl.BlockSpec((1, tr, cols), lambda jj, r, c_ref: (jj, r, 0)),
    )
    return pl.pallas_call(
        body,
        name=name,
        grid_spec=grid_spec,
        out_shape=jax.ShapeDtypeStruct((N_CHIPS, rows, cols), BF16),
    )(c_arr, buf, recv)


def _chip_sum(mine, got, j_arr, tr, name, loss_part=None):
    _, rows, cols = got.shape
    extra = [] if loss_part is None else [loss_part]

    def body(j_ref, mine_ref, got_ref, *rest):
        out_ref = rest[-1]
        j = j_ref[0]
        acc = None
        for jj in range(N_CHIPS):
            term = jnp.where(j == jj, mine_ref[0], got_ref[jj]).astype(F32)
            acc = term if acc is None else acc + term
        out_ref[...] = acc
        if loss_part is not None:
            out_ref[ROW_LOSS:ROW_LOSS + 1, :] = rest[0][0:1, :]

    grid_spec = pltpu.PrefetchScalarGridSpec(
        num_scalar_prefetch=1,
        grid=(rows // tr,),
        in_specs=[pl.BlockSpec((1, tr, cols), lambda r, j_ref: (j_ref[0], r, 0)),
                  pl.BlockSpec((N_CHIPS, tr, cols), lambda r, j_ref: (0, r, 0))] +
                 [pl.BlockSpec(e.shape, lambda r, j_ref: (0, 0)) for e in extra],
        out_specs=pl.BlockSpec((tr, cols), lambda r, j_ref: (r, 0)),
    )
    return pl.pallas_call(
        body,
        name=name,
        grid_spec=grid_spec,
        out_shape=jax.ShapeDtypeStruct((rows, cols), F32),
    )(j_arr, mine, got, *extra)


def _finish_exchange(f_in, f_small):
    def body(fin_ref, fs_ref, rin_ref, os_ref, send_sems, recv_sems, local_sem):
        x, y, c, chips = _position()
        j = 2 * x + y
        me = 2 * j + c
        sibling = (x, y, 1 - c)
        local = pltpu.make_async_copy(fs_ref, os_ref.at[me], local_sem)
        local.start()

        def copy(k, src, dst, to):
            return _remote(src, dst, send_sems, recv_sems, k, to)

        first = [copy(0, fin_ref, rin_ref, sibling), copy(1, fs_ref, os_ref.at[me], sibling)]
        first += [copy(2 + k, fs_ref, os_ref.at[me], (cx, cy, c)) for k, (cx, cy) in enumerate(chips)]
        for cp in first:
            cp.start()
        passed = []
        for k, (cx, cy) in enumerate(chips):
            unit = 2 * (2 * cx + cy) + c
            copy(2 + k, fs_ref, os_ref.at[unit], sibling).wait_recv()
            fwd = copy(5 + k, os_ref.at[unit], os_ref.at[unit], sibling)
            fwd.start()
            passed.append(fwd)
        copy(0, fin_ref, rin_ref, sibling).wait_recv()
        copy(1, fs_ref, os_ref.at[2 * j + 1 - c], sibling).wait_recv()
        for k, (cx, cy) in enumerate(chips):
            unit = 2 * (2 * cx + cy) + 1 - c
            copy(5 + k, fs_ref, os_ref.at[unit], sibling).wait_recv()
        for cp in first + passed:
            cp.wait_send()
        local.wait()

    return pl.pallas_call(
        body,
        name="grad_finish_exchange",
        in_specs=[ANY_SPEC] * 2,
        out_specs=[ANY_SPEC] * 2,
        out_shape=[jax.ShapeDtypeStruct(f_in.shape, F32), jax.ShapeDtypeStruct((N_DEV,) + f_small.shape, F32)],
        scratch_shapes=[pltpu.SemaphoreType.DMA((8,)), pltpu.SemaphoreType.DMA((8,)), pltpu.SemaphoreType.DMA],
    )(f_in, f_small)


def _adamw_math(w, g, m, v):
    m = ADAM_B1 * m + (1.0 - ADAM_B1) * g
    v = ADAM_B2 * v + (1.0 - ADAM_B2) * (g * g)
    m_hat = m / (1.0 - ADAM_B1 ** ADAM_STEP)
    v_hat = v / (1.0 - ADAM_B2 ** ADAM_STEP)
    delta = -ADAM_LR * (m_hat / (jnp.sqrt(v_hat) + ADAM_EPS) + ADAM_WD * w)
    return delta, m, v


def _adamw_big(w, g_mine, g_sib, m, v, c_arr, tr, name):
    rows, cols = w.shape
    half = rows // 2
    per = half // tr

    def body(c_ref, w_ref, gm_ref, gs_ref, m_ref, v_ref, g_ref, d_ref, mo_ref, vo_ref):
        g = jnp.where(pl.program_id(0) == c_ref[0], gm_ref[...], gs_ref[...])
        g_ref[...] = g
        d_ref[...], mo_ref[...], vo_ref[...] = _adamw_math(w_ref[...], g, m_ref[...], v_ref[...])

    full = pl.BlockSpec((tr, cols), lambda h, r, c_ref: (h * per + r, 0))
    unit = pl.BlockSpec((tr, cols), lambda h, r, c_ref: (r, 0))
    grid_spec = pltpu.PrefetchScalarGridSpec(
        num_scalar_prefetch=1,
        grid=(2, per),
        in_specs=[full, unit, unit, full, full],
        out_specs=[full] * 4,
    )
    return pl.pallas_call(
        body,
        name=name,
        grid_spec=grid_spec,
        out_shape=[jax.ShapeDtypeStruct(w.shape, F32)] * 4,
    )(c_arr, w, g_mine, g_sib, m, v)


def _adamw_small(j_arr, packed, params):
    names = list(params)
    n = len(names)
    vec_names = ["norm_gain", "conv_b", "b_rg", "b_ig", "lru_lambda", "ret_norm_gain", "final_norm_gain"]

    def body(j_ref, pk_ref, *refs):
        ins = refs[:3 * n]
        outs = refs[3 * n:]
        j = j_ref[0]

        def shard(row, rows):
            return jnp.concatenate([pk_ref[2 * j, row:row + rows, :], pk_ref[2 * j + 1, row:row + rows, :]], axis=1)

        def tail_sum(unit, row, rows):
            start = pl.multiple_of(UNIT_ROWS + TAIL_ROWS * unit + row, SUBLANES)
            total = pk_ref[0, pl.ds(start, rows), :]
            for dev in range(1, N_DEV):
                total = total + pk_ref[dev, pl.ds(start, rows), :]
            return total

        for idx, name in enumerate(names):
            if name == "w_rg":
                g = pk_ref[:, ROW_WR:ROW_WR + LANES, :]
            elif name == "w_ig":
                g = pk_ref[:, ROW_WI:ROW_WI + LANES, :]
            elif name == "meta_tokens":
                g = jnp.concatenate([tail_sum(2 * j, 0, N_META), tail_sum(2 * j + 1, 0, N_META)], axis=1)
            elif name == "norm_gain":
                g = jnp.concatenate([tail_sum(u, N_META, SUBLANES)[0:1] for u in range(N_DEV)], axis=1)
            elif name == "conv_w":
                g = shard(ROW_CONV, 4)
            else:
                row = ROW_VEC + vec_names.index(name)
                g = jnp.concatenate([pk_ref[u, row:row + 1, :] for u in range(N_DEV)], axis=1)
            w_ref, m_ref, v_ref = ins[3 * idx:3 * idx + 3]
            delta, m, v = _adamw_math(w_ref[...], g, m_ref[...], v_ref[...])
            g_ref, d_ref, mo_ref, vo_ref = outs[4 * idx:4 * idx + 4]
            g_ref[...], d_ref[...], mo_ref[...], vo_ref[...] = g, delta, m, v
        total = pk_ref[0, ROW_LOSS:ROW_LOSS + 1, :]
        for u in range(1, N_DEV):
            total = total + pk_ref[u, ROW_LOSS:ROW_LOSS + 1, :]
        outs[4 * n][...] = jnp.broadcast_to(total, (SUBLANES, LANES))

    flat_in, out_shape = [], []
    for name in names:
        w, m, v = params[name]
        flat_in += [w, m, v]
        out_shape += [jax.ShapeDtypeStruct(w.shape, F32)] * 4
    out_shape.append(jax.ShapeDtypeStruct((SUBLANES, LANES), F32))
    res = pl.pallas_call(
        body,
        name="adamw_small",
        in_specs=[SMEM_SPEC, VMEM_SPEC] + [VMEM_SPEC] * (3 * n),
        out_specs=[VMEM_SPEC] * (4 * n + 1),
        out_shape=out_shape,
    )(j_arr, packed, *flat_in)
    return {name: tuple(res[4 * idx:4 * idx + 4]) for idx, name in enumerate(names)}, res[4 * n][0, 0]


def _units(a):
    rows = a.shape[0]
    return jnp.transpose(a.reshape(rows, N_DEV, LANES), (1, 0, 2))


def kernel(x, meta_tokens, norm_gain, w_in, conv_w, conv_b, w_rg, b_rg, w_ig, b_ig, lru_lambda, ret_norm_gain, w_out, final_norm_gain, loss_target, m_meta_tokens, m_norm_gain, m_w_in, m_conv_w, m_conv_b, m_w_rg, m_b_rg, m_w_ig, m_b_ig, m_lru_lambda, m_ret_norm_gain, m_w_out, m_final_norm_gain, v_meta_tokens, v_norm_gain, v_w_in, v_conv_w, v_conv_b, v_w_rg, v_b_rg, v_w_ig, v_b_ig, v_lru_lambda, v_ret_norm_gain, v_w_out, v_final_norm_gain):
    s_len, d = x.shape[1], x.shape[2]
    d_lru = w_rg.shape[1] * w_rg.shape[2]
    d_ret = ret_norm_gain.shape[1]
    d_qk = HEADS * QK_DIM
    tp = s_len + CHUNK
    tm = TOKEN_TILE
    assert tp % tm == 0 and d_lru == HEADS * LANES and d_ret == HEADS * LANES
    ax, ay, ac = lax.axis_index("x"), lax.axis_index("y"), lax.axis_index("c")
    c_arr = jnp.reshape(ac, (1,)).astype(jnp.int32)
    j_arr = jnp.reshape(2 * ax + ay, (1,)).astype(jnp.int32)

    small = jnp.concatenate([meta_tokens, conv_w[0], jnp.zeros((4, meta_tokens.shape[1]), F32)], axis=0)
    wg, sg = _gather_weights(w_in[0], small)
    cols = sg.shape[2]
    meta_full = jnp.transpose(sg[:, :N_META, :], (1, 0, 2)).reshape(N_META, N_CHIPS * cols)
    cw_full = jnp.transpose(sg[:, N_META:N_META + 4, :], (1, 0, 2)).reshape(4, N_CHIPS * cols)
    cw8 = jnp.concatenate([cw_full, jnp.zeros((4, cw_full.shape[1]), F32)], axis=0)

    half = QK_DIM // 2
    inv = ROPE_BASE ** (-jnp.arange(half, dtype=F32) / half)
    pos = (jnp.arange(tp) - PAD_ROWS).astype(F32)
    ang = pos[:, None] * inv[None, :]
    cos_t = jnp.tile(jnp.cos(ang), (1, LANES // half))
    sign = jnp.where((jnp.arange(LANES) % QK_DIM) < half, -1.0, 1.0).astype(F32)
    sin_t = jnp.tile(jnp.sin(ang), (1, LANES // half)) * sign[None, :]
    tables = _ret_tables()
    gain_f = final_norm_gain.reshape(1, d)

    hp, lx, lg, q, k, v, rg, wo4 = _in_proj(x[0], meta_full, norm_gain, wg, cos_t, sin_t, w_out[0], tm,
                                            d_lru, d_qk, d_ret)
    wo = wo4.reshape(N_CHIPS * wo4.shape[1], wo4.shape[2])
    hl, y_lru, *lru_saved = _lru_fwd(lx, lg, cw8, conv_b, w_rg[0], b_rg, w_ig[0], b_ig, lru_lambda, tm)
    o, y_ret, rprev = _ret_fwd(q, k, v, rg, ret_norm_gain, tables)
    dh2, dy_lru, dy_ret, dwo, dgf, loss_acc = _out_proj_loss(y_lru, y_ret, hp, loss_target[0], wo, gain_f, tm)

    g_out = dwo.reshape(N_DEV, dwo.shape[0] // N_DEV, dwo.shape[1])
    (dq, dk, dv, drg, dgain), (r_out,) = _ret_bwd(q, k, v, rg, o, rprev, dy_ret, ret_norm_gain, cos_t, sin_t, tables,
                                                 ride=_pair_ride([g_out]))
    q_out = _pair_sum(g_out, r_out, c_arr, 128, "grad_pair_sum_out")
    (dlx, dlg, dcw, dcb, dwr, dbr, dwi, dbi, dlam), (e_out,) = _lru_bwd(
        lx, lg, hl, dy_lru, lru_saved, cw8, w_rg[0], w_ig[0], lru_lambda, tm, ride=_chip_ride([q_out]))
    f_out = _chip_sum(q_out, e_out, j_arr, 128, "grad_chip_sum_out")
    zero_row = jnp.zeros((1, d), F32)
    vecs = [zero_row, dcb, dbr, dbi, dlam, dgain, dgf]
    g_small = jnp.concatenate([dwr, dwi, jnp.zeros((N_DEV, N_META, LANES), F32), _units(dcw[0:4])]
                              + [_units(a) for a in vecs]
                              + [jnp.zeros((N_DEV, UNIT_ROWS - ROW_VEC - N_VEC, LANES), F32)], axis=1)
    dparts = [dlx, dlg, dq, dk, dv, drg]
    dwg, (s_out, r_small) = _in_proj_dw(dparts, hp, norm_gain, wg.shape, tm,
                                        ride=_join_rides(_sibling_ride([f_out]), _pair_ride([g_small])))
    g_in = dwg.reshape(N_DEV, dwg.shape[1] // 2, dwg.shape[2])
    (r_in,) = _exchange_call(_pair_ride([g_in]), "grad_pair_exchange")
    q_in = _pair_sum(g_in, r_in, c_arr, 128, "grad_pair_sum_in")
    q_small = _pair_sum(g_small, r_small, c_arr, UNIT_ROWS, "grad_pair_sum_small")
    (grad_x, dmeta, dg1), (e_in, e_small) = _in_proj_dx(dparts, hp, dh2, norm_gain, wg, s_len, tm,
                                                        ride=_chip_ride([q_in, q_small]))
    f_in = _chip_sum(q_in, e_in, j_arr, 128, "grad_chip_sum_in")
    f_small = _chip_sum(q_small, e_small, j_arr, UNIT_ROWS, "grad_chip_sum_small", loss_part=loss_acc)
    tail = jnp.concatenate([_units(dmeta), _units(dg1), jnp.zeros((N_DEV, TAIL_ROWS - N_META - 1, LANES), F32)],
                           axis=1).reshape(N_DEV * TAIL_ROWS, LANES)
    s_in, o_small = _finish_exchange(f_in, jnp.concatenate([f_small, tail], axis=0))

    res_in = _adamw_big(w_in[0], f_in, s_in, m_w_in[0], v_w_in[0], c_arr, 256, "adamw_w_in")
    res_out = _adamw_big(w_out[0], f_out, s_out, m_w_out[0], v_w_out[0], c_arr, 256, "adamw_w_out")
    small_params = {
        "meta_tokens": (meta_tokens, m_meta_tokens, v_meta_tokens),
        "norm_gain": (norm_gain, m_norm_gain, v_norm_gain),
        "conv_w": (conv_w[0], m_conv_w[0], v_conv_w[0]),
        "conv_b": (conv_b, m_conv_b, v_conv_b),
        "w_rg": (w_rg[0], m_w_rg[0], v_w_rg[0]),
        "b_rg": (b_rg, m_b_rg, v_b_rg),
        "w_ig": (w_ig[0], m_w_ig[0], v_w_ig[0]),
        "b_ig": (b_ig, m_b_ig, v_b_ig),
        "lru_lambda": (lru_lambda, m_lru_lambda, v_lru_lambda),
        "ret_norm_gain": (ret_norm_gain, m_ret_norm_gain, v_ret_norm_gain),
        "final_norm_gain": (gain_f, m_final_norm_gain.reshape(1, d), v_final_norm_gain.reshape(1, d)),
    }
    res, loss = _adamw_small(j_arr, o_small, small_params)
    res["w_in"] = tuple(res_in)
    res["w_out"] = tuple(res_out)

    order = ["meta_tokens", "norm_gain", "w_in", "conv_w", "conv_b", "w_rg", "b_rg", "w_ig", "b_ig", "lru_lambda",
             "ret_norm_gain", "w_out", "final_norm_gain"]
    shapes = {"w_in": w_in.shape, "conv_w": conv_w.shape, "w_rg": w_rg.shape, "w_ig": w_ig.shape,
              "w_out": w_out.shape, "final_norm_gain": final_norm_gain.shape}
    outs = [loss, grad_x.reshape(x.shape)]
    for kind in range(4):
        for name in order:
            a = res[name][kind]
            outs.append(a.reshape(shapes[name]) if name in shapes else a)
    return tuple(outs)
```

```python
import functools

import jax
import jax.numpy as jnp
from jax import lax
from jax.experimental import pallas as pl
from jax.experimental.pallas import tpu as pltpu

F32 = jnp.float32
BF16 = jnp.bfloat16

N_META = 16
CHUNK = 128
PAD_ROWS = CHUNK - N_META
HEADS = 8
QK_DIM = 64
LANES = 128
SUBLANES = 8
LRU_C = 8.0
EPS = 1e-6
ROPE_BASE = 10000.0
ADAM_LR = 0.001
ADAM_B1 = 0.9
ADAM_B2 = 0.999
ADAM_EPS = 1e-08
ADAM_WD = 0.01
ADAM_STEP = 10
N_CHIPS = 4
N_DEV = 8
TOKEN_TILE = 384
VMEM_LIMIT = 58 * 1024 * 1024
MESH = pl.DeviceIdType.MESH

VMEM_SPEC = pl.BlockSpec(memory_space=pltpu.VMEM)
SMEM_SPEC = pl.BlockSpec(memory_space=pltpu.SMEM)
ANY_SPEC = pl.BlockSpec(memory_space=pl.ANY)

ROW_WR, ROW_WI, ROW_META, ROW_CONV, ROW_VEC, UNIT_ROWS = 0, 128, 256, 272, 276, 288
N_VEC = 7
ROW_LOSS = ROW_VEC + N_VEC
TAIL_ROWS = 24


def _dot(a, b):
    return jnp.dot(a, b, preferred_element_type=F32)


def _dot_nt(a, b):
    return lax.dot_general(a, b, (((1,), (1,)), ((), ())), preferred_element_type=F32)


def _dot_tn(a, b):
    return lax.dot_general(a, b, (((0,), (0,)), ((), ())), preferred_element_type=F32)


def _sigmoid(x):
    return 0.5 * jnp.tanh(0.5 * x) + 0.5


def _shift_down(x, prev8, s):
    rolled = pltpu.roll(x, s, 0)
    rows = lax.broadcasted_iota(jnp.int32, (SUBLANES, x.shape[1]), 0)
    top = jnp.where(rows < s, pltpu.roll(prev8, s, 0), rolled[0:SUBLANES])
    return jnp.concatenate([top, rolled[SUBLANES:]], axis=0)


def _shift_up(x, next8, s):
    n = x.shape[0]
    rolled = pltpu.roll(x, n - s, 0)
    rows = lax.broadcasted_iota(jnp.int32, (SUBLANES, x.shape[1]), 0)
    bot = jnp.where(rows >= SUBLANES - s, pltpu.roll(next8, SUBLANES - s, 0), rolled[n - SUBLANES:n])
    return jnp.concatenate([rolled[:n - SUBLANES], bot], axis=0)


def _rot_partner(t):
    w = t.shape[1]
    lane = lax.broadcasted_iota(jnp.int32, t.shape, 1)
    first = (lane % QK_DIM) < (QK_DIM // 2)
    return jnp.where(first, pltpu.roll(t, w - QK_DIM // 2, 1), pltpu.roll(t, QK_DIM // 2, 1))


def _tile_lanes(t, reps):
    return jnp.concatenate([t] * reps, axis=1)


class _Ride:
    def __init__(self, srcs, dst_shapes, n_copies, make):
        self.srcs, self.dst_shapes, self.n_copies, self.make = list(srcs), list(dst_shapes), n_copies, make


def _join_rides(a, b):
    def make(src, dst, send_sems, recv_sems, base):
        na, da = len(a.srcs), len(a.dst_shapes)
        return (a.make(src[:na], dst[:da], send_sems, recv_sems, base)
                + b.make(src[na:], dst[da:], send_sems, recv_sems, base + a.n_copies))

    return _Ride(a.srcs + b.srcs, a.dst_shapes + b.dst_shapes, a.n_copies + b.n_copies, make)


def _position():
    x, y, c = lax.axis_index("x"), lax.axis_index("y"), lax.axis_index("c")
    return x, y, c, [(1 - x, y), (x, 1 - y), (1 - x, 1 - y)]


def _remote(src, dst, send_sems, recv_sems, k, to):
    return pltpu.make_async_remote_copy(src_ref=src, dst_ref=dst, send_sem=send_sems.at[k], recv_sem=recv_sems.at[k],
                                        device_id=to, device_id_type=MESH)


def _pair_ride(bufs):
    def make(src, dst, send_sems, recv_sems, base):
        x, y, c, _ = _position()
        return [_remote(src[b].at[2 * jj + 1 - c], dst[b].at[jj], send_sems, recv_sems, base + b * N_CHIPS + jj,
                        (x, y, 1 - c)) for b in range(len(bufs)) for jj in range(N_CHIPS)]

    shapes = [jax.ShapeDtypeStruct((N_CHIPS,) + b.shape[1:], b.dtype) for b in bufs]
    return _Ride(bufs, shapes, N_CHIPS * len(bufs), make)


def _chip_ride(bufs):
    def make(src, dst, send_sems, recv_sems, base):
        x, y, c, chips = _position()
        return [_remote(src[b].at[2 * cx + cy], dst[b].at[2 * x + y], send_sems, recv_sems, base + b * 3 + k,
                        (cx, cy, c)) for b in range(len(bufs)) for k, (cx, cy) in enumerate(chips)]

    shapes = [jax.ShapeDtypeStruct(b.shape, b.dtype) for b in bufs]
    return _Ride(bufs, shapes, 3 * len(bufs), make)


def _sibling_ride(bufs):
    def make(src, dst, send_sems, recv_sems, base):
        x, y, c, _ = _position()
        return [_remote(src[b], dst[b], send_sems, recv_sems, base + b, (x, y, 1 - c)) for b in range(len(bufs))]

    shapes = [jax.ShapeDtypeStruct(b.shape, b.dtype) for b in bufs]
    return _Ride(bufs, shapes, len(bufs), make)


def _exchange_call(ride, name):
    n_src, n_dst = len(ride.srcs), len(ride.dst_shapes)

    def body(*refs):
        copies = ride.make(refs[:n_src], refs[n_src:n_src + n_dst], refs[-2], refs[-1], 0)
        for cp in copies:
            cp.start()
        for cp in copies:
            cp.wait()

    return pl.pallas_call(
        body,
        name=name,
        in_specs=[ANY_SPEC] * n_src,
        out_specs=[ANY_SPEC] * n_dst,
        out_shape=ride.dst_shapes,
        scratch_shapes=[pltpu.SemaphoreType.DMA((ride.n_copies,)), pltpu.SemaphoreType.DMA((ride.n_copies,))],
    )(*ride.srcs)


def _hosted_call(body, ride, n_steps, *, name, in_specs, out_specs, out_shape, scratch_shapes, args):
    params = pltpu.CompilerParams(dimension_semantics=("arbitrary",), vmem_limit_bytes=VMEM_LIMIT)
    if ride is None:
        res = pl.pallas_call(body, name=name, grid=(n_steps,), in_specs=list(in_specs), out_specs=list(out_specs),
                             out_shape=list(out_shape), scratch_shapes=list(scratch_shapes),
                             compiler_params=params)(*args)
        return list(res), []
    sizes = [len(in_specs), len(ride.srcs), len(out_specs), len(ride.dst_shapes), len(scratch_shapes), 2]

    def hosted(*refs):
        groups, pos = [], 0
        for n in sizes:
            groups.append(refs[pos:pos + n])
            pos += n
        ins, rin, outs, rout, scr, (send_sems, recv_sems) = groups
        i = pl.program_id(0)

        @pl.when(i == 0)
        def _():
            for cp in ride.make(rin, rout, send_sems, recv_sems, 0):
                cp.start()

        body(*ins, *outs, *scr)

        @pl.when(i == n_steps - 1)
        def _():
            for cp in ride.make(rin, rout, send_sems, recv_sems, 0):
                cp.wait()

    n_out = len(out_specs)
    res = pl.pallas_call(
        hosted,
        name=name,
        grid=(n_steps,),
        in_specs=list(in_specs) + [ANY_SPEC] * len(ride.srcs),
        out_specs=list(out_specs) + [ANY_SPEC] * len(ride.dst_shapes),
        out_shape=list(out_shape) + ride.dst_shapes,
        scratch_shapes=list(scratch_shapes) + [pltpu.SemaphoreType.DMA((ride.n_copies,)),
                                               pltpu.SemaphoreType.DMA((ride.n_copies,))],
        compiler_params=params,
    )(*args, *ride.srcs)
    return list(res[:n_out]), list(res[n_out:])


def _gather_weights(w_in, small):
    r_in, c_in = w_in.shape
    h_in = r_in // 2

    def body(win_ref, small_ref, wg_ref, sg_ref, send_sems, recv_sems):
        x, y, c, chips = _position()
        j = 2 * x + y
        sibling = (x, y, 1 - c)

        wg_ref[j] = win_ref[...].astype(BF16)
        sg_ref[j] = small_ref[...]

        def half_in(jj, cc):
            return wg_ref.at[jj, pl.ds(cc * h_in, h_in), :]

        def copy(k, ref, to):
            return _remote(ref, ref, send_sems, recv_sems, k, to)

        first = []
        for k, (cx, cy) in enumerate(chips):
            first.append(copy(k, half_in(j, c), (cx, cy, c)))
            first.append(copy(3 + k, sg_ref.at[j], (cx, cy, c)))
        for cp in first:
            cp.start()
        passed = []
        for k, (cx, cy) in enumerate(chips):
            jk = 2 * cx + cy
            copy(k, half_in(jk, c), sibling).wait_recv()
            fwd = copy(6 + k, half_in(jk, c), sibling)
            fwd.start()
            passed.append(fwd)
        for k, (cx, cy) in enumerate(chips):
            jk = 2 * cx + cy
            copy(6 + k, half_in(jk, 1 - c), sibling).wait_recv()
            copy(3 + k, sg_ref.at[jk], sibling).wait_recv()
        for cp in first + passed:
            cp.wait_send()

    return pl.pallas_call(
        body,
        name="gather_weights",
        out_shape=(jax.ShapeDtypeStruct((N_CHIPS, r_in, c_in), BF16),
                   jax.ShapeDtypeStruct((N_CHIPS,) + small.shape, F32)),
        in_specs=[VMEM_SPEC, VMEM_SPEC],
        out_specs=(VMEM_SPEC, VMEM_SPEC),
        scratch_shapes=[pltpu.SemaphoreType.DMA((9,)), pltpu.SemaphoreType.DMA((9,))],
        compiler_params=pltpu.CompilerParams(vmem_limit_bytes=VMEM_LIMIT),
    )(w_in, small)


def _proj_segments(d_lru, d_qk, d_ret, chunk_w):
    widths = [d_lru, d_lru, d_qk, d_qk, d_ret, d_ret]
    segs, col = [], 0
    for w in widths:
        parts, off = [], 0
        while off < w:
            jj, inner = divmod(col + off, chunk_w)
            take = min(w - off, chunk_w - inner)
            parts.append((jj, inner, off, take))
            off += take
        segs.append(parts)
        col += w
    return segs


def _in_proj(x2, meta, gain, wg, cos_t, sin_t, w_out, tm, d_lru, d_qk, d_ret):
    s_len, d = x2.shape
    tp = s_len + CHUNK
    nt, nb = tp // tm, tm // CHUNK
    segs = _proj_segments(d_lru, d_qk, d_ret, wg.shape[2])
    widths = [d_lru, d_lru, d_qk, d_qk, d_ret, d_ret]
    r_out, c_out = w_out.shape
    h_out = r_out // 2
    fwd_step = min(4, nt - 1)

    def gather_w_out(i, wout_ref, wo_ref, wob, send_sems, recv_sems, local_sem):
        x, y, c, chips = _position()
        j = 2 * x + y
        sibling = (x, y, 1 - c)

        def half(jj, cc):
            return wo_ref.at[jj, pl.ds(cc * h_out, h_out), :]

        local = pltpu.make_async_copy(wob, wo_ref.at[j], local_sem)
        first = [_remote(wob.at[pl.ds(c * h_out, h_out), :], half(j, c), send_sems, recv_sems, k, (cx, cy, c))
                 for k, (cx, cy) in enumerate(chips)]
        passed = [_remote(half(2 * cx + cy, c), half(2 * cx + cy, c), send_sems, recv_sems, 3 + k, sibling)
                  for k, (cx, cy) in enumerate(chips)]

        @pl.when(i == 0)
        def _():
            wob[...] = wout_ref[...].astype(BF16)
            local.start()
            for cp in first:
                cp.start()

        @pl.when(i == fwd_step)
        def _():
            for k, (cx, cy) in enumerate(chips):
                _remote(half(2 * cx + cy, c), half(2 * cx + cy, c), send_sems, recv_sems, k, sibling).wait_recv()
                passed[k].start()

        @pl.when(i == nt - 1)
        def _():
            for k, (cx, cy) in enumerate(chips):
                jk = 2 * cx + cy
                _remote(half(jk, 1 - c), half(jk, 1 - c), send_sems, recv_sems, 3 + k, sibling).wait_recv()
            for cp in first + passed:
                cp.wait_send()
            local.wait()

    def body(*refs):
        xb = refs[:nb]
        meta_ref, g_ref, w_ref, cos_ref, sin_ref, wout_ref = refs[nb:nb + 6]
        hp_ref = refs[nb + 6]
        outs = refs[nb + 7:nb + 13]
        wo_ref, wob, send_sems, recv_sems, local_sem = refs[nb + 13:]
        i = pl.program_id(0)
        gather_w_out(i, wout_ref, wo_ref, wob, send_sems, recv_sems, local_sem)
        blocks = [r[...] for r in xb]
        head = jnp.concatenate([jnp.zeros((PAD_ROWS, d), F32), meta_ref[...]], axis=0)
        blocks[0] = jnp.where(i == 0, head, blocks[0])
        h = jnp.concatenate(blocks, axis=0)
        hp_ref[...] = h
        rinv = lax.rsqrt(jnp.mean(h * h, axis=-1, keepdims=True) + EPS)
        u = ((h * rinv) * g_ref[...]).astype(BF16)
        for out_ref, parts in zip(outs, segs):
            for jj, inner, off, take in parts:
                out_ref[:, off:off + take] = _dot(u, w_ref[jj, :, inner:inner + take])
        cos = _tile_lanes(cos_ref[...], d_qk // LANES)
        sin = _tile_lanes(sin_ref[...], d_qk // LANES)
        q = outs[2][...]
        outs[2][...] = q * cos + _rot_partner(q) * sin
        k = outs[3][...]
        outs[3][...] = (k * cos + _rot_partner(k) * sin) * (QK_DIM ** -0.5)

    x_specs = [pl.BlockSpec((CHUNK, d), functools.partial(lambda i, b: (jnp.maximum(i * nb + b - 1, 0), 0), b=b))
               for b in range(nb)]
    tile = lambda w: pl.BlockSpec((tm, w), lambda i: (i, 0))
    return pl.pallas_call(
        body,
        name="in_proj",
        grid=(nt,),
        in_specs=x_specs + [pl.BlockSpec(meta.shape, lambda i: (0, 0)),
                            pl.BlockSpec(gain.shape, lambda i: (0, 0)),
                            pl.BlockSpec(wg.shape, lambda i: (0, 0, 0)),
                            tile(LANES), tile(LANES),
                            pl.BlockSpec(w_out.shape, lambda i: (0, 0))],
        out_specs=[tile(d)] + [tile(w) for w in widths] + [ANY_SPEC],
        out_shape=[jax.ShapeDtypeStruct((tp, d), F32)] + [jax.ShapeDtypeStruct((tp, w), F32) for w in widths]
                  + [jax.ShapeDtypeStruct((N_CHIPS, r_out, c_out), BF16)],
        scratch_shapes=[pltpu.VMEM((r_out, c_out), BF16), pltpu.SemaphoreType.DMA((6,)),
                        pltpu.SemaphoreType.DMA((6,)), pltpu.SemaphoreType.DMA],
        compiler_params=pltpu.CompilerParams(dimension_semantics=("arbitrary",), vmem_limit_bytes=VMEM_LIMIT),
    )(*([x2] * nb), meta, gain, wg, cos_t, sin_t, w_out)


def _to_groups(ref3, x):
    for g in range(ref3.shape[0]):
        ref3[g] = x[:, g * LANES:(g + 1) * LANES]


def _from_groups(ref3):
    return jnp.concatenate([ref3[g] for g in range(ref3.shape[0])], axis=1)


def _segment_scan(a3, u3, out3, p3, carry, tm, reverse):
    groups = a3.shape[0]
    seg = tm // SUBLANES

    def step(j, state):
        hs, ps = state
        rows = pl.ds((seg - 1 - j) if reverse else j, SUBLANES, stride=seg)
        new_h, new_p = [], []
        for g in range(groups):
            a = a3[g, rows, :]
            h = a * hs[g] + u3[g, rows, :]
            p = ps[g] * a
            out3[g, rows, :] = h
            p3[g, rows, :] = p
            new_h.append(h)
            new_p.append(p)
        return tuple(new_h), tuple(new_p)

    zeros = tuple(jnp.zeros((SUBLANES, LANES), F32) for _ in range(groups))
    ones = tuple(jnp.ones((SUBLANES, LANES), F32) for _ in range(groups))
    lax.fori_loop(0, seg, step, (zeros, ones))
    carries = [carry[:, g * LANES:(g + 1) * LANES] for g in range(groups)]
    for s in (reversed(range(SUBLANES)) if reverse else range(SUBLANES)):
        rows = slice(s * seg, (s + 1) * seg)
        edge = s * seg if reverse else (s + 1) * seg - 1
        for g in range(groups):
            out3[g, rows, :] = out3[g, rows, :] + p3[g, rows, :] * carries[g]
            carries[g] = out3[g, edge:edge + 1, :]
    return jnp.concatenate(carries, axis=1)


def _softplus_neg(lam):
    z = -lam
    e = jnp.exp(-jnp.abs(z))
    e1 = 1.0 + e
    log1p_e = jnp.where(e1 == 1.0, e, jnp.log(e1) * (e / (e1 - 1.0)))
    return jnp.maximum(z, 0.0) + log1p_e


def _lru_fwd(lx, lg, cw, cb, wr, br, wi, bi, lam, tm):
    tp, w = lx.shape
    nt = tp // tm
    per8 = tm // SUBLANES
    n_heads = wr.shape[0]

    def body(lx_ref, lxp_ref, lg_ref, cw_ref, cb_ref, wr_ref, br_ref, wi_ref, bi_ref, lam_ref,
             hl_ref, y_ref, xc_ref, r_ref, ig_ref, a_ref, beta_ref, w4_ref, a_s, u_s, h_s, p_s, carry):
        i = pl.program_id(0)

        @pl.when(i == 0)
        def _():
            carry[...] = jnp.zeros_like(carry)

        lxv = lx_ref[...]
        prev8 = jnp.where(i == 0, 0.0, lxp_ref[...])
        xc = cb_ref[...] + _shift_down(lxv, prev8, 3) * cw_ref[0:1, :]
        xc = xc + _shift_down(lxv, prev8, 2) * cw_ref[1:2, :]
        xc = xc + _shift_down(lxv, prev8, 1) * cw_ref[2:3, :]
        xc = xc + lxv * cw_ref[3:4, :]
        xc_ref[...] = xc
        pre_r, pre_i = [], []
        for hd in range(n_heads):
            xh = xc[:, hd * LANES:(hd + 1) * LANES].astype(BF16)
            pre_r.append(_dot(xh, wr_ref[hd].astype(BF16)))
            pre_i.append(_dot(xh, wi_ref[hd].astype(BF16)))
        r = _sigmoid(jnp.concatenate(pre_r, axis=1) + br_ref[...])
        ig = _sigmoid(jnp.concatenate(pre_i, axis=1) + bi_ref[...])
        r_ref[...] = r
        ig_ref[...] = ig
        log_a = (-LRU_C * r) * _softplus_neg(lam_ref[...])
        a = jnp.exp(log_a)
        a_ref[...] = a
        zz = -2.0 * log_a
        series = zz * (1.0 - zz * (0.5 - zz * (1.0 / 6.0)))
        a2 = a * a
        beta2 = jnp.maximum(jnp.where(zz < 0.015625, series, 1.0 - a2), 1e-37)
        rsb = lax.rsqrt(beta2)
        beta = beta2 * rsb
        beta_ref[...] = beta
        w4_ref[...] = a2 * rsb
        row = lax.broadcasted_iota(jnp.int32, (tm, 1), 0) + i * tm
        _to_groups(a_s, a)
        _to_groups(u_s, jnp.where(row >= PAD_ROWS, beta * ig * xc, 0.0))
        carry[0:1, :] = _segment_scan(a_s, u_s, h_s, p_s, carry[0:1, :], tm, reverse=False)
        hl = _from_groups(h_s)
        hl_ref[...] = hl
        g = lg_ref[...]
        y_ref[...] = (hl * (g * _sigmoid(g))).astype(BF16)

    tile = pl.BlockSpec((tm, w), lambda i: (i, 0))
    prev = pl.BlockSpec((SUBLANES, w), lambda i: (jnp.maximum(i * per8 - 1, 0), 0))
    vec = pl.BlockSpec((1, w), lambda i: (0, 0))
    mat = pl.BlockSpec(wr.shape, lambda i: (0, 0, 0))
    f32_out = jax.ShapeDtypeStruct((tp, w), F32)
    return pl.pallas_call(
        body,
        name="lru_fwd",
        grid=(nt,),
        in_specs=[tile, prev, tile, pl.BlockSpec(cw.shape, lambda i: (0, 0)), vec, mat, vec, mat, vec, vec],
        out_specs=[tile] * 8,
        out_shape=[f32_out, jax.ShapeDtypeStruct((tp, w), BF16)] + [f32_out] * 6,
        scratch_shapes=[pltpu.VMEM((w // LANES, tm, LANES), F32)] * 4 + [pltpu.VMEM((SUBLANES, w), F32)],
        compiler_params=pltpu.CompilerParams(dimension_semantics=("arbitrary",), vmem_limit_bytes=VMEM_LIMIT),
    )(lx, lx, lg, cw, cb, wr, br, wi, bi, lam)


def _ret_tables():
    log_g = jnp.log1p(-jnp.exp2(-5.0 - jnp.arange(HEADS, dtype=F32)))
    idx = jnp.arange(CHUNK, dtype=F32)
    diff = idx[:, None] - idx[None, :]
    dmask = jnp.where(diff[None] >= 0.0, jnp.exp(jnp.maximum(diff, 0.0)[None] * log_g[:, None, None]), 0.0)
    kdec = jnp.repeat(jnp.exp((CHUNK - 1.0 - idx)[:, None] * log_g[None, :]), QK_DIM, axis=1)
    qdec = jnp.repeat(jnp.exp((idx + 1.0)[:, None] * log_g[None, :]), QK_DIM, axis=1)
    g_chunk = jnp.exp(CHUNK * log_g)
    g_rows = jnp.repeat(g_chunk, QK_DIM).reshape(HEADS // 2, 2 * QK_DIM, 1)
    g_state = jnp.broadcast_to(g_rows, (HEADS // 2, 2 * QK_DIM, 2 * LANES))
    r_head = jnp.arange(2 * QK_DIM)[:, None] // QK_DIM
    c_head = jnp.arange(2 * LANES)[None, :] // LANES
    block_diag = (r_head == c_head).astype(F32)
    return dmask, qdec, kdec, g_state, block_diag


def _head_norm(o_h):
    mu = jnp.mean(o_h, axis=-1, keepdims=True)
    oc = o_h - mu
    var = jnp.mean(oc * oc, axis=-1, keepdims=True)
    rstd = lax.rsqrt(var + EPS)
    return oc * rstd, rstd


def _ret_fwd(q, k, v, rg, gain, tables, tm):
    tp, d_qk = q.shape
    d_ret = v.shape[1]
    n_ch = tp // CHUNK
    cps = tm // CHUNK
    n_pairs = HEADS // 2
    dmask, qdec, kdec, g_state, block_diag = tables

    def body(q_ref, k_ref, v_ref, rg_ref, gain_ref, dm_ref, qd_ref, kd_ref, gs_ref, bd_ref,
             o_ref, y_ref, rp_ref, state):
        n = pl.program_id(0)

        @pl.when(n == 0)
        def _():
            state[...] = jnp.zeros_like(state)

        lane = lax.broadcasted_iota(jnp.int32, (CHUNK, LANES), 1)
        for ci in range(cps):
            rs = slice(ci * CHUNK, (ci + 1) * CHUNK)
            for p in range(n_pairs):
                qs = slice(p * LANES, (p + 1) * LANES)
                vs = slice(p * 2 * LANES, (p + 1) * 2 * LANES)
                qp, kp = q_ref[rs, qs], k_ref[rs, qs]
                vb = v_ref[rs, vs].astype(BF16)
                kb = kp.astype(BF16)
                qd = (qp * qd_ref[:, qs]).astype(BF16)
                kd = (kp * kd_ref[:, qs]).astype(BF16)
                st = state[p]
                st_b = st.astype(BF16)
                rp_ref[ci, p] = st_b
                cross = _dot(qd, st_b)
                for e in range(2):
                    hd = 2 * p + e
                    hs = slice(hd * LANES, (hd + 1) * LANES)
                    es = slice(e * LANES, (e + 1) * LANES)
                    qm = jnp.where((lane // QK_DIM) == e, qp, 0.0).astype(BF16)
                    s = _dot_nt(qm, kb) * dm_ref[hd]
                    o_h = _dot(s.astype(BF16), vb[:, es]) + cross[:, es]
                    o_ref[rs, hs] = o_h
                    xhat, _ = _head_norm(o_h)
                    g = rg_ref[rs, hs]
                    y_ref[rs, hs] = ((xhat * gain_ref[:, hs]) * (g * _sigmoid(g))).astype(BF16)
                state[p] = gs_ref[p] * st + bd_ref[...] * _dot_tn(kd, vb)

    ch = lambda w: pl.BlockSpec((tm, w), lambda n: (n, 0))
    const2 = lambda a: pl.BlockSpec(a.shape, lambda n: (0, 0))
    const3 = lambda a: pl.BlockSpec(a.shape, lambda n: (0, 0, 0))
    return pl.pallas_call(
        body,
        name="ret_fwd",
        grid=(n_ch // cps,),
        in_specs=[ch(d_qk), ch(d_qk), ch(d_ret), ch(d_ret), const2(gain), const3(dmask), const2(qdec), const2(kdec),
                  const3(g_state), const2(block_diag)],
        out_specs=[ch(d_ret), ch(d_ret),
                   pl.BlockSpec((cps, n_pairs, 2 * QK_DIM, 2 * LANES), lambda n: (n, 0, 0, 0))],
        out_shape=[jax.ShapeDtypeStruct((tp, d_ret), F32), jax.ShapeDtypeStruct((tp, d_ret), BF16),
                   jax.ShapeDtypeStruct((n_ch, n_pairs, 2 * QK_DIM, 2 * LANES), BF16)],
        scratch_shapes=[pltpu.VMEM((n_pairs, 2 * QK_DIM, 2 * LANES), F32)],
        compiler_params=pltpu.CompilerParams(dimension_semantics=("arbitrary",), vmem_limit_bytes=VMEM_LIMIT),
    )(q, k, v, rg, gain, dmask, qdec, kdec, g_state, block_diag)


def _out_proj_loss(y_lru, y_ret, hp, tgt, wo, gain_f, tm):
    tp, d = hp.shape
    w_lru = y_lru.shape[1]
    w_mix = wo.shape[0]
    nt, nb = tp // tm, tm // CHUNK

    def body(*refs):
        yl_ref, yr_ref, hp_ref = refs[:3]
        tb = refs[3:3 + nb]
        wo_ref, gf_ref = refs[3 + nb:5 + nb]
        dh2_ref, dyl_ref, dyr_ref, dwo_ref, dgf_ref, loss_ref = refs[5 + nb:]
        i = pl.program_id(0)

        @pl.when(i == 0)
        def _():
            dwo_ref[...] = jnp.zeros_like(dwo_ref)
            dgf_ref[...] = jnp.zeros_like(dgf_ref)
            loss_ref[...] = jnp.zeros_like(loss_ref)

        yl, yr = yl_ref[...], yr_ref[...]
        h2 = hp_ref[...] + _dot(yl, wo_ref[0:w_lru, :]) + _dot(yr, wo_ref[w_lru:w_mix, :])
        rinv = lax.rsqrt(jnp.mean(h2 * h2, axis=-1, keepdims=True) + EPS)
        nrm = h2 * rinv
        gf = gf_ref[...]
        tgt_v = jnp.concatenate([r[...] for r in tb], axis=0)
        row = lax.broadcasted_iota(jnp.int32, (tm, 1), 0) + i * tm
        err = jnp.where(row >= CHUNK, nrm * gf - tgt_v, 0.0)
        loss_ref[...] += 0.5 * jnp.sum(jnp.mean(err * err, axis=-1, keepdims=True))
        dout = err * (1.0 / d)
        dgf_ref[...] += jnp.sum(dout * nrm, axis=0, keepdims=True)
        dn = dout * gf
        dh2 = rinv * (dn - nrm * jnp.mean(dn * nrm, axis=-1, keepdims=True))
        dh2_ref[...] = dh2
        dh2b = dh2.astype(BF16)
        dyl_ref[...] = _dot_nt(dh2b, wo_ref[0:w_lru, :])
        dyr_ref[...] = _dot_nt(dh2b, wo_ref[w_lru:w_mix, :])
        dwo_ref[0:w_lru, :] += _dot_tn(yl, dh2b)
        dwo_ref[w_lru:w_mix, :] += _dot_tn(yr, dh2b)

    tile = lambda w: pl.BlockSpec((tm, w), lambda i: (i, 0))
    t_specs = [pl.BlockSpec((CHUNK, d), functools.partial(lambda i, b: (jnp.maximum(i * nb + b - 1, 0), 0), b=b))
               for b in range(nb)]
    return pl.pallas_call(
        body,
        name="out_proj_loss",
        grid=(nt,),
        in_specs=[tile(w_lru), tile(w_mix - w_lru), tile(d)] + t_specs +
                 [pl.BlockSpec(wo.shape, lambda i: (0, 0)), pl.BlockSpec(gain_f.shape, lambda i: (0, 0))],
        out_specs=[tile(d), tile(w_lru), tile(w_mix - w_lru), pl.BlockSpec(wo.shape, lambda i: (0, 0)),
                   pl.BlockSpec((1, d), lambda i: (0, 0)), pl.BlockSpec((SUBLANES, LANES), lambda i: (0, 0))],
        out_shape=[jax.ShapeDtypeStruct((tp, d), F32), jax.ShapeDtypeStruct((tp, w_lru), F32),
                   jax.ShapeDtypeStruct((tp, w_mix - w_lru), F32), jax.ShapeDtypeStruct(wo.shape, F32),
                   jax.ShapeDtypeStruct((1, d), F32), jax.ShapeDtypeStruct((SUBLANES, LANES), F32)],
        compiler_params=pltpu.CompilerParams(dimension_semantics=("arbitrary",), vmem_limit_bytes=VMEM_LIMIT),
    )(y_lru, y_ret, hp, *([tgt] * nb), wo, gain_f)


def _ret_bwd(q, k, v, rg, o, rprev, dy, gain, cos_t, sin_t, tables, tm, ride=None):
    tp, d_qk = q.shape
    d_ret = v.shape[1]
    n_ch = tp // CHUNK
    cps = tm // CHUNK
    n_pairs = HEADS // 2
    dmask, qdec, kdec, g_state, block_diag = tables

    def body(q_ref, k_ref, v_ref, rg_ref, o_ref, rp_ref, dy_ref, gain_ref, cos_ref, sin_ref,
             dm_ref, qd_ref, kd_ref, gs_ref, bd_ref, dq_ref, dk_ref, dv_ref, drg_ref, dgain_ref, dstate):
        n = pl.program_id(0)

        @pl.when(n == 0)
        def _():
            dstate[...] = jnp.zeros_like(dstate)
            dgain_ref[...] = jnp.zeros_like(dgain_ref)

        lane = lax.broadcasted_iota(jnp.int32, (CHUNK, LANES), 1)
        for ci in reversed(range(cps)):
            rs = slice(ci * CHUNK, (ci + 1) * CHUNK)
            dq_parts, dk_parts = [], []
            for p in range(n_pairs):
                qs = slice(p * LANES, (p + 1) * LANES)
                vs = slice(p * 2 * LANES, (p + 1) * 2 * LANES)
                do_parts = []
                for e in range(2):
                    hd = 2 * p + e
                    hs = slice(hd * LANES, (hd + 1) * LANES)
                    xhat, rstd = _head_norm(o_ref[rs, hs])
                    g = rg_ref[rs, hs]
                    sg = _sigmoid(g)
                    dyh = dy_ref[rs, hs]
                    gn = gain_ref[:, hs]
                    d_on = dyh * (g * sg)
                    drg_ref[rs, hs] = (dyh * (xhat * gn) * (sg * (1.0 + g * (1.0 - sg)))).astype(BF16)
                    dgain_ref[:, hs] += jnp.sum(d_on * xhat, axis=0, keepdims=True)
                    dxh = d_on * gn
                    do_parts.append(rstd * (dxh - jnp.mean(dxh, axis=-1, keepdims=True)
                                            - xhat * jnp.mean(dxh * xhat, axis=-1, keepdims=True)))
                do_b = jnp.concatenate(do_parts, axis=1).astype(BF16)
                qp, kp = q_ref[rs, qs], k_ref[rs, qs]
                vb = v_ref[rs, vs].astype(BF16)
                kb = kp.astype(BF16)
                qd = (qp * qd_ref[:, qs]).astype(BF16)
                kd = (kp * kd_ref[:, qs]).astype(BF16)
                dst = dstate[p]
                dst_b = dst.astype(BF16)
                dqp = _dot_nt(do_b, rp_ref[ci, p]) * qd_ref[:, qs]
                dkp = _dot_nt(vb, dst_b) * kd_ref[:, qs]
                dvp = _dot(kd, dst_b)
                dv_parts = []
                for e in range(2):
                    hd = 2 * p + e
                    es = slice(e * LANES, (e + 1) * LANES)
                    mine = (lane // QK_DIM) == e
                    qm = jnp.where(mine, qp, 0.0).astype(BF16)
                    km = jnp.where(mine, kp, 0.0).astype(BF16)
                    dm = dm_ref[hd]
                    s = (_dot_nt(qm, kb) * dm).astype(BF16)
                    ds = (_dot_nt(do_b[:, es], vb[:, es]) * dm).astype(BF16)
                    dv_parts.append(dvp[:, es] + _dot_tn(s, do_b[:, es]))
                    dqp = dqp + _dot(ds, km)
                    dkp = dkp + _dot_tn(ds, qm)
                dv_ref[rs, vs] = jnp.concatenate(dv_parts, axis=1).astype(BF16)
                dstate[p] = gs_ref[p] * dst + bd_ref[...] * _dot_tn(qd, do_b)
                dq_parts.append(dqp)
                dk_parts.append(dkp)
            cos = _tile_lanes(cos_ref[rs, :], d_qk // LANES)
            sin = _tile_lanes(sin_ref[rs, :], d_qk // LANES)
            dq = jnp.concatenate(dq_parts, axis=1)
            dk = jnp.concatenate(dk_parts, axis=1) * (QK_DIM ** -0.5)
            dq_ref[rs, :] = (dq * cos + _rot_partner(dq * sin)).astype(BF16)
            dk_ref[rs, :] = (dk * cos + _rot_partner(dk * sin)).astype(BF16)

    last = n_ch // cps - 1
    ch = lambda w: pl.BlockSpec((tm, w), lambda n: (last - n, 0))
    const2 = lambda a: pl.BlockSpec(a.shape, lambda n: (0, 0))
    const3 = lambda a: pl.BlockSpec(a.shape, lambda n: (0, 0, 0))
    return _hosted_call(
        body, ride, n_ch // cps,
        name="ret_bwd",
        in_specs=[ch(d_qk), ch(d_qk), ch(d_ret), ch(d_ret), ch(d_ret),
                  pl.BlockSpec((cps, n_pairs, 2 * QK_DIM, 2 * LANES), lambda n: (last - n, 0, 0, 0)),
                  ch(d_ret), const2(gain), ch(LANES), ch(LANES),
                  const3(dmask), const2(qdec), const2(kdec), const3(g_state), const2(block_diag)],
        out_specs=[ch(d_qk), ch(d_qk), ch(d_ret), ch(d_ret), pl.BlockSpec((1, d_ret), lambda n: (0, 0))],
        out_shape=[jax.ShapeDtypeStruct((tp, d_qk), BF16), jax.ShapeDtypeStruct((tp, d_qk), BF16),
                   jax.ShapeDtypeStruct((tp, d_ret), BF16), jax.ShapeDtypeStruct((tp, d_ret), BF16),
                   jax.ShapeDtypeStruct((1, d_ret), F32)],
        scratch_shapes=[pltpu.VMEM((n_pairs, 2 * QK_DIM, 2 * LANES), F32)],
        args=(q, k, v, rg, o, rprev, dy, gain, cos_t, sin_t, dmask, qdec, kdec, g_state, block_diag),
    )


def _lru_bwd(lx, lg, hl, dy, saved, cw, wr, wi, lam, tm, ride=None):
    tp, w = lx.shape
    nt = tp // tm
    per8 = tm // SUBLANES
    n_heads = wr.shape[0]

    def body(lx_ref, lg_ref, hl_ref, hlp_ref, dy_ref, xc_ref, r_ref, ig_ref, a_ref, beta_ref, w4_ref,
             cw_ref, wr_ref, wi_ref, lam_ref,
             dlx_ref, dlg_ref, dcw_ref, dcb_ref, dwr_ref, dbr_ref, dwi_ref, dbi_ref, dlam_ref,
             g_s, dh_s, b_s, p_s, carry, dxc_next, a_next):
        i = pl.program_id(0)
        first_tile = i == nt - 1

        @pl.when(i == 0)
        def _():
            carry[...] = jnp.zeros_like(carry)
            dxc_next[...] = jnp.zeros_like(dxc_next)
            a_next[...] = jnp.zeros_like(a_next)
            for r in (dcw_ref, dcb_ref, dwr_ref, dbr_ref, dwi_ref, dbi_ref, dlam_ref):
                r[...] = jnp.zeros_like(r)

        lxv = lx_ref[...]
        a, beta, r, ig, xc = a_ref[...], beta_ref[...], r_ref[...], ig_ref[...], xc_ref[...]
        g = lg_ref[...]
        sg = _sigmoid(g)
        dyv = dy_ref[...]
        hlv = hl_ref[...]
        dlg_ref[...] = (dyv * hlv * (sg * (1.0 + g * (1.0 - sg)))).astype(BF16)
        _to_groups(g_s, dyv * (g * sg))
        _to_groups(b_s, _shift_up(a, a_next[...], 1))
        carry[0:1, :] = _segment_scan(b_s, g_s, dh_s, p_s, carry[0:1, :], tm, reverse=True)
        a_next[...] = a[0:SUBLANES]
        dh = _from_groups(dh_s)
        hprev = _shift_down(hlv, jnp.where(first_tile, 0.0, hlp_ref[...]), 1)
        row = lax.broadcasted_iota(jnp.int32, (tm, 1), 0) + (nt - 1 - i) * tm
        du = jnp.where(row >= PAD_ROWS, dh, 0.0)
        da = dh * hprev
        dbeta = du * ig * xc
        d_ig = du * beta * xc
        dxc = du * beta * ig
        dloga = da * a - dbeta * w4_ref[...]
        lam_v = lam_ref[...]
        dlam_ref[...] += jnp.sum(dloga * r, axis=0, keepdims=True) * (LRU_C * _sigmoid(-lam_v))
        dpr = (dloga * (-LRU_C * _softplus_neg(lam_v))) * r * (1.0 - r)
        dpi = d_ig * ig * (1.0 - ig)
        dbr_ref[...] += jnp.sum(dpr, axis=0, keepdims=True)
        dbi_ref[...] += jnp.sum(dpi, axis=0, keepdims=True)
        dxc_parts = []
        for hd in range(n_heads):
            hs = slice(hd * LANES, (hd + 1) * LANES)
            xh = xc[:, hs].astype(BF16)
            dprh = dpr[:, hs].astype(BF16)
            dpih = dpi[:, hs].astype(BF16)
            dwr_ref[hd] += _dot_tn(xh, dprh)
            dwi_ref[hd] += _dot_tn(xh, dpih)
            dxc_parts.append(_dot_nt(dprh, wr_ref[hd].astype(BF16)) + _dot_nt(dpih, wi_ref[hd].astype(BF16)))
        dxc = dxc + jnp.concatenate(dxc_parts, axis=1)
        nxt = dxc_next[...]
        up1, up2, up3 = _shift_up(dxc, nxt, 1), _shift_up(dxc, nxt, 2), _shift_up(dxc, nxt, 3)
        dlx = dxc * cw_ref[3:4, :]
        dlx = dlx + up1 * cw_ref[2:3, :]
        dlx = dlx + up2 * cw_ref[1:2, :]
        dlx = dlx + up3 * cw_ref[0:1, :]
        dlx_ref[...] = dlx.astype(BF16)
        dxc_next[...] = dxc[0:SUBLANES]
        dcb_ref[...] += jnp.sum(dxc, axis=0, keepdims=True)
        dcw_ref[0:1, :] += jnp.sum(up3 * lxv, axis=0, keepdims=True)
        dcw_ref[1:2, :] += jnp.sum(up2 * lxv, axis=0, keepdims=True)
        dcw_ref[2:3, :] += jnp.sum(up1 * lxv, axis=0, keepdims=True)
        dcw_ref[3:4, :] += jnp.sum(dxc * lxv, axis=0, keepdims=True)

    last = nt - 1
    tile = pl.BlockSpec((tm, w), lambda i: (last - i, 0))
    prev = pl.BlockSpec((SUBLANES, w), lambda i: (jnp.maximum((last - i) * per8 - 1, 0), 0))
    vec = pl.BlockSpec((1, w), lambda i: (0, 0))
    mat = pl.BlockSpec(wr.shape, lambda i: (0, 0, 0))
    cwb = pl.BlockSpec(cw.shape, lambda i: (0, 0))
    return _hosted_call(
        body, ride, nt,
        name="lru_bwd",
        in_specs=[tile, tile, tile, prev, tile] + [tile] * 6 + [cwb, mat, mat, vec],
        out_specs=[tile, tile, cwb, vec, mat, vec, mat, vec, vec],
        out_shape=[jax.ShapeDtypeStruct((tp, w), BF16), jax.ShapeDtypeStruct((tp, w), BF16),
                   jax.ShapeDtypeStruct(cw.shape, F32), jax.ShapeDtypeStruct((1, w), F32),
                   jax.ShapeDtypeStruct(wr.shape, F32), jax.ShapeDtypeStruct((1, w), F32),
                   jax.ShapeDtypeStruct(wr.shape, F32), jax.ShapeDtypeStruct((1, w), F32),
                   jax.ShapeDtypeStruct((1, w), F32)],
        scratch_shapes=[pltpu.VMEM((w // LANES, tm, LANES), F32)] * 4 + [pltpu.VMEM((SUBLANES, w), F32)] * 3,
        args=(lx, lg, hl, hl, dy, *saved, cw, wr, wi, lam),
    )


def _in_proj_dw(dparts, hp, gain, wg_shape, tm, ride=None):
    tp, d = hp.shape
    nt = tp // tm
    widths = [p.shape[1] for p in dparts]
    segs = _proj_segments(widths[0], widths[2], widths[4], wg_shape[2])

    def body(*refs):
        dp = refs[:6]
        hp_ref, g_ref, dwg_ref, acc, sem = refs[6:]
        i = pl.program_id(0)

        @pl.when(i == 0)
        def _():
            acc[...] = jnp.zeros_like(acc)

        h = hp_ref[...]
        rinv = lax.rsqrt(jnp.mean(h * h, axis=-1, keepdims=True) + EPS)
        u = ((h * rinv) * g_ref[...]).astype(BF16)
        for p_ref, parts in zip(dp, segs):
            for jj, inner, off, take in parts:
                acc[jj, :, inner:inner + take] += _dot_tn(u, p_ref[:, off:off + take])

        @pl.when(i == nt - 1)
        def _():
            cp = pltpu.make_async_copy(acc, dwg_ref, sem)
            cp.start()
            cp.wait()

    tile = lambda w: pl.BlockSpec((tm, w), lambda i: (i, 0))
    outs, rides = _hosted_call(
        body, ride, nt,
        name="in_proj_dw",
        in_specs=[tile(w) for w in widths] + [tile(d), pl.BlockSpec(gain.shape, lambda i: (0, 0))],
        out_specs=[ANY_SPEC],
        out_shape=[jax.ShapeDtypeStruct(wg_shape, F32)],
        scratch_shapes=[pltpu.VMEM(wg_shape, F32), pltpu.SemaphoreType.DMA],
        args=(*dparts, hp, gain),
    )
    return outs[0], rides


def _in_proj_dx(dparts, hp, dh2, gain, wg, s_len, tm, ride=None):
    tp, d = hp.shape
    nt = tp // tm
    widths = [p.shape[1] for p in dparts]
    segs = _proj_segments(widths[0], widths[2], widths[4], wg.shape[2])

    def body(*refs):
        dp = refs[:6]
        hp_ref, dh2_ref, g_ref, w_ref = refs[6:10]
        gx_ref, dmeta_ref, dg_ref = refs[10:13]
        stage, sems = refs[13:]
        i = pl.program_id(0)

        @pl.when(i == 0)
        def _():
            dg_ref[...] = jnp.zeros_like(dg_ref)

        h = hp_ref[...]
        rinv = lax.rsqrt(jnp.mean(h * h, axis=-1, keepdims=True) + EPS)
        nrm = h * rinv
        gv = g_ref[...]
        du = jnp.zeros((tm, d), F32)
        for p_ref, parts in zip(dp, segs):
            for jj, inner, off, take in parts:
                du = du + _dot_nt(p_ref[:, off:off + take], w_ref[jj, :, inner:inner + take])
        dg_ref[...] += jnp.sum(du * nrm, axis=0, keepdims=True)
        dn = du * gv
        dh = dh2_ref[...] + rinv * (dn - nrm * jnp.mean(dn * nrm, axis=-1, keepdims=True))

        def first_copy():
            return pltpu.make_async_copy(stage.at[0, pl.ds(CHUNK, tm - CHUNK), :],
                                         gx_ref.at[pl.ds(0, tm - CHUNK), :], sems.at[0])

        def tile_copy(slot, start):
            return pltpu.make_async_copy(stage.at[slot], gx_ref.at[pl.ds(start, tm), :], sems.at[slot])

        @pl.when(i == 0)
        def _():
            dmeta_ref[...] = dh[PAD_ROWS:CHUNK]
            stage[0] = dh
            first_copy().start()

        @pl.when(i > 0)
        def _():
            slot = 1 + i % 2

            @pl.when(i >= 3)
            def _():
                tile_copy(slot, 0).wait()

            stage[slot] = dh
            tile_copy(slot, pl.multiple_of(i * tm - CHUNK, CHUNK)).start()

        @pl.when(i == nt - 1)
        def _():
            first_copy().wait()
            for step in (nt - 2, nt - 1):
                if step >= 1:
                    tile_copy(1 + step % 2, 0).wait()

    tile = lambda w: pl.BlockSpec((tm, w), lambda i: (i, 0))
    return _hosted_call(
        body, ride, nt,
        name="in_proj_dx",
        in_specs=[tile(w) for w in widths] + [tile(d), tile(d), pl.BlockSpec(gain.shape, lambda i: (0, 0)),
                                              pl.BlockSpec(wg.shape, lambda i: (0, 0, 0))],
        out_specs=[ANY_SPEC, pl.BlockSpec((N_META, d), lambda i: (0, 0)), pl.BlockSpec((1, d), lambda i: (0, 0))],
        out_shape=[jax.ShapeDtypeStruct((s_len, d), F32), jax.ShapeDtypeStruct((N_META, d), F32),
                   jax.ShapeDtypeStruct((1, d), F32)],
        scratch_shapes=[pltpu.VMEM((3, tm, d), F32), pltpu.SemaphoreType.DMA((3,))],
        args=(*dparts, hp, dh2, gain, wg),
    )


def _pair_sum(buf, recv, c_arr, tr, name):
    _, rows, cols = buf.shape

    def body(c_ref, mine_ref, got_ref, out_ref):
        out_ref[...] = (mine_ref[...] + got_ref[...]).astype(BF16)

    grid_spec = pltpu.PrefetchScalarGridSpec(
        num_scalar_prefetch=1,
        grid=(N_CHIPS, rows // tr),
        in_specs=[pl.BlockSpec((1, tr, cols), lambda jj, r, c_ref: (2 * jj + c_ref[0], r, 0)),
                  pl.BlockSpec((1, tr, cols), lambda jj, r, c_ref: (jj, r, 0))],
        out_specs=pl.BlockSpec((1, tr, cols), lambda jj, r, c_ref: (jj, r, 0)),
    )
    return pl.pallas_call(
        body,
        name=name,
        grid_spec=grid_spec,
        out_shape=jax.ShapeDtypeStruct((N_CHIPS, rows, cols), BF16),
    )(c_arr, buf, recv)


def _chip_sum(mine, got, j_arr, tr, name, loss_part=None):
    _, rows, cols = got.shape
    extra = [] if loss_part is None else [loss_part]

    def body(j_ref, mine_ref, got_ref, *rest):
        out_ref = rest[-1]
        j = j_ref[0]
        acc = None
        for jj in range(N_CHIPS):
            term = jnp.where(j == jj, mine_ref[0], got_ref[jj]).astype(F32)
            acc = term if acc is None else acc + term
        out_ref[...] = acc
        if loss_part is not None:
            out_ref[ROW_LOSS:ROW_LOSS + 1, :] = rest[0][0:1, :]

    grid_spec = pltpu.PrefetchScalarGridSpec(
        num_scalar_prefetch=1,
        grid=(rows // tr,),
        in_specs=[pl.BlockSpec((1, tr, cols), lambda r, j_ref: (j_ref[0], r, 0)),
                  pl.BlockSpec((N_CHIPS, tr, cols), lambda r, j_ref: (0, r, 0))] +
                 [pl.BlockSpec(e.shape, lambda r, j_ref: (0, 0)) for e in extra],
        out_specs=pl.BlockSpec((tr, cols), lambda r, j_ref: (r, 0)),
    )
    return pl.pallas_call(
        body,
        name=name,
        grid_spec=grid_spec,
        out_shape=jax.ShapeDtypeStruct((rows, cols), F32),
    )(j_arr, mine, got, *extra)


def _finish_exchange(f_in, f_small):
    def body(fin_ref, fs_ref, rin_ref, os_ref, send_sems, recv_sems, local_sem):
        x, y, c, chips = _position()
        j = 2 * x + y
        me = 2 * j + c
        sibling = (x, y, 1 - c)
        local = pltpu.make_async_copy(fs_ref, os_ref.at[me], local_sem)
        local.start()

        def copy(k, src, dst, to):
            return _remote(src, dst, send_sems, recv_sems, k, to)

        first = [copy(0, fin_ref, rin_ref, sibling), copy(1, fs_ref, os_ref.at[me], sibling)]
        first += [copy(2 + k, fs_ref, os_ref.at[me], (cx, cy, c)) for k, (cx, cy) in enumerate(chips)]
        for cp in first:
            cp.start()
        passed = []
        for k, (cx, cy) in enumerate(chips):
            unit = 2 * (2 * cx + cy) + c
            copy(2 + k, fs_ref, os_ref.at[unit], sibling).wait_recv()
            fwd = copy(5 + k, os_ref.at[unit], os_ref.at[unit], sibling)
            fwd.start()
            passed.append(fwd)
        copy(0, fin_ref, rin_ref, sibling).wait_recv()
        copy(1, fs_ref, os_ref.at[2 * j + 1 - c], sibling).wait_recv()
        for k, (cx, cy) in enumerate(chips):
            unit = 2 * (2 * cx + cy) + 1 - c
            copy(5 + k, fs_ref, os_ref.at[unit], sibling).wait_recv()
        for cp in first + passed:
            cp.wait_send()
        local.wait()

    return pl.pallas_call(
        body,
        name="grad_finish_exchange",
        in_specs=[ANY_SPEC] * 2,
        out_specs=[ANY_SPEC] * 2,
        out_shape=[jax.ShapeDtypeStruct(f_in.shape, F32), jax.ShapeDtypeStruct((N_DEV,) + f_small.shape, F32)],
        scratch_shapes=[pltpu.SemaphoreType.DMA((8,)), pltpu.SemaphoreType.DMA((8,)), pltpu.SemaphoreType.DMA],
    )(f_in, f_small)


def _adamw_math(w, g, m, v):
    m = ADAM_B1 * m + (1.0 - ADAM_B1) * g
    v = ADAM_B2 * v + (1.0 - ADAM_B2) * (g * g)
    m_hat = m / (1.0 - ADAM_B1 ** ADAM_STEP)
    v_hat = v / (1.0 - ADAM_B2 ** ADAM_STEP)
    delta = -ADAM_LR * (m_hat / (jnp.sqrt(v_hat) + ADAM_EPS) + ADAM_WD * w)
    return delta, m, v


def _adamw_big(w, g_mine, g_sib, m, v, c_arr, tr, name):
    rows, cols = w.shape
    half = rows // 2
    per = half // tr

    def body(c_ref, w_ref, gm_ref, gs_ref, m_ref, v_ref, g_ref, d_ref, mo_ref, vo_ref):
        g = jnp.where(pl.program_id(0) == c_ref[0], gm_ref[...], gs_ref[...])
        g_ref[...] = g
        d_ref[...], mo_ref[...], vo_ref[...] = _adamw_math(w_ref[...], g, m_ref[...], v_ref[...])

    full = pl.BlockSpec((tr, cols), lambda h, r, c_ref: (h * per + r, 0))
    unit = pl.BlockSpec((tr, cols), lambda h, r, c_ref: (r, 0))
    grid_spec = pltpu.PrefetchScalarGridSpec(
        num_scalar_prefetch=1,
        grid=(2, per),
        in_specs=[full, unit, unit, full, full],
        out_specs=[full] * 4,
    )
    return pl.pallas_call(
        body,
        name=name,
        grid_spec=grid_spec,
        out_shape=[jax.ShapeDtypeStruct(w.shape, F32)] * 4,
    )(c_arr, w, g_mine, g_sib, m, v)


def _adamw_small(j_arr, packed, params):
    names = list(params)
    n = len(names)
    vec_names = ["norm_gain", "conv_b", "b_rg", "b_ig", "lru_lambda", "ret_norm_gain", "final_norm_gain"]

    def body(j_ref, pk_ref, *refs):
        ins = refs[:3 * n]
        outs = refs[3 * n:]
        j = j_ref[0]

        def shard(row, rows):
            return jnp.concatenate([pk_ref[2 * j, row:row + rows, :], pk_ref[2 * j + 1, row:row + rows, :]], axis=1)

        def tail_sum(unit, row, rows):
            start = pl.multiple_of(UNIT_ROWS + TAIL_ROWS * unit + row, SUBLANES)
            total = pk_ref[0, pl.ds(start, rows), :]
            for dev in range(1, N_DEV):
                total = total + pk_ref[dev, pl.ds(start, rows), :]
            return total

        for idx, name in enumerate(names):
            if name == "w_rg":
                g = pk_ref[:, ROW_WR:ROW_WR + LANES, :]
            elif name == "w_ig":
                g = pk_ref[:, ROW_WI:ROW_WI + LANES, :]
            elif name == "meta_tokens":
                g = jnp.concatenate([tail_sum(2 * j, 0, N_META), tail_sum(2 * j + 1, 0, N_META)], axis=1)
            elif name == "norm_gain":
                g = jnp.concatenate([tail_sum(u, N_META, SUBLANES)[0:1] for u in range(N_DEV)], axis=1)
            elif name == "conv_w":
                g = shard(ROW_CONV, 4)
            else:
                row = ROW_VEC + vec_names.index(name)
                g = jnp.concatenate([pk_ref[u, row:row + 1, :] for u in range(N_DEV)], axis=1)
            w_ref, m_ref, v_ref = ins[3 * idx:3 * idx + 3]
            delta, m, v = _adamw_math(w_ref[...], g, m_ref[...], v_ref[...])
            g_ref, d_ref, mo_ref, vo_ref = outs[4 * idx:4 * idx + 4]
            g_ref[...], d_ref[...], mo_ref[...], vo_ref[...] = g, delta, m, v
        total = pk_ref[0, ROW_LOSS:ROW_LOSS + 1, :]
        for u in range(1, N_DEV):
            total = total + pk_ref[u, ROW_LOSS:ROW_LOSS + 1, :]
        outs[4 * n][...] = jnp.broadcast_to(total, (SUBLANES, LANES))

    flat_in, out_shape = [], []
    for name in names:
        w, m, v = params[name]
        flat_in += [w, m, v]
        out_shape += [jax.ShapeDtypeStruct(w.shape, F32)] * 4
    out_shape.append(jax.ShapeDtypeStruct((SUBLANES, LANES), F32))
    res = pl.pallas_call(
        body,
        name="adamw_small",
        in_specs=[SMEM_SPEC, VMEM_SPEC] + [VMEM_SPEC] * (3 * n),
        out_specs=[VMEM_SPEC] * (4 * n + 1),
        out_shape=out_shape,
    )(j_arr, packed, *flat_in)
    return {name: tuple(res[4 * idx:4 * idx + 4]) for idx, name in enumerate(names)}, res[4 * n][0, 0]


def _units(a):
    rows = a.shape[0]
    return jnp.transpose(a.reshape(rows, N_DEV, LANES), (1, 0, 2))


def kernel(x, meta_tokens, norm_gain, w_in, conv_w, conv_b, w_rg, b_rg, w_ig, b_ig, lru_lambda, ret_norm_gain, w_out, final_norm_gain, loss_target, m_meta_tokens, m_norm_gain, m_w_in, m_conv_w, m_conv_b, m_w_rg, m_b_rg, m_w_ig, m_b_ig, m_lru_lambda, m_ret_norm_gain, m_w_out, m_final_norm_gain, v_meta_tokens, v_norm_gain, v_w_in, v_conv_w, v_conv_b, v_w_rg, v_b_rg, v_w_ig, v_b_ig, v_lru_lambda, v_ret_norm_gain, v_w_out, v_final_norm_gain):
    s_len, d = x.shape[1], x.shape[2]
    d_lru = w_rg.shape[1] * w_rg.shape[2]
    d_ret = ret_norm_gain.shape[1]
    d_qk = HEADS * QK_DIM
    tp = s_len + CHUNK
    tm = TOKEN_TILE
    assert tp % tm == 0 and d_lru == HEADS * LANES and d_ret == HEADS * LANES
    ax, ay, ac = lax.axis_index("x"), lax.axis_index("y"), lax.axis_index("c")
    c_arr = jnp.reshape(ac, (1,)).astype(jnp.int32)
    j_arr = jnp.reshape(2 * ax + ay, (1,)).astype(jnp.int32)

    small = jnp.concatenate([meta_tokens, conv_w[0], jnp.zeros((4, meta_tokens.shape[1]), F32)], axis=0)
    wg, sg = _gather_weights(w_in[0], small)
    cols = sg.shape[2]
    meta_full = jnp.transpose(sg[:, :N_META, :], (1, 0, 2)).reshape(N_META, N_CHIPS * cols)
    cw_full = jnp.transpose(sg[:, N_META:N_META + 4, :], (1, 0, 2)).reshape(4, N_CHIPS * cols)
    cw8 = jnp.concatenate([cw_full, jnp.zeros((4, cw_full.shape[1]), F32)], axis=0)

    half = QK_DIM // 2
    inv = ROPE_BASE ** (-jnp.arange(half, dtype=F32) / half)
    pos = (jnp.arange(tp) - PAD_ROWS).astype(F32)
    ang = pos[:, None] * inv[None, :]
    cos_t = jnp.tile(jnp.cos(ang), (1, LANES // half))
    sign = jnp.where((jnp.arange(LANES) % QK_DIM) < half, -1.0, 1.0).astype(F32)
    sin_t = jnp.tile(jnp.sin(ang), (1, LANES // half)) * sign[None, :]
    tables = _ret_tables()
    gain_f = final_norm_gain.reshape(1, d)

    hp, lx, lg, q, k, v, rg, wo4 = _in_proj(x[0], meta_full, norm_gain, wg, cos_t, sin_t, w_out[0], tm,
                                            d_lru, d_qk, d_ret)
    wo = wo4.reshape(N_CHIPS * wo4.shape[1], wo4.shape[2])
    hl, y_lru, *lru_saved = _lru_fwd(lx, lg, cw8, conv_b, w_rg[0], b_rg, w_ig[0], b_ig, lru_lambda, tm)
    o, y_ret, rprev = _ret_fwd(q, k, v, rg, ret_norm_gain, tables, tm)
    dh2, dy_lru, dy_ret, dwo, dgf, loss_acc = _out_proj_loss(y_lru, y_ret, hp, loss_target[0], wo, gain_f, tm)

    g_out = dwo.reshape(N_DEV, dwo.shape[0] // N_DEV, dwo.shape[1])
    (dq, dk, dv, drg, dgain), (r_out,) = _ret_bwd(q, k, v, rg, o, rprev, dy_ret, ret_norm_gain, cos_t, sin_t, tables,
                                                 tm, ride=_pair_ride([g_out]))
    q_out = _pair_sum(g_out, r_out, c_arr, 128, "grad_pair_sum_out")
    (dlx, dlg, dcw, dcb, dwr, dbr, dwi, dbi, dlam), (e_out,) = _lru_bwd(
        lx, lg, hl, dy_lru, lru_saved, cw8, w_rg[0], w_ig[0], lru_lambda, tm, ride=_chip_ride([q_out]))
    f_out = _chip_sum(q_out, e_out, j_arr, 128, "grad_chip_sum_out")
    zero_row = jnp.zeros((1, d), F32)
    vecs = [zero_row, dcb, dbr, dbi, dlam, dgain, dgf]
    g_small = jnp.concatenate([dwr, dwi, jnp.zeros((N_DEV, N_META, LANES), F32), _units(dcw[0:4])]
                              + [_units(a) for a in vecs]
                              + [jnp.zeros((N_DEV, UNIT_ROWS - ROW_VEC - N_VEC, LANES), F32)], axis=1)
    dparts = [dlx, dlg, dq, dk, dv, drg]
    dwg, (s_out, r_small) = _in_proj_dw(dparts, hp, norm_gain, wg.shape, tm,
                                        ride=_join_rides(_sibling_ride([f_out]), _pair_ride([g_small])))
    g_in = dwg.reshape(N_DEV, dwg.shape[1] // 2, dwg.shape[2])
    (r_in,) = _exchange_call(_pair_ride([g_in]), "grad_pair_exchange")
    q_in = _pair_sum(g_in, r_in, c_arr, 128, "grad_pair_sum_in")
    q_small = _pair_sum(g_small, r_small, c_arr, UNIT_ROWS, "grad_pair_sum_small")
    (grad_x, dmeta, dg1), (e_in, e_small) = _in_proj_dx(dparts, hp, dh2, norm_gain, wg, s_len, tm,
                                                        ride=_chip_ride([q_in, q_small]))
    f_in = _chip_sum(q_in, e_in, j_arr, 128, "grad_chip_sum_in")
    f_small = _chip_sum(q_small, e_small, j_arr, UNIT_ROWS, "grad_chip_sum_small", loss_part=loss_acc)
    tail = jnp.concatenate([_units(dmeta), _units(dg1), jnp.zeros((N_DEV, TAIL_ROWS - N_META - 1, LANES), F32)],
                           axis=1).reshape(N_DEV * TAIL_ROWS, LANES)
    s_in, o_small = _finish_exchange(f_in, jnp.concatenate([f_small, tail], axis=0))

    res_in = _adamw_big(w_in[0], f_in, s_in, m_w_in[0], v_w_in[0], c_arr, 256, "adamw_w_in")
    res_out = _adamw_big(w_out[0], f_out, s_out, m_w_out[0], v_w_out[0], c_arr, 256, "adamw_w_out")
    small_params = {
        "meta_tokens": (meta_tokens, m_meta_tokens, v_meta_tokens),
        "norm_gain": (norm_gain, m_norm_gain, v_norm_gain),
        "conv_w": (conv_w[0], m_conv_w[0], v_conv_w[0]),
        "conv_b": (conv_b, m_conv_b, v_conv_b),
        "w_rg": (w_rg[0], m_w_rg[0], v_w_rg[0]),
        "b_rg": (b_rg, m_b_rg, v_b_rg),
        "w_ig": (w_ig[0], m_w_ig[0], v_w_ig[0]),
        "b_ig": (b_ig, m_b_ig, v_b_ig),
        "lru_lambda": (lru_lambda, m_lru_lambda, v_lru_lambda),
        "ret_norm_gain": (ret_norm_gain, m_ret_norm_gain, v_ret_norm_gain),
        "final_norm_gain": (gain_f, m_final_norm_gain.reshape(1, d), v_final_norm_gain.reshape(1, d)),
    }
    res, loss = _adamw_small(j_arr, o_small, small_params)
    res["w_in"] = tuple(res_in)
    res["w_out"] = tuple(res_out)

    order = ["meta_tokens", "norm_gain", "w_in", "conv_w", "conv_b", "w_rg", "b_rg", "w_ig", "b_ig", "lru_lambda",
             "ret_norm_gain", "w_out", "final_norm_gain"]
    shapes = {"w_in": w_in.shape, "conv_w": conv_w.shape, "w_rg": w_rg.shape, "w_ig": w_ig.shape,
              "w_out": w_out.shape, "final_norm_gain": final_norm_gain.shape}
    outs = [loss, grad_x.reshape(x.shape)]
    for kind in range(4):
        for name in order:
            a = res[name][kind]
            outs.append(a.reshape(shapes[name]) if name in shapes else a)
    return tuple(outs)
```

```python
import functools

import jax
import jax.numpy as jnp
from jax import lax
from jax.experimental import pallas as pl
from jax.experimental.pallas import tpu as pltpu

F32 = jnp.float32
BF16 = jnp.bfloat16

N_META = 16
CHUNK = 128
PAD_ROWS = CHUNK - N_META
HEADS = 8
QK_DIM = 64
LANES = 128
SUBLANES = 8
LRU_C = 8.0
EPS = 1e-6
ROPE_BASE = 10000.0
ADAM_LR = 0.001
ADAM_B1 = 0.9
ADAM_B2 = 0.999
ADAM_EPS = 1e-08
ADAM_WD = 0.01
ADAM_STEP = 10
N_CHIPS = 4
N_DEV = 8
TOKEN_TILE = 384
VMEM_LIMIT = 58 * 1024 * 1024
MESH = pl.DeviceIdType.MESH

VMEM_SPEC = pl.BlockSpec(memory_space=pltpu.VMEM)
SMEM_SPEC = pl.BlockSpec(memory_space=pltpu.SMEM)
ANY_SPEC = pl.BlockSpec(memory_space=pl.ANY)

ROW_WR, ROW_WI, ROW_META, ROW_CONV, ROW_VEC, UNIT_ROWS = 0, 128, 256, 272, 276, 288
N_VEC = 7
ROW_LOSS = ROW_VEC + N_VEC
TAIL_ROWS = 24


def _dot(a, b):
    return jnp.dot(a, b, preferred_element_type=F32)


def _dot_nt(a, b):
    return lax.dot_general(a, b, (((1,), (1,)), ((), ())), preferred_element_type=F32)


def _dot_tn(a, b):
    return lax.dot_general(a, b, (((0,), (0,)), ((), ())), preferred_element_type=F32)


def _sigmoid(x):
    return 0.5 * jnp.tanh(0.5 * x) + 0.5


def _shift_down(x, prev8, s):
    rolled = pltpu.roll(x, s, 0)
    rows = lax.broadcasted_iota(jnp.int32, (SUBLANES, x.shape[1]), 0)
    top = jnp.where(rows < s, pltpu.roll(prev8, s, 0), rolled[0:SUBLANES])
    return jnp.concatenate([top, rolled[SUBLANES:]], axis=0)


def _shift_up(x, next8, s):
    n = x.shape[0]
    rolled = pltpu.roll(x, n - s, 0)
    rows = lax.broadcasted_iota(jnp.int32, (SUBLANES, x.shape[1]), 0)
    bot = jnp.where(rows >= SUBLANES - s, pltpu.roll(next8, SUBLANES - s, 0), rolled[n - SUBLANES:n])
    return jnp.concatenate([rolled[:n - SUBLANES], bot], axis=0)


def _rot_partner(t):
    w = t.shape[1]
    lane = lax.broadcasted_iota(jnp.int32, t.shape, 1)
    first = (lane % QK_DIM) < (QK_DIM // 2)
    return jnp.where(first, pltpu.roll(t, w - QK_DIM // 2, 1), pltpu.roll(t, QK_DIM // 2, 1))


def _tile_lanes(t, reps):
    return jnp.concatenate([t] * reps, axis=1)


class _Ride:
    def __init__(self, srcs, dst_shapes, n_copies, make):
        self.srcs, self.dst_shapes, self.n_copies, self.make = list(srcs), list(dst_shapes), n_copies, make


def _join_rides(a, b):
    def make(src, dst, send_sems, recv_sems, base):
        na, da = len(a.srcs), len(a.dst_shapes)
        return (a.make(src[:na], dst[:da], send_sems, recv_sems, base)
                + b.make(src[na:], dst[da:], send_sems, recv_sems, base + a.n_copies))

    return _Ride(a.srcs + b.srcs, a.dst_shapes + b.dst_shapes, a.n_copies + b.n_copies, make)


def _position():
    x, y, c = lax.axis_index("x"), lax.axis_index("y"), lax.axis_index("c")
    return x, y, c, [(1 - x, y), (x, 1 - y), (1 - x, 1 - y)]


def _remote(src, dst, send_sems, recv_sems, k, to):
    return pltpu.make_async_remote_copy(src_ref=src, dst_ref=dst, send_sem=send_sems.at[k], recv_sem=recv_sems.at[k],
                                        device_id=to, device_id_type=MESH)


def _pair_ride(bufs):
    def make(src, dst, send_sems, recv_sems, base):
        x, y, c, _ = _position()
        return [_remote(src[b].at[2 * jj + 1 - c], dst[b].at[jj], send_sems, recv_sems, base + b * N_CHIPS + jj,
                        (x, y, 1 - c)) for b in range(len(bufs)) for jj in range(N_CHIPS)]

    shapes = [jax.ShapeDtypeStruct((N_CHIPS,) + b.shape[1:], b.dtype) for b in bufs]
    return _Ride(bufs, shapes, N_CHIPS * len(bufs), make)


def _chip_ride(bufs):
    def make(src, dst, send_sems, recv_sems, base):
        x, y, c, chips = _position()
        return [_remote(src[b].at[2 * cx + cy], dst[b].at[2 * x + y], send_sems, recv_sems, base + b * 3 + k,
                        (cx, cy, c)) for b in range(len(bufs)) for k, (cx, cy) in enumerate(chips)]

    shapes = [jax.ShapeDtypeStruct(b.shape, b.dtype) for b in bufs]
    return _Ride(bufs, shapes, 3 * len(bufs), make)


def _sibling_ride(bufs):
    def make(src, dst, send_sems, recv_sems, base):
        x, y, c, _ = _position()
        return [_remote(src[b], dst[b], send_sems, recv_sems, base + b, (x, y, 1 - c)) for b in range(len(bufs))]

    shapes = [jax.ShapeDtypeStruct(b.shape, b.dtype) for b in bufs]
    return _Ride(bufs, shapes, len(bufs), make)


def _exchange_call(ride, name):
    n_src, n_dst = len(ride.srcs), len(ride.dst_shapes)

    def body(*refs):
        copies = ride.make(refs[:n_src], refs[n_src:n_src + n_dst], refs[-2], refs[-1], 0)
        for cp in copies:
            cp.start()
        for cp in copies:
            cp.wait()

    return pl.pallas_call(
        body,
        name=name,
        in_specs=[ANY_SPEC] * n_src,
        out_specs=[ANY_SPEC] * n_dst,
        out_shape=ride.dst_shapes,
        scratch_shapes=[pltpu.SemaphoreType.DMA((ride.n_copies,)), pltpu.SemaphoreType.DMA((ride.n_copies,))],
    )(*ride.srcs)


def _hosted_call(body, ride, n_steps, *, name, in_specs, out_specs, out_shape, scratch_shapes, args):
    params = pltpu.CompilerParams(dimension_semantics=("arbitrary",), vmem_limit_bytes=VMEM_LIMIT)
    if ride is None:
        res = pl.pallas_call(body, name=name, grid=(n_steps,), in_specs=list(in_specs), out_specs=list(out_specs),
                             out_shape=list(out_shape), scratch_shapes=list(scratch_shapes),
                             compiler_params=params)(*args)
        return list(res), []
    sizes = [len(in_specs), len(ride.srcs), len(out_specs), len(ride.dst_shapes), len(scratch_shapes), 2]

    def hosted(*refs):
        groups, pos = [], 0
        for n in sizes:
            groups.append(refs[pos:pos + n])
            pos += n
        ins, rin, outs, rout, scr, (send_sems, recv_sems) = groups
        i = pl.program_id(0)

        @pl.when(i == 0)
        def _():
            for cp in ride.make(rin, rout, send_sems, recv_sems, 0):
                cp.start()

        body(*ins, *outs, *scr)

        @pl.when(i == n_steps - 1)
        def _():
            for cp in ride.make(rin, rout, send_sems, recv_sems, 0):
                cp.wait()

    n_out = len(out_specs)
    res = pl.pallas_call(
        hosted,
        name=name,
        grid=(n_steps,),
        in_specs=list(in_specs) + [ANY_SPEC] * len(ride.srcs),
        out_specs=list(out_specs) + [ANY_SPEC] * len(ride.dst_shapes),
        out_shape=list(out_shape) + ride.dst_shapes,
        scratch_shapes=list(scratch_shapes) + [pltpu.SemaphoreType.DMA((ride.n_copies,)),
                                               pltpu.SemaphoreType.DMA((ride.n_copies,))],
        compiler_params=params,
    )(*args, *ride.srcs)
    return list(res[:n_out]), list(res[n_out:])


def _gather_weights(w_in, small):
    r_in, c_in = w_in.shape
    h_in = r_in // 2
    q_in = h_in // 2

    def body(win_ref, small_ref, wg_ref, sg_ref, send_sems, recv_sems):
        x, y, c, chips = _position()
        j = 2 * x + y
        sibling = (x, y, 1 - c)
        xn, yn, dg = chips
        jx, jy, jd = (2 * cx + cy for cx, cy in chips)

        wg_ref[j] = win_ref[...].astype(BF16)
        sg_ref[j] = small_ref[...]

        def half(jj, cc):
            return wg_ref.at[jj, pl.ds(cc * h_in, h_in), :]

        def quarter(jj, qq):
            return wg_ref.at[jj, pl.ds(c * h_in + qq * q_in, q_in), :]

        def copy(k, ref, to):
            return _remote(ref, ref, send_sems, recv_sems, k, to)

        first = [copy(0, quarter(j, 0), (*xn, c)), copy(2, quarter(j, 1), (*yn, c)),
                 copy(1, quarter(j, 1), (*xn, c)), copy(3, quarter(j, 0), (*yn, c))]
        first += [copy(9 + k, sg_ref.at[j], (cx, cy, c)) for k, (cx, cy) in enumerate(chips)]
        for cp in first:
            cp.start()
        copy(0, quarter(jx, 0), sibling).wait_recv()
        along_y = copy(4, quarter(jx, 0), (*yn, c))
        along_y.start()
        copy(2, quarter(jy, 1), sibling).wait_recv()
        along_x = copy(5, quarter(jy, 1), (*xn, c))
        along_x.start()
        copy(1, quarter(jx, 1), sibling).wait_recv()
        to_sib = [copy(6, half(jx, c), sibling)]
        to_sib[-1].start()
        copy(3, quarter(jy, 0), sibling).wait_recv()
        to_sib.append(copy(7, half(jy, c), sibling))
        to_sib[-1].start()
        copy(4, quarter(jd, 0), sibling).wait_recv()
        copy(5, quarter(jd, 1), sibling).wait_recv()
        to_sib.append(copy(8, half(jd, c), sibling))
        to_sib[-1].start()
        for k, jk in enumerate((jx, jy, jd)):
            copy(6 + k, half(jk, 1 - c), sibling).wait_recv()
            copy(9 + k, sg_ref.at[jk], sibling).wait_recv()
        for cp in first + [along_y, along_x] + to_sib:
            cp.wait_send()

    return pl.pallas_call(
        body,
        name="gather_weights",
        out_shape=(jax.ShapeDtypeStruct((N_CHIPS, r_in, c_in), BF16),
                   jax.ShapeDtypeStruct((N_CHIPS,) + small.shape, F32)),
        in_specs=[VMEM_SPEC, VMEM_SPEC],
        out_specs=(VMEM_SPEC, VMEM_SPEC),
        scratch_shapes=[pltpu.SemaphoreType.DMA((12,)), pltpu.SemaphoreType.DMA((12,))],
        compiler_params=pltpu.CompilerParams(vmem_limit_bytes=VMEM_LIMIT),
    )(w_in, small)


def _proj_segments(d_lru, d_qk, d_ret, chunk_w):
    widths = [d_lru, d_lru, d_qk, d_qk, d_ret, d_ret]
    segs, col = [], 0
    for w in widths:
        parts, off = [], 0
        while off < w:
            jj, inner = divmod(col + off, chunk_w)
            take = min(w - off, chunk_w - inner)
            parts.append((jj, inner, off, take))
            off += take
        segs.append(parts)
        col += w
    return segs


def _in_proj(x2, meta, gain, wg, cos_t, sin_t, w_out, tm, d_lru, d_qk, d_ret):
    s_len, d = x2.shape
    tp = s_len + CHUNK
    nt, nb = tp // tm, tm // CHUNK
    segs = _proj_segments(d_lru, d_qk, d_ret, wg.shape[2])
    widths = [d_lru, d_lru, d_qk, d_qk, d_ret, d_ret]
    r_out, c_out = w_out.shape
    h_out = r_out // 2
    fwd_step = min(4, nt - 1)

    def gather_w_out(i, wout_ref, wo_ref, wob, send_sems, recv_sems, local_sem):
        x, y, c, chips = _position()
        j = 2 * x + y
        sibling = (x, y, 1 - c)

        def half(jj, cc):
            return wo_ref.at[jj, pl.ds(cc * h_out, h_out), :]

        local = pltpu.make_async_copy(wob, wo_ref.at[j], local_sem)
        first = [_remote(wob.at[pl.ds(c * h_out, h_out), :], half(j, c), send_sems, recv_sems, k, (cx, cy, c))
                 for k, (cx, cy) in enumerate(chips)]
        passed = [_remote(half(2 * cx + cy, c), half(2 * cx + cy, c), send_sems, recv_sems, 3 + k, sibling)
                  for k, (cx, cy) in enumerate(chips)]

        @pl.when(i == 0)
        def _():
            wob[...] = wout_ref[...].astype(BF16)
            local.start()
            for cp in first:
                cp.start()

        @pl.when(i == fwd_step)
        def _():
            for k, (cx, cy) in enumerate(chips):
                _remote(half(2 * cx + cy, c), half(2 * cx + cy, c), send_sems, recv_sems, k, sibling).wait_recv()
                passed[k].start()

        @pl.when(i == nt - 1)
        def _():
            for k, (cx, cy) in enumerate(chips):
                jk = 2 * cx + cy
                _remote(half(jk, 1 - c), half(jk, 1 - c), send_sems, recv_sems, 3 + k, sibling).wait_recv()
            for cp in first + passed:
                cp.wait_send()
            local.wait()

    def body(*refs):
        xb = refs[:nb]
        meta_ref, g_ref, w_ref, cos_ref, sin_ref, wout_ref = refs[nb:nb + 6]
        hp_ref = refs[nb + 6]
        outs = refs[nb + 7:nb + 13]
        wo_ref, wob, send_sems, recv_sems, local_sem = refs[nb + 13:]
        i = pl.program_id(0)
        gather_w_out(i, wout_ref, wo_ref, wob, send_sems, recv_sems, local_sem)
        blocks = [r[...] for r in xb]
        head = jnp.concatenate([jnp.zeros((PAD_ROWS, d), F32), meta_ref[...]], axis=0)
        blocks[0] = jnp.where(i == 0, head, blocks[0])
        h = jnp.concatenate(blocks, axis=0)
        hp_ref[...] = h
        rinv = lax.rsqrt(jnp.mean(h * h, axis=-1, keepdims=True) + EPS)
        u = ((h * rinv) * g_ref[...]).astype(BF16)
        for out_ref, parts in zip(outs, segs):
            for jj, inner, off, take in parts:
                out_ref[:, off:off + take] = _dot(u, w_ref[jj, :, inner:inner + take])
        cos = _tile_lanes(cos_ref[...], d_qk // LANES)
        sin = _tile_lanes(sin_ref[...], d_qk // LANES)
        q = outs[2][...]
        outs[2][...] = q * cos + _rot_partner(q) * sin
        k = outs[3][...]
        outs[3][...] = (k * cos + _rot_partner(k) * sin) * (QK_DIM ** -0.5)

    x_specs = [pl.BlockSpec((CHUNK, d), functools.partial(lambda i, b: (jnp.maximum(i * nb + b - 1, 0), 0), b=b))
               for b in range(nb)]
    tile = lambda w: pl.BlockSpec((tm, w), lambda i: (i, 0))
    return pl.pallas_call(
        body,
        name="in_proj",
        grid=(nt,),
        in_specs=x_specs + [pl.BlockSpec(meta.shape, lambda i: (0, 0)),
                            pl.BlockSpec(gain.shape, lambda i: (0, 0)),
                            pl.BlockSpec(wg.shape, lambda i: (0, 0, 0)),
                            tile(LANES), tile(LANES),
                            pl.BlockSpec(w_out.shape, lambda i: (0, 0))],
        out_specs=[tile(d)] + [tile(w) for w in widths] + [ANY_SPEC],
        out_shape=[jax.ShapeDtypeStruct((tp, d), F32)] + [jax.ShapeDtypeStruct((tp, w), F32) for w in widths]
                  + [jax.ShapeDtypeStruct((N_CHIPS, r_out, c_out), BF16)],
        scratch_shapes=[pltpu.VMEM((r_out, c_out), BF16), pltpu.SemaphoreType.DMA((6,)),
                        pltpu.SemaphoreType.DMA((6,)), pltpu.SemaphoreType.DMA],
        compiler_params=pltpu.CompilerParams(dimension_semantics=("arbitrary",), vmem_limit_bytes=VMEM_LIMIT),
    )(*([x2] * nb), meta, gain, wg, cos_t, sin_t, w_out)


def _to_groups(ref3, x):
    for g in range(ref3.shape[0]):
        ref3[g] = x[:, g * LANES:(g + 1) * LANES]


def _from_groups(ref3):
    return jnp.concatenate([ref3[g] for g in range(ref3.shape[0])], axis=1)


def _segment_scan(a3, u3, out3, p3, carry, tm, reverse):
    groups = a3.shape[0]
    seg = tm // SUBLANES

    def step(j, state):
        hs, ps = state
        rows = pl.ds((seg - 1 - j) if reverse else j, SUBLANES, stride=seg)
        new_h, new_p = [], []
        for g in range(groups):
            a = a3[g, rows, :]
            h = a * hs[g] + u3[g, rows, :]
            p = ps[g] * a
            out3[g, rows, :] = h
            p3[g, rows, :] = p
            new_h.append(h)
            new_p.append(p)
        return tuple(new_h), tuple(new_p)

    zeros = tuple(jnp.zeros((SUBLANES, LANES), F32) for _ in range(groups))
    ones = tuple(jnp.ones((SUBLANES, LANES), F32) for _ in range(groups))
    lax.fori_loop(0, seg, step, (zeros, ones))
    carries = [carry[:, g * LANES:(g + 1) * LANES] for g in range(groups)]
    for s in (reversed(range(SUBLANES)) if reverse else range(SUBLANES)):
        rows = slice(s * seg, (s + 1) * seg)
        edge = s * seg if reverse else (s + 1) * seg - 1
        for g in range(groups):
            out3[g, rows, :] = out3[g, rows, :] + p3[g, rows, :] * carries[g]
            carries[g] = out3[g, edge:edge + 1, :]
    return jnp.concatenate(carries, axis=1)


def _softplus_neg(lam):
    z = -lam
    e = jnp.exp(-jnp.abs(z))
    e1 = 1.0 + e
    log1p_e = jnp.where(e1 == 1.0, e, jnp.log(e1) * (e / (e1 - 1.0)))
    return jnp.maximum(z, 0.0) + log1p_e


def _lru_fwd(lx, lg, cw, cb, wr, br, wi, bi, lam, tm):
    tp, w = lx.shape
    nt = tp // tm
    per8 = tm // SUBLANES
    n_heads = wr.shape[0]

    def body(lx_ref, lxp_ref, lg_ref, cw_ref, cb_ref, wr_ref, br_ref, wi_ref, bi_ref, lam_ref,
             hl_ref, y_ref, xc_ref, r_ref, ig_ref, a_ref, beta_ref, w4_ref, a_s, u_s, h_s, p_s, carry):
        i = pl.program_id(0)

        @pl.when(i == 0)
        def _():
            carry[...] = jnp.zeros_like(carry)

        lxv = lx_ref[...]
        prev8 = jnp.where(i == 0, 0.0, lxp_ref[...])
        xc = cb_ref[...] + _shift_down(lxv, prev8, 3) * cw_ref[0:1, :]
        xc = xc + _shift_down(lxv, prev8, 2) * cw_ref[1:2, :]
        xc = xc + _shift_down(lxv, prev8, 1) * cw_ref[2:3, :]
        xc = xc + lxv * cw_ref[3:4, :]
        xc_ref[...] = xc
        pre_r, pre_i = [], []
        for hd in range(n_heads):
            xh = xc[:, hd * LANES:(hd + 1) * LANES].astype(BF16)
            pre_r.append(_dot(xh, wr_ref[hd].astype(BF16)))
            pre_i.append(_dot(xh, wi_ref[hd].astype(BF16)))
        r = _sigmoid(jnp.concatenate(pre_r, axis=1) + br_ref[...])
        ig = _sigmoid(jnp.concatenate(pre_i, axis=1) + bi_ref[...])
        r_ref[...] = r
        ig_ref[...] = ig
        log_a = (-LRU_C * r) * _softplus_neg(lam_ref[...])
        a = jnp.exp(log_a)
        a_ref[...] = a
        zz = -2.0 * log_a
        series = zz * (1.0 - zz * (0.5 - zz * (1.0 / 6.0)))
        a2 = a * a
        beta2 = jnp.maximum(jnp.where(zz < 0.015625, series, 1.0 - a2), 1e-37)
        rsb = lax.rsqrt(beta2)
        beta = beta2 * rsb
        beta_ref[...] = beta
        w4_ref[...] = a2 * rsb
        row = lax.broadcasted_iota(jnp.int32, (tm, 1), 0) + i * tm
        _to_groups(a_s, a)
        _to_groups(u_s, jnp.where(row >= PAD_ROWS, beta * ig * xc, 0.0))
        carry[0:1, :] = _segment_scan(a_s, u_s, h_s, p_s, carry[0:1, :], tm, reverse=False)
        hl = _from_groups(h_s)
        hl_ref[...] = hl
        g = lg_ref[...]
        y_ref[...] = (hl * (g * _sigmoid(g))).astype(BF16)

    tile = pl.BlockSpec((tm, w), lambda i: (i, 0))
    prev = pl.BlockSpec((SUBLANES, w), lambda i: (jnp.maximum(i * per8 - 1, 0), 0))
    vec = pl.BlockSpec((1, w), lambda i: (0, 0))
    mat = pl.BlockSpec(wr.shape, lambda i: (0, 0, 0))
    f32_out = jax.ShapeDtypeStruct((tp, w), F32)
    return pl.pallas_call(
        body,
        name="lru_fwd",
        grid=(nt,),
        in_specs=[tile, prev, tile, pl.BlockSpec(cw.shape, lambda i: (0, 0)), vec, mat, vec, mat, vec, vec],
        out_specs=[tile] * 8,
        out_shape=[f32_out, jax.ShapeDtypeStruct((tp, w), BF16)] + [f32_out] * 6,
        scratch_shapes=[pltpu.VMEM((w // LANES, tm, LANES), F32)] * 4 + [pltpu.VMEM((SUBLANES, w), F32)],
        compiler_params=pltpu.CompilerParams(dimension_semantics=("arbitrary",), vmem_limit_bytes=VMEM_LIMIT),
    )(lx, lx, lg, cw, cb, wr, br, wi, bi, lam)


def _ret_tables():
    log_g = jnp.log1p(-jnp.exp2(-5.0 - jnp.arange(HEADS, dtype=F32)))
    idx = jnp.arange(CHUNK, dtype=F32)
    diff = idx[:, None] - idx[None, :]
    dmask = jnp.where(diff[None] >= 0.0, jnp.exp(jnp.maximum(diff, 0.0)[None] * log_g[:, None, None]), 0.0)
    kdec = jnp.repeat(jnp.exp((CHUNK - 1.0 - idx)[:, None] * log_g[None, :]), QK_DIM, axis=1)
    qdec = jnp.repeat(jnp.exp((idx + 1.0)[:, None] * log_g[None, :]), QK_DIM, axis=1)
    g_chunk = jnp.exp(CHUNK * log_g)
    g_rows = jnp.repeat(g_chunk, QK_DIM).reshape(HEADS // 2, 2 * QK_DIM, 1)
    g_state = jnp.broadcast_to(g_rows, (HEADS // 2, 2 * QK_DIM, 2 * LANES))
    r_head = jnp.arange(2 * QK_DIM)[:, None] // QK_DIM
    c_head = jnp.arange(2 * LANES)[None, :] // LANES
    block_diag = (r_head == c_head).astype(F32)
    return dmask, qdec, kdec, g_state, block_diag


def _head_norm(o_h):
    mu = jnp.mean(o_h, axis=-1, keepdims=True)
    oc = o_h - mu
    var = jnp.mean(oc * oc, axis=-1, keepdims=True)
    rstd = lax.rsqrt(var + EPS)
    return oc * rstd, rstd


def _ret_fwd(q, k, v, rg, gain, tables, tm):
    tp, d_qk = q.shape
    d_ret = v.shape[1]
    n_ch = tp // CHUNK
    cps = tm // CHUNK
    n_pairs = HEADS // 2
    dmask, qdec, kdec, g_state, block_diag = tables

    def body(q_ref, k_ref, v_ref, rg_ref, gain_ref, dm_ref, qd_ref, kd_ref, gs_ref, bd_ref,
             o_ref, y_ref, rp_ref, state):
        n = pl.program_id(0)

        @pl.when(n == 0)
        def _():
            state[...] = jnp.zeros_like(state)

        lane = lax.broadcasted_iota(jnp.int32, (CHUNK, LANES), 1)
        for ci in range(cps):
            rs = slice(ci * CHUNK, (ci + 1) * CHUNK)
            for p in range(n_pairs):
                qs = slice(p * LANES, (p + 1) * LANES)
                vs = slice(p * 2 * LANES, (p + 1) * 2 * LANES)
                qp, kp = q_ref[rs, qs], k_ref[rs, qs]
                vb = v_ref[rs, vs].astype(BF16)
                kb = kp.astype(BF16)
                qd = (qp * qd_ref[:, qs]).astype(BF16)
                kd = (kp * kd_ref[:, qs]).astype(BF16)
                st = state[p]
                st_b = st.astype(BF16)
                rp_ref[ci, p] = st_b
                cross = _dot(qd, st_b)
                for e in range(2):
                    hd = 2 * p + e
                    hs = slice(hd * LANES, (hd + 1) * LANES)
                    es = slice(e * LANES, (e + 1) * LANES)
                    qm = jnp.where((lane // QK_DIM) == e, qp, 0.0).astype(BF16)
                    s = _dot_nt(qm, kb) * dm_ref[hd]
                    o_h = _dot(s.astype(BF16), vb[:, es]) + cross[:, es]
                    o_ref[rs, hs] = o_h
                    xhat, _ = _head_norm(o_h)
                    g = rg_ref[rs, hs]
                    y_ref[rs, hs] = ((xhat * gain_ref[:, hs]) * (g * _sigmoid(g))).astype(BF16)
                state[p] = gs_ref[p] * st + bd_ref[...] * _dot_tn(kd, vb)

    ch = lambda w: pl.BlockSpec((tm, w), lambda n: (n, 0))
    const2 = lambda a: pl.BlockSpec(a.shape, lambda n: (0, 0))
    const3 = lambda a: pl.BlockSpec(a.shape, lambda n: (0, 0, 0))
    return pl.pallas_call(
        body,
        name="ret_fwd",
        grid=(n_ch // cps,),
        in_specs=[ch(d_qk), ch(d_qk), ch(d_ret), ch(d_ret), const2(gain), const3(dmask), const2(qdec), const2(kdec),
                  const3(g_state), const2(block_diag)],
        out_specs=[ch(d_ret), ch(d_ret),
                   pl.BlockSpec((cps, n_pairs, 2 * QK_DIM, 2 * LANES), lambda n: (n, 0, 0, 0))],
        out_shape=[jax.ShapeDtypeStruct((tp, d_ret), F32), jax.ShapeDtypeStruct((tp, d_ret), BF16),
                   jax.ShapeDtypeStruct((n_ch, n_pairs, 2 * QK_DIM, 2 * LANES), BF16)],
        scratch_shapes=[pltpu.VMEM((n_pairs, 2 * QK_DIM, 2 * LANES), F32)],
        compiler_params=pltpu.CompilerParams(dimension_semantics=("arbitrary",), vmem_limit_bytes=VMEM_LIMIT),
    )(q, k, v, rg, gain, dmask, qdec, kdec, g_state, block_diag)


def _out_proj_loss(y_lru, y_ret, hp, tgt, wo, gain_f, tm):
    tp, d = hp.shape
    w_lru = y_lru.shape[1]
    w_mix = wo.shape[0]
    nt, nb = tp // tm, tm // CHUNK

    def body(*refs):
        yl_ref, yr_ref, hp_ref = refs[:3]
        tb = refs[3:3 + nb]
        wo_ref, gf_ref = refs[3 + nb:5 + nb]
        dh2_ref, dyl_ref, dyr_ref, dwo_ref, dgf_ref, loss_ref = refs[5 + nb:]
        i = pl.program_id(0)

        @pl.when(i == 0)
        def _():
            dwo_ref[...] = jnp.zeros_like(dwo_ref)
            dgf_ref[...] = jnp.zeros_like(dgf_ref)
            loss_ref[...] = jnp.zeros_like(loss_ref)

        yl, yr = yl_ref[...], yr_ref[...]
        h2 = hp_ref[...] + _dot(yl, wo_ref[0:w_lru, :]) + _dot(yr, wo_ref[w_lru:w_mix, :])
        rinv = lax.rsqrt(jnp.mean(h2 * h2, axis=-1, keepdims=True) + EPS)
        nrm = h2 * rinv
        gf = gf_ref[...]
        tgt_v = jnp.concatenate([r[...] for r in tb], axis=0)
        row = lax.broadcasted_iota(jnp.int32, (tm, 1), 0) + i * tm
        err = jnp.where(row >= CHUNK, nrm * gf - tgt_v, 0.0)
        loss_ref[...] += 0.5 * jnp.sum(jnp.mean(err * err, axis=-1, keepdims=True))
        dout = err * (1.0 / d)
        dgf_ref[...] += jnp.sum(dout * nrm, axis=0, keepdims=True)
        dn = dout * gf
        dh2 = rinv * (dn - nrm * jnp.mean(dn * nrm, axis=-1, keepdims=True))
        dh2_ref[...] = dh2
        dh2b = dh2.astype(BF16)
        dyl_ref[...] = _dot_nt(dh2b, wo_ref[0:w_lru, :])
        dyr_ref[...] = _dot_nt(dh2b, wo_ref[w_lru:w_mix, :])
        dwo_ref[0:w_lru, :] += _dot_tn(yl, dh2b)
        dwo_ref[w_lru:w_mix, :] += _dot_tn(yr, dh2b)

    tile = lambda w: pl.BlockSpec((tm, w), lambda i: (i, 0))
    t_specs = [pl.BlockSpec((CHUNK, d), functools.partial(lambda i, b: (jnp.maximum(i * nb + b - 1, 0), 0), b=b))
               for b in range(nb)]
    return pl.pallas_call(
        body,
        name="out_proj_loss",
        grid=(nt,),
        in_specs=[tile(w_lru), tile(w_mix - w_lru), tile(d)] + t_specs +
                 [pl.BlockSpec(wo.shape, lambda i: (0, 0)), pl.BlockSpec(gain_f.shape, lambda i: (0, 0))],
        out_specs=[tile(d), tile(w_lru), tile(w_mix - w_lru), pl.BlockSpec(wo.shape, lambda i: (0, 0)),
                   pl.BlockSpec((1, d), lambda i: (0, 0)), pl.BlockSpec((SUBLANES, LANES), lambda i: (0, 0))],
        out_shape=[jax.ShapeDtypeStruct((tp, d), F32), jax.ShapeDtypeStruct((tp, w_lru), F32),
                   jax.ShapeDtypeStruct((tp, w_mix - w_lru), F32), jax.ShapeDtypeStruct(wo.shape, F32),
                   jax.ShapeDtypeStruct((1, d), F32), jax.ShapeDtypeStruct((SUBLANES, LANES), F32)],
        compiler_params=pltpu.CompilerParams(dimension_semantics=("arbitrary",), vmem_limit_bytes=VMEM_LIMIT),
    )(y_lru, y_ret, hp, *([tgt] * nb), wo, gain_f)


def _ret_bwd(q, k, v, rg, o, rprev, dy, gain, cos_t, sin_t, tables, tm, ride=None):
    tp, d_qk = q.shape
    d_ret = v.shape[1]
    n_ch = tp // CHUNK
    cps = tm // CHUNK
    n_pairs = HEADS // 2
    dmask, qdec, kdec, g_state, block_diag = tables

    def body(q_ref, k_ref, v_ref, rg_ref, o_ref, rp_ref, dy_ref, gain_ref, cos_ref, sin_ref,
             dm_ref, qd_ref, kd_ref, gs_ref, bd_ref, dq_ref, dk_ref, dv_ref, drg_ref, dgain_ref, dstate):
        n = pl.program_id(0)

        @pl.when(n == 0)
        def _():
            dstate[...] = jnp.zeros_like(dstate)
            dgain_ref[...] = jnp.zeros_like(dgain_ref)

        lane = lax.broadcasted_iota(jnp.int32, (CHUNK, LANES), 1)
        for ci in reversed(range(cps)):
            rs = slice(ci * CHUNK, (ci + 1) * CHUNK)
            dq_parts, dk_parts = [], []
            for p in range(n_pairs):
                qs = slice(p * LANES, (p + 1) * LANES)
                vs = slice(p * 2 * LANES, (p + 1) * 2 * LANES)
                do_parts = []
                for e in range(2):
                    hd = 2 * p + e
                    hs = slice(hd * LANES, (hd + 1) * LANES)
                    xhat, rstd = _head_norm(o_ref[rs, hs])
                    g = rg_ref[rs, hs]
                    sg = _sigmoid(g)
                    dyh = dy_ref[rs, hs]
                    gn = gain_ref[:, hs]
                    d_on = dyh * (g * sg)
                    drg_ref[rs, hs] = (dyh * (xhat * gn) * (sg * (1.0 + g * (1.0 - sg)))).astype(BF16)
                    dgain_ref[:, hs] += jnp.sum(d_on * xhat, axis=0, keepdims=True)
                    dxh = d_on * gn
                    do_parts.append(rstd * (dxh - jnp.mean(dxh, axis=-1, keepdims=True)
                                            - xhat * jnp.mean(dxh * xhat, axis=-1, keepdims=True)))
                do_b = jnp.concatenate(do_parts, axis=1).astype(BF16)
                qp, kp = q_ref[rs, qs], k_ref[rs, qs]
                vb = v_ref[rs, vs].astype(BF16)
                kb = kp.astype(BF16)
                qd = (qp * qd_ref[:, qs]).astype(BF16)
                kd = (kp * kd_ref[:, qs]).astype(BF16)
                dst = dstate[p]
                dst_b = dst.astype(BF16)
                dqp = _dot_nt(do_b, rp_ref[ci, p]) * qd_ref[:, qs]
                dkp = _dot_nt(vb, dst_b) * kd_ref[:, qs]
                dvp = _dot(kd, dst_b)
                dv_parts = []
                for e in range(2):
                    hd = 2 * p + e
                    es = slice(e * LANES, (e + 1) * LANES)
                    mine = (lane // QK_DIM) == e
                    qm = jnp.where(mine, qp, 0.0).astype(BF16)
                    km = jnp.where(mine, kp, 0.0).astype(BF16)
                    dm = dm_ref[hd]
                    s = (_dot_nt(qm, kb) * dm).astype(BF16)
                    ds = (_dot_nt(do_b[:, es], vb[:, es]) * dm).astype(BF16)
                    dv_parts.append(dvp[:, es] + _dot_tn(s, do_b[:, es]))
                    dqp = dqp + _dot(ds, km)
                    dkp = dkp + _dot_tn(ds, qm)
                dv_ref[rs, vs] = jnp.concatenate(dv_parts, axis=1).astype(BF16)
                dstate[p] = gs_ref[p] * dst + bd_ref[...] * _dot_tn(qd, do_b)
                dq_parts.append(dqp)
                dk_parts.append(dkp)
            cos = _tile_lanes(cos_ref[rs, :], d_qk // LANES)
            sin = _tile_lanes(sin_ref[rs, :], d_qk // LANES)
            dq = jnp.concatenate(dq_parts, axis=1)
            dk = jnp.concatenate(dk_parts, axis=1) * (QK_DIM ** -0.5)
            dq_ref[rs, :] = (dq * cos + _rot_partner(dq * sin)).astype(BF16)
            dk_ref[rs, :] = (dk * cos + _rot_partner(dk * sin)).astype(BF16)

    last = n_ch // cps - 1
    ch = lambda w: pl.BlockSpec((tm, w), lambda n: (last - n, 0))
    const2 = lambda a: pl.BlockSpec(a.shape, lambda n: (0, 0))
    const3 = lambda a: pl.BlockSpec(a.shape, lambda n: (0, 0, 0))
    return _hosted_call(
        body, ride, n_ch // cps,
        name="ret_bwd",
        in_specs=[ch(d_qk), ch(d_qk), ch(d_ret), ch(d_ret), ch(d_ret),
                  pl.BlockSpec((cps, n_pairs, 2 * QK_DIM, 2 * LANES), lambda n: (last - n, 0, 0, 0)),
                  ch(d_ret), const2(gain), ch(LANES), ch(LANES),
                  const3(dmask), const2(qdec), const2(kdec), const3(g_state), const2(block_diag)],
        out_specs=[ch(d_qk), ch(d_qk), ch(d_ret), ch(d_ret), pl.BlockSpec((1, d_ret), lambda n: (0, 0))],
        out_shape=[jax.ShapeDtypeStruct((tp, d_qk), BF16), jax.ShapeDtypeStruct((tp, d_qk), BF16),
                   jax.ShapeDtypeStruct((tp, d_ret), BF16), jax.ShapeDtypeStruct((tp, d_ret), BF16),
                   jax.ShapeDtypeStruct((1, d_ret), F32)],
        scratch_shapes=[pltpu.VMEM((n_pairs, 2 * QK_DIM, 2 * LANES), F32)],
        args=(q, k, v, rg, o, rprev, dy, gain, cos_t, sin_t, dmask, qdec, kdec, g_state, block_diag),
    )


def _lru_bwd(lx, lg, hl, dy, saved, cw, wr, wi, lam, tm, ride=None):
    tp, w = lx.shape
    nt = tp // tm
    per8 = tm // SUBLANES
    n_heads = wr.shape[0]

    def body(lx_ref, lg_ref, hl_ref, hlp_ref, dy_ref, xc_ref, r_ref, ig_ref, a_ref, beta_ref, w4_ref,
             cw_ref, wr_ref, wi_ref, lam_ref,
             dlx_ref, dlg_ref, dcw_ref, dcb_ref, dwr_ref, dbr_ref, dwi_ref, dbi_ref, dlam_ref,
             g_s, dh_s, b_s, p_s, carry, dxc_next, a_next):
        i = pl.program_id(0)
        first_tile = i == nt - 1

        @pl.when(i == 0)
        def _():
            carry[...] = jnp.zeros_like(carry)
            dxc_next[...] = jnp.zeros_like(dxc_next)
            a_next[...] = jnp.zeros_like(a_next)
            for r in (dcw_ref, dcb_ref, dwr_ref, dbr_ref, dwi_ref, dbi_ref, dlam_ref):
                r[...] = jnp.zeros_like(r)

        lxv = lx_ref[...]
        a, beta, r, ig, xc = a_ref[...], beta_ref[...], r_ref[...], ig_ref[...], xc_ref[...]
        g = lg_ref[...]
        sg = _sigmoid(g)
        dyv = dy_ref[...]
        hlv = hl_ref[...]
        dlg_ref[...] = (dyv * hlv * (sg * (1.0 + g * (1.0 - sg)))).astype(BF16)
        _to_groups(g_s, dyv * (g * sg))
        _to_groups(b_s, _shift_up(a, a_next[...], 1))
        carry[0:1, :] = _segment_scan(b_s, g_s, dh_s, p_s, carry[0:1, :], tm, reverse=True)
        a_next[...] = a[0:SUBLANES]
        dh = _from_groups(dh_s)
        hprev = _shift_down(hlv, jnp.where(first_tile, 0.0, hlp_ref[...]), 1)
        row = lax.broadcasted_iota(jnp.int32, (tm, 1), 0) + (nt - 1 - i) * tm
        du = jnp.where(row >= PAD_ROWS, dh, 0.0)
        da = dh * hprev
        dbeta = du * ig * xc
        d_ig = du * beta * xc
        dxc = du * beta * ig
        dloga = da * a - dbeta * w4_ref[...]
        lam_v = lam_ref[...]
        dlam_ref[...] += jnp.sum(dloga * r, axis=0, keepdims=True) * (LRU_C * _sigmoid(-lam_v))
        dpr = (dloga * (-LRU_C * _softplus_neg(lam_v))) * r * (1.0 - r)
        dpi = d_ig * ig * (1.0 - ig)
        dbr_ref[...] += jnp.sum(dpr, axis=0, keepdims=True)
        dbi_ref[...] += jnp.sum(dpi, axis=0, keepdims=True)
        dxc_parts = []
        for hd in range(n_heads):
            hs = slice(hd * LANES, (hd + 1) * LANES)
            xh = xc[:, hs].astype(BF16)
            dprh = dpr[:, hs].astype(BF16)
            dpih = dpi[:, hs].astype(BF16)
            dwr_ref[hd] += _dot_tn(xh, dprh)
            dwi_ref[hd] += _dot_tn(xh, dpih)
            dxc_parts.append(_dot_nt(dprh, wr_ref[hd].astype(BF16)) + _dot_nt(dpih, wi_ref[hd].astype(BF16)))
        dxc = dxc + jnp.concatenate(dxc_parts, axis=1)
        nxt = dxc_next[...]
        up1, up2, up3 = _shift_up(dxc, nxt, 1), _shift_up(dxc, nxt, 2), _shift_up(dxc, nxt, 3)
        dlx = dxc * cw_ref[3:4, :]
        dlx = dlx + up1 * cw_ref[2:3, :]
        dlx = dlx + up2 * cw_ref[1:2, :]
        dlx = dlx + up3 * cw_ref[0:1, :]
        dlx_ref[...] = dlx.astype(BF16)
        dxc_next[...] = dxc[0:SUBLANES]
        dcb_ref[...] += jnp.sum(dxc, axis=0, keepdims=True)
        dcw_ref[0:1, :] += jnp.sum(up3 * lxv, axis=0, keepdims=True)
        dcw_ref[1:2, :] += jnp.sum(up2 * lxv, axis=0, keepdims=True)
        dcw_ref[2:3, :] += jnp.sum(up1 * lxv, axis=0, keepdims=True)
        dcw_ref[3:4, :] += jnp.sum(dxc * lxv, axis=0, keepdims=True)

    last = nt - 1
    tile = pl.BlockSpec((tm, w), lambda i: (last - i, 0))
    prev = pl.BlockSpec((SUBLANES, w), lambda i: (jnp.maximum((last - i) * per8 - 1, 0), 0))
    vec = pl.BlockSpec((1, w), lambda i: (0, 0))
    mat = pl.BlockSpec(wr.shape, lambda i: (0, 0, 0))
    cwb = pl.BlockSpec(cw.shape, lambda i: (0, 0))
    return _hosted_call(
        body, ride, nt,
        name="lru_bwd",
        in_specs=[tile, tile, tile, prev, tile] + [tile] * 6 + [cwb, mat, mat, vec],
        out_specs=[tile, tile, cwb, vec, mat, vec, mat, vec, vec],
        out_shape=[jax.ShapeDtypeStruct((tp, w), BF16), jax.ShapeDtypeStruct((tp, w), BF16),
                   jax.ShapeDtypeStruct(cw.shape, F32), jax.ShapeDtypeStruct((1, w), F32),
                   jax.ShapeDtypeStruct(wr.shape, F32), jax.ShapeDtypeStruct((1, w), F32),
                   jax.ShapeDtypeStruct(wr.shape, F32), jax.ShapeDtypeStruct((1, w), F32),
                   jax.ShapeDtypeStruct((1, w), F32)],
        scratch_shapes=[pltpu.VMEM((w // LANES, tm, LANES), F32)] * 4 + [pltpu.VMEM((SUBLANES, w), F32)] * 3,
        args=(lx, lg, hl, hl, dy, *saved, cw, wr, wi, lam),
    )


def _in_proj_dw(dparts, hp, gain, wg_shape, tm, ride=None):
    tp, d = hp.shape
    nt = tp // tm
    widths = [p.shape[1] for p in dparts]
    segs = _proj_segments(widths[0], widths[2], widths[4], wg_shape[2])

    def body(*refs):
        dp = refs[:6]
        hp_ref, g_ref, dwg_ref, acc, sem = refs[6:]
        i = pl.program_id(0)

        @pl.when(i == 0)
        def _():
            acc[...] = jnp.zeros_like(acc)

        h = hp_ref[...]
        rinv = lax.rsqrt(jnp.mean(h * h, axis=-1, keepdims=True) + EPS)
        u = ((h * rinv) * g_ref[...]).astype(BF16)
        for p_ref, parts in zip(dp, segs):
            for jj, inner, off, take in parts:
                acc[jj, :, inner:inner + take] += _dot_tn(u, p_ref[:, off:off + take])

        @pl.when(i == nt - 1)
        def _():
            cp = pltpu.make_async_copy(acc, dwg_ref, sem)
            cp.start()
            cp.wait()

    tile = lambda w: pl.BlockSpec((tm, w), lambda i: (i, 0))
    outs, rides = _hosted_call(
        body, ride, nt,
        name="in_proj_dw",
        in_specs=[tile(w) for w in widths] + [tile(d), pl.BlockSpec(gain.shape, lambda i: (0, 0))],
        out_specs=[ANY_SPEC],
        out_shape=[jax.ShapeDtypeStruct(wg_shape, F32)],
        scratch_shapes=[pltpu.VMEM(wg_shape, F32), pltpu.SemaphoreType.DMA],
        args=(*dparts, hp, gain),
    )
    return outs[0], rides


def _in_proj_dx(dparts, hp, dh2, gain, wg, s_len, tm, ride=None):
    tp, d = hp.shape
    nt = tp // tm
    widths = [p.shape[1] for p in dparts]
    segs = _proj_segments(widths[0], widths[2], widths[4], wg.shape[2])

    def body(*refs):
        dp = refs[:6]
        hp_ref, dh2_ref, g_ref, w_ref = refs[6:10]
        gx_ref, dmeta_ref, dg_ref = refs[10:13]
        stage, sems = refs[13:]
        i = pl.program_id(0)

        @pl.when(i == 0)
        def _():
            dg_ref[...] = jnp.zeros_like(dg_ref)

        h = hp_ref[...]
        rinv = lax.rsqrt(jnp.mean(h * h, axis=-1, keepdims=True) + EPS)
        nrm = h * rinv
        gv = g_ref[...]
        du = jnp.zeros((tm, d), F32)
        for p_ref, parts in zip(dp, segs):
            for jj, inner, off, take in parts:
                du = du + _dot_nt(p_ref[:, off:off + take], w_ref[jj, :, inner:inner + take])
        dg_ref[...] += jnp.sum(du * nrm, axis=0, keepdims=True)
        dn = du * gv
        dh = dh2_ref[...] + rinv * (dn - nrm * jnp.mean(dn * nrm, axis=-1, keepdims=True))

        def first_copy():
            return pltpu.make_async_copy(stage.at[0, pl.ds(CHUNK, tm - CHUNK), :],
                                         gx_ref.at[pl.ds(0, tm - CHUNK), :], sems.at[0])

        def tile_copy(slot, start):
            return pltpu.make_async_copy(stage.at[slot], gx_ref.at[pl.ds(start, tm), :], sems.at[slot])

        @pl.when(i == 0)
        def _():
            dmeta_ref[...] = dh[PAD_ROWS:CHUNK]
            stage[0] = dh
            first_copy().start()

        @pl.when(i > 0)
        def _():
            slot = 1 + i % 2

            @pl.when(i >= 3)
            def _():
                tile_copy(slot, 0).wait()

            stage[slot] = dh
            tile_copy(slot, pl.multiple_of(i * tm - CHUNK, CHUNK)).start()

        @pl.when(i == nt - 1)
        def _():
            first_copy().wait()
            for step in (nt - 2, nt - 1):
                if step >= 1:
                    tile_copy(1 + step % 2, 0).wait()

    tile = lambda w: pl.BlockSpec((tm, w), lambda i: (i, 0))
    return _hosted_call(
        body, ride, nt,
        name="in_proj_dx",
        in_specs=[tile(w) for w in widths] + [tile(d), tile(d), pl.BlockSpec(gain.shape, lambda i: (0, 0)),
                                              pl.BlockSpec(wg.shape, lambda i: (0, 0, 0))],
        out_specs=[ANY_SPEC, pl.BlockSpec((N_META, d), lambda i: (0, 0)), pl.BlockSpec((1, d), lambda i: (0, 0))],
        out_shape=[jax.ShapeDtypeStruct((s_len, d), F32), jax.ShapeDtypeStruct((N_META, d), F32),
                   jax.ShapeDtypeStruct((1, d), F32)],
        scratch_shapes=[pltpu.VMEM((3, tm, d), F32), pltpu.SemaphoreType.DMA((3,))],
        args=(*dparts, hp, dh2, gain, wg),
    )


def _pair_sum(buf, recv, c_arr, tr, name):
    _, rows, cols = buf.shape

    def body(c_ref, mine_ref, got_ref, out_ref):
        out_ref[...] = (mine_ref[...] + got_ref[...]).astype(BF16)

    grid_spec = pltpu.PrefetchScalarGridSpec(
        num_scalar_prefetch=1,
        grid=(N_CHIPS, rows // tr),
        in_specs=[pl.BlockSpec((1, tr, cols), lambda jj, r, c_ref: (2 * jj + c_ref[0], r, 0)),
                  pl.BlockSpec((1, tr, cols), lambda jj, r, c_ref: (jj, r, 0))],
        out_specs=pl.BlockSpec((1, tr, cols), lambda jj, r, c_ref: (jj, r, 0)),
    )
    return pl.pallas_call(
        body,
        name=name,
        grid_spec=grid_spec,
        out_shape=jax.ShapeDtypeStruct((N_CHIPS, rows, cols), BF16),
    )(c_arr, buf, recv)


def _chip_sum(mine, got, j_arr, tr, name, loss_part=None):
    _, rows, cols = got.shape
    extra = [] if loss_part is None else [loss_part]

    def body(j_ref, mine_ref, got_ref, *rest):
        out_ref = rest[-1]
        j = j_ref[0]
        acc = None
        for jj in range(N_CHIPS):
            term = jnp.where(j == jj, mine_ref[0], got_ref[jj]).astype(F32)
            acc = term if acc is None else acc + term
        out_ref[...] = acc
        if loss_part is not None:
            out_ref[ROW_LOSS:ROW_LOSS + 1, :] = rest[0][0:1, :]

    grid_spec = pltpu.PrefetchScalarGridSpec(
        num_scalar_prefetch=1,
        grid=(rows // tr,),
        in_specs=[pl.BlockSpec((1, tr, cols), lambda r, j_ref: (j_ref[0], r, 0)),
                  pl.BlockSpec((N_CHIPS, tr, cols), lambda r, j_ref: (0, r, 0))] +
                 [pl.BlockSpec(e.shape, lambda r, j_ref: (0, 0)) for e in extra],
        out_specs=pl.BlockSpec((tr, cols), lambda r, j_ref: (r, 0)),
    )
    return pl.pallas_call(
        body,
        name=name,
        grid_spec=grid_spec,
        out_shape=jax.ShapeDtypeStruct((rows, cols), F32),
    )(j_arr, mine, got, *extra)


def _finish_exchange(f_in, f_small):
    def body(fin_ref, fs_ref, rin_ref, os_ref, send_sems, recv_sems, local_sem):
        x, y, c, chips = _position()
        j = 2 * x + y
        me = 2 * j + c
        sibling = (x, y, 1 - c)
        local = pltpu.make_async_copy(fs_ref, os_ref.at[me], local_sem)
        local.start()

        def copy(k, src, dst, to):
            return _remote(src, dst, send_sems, recv_sems, k, to)

        first = [copy(0, fin_ref, rin_ref, sibling), copy(1, fs_ref, os_ref.at[me], sibling)]
        first += [copy(2 + k, fs_ref, os_ref.at[me], (cx, cy, c)) for k, (cx, cy) in enumerate(chips)]
        for cp in first:
            cp.start()
        passed = []
        for k, (cx, cy) in enumerate(chips):
            unit = 2 * (2 * cx + cy) + c
            copy(2 + k, fs_ref, os_ref.at[unit], sibling).wait_recv()
            fwd = copy(5 + k, os_ref.at[unit], os_ref.at[unit], sibling)
            fwd.start()
            passed.append(fwd)
        copy(0, fin_ref, rin_ref, sibling).wait_recv()
        copy(1, fs_ref, os_ref.at[2 * j + 1 - c], sibling).wait_recv()
        for k, (cx, cy) in enumerate(chips):
            unit = 2 * (2 * cx + cy) + 1 - c
            copy(5 + k, fs_ref, os_ref.at[unit], sibling).wait_recv()
        for cp in first + passed:
            cp.wait_send()
        local.wait()

    return pl.pallas_call(
        body,
        name="grad_finish_exchange",
        in_specs=[ANY_SPEC] * 2,
        out_specs=[ANY_SPEC] * 2,
        out_shape=[jax.ShapeDtypeStruct(f_in.shape, F32), jax.ShapeDtypeStruct((N_DEV,) + f_small.shape, F32)],
        scratch_shapes=[pltpu.SemaphoreType.DMA((8,)), pltpu.SemaphoreType.DMA((8,)), pltpu.SemaphoreType.DMA],
    )(f_in, f_small)


def _adamw_math(w, g, m, v):
    m = ADAM_B1 * m + (1.0 - ADAM_B1) * g
    v = ADAM_B2 * v + (1.0 - ADAM_B2) * (g * g)
    m_hat = m / (1.0 - ADAM_B1 ** ADAM_STEP)
    v_hat = v / (1.0 - ADAM_B2 ** ADAM_STEP)
    delta = -ADAM_LR * (m_hat / (jnp.sqrt(v_hat) + ADAM_EPS) + ADAM_WD * w)
    return delta, m, v


def _adamw_big(w, g_mine, g_sib, m, v, c_arr, tr, name):
    rows, cols = w.shape
    half = rows // 2
    per = half // tr

    def body(c_ref, w_ref, gm_ref, gs_ref, m_ref, v_ref, g_ref, d_ref, mo_ref, vo_ref):
        g = jnp.where(pl.program_id(0) == c_ref[0], gm_ref[...], gs_ref[...])
        g_ref[...] = g
        d_ref[...], mo_ref[...], vo_ref[...] = _adamw_math(w_ref[...], g, m_ref[...], v_ref[...])

    full = pl.BlockSpec((tr, cols), lambda h, r, c_ref: (h * per + r, 0))
    unit = pl.BlockSpec((tr, cols), lambda h, r, c_ref: (r, 0))
    grid_spec = pltpu.PrefetchScalarGridSpec(
        num_scalar_prefetch=1,
        grid=(2, per),
        in_specs=[full, unit, unit, full, full],
        out_specs=[full] * 4,
    )
    return pl.pallas_call(
        body,
        name=name,
        grid_spec=grid_spec,
        out_shape=[jax.ShapeDtypeStruct(w.shape, F32)] * 4,
    )(c_arr, w, g_mine, g_sib, m, v)


def _adamw_small(j_arr, packed, params):
    names = list(params)
    n = len(names)
    vec_names = ["norm_gain", "conv_b", "b_rg", "b_ig", "lru_lambda", "ret_norm_gain", "final_norm_gain"]

    def body(j_ref, pk_ref, *refs):
        ins = refs[:3 * n]
        outs = refs[3 * n:]
        j = j_ref[0]

        def shard(row, rows):
            return jnp.concatenate([pk_ref[2 * j, row:row + rows, :], pk_ref[2 * j + 1, row:row + rows, :]], axis=1)

        def tail_sum(unit, row, rows):
            start = pl.multiple_of(UNIT_ROWS + TAIL_ROWS * unit + row, SUBLANES)
            total = pk_ref[0, pl.ds(start, rows), :]
            for dev in range(1, N_DEV):
                total = total + pk_ref[dev, pl.ds(start, rows), :]
            return total

        for idx, name in enumerate(names):
            if name == "w_rg":
                g = pk_ref[:, ROW_WR:ROW_WR + LANES, :]
            elif name == "w_ig":
                g = pk_ref[:, ROW_WI:ROW_WI + LANES, :]
            elif name == "meta_tokens":
                g = jnp.concatenate([tail_sum(2 * j, 0, N_META), tail_sum(2 * j + 1, 0, N_META)], axis=1)
            elif name == "norm_gain":
                g = jnp.concatenate([tail_sum(u, N_META, SUBLANES)[0:1] for u in range(N_DEV)], axis=1)
            elif name == "conv_w":
                g = shard(ROW_CONV, 4)
            else:
                row = ROW_VEC + vec_names.index(name)
                g = jnp.concatenate([pk_ref[u, row:row + 1, :] for u in range(N_DEV)], axis=1)
            w_ref, m_ref, v_ref = ins[3 * idx:3 * idx + 3]
            delta, m, v = _adamw_math(w_ref[...], g, m_ref[...], v_ref[...])
            g_ref, d_ref, mo_ref, vo_ref = outs[4 * idx:4 * idx + 4]
            g_ref[...], d_ref[...], mo_ref[...], vo_ref[...] = g, delta, m, v
        total = pk_ref[0, ROW_LOSS:ROW_LOSS + 1, :]
        for u in range(1, N_DEV):
            total = total + pk_ref[u, ROW_LOSS:ROW_LOSS + 1, :]
        outs[4 * n][...] = jnp.broadcast_to(total, (SUBLANES, LANES))

    flat_in, out_shape = [], []
    for name in names:
        w, m, v = params[name]
        flat_in += [w, m, v]
        out_shape += [jax.ShapeDtypeStruct(w.shape, F32)] * 4
    out_shape.append(jax.ShapeDtypeStruct((SUBLANES, LANES), F32))
    res = pl.pallas_call(
        body,
        name="adamw_small",
        in_specs=[SMEM_SPEC, VMEM_SPEC] + [VMEM_SPEC] * (3 * n),
        out_specs=[VMEM_SPEC] * (4 * n + 1),
        out_shape=out_shape,
    )(j_arr, packed, *flat_in)
    return {name: tuple(res[4 * idx:4 * idx + 4]) for idx, name in enumerate(names)}, res[4 * n][0, 0]


def _units(a):
    rows = a.shape[0]
    return jnp.transpose(a.reshape(rows, N_DEV, LANES), (1, 0, 2))


def kernel(x, meta_tokens, norm_gain, w_in, conv_w, conv_b, w_rg, b_rg, w_ig, b_ig, lru_lambda, ret_norm_gain, w_out, final_norm_gain, loss_target, m_meta_tokens, m_norm_gain, m_w_in, m_conv_w, m_conv_b, m_w_rg, m_b_rg, m_w_ig, m_b_ig, m_lru_lambda, m_ret_norm_gain, m_w_out, m_final_norm_gain, v_meta_tokens, v_norm_gain, v_w_in, v_conv_w, v_conv_b, v_w_rg, v_b_rg, v_w_ig, v_b_ig, v_lru_lambda, v_ret_norm_gain, v_w_out, v_final_norm_gain):
    s_len, d = x.shape[1], x.shape[2]
    d_lru = w_rg.shape[1] * w_rg.shape[2]
    d_ret = ret_norm_gain.shape[1]
    d_qk = HEADS * QK_DIM
    tp = s_len + CHUNK
    tm = TOKEN_TILE
    assert tp % tm == 0 and d_lru == HEADS * LANES and d_ret == HEADS * LANES
    ax, ay, ac = lax.axis_index("x"), lax.axis_index("y"), lax.axis_index("c")
    c_arr = jnp.reshape(ac, (1,)).astype(jnp.int32)
    j_arr = jnp.reshape(2 * ax + ay, (1,)).astype(jnp.int32)

    small = jnp.concatenate([meta_tokens, conv_w[0], jnp.zeros((4, meta_tokens.shape[1]), F32)], axis=0)
    wg, sg = _gather_weights(w_in[0], small)
    cols = sg.shape[2]
    meta_full = jnp.transpose(sg[:, :N_META, :], (1, 0, 2)).reshape(N_META, N_CHIPS * cols)
    cw_full = jnp.transpose(sg[:, N_META:N_META + 4, :], (1, 0, 2)).reshape(4, N_CHIPS * cols)
    cw8 = jnp.concatenate([cw_full, jnp.zeros((4, cw_full.shape[1]), F32)], axis=0)

    half = QK_DIM // 2
    inv = ROPE_BASE ** (-jnp.arange(half, dtype=F32) / half)
    pos = (jnp.arange(tp) - PAD_ROWS).astype(F32)
    ang = pos[:, None] * inv[None, :]
    cos_t = jnp.tile(jnp.cos(ang), (1, LANES // half))
    sign = jnp.where((jnp.arange(LANES) % QK_DIM) < half, -1.0, 1.0).astype(F32)
    sin_t = jnp.tile(jnp.sin(ang), (1, LANES // half)) * sign[None, :]
    tables = _ret_tables()
    gain_f = final_norm_gain.reshape(1, d)

    hp, lx, lg, q, k, v, rg, wo4 = _in_proj(x[0], meta_full, norm_gain, wg, cos_t, sin_t, w_out[0], tm,
                                            d_lru, d_qk, d_ret)
    wo = wo4.reshape(N_CHIPS * wo4.shape[1], wo4.shape[2])
    hl, y_lru, *lru_saved = _lru_fwd(lx, lg, cw8, conv_b, w_rg[0], b_rg, w_ig[0], b_ig, lru_lambda, tm)
    o, y_ret, rprev = _ret_fwd(q, k, v, rg, ret_norm_gain, tables, tm)
    dh2, dy_lru, dy_ret, dwo, dgf, loss_acc = _out_proj_loss(y_lru, y_ret, hp, loss_target[0], wo, gain_f, tm)

    g_out = dwo.reshape(N_DEV, dwo.shape[0] // N_DEV, dwo.shape[1])
    (dq, dk, dv, drg, dgain), (r_out,) = _ret_bwd(q, k, v, rg, o, rprev, dy_ret, ret_norm_gain, cos_t, sin_t, tables,
                                                 tm, ride=_pair_ride([g_out]))
    q_out = _pair_sum(g_out, r_out, c_arr, 128, "grad_pair_sum_out")
    (dlx, dlg, dcw, dcb, dwr, dbr, dwi, dbi, dlam), (e_out,) = _lru_bwd(
        lx, lg, hl, dy_lru, lru_saved, cw8, w_rg[0], w_ig[0], lru_lambda, tm, ride=_chip_ride([q_out]))
    f_out = _chip_sum(q_out, e_out, j_arr, 128, "grad_chip_sum_out")
    zero_row = jnp.zeros((1, d), F32)
    vecs = [zero_row, dcb, dbr, dbi, dlam, dgain, dgf]
    g_small = jnp.concatenate([dwr, dwi, jnp.zeros((N_DEV, N_META, LANES), F32), _units(dcw[0:4])]
                              + [_units(a) for a in vecs]
                              + [jnp.zeros((N_DEV, UNIT_ROWS - ROW_VEC - N_VEC, LANES), F32)], axis=1)
    dparts = [dlx, dlg, dq, dk, dv, drg]
    dwg, (s_out, r_small) = _in_proj_dw(dparts, hp, norm_gain, wg.shape, tm,
                                        ride=_join_rides(_sibling_ride([f_out]), _pair_ride([g_small])))
    g_in = dwg.reshape(N_DEV, dwg.shape[1] // 2, dwg.shape[2])
    (r_in,) = _exchange_call(_pair_ride([g_in]), "grad_pair_exchange")
    q_in = _pair_sum(g_in, r_in, c_arr, 128, "grad_pair_sum_in")
    q_small = _pair_sum(g_small, r_small, c_arr, UNIT_ROWS, "grad_pair_sum_small")
    (grad_x, dmeta, dg1), (e_in, e_small) = _in_proj_dx(dparts, hp, dh2, norm_gain, wg, s_len, tm,
                                                        ride=_chip_ride([q_in, q_small]))
    f_in = _chip_sum(q_in, e_in, j_arr, 128, "grad_chip_sum_in")
    f_small = _chip_sum(q_small, e_small, j_arr, UNIT_ROWS, "grad_chip_sum_small", loss_part=loss_acc)
    tail = jnp.concatenate([_units(dmeta), _units(dg1), jnp.zeros((N_DEV, TAIL_ROWS - N_META - 1, LANES), F32)],
                           axis=1).reshape(N_DEV * TAIL_ROWS, LANES)
    s_in, o_small = _finish_exchange(f_in, jnp.concatenate([f_small, tail], axis=0))

    res_in = _adamw_big(w_in[0], f_in, s_in, m_w_in[0], v_w_in[0], c_arr, 256, "adamw_w_in")
    res_out = _adamw_big(w_out[0], f_out, s_out, m_w_out[0], v_w_out[0], c_arr, 256, "adamw_w_out")
    small_params = {
        "meta_tokens": (meta_tokens, m_meta_tokens, v_meta_tokens),
        "norm_gain": (norm_gain, m_norm_gain, v_norm_gain),
        "conv_w": (conv_w[0], m_conv_w[0], v_conv_w[0]),
        "conv_b": (conv_b, m_conv_b, v_conv_b),
        "w_rg": (w_rg[0], m_w_rg[0], v_w_rg[0]),
        "b_rg": (b_rg, m_b_rg, v_b_rg),
        "w_ig": (w_ig[0], m_w_ig[0], v_w_ig[0]),
        "b_ig": (b_ig, m_b_ig, v_b_ig),
        "lru_lambda": (lru_lambda, m_lru_lambda, v_lru_lambda),
        "ret_norm_gain": (ret_norm_gain, m_ret_norm_gain, v_ret_norm_gain),
        "final_norm_gain": (gain_f, m_final_norm_gain.reshape(1, d), v_final_norm_gain.reshape(1, d)),
    }
    res, loss = _adamw_small(j_arr, o_small, small_params)
    res["w_in"] = tuple(res_in)
    res["w_out"] = tuple(res_out)

    order = ["meta_tokens", "norm_gain", "w_in", "conv_w", "conv_b", "w_rg", "b_rg", "w_ig", "b_ig", "lru_lambda",
             "ret_norm_gain", "w_out", "final_norm_gain"]
    shapes = {"w_in": w_in.shape, "conv_w": conv_w.shape, "w_rg": w_rg.shape, "w_ig": w_ig.shape,
              "w_out": w_out.shape, "final_norm_gain": final_norm_gain.shape}
    outs = [loss, grad_x.reshape(x.shape)]
    for kind in range(4):
        for name in order:
            a = res[name][kind]
            outs.append(a.reshape(shapes[name]) if name in shapes else a)
    return tuple(outs)
```

```python
import functools

import jax
import jax.numpy as jnp
from jax import lax
from jax.experimental import pallas as pl
from jax.experimental.pallas import tpu as pltpu

F32 = jnp.float32
BF16 = jnp.bfloat16

N_META = 16
CHUNK = 128
PAD_ROWS = CHUNK - N_META
HEADS = 8
QK_DIM = 64
LANES = 128
SUBLANES = 8
LRU_C = 8.0
EPS = 1e-6
ROPE_BASE = 10000.0
ADAM_LR = 0.001
ADAM_B1 = 0.9
ADAM_B2 = 0.999
ADAM_EPS = 1e-08
ADAM_WD = 0.01
ADAM_STEP = 10
N_CHIPS = 4
N_DEV = 8
TOKEN_TILE = 384
VMEM_LIMIT = 58 * 1024 * 1024
MESH = pl.DeviceIdType.MESH

VMEM_SPEC = pl.BlockSpec(memory_space=pltpu.VMEM)
SMEM_SPEC = pl.BlockSpec(memory_space=pltpu.SMEM)
ANY_SPEC = pl.BlockSpec(memory_space=pl.ANY)

ROW_WR, ROW_WI, ROW_META, ROW_CONV, ROW_VEC, UNIT_ROWS = 0, 128, 256, 272, 276, 288
N_VEC = 7
ROW_LOSS = ROW_VEC + N_VEC
TAIL_ROWS = 24


def _dot(a, b):
    return jnp.dot(a, b, preferred_element_type=F32)


def _dot_nt(a, b):
    return lax.dot_general(a, b, (((1,), (1,)), ((), ())), preferred_element_type=F32)


def _dot_tn(a, b):
    return lax.dot_general(a, b, (((0,), (0,)), ((), ())), preferred_element_type=F32)


def _sigmoid(x):
    return 0.5 * jnp.tanh(0.5 * x) + 0.5


def _shift_down(x, prev8, s):
    rolled = pltpu.roll(x, s, 0)
    rows = lax.broadcasted_iota(jnp.int32, (SUBLANES, x.shape[1]), 0)
    top = jnp.where(rows < s, pltpu.roll(prev8, s, 0), rolled[0:SUBLANES])
    return jnp.concatenate([top, rolled[SUBLANES:]], axis=0)


def _shift_up(x, next8, s):
    n = x.shape[0]
    rolled = pltpu.roll(x, n - s, 0)
    rows = lax.broadcasted_iota(jnp.int32, (SUBLANES, x.shape[1]), 0)
    bot = jnp.where(rows >= SUBLANES - s, pltpu.roll(next8, SUBLANES - s, 0), rolled[n - SUBLANES:n])
    return jnp.concatenate([rolled[:n - SUBLANES], bot], axis=0)


def _rot_partner(t):
    w = t.shape[1]
    lane = lax.broadcasted_iota(jnp.int32, t.shape, 1)
    first = (lane % QK_DIM) < (QK_DIM // 2)
    return jnp.where(first, pltpu.roll(t, w - QK_DIM // 2, 1), pltpu.roll(t, QK_DIM // 2, 1))


def _tile_lanes(t, reps):
    return jnp.concatenate([t] * reps, axis=1)


class _Ride:
    def __init__(self, srcs, dst_shapes, n_copies, make):
        self.srcs, self.dst_shapes, self.n_copies, self.make = list(srcs), list(dst_shapes), n_copies, make


def _join_rides(a, b):
    def make(src, dst, send_sems, recv_sems, base):
        na, da = len(a.srcs), len(a.dst_shapes)
        return (a.make(src[:na], dst[:da], send_sems, recv_sems, base)
                + b.make(src[na:], dst[da:], send_sems, recv_sems, base + a.n_copies))

    return _Ride(a.srcs + b.srcs, a.dst_shapes + b.dst_shapes, a.n_copies + b.n_copies, make)


def _position():
    x, y, c = lax.axis_index("x"), lax.axis_index("y"), lax.axis_index("c")
    return x, y, c, [(1 - x, y), (x, 1 - y), (1 - x, 1 - y)]


def _remote(src, dst, send_sems, recv_sems, k, to):
    return pltpu.make_async_remote_copy(src_ref=src, dst_ref=dst, send_sem=send_sems.at[k], recv_sem=recv_sems.at[k],
                                        device_id=to, device_id_type=MESH)


def _pair_ride(bufs):
    def make(src, dst, send_sems, recv_sems, base):
        x, y, c, _ = _position()
        return [_remote(src[b].at[2 * jj + 1 - c], dst[b].at[jj], send_sems, recv_sems, base + b * N_CHIPS + jj,
                        (x, y, 1 - c)) for b in range(len(bufs)) for jj in range(N_CHIPS)]

    shapes = [jax.ShapeDtypeStruct((N_CHIPS,) + b.shape[1:], b.dtype) for b in bufs]
    return _Ride(bufs, shapes, N_CHIPS * len(bufs), make)


def _chip_ride(bufs):
    def make(src, dst, send_sems, recv_sems, base):
        x, y, c, chips = _position()
        return [_remote(src[b].at[2 * cx + cy], dst[b].at[2 * x + y], send_sems, recv_sems, base + b * 3 + k,
                        (cx, cy, c)) for b in range(len(bufs)) for k, (cx, cy) in enumerate(chips)]

    shapes = [jax.ShapeDtypeStruct(b.shape, b.dtype) for b in bufs]
    return _Ride(bufs, shapes, 3 * len(bufs), make)


def _sibling_ride(bufs):
    def make(src, dst, send_sems, recv_sems, base):
        x, y, c, _ = _position()
        return [_remote(src[b], dst[b], send_sems, recv_sems, base + b, (x, y, 1 - c)) for b in range(len(bufs))]

    shapes = [jax.ShapeDtypeStruct(b.shape, b.dtype) for b in bufs]
    return _Ride(bufs, shapes, len(bufs), make)


def _exchange_call(ride, name):
    n_src, n_dst = len(ride.srcs), len(ride.dst_shapes)

    def body(*refs):
        copies = ride.make(refs[:n_src], refs[n_src:n_src + n_dst], refs[-2], refs[-1], 0)
        for cp in copies:
            cp.start()
        for cp in copies:
            cp.wait()

    return pl.pallas_call(
        body,
        name=name,
        in_specs=[ANY_SPEC] * n_src,
        out_specs=[ANY_SPEC] * n_dst,
        out_shape=ride.dst_shapes,
        scratch_shapes=[pltpu.SemaphoreType.DMA((ride.n_copies,)), pltpu.SemaphoreType.DMA((ride.n_copies,))],
    )(*ride.srcs)


def _hosted_call(body, ride, n_steps, *, name, in_specs, out_specs, out_shape, scratch_shapes, args):
    params = pltpu.CompilerParams(dimension_semantics=("arbitrary",), vmem_limit_bytes=VMEM_LIMIT)
    if ride is None:
        res = pl.pallas_call(body, name=name, grid=(n_steps,), in_specs=list(in_specs), out_specs=list(out_specs),
                             out_shape=list(out_shape), scratch_shapes=list(scratch_shapes),
                             compiler_params=params)(*args)
        return list(res), []
    sizes = [len(in_specs), len(ride.srcs), len(out_specs), len(ride.dst_shapes), len(scratch_shapes), 2]

    def hosted(*refs):
        groups, pos = [], 0
        for n in sizes:
            groups.append(refs[pos:pos + n])
            pos += n
        ins, rin, outs, rout, scr, (send_sems, recv_sems) = groups
        i = pl.program_id(0)

        @pl.when(i == 0)
        def _():
            for cp in ride.make(rin, rout, send_sems, recv_sems, 0):
                cp.start()

        body(*ins, *outs, *scr)

        @pl.when(i == n_steps - 1)
        def _():
            for cp in ride.make(rin, rout, send_sems, recv_sems, 0):
                cp.wait()

    n_out = len(out_specs)
    res = pl.pallas_call(
        hosted,
        name=name,
        grid=(n_steps,),
        in_specs=list(in_specs) + [ANY_SPEC] * len(ride.srcs),
        out_specs=list(out_specs) + [ANY_SPEC] * len(ride.dst_shapes),
        out_shape=list(out_shape) + ride.dst_shapes,
        scratch_shapes=list(scratch_shapes) + [pltpu.SemaphoreType.DMA((ride.n_copies,)),
                                               pltpu.SemaphoreType.DMA((ride.n_copies,))],
        compiler_params=params,
    )(*args, *ride.srcs)
    return list(res[:n_out]), list(res[n_out:])


def _gather_weights(w_in, small):
    r_in, c_in = w_in.shape
    h_in = r_in // 2
    q_in = h_in // 2

    def body(win_ref, small_ref, wg_ref, sg_ref, send_sems, recv_sems):
        x, y, c, chips = _position()
        j = 2 * x + y
        sibling = (x, y, 1 - c)
        xn, yn, dg = chips
        jx, jy, jd = (2 * cx + cy for cx, cy in chips)

        wg_ref[j] = win_ref[...].astype(BF16)
        sg_ref[j] = small_ref[...]

        def half(jj, cc):
            return wg_ref.at[jj, pl.ds(cc * h_in, h_in), :]

        def quarter(jj, qq):
            return wg_ref.at[jj, pl.ds(c * h_in + qq * q_in, q_in), :]

        def copy(k, ref, to):
            return _remote(ref, ref, send_sems, recv_sems, k, to)

        first = [copy(0, quarter(j, 0), (*xn, c)), copy(2, quarter(j, 1), (*yn, c)),
                 copy(1, quarter(j, 1), (*xn, c)), copy(3, quarter(j, 0), (*yn, c))]
        first += [copy(9 + k, sg_ref.at[j], (cx, cy, c)) for k, (cx, cy) in enumerate(chips)]
        for cp in first:
            cp.start()
        copy(0, quarter(jx, 0), sibling).wait_recv()
        along_y = copy(4, quarter(jx, 0), (*yn, c))
        along_y.start()
        copy(2, quarter(jy, 1), sibling).wait_recv()
        along_x = copy(5, quarter(jy, 1), (*xn, c))
        along_x.start()
        copy(1, quarter(jx, 1), sibling).wait_recv()
        to_sib = [copy(6, half(jx, c), sibling)]
        to_sib[-1].start()
        copy(3, quarter(jy, 0), sibling).wait_recv()
        to_sib.append(copy(7, half(jy, c), sibling))
        to_sib[-1].start()
        copy(4, quarter(jd, 0), sibling).wait_recv()
        copy(5, quarter(jd, 1), sibling).wait_recv()
        to_sib.append(copy(8, half(jd, c), sibling))
        to_sib[-1].start()
        for k, jk in enumerate((jx, jy, jd)):
            copy(6 + k, half(jk, 1 - c), sibling).wait_recv()
            copy(9 + k, sg_ref.at[jk], sibling).wait_recv()
        for cp in first + [along_y, along_x] + to_sib:
            cp.wait_send()

    return pl.pallas_call(
        body,
        name="gather_weights",
        out_shape=(jax.ShapeDtypeStruct((N_CHIPS, r_in, c_in), BF16),
                   jax.ShapeDtypeStruct((N_CHIPS,) + small.shape, F32)),
        in_specs=[VMEM_SPEC, VMEM_SPEC],
        out_specs=(VMEM_SPEC, VMEM_SPEC),
        scratch_shapes=[pltpu.SemaphoreType.DMA((12,)), pltpu.SemaphoreType.DMA((12,))],
        compiler_params=pltpu.CompilerParams(vmem_limit_bytes=VMEM_LIMIT),
    )(w_in, small)


def _proj_segments(d_lru, d_qk, d_ret, chunk_w):
    widths = [d_lru, d_lru, d_qk, d_qk, d_ret, d_ret]
    segs, col = [], 0
    for w in widths:
        parts, off = [], 0
        while off < w:
            jj, inner = divmod(col + off, chunk_w)
            take = min(w - off, chunk_w - inner)
            parts.append((jj, inner, off, take))
            off += take
        segs.append(parts)
        col += w
    return segs


def _in_proj(x2, meta, gain, wg, cos_t, sin_t, w_out, tm, d_lru, d_qk, d_ret):
    s_len, d = x2.shape
    tp = s_len + CHUNK
    nt, nb = tp // tm, tm // CHUNK
    segs = _proj_segments(d_lru, d_qk, d_ret, wg.shape[2])
    widths = [d_lru, d_lru, d_qk, d_qk, d_ret, d_ret]
    r_out, c_out = w_out.shape
    h_out = r_out // 2
    fwd_step = min(6, nt - 1)

    def gather_w_out(i, wout_ref, wo_ref, wob, send_sems, recv_sems, local_sem):
        x, y, c, chips = _position()
        j = 2 * x + y
        sibling = (x, y, 1 - c)

        def half(jj, cc):
            return wo_ref.at[jj, pl.ds(cc * h_out, h_out), :]

        local = pltpu.make_async_copy(wob, wo_ref.at[j], local_sem)
        first = [_remote(wob.at[pl.ds(c * h_out, h_out), :], half(j, c), send_sems, recv_sems, k, (cx, cy, c))
                 for k, (cx, cy) in enumerate(chips)]
        passed = [_remote(half(2 * cx + cy, c), half(2 * cx + cy, c), send_sems, recv_sems, 3 + k, sibling)
                  for k, (cx, cy) in enumerate(chips)]

        @pl.when(i == 0)
        def _():
            wob[...] = wout_ref[...].astype(BF16)
            local.start()
            for cp in first:
                cp.start()

        @pl.when(i == fwd_step)
        def _():
            for k, (cx, cy) in enumerate(chips):
                _remote(half(2 * cx + cy, c), half(2 * cx + cy, c), send_sems, recv_sems, k, sibling).wait_recv()
                passed[k].start()

        @pl.when(i == nt - 1)
        def _():
            for k, (cx, cy) in enumerate(chips):
                jk = 2 * cx + cy
                _remote(half(jk, 1 - c), half(jk, 1 - c), send_sems, recv_sems, 3 + k, sibling).wait_recv()
            for cp in first + passed:
                cp.wait_send()
            local.wait()

    def body(*refs):
        xb = refs[:nb]
        meta_ref, g_ref, w_ref, cos_ref, sin_ref, wout_ref = refs[nb:nb + 6]
        hp_ref = refs[nb + 6]
        outs = refs[nb + 7:nb + 13]
        wo_ref, wob, send_sems, recv_sems, local_sem = refs[nb + 13:]
        i = pl.program_id(0)
        gather_w_out(i, wout_ref, wo_ref, wob, send_sems, recv_sems, local_sem)
        blocks = [r[...] for r in xb]
        head = jnp.concatenate([jnp.zeros((PAD_ROWS, d), F32), meta_ref[...]], axis=0)
        blocks[0] = jnp.where(i == 0, head, blocks[0])
        h = jnp.concatenate(blocks, axis=0)
        hp_ref[...] = h
        rinv = lax.rsqrt(jnp.mean(h * h, axis=-1, keepdims=True) + EPS)
        u = ((h * rinv) * g_ref[...]).astype(BF16)
        for out_ref, parts in zip(outs, segs):
            for jj, inner, off, take in parts:
                out_ref[:, off:off + take] = _dot(u, w_ref[jj, :, inner:inner + take])
        cos = _tile_lanes(cos_ref[...], d_qk // LANES)
        sin = _tile_lanes(sin_ref[...], d_qk // LANES)
        q = outs[2][...]
        outs[2][...] = q * cos + _rot_partner(q) * sin
        k = outs[3][...]
        outs[3][...] = (k * cos + _rot_partner(k) * sin) * (QK_DIM ** -0.5)

    x_specs = [pl.BlockSpec((CHUNK, d), functools.partial(lambda i, b: (jnp.maximum(i * nb + b - 1, 0), 0), b=b))
               for b in range(nb)]
    tile = lambda w: pl.BlockSpec((tm, w), lambda i: (i, 0))
    return pl.pallas_call(
        body,
        name="in_proj",
        grid=(nt,),
        in_specs=x_specs + [pl.BlockSpec(meta.shape, lambda i: (0, 0)),
                            pl.BlockSpec(gain.shape, lambda i: (0, 0)),
                            pl.BlockSpec(wg.shape, lambda i: (0, 0, 0)),
                            tile(LANES), tile(LANES),
                            pl.BlockSpec(w_out.shape, lambda i: (0, 0))],
        out_specs=[tile(d)] + [tile(w) for w in widths] + [ANY_SPEC],
        out_shape=[jax.ShapeDtypeStruct((tp, d), F32)] + [jax.ShapeDtypeStruct((tp, w), F32) for w in widths]
                  + [jax.ShapeDtypeStruct((N_CHIPS, r_out, c_out), BF16)],
        scratch_shapes=[pltpu.VMEM((r_out, c_out), BF16), pltpu.SemaphoreType.DMA((6,)),
                        pltpu.SemaphoreType.DMA((6,)), pltpu.SemaphoreType.DMA],
        compiler_params=pltpu.CompilerParams(dimension_semantics=("arbitrary",), vmem_limit_bytes=VMEM_LIMIT),
    )(*([x2] * nb), meta, gain, wg, cos_t, sin_t, w_out)


def _to_groups(ref3, x):
    for g in range(ref3.shape[0]):
        ref3[g] = x[:, g * LANES:(g + 1) * LANES]


def _from_groups(ref3):
    return jnp.concatenate([ref3[g] for g in range(ref3.shape[0])], axis=1)


def _segment_scan(a3, u3, out3, p3, carry, tm, reverse):
    groups = a3.shape[0]
    seg = tm // SUBLANES

    def step(j, state):
        hs, ps = state
        rows = pl.ds((seg - 1 - j) if reverse else j, SUBLANES, stride=seg)
        new_h, new_p = [], []
        for g in range(groups):
            a = a3[g, rows, :]
            h = a * hs[g] + u3[g, rows, :]
            p = ps[g] * a
            out3[g, rows, :] = h
            p3[g, rows, :] = p
            new_h.append(h)
            new_p.append(p)
        return tuple(new_h), tuple(new_p)

    zeros = tuple(jnp.zeros((SUBLANES, LANES), F32) for _ in range(groups))
    ones = tuple(jnp.ones((SUBLANES, LANES), F32) for _ in range(groups))
    lax.fori_loop(0, seg, step, (zeros, ones))
    carries = [carry[:, g * LANES:(g + 1) * LANES] for g in range(groups)]
    for s in (reversed(range(SUBLANES)) if reverse else range(SUBLANES)):
        rows = slice(s * seg, (s + 1) * seg)
        edge = s * seg if reverse else (s + 1) * seg - 1
        for g in range(groups):
            out3[g, rows, :] = out3[g, rows, :] + p3[g, rows, :] * carries[g]
            carries[g] = out3[g, edge:edge + 1, :]
    return jnp.concatenate(carries, axis=1)


def _softplus_neg(lam):
    z = -lam
    e = jnp.exp(-jnp.abs(z))
    e1 = 1.0 + e
    log1p_e = jnp.where(e1 == 1.0, e, jnp.log(e1) * (e / (e1 - 1.0)))
    return jnp.maximum(z, 0.0) + log1p_e


def _lru_fwd(lx, lg, cw, cb, wr, br, wi, bi, lam, tm):
    tp, w = lx.shape
    nt = tp // tm
    per8 = tm // SUBLANES
    n_heads = wr.shape[0]

    def body(lx_ref, lxp_ref, lg_ref, cw_ref, cb_ref, wr_ref, br_ref, wi_ref, bi_ref, lam_ref,
             hl_ref, y_ref, xc_ref, r_ref, ig_ref, a_ref, beta_ref, w4_ref, a_s, u_s, h_s, p_s, carry):
        i = pl.program_id(0)

        @pl.when(i == 0)
        def _():
            carry[...] = jnp.zeros_like(carry)

        lxv = lx_ref[...]
        prev8 = jnp.where(i == 0, 0.0, lxp_ref[...])
        xc = cb_ref[...] + _shift_down(lxv, prev8, 3) * cw_ref[0:1, :]
        xc = xc + _shift_down(lxv, prev8, 2) * cw_ref[1:2, :]
        xc = xc + _shift_down(lxv, prev8, 1) * cw_ref[2:3, :]
        xc = xc + lxv * cw_ref[3:4, :]
        xc_ref[...] = xc
        pre_r, pre_i = [], []
        for hd in range(n_heads):
            xh = xc[:, hd * LANES:(hd + 1) * LANES].astype(BF16)
            pre_r.append(_dot(xh, wr_ref[hd].astype(BF16)))
            pre_i.append(_dot(xh, wi_ref[hd].astype(BF16)))
        r = _sigmoid(jnp.concatenate(pre_r, axis=1) + br_ref[...])
        ig = _sigmoid(jnp.concatenate(pre_i, axis=1) + bi_ref[...])
        r_ref[...] = r
        ig_ref[...] = ig
        log_a = (-LRU_C * r) * _softplus_neg(lam_ref[...])
        a = jnp.exp(log_a)
        a_ref[...] = a
        zz = -2.0 * log_a
        series = zz * (1.0 - zz * (0.5 - zz * (1.0 / 6.0)))
        a2 = a * a
        beta2 = jnp.maximum(jnp.where(zz < 0.015625, series, 1.0 - a2), 1e-37)
        rsb = lax.rsqrt(beta2)
        beta = beta2 * rsb
        beta_ref[...] = beta
        w4_ref[...] = a2 * rsb
        row = lax.broadcasted_iota(jnp.int32, (tm, 1), 0) + i * tm
        _to_groups(a_s, a)
        _to_groups(u_s, jnp.where(row >= PAD_ROWS, beta * ig * xc, 0.0))
        carry[0:1, :] = _segment_scan(a_s, u_s, h_s, p_s, carry[0:1, :], tm, reverse=False)
        hl = _from_groups(h_s)
        hl_ref[...] = hl
        g = lg_ref[...]
        y_ref[...] = (hl * (g * _sigmoid(g))).astype(BF16)

    tile = pl.BlockSpec((tm, w), lambda i: (i, 0))
    prev = pl.BlockSpec((SUBLANES, w), lambda i: (jnp.maximum(i * per8 - 1, 0), 0))
    vec = pl.BlockSpec((1, w), lambda i: (0, 0))
    mat = pl.BlockSpec(wr.shape, lambda i: (0, 0, 0))
    f32_out = jax.ShapeDtypeStruct((tp, w), F32)
    return pl.pallas_call(
        body,
        name="lru_fwd",
        grid=(nt,),
        in_specs=[tile, prev, tile, pl.BlockSpec(cw.shape, lambda i: (0, 0)), vec, mat, vec, mat, vec, vec],
        out_specs=[tile] * 8,
        out_shape=[f32_out, jax.ShapeDtypeStruct((tp, w), BF16)] + [f32_out] * 6,
        scratch_shapes=[pltpu.VMEM((w // LANES, tm, LANES), F32)] * 4 + [pltpu.VMEM((SUBLANES, w), F32)],
        compiler_params=pltpu.CompilerParams(dimension_semantics=("arbitrary",), vmem_limit_bytes=VMEM_LIMIT),
    )(lx, lx, lg, cw, cb, wr, br, wi, bi, lam)


def _ret_tables():
    log_g = jnp.log1p(-jnp.exp2(-5.0 - jnp.arange(HEADS, dtype=F32)))
    idx = jnp.arange(CHUNK, dtype=F32)
    diff = idx[:, None] - idx[None, :]
    dmask = jnp.where(diff[None] >= 0.0, jnp.exp(jnp.maximum(diff, 0.0)[None] * log_g[:, None, None]), 0.0)
    kdec = jnp.repeat(jnp.exp((CHUNK - 1.0 - idx)[:, None] * log_g[None, :]), QK_DIM, axis=1)
    qdec = jnp.repeat(jnp.exp((idx + 1.0)[:, None] * log_g[None, :]), QK_DIM, axis=1)
    g_chunk = jnp.exp(CHUNK * log_g)
    g_rows = jnp.repeat(g_chunk, QK_DIM).reshape(HEADS // 2, 2 * QK_DIM, 1)
    g_state = jnp.broadcast_to(g_rows, (HEADS // 2, 2 * QK_DIM, 2 * LANES))
    r_head = jnp.arange(2 * QK_DIM)[:, None] // QK_DIM
    c_head = jnp.arange(2 * LANES)[None, :] // LANES
    block_diag = (r_head == c_head).astype(F32)
    return dmask, qdec, kdec, g_state, block_diag


def _head_norm(o_h):
    mu = jnp.mean(o_h, axis=-1, keepdims=True)
    oc = o_h - mu
    var = jnp.mean(oc * oc, axis=-1, keepdims=True)
    rstd = lax.rsqrt(var + EPS)
    return oc * rstd, rstd


def _ret_fwd(q, k, v, rg, gain, tables, tm):
    tp, d_qk = q.shape
    d_ret = v.shape[1]
    n_ch = tp // CHUNK
    cps = tm // CHUNK
    n_pairs = HEADS // 2
    dmask, qdec, kdec, g_state, block_diag = tables

    def body(q_ref, k_ref, v_ref, rg_ref, gain_ref, dm_ref, qd_ref, kd_ref, gs_ref, bd_ref,
             o_ref, y_ref, rp_ref, state):
        n = pl.program_id(0)

        @pl.when(n == 0)
        def _():
            state[...] = jnp.zeros_like(state)

        lane = lax.broadcasted_iota(jnp.int32, (CHUNK, LANES), 1)
        for ci in range(cps):
            rs = slice(ci * CHUNK, (ci + 1) * CHUNK)
            for p in range(n_pairs):
                qs = slice(p * LANES, (p + 1) * LANES)
                vs = slice(p * 2 * LANES, (p + 1) * 2 * LANES)
                qp, kp = q_ref[rs, qs], k_ref[rs, qs]
                vb = v_ref[rs, vs].astype(BF16)
                kb = kp.astype(BF16)
                qd = (qp * qd_ref[:, qs]).astype(BF16)
                kd = (kp * kd_ref[:, qs]).astype(BF16)
                st = state[p]
                st_b = st.astype(BF16)
                rp_ref[ci, p] = st_b
                cross = _dot(qd, st_b)
                for e in range(2):
                    hd = 2 * p + e
                    hs = slice(hd * LANES, (hd + 1) * LANES)
                    es = slice(e * LANES, (e + 1) * LANES)
                    qm = jnp.where((lane // QK_DIM) == e, qp, 0.0).astype(BF16)
                    s = _dot_nt(qm, kb) * dm_ref[hd]
                    o_h = _dot(s.astype(BF16), vb[:, es]) + cross[:, es]
                    o_ref[rs, hs] = o_h
                    xhat, _ = _head_norm(o_h)
                    g = rg_ref[rs, hs]
                    y_ref[rs, hs] = ((xhat * gain_ref[:, hs]) * (g * _sigmoid(g))).astype(BF16)
                state[p] = gs_ref[p] * st + bd_ref[...] * _dot_tn(kd, vb)

    ch = lambda w: pl.BlockSpec((tm, w), lambda n: (n, 0))
    const2 = lambda a: pl.BlockSpec(a.shape, lambda n: (0, 0))
    const3 = lambda a: pl.BlockSpec(a.shape, lambda n: (0, 0, 0))
    return pl.pallas_call(
        body,
        name="ret_fwd",
        grid=(n_ch // cps,),
        in_specs=[ch(d_qk), ch(d_qk), ch(d_ret), ch(d_ret), const2(gain), const3(dmask), const2(qdec), const2(kdec),
                  const3(g_state), const2(block_diag)],
        out_specs=[ch(d_ret), ch(d_ret),
                   pl.BlockSpec((cps, n_pairs, 2 * QK_DIM, 2 * LANES), lambda n: (n, 0, 0, 0))],
        out_shape=[jax.ShapeDtypeStruct((tp, d_ret), F32), jax.ShapeDtypeStruct((tp, d_ret), BF16),
                   jax.ShapeDtypeStruct((n_ch, n_pairs, 2 * QK_DIM, 2 * LANES), BF16)],
        scratch_shapes=[pltpu.VMEM((n_pairs, 2 * QK_DIM, 2 * LANES), F32)],
        compiler_params=pltpu.CompilerParams(dimension_semantics=("arbitrary",), vmem_limit_bytes=VMEM_LIMIT),
    )(q, k, v, rg, gain, dmask, qdec, kdec, g_state, block_diag)


def _out_proj_loss(y_lru, y_ret, hp, tgt, wo, gain_f, tm):
    tp, d = hp.shape
    w_lru = y_lru.shape[1]
    w_mix = wo.shape[0]
    nt, nb = tp // tm, tm // CHUNK

    def body(*refs):
        yl_ref, yr_ref, hp_ref = refs[:3]
        tb = refs[3:3 + nb]
        wo_ref, gf_ref = refs[3 + nb:5 + nb]
        dh2_ref, dyl_ref, dyr_ref, dwo_ref, dgf_ref, loss_ref = refs[5 + nb:]
        i = pl.program_id(0)

        @pl.when(i == 0)
        def _():
            dwo_ref[...] = jnp.zeros_like(dwo_ref)
            dgf_ref[...] = jnp.zeros_like(dgf_ref)
            loss_ref[...] = jnp.zeros_like(loss_ref)

        yl, yr = yl_ref[...], yr_ref[...]
        h2 = hp_ref[...] + _dot(yl, wo_ref[0:w_lru, :]) + _dot(yr, wo_ref[w_lru:w_mix, :])
        rinv = lax.rsqrt(jnp.mean(h2 * h2, axis=-1, keepdims=True) + EPS)
        nrm = h2 * rinv
        gf = gf_ref[...]
        tgt_v = jnp.concatenate([r[...] for r in tb], axis=0)
        row = lax.broadcasted_iota(jnp.int32, (tm, 1), 0) + i * tm
        err = jnp.where(row >= CHUNK, nrm * gf - tgt_v, 0.0)
        loss_ref[...] += 0.5 * jnp.sum(jnp.mean(err * err, axis=-1, keepdims=True))
        dout = err * (1.0 / d)
        dgf_ref[...] += jnp.sum(dout * nrm, axis=0, keepdims=True)
        dn = dout * gf
        dh2 = rinv * (dn - nrm * jnp.mean(dn * nrm, axis=-1, keepdims=True))
        dh2_ref[...] = dh2
        dh2b = dh2.astype(BF16)
        dyl_ref[...] = _dot_nt(dh2b, wo_ref[0:w_lru, :])
        dyr_ref[...] = _dot_nt(dh2b, wo_ref[w_lru:w_mix, :])
        dwo_ref[0:w_lru, :] += _dot_tn(yl, dh2b)
        dwo_ref[w_lru:w_mix, :] += _dot_tn(yr, dh2b)

    tile = lambda w: pl.BlockSpec((tm, w), lambda i: (i, 0))
    t_specs = [pl.BlockSpec((CHUNK, d), functools.partial(lambda i, b: (jnp.maximum(i * nb + b - 1, 0), 0), b=b))
               for b in range(nb)]
    return pl.pallas_call(
        body,
        name="out_proj_loss",
        grid=(nt,),
        in_specs=[tile(w_lru), tile(w_mix - w_lru), tile(d)] + t_specs +
                 [pl.BlockSpec(wo.shape, lambda i: (0, 0)), pl.BlockSpec(gain_f.shape, lambda i: (0, 0))],
        out_specs=[tile(d), tile(w_lru), tile(w_mix - w_lru), pl.BlockSpec(wo.shape, lambda i: (0, 0)),
                   pl.BlockSpec((1, d), lambda i: (0, 0)), pl.BlockSpec((SUBLANES, LANES), lambda i: (0, 0))],
        out_shape=[jax.ShapeDtypeStruct((tp, d), F32), jax.ShapeDtypeStruct((tp, w_lru), F32),
                   jax.ShapeDtypeStruct((tp, w_mix - w_lru), F32), jax.ShapeDtypeStruct(wo.shape, F32),
                   jax.ShapeDtypeStruct((1, d), F32), jax.ShapeDtypeStruct((SUBLANES, LANES), F32)],
        compiler_params=pltpu.CompilerParams(dimension_semantics=("arbitrary",), vmem_limit_bytes=VMEM_LIMIT),
    )(y_lru, y_ret, hp, *([tgt] * nb), wo, gain_f)


def _ret_bwd(q, k, v, rg, o, rprev, dy, gain, cos_t, sin_t, tables, tm, ride=None):
    tp, d_qk = q.shape
    d_ret = v.shape[1]
    n_ch = tp // CHUNK
    cps = tm // CHUNK
    n_pairs = HEADS // 2
    dmask, qdec, kdec, g_state, block_diag = tables

    dmask_t = jnp.swapaxes(dmask, 1, 2)

    def body(q_ref, k_ref, v_ref, rg_ref, o_ref, rp_ref, dy_ref, gain_ref, cos_ref, sin_ref,
             dm_ref, dmt_ref, qd_ref, kd_ref, gs_ref, bd_ref, dq_ref, dk_ref, dv_ref, drg_ref, dgain_ref, dstate):
        n = pl.program_id(0)

        @pl.when(n == 0)
        def _():
            dstate[...] = jnp.zeros_like(dstate)
            dgain_ref[...] = jnp.zeros_like(dgain_ref)

        lane = lax.broadcasted_iota(jnp.int32, (CHUNK, LANES), 1)
        for ci in reversed(range(cps)):
            rs = slice(ci * CHUNK, (ci + 1) * CHUNK)
            dq_parts, dk_parts = [], []
            for p in range(n_pairs):
                qs = slice(p * LANES, (p + 1) * LANES)
                vs = slice(p * 2 * LANES, (p + 1) * 2 * LANES)
                do_parts = []
                for e in range(2):
                    hd = 2 * p + e
                    hs = slice(hd * LANES, (hd + 1) * LANES)
                    xhat, rstd = _head_norm(o_ref[rs, hs])
                    g = rg_ref[rs, hs]
                    sg = _sigmoid(g)
                    dyh = dy_ref[rs, hs]
                    gn = gain_ref[:, hs]
                    d_on = dyh * (g * sg)
                    drg_ref[rs, hs] = (dyh * (xhat * gn) * (sg * (1.0 + g * (1.0 - sg)))).astype(BF16)
                    dgain_ref[:, hs] += jnp.sum(d_on * xhat, axis=0, keepdims=True)
                    dxh = d_on * gn
                    do_parts.append(rstd * (dxh - jnp.mean(dxh, axis=-1, keepdims=True)
                                            - xhat * jnp.mean(dxh * xhat, axis=-1, keepdims=True)))
                do_b = jnp.concatenate(do_parts, axis=1).astype(BF16)
                qp, kp = q_ref[rs, qs], k_ref[rs, qs]
                vb = v_ref[rs, vs].astype(BF16)
                kb = kp.astype(BF16)
                qd = (qp * qd_ref[:, qs]).astype(BF16)
                kd = (kp * kd_ref[:, qs]).astype(BF16)
                dst = dstate[p]
                dst_b = dst.astype(BF16)
                dqp = _dot_nt(do_b, rp_ref[ci, p]) * qd_ref[:, qs]
                dkp = _dot_nt(vb, dst_b) * kd_ref[:, qs]
                dvp = _dot(kd, dst_b)
                dv_parts = []
                for e in range(2):
                    hd = 2 * p + e
                    es = slice(e * LANES, (e + 1) * LANES)
                    mine = (lane // QK_DIM) == e
                    qm = jnp.where(mine, qp, 0.0).astype(BF16)
                    km = jnp.where(mine, kp, 0.0).astype(BF16)
                    ds = (_dot_nt(do_b[:, es], vb[:, es]) * dm_ref[hd]).astype(BF16)
                    s_t = (_dot_nt(kb, qm) * dmt_ref[hd]).astype(BF16)
                    ds_t = (_dot_nt(vb[:, es], do_b[:, es]) * dmt_ref[hd]).astype(BF16)
                    dv_parts.append(dvp[:, es] + _dot(s_t, do_b[:, es]))
                    dqp = dqp + _dot(ds, km)
                    dkp = dkp + _dot(ds_t, qm)
                dv_ref[rs, vs] = jnp.concatenate(dv_parts, axis=1).astype(BF16)
                dstate[p] = gs_ref[p] * dst + bd_ref[...] * _dot_tn(qd, do_b)
                dq_parts.append(dqp)
                dk_parts.append(dkp)
            cos = _tile_lanes(cos_ref[rs, :], d_qk // LANES)
            sin = _tile_lanes(sin_ref[rs, :], d_qk // LANES)
            dq = jnp.concatenate(dq_parts, axis=1)
            dk = jnp.concatenate(dk_parts, axis=1) * (QK_DIM ** -0.5)
            dq_ref[rs, :] = (dq * cos + _rot_partner(dq * sin)).astype(BF16)
            dk_ref[rs, :] = (dk * cos + _rot_partner(dk * sin)).astype(BF16)

    last = n_ch // cps - 1
    ch = lambda w: pl.BlockSpec((tm, w), lambda n: (last - n, 0))
    const2 = lambda a: pl.BlockSpec(a.shape, lambda n: (0, 0))
    const3 = lambda a: pl.BlockSpec(a.shape, lambda n: (0, 0, 0))
    return _hosted_call(
        body, ride, n_ch // cps,
        name="ret_bwd",
        in_specs=[ch(d_qk), ch(d_qk), ch(d_ret), ch(d_ret), ch(d_ret),
                  pl.BlockSpec((cps, n_pairs, 2 * QK_DIM, 2 * LANES), lambda n: (last - n, 0, 0, 0)),
                  ch(d_ret), const2(gain), ch(LANES), ch(LANES),
                  const3(dmask), const3(dmask_t), const2(qdec), const2(kdec), const3(g_state), const2(block_diag)],
        out_specs=[ch(d_qk), ch(d_qk), ch(d_ret), ch(d_ret), pl.BlockSpec((1, d_ret), lambda n: (0, 0))],
        out_shape=[jax.ShapeDtypeStruct((tp, d_qk), BF16), jax.ShapeDtypeStruct((tp, d_qk), BF16),
                   jax.ShapeDtypeStruct((tp, d_ret), BF16), jax.ShapeDtypeStruct((tp, d_ret), BF16),
                   jax.ShapeDtypeStruct((1, d_ret), F32)],
        scratch_shapes=[pltpu.VMEM((n_pairs, 2 * QK_DIM, 2 * LANES), F32)],
        args=(q, k, v, rg, o, rprev, dy, gain, cos_t, sin_t, dmask, dmask_t, qdec, kdec, g_state, block_diag),
    )


def _lru_bwd(lx, lg, hl, dy, saved, cw, wr, wi, lam, tm, ride=None):
    tp, w = lx.shape
    nt = tp // tm
    per8 = tm // SUBLANES
    n_heads = wr.shape[0]

    def body(lx_ref, lg_ref, hl_ref, hlp_ref, dy_ref, xc_ref, r_ref, ig_ref, a_ref, beta_ref, w4_ref,
             cw_ref, wr_ref, wi_ref, lam_ref,
             dlx_ref, dlg_ref, dcw_ref, dcb_ref, dwr_ref, dbr_ref, dwi_ref, dbi_ref, dlam_ref,
             g_s, dh_s, b_s, p_s, carry, dxc_next, a_next):
        i = pl.program_id(0)
        first_tile = i == nt - 1

        @pl.when(i == 0)
        def _():
            carry[...] = jnp.zeros_like(carry)
            dxc_next[...] = jnp.zeros_like(dxc_next)
            a_next[...] = jnp.zeros_like(a_next)
            for r in (dcw_ref, dcb_ref, dwr_ref, dbr_ref, dwi_ref, dbi_ref, dlam_ref):
                r[...] = jnp.zeros_like(r)

        lxv = lx_ref[...]
        a, beta, r, ig, xc = a_ref[...], beta_ref[...], r_ref[...], ig_ref[...], xc_ref[...]
        g = lg_ref[...]
        sg = _sigmoid(g)
        dyv = dy_ref[...]
        hlv = hl_ref[...]
        dlg_ref[...] = (dyv * hlv * (sg * (1.0 + g * (1.0 - sg)))).astype(BF16)
        _to_groups(g_s, dyv * (g * sg))
        _to_groups(b_s, _shift_up(a, a_next[...], 1))
        carry[0:1, :] = _segment_scan(b_s, g_s, dh_s, p_s, carry[0:1, :], tm, reverse=True)
        a_next[...] = a[0:SUBLANES]
        dh = _from_groups(dh_s)
        hprev = _shift_down(hlv, jnp.where(first_tile, 0.0, hlp_ref[...]), 1)
        row = lax.broadcasted_iota(jnp.int32, (tm, 1), 0) + (nt - 1 - i) * tm
        du = jnp.where(row >= PAD_ROWS, dh, 0.0)
        da = dh * hprev
        dbeta = du * ig * xc
        d_ig = du * beta * xc
        dxc = du * beta * ig
        dloga = da * a - dbeta * w4_ref[...]
        lam_v = lam_ref[...]
        dlam_ref[...] += jnp.sum(dloga * r, axis=0, keepdims=True) * (LRU_C * _sigmoid(-lam_v))
        dpr = (dloga * (-LRU_C * _softplus_neg(lam_v))) * r * (1.0 - r)
        dpi = d_ig * ig * (1.0 - ig)
        dbr_ref[...] += jnp.sum(dpr, axis=0, keepdims=True)
        dbi_ref[...] += jnp.sum(dpi, axis=0, keepdims=True)
        dxc_parts = []
        for hd in range(n_heads):
            hs = slice(hd * LANES, (hd + 1) * LANES)
            xh = xc[:, hs].astype(BF16)
            dprh = dpr[:, hs].astype(BF16)
            dpih = dpi[:, hs].astype(BF16)
            dwr_ref[hd] += _dot_tn(xh, dprh)
            dwi_ref[hd] += _dot_tn(xh, dpih)
            dxc_parts.append(_dot_nt(dprh, wr_ref[hd].astype(BF16)) + _dot_nt(dpih, wi_ref[hd].astype(BF16)))
        dxc = dxc + jnp.concatenate(dxc_parts, axis=1)
        nxt = dxc_next[...]
        up1, up2, up3 = _shift_up(dxc, nxt, 1), _shift_up(dxc, nxt, 2), _shift_up(dxc, nxt, 3)
        dlx = dxc * cw_ref[3:4, :]
        dlx = dlx + up1 * cw_ref[2:3, :]
        dlx = dlx + up2 * cw_ref[1:2, :]
        dlx = dlx + up3 * cw_ref[0:1, :]
        dlx_ref[...] = dlx.astype(BF16)
        dxc_next[...] = dxc[0:SUBLANES]
        dcb_ref[...] += jnp.sum(dxc, axis=0, keepdims=True)
        dcw_ref[0:1, :] += jnp.sum(up3 * lxv, axis=0, keepdims=True)
        dcw_ref[1:2, :] += jnp.sum(up2 * lxv, axis=0, keepdims=True)
        dcw_ref[2:3, :] += jnp.sum(up1 * lxv, axis=0, keepdims=True)
        dcw_ref[3:4, :] += jnp.sum(dxc * lxv, axis=0, keepdims=True)

    last = nt - 1
    tile = pl.BlockSpec((tm, w), lambda i: (last - i, 0))
    prev = pl.BlockSpec((SUBLANES, w), lambda i: (jnp.maximum((last - i) * per8 - 1, 0), 0))
    vec = pl.BlockSpec((1, w), lambda i: (0, 0))
    mat = pl.BlockSpec(wr.shape, lambda i: (0, 0, 0))
    cwb = pl.BlockSpec(cw.shape, lambda i: (0, 0))
    return _hosted_call(
        body, ride, nt,
        name="lru_bwd",
        in_specs=[tile, tile, tile, prev, tile] + [tile] * 6 + [cwb, mat, mat, vec],
        out_specs=[tile, tile, cwb, vec, mat, vec, mat, vec, vec],
        out_shape=[jax.ShapeDtypeStruct((tp, w), BF16), jax.ShapeDtypeStruct((tp, w), BF16),
                   jax.ShapeDtypeStruct(cw.shape, F32), jax.ShapeDtypeStruct((1, w), F32),
                   jax.ShapeDtypeStruct(wr.shape, F32), jax.ShapeDtypeStruct((1, w), F32),
                   jax.ShapeDtypeStruct(wr.shape, F32), jax.ShapeDtypeStruct((1, w), F32),
                   jax.ShapeDtypeStruct((1, w), F32)],
        scratch_shapes=[pltpu.VMEM((w // LANES, tm, LANES), F32)] * 4 + [pltpu.VMEM((SUBLANES, w), F32)] * 3,
        args=(lx, lg, hl, hl, dy, *saved, cw, wr, wi, lam),
    )


def _in_proj_dw(dparts, hp, gain, wg_shape, ride=None):
    tp, d = hp.shape
    n_ch = tp // CHUNK
    per = next(p for p in (4, 2, 5, 3, 1) if (n_ch - 1) % p == 0)
    n_steps = 1 + (n_ch - 1) // per
    widths = [p.shape[1] for p in dparts]
    segs = _proj_segments(widths[0], widths[2], widths[4], wg_shape[2])

    def body(*refs):
        dp = [refs[p * per:(p + 1) * per] for p in range(6)]
        hp_b = refs[6 * per:7 * per]
        g_ref, dwg_ref, acc, sem = refs[7 * per:]
        i = pl.program_id(0)

        def accumulate(blocks):
            h = jnp.concatenate([hp_b[b][...] for b in blocks], axis=0)
            rinv = lax.rsqrt(jnp.mean(h * h, axis=-1, keepdims=True) + EPS)
            u = ((h * rinv) * g_ref[...]).astype(BF16)
            for p_refs, parts in zip(dp, segs):
                for jj, inner, off, take in parts:
                    seg = jnp.concatenate([p_refs[b][:, off:off + take] for b in blocks], axis=0)
                    acc[jj, :, inner:inner + take] += _dot_tn(u, seg)

        @pl.when(i == 0)
        def _():
            acc[...] = jnp.zeros_like(acc)
            accumulate([0])

        @pl.when(i > 0)
        def _():
            accumulate(list(range(per)))

        @pl.when(i == n_steps - 1)
        def _():
            cp = pltpu.make_async_copy(acc, dwg_ref, sem)
            cp.start()
            cp.wait()

    def blocks(w):
        return [pl.BlockSpec((CHUNK, w), functools.partial(
            lambda i, b: (jnp.where(i == 0, b, per * (i - 1) + 1 + b), 0), b=b)) for b in range(per)]

    in_specs, args = [], []
    for a, w in list(zip(dparts, widths)) + [(hp, d)]:
        in_specs += blocks(w)
        args += [a] * per
    outs, rides = _hosted_call(
        body, ride, n_steps,
        name="in_proj_dw",
        in_specs=in_specs + [pl.BlockSpec(gain.shape, lambda i: (0, 0))],
        out_specs=[ANY_SPEC],
        out_shape=[jax.ShapeDtypeStruct(wg_shape, F32)],
        scratch_shapes=[pltpu.VMEM(wg_shape, F32), pltpu.SemaphoreType.DMA],
        args=(*args, gain),
    )
    return outs[0], rides


def _in_proj_dx(dparts, hp, dh2, gain, wg, s_len, tm, ride=None):
    tp, d = hp.shape
    nt = tp // tm
    widths = [p.shape[1] for p in dparts]
    segs = _proj_segments(widths[0], widths[2], widths[4], wg.shape[2])

    def body(*refs):
        dp = refs[:6]
        hp_ref, dh2_ref, g_ref, w_ref = refs[6:10]
        gx_ref, dmeta_ref, dg_ref = refs[10:13]
        stage, sems = refs[13:]
        i = pl.program_id(0)

        @pl.when(i == 0)
        def _():
            dg_ref[...] = jnp.zeros_like(dg_ref)

        h = hp_ref[...]
        rinv = lax.rsqrt(jnp.mean(h * h, axis=-1, keepdims=True) + EPS)
        nrm = h * rinv
        gv = g_ref[...]
        du = jnp.zeros((tm, d), F32)
        for p_ref, parts in zip(dp, segs):
            for jj, inner, off, take in parts:
                du = du + _dot_nt(p_ref[:, off:off + take], w_ref[jj, :, inner:inner + take])
        dg_ref[...] += jnp.sum(du * nrm, axis=0, keepdims=True)
        dn = du * gv
        dh = dh2_ref[...] + rinv * (dn - nrm * jnp.mean(dn * nrm, axis=-1, keepdims=True))

        def first_copy():
            return pltpu.make_async_copy(stage.at[0, pl.ds(CHUNK, tm - CHUNK), :],
                                         gx_ref.at[pl.ds(0, tm - CHUNK), :], sems.at[0])

        def tile_copy(slot, start):
            return pltpu.make_async_copy(stage.at[slot], gx_ref.at[pl.ds(start, tm), :], sems.at[slot])

        @pl.when(i == 0)
        def _():
            dmeta_ref[...] = dh[PAD_ROWS:CHUNK]
            stage[0] = dh
            first_copy().start()

        @pl.when(i > 0)
        def _():
            slot = 1 + i % 2

            @pl.when(i >= 3)
            def _():
                tile_copy(slot, 0).wait()

            stage[slot] = dh
            tile_copy(slot, pl.multiple_of(i * tm - CHUNK, CHUNK)).start()

        @pl.when(i == nt - 1)
        def _():
            first_copy().wait()
            for step in (nt - 2, nt - 1):
                if step >= 1:
                    tile_copy(1 + step % 2, 0).wait()

    tile = lambda w: pl.BlockSpec((tm, w), lambda i: (i, 0))
    return _hosted_call(
        body, ride, nt,
        name="in_proj_dx",
        in_specs=[tile(w) for w in widths] + [tile(d), tile(d), pl.BlockSpec(gain.shape, lambda i: (0, 0)),
                                              pl.BlockSpec(wg.shape, lambda i: (0, 0, 0))],
        out_specs=[ANY_SPEC, pl.BlockSpec((N_META, d), lambda i: (0, 0)), pl.BlockSpec((1, d), lambda i: (0, 0))],
        out_shape=[jax.ShapeDtypeStruct((s_len, d), F32), jax.ShapeDtypeStruct((N_META, d), F32),
                   jax.ShapeDtypeStruct((1, d), F32)],
        scratch_shapes=[pltpu.VMEM((3, tm, d), F32), pltpu.SemaphoreType.DMA((3,))],
        args=(*dparts, hp, dh2, gain, wg),
    )


def _pair_sum(buf, recv, c_arr, tr, name):
    _, rows, cols = buf.shape

    def body(c_ref, mine_ref, got_ref, out_ref):
        out_ref[...] = (mine_ref[...] + got_ref[...]).astype(BF16)

    grid_spec = pltpu.PrefetchScalarGridSpec(
        num_scalar_prefetch=1,
        grid=(N_CHIPS, rows // tr),
        in_specs=[pl.BlockSpec((1, tr, cols), lambda jj, r, c_ref: (2 * jj + c_ref[0], r, 0)),
                  pl.BlockSpec((1, tr, cols), lambda jj, r, c_ref: (jj, r, 0))],
        out_specs=pl.BlockSpec((1, tr, cols), lambda jj, r, c_ref: (jj, r, 0)),
    )
    return pl.pallas_call(
        body,
        name=name,
        grid_spec=grid_spec,
        out_shape=jax.ShapeDtypeStruct((N_CHIPS, rows, cols), BF16),
    )(c_arr, buf, recv)


def _chip_sum(mine, got, j_arr, tr, name, loss_part=None):
    _, rows, cols = got.shape
    extra = [] if loss_part is None else [loss_part]

    def body(j_ref, mine_ref, got_ref, *rest):
        out_ref = rest[-1]
        j = j_ref[0]
        acc = None
        for jj in range(N_CHIPS):
            term = jnp.where(j == jj, mine_ref[0], got_ref[jj]).astype(F32)
            acc = term if acc is None else acc + term
        out_ref[...] = acc
        if loss_part is not None:
            out_ref[ROW_LOSS:ROW_LOSS + 1, :] = rest[0][0:1, :]

    grid_spec = pltpu.PrefetchScalarGridSpec(
        num_scalar_prefetch=1,
        grid=(rows // tr,),
        in_specs=[pl.BlockSpec((1, tr, cols), lambda r, j_ref: (j_ref[0], r, 0)),
                  pl.BlockSpec((N_CHIPS, tr, cols), lambda r, j_ref: (0, r, 0))] +
                 [pl.BlockSpec(e.shape, lambda r, j_ref: (0, 0)) for e in extra],
        out_specs=pl.BlockSpec((tr, cols), lambda r, j_ref: (r, 0)),
    )
    return pl.pallas_call(
        body,
        name=name,
        grid_spec=grid_spec,
        out_shape=jax.ShapeDtypeStruct((rows, cols), F32),
    )(j_arr, mine, got, *extra)


def _finish_exchange(f_in, f_small):
    def body(fin_ref, fs_ref, rin_ref, os_ref, send_sems, recv_sems, local_sem):
        x, y, c, chips = _position()
        j = 2 * x + y
        me = 2 * j + c
        sibling = (x, y, 1 - c)
        local = pltpu.make_async_copy(fs_ref, os_ref.at[me], local_sem)
        local.start()

        def copy(k, src, dst, to):
            return _remote(src, dst, send_sems, recv_sems, k, to)

        first = [copy(0, fin_ref, rin_ref, sibling), copy(1, fs_ref, os_ref.at[me], sibling)]
        first += [copy(2 + k, fs_ref, os_ref.at[me], (cx, cy, c)) for k, (cx, cy) in enumerate(chips)]
        for cp in first:
            cp.start()
        passed = []
        for k, (cx, cy) in enumerate(chips):
            unit = 2 * (2 * cx + cy) + c
            copy(2 + k, fs_ref, os_ref.at[unit], sibling).wait_recv()
            fwd = copy(5 + k, os_ref.at[unit], os_ref.at[unit], sibling)
            fwd.start()
            passed.append(fwd)
        copy(0, fin_ref, rin_ref, sibling).wait_recv()
        copy(1, fs_ref, os_ref.at[2 * j + 1 - c], sibling).wait_recv()
        for k, (cx, cy) in enumerate(chips):
            unit = 2 * (2 * cx + cy) + 1 - c
            copy(5 + k, fs_ref, os_ref.at[unit], sibling).wait_recv()
        for cp in first + passed:
            cp.wait_send()
        local.wait()

    return pl.pallas_call(
        body,
        name="grad_finish_exchange",
        in_specs=[ANY_SPEC] * 2,
        out_specs=[ANY_SPEC] * 2,
        out_shape=[jax.ShapeDtypeStruct(f_in.shape, F32), jax.ShapeDtypeStruct((N_DEV,) + f_small.shape, F32)],
        scratch_shapes=[pltpu.SemaphoreType.DMA((8,)), pltpu.SemaphoreType.DMA((8,)), pltpu.SemaphoreType.DMA],
    )(f_in, f_small)


def _adamw_math(w, g, m, v):
    m = ADAM_B1 * m + (1.0 - ADAM_B1) * g
    v = ADAM_B2 * v + (1.0 - ADAM_B2) * (g * g)
    m_hat = m / (1.0 - ADAM_B1 ** ADAM_STEP)
    v_hat = v / (1.0 - ADAM_B2 ** ADAM_STEP)
    delta = -ADAM_LR * (m_hat / (jnp.sqrt(v_hat) + ADAM_EPS) + ADAM_WD * w)
    return delta, m, v


def _adamw_big(w, g_mine, g_sib, m, v, c_arr, tr, name):
    rows, cols = w.shape
    half = rows // 2
    per = half // tr

    def body(c_ref, w_ref, gm_ref, gs_ref, m_ref, v_ref, g_ref, d_ref, mo_ref, vo_ref):
        g = jnp.where(pl.program_id(0) == c_ref[0], gm_ref[...], gs_ref[...])
        g_ref[...] = g
        d_ref[...], mo_ref[...], vo_ref[...] = _adamw_math(w_ref[...], g, m_ref[...], v_ref[...])

    full = pl.BlockSpec((tr, cols), lambda h, r, c_ref: (h * per + r, 0))
    unit = pl.BlockSpec((tr, cols), lambda h, r, c_ref: (r, 0))
    grid_spec = pltpu.PrefetchScalarGridSpec(
        num_scalar_prefetch=1,
        grid=(2, per),
        in_specs=[full, unit, unit, full, full],
        out_specs=[full] * 4,
    )
    return pl.pallas_call(
        body,
        name=name,
        grid_spec=grid_spec,
        out_shape=[jax.ShapeDtypeStruct(w.shape, F32)] * 4,
    )(c_arr, w, g_mine, g_sib, m, v)


def _adamw_small(j_arr, packed, params):
    names = list(params)
    n = len(names)
    vec_names = ["norm_gain", "conv_b", "b_rg", "b_ig", "lru_lambda", "ret_norm_gain", "final_norm_gain"]

    def body(j_ref, pk_ref, *refs):
        ins = refs[:3 * n]
        outs = refs[3 * n:]
        j = j_ref[0]

        def shard(row, rows):
            return jnp.concatenate([pk_ref[2 * j, row:row + rows, :], pk_ref[2 * j + 1, row:row + rows, :]], axis=1)

        def tail_sum(unit, row, rows):
            start = pl.multiple_of(UNIT_ROWS + TAIL_ROWS * unit + row, SUBLANES)
            total = pk_ref[0, pl.ds(start, rows), :]
            for dev in range(1, N_DEV):
                total = total + pk_ref[dev, pl.ds(start, rows), :]
            return total

        for idx, name in enumerate(names):
            if name == "w_rg":
                g = pk_ref[:, ROW_WR:ROW_WR + LANES, :]
            elif name == "w_ig":
                g = pk_ref[:, ROW_WI:ROW_WI + LANES, :]
            elif name == "meta_tokens":
                g = jnp.concatenate([tail_sum(2 * j, 0, N_META), tail_sum(2 * j + 1, 0, N_META)], axis=1)
            elif name == "norm_gain":
                g = jnp.concatenate([tail_sum(u, N_META, SUBLANES)[0:1] for u in range(N_DEV)], axis=1)
            elif name == "conv_w":
                g = shard(ROW_CONV, 4)
            else:
                row = ROW_VEC + vec_names.index(name)
                g = jnp.concatenate([pk_ref[u, row:row + 1, :] for u in range(N_DEV)], axis=1)
            w_ref, m_ref, v_ref = ins[3 * idx:3 * idx + 3]
            delta, m, v = _adamw_math(w_ref[...], g, m_ref[...], v_ref[...])
            g_ref, d_ref, mo_ref, vo_ref = outs[4 * idx:4 * idx + 4]
            g_ref[...], d_ref[...], mo_ref[...], vo_ref[...] = g, delta, m, v
        total = pk_ref[0, ROW_LOSS:ROW_LOSS + 1, :]
        for u in range(1, N_DEV):
            total = total + pk_ref[u, ROW_LOSS:ROW_LOSS + 1, :]
        outs[4 * n][...] = jnp.broadcast_to(total, (SUBLANES, LANES))

    flat_in, out_shape = [], []
    for name in names:
        w, m, v = params[name]
        flat_in += [w, m, v]
        out_shape += [jax.ShapeDtypeStruct(w.shape, F32)] * 4
    out_shape.append(jax.ShapeDtypeStruct((SUBLANES, LANES), F32))
    res = pl.pallas_call(
        body,
        name="adamw_small",
        in_specs=[SMEM_SPEC, VMEM_SPEC] + [VMEM_SPEC] * (3 * n),
        out_specs=[VMEM_SPEC] * (4 * n + 1),
        out_shape=out_shape,
    )(j_arr, packed, *flat_in)
    return {name: tuple(res[4 * idx:4 * idx + 4]) for idx, name in enumerate(names)}, res[4 * n][0, 0]


def _units(a):
    rows = a.shape[0]
    return jnp.transpose(a.reshape(rows, N_DEV, LANES), (1, 0, 2))


def kernel(x, meta_tokens, norm_gain, w_in, conv_w, conv_b, w_rg, b_rg, w_ig, b_ig, lru_lambda, ret_norm_gain, w_out, final_norm_gain, loss_target, m_meta_tokens, m_norm_gain, m_w_in, m_conv_w, m_conv_b, m_w_rg, m_b_rg, m_w_ig, m_b_ig, m_lru_lambda, m_ret_norm_gain, m_w_out, m_final_norm_gain, v_meta_tokens, v_norm_gain, v_w_in, v_conv_w, v_conv_b, v_w_rg, v_b_rg, v_w_ig, v_b_ig, v_lru_lambda, v_ret_norm_gain, v_w_out, v_final_norm_gain):
    s_len, d = x.shape[1], x.shape[2]
    d_lru = w_rg.shape[1] * w_rg.shape[2]
    d_ret = ret_norm_gain.shape[1]
    d_qk = HEADS * QK_DIM
    tp = s_len + CHUNK
    tm = TOKEN_TILE
    assert tp % tm == 0 and d_lru == HEADS * LANES and d_ret == HEADS * LANES
    ax, ay, ac = lax.axis_index("x"), lax.axis_index("y"), lax.axis_index("c")
    c_arr = jnp.reshape(ac, (1,)).astype(jnp.int32)
    j_arr = jnp.reshape(2 * ax + ay, (1,)).astype(jnp.int32)

    small = jnp.concatenate([meta_tokens, conv_w[0], jnp.zeros((4, meta_tokens.shape[1]), F32)], axis=0)
    wg, sg = _gather_weights(w_in[0], small)
    cols = sg.shape[2]
    meta_full = jnp.transpose(sg[:, :N_META, :], (1, 0, 2)).reshape(N_META, N_CHIPS * cols)
    cw_full = jnp.transpose(sg[:, N_META:N_META + 4, :], (1, 0, 2)).reshape(4, N_CHIPS * cols)
    cw8 = jnp.concatenate([cw_full, jnp.zeros((4, cw_full.shape[1]), F32)], axis=0)

    half = QK_DIM // 2
    inv = ROPE_BASE ** (-jnp.arange(half, dtype=F32) / half)
    pos = (jnp.arange(tp) - PAD_ROWS).astype(F32)
    ang = pos[:, None] * inv[None, :]
    cos_t = jnp.tile(jnp.cos(ang), (1, LANES // half))
    sign = jnp.where((jnp.arange(LANES) % QK_DIM) < half, -1.0, 1.0).astype(F32)
    sin_t = jnp.tile(jnp.sin(ang), (1, LANES // half)) * sign[None, :]
    tables = _ret_tables()
    gain_f = final_norm_gain.reshape(1, d)

    hp, lx, lg, q, k, v, rg, wo4 = _in_proj(x[0], meta_full, norm_gain, wg, cos_t, sin_t, w_out[0], tm,
                                            d_lru, d_qk, d_ret)
    wo = wo4.reshape(N_CHIPS * wo4.shape[1], wo4.shape[2])
    hl, y_lru, *lru_saved = _lru_fwd(lx, lg, cw8, conv_b, w_rg[0], b_rg, w_ig[0], b_ig, lru_lambda, tm)
    o, y_ret, rprev = _ret_fwd(q, k, v, rg, ret_norm_gain, tables, tm)
    dh2, dy_lru, dy_ret, dwo, dgf, loss_acc = _out_proj_loss(y_lru, y_ret, hp, loss_target[0], wo, gain_f, tm)

    g_out = dwo.reshape(N_DEV, dwo.shape[0] // N_DEV, dwo.shape[1])
    (dq, dk, dv, drg, dgain), (r_out,) = _ret_bwd(q, k, v, rg, o, rprev, dy_ret, ret_norm_gain, cos_t, sin_t, tables,
                                                 tm, ride=_pair_ride([g_out]))
    q_out = _pair_sum(g_out, r_out, c_arr, 128, "grad_pair_sum_out")
    (dlx, dlg, dcw, dcb, dwr, dbr, dwi, dbi, dlam), (e_out,) = _lru_bwd(
        lx, lg, hl, dy_lru, lru_saved, cw8, w_rg[0], w_ig[0], lru_lambda, tm, ride=_chip_ride([q_out]))
    f_out = _chip_sum(q_out, e_out, j_arr, 128, "grad_chip_sum_out")
    zero_row = jnp.zeros((1, d), F32)
    vecs = [zero_row, dcb, dbr, dbi, dlam, dgain, dgf]
    g_small = jnp.concatenate([dwr, dwi, jnp.zeros((N_DEV, N_META, LANES), F32), _units(dcw[0:4])]
                              + [_units(a) for a in vecs]
                              + [jnp.zeros((N_DEV, UNIT_ROWS - ROW_VEC - N_VEC, LANES), F32)], axis=1)
    dparts = [dlx, dlg, dq, dk, dv, drg]
    dwg, (s_out, r_small) = _in_proj_dw(dparts, hp, norm_gain, wg.shape,
                                        ride=_join_rides(_sibling_ride([f_out]), _pair_ride([g_small])))
    g_in = dwg.reshape(N_DEV, dwg.shape[1] // 2, dwg.shape[2])
    (r_in,) = _exchange_call(_pair_ride([g_in]), "grad_pair_exchange")
    q_in = _pair_sum(g_in, r_in, c_arr, 128, "grad_pair_sum_in")
    q_small = _pair_sum(g_small, r_small, c_arr, UNIT_ROWS, "grad_pair_sum_small")
    (grad_x, dmeta, dg1), (e_in, e_small) = _in_proj_dx(dparts, hp, dh2, norm_gain, wg, s_len, tm,
                                                        ride=_chip_ride([q_in, q_small]))
    f_in = _chip_sum(q_in, e_in, j_arr, 128, "grad_chip_sum_in")
    f_small = _chip_sum(q_small, e_small, j_arr, UNIT_ROWS, "grad_chip_sum_small", loss_part=loss_acc)
    tail = jnp.concatenate([_units(dmeta), _units(dg1), jnp.zeros((N_DEV, TAIL_ROWS - N_META - 1, LANES), F32)],
                           axis=1).reshape(N_DEV * TAIL_ROWS, LANES)
    s_in, o_small = _finish_exchange(f_in, jnp.concatenate([f_small, tail], axis=0))

    res_in = _adamw_big(w_in[0], f_in, s_in, m_w_in[0], v_w_in[0], c_arr, 256, "adamw_w_in")
    res_out = _adamw_big(w_out[0], f_out, s_out, m_w_out[0], v_w_out[0], c_arr, 256, "adamw_w_out")
    small_params = {
        "meta_tokens": (meta_tokens, m_meta_tokens, v_meta_tokens),
        "norm_gain": (norm_gain, m_norm_gain, v_norm_gain),
        "conv_w": (conv_w[0], m_conv_w[0], v_conv_w[0]),
        "conv_b": (conv_b, m_conv_b, v_conv_b),
        "w_rg": (w_rg[0], m_w_rg[0], v_w_rg[0]),
        "b_rg": (b_rg, m_b_rg, v_b_rg),
        "w_ig": (w_ig[0], m_w_ig[0], v_w_ig[0]),
        "b_ig": (b_ig, m_b_ig, v_b_ig),
        "lru_lambda": (lru_lambda, m_lru_lambda, v_lru_lambda),
        "ret_norm_gain": (ret_norm_gain, m_ret_norm_gain, v_ret_norm_gain),
        "final_norm_gain": (gain_f, m_final_norm_gain.reshape(1, d), v_final_norm_gain.reshape(1, d)),
    }
    res, loss = _adamw_small(j_arr, o_small, small_params)
    res["w_in"] = tuple(res_in)
    res["w_out"] = tuple(res_out)

    order = ["meta_tokens", "norm_gain", "w_in", "conv_w", "conv_b", "w_rg", "b_rg", "w_ig", "b_ig", "lru_lambda",
             "ret_norm_gain", "w_out", "final_norm_gain"]
    shapes = {"w_in": w_in.shape, "conv_w": conv_w.shape, "w_rg": w_rg.shape, "w_ig": w_ig.shape,
              "w_out": w_out.shape, "final_norm_gain": final_norm_gain.shape}
    outs = [loss, grad_x.reshape(x.shape)]
    for kind in range(4):
        for name in order:
            a = res[name][kind]
            outs.append(a.reshape(shapes[name]) if name in shapes else a)
    return tuple(outs)
```

```python
import functools

import jax
import jax.numpy as jnp
from jax import lax
from jax.experimental import pallas as pl
from jax.experimental.pallas import tpu as pltpu

F32 = jnp.float32
BF16 = jnp.bfloat16

N_META = 16
CHUNK = 128
PAD_ROWS = CHUNK - N_META
HEADS = 8
QK_DIM = 64
LANES = 128
SUBLANES = 8
LRU_C = 8.0
EPS = 1e-6
ROPE_BASE = 10000.0
ADAM_LR = 0.001
ADAM_B1 = 0.9
ADAM_B2 = 0.999
ADAM_EPS = 1e-08
ADAM_WD = 0.01
ADAM_STEP = 10
N_CHIPS = 4
N_DEV = 8
TOKEN_TILE = 384
VMEM_LIMIT = 58 * 1024 * 1024
MESH = pl.DeviceIdType.MESH

VMEM_SPEC = pl.BlockSpec(memory_space=pltpu.VMEM)
SMEM_SPEC = pl.BlockSpec(memory_space=pltpu.SMEM)
ANY_SPEC = pl.BlockSpec(memory_space=pl.ANY)

ROW_WR, ROW_WI, ROW_META, ROW_CONV, ROW_VEC, UNIT_ROWS = 0, 128, 256, 272, 276, 288
N_VEC = 7
ROW_LOSS = ROW_VEC + N_VEC
TAIL_ROWS = 24


def _dot(a, b):
    return jnp.dot(a, b, preferred_element_type=F32)


def _dot_nt(a, b):
    return lax.dot_general(a, b, (((1,), (1,)), ((), ())), preferred_element_type=F32)


def _dot_tn(a, b):
    return lax.dot_general(a, b, (((0,), (0,)), ((), ())), preferred_element_type=F32)


def _sigmoid(x):
    return 0.5 * jnp.tanh(0.5 * x) + 0.5


def _shift_down(x, prev8, s):
    rolled = pltpu.roll(x, s, 0)
    rows = lax.broadcasted_iota(jnp.int32, (SUBLANES, x.shape[1]), 0)
    top = jnp.where(rows < s, pltpu.roll(prev8, s, 0), rolled[0:SUBLANES])
    return jnp.concatenate([top, rolled[SUBLANES:]], axis=0)


def _shift_up(x, next8, s):
    n = x.shape[0]
    rolled = pltpu.roll(x, n - s, 0)
    rows = lax.broadcasted_iota(jnp.int32, (SUBLANES, x.shape[1]), 0)
    bot = jnp.where(rows >= SUBLANES - s, pltpu.roll(next8, SUBLANES - s, 0), rolled[n - SUBLANES:n])
    return jnp.concatenate([rolled[:n - SUBLANES], bot], axis=0)


def _rot_partner(t):
    w = t.shape[1]
    lane = lax.broadcasted_iota(jnp.int32, t.shape, 1)
    first = (lane % QK_DIM) < (QK_DIM // 2)
    return jnp.where(first, pltpu.roll(t, w - QK_DIM // 2, 1), pltpu.roll(t, QK_DIM // 2, 1))


def _tile_lanes(t, reps):
    return jnp.concatenate([t] * reps, axis=1)


class _Ride:
    def __init__(self, srcs, dst_shapes, n_copies, make):
        self.srcs, self.dst_shapes, self.n_copies, self.make = list(srcs), list(dst_shapes), n_copies, make


def _join_rides(a, b):
    def make(src, dst, send_sems, recv_sems, base):
        na, da = len(a.srcs), len(a.dst_shapes)
        return (a.make(src[:na], dst[:da], send_sems, recv_sems, base)
                + b.make(src[na:], dst[da:], send_sems, recv_sems, base + a.n_copies))

    return _Ride(a.srcs + b.srcs, a.dst_shapes + b.dst_shapes, a.n_copies + b.n_copies, make)


def _position():
    x, y, c = lax.axis_index("x"), lax.axis_index("y"), lax.axis_index("c")
    return x, y, c, [(1 - x, y), (x, 1 - y), (1 - x, 1 - y)]


def _remote(src, dst, send_sems, recv_sems, k, to):
    return pltpu.make_async_remote_copy(src_ref=src, dst_ref=dst, send_sem=send_sems.at[k], recv_sem=recv_sems.at[k],
                                        device_id=to, device_id_type=MESH)


def _pair_ride(bufs):
    def make(src, dst, send_sems, recv_sems, base):
        x, y, c, _ = _position()
        return [_remote(src[b].at[2 * jj + 1 - c], dst[b].at[jj], send_sems, recv_sems, base + b * N_CHIPS + jj,
                        (x, y, 1 - c)) for b in range(len(bufs)) for jj in range(N_CHIPS)]

    shapes = [jax.ShapeDtypeStruct((N_CHIPS,) + b.shape[1:], b.dtype) for b in bufs]
    return _Ride(bufs, shapes, N_CHIPS * len(bufs), make)


def _chip_ride(bufs):
    def make(src, dst, send_sems, recv_sems, base):
        x, y, c, chips = _position()
        return [_remote(src[b].at[2 * cx + cy], dst[b].at[2 * x + y], send_sems, recv_sems, base + b * 3 + k,
                        (cx, cy, c)) for b in range(len(bufs)) for k, (cx, cy) in enumerate(chips)]

    shapes = [jax.ShapeDtypeStruct(b.shape, b.dtype) for b in bufs]
    return _Ride(bufs, shapes, 3 * len(bufs), make)


def _sibling_ride(bufs):
    def make(src, dst, send_sems, recv_sems, base):
        x, y, c, _ = _position()
        return [_remote(src[b], dst[b], send_sems, recv_sems, base + b, (x, y, 1 - c)) for b in range(len(bufs))]

    shapes = [jax.ShapeDtypeStruct(b.shape, b.dtype) for b in bufs]
    return _Ride(bufs, shapes, len(bufs), make)


def _exchange_call(ride, name):
    n_src, n_dst = len(ride.srcs), len(ride.dst_shapes)

    def body(*refs):
        copies = ride.make(refs[:n_src], refs[n_src:n_src + n_dst], refs[-2], refs[-1], 0)
        for cp in copies:
            cp.start()
        for cp in copies:
            cp.wait()

    return pl.pallas_call(
        body,
        name=name,
        in_specs=[ANY_SPEC] * n_src,
        out_specs=[ANY_SPEC] * n_dst,
        out_shape=ride.dst_shapes,
        scratch_shapes=[pltpu.SemaphoreType.DMA((ride.n_copies,)), pltpu.SemaphoreType.DMA((ride.n_copies,))],
    )(*ride.srcs)


def _hosted_call(body, ride, n_steps, *, name, in_specs, out_specs, out_shape, scratch_shapes, args):
    params = pltpu.CompilerParams(dimension_semantics=("arbitrary",), vmem_limit_bytes=VMEM_LIMIT)
    if ride is None:
        res = pl.pallas_call(body, name=name, grid=(n_steps,), in_specs=list(in_specs), out_specs=list(out_specs),
                             out_shape=list(out_shape), scratch_shapes=list(scratch_shapes),
                             compiler_params=params)(*args)
        return list(res), []
    sizes = [len(in_specs), len(ride.srcs), len(out_specs), len(ride.dst_shapes), len(scratch_shapes), 2]

    def hosted(*refs):
        groups, pos = [], 0
        for n in sizes:
            groups.append(refs[pos:pos + n])
            pos += n
        ins, rin, outs, rout, scr, (send_sems, recv_sems) = groups
        i = pl.program_id(0)

        @pl.when(i == 0)
        def _():
            for cp in ride.make(rin, rout, send_sems, recv_sems, 0):
                cp.start()

        body(*ins, *outs, *scr)

        @pl.when(i == n_steps - 1)
        def _():
            for cp in ride.make(rin, rout, send_sems, recv_sems, 0):
                cp.wait()

    n_out = len(out_specs)
    res = pl.pallas_call(
        hosted,
        name=name,
        grid=(n_steps,),
        in_specs=list(in_specs) + [ANY_SPEC] * len(ride.srcs),
        out_specs=list(out_specs) + [ANY_SPEC] * len(ride.dst_shapes),
        out_shape=list(out_shape) + ride.dst_shapes,
        scratch_shapes=list(scratch_shapes) + [pltpu.SemaphoreType.DMA((ride.n_copies,)),
                                               pltpu.SemaphoreType.DMA((ride.n_copies,))],
        compiler_params=params,
    )(*args, *ride.srcs)
    return list(res[:n_out]), list(res[n_out:])


def _gather_weights(w_in, small):
    r_in, c_in = w_in.shape
    h_in = r_in // 2
    q_in = h_in // 2

    def body(win_ref, small_ref, wg_ref, sg_ref, send_sems, recv_sems):
        x, y, c, chips = _position()
        j = 2 * x + y
        sibling = (x, y, 1 - c)
        xn, yn, dg = chips
        jx, jy, jd = (2 * cx + cy for cx, cy in chips)

        wg_ref[j] = win_ref[...].astype(BF16)
        sg_ref[j] = small_ref[...]

        def half(jj, cc):
            return wg_ref.at[jj, pl.ds(cc * h_in, h_in), :]

        def quarter(jj, qq):
            return wg_ref.at[jj, pl.ds(c * h_in + qq * q_in, q_in), :]

        def copy(k, ref, to):
            return _remote(ref, ref, send_sems, recv_sems, k, to)

        first = [copy(0, quarter(j, 0), (*xn, c)), copy(2, quarter(j, 1), (*yn, c)),
                 copy(1, quarter(j, 1), (*xn, c)), copy(3, quarter(j, 0), (*yn, c))]
        first += [copy(9 + k, sg_ref.at[j], (cx, cy, c)) for k, (cx, cy) in enumerate(chips)]
        for cp in first:
            cp.start()
        copy(0, quarter(jx, 0), sibling).wait_recv()
        along_y = copy(4, quarter(jx, 0), (*yn, c))
        along_y.start()
        copy(2, quarter(jy, 1), sibling).wait_recv()
        along_x = copy(5, quarter(jy, 1), (*xn, c))
        along_x.start()
        copy(1, quarter(jx, 1), sibling).wait_recv()
        to_sib = [copy(6, half(jx, c), sibling)]
        to_sib[-1].start()
        copy(3, quarter(jy, 0), sibling).wait_recv()
        to_sib.append(copy(7, half(jy, c), sibling))
        to_sib[-1].start()
        copy(4, quarter(jd, 0), sibling).wait_recv()
        copy(5, quarter(jd, 1), sibling).wait_recv()
        to_sib.append(copy(8, half(jd, c), sibling))
        to_sib[-1].start()
        for k, jk in enumerate((jx, jy, jd)):
            copy(6 + k, half(jk, 1 - c), sibling).wait_recv()
            copy(9 + k, sg_ref.at[jk], sibling).wait_recv()
        for cp in first + [along_y, along_x] + to_sib:
            cp.wait_send()

    return pl.pallas_call(
        body,
        name="gather_weights",
        out_shape=(jax.ShapeDtypeStruct((N_CHIPS, r_in, c_in), BF16),
                   jax.ShapeDtypeStruct((N_CHIPS,) + small.shape, F32)),
        in_specs=[VMEM_SPEC, VMEM_SPEC],
        out_specs=(VMEM_SPEC, VMEM_SPEC),
        scratch_shapes=[pltpu.SemaphoreType.DMA((12,)), pltpu.SemaphoreType.DMA((12,))],
        compiler_params=pltpu.CompilerParams(vmem_limit_bytes=VMEM_LIMIT),
    )(w_in, small)


def _proj_segments(d_lru, d_qk, d_ret, chunk_w):
    widths = [d_lru, d_lru, d_qk, d_qk, d_ret, d_ret]
    segs, col = [], 0
    for w in widths:
        parts, off = [], 0
        while off < w:
            jj, inner = divmod(col + off, chunk_w)
            take = min(w - off, chunk_w - inner)
            parts.append((jj, inner, off, take))
            off += take
        segs.append(parts)
        col += w
    return segs


def _in_proj(x2, meta, gain, wg, cos_t, sin_t, w_out, tm, d_lru, d_qk, d_ret):
    s_len, d = x2.shape
    tp = s_len + CHUNK
    nt, nb = tp // tm, tm // CHUNK
    segs = _proj_segments(d_lru, d_qk, d_ret, wg.shape[2])
    widths = [d_lru, d_lru, d_qk, d_qk, d_ret, d_ret]
    r_out, c_out = w_out.shape
    h_out = r_out // 2
    fwd_step = min(6, nt - 1)

    def gather_w_out(i, wout_ref, wo_ref, wob, send_sems, recv_sems, local_sem):
        x, y, c, chips = _position()
        j = 2 * x + y
        sibling = (x, y, 1 - c)

        def half(jj, cc):
            return wo_ref.at[jj, pl.ds(cc * h_out, h_out), :]

        local = pltpu.make_async_copy(wob, wo_ref.at[j], local_sem)
        first = [_remote(wob.at[pl.ds(c * h_out, h_out), :], half(j, c), send_sems, recv_sems, k, (cx, cy, c))
                 for k, (cx, cy) in enumerate(chips)]
        passed = [_remote(half(2 * cx + cy, c), half(2 * cx + cy, c), send_sems, recv_sems, 3 + k, sibling)
                  for k, (cx, cy) in enumerate(chips)]

        @pl.when(i == 0)
        def _():
            wob[...] = wout_ref[...].astype(BF16)
            local.start()
            for cp in first:
                cp.start()

        @pl.when(i == fwd_step)
        def _():
            for k, (cx, cy) in enumerate(chips):
                _remote(half(2 * cx + cy, c), half(2 * cx + cy, c), send_sems, recv_sems, k, sibling).wait_recv()
                passed[k].start()

        @pl.when(i == nt - 1)
        def _():
            for k, (cx, cy) in enumerate(chips):
                jk = 2 * cx + cy
                _remote(half(jk, 1 - c), half(jk, 1 - c), send_sems, recv_sems, 3 + k, sibling).wait_recv()
            for cp in first + passed:
                cp.wait_send()
            local.wait()

    def body(*refs):
        xb = refs[:nb]
        meta_ref, g_ref, w_ref, cos_ref, sin_ref, wout_ref = refs[nb:nb + 6]
        hp_ref = refs[nb + 6]
        outs = refs[nb + 7:nb + 13]
        wo_ref, wob, send_sems, recv_sems, local_sem = refs[nb + 13:]
        i = pl.program_id(0)
        gather_w_out(i, wout_ref, wo_ref, wob, send_sems, recv_sems, local_sem)
        blocks = [r[...] for r in xb]
        head = jnp.concatenate([jnp.zeros((PAD_ROWS, d), F32), meta_ref[...]], axis=0)
        blocks[0] = jnp.where(i == 0, head, blocks[0])
        h = jnp.concatenate(blocks, axis=0)
        hp_ref[...] = h
        rinv = lax.rsqrt(jnp.mean(h * h, axis=-1, keepdims=True) + EPS)
        u = ((h * rinv) * g_ref[...]).astype(BF16)
        for out_ref, parts in zip(outs, segs):
            for jj, inner, off, take in parts:
                out_ref[:, off:off + take] = _dot(u, w_ref[jj, :, inner:inner + take])
        cos = _tile_lanes(cos_ref[...], d_qk // LANES)
        sin = _tile_lanes(sin_ref[...], d_qk // LANES)
        q = outs[2][...]
        outs[2][...] = q * cos + _rot_partner(q) * sin
        k = outs[3][...]
        outs[3][...] = (k * cos + _rot_partner(k) * sin) * (QK_DIM ** -0.5)

    x_specs = [pl.BlockSpec((CHUNK, d), functools.partial(lambda i, b: (jnp.maximum(i * nb + b - 1, 0), 0), b=b))
               for b in range(nb)]
    tile = lambda w: pl.BlockSpec((tm, w), lambda i: (i, 0))
    return pl.pallas_call(
        body,
        name="in_proj",
        grid=(nt,),
        in_specs=x_specs + [pl.BlockSpec(meta.shape, lambda i: (0, 0)),
                            pl.BlockSpec(gain.shape, lambda i: (0, 0)),
                            pl.BlockSpec(wg.shape, lambda i: (0, 0, 0)),
                            tile(LANES), tile(LANES),
                            pl.BlockSpec(w_out.shape, lambda i: (0, 0))],
        out_specs=[tile(d)] + [tile(w) for w in widths] + [ANY_SPEC],
        out_shape=[jax.ShapeDtypeStruct((tp, d), F32)] + [jax.ShapeDtypeStruct((tp, w), F32) for w in widths]
                  + [jax.ShapeDtypeStruct((N_CHIPS, r_out, c_out), BF16)],
        scratch_shapes=[pltpu.VMEM((r_out, c_out), BF16), pltpu.SemaphoreType.DMA((6,)),
                        pltpu.SemaphoreType.DMA((6,)), pltpu.SemaphoreType.DMA],
        compiler_params=pltpu.CompilerParams(dimension_semantics=("arbitrary",), vmem_limit_bytes=VMEM_LIMIT),
    )(*([x2] * nb), meta, gain, wg, cos_t, sin_t, w_out)


def _to_groups(ref3, x):
    for g in range(ref3.shape[0]):
        ref3[g] = x[:, g * LANES:(g + 1) * LANES]


def _from_groups(ref3):
    return jnp.concatenate([ref3[g] for g in range(ref3.shape[0])], axis=1)


def _segment_scan(a3, u3, out3, p3, carry, tm, reverse):
    groups = a3.shape[0]
    seg = tm // SUBLANES

    def step(j, state):
        hs, ps = state
        rows = pl.ds((seg - 1 - j) if reverse else j, SUBLANES, stride=seg)
        new_h, new_p = [], []
        for g in range(groups):
            a = a3[g, rows, :]
            h = a * hs[g] + u3[g, rows, :]
            p = ps[g] * a
            out3[g, rows, :] = h
            p3[g, rows, :] = p
            new_h.append(h)
            new_p.append(p)
        return tuple(new_h), tuple(new_p)

    zeros = tuple(jnp.zeros((SUBLANES, LANES), F32) for _ in range(groups))
    ones = tuple(jnp.ones((SUBLANES, LANES), F32) for _ in range(groups))
    lax.fori_loop(0, seg, step, (zeros, ones))
    carries = [carry[:, g * LANES:(g + 1) * LANES] for g in range(groups)]
    for s in (reversed(range(SUBLANES)) if reverse else range(SUBLANES)):
        rows = slice(s * seg, (s + 1) * seg)
        edge = s * seg if reverse else (s + 1) * seg - 1
        for g in range(groups):
            out3[g, rows, :] = out3[g, rows, :] + p3[g, rows, :] * carries[g]
            carries[g] = out3[g, edge:edge + 1, :]
    return jnp.concatenate(carries, axis=1)


def _softplus_neg(lam):
    z = -lam
    e = jnp.exp(-jnp.abs(z))
    e1 = 1.0 + e
    log1p_e = jnp.where(e1 == 1.0, e, jnp.log(e1) * (e / (e1 - 1.0)))
    return jnp.maximum(z, 0.0) + log1p_e


def _lru_fwd(lx, lg, cw, cb, wr, br, wi, bi, lam, tm):
    tp, w = lx.shape
    nt = tp // tm
    per8 = tm // SUBLANES
    n_heads = wr.shape[0]

    def body(lx_ref, lxp_ref, lg_ref, cw_ref, cb_ref, wr_ref, br_ref, wi_ref, bi_ref, lam_ref,
             hl_ref, y_ref, xc_ref, r_ref, ig_ref, a_ref, beta_ref, w4_ref, a_s, u_s, h_s, p_s, carry):
        i = pl.program_id(0)

        @pl.when(i == 0)
        def _():
            carry[...] = jnp.zeros_like(carry)

        lxv = lx_ref[...]
        prev8 = jnp.where(i == 0, 0.0, lxp_ref[...])
        xc = cb_ref[...] + _shift_down(lxv, prev8, 3) * cw_ref[0:1, :]
        xc = xc + _shift_down(lxv, prev8, 2) * cw_ref[1:2, :]
        xc = xc + _shift_down(lxv, prev8, 1) * cw_ref[2:3, :]
        xc = xc + lxv * cw_ref[3:4, :]
        xc_ref[...] = xc
        pre_r, pre_i = [], []
        for hd in range(n_heads):
            xh = xc[:, hd * LANES:(hd + 1) * LANES].astype(BF16)
            pre_r.append(_dot(xh, wr_ref[hd].astype(BF16)))
            pre_i.append(_dot(xh, wi_ref[hd].astype(BF16)))
        r = _sigmoid(jnp.concatenate(pre_r, axis=1) + br_ref[...])
        ig = _sigmoid(jnp.concatenate(pre_i, axis=1) + bi_ref[...])
        r_ref[...] = r
        ig_ref[...] = ig
        log_a = (-LRU_C * r) * _softplus_neg(lam_ref[...])
        a = jnp.exp(log_a)
        a_ref[...] = a
        zz = -2.0 * log_a
        series = zz * (1.0 - zz * (0.5 - zz * (1.0 / 6.0)))
        a2 = a * a
        beta2 = jnp.maximum(jnp.where(zz < 0.015625, series, 1.0 - a2), 1e-37)
        rsb = lax.rsqrt(beta2)
        beta = beta2 * rsb
        beta_ref[...] = beta
        w4_ref[...] = a2 * rsb
        row = lax.broadcasted_iota(jnp.int32, (tm, 1), 0) + i * tm
        _to_groups(a_s, a)
        _to_groups(u_s, jnp.where(row >= PAD_ROWS, beta * ig * xc, 0.0))
        carry[0:1, :] = _segment_scan(a_s, u_s, h_s, p_s, carry[0:1, :], tm, reverse=False)
        hl = _from_groups(h_s)
        hl_ref[...] = hl
        g = lg_ref[...]
        y_ref[...] = (hl * (g * _sigmoid(g))).astype(BF16)

    tile = pl.BlockSpec((tm, w), lambda i: (i, 0))
    prev = pl.BlockSpec((SUBLANES, w), lambda i: (jnp.maximum(i * per8 - 1, 0), 0))
    vec = pl.BlockSpec((1, w), lambda i: (0, 0))
    mat = pl.BlockSpec(wr.shape, lambda i: (0, 0, 0))
    f32_out = jax.ShapeDtypeStruct((tp, w), F32)
    return pl.pallas_call(
        body,
        name="lru_fwd",
        grid=(nt,),
        in_specs=[tile, prev, tile, pl.BlockSpec(cw.shape, lambda i: (0, 0)), vec, mat, vec, mat, vec, vec],
        out_specs=[tile] * 8,
        out_shape=[f32_out, jax.ShapeDtypeStruct((tp, w), BF16)] + [f32_out] * 6,
        scratch_shapes=[pltpu.VMEM((w // LANES, tm, LANES), F32)] * 4 + [pltpu.VMEM((SUBLANES, w), F32)],
        compiler_params=pltpu.CompilerParams(dimension_semantics=("arbitrary",), vmem_limit_bytes=VMEM_LIMIT),
    )(lx, lx, lg, cw, cb, wr, br, wi, bi, lam)


def _ret_tables():
    log_g = jnp.log1p(-jnp.exp2(-5.0 - jnp.arange(HEADS, dtype=F32)))
    idx = jnp.arange(CHUNK, dtype=F32)
    diff = idx[:, None] - idx[None, :]
    dmask = jnp.where(diff[None] >= 0.0, jnp.exp(jnp.maximum(diff, 0.0)[None] * log_g[:, None, None]), 0.0)
    kdec = jnp.repeat(jnp.exp((CHUNK - 1.0 - idx)[:, None] * log_g[None, :]), QK_DIM, axis=1)
    qdec = jnp.repeat(jnp.exp((idx + 1.0)[:, None] * log_g[None, :]), QK_DIM, axis=1)
    g_chunk = jnp.exp(CHUNK * log_g)
    g_rows = jnp.repeat(g_chunk, QK_DIM).reshape(HEADS // 2, 2 * QK_DIM, 1)
    g_state = jnp.broadcast_to(g_rows, (HEADS // 2, 2 * QK_DIM, 2 * LANES))
    r_head = jnp.arange(2 * QK_DIM)[:, None] // QK_DIM
    c_head = jnp.arange(2 * LANES)[None, :] // LANES
    block_diag = (r_head == c_head).astype(F32)
    return dmask, qdec, kdec, g_state, block_diag


def _head_norm(o_h):
    mu = jnp.mean(o_h, axis=-1, keepdims=True)
    oc = o_h - mu
    var = jnp.mean(oc * oc, axis=-1, keepdims=True)
    rstd = lax.rsqrt(var + EPS)
    return oc * rstd, rstd


def _ret_fwd(q, k, v, rg, gain, tables, tm):
    tp, d_qk = q.shape
    d_ret = v.shape[1]
    n_ch = tp // CHUNK
    cps = tm // CHUNK
    n_pairs = HEADS // 2
    dmask, qdec, kdec, g_state, block_diag = tables

    def body(q_ref, k_ref, v_ref, rg_ref, gain_ref, dm_ref, qd_ref, kd_ref, gs_ref, bd_ref,
             o_ref, y_ref, rp_ref, state):
        n = pl.program_id(0)

        @pl.when(n == 0)
        def _():
            state[...] = jnp.zeros_like(state)

        lane = lax.broadcasted_iota(jnp.int32, (CHUNK, LANES), 1)
        for ci in range(cps):
            rs = slice(ci * CHUNK, (ci + 1) * CHUNK)
            for p in range(n_pairs):
                qs = slice(p * LANES, (p + 1) * LANES)
                vs = slice(p * 2 * LANES, (p + 1) * 2 * LANES)
                qp, kp = q_ref[rs, qs], k_ref[rs, qs]
                vb = v_ref[rs, vs].astype(BF16)
                kb = kp.astype(BF16)
                qd = (qp * qd_ref[:, qs]).astype(BF16)
                kd = (kp * kd_ref[:, qs]).astype(BF16)
                st = state[p]
                st_b = st.astype(BF16)
                rp_ref[ci, p] = st_b
                cross = _dot(qd, st_b)
                for e in range(2):
                    hd = 2 * p + e
                    hs = slice(hd * LANES, (hd + 1) * LANES)
                    es = slice(e * LANES, (e + 1) * LANES)
                    qm = jnp.where((lane // QK_DIM) == e, qp, 0.0).astype(BF16)
                    s = _dot_nt(qm, kb) * dm_ref[hd]
                    o_h = _dot(s.astype(BF16), vb[:, es]) + cross[:, es]
                    o_ref[rs, hs] = o_h
                    xhat, _ = _head_norm(o_h)
                    g = rg_ref[rs, hs]
                    y_ref[rs, hs] = ((xhat * gain_ref[:, hs]) * (g * _sigmoid(g))).astype(BF16)
                state[p] = gs_ref[p] * st + bd_ref[...] * _dot_tn(kd, vb)

    ch = lambda w: pl.BlockSpec((tm, w), lambda n: (n, 0))
    const2 = lambda a: pl.BlockSpec(a.shape, lambda n: (0, 0))
    const3 = lambda a: pl.BlockSpec(a.shape, lambda n: (0, 0, 0))
    return pl.pallas_call(
        body,
        name="ret_fwd",
        grid=(n_ch // cps,),
        in_specs=[ch(d_qk), ch(d_qk), ch(d_ret), ch(d_ret), const2(gain), const3(dmask), const2(qdec), const2(kdec),
                  const3(g_state), const2(block_diag)],
        out_specs=[ch(d_ret), ch(d_ret),
                   pl.BlockSpec((cps, n_pairs, 2 * QK_DIM, 2 * LANES), lambda n: (n, 0, 0, 0))],
        out_shape=[jax.ShapeDtypeStruct((tp, d_ret), F32), jax.ShapeDtypeStruct((tp, d_ret), BF16),
                   jax.ShapeDtypeStruct((n_ch, n_pairs, 2 * QK_DIM, 2 * LANES), BF16)],
        scratch_shapes=[pltpu.VMEM((n_pairs, 2 * QK_DIM, 2 * LANES), F32)],
        compiler_params=pltpu.CompilerParams(dimension_semantics=("arbitrary",), vmem_limit_bytes=VMEM_LIMIT),
    )(q, k, v, rg, gain, dmask, qdec, kdec, g_state, block_diag)


def _out_proj_loss(y_lru, y_ret, hp, tgt, wo, gain_f, tm):
    tp, d = hp.shape
    w_lru = y_lru.shape[1]
    w_mix = wo.shape[0]
    nt, nb = tp // tm, tm // CHUNK

    def body(*refs):
        yl_ref, yr_ref, hp_ref = refs[:3]
        tb = refs[3:3 + nb]
        wo_ref, gf_ref = refs[3 + nb:5 + nb]
        dh2_ref, dyl_ref, dyr_ref, dwo_ref, dgf_ref, loss_ref = refs[5 + nb:11 + nb]
        yl_s, yr_s, d_s = refs[11 + nb:]
        i = pl.program_id(0)

        @pl.when(i == 0)
        def _():
            dwo_ref[...] = jnp.zeros_like(dwo_ref)
            dgf_ref[...] = jnp.zeros_like(dgf_ref)
            loss_ref[...] = jnp.zeros_like(loss_ref)

        yl, yr = yl_ref[...], yr_ref[...]
        h2 = hp_ref[...] + _dot(yl, wo_ref[0:w_lru, :]) + _dot(yr, wo_ref[w_lru:w_mix, :])
        rinv = lax.rsqrt(jnp.mean(h2 * h2, axis=-1, keepdims=True) + EPS)
        nrm = h2 * rinv
        gf = gf_ref[...]
        tgt_v = jnp.concatenate([r[...] for r in tb], axis=0)
        row = lax.broadcasted_iota(jnp.int32, (tm, 1), 0) + i * tm
        err = jnp.where(row >= CHUNK, nrm * gf - tgt_v, 0.0)
        loss_ref[...] += 0.5 * jnp.sum(jnp.mean(err * err, axis=-1, keepdims=True))
        dout = err * (1.0 / d)
        dgf_ref[...] += jnp.sum(dout * nrm, axis=0, keepdims=True)
        dn = dout * gf
        dh2 = rinv * (dn - nrm * jnp.mean(dn * nrm, axis=-1, keepdims=True))
        dh2_ref[...] = dh2
        dh2b = dh2.astype(BF16)
        dyl_ref[...] = _dot_nt(dh2b, wo_ref[0:w_lru, :])
        dyr_ref[...] = _dot_nt(dh2b, wo_ref[w_lru:w_mix, :])

        def add_dwo(left, right, dd):
            dwo_ref[0:w_lru, :] += _dot_tn(left, dd)
            dwo_ref[w_lru:w_mix, :] += _dot_tn(right, dd)

        odd = i % 2 == 1

        @pl.when(odd)
        def _():
            add_dwo(jnp.concatenate([yl_s[...], yl], axis=0), jnp.concatenate([yr_s[...], yr], axis=0),
                    jnp.concatenate([d_s[...], dh2b], axis=0))

        @pl.when(jnp.logical_not(odd) & (i < nt - 1))
        def _():
            yl_s[...], yr_s[...], d_s[...] = yl, yr, dh2b

        if nt % 2 == 1:
            @pl.when(i == nt - 1)
            def _():
                add_dwo(yl, yr, dh2b)

    tile = lambda w: pl.BlockSpec((tm, w), lambda i: (i, 0))
    t_specs = [pl.BlockSpec((CHUNK, d), functools.partial(lambda i, b: (jnp.maximum(i * nb + b - 1, 0), 0), b=b))
               for b in range(nb)]
    return pl.pallas_call(
        body,
        name="out_proj_loss",
        grid=(nt,),
        in_specs=[tile(w_lru), tile(w_mix - w_lru), tile(d)] + t_specs +
                 [pl.BlockSpec(wo.shape, lambda i: (0, 0)), pl.BlockSpec(gain_f.shape, lambda i: (0, 0))],
        out_specs=[tile(d), tile(w_lru), tile(w_mix - w_lru), pl.BlockSpec(wo.shape, lambda i: (0, 0)),
                   pl.BlockSpec((1, d), lambda i: (0, 0)), pl.BlockSpec((SUBLANES, LANES), lambda i: (0, 0))],
        out_shape=[jax.ShapeDtypeStruct((tp, d), F32), jax.ShapeDtypeStruct((tp, w_lru), F32),
                   jax.ShapeDtypeStruct((tp, w_mix - w_lru), F32), jax.ShapeDtypeStruct(wo.shape, F32),
                   jax.ShapeDtypeStruct((1, d), F32), jax.ShapeDtypeStruct((SUBLANES, LANES), F32)],
        scratch_shapes=[pltpu.VMEM((tm, w_lru), BF16), pltpu.VMEM((tm, w_mix - w_lru), BF16),
                        pltpu.VMEM((tm, d), BF16)],
        compiler_params=pltpu.CompilerParams(dimension_semantics=("arbitrary",), vmem_limit_bytes=VMEM_LIMIT),
    )(y_lru, y_ret, hp, *([tgt] * nb), wo, gain_f)


def _ret_bwd(q, k, v, rg, o, rprev, dy, gain, cos_t, sin_t, tables, tm, ride=None):
    tp, d_qk = q.shape
    d_ret = v.shape[1]
    n_ch = tp // CHUNK
    cps = tm // CHUNK
    n_pairs = HEADS // 2
    dmask, qdec, kdec, g_state, block_diag = tables

    dmask_t = jnp.swapaxes(dmask, 1, 2)

    def body(q_ref, k_ref, v_ref, rg_ref, o_ref, rp_ref, dy_ref, gain_ref, cos_ref, sin_ref,
             dm_ref, dmt_ref, qd_ref, kd_ref, gs_ref, bd_ref, dq_ref, dk_ref, dv_ref, drg_ref, dgain_ref, dstate):
        n = pl.program_id(0)

        @pl.when(n == 0)
        def _():
            dstate[...] = jnp.zeros_like(dstate)
            dgain_ref[...] = jnp.zeros_like(dgain_ref)

        lane = lax.broadcasted_iota(jnp.int32, (CHUNK, LANES), 1)
        for ci in reversed(range(cps)):
            rs = slice(ci * CHUNK, (ci + 1) * CHUNK)
            dq_parts, dk_parts = [], []
            for p in range(n_pairs):
                qs = slice(p * LANES, (p + 1) * LANES)
                vs = slice(p * 2 * LANES, (p + 1) * 2 * LANES)
                do_parts = []
                for e in range(2):
                    hd = 2 * p + e
                    hs = slice(hd * LANES, (hd + 1) * LANES)
                    xhat, rstd = _head_norm(o_ref[rs, hs])
                    g = rg_ref[rs, hs]
                    sg = _sigmoid(g)
                    dyh = dy_ref[rs, hs]
                    gn = gain_ref[:, hs]
                    d_on = dyh * (g * sg)
                    drg_ref[rs, hs] = (dyh * (xhat * gn) * (sg * (1.0 + g * (1.0 - sg)))).astype(BF16)
                    dgain_ref[:, hs] += jnp.sum(d_on * xhat, axis=0, keepdims=True)
                    dxh = d_on * gn
                    do_parts.append(rstd * (dxh - jnp.mean(dxh, axis=-1, keepdims=True)
                                            - xhat * jnp.mean(dxh * xhat, axis=-1, keepdims=True)))
                do_b = jnp.concatenate(do_parts, axis=1).astype(BF16)
                qp, kp = q_ref[rs, qs], k_ref[rs, qs]
                vb = v_ref[rs, vs].astype(BF16)
                kb = kp.astype(BF16)
                qd = (qp * qd_ref[:, qs]).astype(BF16)
                kd = (kp * kd_ref[:, qs]).astype(BF16)
                dst = dstate[p]
                dst_b = dst.astype(BF16)
                dqp = _dot_nt(do_b, rp_ref[ci, p]) * qd_ref[:, qs]
                dkp = _dot_nt(vb, dst_b) * kd_ref[:, qs]
                dvp = _dot(kd, dst_b)
                dv_parts = []
                for e in range(2):
                    hd = 2 * p + e
                    es = slice(e * LANES, (e + 1) * LANES)
                    mine = (lane // QK_DIM) == e
                    qm = jnp.where(mine, qp, 0.0).astype(BF16)
                    km = jnp.where(mine, kp, 0.0).astype(BF16)
                    ds = (_dot_nt(do_b[:, es], vb[:, es]) * dm_ref[hd]).astype(BF16)
                    s_t = (_dot_nt(kb, qm) * dmt_ref[hd]).astype(BF16)
                    ds_t = (_dot_nt(vb[:, es], do_b[:, es]) * dmt_ref[hd]).astype(BF16)
                    dv_parts.append(dvp[:, es] + _dot(s_t, do_b[:, es]))
                    dqp = dqp + _dot(ds, km)
                    dkp = dkp + _dot(ds_t, qm)
                dv_ref[rs, vs] = jnp.concatenate(dv_parts, axis=1).astype(BF16)
                dstate[p] = gs_ref[p] * dst + bd_ref[...] * _dot_tn(qd, do_b)
                dq_parts.append(dqp)
                dk_parts.append(dkp)
            cos = _tile_lanes(cos_ref[rs, :], d_qk // LANES)
            sin = _tile_lanes(sin_ref[rs, :], d_qk // LANES)
            dq = jnp.concatenate(dq_parts, axis=1)
            dk = jnp.concatenate(dk_parts, axis=1) * (QK_DIM ** -0.5)
            dq_ref[rs, :] = (dq * cos + _rot_partner(dq * sin)).astype(BF16)
            dk_ref[rs, :] = (dk * cos + _rot_partner(dk * sin)).astype(BF16)

    last = n_ch // cps - 1
    ch = lambda w: pl.BlockSpec((tm, w), lambda n: (last - n, 0))
    const2 = lambda a: pl.BlockSpec(a.shape, lambda n: (0, 0))
    const3 = lambda a: pl.BlockSpec(a.shape, lambda n: (0, 0, 0))
    return _hosted_call(
        body, ride, n_ch // cps,
        name="ret_bwd",
        in_specs=[ch(d_qk), ch(d_qk), ch(d_ret), ch(d_ret), ch(d_ret),
                  pl.BlockSpec((cps, n_pairs, 2 * QK_DIM, 2 * LANES), lambda n: (last - n, 0, 0, 0)),
                  ch(d_ret), const2(gain), ch(LANES), ch(LANES),
                  const3(dmask), const3(dmask_t), const2(qdec), const2(kdec), const3(g_state), const2(block_diag)],
        out_specs=[ch(d_qk), ch(d_qk), ch(d_ret), ch(d_ret), pl.BlockSpec((1, d_ret), lambda n: (0, 0))],
        out_shape=[jax.ShapeDtypeStruct((tp, d_qk), BF16), jax.ShapeDtypeStruct((tp, d_qk), BF16),
                   jax.ShapeDtypeStruct((tp, d_ret), BF16), jax.ShapeDtypeStruct((tp, d_ret), BF16),
                   jax.ShapeDtypeStruct((1, d_ret), F32)],
        scratch_shapes=[pltpu.VMEM((n_pairs, 2 * QK_DIM, 2 * LANES), F32)],
        args=(q, k, v, rg, o, rprev, dy, gain, cos_t, sin_t, dmask, dmask_t, qdec, kdec, g_state, block_diag),
    )


def _lru_bwd(lx, lg, hl, dy, saved, cw, wr, wi, lam, tm, ride=None):
    tp, w = lx.shape
    nt = tp // tm
    per8 = tm // SUBLANES
    n_heads = wr.shape[0]

    def body(lx_ref, lg_ref, hl_ref, hlp_ref, dy_ref, xc_ref, r_ref, ig_ref, a_ref, beta_ref, w4_ref,
             cw_ref, wr_ref, wi_ref, lam_ref,
             dlx_ref, dlg_ref, dcw_ref, dcb_ref, dwr_ref, dbr_ref, dwi_ref, dbi_ref, dlam_ref,
             g_s, dh_s, b_s, p_s, carry, dxc_next, a_next):
        i = pl.program_id(0)
        first_tile = i == nt - 1

        @pl.when(i == 0)
        def _():
            carry[...] = jnp.zeros_like(carry)
            dxc_next[...] = jnp.zeros_like(dxc_next)
            a_next[...] = jnp.zeros_like(a_next)
            for r in (dcw_ref, dcb_ref, dwr_ref, dbr_ref, dwi_ref, dbi_ref, dlam_ref):
                r[...] = jnp.zeros_like(r)

        lxv = lx_ref[...]
        a, beta, r, ig, xc = a_ref[...], beta_ref[...], r_ref[...], ig_ref[...], xc_ref[...]
        g = lg_ref[...]
        sg = _sigmoid(g)
        dyv = dy_ref[...]
        hlv = hl_ref[...]
        dlg_ref[...] = (dyv * hlv * (sg * (1.0 + g * (1.0 - sg)))).astype(BF16)
        _to_groups(g_s, dyv * (g * sg))
        _to_groups(b_s, _shift_up(a, a_next[...], 1))
        carry[0:1, :] = _segment_scan(b_s, g_s, dh_s, p_s, carry[0:1, :], tm, reverse=True)
        a_next[...] = a[0:SUBLANES]
        dh = _from_groups(dh_s)
        hprev = _shift_down(hlv, jnp.where(first_tile, 0.0, hlp_ref[...]), 1)
        row = lax.broadcasted_iota(jnp.int32, (tm, 1), 0) + (nt - 1 - i) * tm
        du = jnp.where(row >= PAD_ROWS, dh, 0.0)
        da = dh * hprev
        dbeta = du * ig * xc
        d_ig = du * beta * xc
        dxc = du * beta * ig
        dloga = da * a - dbeta * w4_ref[...]
        lam_v = lam_ref[...]
        dlam_ref[...] += jnp.sum(dloga * r, axis=0, keepdims=True) * (LRU_C * _sigmoid(-lam_v))
        dpr = (dloga * (-LRU_C * _softplus_neg(lam_v))) * r * (1.0 - r)
        dpi = d_ig * ig * (1.0 - ig)
        dbr_ref[...] += jnp.sum(dpr, axis=0, keepdims=True)
        dbi_ref[...] += jnp.sum(dpi, axis=0, keepdims=True)
        dxc_parts = []
        for hd in range(n_heads):
            hs = slice(hd * LANES, (hd + 1) * LANES)
            xh = xc[:, hs].astype(BF16)
            dprh = dpr[:, hs].astype(BF16)
            dpih = dpi[:, hs].astype(BF16)
            dwr_ref[hd] += _dot_tn(xh, dprh)
            dwi_ref[hd] += _dot_tn(xh, dpih)
            dxc_parts.append(_dot_nt(dprh, wr_ref[hd].astype(BF16)) + _dot_nt(dpih, wi_ref[hd].astype(BF16)))
        dxc = dxc + jnp.concatenate(dxc_parts, axis=1)
        nxt = dxc_next[...]
        up1, up2, up3 = _shift_up(dxc, nxt, 1), _shift_up(dxc, nxt, 2), _shift_up(dxc, nxt, 3)
        dlx = dxc * cw_ref[3:4, :]
        dlx = dlx + up1 * cw_ref[2:3, :]
        dlx = dlx + up2 * cw_ref[1:2, :]
        dlx = dlx + up3 * cw_ref[0:1, :]
        dlx_ref[...] = dlx.astype(BF16)
        dxc_next[...] = dxc[0:SUBLANES]
        dcb_ref[...] += jnp.sum(dxc, axis=0, keepdims=True)
        dcw_ref[0:1, :] += jnp.sum(up3 * lxv, axis=0, keepdims=True)
        dcw_ref[1:2, :] += jnp.sum(up2 * lxv, axis=0, keepdims=True)
        dcw_ref[2:3, :] += jnp.sum(up1 * lxv, axis=0, keepdims=True)
        dcw_ref[3:4, :] += jnp.sum(dxc * lxv, axis=0, keepdims=True)

    last = nt - 1
    tile = pl.BlockSpec((tm, w), lambda i: (last - i, 0))
    prev = pl.BlockSpec((SUBLANES, w), lambda i: (jnp.maximum((last - i) * per8 - 1, 0), 0))
    vec = pl.BlockSpec((1, w), lambda i: (0, 0))
    mat = pl.BlockSpec(wr.shape, lambda i: (0, 0, 0))
    cwb = pl.BlockSpec(cw.shape, lambda i: (0, 0))
    return _hosted_call(
        body, ride, nt,
        name="lru_bwd",
        in_specs=[tile, tile, tile, prev, tile] + [tile] * 6 + [cwb, mat, mat, vec],
        out_specs=[tile, tile, cwb, vec, mat, vec, mat, vec, vec],
        out_shape=[jax.ShapeDtypeStruct((tp, w), BF16), jax.ShapeDtypeStruct((tp, w), BF16),
                   jax.ShapeDtypeStruct(cw.shape, F32), jax.ShapeDtypeStruct((1, w), F32),
                   jax.ShapeDtypeStruct(wr.shape, F32), jax.ShapeDtypeStruct((1, w), F32),
                   jax.ShapeDtypeStruct(wr.shape, F32), jax.ShapeDtypeStruct((1, w), F32),
                   jax.ShapeDtypeStruct((1, w), F32)],
        scratch_shapes=[pltpu.VMEM((w // LANES, tm, LANES), F32)] * 4 + [pltpu.VMEM((SUBLANES, w), F32)] * 3,
        args=(lx, lg, hl, hl, dy, *saved, cw, wr, wi, lam),
    )


def _in_proj_dw(dparts, hp, gain, wg_shape, ride=None):
    tp, d = hp.shape
    n_ch = tp // CHUNK
    per = next(p for p in (4, 2, 5, 3, 1) if (n_ch - 1) % p == 0)
    n_steps = 1 + (n_ch - 1) // per
    widths = [p.shape[1] for p in dparts]
    segs = _proj_segments(widths[0], widths[2], widths[4], wg_shape[2])

    def body(*refs):
        dp = [refs[p * per:(p + 1) * per] for p in range(6)]
        hp_b = refs[6 * per:7 * per]
        g_ref, dwg_ref, acc, sem = refs[7 * per:]
        i = pl.program_id(0)

        def accumulate(blocks):
            h = jnp.concatenate([hp_b[b][...] for b in blocks], axis=0)
            rinv = lax.rsqrt(jnp.mean(h * h, axis=-1, keepdims=True) + EPS)
            u = ((h * rinv) * g_ref[...]).astype(BF16)
            for p_refs, parts in zip(dp, segs):
                for jj, inner, off, take in parts:
                    seg = jnp.concatenate([p_refs[b][:, off:off + take] for b in blocks], axis=0)
                    acc[jj, :, inner:inner + take] += _dot_tn(u, seg)

        @pl.when(i == 0)
        def _():
            acc[...] = jnp.zeros_like(acc)
            accumulate([0])

        @pl.when(i > 0)
        def _():
            accumulate(list(range(per)))

        @pl.when(i == n_steps - 1)
        def _():
            cp = pltpu.make_async_copy(acc, dwg_ref, sem)
            cp.start()
            cp.wait()

    def blocks(w):
        return [pl.BlockSpec((CHUNK, w), functools.partial(
            lambda i, b: (jnp.where(i == 0, b, per * (i - 1) + 1 + b), 0), b=b)) for b in range(per)]

    in_specs, args = [], []
    for a, w in list(zip(dparts, widths)) + [(hp, d)]:
        in_specs += blocks(w)
        args += [a] * per
    outs, rides = _hosted_call(
        body, ride, n_steps,
        name="in_proj_dw",
        in_specs=in_specs + [pl.BlockSpec(gain.shape, lambda i: (0, 0))],
        out_specs=[ANY_SPEC],
        out_shape=[jax.ShapeDtypeStruct(wg_shape, F32)],
        scratch_shapes=[pltpu.VMEM(wg_shape, F32), pltpu.SemaphoreType.DMA],
        args=(*args, gain),
    )
    return outs[0], rides


def _in_proj_dx(dparts, hp, dh2, gain, wg, s_len, tm, ride=None):
    tp, d = hp.shape
    nt = tp // tm
    widths = [p.shape[1] for p in dparts]
    segs = _proj_segments(widths[0], widths[2], widths[4], wg.shape[2])

    def body(*refs):
        dp = refs[:6]
        hp_ref, dh2_ref, g_ref, w_ref = refs[6:10]
        gx_ref, dmeta_ref, dg_ref = refs[10:13]
        stage, sems = refs[13:]
        i = pl.program_id(0)

        @pl.when(i == 0)
        def _():
            dg_ref[...] = jnp.zeros_like(dg_ref)

        h = hp_ref[...]
        rinv = lax.rsqrt(jnp.mean(h * h, axis=-1, keepdims=True) + EPS)
        nrm = h * rinv
        gv = g_ref[...]
        du = jnp.zeros((tm, d), F32)
        for p_ref, parts in zip(dp, segs):
            for jj, inner, off, take in parts:
                du = du + _dot_nt(p_ref[:, off:off + take], w_ref[jj, :, inner:inner + take])
        dg_ref[...] += jnp.sum(du * nrm, axis=0, keepdims=True)
        dn = du * gv
        dh = dh2_ref[...] + rinv * (dn - nrm * jnp.mean(dn * nrm, axis=-1, keepdims=True))

        def first_copy():
            return pltpu.make_async_copy(stage.at[0, pl.ds(CHUNK, tm - CHUNK), :],
                                         gx_ref.at[pl.ds(0, tm - CHUNK), :], sems.at[0])

        def tile_copy(slot, start):
            return pltpu.make_async_copy(stage.at[slot], gx_ref.at[pl.ds(start, tm), :], sems.at[slot])

        @pl.when(i == 0)
        def _():
            dmeta_ref[...] = dh[PAD_ROWS:CHUNK]
            stage[0] = dh
            first_copy().start()

        @pl.when(i > 0)
        def _():
            slot = 1 + i % 2

            @pl.when(i >= 3)
            def _():
                tile_copy(slot, 0).wait()

            stage[slot] = dh
            tile_copy(slot, pl.multiple_of(i * tm - CHUNK, CHUNK)).start()

        @pl.when(i == nt - 1)
        def _():
            first_copy().wait()
            for step in (nt - 2, nt - 1):
                if step >= 1:
                    tile_copy(1 + step % 2, 0).wait()

    tile = lambda w: pl.BlockSpec((tm, w), lambda i: (i, 0))
    return _hosted_call(
        body, ride, nt,
        name="in_proj_dx",
        in_specs=[tile(w) for w in widths] + [tile(d), tile(d), pl.BlockSpec(gain.shape, lambda i: (0, 0)),
                                              pl.BlockSpec(wg.shape, lambda i: (0, 0, 0))],
        out_specs=[ANY_SPEC, pl.BlockSpec((N_META, d), lambda i: (0, 0)), pl.BlockSpec((1, d), lambda i: (0, 0))],
        out_shape=[jax.ShapeDtypeStruct((s_len, d), F32), jax.ShapeDtypeStruct((N_META, d), F32),
                   jax.ShapeDtypeStruct((1, d), F32)],
        scratch_shapes=[pltpu.VMEM((3, tm, d), F32), pltpu.SemaphoreType.DMA((3,))],
        args=(*dparts, hp, dh2, gain, wg),
    )


def _pair_sum(buf, recv, c_arr, tr, name):
    _, rows, cols = buf.shape

    def body(c_ref, mine_ref, got_ref, out_ref):
        out_ref[...] = (mine_ref[...] + got_ref[...]).astype(BF16)

    grid_spec = pltpu.PrefetchScalarGridSpec(
        num_scalar_prefetch=1,
        grid=(N_CHIPS, rows // tr),
        in_specs=[pl.BlockSpec((1, tr, cols), lambda jj, r, c_ref: (2 * jj + c_ref[0], r, 0)),
                  pl.BlockSpec((1, tr, cols), lambda jj, r, c_ref: (jj, r, 0))],
        out_specs=pl.BlockSpec((1, tr, cols), lambda jj, r, c_ref: (jj, r, 0)),
    )
    return pl.pallas_call(
        body,
        name=name,
        grid_spec=grid_spec,
        out_shape=jax.ShapeDtypeStruct((N_CHIPS, rows, cols), BF16),
    )(c_arr, buf, recv)


def _pair_exchange_sum(buf, c_arr, tr, name):
    _, rows, cols = buf.shape
    per = rows // tr

    def body(c_ref, src_ref, mine_ref, out_ref, got, send_sems, recv_sems):
        jj, r = pl.program_id(0), pl.program_id(1)
        x, y, c, _ = _position()
        copies = [_remote(src_ref.at[2 * k + 1 - c], got.at[k], send_sems, recv_sems, k, (x, y, 1 - c))
                  for k in range(N_CHIPS)]

        @pl.when((jj == 0) & (r == 0))
        def _():
            for cp in copies:
                cp.start()

        for k in range(N_CHIPS):
            @pl.when((jj == k) & (r == 0))
            def _():
                copies[k].wait_recv()

        rows_r = pl.ds(pl.multiple_of(r * tr, tr), tr)
        out_ref[0] = (mine_ref[0] + got[jj, rows_r, :]).astype(BF16)

        @pl.when((jj == N_CHIPS - 1) & (r == per - 1))
        def _():
            for cp in copies:
                cp.wait_send()

    grid_spec = pltpu.PrefetchScalarGridSpec(
        num_scalar_prefetch=1,
        grid=(N_CHIPS, per),
        in_specs=[ANY_SPEC, pl.BlockSpec((1, tr, cols), lambda jj, r, c_ref: (2 * jj + c_ref[0], r, 0))],
        out_specs=pl.BlockSpec((1, tr, cols), lambda jj, r, c_ref: (jj, r, 0)),
        scratch_shapes=[pltpu.VMEM((N_CHIPS, rows, cols), F32), pltpu.SemaphoreType.DMA((N_CHIPS,)),
                        pltpu.SemaphoreType.DMA((N_CHIPS,))],
    )
    return pl.pallas_call(
        body,
        name=name,
        grid_spec=grid_spec,
        out_shape=jax.ShapeDtypeStruct((N_CHIPS, rows, cols), BF16),
        compiler_params=pltpu.CompilerParams(dimension_semantics=("arbitrary", "arbitrary"),
                                             vmem_limit_bytes=VMEM_LIMIT),
    )(c_arr, buf, buf)


def _chip_sum(mine, got, j_arr, tr, name, loss_part=None):
    _, rows, cols = got.shape
    extra = [] if loss_part is None else [loss_part]

    def body(j_ref, mine_ref, got_ref, *rest):
        out_ref = rest[-1]
        j = j_ref[0]
        acc = None
        for jj in range(N_CHIPS):
            term = jnp.where(j == jj, mine_ref[0], got_ref[jj]).astype(F32)
            acc = term if acc is None else acc + term
        out_ref[...] = acc
        if loss_part is not None:
            out_ref[ROW_LOSS:ROW_LOSS + 1, :] = rest[0][0:1, :]

    grid_spec = pltpu.PrefetchScalarGridSpec(
        num_scalar_prefetch=1,
        grid=(rows // tr,),
        in_specs=[pl.BlockSpec((1, tr, cols), lambda r, j_ref: (j_ref[0], r, 0)),
                  pl.BlockSpec((N_CHIPS, tr, cols), lambda r, j_ref: (0, r, 0))] +
                 [pl.BlockSpec(e.shape, lambda r, j_ref: (0, 0)) for e in extra],
        out_specs=pl.BlockSpec((tr, cols), lambda r, j_ref: (r, 0)),
    )
    return pl.pallas_call(
        body,
        name=name,
        grid_spec=grid_spec,
        out_shape=jax.ShapeDtypeStruct((rows, cols), F32),
    )(j_arr, mine, got, *extra)


def _finish_exchange(f_in, f_small):
    def body(fin_ref, fs_ref, rin_ref, os_ref, send_sems, recv_sems, local_sem):
        x, y, c, chips = _position()
        j = 2 * x + y
        me = 2 * j + c
        sibling = (x, y, 1 - c)
        local = pltpu.make_async_copy(fs_ref, os_ref.at[me], local_sem)
        local.start()

        def copy(k, src, dst, to):
            return _remote(src, dst, send_sems, recv_sems, k, to)

        first = [copy(0, fin_ref, rin_ref, sibling), copy(1, fs_ref, os_ref.at[me], sibling)]
        first += [copy(2 + k, fs_ref, os_ref.at[me], (cx, cy, c)) for k, (cx, cy) in enumerate(chips)]
        for cp in first:
            cp.start()
        passed = []
        for k, (cx, cy) in enumerate(chips):
            unit = 2 * (2 * cx + cy) + c
            copy(2 + k, fs_ref, os_ref.at[unit], sibling).wait_recv()
            fwd = copy(5 + k, os_ref.at[unit], os_ref.at[unit], sibling)
            fwd.start()
            passed.append(fwd)
        copy(0, fin_ref, rin_ref, sibling).wait_recv()
        copy(1, fs_ref, os_ref.at[2 * j + 1 - c], sibling).wait_recv()
        for k, (cx, cy) in enumerate(chips):
            unit = 2 * (2 * cx + cy) + 1 - c
            copy(5 + k, fs_ref, os_ref.at[unit], sibling).wait_recv()
        for cp in first + passed:
            cp.wait_send()
        local.wait()

    return pl.pallas_call(
        body,
        name="grad_finish_exchange",
        in_specs=[ANY_SPEC] * 2,
        out_specs=[ANY_SPEC] * 2,
        out_shape=[jax.ShapeDtypeStruct(f_in.shape, F32), jax.ShapeDtypeStruct((N_DEV,) + f_small.shape, F32)],
        scratch_shapes=[pltpu.SemaphoreType.DMA((8,)), pltpu.SemaphoreType.DMA((8,)), pltpu.SemaphoreType.DMA],
    )(f_in, f_small)


def _adamw_math(w, g, m, v):
    m = ADAM_B1 * m + (1.0 - ADAM_B1) * g
    v = ADAM_B2 * v + (1.0 - ADAM_B2) * (g * g)
    m_hat = m / (1.0 - ADAM_B1 ** ADAM_STEP)
    v_hat = v / (1.0 - ADAM_B2 ** ADAM_STEP)
    delta = -ADAM_LR * (m_hat / (jnp.sqrt(v_hat) + ADAM_EPS) + ADAM_WD * w)
    return delta, m, v


def _adamw_big(w, g_mine, g_sib, m, v, c_arr, tr, name):
    rows, cols = w.shape
    half = rows // 2
    per = half // tr

    def body(c_ref, w_ref, gm_ref, gs_ref, m_ref, v_ref, g_ref, d_ref, mo_ref, vo_ref):
        g = jnp.where(pl.program_id(0) == c_ref[0], gm_ref[...], gs_ref[...])
        g_ref[...] = g
        d_ref[...], mo_ref[...], vo_ref[...] = _adamw_math(w_ref[...], g, m_ref[...], v_ref[...])

    full = pl.BlockSpec((tr, cols), lambda h, r, c_ref: (h * per + r, 0))
    unit = pl.BlockSpec((tr, cols), lambda h, r, c_ref: (r, 0))
    grid_spec = pltpu.PrefetchScalarGridSpec(
        num_scalar_prefetch=1,
        grid=(2, per),
        in_specs=[full, unit, unit, full, full],
        out_specs=[full] * 4,
    )
    return pl.pallas_call(
        body,
        name=name,
        grid_spec=grid_spec,
        out_shape=[jax.ShapeDtypeStruct(w.shape, F32)] * 4,
    )(c_arr, w, g_mine, g_sib, m, v)


def _adamw_small(j_arr, packed, params):
    names = list(params)
    n = len(names)
    vec_names = ["norm_gain", "conv_b", "b_rg", "b_ig", "lru_lambda", "ret_norm_gain", "final_norm_gain"]

    def body(j_ref, pk_ref, *refs):
        ins = refs[:3 * n]
        outs = refs[3 * n:]
        j = j_ref[0]

        def shard(row, rows):
            return jnp.concatenate([pk_ref[2 * j, row:row + rows, :], pk_ref[2 * j + 1, row:row + rows, :]], axis=1)

        def tail_sum(unit, row, rows):
            start = pl.multiple_of(UNIT_ROWS + TAIL_ROWS * unit + row, SUBLANES)
            total = pk_ref[0, pl.ds(start, rows), :]
            for dev in range(1, N_DEV):
                total = total + pk_ref[dev, pl.ds(start, rows), :]
            return total

        for idx, name in enumerate(names):
            if name == "w_rg":
                g = pk_ref[:, ROW_WR:ROW_WR + LANES, :]
            elif name == "w_ig":
                g = pk_ref[:, ROW_WI:ROW_WI + LANES, :]
            elif name == "meta_tokens":
                g = jnp.concatenate([tail_sum(2 * j, 0, N_META), tail_sum(2 * j + 1, 0, N_META)], axis=1)
            elif name == "norm_gain":
                g = jnp.concatenate([tail_sum(u, N_META, SUBLANES)[0:1] for u in range(N_DEV)], axis=1)
            elif name == "conv_w":
                g = shard(ROW_CONV, 4)
            else:
                row = ROW_VEC + vec_names.index(name)
                g = jnp.concatenate([pk_ref[u, row:row + 1, :] for u in range(N_DEV)], axis=1)
            w_ref, m_ref, v_ref = ins[3 * idx:3 * idx + 3]
            delta, m, v = _adamw_math(w_ref[...], g, m_ref[...], v_ref[...])
            g_ref, d_ref, mo_ref, vo_ref = outs[4 * idx:4 * idx + 4]
            g_ref[...], d_ref[...], mo_ref[...], vo_ref[...] = g, delta, m, v
        total = pk_ref[0, ROW_LOSS:ROW_LOSS + 1, :]
        for u in range(1, N_DEV):
            total = total + pk_ref[u, ROW_LOSS:ROW_LOSS + 1, :]
        outs[4 * n][...] = jnp.broadcast_to(total, (SUBLANES, LANES))

    flat_in, out_shape = [], []
    for name in names:
        w, m, v = params[name]
        flat_in += [w, m, v]
        out_shape += [jax.ShapeDtypeStruct(w.shape, F32)] * 4
    out_shape.append(jax.ShapeDtypeStruct((SUBLANES, LANES), F32))
    res = pl.pallas_call(
        body,
        name="adamw_small",
        in_specs=[SMEM_SPEC, VMEM_SPEC] + [VMEM_SPEC] * (3 * n),
        out_specs=[VMEM_SPEC] * (4 * n + 1),
        out_shape=out_shape,
    )(j_arr, packed, *flat_in)
    return {name: tuple(res[4 * idx:4 * idx + 4]) for idx, name in enumerate(names)}, res[4 * n][0, 0]


def _units(a):
    rows = a.shape[0]
    return jnp.transpose(a.reshape(rows, N_DEV, LANES), (1, 0, 2))


def kernel(x, meta_tokens, norm_gain, w_in, conv_w, conv_b, w_rg, b_rg, w_ig, b_ig, lru_lambda, ret_norm_gain, w_out, final_norm_gain, loss_target, m_meta_tokens, m_norm_gain, m_w_in, m_conv_w, m_conv_b, m_w_rg, m_b_rg, m_w_ig, m_b_ig, m_lru_lambda, m_ret_norm_gain, m_w_out, m_final_norm_gain, v_meta_tokens, v_norm_gain, v_w_in, v_conv_w, v_conv_b, v_w_rg, v_b_rg, v_w_ig, v_b_ig, v_lru_lambda, v_ret_norm_gain, v_w_out, v_final_norm_gain):
    s_len, d = x.shape[1], x.shape[2]
    d_lru = w_rg.shape[1] * w_rg.shape[2]
    d_ret = ret_norm_gain.shape[1]
    d_qk = HEADS * QK_DIM
    tp = s_len + CHUNK
    tm = TOKEN_TILE
    assert tp % tm == 0 and d_lru == HEADS * LANES and d_ret == HEADS * LANES
    ax, ay, ac = lax.axis_index("x"), lax.axis_index("y"), lax.axis_index("c")
    c_arr = jnp.reshape(ac, (1,)).astype(jnp.int32)
    j_arr = jnp.reshape(2 * ax + ay, (1,)).astype(jnp.int32)

    small = jnp.concatenate([meta_tokens, conv_w[0], jnp.zeros((4, meta_tokens.shape[1]), F32)], axis=0)
    wg, sg = _gather_weights(w_in[0], small)
    cols = sg.shape[2]
    meta_full = jnp.transpose(sg[:, :N_META, :], (1, 0, 2)).reshape(N_META, N_CHIPS * cols)
    cw_full = jnp.transpose(sg[:, N_META:N_META + 4, :], (1, 0, 2)).reshape(4, N_CHIPS * cols)
    cw8 = jnp.concatenate([cw_full, jnp.zeros((4, cw_full.shape[1]), F32)], axis=0)

    half = QK_DIM // 2
    inv = ROPE_BASE ** (-jnp.arange(half, dtype=F32) / half)
    pos = (jnp.arange(tp) - PAD_ROWS).astype(F32)
    ang = pos[:, None] * inv[None, :]
    cos_t = jnp.tile(jnp.cos(ang), (1, LANES // half))
    sign = jnp.where((jnp.arange(LANES) % QK_DIM) < half, -1.0, 1.0).astype(F32)
    sin_t = jnp.tile(jnp.sin(ang), (1, LANES // half)) * sign[None, :]
    tables = _ret_tables()
    gain_f = final_norm_gain.reshape(1, d)

    hp, lx, lg, q, k, v, rg, wo4 = _in_proj(x[0], meta_full, norm_gain, wg, cos_t, sin_t, w_out[0], tm,
                                            d_lru, d_qk, d_ret)
    wo = wo4.reshape(N_CHIPS * wo4.shape[1], wo4.shape[2])
    hl, y_lru, *lru_saved = _lru_fwd(lx, lg, cw8, conv_b, w_rg[0], b_rg, w_ig[0], b_ig, lru_lambda, tm)
    o, y_ret, rprev = _ret_fwd(q, k, v, rg, ret_norm_gain, tables, tm)
    dh2, dy_lru, dy_ret, dwo, dgf, loss_acc = _out_proj_loss(y_lru, y_ret, hp, loss_target[0], wo, gain_f, tm)

    g_out = dwo.reshape(N_DEV, dwo.shape[0] // N_DEV, dwo.shape[1])
    (dq, dk, dv, drg, dgain), (r_out,) = _ret_bwd(q, k, v, rg, o, rprev, dy_ret, ret_norm_gain, cos_t, sin_t, tables,
                                                 tm, ride=_pair_ride([g_out]))
    q_out = _pair_sum(g_out, r_out, c_arr, 128, "grad_pair_sum_out")
    (dlx, dlg, dcw, dcb, dwr, dbr, dwi, dbi, dlam), (e_out,) = _lru_bwd(
        lx, lg, hl, dy_lru, lru_saved, cw8, w_rg[0], w_ig[0], lru_lambda, tm, ride=_chip_ride([q_out]))
    f_out = _chip_sum(q_out, e_out, j_arr, 128, "grad_chip_sum_out")
    zero_row = jnp.zeros((1, d), F32)
    vecs = [zero_row, dcb, dbr, dbi, dlam, dgain, dgf]
    g_small = jnp.concatenate([dwr, dwi, jnp.zeros((N_DEV, N_META, LANES), F32), _units(dcw[0:4])]
                              + [_units(a) for a in vecs]
                              + [jnp.zeros((N_DEV, UNIT_ROWS - ROW_VEC - N_VEC, LANES), F32)], axis=1)
    dparts = [dlx, dlg, dq, dk, dv, drg]
    dwg, (s_out, r_small) = _in_proj_dw(dparts, hp, norm_gain, wg.shape,
                                        ride=_join_rides(_sibling_ride([f_out]), _pair_ride([g_small])))
    g_in = dwg.reshape(N_DEV, dwg.shape[1] // 2, dwg.shape[2])
    q_in = _pair_exchange_sum(g_in, c_arr, 128, "grad_pair_exchange_sum_in")
    q_small = _pair_sum(g_small, r_small, c_arr, UNIT_ROWS, "grad_pair_sum_small")
    (grad_x, dmeta, dg1), (e_in, e_small) = _in_proj_dx(dparts, hp, dh2, norm_gain, wg, s_len, tm,
                                                        ride=_chip_ride([q_in, q_small]))
    f_in = _chip_sum(q_in, e_in, j_arr, 128, "grad_chip_sum_in")
    f_small = _chip_sum(q_small, e_small, j_arr, UNIT_ROWS, "grad_chip_sum_small", loss_part=loss_acc)
    tail = jnp.concatenate([_units(dmeta), _units(dg1), jnp.zeros((N_DEV, TAIL_ROWS - N_META - 1, LANES), F32)],
                           axis=1).reshape(N_DEV * TAIL_ROWS, LANES)
    s_in, o_small = _finish_exchange(f_in, jnp.concatenate([f_small, tail], axis=0))

    res_in = _adamw_big(w_in[0], f_in, s_in, m_w_in[0], v_w_in[0], c_arr, 256, "adamw_w_in")
    res_out = _adamw_big(w_out[0], f_out, s_out, m_w_out[0], v_w_out[0], c_arr, 256, "adamw_w_out")
    small_params = {
        "meta_tokens": (meta_tokens, m_meta_tokens, v_meta_tokens),
        "norm_gain": (norm_gain, m_norm_gain, v_norm_gain),
        "conv_w": (conv_w[0], m_conv_w[0], v_conv_w[0]),
        "conv_b": (conv_b, m_conv_b, v_conv_b),
        "w_rg": (w_rg[0], m_w_rg[0], v_w_rg[0]),
        "b_rg": (b_rg, m_b_rg, v_b_rg),
        "w_ig": (w_ig[0], m_w_ig[0], v_w_ig[0]),
        "b_ig": (b_ig, m_b_ig, v_b_ig),
        "lru_lambda": (lru_lambda, m_lru_lambda, v_lru_lambda),
        "ret_norm_gain": (ret_norm_gain, m_ret_norm_gain, v_ret_norm_gain),
        "final_norm_gain": (gain_f, m_final_norm_gain.reshape(1, d), v_final_norm_gain.reshape(1, d)),
    }
    res, loss = _adamw_small(j_arr, o_small, small_params)
    res["w_in"] = tuple(res_in)
    res["w_out"] = tuple(res_out)

    order = ["meta_tokens", "norm_gain", "w_in", "conv_w", "conv_b", "w_rg", "b_rg", "w_ig", "b_ig", "lru_lambda",
             "ret_norm_gain", "w_out", "final_norm_gain"]
    shapes = {"w_in": w_in.shape, "conv_w": conv_w.shape, "w_rg": w_rg.shape, "w_ig": w_ig.shape,
              "w_out": w_out.shape, "final_norm_gain": final_norm_gain.shape}
    outs = [loss, grad_x.reshape(x.shape)]
    for kind in range(4):
        for name in order:
            a = res[name][kind]
            outs.append(a.reshape(shapes[name]) if name in shapes else a)
    return tuple(outs)
```

```python
import functools

import jax
import jax.numpy as jnp
from jax import lax
from jax.experimental import pallas as pl
from jax.experimental.pallas import tpu as pltpu

F32 = jnp.float32
BF16 = jnp.bfloat16

N_META = 16
CHUNK = 128
PAD_ROWS = CHUNK - N_META
HEADS = 8
QK_DIM = 64
LANES = 128
SUBLANES = 8
LRU_C = 8.0
EPS = 1e-6
ROPE_BASE = 10000.0
ADAM_LR = 0.001
ADAM_B1 = 0.9
ADAM_B2 = 0.999
ADAM_EPS = 1e-08
ADAM_WD = 0.01
ADAM_STEP = 10
N_CHIPS = 4
N_DEV = 8
TOKEN_TILE = 384
VMEM_LIMIT = 58 * 1024 * 1024
MESH = pl.DeviceIdType.MESH

VMEM_SPEC = pl.BlockSpec(memory_space=pltpu.VMEM)
SMEM_SPEC = pl.BlockSpec(memory_space=pltpu.SMEM)
ANY_SPEC = pl.BlockSpec(memory_space=pl.ANY)

ROW_WR, ROW_WI, ROW_META, ROW_CONV, ROW_VEC, UNIT_ROWS = 0, 128, 256, 272, 276, 288
N_VEC = 7
ROW_LOSS = ROW_VEC + N_VEC
TAIL_ROWS = 24


def _dot(a, b):
    return jnp.dot(a, b, preferred_element_type=F32)


def _dot_nt(a, b):
    return lax.dot_general(a, b, (((1,), (1,)), ((), ())), preferred_element_type=F32)


def _dot_tn(a, b):
    return lax.dot_general(a, b, (((0,), (0,)), ((), ())), preferred_element_type=F32)


def _sigmoid(x):
    return 0.5 * jnp.tanh(0.5 * x) + 0.5


def _shift_down(x, prev8, s):
    rolled = pltpu.roll(x, s, 0)
    rows = lax.broadcasted_iota(jnp.int32, (SUBLANES, x.shape[1]), 0)
    top = jnp.where(rows < s, pltpu.roll(prev8, s, 0), rolled[0:SUBLANES])
    return jnp.concatenate([top, rolled[SUBLANES:]], axis=0)


def _shift_up(x, next8, s):
    n = x.shape[0]
    rolled = pltpu.roll(x, n - s, 0)
    rows = lax.broadcasted_iota(jnp.int32, (SUBLANES, x.shape[1]), 0)
    bot = jnp.where(rows >= SUBLANES - s, pltpu.roll(next8, SUBLANES - s, 0), rolled[n - SUBLANES:n])
    return jnp.concatenate([rolled[:n - SUBLANES], bot], axis=0)


def _rot_partner(t):
    w = t.shape[1]
    lane = lax.broadcasted_iota(jnp.int32, t.shape, 1)
    first = (lane % QK_DIM) < (QK_DIM // 2)
    return jnp.where(first, pltpu.roll(t, w - QK_DIM // 2, 1), pltpu.roll(t, QK_DIM // 2, 1))


def _tile_lanes(t, reps):
    return jnp.concatenate([t] * reps, axis=1)


class _Ride:
    def __init__(self, srcs, dst_shapes, n_copies, make):
        self.srcs, self.dst_shapes, self.n_copies, self.make = list(srcs), list(dst_shapes), n_copies, make


def _join_rides(a, b):
    def make(src, dst, send_sems, recv_sems, base):
        na, da = len(a.srcs), len(a.dst_shapes)
        return (a.make(src[:na], dst[:da], send_sems, recv_sems, base)
                + b.make(src[na:], dst[da:], send_sems, recv_sems, base + a.n_copies))

    return _Ride(a.srcs + b.srcs, a.dst_shapes + b.dst_shapes, a.n_copies + b.n_copies, make)


def _position():
    x, y, c = lax.axis_index("x"), lax.axis_index("y"), lax.axis_index("c")
    return x, y, c, [(1 - x, y), (x, 1 - y), (1 - x, 1 - y)]


def _remote(src, dst, send_sems, recv_sems, k, to):
    return pltpu.make_async_remote_copy(src_ref=src, dst_ref=dst, send_sem=send_sems.at[k], recv_sem=recv_sems.at[k],
                                        device_id=to, device_id_type=MESH)


def _pair_ride(bufs):
    def make(src, dst, send_sems, recv_sems, base):
        x, y, c, _ = _position()
        return [_remote(src[b].at[2 * jj + 1 - c], dst[b].at[jj], send_sems, recv_sems, base + b * N_CHIPS + jj,
                        (x, y, 1 - c)) for b in range(len(bufs)) for jj in range(N_CHIPS)]

    shapes = [jax.ShapeDtypeStruct((N_CHIPS,) + b.shape[1:], b.dtype) for b in bufs]
    return _Ride(bufs, shapes, N_CHIPS * len(bufs), make)


def _chip_ride(bufs):
    def make(src, dst, send_sems, recv_sems, base):
        x, y, c, chips = _position()
        return [_remote(src[b].at[2 * cx + cy], dst[b].at[2 * x + y], send_sems, recv_sems, base + b * 3 + k,
                        (cx, cy, c)) for b in range(len(bufs)) for k, (cx, cy) in enumerate(chips)]

    shapes = [jax.ShapeDtypeStruct(b.shape, b.dtype) for b in bufs]
    return _Ride(bufs, shapes, 3 * len(bufs), make)


def _sibling_ride(bufs):
    def make(src, dst, send_sems, recv_sems, base):
        x, y, c, _ = _position()
        return [_remote(src[b], dst[b], send_sems, recv_sems, base + b, (x, y, 1 - c)) for b in range(len(bufs))]

    shapes = [jax.ShapeDtypeStruct(b.shape, b.dtype) for b in bufs]
    return _Ride(bufs, shapes, len(bufs), make)


def _exchange_call(ride, name):
    n_src, n_dst = len(ride.srcs), len(ride.dst_shapes)

    def body(*refs):
        copies = ride.make(refs[:n_src], refs[n_src:n_src + n_dst], refs[-2], refs[-1], 0)
        for cp in copies:
            cp.start()
        for cp in copies:
            cp.wait()

    return pl.pallas_call(
        body,
        name=name,
        in_specs=[ANY_SPEC] * n_src,
        out_specs=[ANY_SPEC] * n_dst,
        out_shape=ride.dst_shapes,
        scratch_shapes=[pltpu.SemaphoreType.DMA((ride.n_copies,)), pltpu.SemaphoreType.DMA((ride.n_copies,))],
    )(*ride.srcs)


def _hosted_call(body, ride, n_steps, *, name, in_specs, out_specs, out_shape, scratch_shapes, args):
    params = pltpu.CompilerParams(dimension_semantics=("arbitrary",), vmem_limit_bytes=VMEM_LIMIT)
    if ride is None:
        res = pl.pallas_call(body, name=name, grid=(n_steps,), in_specs=list(in_specs), out_specs=list(out_specs),
                             out_shape=list(out_shape), scratch_shapes=list(scratch_shapes),
                             compiler_params=params)(*args)
        return list(res), []
    sizes = [len(in_specs), len(ride.srcs), len(out_specs), len(ride.dst_shapes), len(scratch_shapes), 2]

    def hosted(*refs):
        groups, pos = [], 0
        for n in sizes:
            groups.append(refs[pos:pos + n])
            pos += n
        ins, rin, outs, rout, scr, (send_sems, recv_sems) = groups
        i = pl.program_id(0)

        @pl.when(i == 0)
        def _():
            for cp in ride.make(rin, rout, send_sems, recv_sems, 0):
                cp.start()

        body(*ins, *outs, *scr)

        @pl.when(i == n_steps - 1)
        def _():
            for cp in ride.make(rin, rout, send_sems, recv_sems, 0):
                cp.wait()

    n_out = len(out_specs)
    res = pl.pallas_call(
        hosted,
        name=name,
        grid=(n_steps,),
        in_specs=list(in_specs) + [ANY_SPEC] * len(ride.srcs),
        out_specs=list(out_specs) + [ANY_SPEC] * len(ride.dst_shapes),
        out_shape=list(out_shape) + ride.dst_shapes,
        scratch_shapes=list(scratch_shapes) + [pltpu.SemaphoreType.DMA((ride.n_copies,)),
                                               pltpu.SemaphoreType.DMA((ride.n_copies,))],
        compiler_params=params,
    )(*args, *ride.srcs)
    return list(res[:n_out]), list(res[n_out:])


def _gather_weights(w_in, small):
    r_in, c_in = w_in.shape
    h_in = r_in // 2
    q_in = h_in // 2

    def body(win_ref, small_ref, wg_ref, sg_ref, send_sems, recv_sems):
        x, y, c, chips = _position()
        j = 2 * x + y
        sibling = (x, y, 1 - c)
        xn, yn, dg = chips
        jx, jy, jd = (2 * cx + cy for cx, cy in chips)

        wg_ref[j] = win_ref[...].astype(BF16)
        sg_ref[j] = small_ref[...]

        def half(jj, cc):
            return wg_ref.at[jj, pl.ds(cc * h_in, h_in), :]

        def quarter(jj, qq):
            return wg_ref.at[jj, pl.ds(c * h_in + qq * q_in, q_in), :]

        def copy(k, ref, to):
            return _remote(ref, ref, send_sems, recv_sems, k, to)

        first = [copy(0, quarter(j, 0), (*xn, c)), copy(2, quarter(j, 1), (*yn, c)),
                 copy(1, quarter(j, 1), (*xn, c)), copy(3, quarter(j, 0), (*yn, c))]
        first += [copy(9 + k, sg_ref.at[j], (cx, cy, c)) for k, (cx, cy) in enumerate(chips)]
        for cp in first:
            cp.start()
        copy(0, quarter(jx, 0), sibling).wait_recv()
        along_y = copy(4, quarter(jx, 0), (*yn, c))
        along_y.start()
        copy(2, quarter(jy, 1), sibling).wait_recv()
        along_x = copy(5, quarter(jy, 1), (*xn, c))
        along_x.start()
        copy(1, quarter(jx, 1), sibling).wait_recv()
        to_sib = [copy(6, half(jx, c), sibling)]
        to_sib[-1].start()
        copy(3, quarter(jy, 0), sibling).wait_recv()
        to_sib.append(copy(7, half(jy, c), sibling))
        to_sib[-1].start()
        copy(4, quarter(jd, 0), sibling).wait_recv()
        copy(5, quarter(jd, 1), sibling).wait_recv()
        to_sib.append(copy(8, half(jd, c), sibling))
        to_sib[-1].start()
        for k, jk in enumerate((jx, jy, jd)):
            copy(6 + k, half(jk, 1 - c), sibling).wait_recv()
            copy(9 + k, sg_ref.at[jk], sibling).wait_recv()
        for cp in first + [along_y, along_x] + to_sib:
            cp.wait_send()

    return pl.pallas_call(
        body,
        name="gather_weights",
        out_shape=(jax.ShapeDtypeStruct((N_CHIPS, r_in, c_in), BF16),
                   jax.ShapeDtypeStruct((N_CHIPS,) + small.shape, F32)),
        in_specs=[VMEM_SPEC, VMEM_SPEC],
        out_specs=(VMEM_SPEC, VMEM_SPEC),
        scratch_shapes=[pltpu.SemaphoreType.DMA((12,)), pltpu.SemaphoreType.DMA((12,))],
        compiler_params=pltpu.CompilerParams(vmem_limit_bytes=VMEM_LIMIT),
    )(w_in, small)


def _proj_segments(d_lru, d_qk, d_ret, chunk_w):
    widths = [d_lru, d_lru, d_qk, d_qk, d_ret, d_ret]
    segs, col = [], 0
    for w in widths:
        parts, off = [], 0
        while off < w:
            jj, inner = divmod(col + off, chunk_w)
            take = min(w - off, chunk_w - inner)
            parts.append((jj, inner, off, take))
            off += take
        segs.append(parts)
        col += w
    return segs


def _in_proj(x2, meta, gain, wg, cos_t, sin_t, w_out, tm, d_lru, d_qk, d_ret):
    s_len, d = x2.shape
    tp = s_len + CHUNK
    nt, nb = tp // tm, tm // CHUNK
    segs = _proj_segments(d_lru, d_qk, d_ret, wg.shape[2])
    widths = [d_lru, d_lru, d_qk, d_qk, d_ret, d_ret]
    r_out, c_out = w_out.shape
    h_out = r_out // 2
    fwd_step = min(6, nt - 1)

    def gather_w_out(i, wout_ref, wo_ref, wob, send_sems, recv_sems, local_sem):
        x, y, c, chips = _position()
        j = 2 * x + y
        sibling = (x, y, 1 - c)

        def half(jj, cc):
            return wo_ref.at[jj, pl.ds(cc * h_out, h_out), :]

        local = pltpu.make_async_copy(wob, wo_ref.at[j], local_sem)
        first = [_remote(wob.at[pl.ds(c * h_out, h_out), :], half(j, c), send_sems, recv_sems, k, (cx, cy, c))
                 for k, (cx, cy) in enumerate(chips)]
        passed = [_remote(half(2 * cx + cy, c), half(2 * cx + cy, c), send_sems, recv_sems, 3 + k, sibling)
                  for k, (cx, cy) in enumerate(chips)]

        @pl.when(i == 0)
        def _():
            wob[...] = wout_ref[...].astype(BF16)
            local.start()
            for cp in first:
                cp.start()

        @pl.when(i == fwd_step)
        def _():
            for k, (cx, cy) in enumerate(chips):
                _remote(half(2 * cx + cy, c), half(2 * cx + cy, c), send_sems, recv_sems, k, sibling).wait_recv()
                passed[k].start()

        @pl.when(i == nt - 1)
        def _():
            for k, (cx, cy) in enumerate(chips):
                jk = 2 * cx + cy
                _remote(half(jk, 1 - c), half(jk, 1 - c), send_sems, recv_sems, 3 + k, sibling).wait_recv()
            for cp in first + passed:
                cp.wait_send()
            local.wait()

    def body(*refs):
        xb = refs[:nb]
        meta_ref, g_ref, w_ref, cos_ref, sin_ref, wout_ref = refs[nb:nb + 6]
        hp_ref = refs[nb + 6]
        outs = refs[nb + 7:nb + 13]
        wo_ref, wob, send_sems, recv_sems, local_sem = refs[nb + 13:]
        i = pl.program_id(0)
        gather_w_out(i, wout_ref, wo_ref, wob, send_sems, recv_sems, local_sem)
        blocks = [r[...] for r in xb]
        head = jnp.concatenate([jnp.zeros((PAD_ROWS, d), F32), meta_ref[...]], axis=0)
        blocks[0] = jnp.where(i == 0, head, blocks[0])
        h = jnp.concatenate(blocks, axis=0)
        hp_ref[...] = h
        rinv = lax.rsqrt(jnp.mean(h * h, axis=-1, keepdims=True) + EPS)
        u = ((h * rinv) * g_ref[...]).astype(BF16)
        for out_ref, parts in zip(outs, segs):
            for jj, inner, off, take in parts:
                out_ref[:, off:off + take] = _dot(u, w_ref[jj, :, inner:inner + take])
        cos = _tile_lanes(cos_ref[...], d_qk // LANES)
        sin = _tile_lanes(sin_ref[...], d_qk // LANES)
        q = outs[2][...]
        outs[2][...] = q * cos + _rot_partner(q) * sin
        k = outs[3][...]
        outs[3][...] = (k * cos + _rot_partner(k) * sin) * (QK_DIM ** -0.5)

    x_specs = [pl.BlockSpec((CHUNK, d), functools.partial(lambda i, b: (jnp.maximum(i * nb + b - 1, 0), 0), b=b))
               for b in range(nb)]
    tile = lambda w: pl.BlockSpec((tm, w), lambda i: (i, 0))
    return pl.pallas_call(
        body,
        name="in_proj",
        grid=(nt,),
        in_specs=x_specs + [pl.BlockSpec(meta.shape, lambda i: (0, 0)),
                            pl.BlockSpec(gain.shape, lambda i: (0, 0)),
                            pl.BlockSpec(wg.shape, lambda i: (0, 0, 0)),
                            tile(LANES), tile(LANES),
                            pl.BlockSpec(w_out.shape, lambda i: (0, 0))],
        out_specs=[tile(d)] + [tile(w) for w in widths] + [ANY_SPEC],
        out_shape=[jax.ShapeDtypeStruct((tp, d), F32)] + [jax.ShapeDtypeStruct((tp, w), F32) for w in widths]
                  + [jax.ShapeDtypeStruct((N_CHIPS, r_out, c_out), BF16)],
        scratch_shapes=[pltpu.VMEM((r_out, c_out), BF16), pltpu.SemaphoreType.DMA((6,)),
                        pltpu.SemaphoreType.DMA((6,)), pltpu.SemaphoreType.DMA],
        compiler_params=pltpu.CompilerParams(dimension_semantics=("arbitrary",), vmem_limit_bytes=VMEM_LIMIT),
    )(*([x2] * nb), meta, gain, wg, cos_t, sin_t, w_out)


def _to_groups(ref3, x):
    for g in range(ref3.shape[0]):
        ref3[g] = x[:, g * LANES:(g + 1) * LANES]


def _from_groups(ref3):
    return jnp.concatenate([ref3[g] for g in range(ref3.shape[0])], axis=1)


def _segment_scan(a3, u3, out3, p3, carry, tm, reverse):
    groups = a3.shape[0]
    seg = tm // SUBLANES

    def step(j, state):
        hs, ps = state
        rows = pl.ds((seg - 1 - j) if reverse else j, SUBLANES, stride=seg)
        new_h, new_p = [], []
        for g in range(groups):
            a = a3[g, rows, :]
            h = a * hs[g] + u3[g, rows, :]
            p = ps[g] * a
            out3[g, rows, :] = h
            p3[g, rows, :] = p
            new_h.append(h)
            new_p.append(p)
        return tuple(new_h), tuple(new_p)

    zeros = tuple(jnp.zeros((SUBLANES, LANES), F32) for _ in range(groups))
    ones = tuple(jnp.ones((SUBLANES, LANES), F32) for _ in range(groups))
    lax.fori_loop(0, seg, step, (zeros, ones))
    carries = [carry[:, g * LANES:(g + 1) * LANES] for g in range(groups)]
    for s in (reversed(range(SUBLANES)) if reverse else range(SUBLANES)):
        rows = slice(s * seg, (s + 1) * seg)
        edge = s * seg if reverse else (s + 1) * seg - 1
        for g in range(groups):
            out3[g, rows, :] = out3[g, rows, :] + p3[g, rows, :] * carries[g]
            carries[g] = out3[g, edge:edge + 1, :]
    return jnp.concatenate(carries, axis=1)


def _softplus_neg(lam):
    z = -lam
    e = jnp.exp(-jnp.abs(z))
    e1 = 1.0 + e
    log1p_e = jnp.where(e1 == 1.0, e, jnp.log(e1) * (e / (e1 - 1.0)))
    return jnp.maximum(z, 0.0) + log1p_e


def _lru_fwd(lx, lg, cw, cb, wr, br, wi, bi, lam, tm):
    tp, w = lx.shape
    nt = tp // tm
    per8 = tm // SUBLANES
    n_heads = wr.shape[0]

    def body(lx_ref, lxp_ref, lg_ref, cw_ref, cb_ref, wr_ref, br_ref, wi_ref, bi_ref, lam_ref,
             hl_ref, y_ref, xc_ref, r_ref, ig_ref, a_ref, beta_ref, w4_ref, a_s, u_s, h_s, p_s, carry):
        i = pl.program_id(0)

        @pl.when(i == 0)
        def _():
            carry[...] = jnp.zeros_like(carry)

        lxv = lx_ref[...]
        prev8 = jnp.where(i == 0, 0.0, lxp_ref[...])
        xc = cb_ref[...] + _shift_down(lxv, prev8, 3) * cw_ref[0:1, :]
        xc = xc + _shift_down(lxv, prev8, 2) * cw_ref[1:2, :]
        xc = xc + _shift_down(lxv, prev8, 1) * cw_ref[2:3, :]
        xc = xc + lxv * cw_ref[3:4, :]
        xc_ref[...] = xc
        pre_r, pre_i = [], []
        for hd in range(n_heads):
            xh = xc[:, hd * LANES:(hd + 1) * LANES].astype(BF16)
            pre_r.append(_dot(xh, wr_ref[hd].astype(BF16)))
            pre_i.append(_dot(xh, wi_ref[hd].astype(BF16)))
        r = _sigmoid(jnp.concatenate(pre_r, axis=1) + br_ref[...])
        ig = _sigmoid(jnp.concatenate(pre_i, axis=1) + bi_ref[...])
        r_ref[...] = r
        ig_ref[...] = ig
        log_a = (-LRU_C * r) * _softplus_neg(lam_ref[...])
        a = jnp.exp(log_a)
        a_ref[...] = a
        zz = -2.0 * log_a
        series = zz * (1.0 - zz * (0.5 - zz * (1.0 / 6.0)))
        a2 = a * a
        beta2 = jnp.maximum(jnp.where(zz < 0.015625, series, 1.0 - a2), 1e-37)
        rsb = lax.rsqrt(beta2)
        beta = beta2 * rsb
        beta_ref[...] = beta
        w4_ref[...] = a2 * rsb
        row = lax.broadcasted_iota(jnp.int32, (tm, 1), 0) + i * tm
        _to_groups(a_s, a)
        _to_groups(u_s, jnp.where(row >= PAD_ROWS, beta * ig * xc, 0.0))
        carry[0:1, :] = _segment_scan(a_s, u_s, h_s, p_s, carry[0:1, :], tm, reverse=False)
        hl = _from_groups(h_s)
        hl_ref[...] = hl
        g = lg_ref[...]
        y_ref[...] = (hl * (g * _sigmoid(g))).astype(BF16)

    tile = pl.BlockSpec((tm, w), lambda i: (i, 0))
    prev = pl.BlockSpec((SUBLANES, w), lambda i: (jnp.maximum(i * per8 - 1, 0), 0))
    vec = pl.BlockSpec((1, w), lambda i: (0, 0))
    mat = pl.BlockSpec(wr.shape, lambda i: (0, 0, 0))
    f32_out = jax.ShapeDtypeStruct((tp, w), F32)
    return pl.pallas_call(
        body,
        name="lru_fwd",
        grid=(nt,),
        in_specs=[tile, prev, tile, pl.BlockSpec(cw.shape, lambda i: (0, 0)), vec, mat, vec, mat, vec, vec],
        out_specs=[tile] * 8,
        out_shape=[f32_out, jax.ShapeDtypeStruct((tp, w), BF16)] + [f32_out] * 6,
        scratch_shapes=[pltpu.VMEM((w // LANES, tm, LANES), F32)] * 4 + [pltpu.VMEM((SUBLANES, w), F32)],
        compiler_params=pltpu.CompilerParams(dimension_semantics=("arbitrary",), vmem_limit_bytes=VMEM_LIMIT),
    )(lx, lx, lg, cw, cb, wr, br, wi, bi, lam)


def _ret_tables():
    log_g = jnp.log1p(-jnp.exp2(-5.0 - jnp.arange(HEADS, dtype=F32)))
    idx = jnp.arange(CHUNK, dtype=F32)
    diff = idx[:, None] - idx[None, :]
    dmask = jnp.where(diff[None] >= 0.0, jnp.exp(jnp.maximum(diff, 0.0)[None] * log_g[:, None, None]), 0.0)
    kdec = jnp.repeat(jnp.exp((CHUNK - 1.0 - idx)[:, None] * log_g[None, :]), QK_DIM, axis=1)
    qdec = jnp.repeat(jnp.exp((idx + 1.0)[:, None] * log_g[None, :]), QK_DIM, axis=1)
    g_chunk = jnp.exp(CHUNK * log_g)
    g_rows = jnp.repeat(g_chunk, QK_DIM).reshape(HEADS // 2, 2 * QK_DIM, 1)
    g_state = jnp.broadcast_to(g_rows, (HEADS // 2, 2 * QK_DIM, 2 * LANES))
    r_head = jnp.arange(2 * QK_DIM)[:, None] // QK_DIM
    c_head = jnp.arange(2 * LANES)[None, :] // LANES
    block_diag = (r_head == c_head).astype(F32)
    return dmask, qdec, kdec, g_state, block_diag


def _head_norm(o_h):
    mu = jnp.mean(o_h, axis=-1, keepdims=True)
    oc = o_h - mu
    var = jnp.mean(oc * oc, axis=-1, keepdims=True)
    rstd = lax.rsqrt(var + EPS)
    return oc * rstd, rstd


def _ret_fwd(q, k, v, rg, gain, tables, tm):
    tp, d_qk = q.shape
    d_ret = v.shape[1]
    n_ch = tp // CHUNK
    cps = tm // CHUNK
    n_pairs = HEADS // 2
    dmask, qdec, kdec, g_state, block_diag = tables

    def body(q_ref, k_ref, v_ref, rg_ref, gain_ref, dm_ref, qd_ref, kd_ref, gs_ref, bd_ref,
             o_ref, y_ref, rp_ref, state):
        n = pl.program_id(0)

        @pl.when(n == 0)
        def _():
            state[...] = jnp.zeros_like(state)

        lane = lax.broadcasted_iota(jnp.int32, (CHUNK, LANES), 1)
        for ci in range(cps):
            rs = slice(ci * CHUNK, (ci + 1) * CHUNK)
            for p in range(n_pairs):
                qs = slice(p * LANES, (p + 1) * LANES)
                vs = slice(p * 2 * LANES, (p + 1) * 2 * LANES)
                qp, kp = q_ref[rs, qs], k_ref[rs, qs]
                vb = v_ref[rs, vs].astype(BF16)
                kb = kp.astype(BF16)
                qd = (qp * qd_ref[:, qs]).astype(BF16)
                kd = (kp * kd_ref[:, qs]).astype(BF16)
                st = state[p]
                st_b = st.astype(BF16)
                rp_ref[ci, p] = st_b
                cross = _dot(qd, st_b)
                for e in range(2):
                    hd = 2 * p + e
                    hs = slice(hd * LANES, (hd + 1) * LANES)
                    es = slice(e * LANES, (e + 1) * LANES)
                    qm = jnp.where((lane // QK_DIM) == e, qp, 0.0).astype(BF16)
                    s = _dot_nt(qm, kb) * dm_ref[hd]
                    o_h = _dot(s.astype(BF16), vb[:, es]) + cross[:, es]
                    o_ref[rs, hs] = o_h
                    xhat, _ = _head_norm(o_h)
                    g = rg_ref[rs, hs]
                    y_ref[rs, hs] = ((xhat * gain_ref[:, hs]) * (g * _sigmoid(g))).astype(BF16)
                state[p] = gs_ref[p] * st + bd_ref[...] * _dot_tn(kd, vb)

    ch = lambda w: pl.BlockSpec((tm, w), lambda n: (n, 0))
    const2 = lambda a: pl.BlockSpec(a.shape, lambda n: (0, 0))
    const3 = lambda a: pl.BlockSpec(a.shape, lambda n: (0, 0, 0))
    return pl.pallas_call(
        body,
        name="ret_fwd",
        grid=(n_ch // cps,),
        in_specs=[ch(d_qk), ch(d_qk), ch(d_ret), ch(d_ret), const2(gain), const3(dmask), const2(qdec), const2(kdec),
                  const3(g_state), const2(block_diag)],
        out_specs=[ch(d_ret), ch(d_ret),
                   pl.BlockSpec((cps, n_pairs, 2 * QK_DIM, 2 * LANES), lambda n: (n, 0, 0, 0))],
        out_shape=[jax.ShapeDtypeStruct((tp, d_ret), F32), jax.ShapeDtypeStruct((tp, d_ret), BF16),
                   jax.ShapeDtypeStruct((n_ch, n_pairs, 2 * QK_DIM, 2 * LANES), BF16)],
        scratch_shapes=[pltpu.VMEM((n_pairs, 2 * QK_DIM, 2 * LANES), F32)],
        compiler_params=pltpu.CompilerParams(dimension_semantics=("arbitrary",), vmem_limit_bytes=VMEM_LIMIT),
    )(q, k, v, rg, gain, dmask, qdec, kdec, g_state, block_diag)


def _out_proj_loss(y_lru, y_ret, hp, tgt, wo, gain_f, tm):
    tp, d = hp.shape
    w_lru = y_lru.shape[1]
    w_mix = wo.shape[0]
    nt, nb = tp // tm, tm // CHUNK

    def body(*refs):
        yl_ref, yr_ref, hp_ref = refs[:3]
        tb = refs[3:3 + nb]
        wo_ref, gf_ref = refs[3 + nb:5 + nb]
        dh2_ref, dyl_ref, dyr_ref, dwo_ref, dgf_ref, loss_ref = refs[5 + nb:11 + nb]
        yl_s, yr_s, d_s = refs[11 + nb:]
        i = pl.program_id(0)

        @pl.when(i == 0)
        def _():
            dwo_ref[...] = jnp.zeros_like(dwo_ref)
            dgf_ref[...] = jnp.zeros_like(dgf_ref)
            loss_ref[...] = jnp.zeros_like(loss_ref)
            for s_ref in (yl_s, yr_s, d_s):
                s_ref[...] = jnp.zeros_like(s_ref)

        def add_dwo(left, right, dd):
            dwo_ref[0:w_lru, :] += _dot_tn(left, dd)
            dwo_ref[w_lru:w_mix, :] += _dot_tn(right, dd)

        yl, yr = yl_ref[...], yr_ref[...]
        h2 = hp_ref[...] + _dot(yl, wo_ref[0:w_lru, :]) + _dot(yr, wo_ref[w_lru:w_mix, :])

        add_dwo(yl_s[...], yr_s[...], d_s[...])

        rinv = lax.rsqrt(jnp.mean(h2 * h2, axis=-1, keepdims=True) + EPS)
        nrm = h2 * rinv
        gf = gf_ref[...]
        tgt_v = jnp.concatenate([r[...] for r in tb], axis=0)
        row = lax.broadcasted_iota(jnp.int32, (tm, 1), 0) + i * tm
        err = jnp.where(row >= CHUNK, nrm * gf - tgt_v, 0.0)
        loss_ref[...] += 0.5 * jnp.sum(jnp.mean(err * err, axis=-1, keepdims=True))
        dout = err * (1.0 / d)
        dgf_ref[...] += jnp.sum(dout * nrm, axis=0, keepdims=True)
        dn = dout * gf
        dh2 = rinv * (dn - nrm * jnp.mean(dn * nrm, axis=-1, keepdims=True))
        dh2_ref[...] = dh2
        dh2b = dh2.astype(BF16)
        dyl_ref[...] = _dot_nt(dh2b, wo_ref[0:w_lru, :])
        dyr_ref[...] = _dot_nt(dh2b, wo_ref[w_lru:w_mix, :])

        yl_s[...], yr_s[...], d_s[...] = yl, yr, dh2b

        @pl.when(i == nt - 1)
        def _():
            add_dwo(yl, yr, dh2b)

    tile = lambda w: pl.BlockSpec((tm, w), lambda i: (i, 0))
    t_specs = [pl.BlockSpec((CHUNK, d), functools.partial(lambda i, b: (jnp.maximum(i * nb + b - 1, 0), 0), b=b))
               for b in range(nb)]
    return pl.pallas_call(
        body,
        name="out_proj_loss",
        grid=(nt,),
        in_specs=[tile(w_lru), tile(w_mix - w_lru), tile(d)] + t_specs +
                 [pl.BlockSpec(wo.shape, lambda i: (0, 0)), pl.BlockSpec(gain_f.shape, lambda i: (0, 0))],
        out_specs=[tile(d), tile(w_lru), tile(w_mix - w_lru), pl.BlockSpec(wo.shape, lambda i: (0, 0)),
                   pl.BlockSpec((1, d), lambda i: (0, 0)), pl.BlockSpec((SUBLANES, LANES), lambda i: (0, 0))],
        out_shape=[jax.ShapeDtypeStruct((tp, d), F32), jax.ShapeDtypeStruct((tp, w_lru), F32),
                   jax.ShapeDtypeStruct((tp, w_mix - w_lru), F32), jax.ShapeDtypeStruct(wo.shape, F32),
                   jax.ShapeDtypeStruct((1, d), F32), jax.ShapeDtypeStruct((SUBLANES, LANES), F32)],
        scratch_shapes=[pltpu.VMEM((tm, w_lru), BF16), pltpu.VMEM((tm, w_mix - w_lru), BF16),
                        pltpu.VMEM((tm, d), BF16)],
        compiler_params=pltpu.CompilerParams(dimension_semantics=("arbitrary",), vmem_limit_bytes=VMEM_LIMIT),
    )(y_lru, y_ret, hp, *([tgt] * nb), wo, gain_f)


def _ret_bwd(q, k, v, rg, o, rprev, dy, gain, cos_t, sin_t, tables, tm, ride=None):
    tp, d_qk = q.shape
    d_ret = v.shape[1]
    n_ch = tp // CHUNK
    cps = tm // CHUNK
    n_pairs = HEADS // 2
    dmask, qdec, kdec, g_state, block_diag = tables

    dmask_t = jnp.swapaxes(dmask, 1, 2)

    def body(q_ref, k_ref, v_ref, rg_ref, o_ref, rp_ref, dy_ref, gain_ref, cos_ref, sin_ref,
             dm_ref, dmt_ref, qd_ref, kd_ref, gs_ref, bd_ref, dq_ref, dk_ref, dv_ref, drg_ref, dgain_ref, dstate):
        n = pl.program_id(0)

        @pl.when(n == 0)
        def _():
            dstate[...] = jnp.zeros_like(dstate)
            dgain_ref[...] = jnp.zeros_like(dgain_ref)

        lane = lax.broadcasted_iota(jnp.int32, (CHUNK, LANES), 1)
        for ci in reversed(range(cps)):
            rs = slice(ci * CHUNK, (ci + 1) * CHUNK)
            dq_parts, dk_parts = [], []
            for p in range(n_pairs):
                qs = slice(p * LANES, (p + 1) * LANES)
                vs = slice(p * 2 * LANES, (p + 1) * 2 * LANES)
                do_parts = []
                for e in range(2):
                    hd = 2 * p + e
                    hs = slice(hd * LANES, (hd + 1) * LANES)
                    xhat, rstd = _head_norm(o_ref[rs, hs])
                    g = rg_ref[rs, hs]
                    sg = _sigmoid(g)
                    dyh = dy_ref[rs, hs]
                    gn = gain_ref[:, hs]
                    d_on = dyh * (g * sg)
                    drg_ref[rs, hs] = (dyh * (xhat * gn) * (sg * (1.0 + g * (1.0 - sg)))).astype(BF16)
                    dgain_ref[:, hs] += jnp.sum(d_on * xhat, axis=0, keepdims=True)
                    dxh = d_on * gn
                    do_parts.append(rstd * (dxh - jnp.mean(dxh, axis=-1, keepdims=True)
                                            - xhat * jnp.mean(dxh * xhat, axis=-1, keepdims=True)))
                do_b = jnp.concatenate(do_parts, axis=1).astype(BF16)
                qp, kp = q_ref[rs, qs], k_ref[rs, qs]
                vb = v_ref[rs, vs].astype(BF16)
                kb = kp.astype(BF16)
                qd = (qp * qd_ref[:, qs]).astype(BF16)
                kd = (kp * kd_ref[:, qs]).astype(BF16)
                dst = dstate[p]
                dst_b = dst.astype(BF16)
                dqp = _dot_nt(do_b, rp_ref[ci, p]) * qd_ref[:, qs]
                dkp = _dot_nt(vb, dst_b) * kd_ref[:, qs]
                dvp = _dot(kd, dst_b)
                dv_parts = []
                for e in range(2):
                    hd = 2 * p + e
                    es = slice(e * LANES, (e + 1) * LANES)
                    mine = (lane // QK_DIM) == e
                    qm = jnp.where(mine, qp, 0.0).astype(BF16)
                    km = jnp.where(mine, kp, 0.0).astype(BF16)
                    ds = (_dot_nt(do_b[:, es], vb[:, es]) * dm_ref[hd]).astype(BF16)
                    s_t = (_dot_nt(kb, qm) * dmt_ref[hd]).astype(BF16)
                    ds_t = (_dot_nt(vb[:, es], do_b[:, es]) * dmt_ref[hd]).astype(BF16)
                    dv_parts.append(dvp[:, es] + _dot(s_t, do_b[:, es]))
                    dqp = dqp + _dot(ds, km)
                    dkp = dkp + _dot(ds_t, qm)
                dv_ref[rs, vs] = jnp.concatenate(dv_parts, axis=1).astype(BF16)
                dstate[p] = gs_ref[p] * dst + bd_ref[...] * _dot_tn(qd, do_b)
                dq_parts.append(dqp)
                dk_parts.append(dkp)
            cos = _tile_lanes(cos_ref[rs, :], d_qk // LANES)
            sin = _tile_lanes(sin_ref[rs, :], d_qk // LANES)
            dq = jnp.concatenate(dq_parts, axis=1)
            dk = jnp.concatenate(dk_parts, axis=1) * (QK_DIM ** -0.5)
            dq_ref[rs, :] = (dq * cos + _rot_partner(dq * sin)).astype(BF16)
            dk_ref[rs, :] = (dk * cos + _rot_partner(dk * sin)).astype(BF16)

    last = n_ch // cps - 1
    ch = lambda w: pl.BlockSpec((tm, w), lambda n: (last - n, 0))
    const2 = lambda a: pl.BlockSpec(a.shape, lambda n: (0, 0))
    const3 = lambda a: pl.BlockSpec(a.shape, lambda n: (0, 0, 0))
    return _hosted_call(
        body, ride, n_ch // cps,
        name="ret_bwd",
        in_specs=[ch(d_qk), ch(d_qk), ch(d_ret), ch(d_ret), ch(d_ret),
                  pl.BlockSpec((cps, n_pairs, 2 * QK_DIM, 2 * LANES), lambda n: (last - n, 0, 0, 0)),
                  ch(d_ret), const2(gain), ch(LANES), ch(LANES),
                  const3(dmask), const3(dmask_t), const2(qdec), const2(kdec), const3(g_state), const2(block_diag)],
        out_specs=[ch(d_qk), ch(d_qk), ch(d_ret), ch(d_ret), pl.BlockSpec((1, d_ret), lambda n: (0, 0))],
        out_shape=[jax.ShapeDtypeStruct((tp, d_qk), BF16), jax.ShapeDtypeStruct((tp, d_qk), BF16),
                   jax.ShapeDtypeStruct((tp, d_ret), BF16), jax.ShapeDtypeStruct((tp, d_ret), BF16),
                   jax.ShapeDtypeStruct((1, d_ret), F32)],
        scratch_shapes=[pltpu.VMEM((n_pairs, 2 * QK_DIM, 2 * LANES), F32)],
        args=(q, k, v, rg, o, rprev, dy, gain, cos_t, sin_t, dmask, dmask_t, qdec, kdec, g_state, block_diag),
    )


def _lru_bwd(lx, lg, hl, dy, saved, cw, wr, wi, lam, tm, ride=None):
    tp, w = lx.shape
    nt = tp // tm
    per8 = tm // SUBLANES
    n_heads = wr.shape[0]

    def body(lx_ref, lg_ref, hl_ref, hlp_ref, dy_ref, xc_ref, r_ref, ig_ref, a_ref, beta_ref, w4_ref,
             cw_ref, wr_ref, wi_ref, lam_ref,
             dlx_ref, dlg_ref, dcw_ref, dcb_ref, dwr_ref, dbr_ref, dwi_ref, dbi_ref, dlam_ref,
             g_s, dh_s, b_s, p_s, carry, dxc_next, a_next):
        i = pl.program_id(0)
        first_tile = i == nt - 1

        @pl.when(i == 0)
        def _():
            carry[...] = jnp.zeros_like(carry)
            dxc_next[...] = jnp.zeros_like(dxc_next)
            a_next[...] = jnp.zeros_like(a_next)
            for r in (dcw_ref, dcb_ref, dwr_ref, dbr_ref, dwi_ref, dbi_ref, dlam_ref):
                r[...] = jnp.zeros_like(r)

        lxv = lx_ref[...]
        a, beta, r, ig, xc = a_ref[...], beta_ref[...], r_ref[...], ig_ref[...], xc_ref[...]
        g = lg_ref[...]
        sg = _sigmoid(g)
        dyv = dy_ref[...]
        hlv = hl_ref[...]
        dlg_ref[...] = (dyv * hlv * (sg * (1.0 + g * (1.0 - sg)))).astype(BF16)
        _to_groups(g_s, dyv * (g * sg))
        _to_groups(b_s, _shift_up(a, a_next[...], 1))
        carry[0:1, :] = _segment_scan(b_s, g_s, dh_s, p_s, carry[0:1, :], tm, reverse=True)
        a_next[...] = a[0:SUBLANES]
        dh = _from_groups(dh_s)
        hprev = _shift_down(hlv, jnp.where(first_tile, 0.0, hlp_ref[...]), 1)
        row = lax.broadcasted_iota(jnp.int32, (tm, 1), 0) + (nt - 1 - i) * tm
        du = jnp.where(row >= PAD_ROWS, dh, 0.0)
        da = dh * hprev
        dbeta = du * ig * xc
        d_ig = du * beta * xc
        dxc = du * beta * ig
        dloga = da * a - dbeta * w4_ref[...]
        lam_v = lam_ref[...]
        dlam_ref[...] += jnp.sum(dloga * r, axis=0, keepdims=True) * (LRU_C * _sigmoid(-lam_v))
        dpr = (dloga * (-LRU_C * _softplus_neg(lam_v))) * r * (1.0 - r)
        dpi = d_ig * ig * (1.0 - ig)
        dbr_ref[...] += jnp.sum(dpr, axis=0, keepdims=True)
        dbi_ref[...] += jnp.sum(dpi, axis=0, keepdims=True)
        dxc_parts = []
        for hd in range(n_heads):
            hs = slice(hd * LANES, (hd + 1) * LANES)
            xh = xc[:, hs].astype(BF16)
            dprh = dpr[:, hs].astype(BF16)
            dpih = dpi[:, hs].astype(BF16)
            dwr_ref[hd] += _dot_tn(xh, dprh)
            dwi_ref[hd] += _dot_tn(xh, dpih)
            dxc_parts.append(_dot_nt(dprh, wr_ref[hd].astype(BF16)) + _dot_nt(dpih, wi_ref[hd].astype(BF16)))
        dxc = dxc + jnp.concatenate(dxc_parts, axis=1)
        nxt = dxc_next[...]
        up1, up2, up3 = _shift_up(dxc, nxt, 1), _shift_up(dxc, nxt, 2), _shift_up(dxc, nxt, 3)
        dlx = dxc * cw_ref[3:4, :]
        dlx = dlx + up1 * cw_ref[2:3, :]
        dlx = dlx + up2 * cw_ref[1:2, :]
        dlx = dlx + up3 * cw_ref[0:1, :]
        dlx_ref[...] = dlx.astype(BF16)
        dxc_next[...] = dxc[0:SUBLANES]
        dcb_ref[...] += jnp.sum(dxc, axis=0, keepdims=True)
        dcw_ref[0:1, :] += jnp.sum(up3 * lxv, axis=0, keepdims=True)
        dcw_ref[1:2, :] += jnp.sum(up2 * lxv, axis=0, keepdims=True)
        dcw_ref[2:3, :] += jnp.sum(up1 * lxv, axis=0, keepdims=True)
        dcw_ref[3:4, :] += jnp.sum(dxc * lxv, axis=0, keepdims=True)

    last = nt - 1
    tile = pl.BlockSpec((tm, w), lambda i: (last - i, 0))
    prev = pl.BlockSpec((SUBLANES, w), lambda i: (jnp.maximum((last - i) * per8 - 1, 0), 0))
    vec = pl.BlockSpec((1, w), lambda i: (0, 0))
    mat = pl.BlockSpec(wr.shape, lambda i: (0, 0, 0))
    cwb = pl.BlockSpec(cw.shape, lambda i: (0, 0))
    return _hosted_call(
        body, ride, nt,
        name="lru_bwd",
        in_specs=[tile, tile, tile, prev, tile] + [tile] * 6 + [cwb, mat, mat, vec],
        out_specs=[tile, tile, cwb, vec, mat, vec, mat, vec, vec],
        out_shape=[jax.ShapeDtypeStruct((tp, w), BF16), jax.ShapeDtypeStruct((tp, w), BF16),
                   jax.ShapeDtypeStruct(cw.shape, F32), jax.ShapeDtypeStruct((1, w), F32),
                   jax.ShapeDtypeStruct(wr.shape, F32), jax.ShapeDtypeStruct((1, w), F32),
                   jax.ShapeDtypeStruct(wr.shape, F32), jax.ShapeDtypeStruct((1, w), F32),
                   jax.ShapeDtypeStruct((1, w), F32)],
        scratch_shapes=[pltpu.VMEM((w // LANES, tm, LANES), F32)] * 4 + [pltpu.VMEM((SUBLANES, w), F32)] * 3,
        args=(lx, lg, hl, hl, dy, *saved, cw, wr, wi, lam),
    )


def _in_proj_dw(dparts, hp, gain, wg_shape, ride=None):
    tp, d = hp.shape
    n_ch = tp // CHUNK
    per = next(p for p in (4, 2, 5, 3, 1) if (n_ch - 1) % p == 0)
    n_steps = 1 + (n_ch - 1) // per
    widths = [p.shape[1] for p in dparts]
    segs = _proj_segments(widths[0], widths[2], widths[4], wg_shape[2])

    def body(*refs):
        dp = [refs[p * per:(p + 1) * per] for p in range(6)]
        hp_b = refs[6 * per:7 * per]
        g_ref, dwg_ref, acc, sem = refs[7 * per:]
        i = pl.program_id(0)

        def accumulate(blocks):
            h = jnp.concatenate([hp_b[b][...] for b in blocks], axis=0)
            rinv = lax.rsqrt(jnp.mean(h * h, axis=-1, keepdims=True) + EPS)
            u = ((h * rinv) * g_ref[...]).astype(BF16)
            for p_refs, parts in zip(dp, segs):
                for jj, inner, off, take in parts:
                    seg = jnp.concatenate([p_refs[b][:, off:off + take] for b in blocks], axis=0)
                    acc[jj, :, inner:inner + take] += _dot_tn(u, seg)

        @pl.when(i == 0)
        def _():
            acc[...] = jnp.zeros_like(acc)
            accumulate([0])

        @pl.when(i > 0)
        def _():
            accumulate(list(range(per)))

        @pl.when(i == n_steps - 1)
        def _():
            cp = pltpu.make_async_copy(acc, dwg_ref, sem)
            cp.start()
            cp.wait()

    def blocks(w):
        return [pl.BlockSpec((CHUNK, w), functools.partial(
            lambda i, b: (jnp.where(i == 0, b, per * (i - 1) + 1 + b), 0), b=b)) for b in range(per)]

    in_specs, args = [], []
    for a, w in list(zip(dparts, widths)) + [(hp, d)]:
        in_specs += blocks(w)
        args += [a] * per
    outs, rides = _hosted_call(
        body, ride, n_steps,
        name="in_proj_dw",
        in_specs=in_specs + [pl.BlockSpec(gain.shape, lambda i: (0, 0))],
        out_specs=[ANY_SPEC],
        out_shape=[jax.ShapeDtypeStruct(wg_shape, F32)],
        scratch_shapes=[pltpu.VMEM(wg_shape, F32), pltpu.SemaphoreType.DMA],
        args=(*args, gain),
    )
    return outs[0], rides


def _in_proj_dx(dparts, hp, dh2, gain, wg, s_len, tm, ride=None):
    tp, d = hp.shape
    nt = tp // tm
    widths = [p.shape[1] for p in dparts]
    segs = _proj_segments(widths[0], widths[2], widths[4], wg.shape[2])

    def body(*refs):
        dp = refs[:6]
        hp_ref, dh2_ref, g_ref, w_ref = refs[6:10]
        gx_ref, dmeta_ref, dg_ref = refs[10:13]
        stage, sems = refs[13:]
        i = pl.program_id(0)

        @pl.when(i == 0)
        def _():
            dg_ref[...] = jnp.zeros_like(dg_ref)

        h = hp_ref[...]
        rinv = lax.rsqrt(jnp.mean(h * h, axis=-1, keepdims=True) + EPS)
        nrm = h * rinv
        gv = g_ref[...]
        du = jnp.zeros((tm, d), F32)
        for p_ref, parts in zip(dp, segs):
            for jj, inner, off, take in parts:
                du = du + _dot_nt(p_ref[:, off:off + take], w_ref[jj, :, inner:inner + take])
        dg_ref[...] += jnp.sum(du * nrm, axis=0, keepdims=True)
        dn = du * gv
        dh = dh2_ref[...] + rinv * (dn - nrm * jnp.mean(dn * nrm, axis=-1, keepdims=True))

        def first_copy():
            return pltpu.make_async_copy(stage.at[0, pl.ds(CHUNK, tm - CHUNK), :],
                                         gx_ref.at[pl.ds(0, tm - CHUNK), :], sems.at[0])

        def tile_copy(slot, start):
            return pltpu.make_async_copy(stage.at[slot], gx_ref.at[pl.ds(start, tm), :], sems.at[slot])

        @pl.when(i == 0)
        def _():
            dmeta_ref[...] = dh[PAD_ROWS:CHUNK]
            stage[0] = dh
            first_copy().start()

        @pl.when(i > 0)
        def _():
            slot = 1 + i % 2

            @pl.when(i >= 3)
            def _():
                tile_copy(slot, 0).wait()

            stage[slot] = dh
            tile_copy(slot, pl.multiple_of(i * tm - CHUNK, CHUNK)).start()

        @pl.when(i == nt - 1)
        def _():
            first_copy().wait()
            for step in (nt - 2, nt - 1):
                if step >= 1:
                    tile_copy(1 + step % 2, 0).wait()

    tile = lambda w: pl.BlockSpec((tm, w), lambda i: (i, 0))
    return _hosted_call(
        body, ride, nt,
        name="in_proj_dx",
        in_specs=[tile(w) for w in widths] + [tile(d), tile(d), pl.BlockSpec(gain.shape, lambda i: (0, 0)),
                                              pl.BlockSpec(wg.shape, lambda i: (0, 0, 0))],
        out_specs=[ANY_SPEC, pl.BlockSpec((N_META, d), lambda i: (0, 0)), pl.BlockSpec((1, d), lambda i: (0, 0))],
        out_shape=[jax.ShapeDtypeStruct((s_len, d), F32), jax.ShapeDtypeStruct((N_META, d), F32),
                   jax.ShapeDtypeStruct((1, d), F32)],
        scratch_shapes=[pltpu.VMEM((3, tm, d), F32), pltpu.SemaphoreType.DMA((3,))],
        args=(*dparts, hp, dh2, gain, wg),
    )


def _pair_sum(buf, recv, c_arr, tr, name):
    _, rows, cols = buf.shape

    def body(c_ref, mine_ref, got_ref, out_ref):
        out_ref[...] = (mine_ref[...] + got_ref[...]).astype(BF16)

    grid_spec = pltpu.PrefetchScalarGridSpec(
        num_scalar_prefetch=1,
        grid=(N_CHIPS, rows // tr),
        in_specs=[pl.BlockSpec((1, tr, cols), lambda jj, r, c_ref: (2 * jj + c_ref[0], r, 0)),
                  pl.BlockSpec((1, tr, cols), lambda jj, r, c_ref: (jj, r, 0))],
        out_specs=pl.BlockSpec((1, tr, cols), lambda jj, r, c_ref: (jj, r, 0)),
    )
    return pl.pallas_call(
        body,
        name=name,
        grid_spec=grid_spec,
        out_shape=jax.ShapeDtypeStruct((N_CHIPS, rows, cols), BF16),
    )(c_arr, buf, recv)


def _pair_exchange_sum(buf, c_arr, tr, name):
    _, rows, cols = buf.shape
    per = rows // tr

    def body(c_ref, src_ref, mine_ref, out_ref, got, send_sems, recv_sems):
        jj, r = pl.program_id(0), pl.program_id(1)
        x, y, c, _ = _position()
        copies = [_remote(src_ref.at[2 * k + 1 - c], got.at[k], send_sems, recv_sems, k, (x, y, 1 - c))
                  for k in range(N_CHIPS)]

        @pl.when((jj == 0) & (r == 0))
        def _():
            for cp in copies:
                cp.start()

        for k in range(N_CHIPS):
            @pl.when((jj == k) & (r == 0))
            def _():
                copies[k].wait_recv()

        rows_r = pl.ds(pl.multiple_of(r * tr, tr), tr)
        out_ref[0] = (mine_ref[0] + got[jj, rows_r, :]).astype(BF16)

        @pl.when((jj == N_CHIPS - 1) & (r == per - 1))
        def _():
            for cp in copies:
                cp.wait_send()

    grid_spec = pltpu.PrefetchScalarGridSpec(
        num_scalar_prefetch=1,
        grid=(N_CHIPS, per),
        in_specs=[ANY_SPEC, pl.BlockSpec((1, tr, cols), lambda jj, r, c_ref: (2 * jj + c_ref[0], r, 0))],
        out_specs=pl.BlockSpec((1, tr, cols), lambda jj, r, c_ref: (jj, r, 0)),
        scratch_shapes=[pltpu.VMEM((N_CHIPS, rows, cols), F32), pltpu.SemaphoreType.DMA((N_CHIPS,)),
                        pltpu.SemaphoreType.DMA((N_CHIPS,))],
    )
    return pl.pallas_call(
        body,
        name=name,
        grid_spec=grid_spec,
        out_shape=jax.ShapeDtypeStruct((N_CHIPS, rows, cols), BF16),
        compiler_params=pltpu.CompilerParams(dimension_semantics=("arbitrary", "arbitrary"),
                                             vmem_limit_bytes=VMEM_LIMIT),
    )(c_arr, buf, buf)


def _chip_sum(mine, got, j_arr, tr, name, loss_part=None):
    _, rows, cols = got.shape
    extra = [] if loss_part is None else [loss_part]

    def body(j_ref, mine_ref, got_ref, *rest):
        out_ref = rest[-1]
        j = j_ref[0]
        acc = None
        for jj in range(N_CHIPS):
            term = jnp.where(j == jj, mine_ref[0], got_ref[jj]).astype(F32)
            acc = term if acc is None else acc + term
        out_ref[...] = acc
        if loss_part is not None:
            out_ref[ROW_LOSS:ROW_LOSS + 1, :] = rest[0][0:1, :]

    grid_spec = pltpu.PrefetchScalarGridSpec(
        num_scalar_prefetch=1,
        grid=(rows // tr,),
        in_specs=[pl.BlockSpec((1, tr, cols), lambda r, j_ref: (j_ref[0], r, 0)),
                  pl.BlockSpec((N_CHIPS, tr, cols), lambda r, j_ref: (0, r, 0))] +
                 [pl.BlockSpec(e.shape, lambda r, j_ref: (0, 0)) for e in extra],
        out_specs=pl.BlockSpec((tr, cols), lambda r, j_ref: (r, 0)),
    )
    return pl.pallas_call(
        body,
        name=name,
        grid_spec=grid_spec,
        out_shape=jax.ShapeDtypeStruct((rows, cols), F32),
    )(j_arr, mine, got, *extra)


def _finish_exchange(f_in, f_small):
    def body(fin_ref, fs_ref, rin_ref, os_ref, send_sems, recv_sems, local_sem):
        x, y, c, chips = _position()
        j = 2 * x + y
        me = 2 * j + c
        sibling = (x, y, 1 - c)
        local = pltpu.make_async_copy(fs_ref, os_ref.at[me], local_sem)
        local.start()

        def copy(k, src, dst, to):
            return _remote(src, dst, send_sems, recv_sems, k, to)

        first = [copy(0, fin_ref, rin_ref, sibling), copy(1, fs_ref, os_ref.at[me], sibling)]
        first += [copy(2 + k, fs_ref, os_ref.at[me], (cx, cy, c)) for k, (cx, cy) in enumerate(chips)]
        for cp in first:
            cp.start()
        passed = []
        for k, (cx, cy) in enumerate(chips):
            unit = 2 * (2 * cx + cy) + c
            copy(2 + k, fs_ref, os_ref.at[unit], sibling).wait_recv()
            fwd = copy(5 + k, os_ref.at[unit], os_ref.at[unit], sibling)
            fwd.start()
            passed.append(fwd)
        copy(0, fin_ref, rin_ref, sibling).wait_recv()
        copy(1, fs_ref, os_ref.at[2 * j + 1 - c], sibling).wait_recv()
        for k, (cx, cy) in enumerate(chips):
            unit = 2 * (2 * cx + cy) + 1 - c
            copy(5 + k, fs_ref, os_ref.at[unit], sibling).wait_recv()
        for cp in first + passed:
            cp.wait_send()
        local.wait()

    return pl.pallas_call(
        body,
        name="grad_finish_exchange",
        in_specs=[ANY_SPEC] * 2,
        out_specs=[ANY_SPEC] * 2,
        out_shape=[jax.ShapeDtypeStruct(f_in.shape, F32), jax.ShapeDtypeStruct((N_DEV,) + f_small.shape, F32)],
        scratch_shapes=[pltpu.SemaphoreType.DMA((8,)), pltpu.SemaphoreType.DMA((8,)), pltpu.SemaphoreType.DMA],
    )(f_in, f_small)


def _adamw_math(w, g, m, v):
    m = ADAM_B1 * m + (1.0 - ADAM_B1) * g
    v = ADAM_B2 * v + (1.0 - ADAM_B2) * (g * g)
    m_hat = m / (1.0 - ADAM_B1 ** ADAM_STEP)
    v_hat = v / (1.0 - ADAM_B2 ** ADAM_STEP)
    delta = -ADAM_LR * (m_hat / (jnp.sqrt(v_hat) + ADAM_EPS) + ADAM_WD * w)
    return delta, m, v


def _adamw_big(w, g_mine, g_sib, m, v, c_arr, tr, name):
    rows, cols = w.shape
    half = rows // 2
    per = half // tr

    def body(c_ref, w_ref, gm_ref, gs_ref, m_ref, v_ref, g_ref, d_ref, mo_ref, vo_ref):
        g = jnp.where(pl.program_id(0) == c_ref[0], gm_ref[...], gs_ref[...])
        g_ref[...] = g
        d_ref[...], mo_ref[...], vo_ref[...] = _adamw_math(w_ref[...], g, m_ref[...], v_ref[...])

    full = pl.BlockSpec((tr, cols), lambda h, r, c_ref: (h * per + r, 0))
    unit = pl.BlockSpec((tr, cols), lambda h, r, c_ref: (r, 0))
    grid_spec = pltpu.PrefetchScalarGridSpec(
        num_scalar_prefetch=1,
        grid=(2, per),
        in_specs=[full, unit, unit, full, full],
        out_specs=[full] * 4,
    )
    return pl.pallas_call(
        body,
        name=name,
        grid_spec=grid_spec,
        out_shape=[jax.ShapeDtypeStruct(w.shape, F32)] * 4,
    )(c_arr, w, g_mine, g_sib, m, v)


def _adamw_small(j_arr, packed, params):
    names = list(params)
    n = len(names)
    vec_names = ["norm_gain", "conv_b", "b_rg", "b_ig", "lru_lambda", "ret_norm_gain", "final_norm_gain"]

    def body(j_ref, pk_ref, *refs):
        ins = refs[:3 * n]
        outs = refs[3 * n:]
        j = j_ref[0]

        def shard(row, rows):
            return jnp.concatenate([pk_ref[2 * j, row:row + rows, :], pk_ref[2 * j + 1, row:row + rows, :]], axis=1)

        def tail_sum(unit, row, rows):
            start = pl.multiple_of(UNIT_ROWS + TAIL_ROWS * unit + row, SUBLANES)
            total = pk_ref[0, pl.ds(start, rows), :]
            for dev in range(1, N_DEV):
                total = total + pk_ref[dev, pl.ds(start, rows), :]
            return total

        for idx, name in enumerate(names):
            if name == "w_rg":
                g = pk_ref[:, ROW_WR:ROW_WR + LANES, :]
            elif name == "w_ig":
                g = pk_ref[:, ROW_WI:ROW_WI + LANES, :]
            elif name == "meta_tokens":
                g = jnp.concatenate([tail_sum(2 * j, 0, N_META), tail_sum(2 * j + 1, 0, N_META)], axis=1)
            elif name == "norm_gain":
                g = jnp.concatenate([tail_sum(u, N_META, SUBLANES)[0:1] for u in range(N_DEV)], axis=1)
            elif name == "conv_w":
                g = shard(ROW_CONV, 4)
            else:
                row = ROW_VEC + vec_names.index(name)
                g = jnp.concatenate([pk_ref[u, row:row + 1, :] for u in range(N_DEV)], axis=1)
            w_ref, m_ref, v_ref = ins[3 * idx:3 * idx + 3]
            delta, m, v = _adamw_math(w_ref[...], g, m_ref[...], v_ref[...])
            g_ref, d_ref, mo_ref, vo_ref = outs[4 * idx:4 * idx + 4]
            g_ref[...], d_ref[...], mo_ref[...], vo_ref[...] = g, delta, m, v
        total = pk_ref[0, ROW_LOSS:ROW_LOSS + 1, :]
        for u in range(1, N_DEV):
            total = total + pk_ref[u, ROW_LOSS:ROW_LOSS + 1, :]
        outs[4 * n][...] = jnp.broadcast_to(total, (SUBLANES, LANES))

    flat_in, out_shape = [], []
    for name in names:
        w, m, v = params[name]
        flat_in += [w, m, v]
        out_shape += [jax.ShapeDtypeStruct(w.shape, F32)] * 4
    out_shape.append(jax.ShapeDtypeStruct((SUBLANES, LANES), F32))
    res = pl.pallas_call(
        body,
        name="adamw_small",
        in_specs=[SMEM_SPEC, VMEM_SPEC] + [VMEM_SPEC] * (3 * n),
        out_specs=[VMEM_SPEC] * (4 * n + 1),
        out_shape=out_shape,
    )(j_arr, packed, *flat_in)
    return {name: tuple(res[4 * idx:4 * idx + 4]) for idx, name in enumerate(names)}, res[4 * n][0, 0]


def _units(a):
    rows = a.shape[0]
    return jnp.transpose(a.reshape(rows, N_DEV, LANES), (1, 0, 2))


def kernel(x, meta_tokens, norm_gain, w_in, conv_w, conv_b, w_rg, b_rg, w_ig, b_ig, lru_lambda, ret_norm_gain, w_out, final_norm_gain, loss_target, m_meta_tokens, m_norm_gain, m_w_in, m_conv_w, m_conv_b, m_w_rg, m_b_rg, m_w_ig, m_b_ig, m_lru_lambda, m_ret_norm_gain, m_w_out, m_final_norm_gain, v_meta_tokens, v_norm_gain, v_w_in, v_conv_w, v_conv_b, v_w_rg, v_b_rg, v_w_ig, v_b_ig, v_lru_lambda, v_ret_norm_gain, v_w_out, v_final_norm_gain):
    s_len, d = x.shape[1], x.shape[2]
    d_lru = w_rg.shape[1] * w_rg.shape[2]
    d_ret = ret_norm_gain.shape[1]
    d_qk = HEADS * QK_DIM
    tp = s_len + CHUNK
    tm = TOKEN_TILE
    assert tp % tm == 0 and d_lru == HEADS * LANES and d_ret == HEADS * LANES
    ax, ay, ac = lax.axis_index("x"), lax.axis_index("y"), lax.axis_index("c")
    c_arr = jnp.reshape(ac, (1,)).astype(jnp.int32)
    j_arr = jnp.reshape(2 * ax + ay, (1,)).astype(jnp.int32)

    small = jnp.concatenate([meta_tokens, conv_w[0], jnp.zeros((4, meta_tokens.shape[1]), F32)], axis=0)
    wg, sg = _gather_weights(w_in[0], small)
    cols = sg.shape[2]
    meta_full = jnp.transpose(sg[:, :N_META, :], (1, 0, 2)).reshape(N_META, N_CHIPS * cols)
    cw_full = jnp.transpose(sg[:, N_META:N_META + 4, :], (1, 0, 2)).reshape(4, N_CHIPS * cols)
    cw8 = jnp.concatenate([cw_full, jnp.zeros((4, cw_full.shape[1]), F32)], axis=0)

    half = QK_DIM // 2
    inv = ROPE_BASE ** (-jnp.arange(half, dtype=F32) / half)
    pos = (jnp.arange(tp) - PAD_ROWS).astype(F32)
    ang = pos[:, None] * inv[None, :]
    cos_t = jnp.tile(jnp.cos(ang), (1, LANES // half))
    sign = jnp.where((jnp.arange(LANES) % QK_DIM) < half, -1.0, 1.0).astype(F32)
    sin_t = jnp.tile(jnp.sin(ang), (1, LANES // half)) * sign[None, :]
    tables = _ret_tables()
    gain_f = final_norm_gain.reshape(1, d)

    hp, lx, lg, q, k, v, rg, wo4 = _in_proj(x[0], meta_full, norm_gain, wg, cos_t, sin_t, w_out[0], tm,
                                            d_lru, d_qk, d_ret)
    wo = wo4.reshape(N_CHIPS * wo4.shape[1], wo4.shape[2])
    hl, y_lru, *lru_saved = _lru_fwd(lx, lg, cw8, conv_b, w_rg[0], b_rg, w_ig[0], b_ig, lru_lambda, tm)
    o, y_ret, rprev = _ret_fwd(q, k, v, rg, ret_norm_gain, tables, tm)
    dh2, dy_lru, dy_ret, dwo, dgf, loss_acc = _out_proj_loss(y_lru, y_ret, hp, loss_target[0], wo, gain_f, tm)

    g_out = dwo.reshape(N_DEV, dwo.shape[0] // N_DEV, dwo.shape[1])
    (dq, dk, dv, drg, dgain), (r_out,) = _ret_bwd(q, k, v, rg, o, rprev, dy_ret, ret_norm_gain, cos_t, sin_t, tables,
                                                 tm, ride=_pair_ride([g_out]))
    q_out = _pair_sum(g_out, r_out, c_arr, 128, "grad_pair_sum_out")
    (dlx, dlg, dcw, dcb, dwr, dbr, dwi, dbi, dlam), (e_out,) = _lru_bwd(
        lx, lg, hl, dy_lru, lru_saved, cw8, w_rg[0], w_ig[0], lru_lambda, tm, ride=_chip_ride([q_out]))
    f_out = _chip_sum(q_out, e_out, j_arr, 128, "grad_chip_sum_out")
    zero_row = jnp.zeros((1, d), F32)
    vecs = [zero_row, dcb, dbr, dbi, dlam, dgain, dgf]
    g_small = jnp.concatenate([dwr, dwi, jnp.zeros((N_DEV, N_META, LANES), F32), _units(dcw[0:4])]
                              + [_units(a) for a in vecs]
                              + [jnp.zeros((N_DEV, UNIT_ROWS - ROW_VEC - N_VEC, LANES), F32)], axis=1)
    dparts = [dlx, dlg, dq, dk, dv, drg]
    dwg, (s_out, r_small) = _in_proj_dw(dparts, hp, norm_gain, wg.shape,
                                        ride=_join_rides(_sibling_ride([f_out]), _pair_ride([g_small])))
    g_in = dwg.reshape(N_DEV, dwg.shape[1] // 2, dwg.shape[2])
    q_in = _pair_exchange_sum(g_in, c_arr, 128, "grad_pair_exchange_sum_in")
    q_small = _pair_sum(g_small, r_small, c_arr, UNIT_ROWS, "grad_pair_sum_small")
    (grad_x, dmeta, dg1), (e_in, e_small) = _in_proj_dx(dparts, hp, dh2, norm_gain, wg, s_len, tm,
                                                        ride=_chip_ride([q_in, q_small]))
    f_in = _chip_sum(q_in, e_in, j_arr, 128, "grad_chip_sum_in")
    f_small = _chip_sum(q_small, e_small, j_arr, UNIT_ROWS, "grad_chip_sum_small", loss_part=loss_acc)
    tail = jnp.concatenate([_units(dmeta), _units(dg1), jnp.zeros((N_DEV, TAIL_ROWS - N_META - 1, LANES), F32)],
                           axis=1).reshape(N_DEV * TAIL_ROWS, LANES)
    s_in, o_small = _finish_exchange(f_in, jnp.concatenate([f_small, tail], axis=0))

    res_in = _adamw_big(w_in[0], f_in, s_in, m_w_in[0], v_w_in[0], c_arr, 256, "adamw_w_in")
    res_out = _adamw_big(w_out[0], f_out, s_out, m_w_out[0], v_w_out[0], c_arr, 256, "adamw_w_out")
    small_params = {
        "meta_tokens": (meta_tokens, m_meta_tokens, v_meta_tokens),
        "norm_gain": (norm_gain, m_norm_gain, v_norm_gain),
        "conv_w": (conv_w[0], m_conv_w[0], v_conv_w[0]),
        "conv_b": (conv_b, m_conv_b, v_conv_b),
        "w_rg": (w_rg[0], m_w_rg[0], v_w_rg[0]),
        "b_rg": (b_rg, m_b_rg, v_b_rg),
        "w_ig": (w_ig[0], m_w_ig[0], v_w_ig[0]),
        "b_ig": (b_ig, m_b_ig, v_b_ig),
        "lru_lambda": (lru_lambda, m_lru_lambda, v_lru_lambda),
        "ret_norm_gain": (ret_norm_gain, m_ret_norm_gain, v_ret_norm_gain),
        "final_norm_gain": (gain_f, m_final_norm_gain.reshape(1, d), v_final_norm_gain.reshape(1, d)),
    }
    res, loss = _adamw_small(j_arr, o_small, small_params)
    res["w_in"] = tuple(res_in)
    res["w_out"] = tuple(res_out)

    order = ["meta_tokens", "norm_gain", "w_in", "conv_w", "conv_b", "w_rg", "b_rg", "w_ig", "b_ig", "lru_lambda",
             "ret_norm_gain", "w_out", "final_norm_gain"]
    shapes = {"w_in": w_in.shape, "conv_w": conv_w.shape, "w_rg": w_rg.shape, "w_ig": w_ig.shape,
              "w_out": w_out.shape, "final_norm_gain": final_norm_gain.shape}
    outs = [loss, grad_x.reshape(x.shape)]
    for kind in range(4):
        for name in order:
            a = res[name][kind]
            outs.append(a.reshape(shapes[name]) if name in shapes else a)
    return tuple(outs)
```

```python
import functools

import jax
import jax.numpy as jnp
from jax import lax
from jax.experimental import pallas as pl
from jax.experimental.pallas import tpu as pltpu

F32 = jnp.float32
BF16 = jnp.bfloat16

N_META = 16
CHUNK = 128
PAD_ROWS = CHUNK - N_META
HEADS = 8
QK_DIM = 64
LANES = 128
SUBLANES = 8
LRU_C = 8.0
EPS = 1e-6
ROPE_BASE = 10000.0
ADAM_LR = 0.001
ADAM_B1 = 0.9
ADAM_B2 = 0.999
ADAM_EPS = 1e-08
ADAM_WD = 0.01
ADAM_STEP = 10
N_CHIPS = 4
N_DEV = 8
TOKEN_TILE = 384
VMEM_LIMIT = 58 * 1024 * 1024
MESH = pl.DeviceIdType.MESH

VMEM_SPEC = pl.BlockSpec(memory_space=pltpu.VMEM)
SMEM_SPEC = pl.BlockSpec(memory_space=pltpu.SMEM)
ANY_SPEC = pl.BlockSpec(memory_space=pl.ANY)

ROW_WR, ROW_WI, ROW_META, ROW_CONV, ROW_VEC, UNIT_ROWS = 0, 128, 256, 272, 276, 288
N_VEC = 7
ROW_LOSS = ROW_VEC + N_VEC
TAIL_ROWS = 24


def _dot(a, b):
    return jnp.dot(a, b, preferred_element_type=F32)


def _dot_nt(a, b):
    return lax.dot_general(a, b, (((1,), (1,)), ((), ())), preferred_element_type=F32)


def _dot_tn(a, b):
    return lax.dot_general(a, b, (((0,), (0,)), ((), ())), preferred_element_type=F32)


def _sigmoid(x):
    return 0.5 * jnp.tanh(0.5 * x) + 0.5


def _shift_down(x, prev8, s):
    rolled = pltpu.roll(x, s, 0)
    rows = lax.broadcasted_iota(jnp.int32, (SUBLANES, x.shape[1]), 0)
    top = jnp.where(rows < s, pltpu.roll(prev8, s, 0), rolled[0:SUBLANES])
    return jnp.concatenate([top, rolled[SUBLANES:]], axis=0)


def _shift_up(x, next8, s):
    n = x.shape[0]
    rolled = pltpu.roll(x, n - s, 0)
    rows = lax.broadcasted_iota(jnp.int32, (SUBLANES, x.shape[1]), 0)
    bot = jnp.where(rows >= SUBLANES - s, pltpu.roll(next8, SUBLANES - s, 0), rolled[n - SUBLANES:n])
    return jnp.concatenate([rolled[:n - SUBLANES], bot], axis=0)


def _rot_partner(t):
    w = t.shape[1]
    lane = lax.broadcasted_iota(jnp.int32, t.shape, 1)
    first = (lane % QK_DIM) < (QK_DIM // 2)
    return jnp.where(first, pltpu.roll(t, w - QK_DIM // 2, 1), pltpu.roll(t, QK_DIM // 2, 1))


def _tile_lanes(t, reps):
    return jnp.concatenate([t] * reps, axis=1)


class _Ride:
    def __init__(self, srcs, dst_shapes, n_copies, make):
        self.srcs, self.dst_shapes, self.n_copies, self.make = list(srcs), list(dst_shapes), n_copies, make


def _join_rides(a, b):
    def make(src, dst, send_sems, recv_sems, base):
        na, da = len(a.srcs), len(a.dst_shapes)
        return (a.make(src[:na], dst[:da], send_sems, recv_sems, base)
                + b.make(src[na:], dst[da:], send_sems, recv_sems, base + a.n_copies))

    return _Ride(a.srcs + b.srcs, a.dst_shapes + b.dst_shapes, a.n_copies + b.n_copies, make)


def _position():
    x, y, c = lax.axis_index("x"), lax.axis_index("y"), lax.axis_index("c")
    return x, y, c, [(1 - x, y), (x, 1 - y), (1 - x, 1 - y)]


def _remote(src, dst, send_sems, recv_sems, k, to):
    return pltpu.make_async_remote_copy(src_ref=src, dst_ref=dst, send_sem=send_sems.at[k], recv_sem=recv_sems.at[k],
                                        device_id=to, device_id_type=MESH)


def _pair_ride(bufs):
    def make(src, dst, send_sems, recv_sems, base):
        x, y, c, _ = _position()
        return [_remote(src[b].at[2 * jj + 1 - c], dst[b].at[jj], send_sems, recv_sems, base + b * N_CHIPS + jj,
                        (x, y, 1 - c)) for b in range(len(bufs)) for jj in range(N_CHIPS)]

    shapes = [jax.ShapeDtypeStruct((N_CHIPS,) + b.shape[1:], b.dtype) for b in bufs]
    return _Ride(bufs, shapes, N_CHIPS * len(bufs), make)


def _chip_ride(bufs):
    def make(src, dst, send_sems, recv_sems, base):
        x, y, c, chips = _position()
        return [_remote(src[b].at[2 * cx + cy], dst[b].at[2 * x + y], send_sems, recv_sems, base + b * 3 + k,
                        (cx, cy, c)) for b in range(len(bufs)) for k, (cx, cy) in enumerate(chips)]

    shapes = [jax.ShapeDtypeStruct(b.shape, b.dtype) for b in bufs]
    return _Ride(bufs, shapes, 3 * len(bufs), make)


def _sibling_ride(bufs):
    def make(src, dst, send_sems, recv_sems, base):
        x, y, c, _ = _position()
        return [_remote(src[b], dst[b], send_sems, recv_sems, base + b, (x, y, 1 - c)) for b in range(len(bufs))]

    shapes = [jax.ShapeDtypeStruct(b.shape, b.dtype) for b in bufs]
    return _Ride(bufs, shapes, len(bufs), make)


def _exchange_call(ride, name):
    n_src, n_dst = len(ride.srcs), len(ride.dst_shapes)

    def body(*refs):
        copies = ride.make(refs[:n_src], refs[n_src:n_src + n_dst], refs[-2], refs[-1], 0)
        for cp in copies:
            cp.start()
        for cp in copies:
            cp.wait()

    return pl.pallas_call(
        body,
        name=name,
        in_specs=[ANY_SPEC] * n_src,
        out_specs=[ANY_SPEC] * n_dst,
        out_shape=ride.dst_shapes,
        scratch_shapes=[pltpu.SemaphoreType.DMA((ride.n_copies,)), pltpu.SemaphoreType.DMA((ride.n_copies,))],
    )(*ride.srcs)


def _hosted_call(body, ride, n_steps, *, name, in_specs, out_specs, out_shape, scratch_shapes, args, aliases=None):
    params = pltpu.CompilerParams(dimension_semantics=("arbitrary",), vmem_limit_bytes=VMEM_LIMIT)
    aliases = aliases or {}
    if ride is None:
        res = pl.pallas_call(body, name=name, grid=(n_steps,), in_specs=list(in_specs), out_specs=list(out_specs),
                             out_shape=list(out_shape), scratch_shapes=list(scratch_shapes),
                             input_output_aliases=aliases, compiler_params=params)(*args)
        return list(res), []
    sizes = [len(in_specs), len(ride.srcs), len(out_specs), len(ride.dst_shapes), len(scratch_shapes), 2]

    def hosted(*refs):
        groups, pos = [], 0
        for n in sizes:
            groups.append(refs[pos:pos + n])
            pos += n
        ins, rin, outs, rout, scr, (send_sems, recv_sems) = groups
        i = pl.program_id(0)

        @pl.when(i == 0)
        def _():
            for cp in ride.make(rin, rout, send_sems, recv_sems, 0):
                cp.start()

        body(*ins, *outs, *scr)

        @pl.when(i == n_steps - 1)
        def _():
            for cp in ride.make(rin, rout, send_sems, recv_sems, 0):
                cp.wait()

    n_out = len(out_specs)
    res = pl.pallas_call(
        hosted,
        name=name,
        grid=(n_steps,),
        in_specs=list(in_specs) + [ANY_SPEC] * len(ride.srcs),
        out_specs=list(out_specs) + [ANY_SPEC] * len(ride.dst_shapes),
        out_shape=list(out_shape) + ride.dst_shapes,
        scratch_shapes=list(scratch_shapes) + [pltpu.SemaphoreType.DMA((ride.n_copies,)),
                                               pltpu.SemaphoreType.DMA((ride.n_copies,))],
        input_output_aliases=aliases,
        compiler_params=params,
    )(*args, *ride.srcs)
    return list(res[:n_out]), list(res[n_out:])


def _gather_weights(w_in, small):
    r_in, c_in = w_in.shape
    h_in = r_in // 2
    q_in = h_in // 2

    def body(win_ref, small_ref, wg_ref, sg_ref, send_sems, recv_sems):
        x, y, c, chips = _position()
        j = 2 * x + y
        sibling = (x, y, 1 - c)
        xn, yn, dg = chips
        jx, jy, jd = (2 * cx + cy for cx, cy in chips)

        wg_ref[j] = win_ref[...].astype(BF16)
        sg_ref[j] = small_ref[...]

        def half(jj, cc):
            return wg_ref.at[jj, pl.ds(cc * h_in, h_in), :]

        def quarter(jj, qq):
            return wg_ref.at[jj, pl.ds(c * h_in + qq * q_in, q_in), :]

        def copy(k, ref, to):
            return _remote(ref, ref, send_sems, recv_sems, k, to)

        first = [copy(0, quarter(j, 0), (*xn, c)), copy(2, quarter(j, 1), (*yn, c)),
                 copy(1, quarter(j, 1), (*xn, c)), copy(3, quarter(j, 0), (*yn, c))]
        first += [copy(9 + k, sg_ref.at[j], (cx, cy, c)) for k, (cx, cy) in enumerate(chips)]
        for cp in first:
            cp.start()
        copy(0, quarter(jx, 0), sibling).wait_recv()
        along_y = copy(4, quarter(jx, 0), (*yn, c))
        along_y.start()
        copy(2, quarter(jy, 1), sibling).wait_recv()
        along_x = copy(5, quarter(jy, 1), (*xn, c))
        along_x.start()
        copy(1, quarter(jx, 1), sibling).wait_recv()
        to_sib = [copy(6, half(jx, c), sibling)]
        to_sib[-1].start()
        copy(3, quarter(jy, 0), sibling).wait_recv()
        to_sib.append(copy(7, half(jy, c), sibling))
        to_sib[-1].start()
        copy(4, quarter(jd, 0), sibling).wait_recv()
        copy(5, quarter(jd, 1), sibling).wait_recv()
        to_sib.append(copy(8, half(jd, c), sibling))
        to_sib[-1].start()
        for k, jk in enumerate((jx, jy, jd)):
            copy(6 + k, half(jk, 1 - c), sibling).wait_recv()
            copy(9 + k, sg_ref.at[jk], sibling).wait_recv()
        for cp in first + [along_y, along_x] + to_sib:
            cp.wait_send()

    return pl.pallas_call(
        body,
        name="gather_weights",
        out_shape=(jax.ShapeDtypeStruct((N_CHIPS, r_in, c_in), BF16),
                   jax.ShapeDtypeStruct((N_CHIPS,) + small.shape, F32)),
        in_specs=[VMEM_SPEC, VMEM_SPEC],
        out_specs=(VMEM_SPEC, VMEM_SPEC),
        scratch_shapes=[pltpu.SemaphoreType.DMA((12,)), pltpu.SemaphoreType.DMA((12,))],
        compiler_params=pltpu.CompilerParams(vmem_limit_bytes=VMEM_LIMIT),
    )(w_in, small)


def _proj_segments(d_lru, d_qk, d_ret, chunk_w):
    widths = [d_lru, d_lru, d_qk, d_qk, d_ret, d_ret]
    segs, col = [], 0
    for w in widths:
        parts, off = [], 0
        while off < w:
            jj, inner = divmod(col + off, chunk_w)
            take = min(w - off, chunk_w - inner)
            parts.append((jj, inner, off, take))
            off += take
        segs.append(parts)
        col += w
    return segs


def _in_proj(x2, meta, gain, wg, cos_t, sin_t, w_out, tm, d_lru, d_qk, d_ret):
    s_len, d = x2.shape
    tp = s_len + CHUNK
    nt, nb = tp // tm, tm // CHUNK
    segs = _proj_segments(d_lru, d_qk, d_ret, wg.shape[2])
    widths = [d_lru, d_lru, d_qk, d_qk, d_ret, d_ret]
    r_out, c_out = w_out.shape
    h_out = r_out // 2
    fwd_step = min(6, nt - 1)

    def gather_w_out(i, wout_ref, wo_ref, wob, send_sems, recv_sems, local_sem):
        x, y, c, chips = _position()
        j = 2 * x + y
        sibling = (x, y, 1 - c)

        def half(jj, cc):
            return wo_ref.at[jj, pl.ds(cc * h_out, h_out), :]

        local = pltpu.make_async_copy(wob, wo_ref.at[j], local_sem)
        first = [_remote(wob.at[pl.ds(c * h_out, h_out), :], half(j, c), send_sems, recv_sems, k, (cx, cy, c))
                 for k, (cx, cy) in enumerate(chips)]
        passed = [_remote(half(2 * cx + cy, c), half(2 * cx + cy, c), send_sems, recv_sems, 3 + k, sibling)
                  for k, (cx, cy) in enumerate(chips)]

        @pl.when(i == 0)
        def _():
            wob[...] = wout_ref[...].astype(BF16)
            local.start()
            for cp in first:
                cp.start()

        @pl.when(i == fwd_step)
        def _():
            for k, (cx, cy) in enumerate(chips):
                _remote(half(2 * cx + cy, c), half(2 * cx + cy, c), send_sems, recv_sems, k, sibling).wait_recv()
                passed[k].start()

        @pl.when(i == nt - 1)
        def _():
            for k, (cx, cy) in enumerate(chips):
                jk = 2 * cx + cy
                _remote(half(jk, 1 - c), half(jk, 1 - c), send_sems, recv_sems, 3 + k, sibling).wait_recv()
            for cp in first + passed:
                cp.wait_send()
            local.wait()

    def body(*refs):
        xb = refs[:nb]
        meta_ref, g_ref, w_ref, cos_ref, sin_ref, wout_ref = refs[nb:nb + 6]
        hp_ref = refs[nb + 6]
        outs = refs[nb + 7:nb + 13]
        wo_ref, wob, send_sems, recv_sems, local_sem = refs[nb + 13:]
        i = pl.program_id(0)
        gather_w_out(i, wout_ref, wo_ref, wob, send_sems, recv_sems, local_sem)
        blocks = [r[...] for r in xb]
        head = jnp.concatenate([jnp.zeros((PAD_ROWS, d), F32), meta_ref[...]], axis=0)
        blocks[0] = jnp.where(i == 0, head, blocks[0])
        h = jnp.concatenate(blocks, axis=0)
        hp_ref[...] = h
        rinv = lax.rsqrt(jnp.mean(h * h, axis=-1, keepdims=True) + EPS)
        u = ((h * rinv) * g_ref[...]).astype(BF16)
        for out_ref, parts in zip(outs, segs):
            for jj, inner, off, take in parts:
                out_ref[:, off:off + take] = _dot(u, w_ref[jj, :, inner:inner + take])
        cos = _tile_lanes(cos_ref[...], d_qk // LANES)
        sin = _tile_lanes(sin_ref[...], d_qk // LANES)
        q = outs[2][...]
        outs[2][...] = q * cos + _rot_partner(q) * sin
        k = outs[3][...]
        outs[3][...] = (k * cos + _rot_partner(k) * sin) * (QK_DIM ** -0.5)

    x_specs = [pl.BlockSpec((CHUNK, d), functools.partial(lambda i, b: (jnp.maximum(i * nb + b - 1, 0), 0), b=b))
               for b in range(nb)]
    tile = lambda w: pl.BlockSpec((tm, w), lambda i: (i, 0))
    return pl.pallas_call(
        body,
        name="in_proj",
        grid=(nt,),
        in_specs=x_specs + [pl.BlockSpec(meta.shape, lambda i: (0, 0)),
                            pl.BlockSpec(gain.shape, lambda i: (0, 0)),
                            pl.BlockSpec(wg.shape, lambda i: (0, 0, 0)),
                            tile(LANES), tile(LANES),
                            pl.BlockSpec(w_out.shape, lambda i: (0, 0))],
        out_specs=[tile(d)] + [tile(w) for w in widths] + [ANY_SPEC],
        out_shape=[jax.ShapeDtypeStruct((tp, d), F32)] + [jax.ShapeDtypeStruct((tp, w), F32) for w in widths]
                  + [jax.ShapeDtypeStruct((N_CHIPS, r_out, c_out), BF16)],
        scratch_shapes=[pltpu.VMEM((r_out, c_out), BF16), pltpu.SemaphoreType.DMA((6,)),
                        pltpu.SemaphoreType.DMA((6,)), pltpu.SemaphoreType.DMA],
        compiler_params=pltpu.CompilerParams(dimension_semantics=("arbitrary",), vmem_limit_bytes=VMEM_LIMIT),
    )(*([x2] * nb), meta, gain, wg, cos_t, sin_t, w_out)


def _to_groups(ref3, x):
    for g in range(ref3.shape[0]):
        ref3[g] = x[:, g * LANES:(g + 1) * LANES]


def _from_groups(ref3):
    return jnp.concatenate([ref3[g] for g in range(ref3.shape[0])], axis=1)


def _segment_scan(a3, u3, out3, p3, carry, tm, reverse):
    groups = a3.shape[0]
    seg = tm // SUBLANES

    def step(j, state):
        hs, ps = state
        rows = pl.ds((seg - 1 - j) if reverse else j, SUBLANES, stride=seg)
        new_h, new_p = [], []
        for g in range(groups):
            a = a3[g, rows, :]
            h = a * hs[g] + u3[g, rows, :]
            p = ps[g] * a
            out3[g, rows, :] = h
            p3[g, rows, :] = p
            new_h.append(h)
            new_p.append(p)
        return tuple(new_h), tuple(new_p)

    zeros = tuple(jnp.zeros((SUBLANES, LANES), F32) for _ in range(groups))
    ones = tuple(jnp.ones((SUBLANES, LANES), F32) for _ in range(groups))
    lax.fori_loop(0, seg, step, (zeros, ones))
    carries = [carry[:, g * LANES:(g + 1) * LANES] for g in range(groups)]
    for s in (reversed(range(SUBLANES)) if reverse else range(SUBLANES)):
        rows = slice(s * seg, (s + 1) * seg)
        edge = s * seg if reverse else (s + 1) * seg - 1
        for g in range(groups):
            out3[g, rows, :] = out3[g, rows, :] + p3[g, rows, :] * carries[g]
            carries[g] = out3[g, edge:edge + 1, :]
    return jnp.concatenate(carries, axis=1)


def _softplus_neg(lam):
    z = -lam
    e = jnp.exp(-jnp.abs(z))
    e1 = 1.0 + e
    log1p_e = jnp.where(e1 == 1.0, e, jnp.log(e1) * (e / (e1 - 1.0)))
    return jnp.maximum(z, 0.0) + log1p_e


def _lru_fwd(lx, lg, cw, cb, wr, br, wi, bi, lam, tm):
    tp, w = lx.shape
    nt = tp // tm
    per8 = tm // SUBLANES
    n_heads = wr.shape[0]

    def body(lx_ref, lxp_ref, lg_ref, cw_ref, cb_ref, wr_ref, br_ref, wi_ref, bi_ref, lam_ref,
             hl_ref, y_ref, xc_ref, r_ref, ig_ref, a_ref, beta_ref, w4_ref, a_s, u_s, h_s, p_s, carry):
        i = pl.program_id(0)

        @pl.when(i == 0)
        def _():
            carry[...] = jnp.zeros_like(carry)

        lxv = lx_ref[...]
        prev8 = jnp.where(i == 0, 0.0, lxp_ref[...])
        xc = cb_ref[...] + _shift_down(lxv, prev8, 3) * cw_ref[0:1, :]
        xc = xc + _shift_down(lxv, prev8, 2) * cw_ref[1:2, :]
        xc = xc + _shift_down(lxv, prev8, 1) * cw_ref[2:3, :]
        xc = xc + lxv * cw_ref[3:4, :]
        xc_ref[...] = xc
        pre_r, pre_i = [], []
        for hd in range(n_heads):
            xh = xc[:, hd * LANES:(hd + 1) * LANES].astype(BF16)
            pre_r.append(_dot(xh, wr_ref[hd].astype(BF16)))
            pre_i.append(_dot(xh, wi_ref[hd].astype(BF16)))
        r = _sigmoid(jnp.concatenate(pre_r, axis=1) + br_ref[...])
        ig = _sigmoid(jnp.concatenate(pre_i, axis=1) + bi_ref[...])
        r_ref[...] = r
        ig_ref[...] = ig
        log_a = (-LRU_C * r) * _softplus_neg(lam_ref[...])
        a = jnp.exp(log_a)
        a_ref[...] = a
        zz = -2.0 * log_a
        series = zz * (1.0 - zz * (0.5 - zz * (1.0 / 6.0)))
        a2 = a * a
        beta2 = jnp.maximum(jnp.where(zz < 0.015625, series, 1.0 - a2), 1e-37)
        rsb = lax.rsqrt(beta2)
        beta = beta2 * rsb
        beta_ref[...] = beta
        w4_ref[...] = a2 * rsb
        row = lax.broadcasted_iota(jnp.int32, (tm, 1), 0) + i * tm
        _to_groups(a_s, a)
        _to_groups(u_s, jnp.where(row >= PAD_ROWS, beta * ig * xc, 0.0))
        carry[0:1, :] = _segment_scan(a_s, u_s, h_s, p_s, carry[0:1, :], tm, reverse=False)
        hl = _from_groups(h_s)
        hl_ref[...] = hl
        g = lg_ref[...]
        y_ref[...] = (hl * (g * _sigmoid(g))).astype(BF16)

    tile = pl.BlockSpec((tm, w), lambda i: (i, 0))
    prev = pl.BlockSpec((SUBLANES, w), lambda i: (jnp.maximum(i * per8 - 1, 0), 0))
    vec = pl.BlockSpec((1, w), lambda i: (0, 0))
    mat = pl.BlockSpec(wr.shape, lambda i: (0, 0, 0))
    f32_out = jax.ShapeDtypeStruct((tp, w), F32)
    return pl.pallas_call(
        body,
        name="lru_fwd",
        grid=(nt,),
        in_specs=[tile, prev, tile, pl.BlockSpec(cw.shape, lambda i: (0, 0)), vec, mat, vec, mat, vec, vec],
        out_specs=[tile] * 8,
        out_shape=[f32_out, jax.ShapeDtypeStruct((tp, w), BF16)] + [f32_out] * 6,
        scratch_shapes=[pltpu.VMEM((w // LANES, tm, LANES), F32)] * 4 + [pltpu.VMEM((SUBLANES, w), F32)],
        compiler_params=pltpu.CompilerParams(dimension_semantics=("arbitrary",), vmem_limit_bytes=VMEM_LIMIT),
    )(lx, lx, lg, cw, cb, wr, br, wi, bi, lam)


def _ret_tables():
    log_g = jnp.log1p(-jnp.exp2(-5.0 - jnp.arange(HEADS, dtype=F32)))
    idx = jnp.arange(CHUNK, dtype=F32)
    diff = idx[:, None] - idx[None, :]
    dmask = jnp.where(diff[None] >= 0.0, jnp.exp(jnp.maximum(diff, 0.0)[None] * log_g[:, None, None]), 0.0)
    kdec = jnp.repeat(jnp.exp((CHUNK - 1.0 - idx)[:, None] * log_g[None, :]), QK_DIM, axis=1)
    qdec = jnp.repeat(jnp.exp((idx + 1.0)[:, None] * log_g[None, :]), QK_DIM, axis=1)
    g_chunk = jnp.exp(CHUNK * log_g)
    g_rows = jnp.repeat(g_chunk, QK_DIM).reshape(HEADS // 2, 2 * QK_DIM, 1)
    g_state = jnp.broadcast_to(g_rows, (HEADS // 2, 2 * QK_DIM, 2 * LANES))
    r_head = jnp.arange(2 * QK_DIM)[:, None] // QK_DIM
    c_head = jnp.arange(2 * LANES)[None, :] // LANES
    block_diag = (r_head == c_head).astype(F32)
    return dmask, qdec, kdec, g_state, block_diag


def _head_norm(o_h):
    mu = jnp.mean(o_h, axis=-1, keepdims=True)
    oc = o_h - mu
    var = jnp.mean(oc * oc, axis=-1, keepdims=True)
    rstd = lax.rsqrt(var + EPS)
    return oc * rstd, rstd


def _ret_fwd(q, k, v, rg, gain, tables, tm):
    tp, d_qk = q.shape
    d_ret = v.shape[1]
    n_ch = tp // CHUNK
    cps = tm // CHUNK
    n_pairs = HEADS // 2
    dmask, qdec, kdec, g_state, block_diag = tables

    def body(q_ref, k_ref, v_ref, rg_ref, gain_ref, dm_ref, qd_ref, kd_ref, gs_ref, bd_ref,
             o_ref, y_ref, rp_ref, state):
        n = pl.program_id(0)

        @pl.when(n == 0)
        def _():
            state[...] = jnp.zeros_like(state)

        lane = lax.broadcasted_iota(jnp.int32, (CHUNK, LANES), 1)
        for ci in range(cps):
            rs = slice(ci * CHUNK, (ci + 1) * CHUNK)
            for p in range(n_pairs):
                qs = slice(p * LANES, (p + 1) * LANES)
                vs = slice(p * 2 * LANES, (p + 1) * 2 * LANES)
                qp, kp = q_ref[rs, qs], k_ref[rs, qs]
                vb = v_ref[rs, vs].astype(BF16)
                kb = kp.astype(BF16)
                qd = (qp * qd_ref[:, qs]).astype(BF16)
                kd = (kp * kd_ref[:, qs]).astype(BF16)
                st = state[p]
                st_b = st.astype(BF16)
                rp_ref[ci, p] = st_b
                cross = _dot(qd, st_b)
                for e in range(2):
                    hd = 2 * p + e
                    hs = slice(hd * LANES, (hd + 1) * LANES)
                    es = slice(e * LANES, (e + 1) * LANES)
                    qm = jnp.where((lane // QK_DIM) == e, qp, 0.0).astype(BF16)
                    s = _dot_nt(qm, kb) * dm_ref[hd]
                    o_h = _dot(s.astype(BF16), vb[:, es]) + cross[:, es]
                    o_ref[rs, hs] = o_h
                    xhat, _ = _head_norm(o_h)
                    g = rg_ref[rs, hs]
                    y_ref[rs, hs] = ((xhat * gain_ref[:, hs]) * (g * _sigmoid(g))).astype(BF16)
                state[p] = gs_ref[p] * st + bd_ref[...] * _dot_tn(kd, vb)

    ch = lambda w: pl.BlockSpec((tm, w), lambda n: (n, 0))
    const2 = lambda a: pl.BlockSpec(a.shape, lambda n: (0, 0))
    const3 = lambda a: pl.BlockSpec(a.shape, lambda n: (0, 0, 0))
    return pl.pallas_call(
        body,
        name="ret_fwd",
        grid=(n_ch // cps,),
        in_specs=[ch(d_qk), ch(d_qk), ch(d_ret), ch(d_ret), const2(gain), const3(dmask), const2(qdec), const2(kdec),
                  const3(g_state), const2(block_diag)],
        out_specs=[ch(d_ret), ch(d_ret),
                   pl.BlockSpec((cps, n_pairs, 2 * QK_DIM, 2 * LANES), lambda n: (n, 0, 0, 0))],
        out_shape=[jax.ShapeDtypeStruct((tp, d_ret), F32), jax.ShapeDtypeStruct((tp, d_ret), BF16),
                   jax.ShapeDtypeStruct((n_ch, n_pairs, 2 * QK_DIM, 2 * LANES), BF16)],
        scratch_shapes=[pltpu.VMEM((n_pairs, 2 * QK_DIM, 2 * LANES), F32)],
        compiler_params=pltpu.CompilerParams(dimension_semantics=("arbitrary",), vmem_limit_bytes=VMEM_LIMIT),
    )(q, k, v, rg, gain, dmask, qdec, kdec, g_state, block_diag)


def _out_proj_loss(y_lru, y_ret, hp, tgt, wo, gain_f, tm):
    tp, d = hp.shape
    w_lru = y_lru.shape[1]
    w_mix = wo.shape[0]
    nt, nb = tp // tm, tm // CHUNK

    def body(*refs):
        yl_ref, yr_ref, hp_ref = refs[:3]
        tb = refs[3:3 + nb]
        wo_ref, gf_ref = refs[3 + nb:5 + nb]
        dh2_ref, dyl_ref, dyr_ref, dwo_ref, dgf_ref, loss_ref = refs[5 + nb:]
        i = pl.program_id(0)

        @pl.when(i == 0)
        def _():
            dwo_ref[...] = jnp.zeros_like(dwo_ref)
            dgf_ref[...] = jnp.zeros_like(dgf_ref)
            loss_ref[...] = jnp.zeros_like(loss_ref)

        yl, yr = yl_ref[...], yr_ref[...]
        h2 = hp_ref[...] + _dot(yl, wo_ref[0:w_lru, :]) + _dot(yr, wo_ref[w_lru:w_mix, :])
        rinv = lax.rsqrt(jnp.mean(h2 * h2, axis=-1, keepdims=True) + EPS)
        nrm = h2 * rinv
        gf = gf_ref[...]
        tgt_v = jnp.concatenate([r[...] for r in tb], axis=0)
        row = lax.broadcasted_iota(jnp.int32, (tm, 1), 0) + i * tm
        err = jnp.where(row >= CHUNK, nrm * gf - tgt_v, 0.0)
        loss_ref[...] += 0.5 * jnp.sum(jnp.mean(err * err, axis=-1, keepdims=True))
        dout = err * (1.0 / d)
        dgf_ref[...] += jnp.sum(dout * nrm, axis=0, keepdims=True)
        dn = dout * gf
        dh2 = rinv * (dn - nrm * jnp.mean(dn * nrm, axis=-1, keepdims=True))
        dh2_ref[...] = dh2
        dh2b = dh2.astype(BF16)
        dyl_ref[...] = _dot_nt(dh2b, wo_ref[0:w_lru, :])
        dyr_ref[...] = _dot_nt(dh2b, wo_ref[w_lru:w_mix, :])
        dwo_ref[0:w_lru, :] += _dot_tn(yl, dh2b)
        dwo_ref[w_lru:w_mix, :] += _dot_tn(yr, dh2b)

    tile = lambda w: pl.BlockSpec((tm, w), lambda i: (i, 0))
    t_specs = [pl.BlockSpec((CHUNK, d), functools.partial(lambda i, b: (jnp.maximum(i * nb + b - 1, 0), 0), b=b))
               for b in range(nb)]
    return pl.pallas_call(
        body,
        name="out_proj_loss",
        grid=(nt,),
        in_specs=[tile(w_lru), tile(w_mix - w_lru), tile(d)] + t_specs +
                 [pl.BlockSpec(wo.shape, lambda i: (0, 0)), pl.BlockSpec(gain_f.shape, lambda i: (0, 0))],
        out_specs=[tile(d), tile(w_lru), tile(w_mix - w_lru), pl.BlockSpec(wo.shape, lambda i: (0, 0)),
                   pl.BlockSpec((1, d), lambda i: (0, 0)), pl.BlockSpec((SUBLANES, LANES), lambda i: (0, 0))],
        out_shape=[jax.ShapeDtypeStruct((tp, d), F32), jax.ShapeDtypeStruct((tp, w_lru), F32),
                   jax.ShapeDtypeStruct((tp, w_mix - w_lru), F32), jax.ShapeDtypeStruct(wo.shape, F32),
                   jax.ShapeDtypeStruct((1, d), F32), jax.ShapeDtypeStruct((SUBLANES, LANES), F32)],
        compiler_params=pltpu.CompilerParams(dimension_semantics=("arbitrary",), vmem_limit_bytes=VMEM_LIMIT),
    )(y_lru, y_ret, hp, *([tgt] * nb), wo, gain_f)


def _dw_in_accumulate(acc, hp_ref, g1_ref, piece_refs):
    h = hp_ref[...]
    rinv = lax.rsqrt(jnp.mean(h * h, axis=-1, keepdims=True) + EPS)
    u = ((h * rinv) * g1_ref[...]).astype(BF16)
    off = 0
    for ref in piece_refs:
        w = ref.shape[1]
        acc[:, off:off + w] += _dot_tn(u, ref[...])
        off += w


def _dw_in_store(acc, dwg_ref, sem, col0, chunk_w):
    width, off = acc.shape[1], 0
    while off < width:
        jj, inner = divmod(col0 + off, chunk_w)
        take = min(width - off, chunk_w - inner)
        cp = pltpu.make_async_copy(acc.at[:, pl.ds(off, take)], dwg_ref.at[jj, :, pl.ds(inner, take)], sem)
        cp.start()
        cp.wait()
        off += take


def _ret_bwd(q, k, v, rg, o, rprev, dy, gain, cos_t, sin_t, tables, tm, hp, gain_in, dlg, dwg, ride=None):
    wg_shape = dwg.shape
    tp, d_qk = q.shape
    d_ret = v.shape[1]
    n_ch = tp // CHUNK
    cps = tm // CHUNK
    n_pairs = HEADS // 2
    dmask, qdec, kdec, g_state, block_diag = tables

    dmask_t = jnp.swapaxes(dmask, 1, 2)

    n_steps = n_ch // cps
    acc_w = dlg.shape[1] + 2 * d_qk + 2 * d_ret
    col0 = wg_shape[0] * wg_shape[2] - acc_w

    def body(q_ref, k_ref, v_ref, rg_ref, o_ref, rp_ref, dy_ref, gain_ref, cos_ref, sin_ref,
             dm_ref, dmt_ref, qd_ref, kd_ref, gs_ref, bd_ref, hp_ref, g1_ref, dlg_ref, dwg_in_ref,
             dq_ref, dk_ref, dv_ref, drg_ref, dgain_ref, dwg_ref, dstate, acc, sem):
        n = pl.program_id(0)

        @pl.when(n == 0)
        def _():
            dstate[...] = jnp.zeros_like(dstate)
            dgain_ref[...] = jnp.zeros_like(dgain_ref)
            acc[...] = jnp.zeros_like(acc)

        lane = lax.broadcasted_iota(jnp.int32, (CHUNK, LANES), 1)
        for ci in reversed(range(cps)):
            rs = slice(ci * CHUNK, (ci + 1) * CHUNK)
            dq_parts, dk_parts = [], []
            for p in range(n_pairs):
                qs = slice(p * LANES, (p + 1) * LANES)
                vs = slice(p * 2 * LANES, (p + 1) * 2 * LANES)
                do_parts = []
                for e in range(2):
                    hd = 2 * p + e
                    hs = slice(hd * LANES, (hd + 1) * LANES)
                    xhat, rstd = _head_norm(o_ref[rs, hs])
                    g = rg_ref[rs, hs]
                    sg = _sigmoid(g)
                    dyh = dy_ref[rs, hs]
                    gn = gain_ref[:, hs]
                    d_on = dyh * (g * sg)
                    drg_ref[rs, hs] = (dyh * (xhat * gn) * (sg * (1.0 + g * (1.0 - sg)))).astype(BF16)
                    dgain_ref[:, hs] += jnp.sum(d_on * xhat, axis=0, keepdims=True)
                    dxh = d_on * gn
                    do_parts.append(rstd * (dxh - jnp.mean(dxh, axis=-1, keepdims=True)
                                            - xhat * jnp.mean(dxh * xhat, axis=-1, keepdims=True)))
                do_b = jnp.concatenate(do_parts, axis=1).astype(BF16)
                qp, kp = q_ref[rs, qs], k_ref[rs, qs]
                vb = v_ref[rs, vs].astype(BF16)
                kb = kp.astype(BF16)
                qd = (qp * qd_ref[:, qs]).astype(BF16)
                kd = (kp * kd_ref[:, qs]).astype(BF16)
                dst = dstate[p]
                dst_b = dst.astype(BF16)
                dqp = _dot_nt(do_b, rp_ref[ci, p]) * qd_ref[:, qs]
                dkp = _dot_nt(vb, dst_b) * kd_ref[:, qs]
                dvp = _dot(kd, dst_b)
                dv_parts = []
                for e in range(2):
                    hd = 2 * p + e
                    es = slice(e * LANES, (e + 1) * LANES)
                    mine = (lane // QK_DIM) == e
                    qm = jnp.where(mine, qp, 0.0).astype(BF16)
                    km = jnp.where(mine, kp, 0.0).astype(BF16)
                    ds = (_dot_nt(do_b[:, es], vb[:, es]) * dm_ref[hd]).astype(BF16)
                    s_t = (_dot_nt(kb, qm) * dmt_ref[hd]).astype(BF16)
                    ds_t = (_dot_nt(vb[:, es], do_b[:, es]) * dmt_ref[hd]).astype(BF16)
                    dv_parts.append(dvp[:, es] + _dot(s_t, do_b[:, es]))
                    dqp = dqp + _dot(ds, km)
                    dkp = dkp + _dot(ds_t, qm)
                dv_ref[rs, vs] = jnp.concatenate(dv_parts, axis=1).astype(BF16)
                dstate[p] = gs_ref[p] * dst + bd_ref[...] * _dot_tn(qd, do_b)
                dq_parts.append(dqp)
                dk_parts.append(dkp)
            cos = _tile_lanes(cos_ref[rs, :], d_qk // LANES)
            sin = _tile_lanes(sin_ref[rs, :], d_qk // LANES)
            dq = jnp.concatenate(dq_parts, axis=1)
            dk = jnp.concatenate(dk_parts, axis=1) * (QK_DIM ** -0.5)
            dq_ref[rs, :] = (dq * cos + _rot_partner(dq * sin)).astype(BF16)
            dk_ref[rs, :] = (dk * cos + _rot_partner(dk * sin)).astype(BF16)

        _dw_in_accumulate(acc, hp_ref, g1_ref, [dlg_ref, dq_ref, dk_ref, dv_ref, drg_ref])

        @pl.when(n == n_steps - 1)
        def _():
            _dw_in_store(acc, dwg_ref, sem, col0, wg_shape[2])

    last = n_ch // cps - 1
    ch = lambda w: pl.BlockSpec((tm, w), lambda n: (last - n, 0))
    const2 = lambda a: pl.BlockSpec(a.shape, lambda n: (0, 0))
    const3 = lambda a: pl.BlockSpec(a.shape, lambda n: (0, 0, 0))
    in_specs = [ch(d_qk), ch(d_qk), ch(d_ret), ch(d_ret), ch(d_ret),
                pl.BlockSpec((cps, n_pairs, 2 * QK_DIM, 2 * LANES), lambda n: (last - n, 0, 0, 0)),
                ch(d_ret), const2(gain), ch(LANES), ch(LANES),
                const3(dmask), const3(dmask_t), const2(qdec), const2(kdec), const3(g_state), const2(block_diag),
                ch(hp.shape[1]), const2(gain_in), ch(dlg.shape[1]), ANY_SPEC]
    return _hosted_call(
        body, ride, n_ch // cps,
        name="ret_bwd",
        in_specs=in_specs,
        out_specs=[ch(d_qk), ch(d_qk), ch(d_ret), ch(d_ret), pl.BlockSpec((1, d_ret), lambda n: (0, 0)), ANY_SPEC],
        out_shape=[jax.ShapeDtypeStruct((tp, d_qk), BF16), jax.ShapeDtypeStruct((tp, d_qk), BF16),
                   jax.ShapeDtypeStruct((tp, d_ret), BF16), jax.ShapeDtypeStruct((tp, d_ret), BF16),
                   jax.ShapeDtypeStruct((1, d_ret), F32), jax.ShapeDtypeStruct(wg_shape, F32)],
        scratch_shapes=[pltpu.VMEM((n_pairs, 2 * QK_DIM, 2 * LANES), F32),
                        pltpu.VMEM((wg_shape[1], acc_w), F32), pltpu.SemaphoreType.DMA],
        args=(q, k, v, rg, o, rprev, dy, gain, cos_t, sin_t, dmask, dmask_t, qdec, kdec, g_state, block_diag,
              hp, gain_in, dlg, dwg),
        aliases={len(in_specs) - 1: 5},
    )


def _lru_bwd(lx, lg, hl, dy, saved, cw, wr, wi, lam, tm, hp, gain_in, wg_shape, ride=None):
    tp, w = lx.shape
    nt = tp // tm
    per8 = tm // SUBLANES
    n_heads = wr.shape[0]

    def body(lx_ref, lg_ref, hl_ref, hlp_ref, dy_ref, xc_ref, r_ref, ig_ref, a_ref, beta_ref, w4_ref,
             cw_ref, wr_ref, wi_ref, lam_ref, hp_ref, g1_ref,
             dlx_ref, dlg_ref, dcw_ref, dcb_ref, dwr_ref, dbr_ref, dwi_ref, dbi_ref, dlam_ref, dwg_ref,
             g_s, b_s, carry, dxc_next, a_next, acc, sem):
        i = pl.program_id(0)
        first_tile = i == nt - 1

        @pl.when(i == 0)
        def _():
            carry[...] = jnp.zeros_like(carry)
            dxc_next[...] = jnp.zeros_like(dxc_next)
            a_next[...] = jnp.zeros_like(a_next)
            acc[...] = jnp.zeros_like(acc)
            for r in (dcw_ref, dcb_ref, dwr_ref, dbr_ref, dwi_ref, dbi_ref, dlam_ref):
                r[...] = jnp.zeros_like(r)

        lxv = lx_ref[...]
        a, beta, r, ig, xc = a_ref[...], beta_ref[...], r_ref[...], ig_ref[...], xc_ref[...]
        g = lg_ref[...]
        sg = _sigmoid(g)
        dyv = dy_ref[...]
        hlv = hl_ref[...]
        dlg_ref[...] = (dyv * hlv * (sg * (1.0 + g * (1.0 - sg)))).astype(BF16)
        _to_groups(g_s, dyv * (g * sg))
        _to_groups(b_s, _shift_up(a, a_next[...], 1))
        carry[0:1, :] = _segment_scan(b_s, g_s, g_s, b_s, carry[0:1, :], tm, reverse=True)
        a_next[...] = a[0:SUBLANES]
        dh = _from_groups(g_s)
        hprev = _shift_down(hlv, jnp.where(first_tile, 0.0, hlp_ref[...]), 1)
        row = lax.broadcasted_iota(jnp.int32, (tm, 1), 0) + (nt - 1 - i) * tm
        du = jnp.where(row >= PAD_ROWS, dh, 0.0)
        da = dh * hprev
        dbeta = du * ig * xc
        d_ig = du * beta * xc
        dxc = du * beta * ig
        dloga = da * a - dbeta * w4_ref[...]
        lam_v = lam_ref[...]
        dlam_ref[...] += jnp.sum(dloga * r, axis=0, keepdims=True) * (LRU_C * _sigmoid(-lam_v))
        dpr = (dloga * (-LRU_C * _softplus_neg(lam_v))) * r * (1.0 - r)
        dpi = d_ig * ig * (1.0 - ig)
        dbr_ref[...] += jnp.sum(dpr, axis=0, keepdims=True)
        dbi_ref[...] += jnp.sum(dpi, axis=0, keepdims=True)
        dxc_parts = []
        for hd in range(n_heads):
            hs = slice(hd * LANES, (hd + 1) * LANES)
            xh = xc[:, hs].astype(BF16)
            dprh = dpr[:, hs].astype(BF16)
            dpih = dpi[:, hs].astype(BF16)
            dwr_ref[hd] += _dot_tn(xh, dprh)
            dwi_ref[hd] += _dot_tn(xh, dpih)
            dxc_parts.append(_dot_nt(dprh, wr_ref[hd].astype(BF16)) + _dot_nt(dpih, wi_ref[hd].astype(BF16)))
        dxc = dxc + jnp.concatenate(dxc_parts, axis=1)
        nxt = dxc_next[...]
        up1, up2, up3 = _shift_up(dxc, nxt, 1), _shift_up(dxc, nxt, 2), _shift_up(dxc, nxt, 3)
        dlx = dxc * cw_ref[3:4, :]
        dlx = dlx + up1 * cw_ref[2:3, :]
        dlx = dlx + up2 * cw_ref[1:2, :]
        dlx = dlx + up3 * cw_ref[0:1, :]
        dlx_ref[...] = dlx.astype(BF16)
        dxc_next[...] = dxc[0:SUBLANES]
        dcb_ref[...] += jnp.sum(dxc, axis=0, keepdims=True)
        dcw_ref[0:1, :] += jnp.sum(up3 * lxv, axis=0, keepdims=True)
        dcw_ref[1:2, :] += jnp.sum(up2 * lxv, axis=0, keepdims=True)
        dcw_ref[2:3, :] += jnp.sum(up1 * lxv, axis=0, keepdims=True)
        dcw_ref[3:4, :] += jnp.sum(dxc * lxv, axis=0, keepdims=True)

        _dw_in_accumulate(acc, hp_ref, g1_ref, [dlx_ref])

        @pl.when(i == nt - 1)
        def _():
            _dw_in_store(acc, dwg_ref, sem, 0, wg_shape[2])

    last = nt - 1
    tile = pl.BlockSpec((tm, w), lambda i: (last - i, 0))
    prev = pl.BlockSpec((SUBLANES, w), lambda i: (jnp.maximum((last - i) * per8 - 1, 0), 0))
    vec = pl.BlockSpec((1, w), lambda i: (0, 0))
    mat = pl.BlockSpec(wr.shape, lambda i: (0, 0, 0))
    cwb = pl.BlockSpec(cw.shape, lambda i: (0, 0))
    in_specs = ([tile, tile, tile, prev, tile] + [tile] * 6 + [cwb, mat, mat, vec]
                + [pl.BlockSpec((tm, hp.shape[1]), lambda i: (last - i, 0)),
                   pl.BlockSpec(gain_in.shape, lambda i: (0, 0))])
    return _hosted_call(
        body, ride, nt,
        name="lru_bwd",
        in_specs=in_specs,
        out_specs=[tile, tile, cwb, vec, mat, vec, mat, vec, vec, ANY_SPEC],
        out_shape=[jax.ShapeDtypeStruct((tp, w), BF16), jax.ShapeDtypeStruct((tp, w), BF16),
                   jax.ShapeDtypeStruct(cw.shape, F32), jax.ShapeDtypeStruct((1, w), F32),
                   jax.ShapeDtypeStruct(wr.shape, F32), jax.ShapeDtypeStruct((1, w), F32),
                   jax.ShapeDtypeStruct(wr.shape, F32), jax.ShapeDtypeStruct((1, w), F32),
                   jax.ShapeDtypeStruct((1, w), F32), jax.ShapeDtypeStruct(wg_shape, F32)],
        scratch_shapes=[pltpu.VMEM((w // LANES, tm, LANES), F32)] * 2 + [pltpu.VMEM((SUBLANES, w), F32)] * 3
                       + [pltpu.VMEM((wg_shape[1], w), F32), pltpu.SemaphoreType.DMA],
        args=(lx, lg, hl, hl, dy, *saved, cw, wr, wi, lam, hp, gain_in),
    )


def _in_proj_dx(dparts, hp, dh2, gain, wg, s_len, tm, ride=None):
    tp, d = hp.shape
    nt = tp // tm
    widths = [p.shape[1] for p in dparts]
    segs = _proj_segments(widths[0], widths[2], widths[4], wg.shape[2])

    def body(*refs):
        dp = refs[:6]
        hp_ref, dh2_ref, g_ref, w_ref = refs[6:10]
        gx_ref, dmeta_ref, dg_ref = refs[10:13]
        stage, sems = refs[13:]
        i = pl.program_id(0)

        @pl.when(i == 0)
        def _():
            dg_ref[...] = jnp.zeros_like(dg_ref)

        h = hp_ref[...]
        rinv = lax.rsqrt(jnp.mean(h * h, axis=-1, keepdims=True) + EPS)
        nrm = h * rinv
        gv = g_ref[...]
        du = jnp.zeros((tm, d), F32)
        for p_ref, parts in zip(dp, segs):
            for jj, inner, off, take in parts:
                du = du + _dot_nt(p_ref[:, off:off + take], w_ref[jj, :, inner:inner + take])
        dg_ref[...] += jnp.sum(du * nrm, axis=0, keepdims=True)
        dn = du * gv
        dh = dh2_ref[...] + rinv * (dn - nrm * jnp.mean(dn * nrm, axis=-1, keepdims=True))

        def first_copy():
            return pltpu.make_async_copy(stage.at[0, pl.ds(CHUNK, tm - CHUNK), :],
                                         gx_ref.at[pl.ds(0, tm - CHUNK), :], sems.at[0])

        def tile_copy(slot, start):
            return pltpu.make_async_copy(stage.at[slot], gx_ref.at[pl.ds(start, tm), :], sems.at[slot])

        @pl.when(i == 0)
        def _():
            dmeta_ref[...] = dh[PAD_ROWS:CHUNK]
            stage[0] = dh
            first_copy().start()

        @pl.when(i > 0)
        def _():
            slot = 1 + i % 2

            @pl.when(i >= 3)
            def _():
                tile_copy(slot, 0).wait()

            stage[slot] = dh
            tile_copy(slot, pl.multiple_of(i * tm - CHUNK, CHUNK)).start()

        @pl.when(i == nt - 1)
        def _():
            first_copy().wait()
            for step in (nt - 2, nt - 1):
                if step >= 1:
                    tile_copy(1 + step % 2, 0).wait()

    tile = lambda w: pl.BlockSpec((tm, w), lambda i: (i, 0))
    return _hosted_call(
        body, ride, nt,
        name="in_proj_dx",
        in_specs=[tile(w) for w in widths] + [tile(d), tile(d), pl.BlockSpec(gain.shape, lambda i: (0, 0)),
                                              pl.BlockSpec(wg.shape, lambda i: (0, 0, 0))],
        out_specs=[ANY_SPEC, pl.BlockSpec((N_META, d), lambda i: (0, 0)), pl.BlockSpec((1, d), lambda i: (0, 0))],
        out_shape=[jax.ShapeDtypeStruct((s_len, d), F32), jax.ShapeDtypeStruct((N_META, d), F32),
                   jax.ShapeDtypeStruct((1, d), F32)],
        scratch_shapes=[pltpu.VMEM((3, tm, d), F32), pltpu.SemaphoreType.DMA((3,))],
        args=(*dparts, hp, dh2, gain, wg),
    )


def _pair_sum(buf, recv, c_arr, tr, name):
    _, rows, cols = buf.shape

    def body(c_ref, mine_ref, got_ref, out_ref):
        out_ref[...] = (mine_ref[...] + got_ref[...]).astype(BF16)

    grid_spec = pltpu.PrefetchScalarGridSpec(
        num_scalar_prefetch=1,
        grid=(N_CHIPS, rows // tr),
        in_specs=[pl.BlockSpec((1, tr, cols), lambda jj, r, c_ref: (2 * jj + c_ref[0], r, 0)),
                  pl.BlockSpec((1, tr, cols), lambda jj, r, c_ref: (jj, r, 0))],
        out_specs=pl.BlockSpec((1, tr, cols), lambda jj, r, c_ref: (jj, r, 0)),
    )
    return pl.pallas_call(
        body,
        name=name,
        grid_spec=grid_spec,
        out_shape=jax.ShapeDtypeStruct((N_CHIPS, rows, cols), BF16),
    )(c_arr, buf, recv)


def _pair_exchange_sum(buf, c_arr, tr, name):
    _, rows, cols = buf.shape
    per = rows // tr

    def body(c_ref, src_ref, mine_ref, out_ref, got, send_sems, recv_sems):
        jj, r = pl.program_id(0), pl.program_id(1)
        x, y, c, _ = _position()
        copies = [_remote(src_ref.at[2 * k + 1 - c], got.at[k], send_sems, recv_sems, k, (x, y, 1 - c))
                  for k in range(N_CHIPS)]

        @pl.when((jj == 0) & (r == 0))
        def _():
            for cp in copies:
                cp.start()

        for k in range(N_CHIPS):
            @pl.when((jj == k) & (r == 0))
            def _():
                copies[k].wait_recv()

        rows_r = pl.ds(pl.multiple_of(r * tr, tr), tr)
        out_ref[0] = (mine_ref[0] + got[jj, rows_r, :]).astype(BF16)

        @pl.when((jj == N_CHIPS - 1) & (r == per - 1))
        def _():
            for cp in copies:
                cp.wait_send()

    grid_spec = pltpu.PrefetchScalarGridSpec(
        num_scalar_prefetch=1,
        grid=(N_CHIPS, per),
        in_specs=[ANY_SPEC, pl.BlockSpec((1, tr, cols), lambda jj, r, c_ref: (2 * jj + c_ref[0], r, 0))],
        out_specs=pl.BlockSpec((1, tr, cols), lambda jj, r, c_ref: (jj, r, 0)),
        scratch_shapes=[pltpu.VMEM((N_CHIPS, rows, cols), F32), pltpu.SemaphoreType.DMA((N_CHIPS,)),
                        pltpu.SemaphoreType.DMA((N_CHIPS,))],
    )
    return pl.pallas_call(
        body,
        name=name,
        grid_spec=grid_spec,
        out_shape=jax.ShapeDtypeStruct((N_CHIPS, rows, cols), BF16),
        compiler_params=pltpu.CompilerParams(dimension_semantics=("arbitrary", "arbitrary"),
                                             vmem_limit_bytes=VMEM_LIMIT),
    )(c_arr, buf, buf)


def _chip_sum(mine, got, j_arr, tr, name, loss_part=None):
    _, rows, cols = got.shape
    extra = [] if loss_part is None else [loss_part]

    def body(j_ref, mine_ref, got_ref, *rest):
        out_ref = rest[-1]
        j = j_ref[0]
        acc = None
        for jj in range(N_CHIPS):
            term = jnp.where(j == jj, mine_ref[0], got_ref[jj]).astype(F32)
            acc = term if acc is None else acc + term
        out_ref[...] = acc
        if loss_part is not None:
            out_ref[ROW_LOSS:ROW_LOSS + 1, :] = rest[0][0:1, :]

    grid_spec = pltpu.PrefetchScalarGridSpec(
        num_scalar_prefetch=1,
        grid=(rows // tr,),
        in_specs=[pl.BlockSpec((1, tr, cols), lambda r, j_ref: (j_ref[0], r, 0)),
                  pl.BlockSpec((N_CHIPS, tr, cols), lambda r, j_ref: (0, r, 0))] +
                 [pl.BlockSpec(e.shape, lambda r, j_ref: (0, 0)) for e in extra],
        out_specs=pl.BlockSpec((tr, cols), lambda r, j_ref: (r, 0)),
    )
    return pl.pallas_call(
        body,
        name=name,
        grid_spec=grid_spec,
        out_shape=jax.ShapeDtypeStruct((rows, cols), F32),
    )(j_arr, mine, got, *extra)


def _finish_exchange(f_in, f_small):
    def body(fin_ref, fs_ref, rin_ref, os_ref, send_sems, recv_sems, local_sem):
        x, y, c, chips = _position()
        j = 2 * x + y
        me = 2 * j + c
        sibling = (x, y, 1 - c)
        local = pltpu.make_async_copy(fs_ref, os_ref.at[me], local_sem)
        local.start()

        def copy(k, src, dst, to):
            return _remote(src, dst, send_sems, recv_sems, k, to)

        first = [copy(0, fin_ref, rin_ref, sibling), copy(1, fs_ref, os_ref.at[me], sibling)]
        first += [copy(2 + k, fs_ref, os_ref.at[me], (cx, cy, c)) for k, (cx, cy) in enumerate(chips)]
        for cp in first:
            cp.start()
        passed = []
        for k, (cx, cy) in enumerate(chips):
            unit = 2 * (2 * cx + cy) + c
            copy(2 + k, fs_ref, os_ref.at[unit], sibling).wait_recv()
            fwd = copy(5 + k, os_ref.at[unit], os_ref.at[unit], sibling)
            fwd.start()
            passed.append(fwd)
        copy(0, fin_ref, rin_ref, sibling).wait_recv()
        copy(1, fs_ref, os_ref.at[2 * j + 1 - c], sibling).wait_recv()
        for k, (cx, cy) in enumerate(chips):
            unit = 2 * (2 * cx + cy) + 1 - c
            copy(5 + k, fs_ref, os_ref.at[unit], sibling).wait_recv()
        for cp in first + passed:
            cp.wait_send()
        local.wait()

    return pl.pallas_call(
        body,
        name="grad_finish_exchange",
        in_specs=[ANY_SPEC] * 2,
        out_specs=[ANY_SPEC] * 2,
        out_shape=[jax.ShapeDtypeStruct(f_in.shape, F32), jax.ShapeDtypeStruct((N_DEV,) + f_small.shape, F32)],
        scratch_shapes=[pltpu.SemaphoreType.DMA((8,)), pltpu.SemaphoreType.DMA((8,)), pltpu.SemaphoreType.DMA],
    )(f_in, f_small)


def _adamw_math(w, g, m, v):
    m = ADAM_B1 * m + (1.0 - ADAM_B1) * g
    v = ADAM_B2 * v + (1.0 - ADAM_B2) * (g * g)
    m_hat = m / (1.0 - ADAM_B1 ** ADAM_STEP)
    v_hat = v / (1.0 - ADAM_B2 ** ADAM_STEP)
    delta = -ADAM_LR * (m_hat / (jnp.sqrt(v_hat) + ADAM_EPS) + ADAM_WD * w)
    return delta, m, v


def _adamw_big(w, g_mine, g_sib, m, v, c_arr, tr, name):
    rows, cols = w.shape
    half = rows // 2
    per = half // tr

    def body(c_ref, w_ref, gm_ref, gs_ref, m_ref, v_ref, g_ref, d_ref, mo_ref, vo_ref):
        g = jnp.where(pl.program_id(0) == c_ref[0], gm_ref[...], gs_ref[...])
        g_ref[...] = g
        d_ref[...], mo_ref[...], vo_ref[...] = _adamw_math(w_ref[...], g, m_ref[...], v_ref[...])

    full = pl.BlockSpec((tr, cols), lambda h, r, c_ref: (h * per + r, 0))
    unit = pl.BlockSpec((tr, cols), lambda h, r, c_ref: (r, 0))
    grid_spec = pltpu.PrefetchScalarGridSpec(
        num_scalar_prefetch=1,
        grid=(2, per),
        in_specs=[full, unit, unit, full, full],
        out_specs=[full] * 4,
    )
    return pl.pallas_call(
        body,
        name=name,
        grid_spec=grid_spec,
        out_shape=[jax.ShapeDtypeStruct(w.shape, F32)] * 4,
    )(c_arr, w, g_mine, g_sib, m, v)


def _adamw_small(j_arr, packed, params):
    names = list(params)
    n = len(names)
    vec_names = ["norm_gain", "conv_b", "b_rg", "b_ig", "lru_lambda", "ret_norm_gain", "final_norm_gain"]

    def body(j_ref, pk_ref, *refs):
        ins = refs[:3 * n]
        outs = refs[3 * n:]
        j = j_ref[0]

        def shard(row, rows):
            return jnp.concatenate([pk_ref[2 * j, row:row + rows, :], pk_ref[2 * j + 1, row:row + rows, :]], axis=1)

        def tail_sum(unit, row, rows):
            start = pl.multiple_of(UNIT_ROWS + TAIL_ROWS * unit + row, SUBLANES)
            total = pk_ref[0, pl.ds(start, rows), :]
            for dev in range(1, N_DEV):
                total = total + pk_ref[dev, pl.ds(start, rows), :]
            return total

        for idx, name in enumerate(names):
            if name == "w_rg":
                g = pk_ref[:, ROW_WR:ROW_WR + LANES, :]
            elif name == "w_ig":
                g = pk_ref[:, ROW_WI:ROW_WI + LANES, :]
            elif name == "meta_tokens":
                g = jnp.concatenate([tail_sum(2 * j, 0, N_META), tail_sum(2 * j + 1, 0, N_META)], axis=1)
            elif name == "norm_gain":
                g = jnp.concatenate([tail_sum(u, N_META, SUBLANES)[0:1] for u in range(N_DEV)], axis=1)
            elif name == "conv_w":
                g = shard(ROW_CONV, 4)
            else:
                row = ROW_VEC + vec_names.index(name)
                g = jnp.concatenate([pk_ref[u, row:row + 1, :] for u in range(N_DEV)], axis=1)
            w_ref, m_ref, v_ref = ins[3 * idx:3 * idx + 3]
            delta, m, v = _adamw_math(w_ref[...], g, m_ref[...], v_ref[...])
            g_ref, d_ref, mo_ref, vo_ref = outs[4 * idx:4 * idx + 4]
            g_ref[...], d_ref[...], mo_ref[...], vo_ref[...] = g, delta, m, v
        total = pk_ref[0, ROW_LOSS:ROW_LOSS + 1, :]
        for u in range(1, N_DEV):
            total = total + pk_ref[u, ROW_LOSS:ROW_LOSS + 1, :]
        outs[4 * n][...] = jnp.broadcast_to(total, (SUBLANES, LANES))

    flat_in, out_shape = [], []
    for name in names:
        w, m, v = params[name]
        flat_in += [w, m, v]
        out_shape += [jax.ShapeDtypeStruct(w.shape, F32)] * 4
    out_shape.append(jax.ShapeDtypeStruct((SUBLANES, LANES), F32))
    res = pl.pallas_call(
        body,
        name="adamw_small",
        in_specs=[SMEM_SPEC, VMEM_SPEC] + [VMEM_SPEC] * (3 * n),
        out_specs=[VMEM_SPEC] * (4 * n + 1),
        out_shape=out_shape,
    )(j_arr, packed, *flat_in)
    return {name: tuple(res[4 * idx:4 * idx + 4]) for idx, name in enumerate(names)}, res[4 * n][0, 0]


def _units(a):
    rows = a.shape[0]
    return jnp.transpose(a.reshape(rows, N_DEV, LANES), (1, 0, 2))


def kernel(x, meta_tokens, norm_gain, w_in, conv_w, conv_b, w_rg, b_rg, w_ig, b_ig, lru_lambda, ret_norm_gain, w_out, final_norm_gain, loss_target, m_meta_tokens, m_norm_gain, m_w_in, m_conv_w, m_conv_b, m_w_rg, m_b_rg, m_w_ig, m_b_ig, m_lru_lambda, m_ret_norm_gain, m_w_out, m_final_norm_gain, v_meta_tokens, v_norm_gain, v_w_in, v_conv_w, v_conv_b, v_w_rg, v_b_rg, v_w_ig, v_b_ig, v_lru_lambda, v_ret_norm_gain, v_w_out, v_final_norm_gain):
    s_len, d = x.shape[1], x.shape[2]
    d_lru = w_rg.shape[1] * w_rg.shape[2]
    d_ret = ret_norm_gain.shape[1]
    d_qk = HEADS * QK_DIM
    tp = s_len + CHUNK
    tm = TOKEN_TILE
    assert tp % tm == 0 and d_lru == HEADS * LANES and d_ret == HEADS * LANES
    ax, ay, ac = lax.axis_index("x"), lax.axis_index("y"), lax.axis_index("c")
    c_arr = jnp.reshape(ac, (1,)).astype(jnp.int32)
    j_arr = jnp.reshape(2 * ax + ay, (1,)).astype(jnp.int32)

    small = jnp.concatenate([meta_tokens, conv_w[0], jnp.zeros((4, meta_tokens.shape[1]), F32)], axis=0)
    wg, sg = _gather_weights(w_in[0], small)
    cols = sg.shape[2]
    meta_full = jnp.transpose(sg[:, :N_META, :], (1, 0, 2)).reshape(N_META, N_CHIPS * cols)
    cw_full = jnp.transpose(sg[:, N_META:N_META + 4, :], (1, 0, 2)).reshape(4, N_CHIPS * cols)
    cw8 = jnp.concatenate([cw_full, jnp.zeros((4, cw_full.shape[1]), F32)], axis=0)

    half = QK_DIM // 2
    inv = ROPE_BASE ** (-jnp.arange(half, dtype=F32) / half)
    pos = (jnp.arange(tp) - PAD_ROWS).astype(F32)
    ang = pos[:, None] * inv[None, :]
    cos_t = jnp.tile(jnp.cos(ang), (1, LANES // half))
    sign = jnp.where((jnp.arange(LANES) % QK_DIM) < half, -1.0, 1.0).astype(F32)
    sin_t = jnp.tile(jnp.sin(ang), (1, LANES // half)) * sign[None, :]
    tables = _ret_tables()
    gain_f = final_norm_gain.reshape(1, d)

    hp, lx, lg, q, k, v, rg, wo4 = _in_proj(x[0], meta_full, norm_gain, wg, cos_t, sin_t, w_out[0], tm,
                                            d_lru, d_qk, d_ret)
    wo = wo4.reshape(N_CHIPS * wo4.shape[1], wo4.shape[2])
    hl, y_lru, *lru_saved = _lru_fwd(lx, lg, cw8, conv_b, w_rg[0], b_rg, w_ig[0], b_ig, lru_lambda, tm)
    o, y_ret, rprev = _ret_fwd(q, k, v, rg, ret_norm_gain, tables, tm)
    dh2, dy_lru, dy_ret, dwo, dgf, loss_acc = _out_proj_loss(y_lru, y_ret, hp, loss_target[0], wo, gain_f, tm)

    g_out = dwo.reshape(N_DEV, dwo.shape[0] // N_DEV, dwo.shape[1])
    (dlx, dlg, dcw, dcb, dwr, dbr, dwi, dbi, dlam, dwg_lru), (r_out,) = _lru_bwd(
        lx, lg, hl, dy_lru, lru_saved, cw8, w_rg[0], w_ig[0], lru_lambda, tm, hp, norm_gain, wg.shape,
        ride=_pair_ride([g_out]))
    q_out = _pair_sum(g_out, r_out, c_arr, 128, "grad_pair_sum_out")
    (dq, dk, dv, drg, dgain, dwg), (e_out,) = _ret_bwd(
        q, k, v, rg, o, rprev, dy_ret, ret_norm_gain, cos_t, sin_t, tables, tm, hp, norm_gain, dlg, dwg_lru,
        ride=_chip_ride([q_out]))
    f_out = _chip_sum(q_out, e_out, j_arr, 128, "grad_chip_sum_out")
    zero_row = jnp.zeros((1, d), F32)
    vecs = [zero_row, dcb, dbr, dbi, dlam, dgain, dgf]
    g_small = jnp.concatenate([dwr, dwi, jnp.zeros((N_DEV, N_META, LANES), F32), _units(dcw[0:4])]
                              + [_units(a) for a in vecs]
                              + [jnp.zeros((N_DEV, UNIT_ROWS - ROW_VEC - N_VEC, LANES), F32)], axis=1)
    dparts = [dlx, dlg, dq, dk, dv, drg]
    g_in = dwg.reshape(N_DEV, dwg.shape[1] // 2, dwg.shape[2])
    q_in = _pair_exchange_sum(g_in, c_arr, 128, "grad_pair_exchange_sum_in")
    q_small = _pair_exchange_sum(g_small, c_arr, UNIT_ROWS, "grad_pair_exchange_sum_small")
    (grad_x, dmeta, dg1), (e_in, e_small, s_out) = _in_proj_dx(
        dparts, hp, dh2, norm_gain, wg, s_len, tm,
        ride=_join_rides(_chip_ride([q_in, q_small]), _sibling_ride([f_out])))
    f_in = _chip_sum(q_in, e_in, j_arr, 128, "grad_chip_sum_in")
    f_small = _chip_sum(q_small, e_small, j_arr, UNIT_ROWS, "grad_chip_sum_small", loss_part=loss_acc)
    tail = jnp.concatenate([_units(dmeta), _units(dg1), jnp.zeros((N_DEV, TAIL_ROWS - N_META - 1, LANES), F32)],
                           axis=1).reshape(N_DEV * TAIL_ROWS, LANES)
    s_in, o_small = _finish_exchange(f_in, jnp.concatenate([f_small, tail], axis=0))

    res_in = _adamw_big(w_in[0], f_in, s_in, m_w_in[0], v_w_in[0], c_arr, 256, "adamw_w_in")
    res_out = _adamw_big(w_out[0], f_out, s_out, m_w_out[0], v_w_out[0], c_arr, 256, "adamw_w_out")
    small_params = {
        "meta_tokens": (meta_tokens, m_meta_tokens, v_meta_tokens),
        "norm_gain": (norm_gain, m_norm_gain, v_norm_gain),
        "conv_w": (conv_w[0], m_conv_w[0], v_conv_w[0]),
        "conv_b": (conv_b, m_conv_b, v_conv_b),
        "w_rg": (w_rg[0], m_w_rg[0], v_w_rg[0]),
        "b_rg": (b_rg, m_b_rg, v_b_rg),
        "w_ig": (w_ig[0], m_w_ig[0], v_w_ig[0]),
        "b_ig": (b_ig, m_b_ig, v_b_ig),
        "lru_lambda": (lru_lambda, m_lru_lambda, v_lru_lambda),
        "ret_norm_gain": (ret_norm_gain, m_ret_norm_gain, v_ret_norm_gain),
        "final_norm_gain": (gain_f, m_final_norm_gain.reshape(1, d), v_final_norm_gain.reshape(1, d)),
    }
    res, loss = _adamw_small(j_arr, o_small, small_params)
    res["w_in"] = tuple(res_in)
    res["w_out"] = tuple(res_out)

    order = ["meta_tokens", "norm_gain", "w_in", "conv_w", "conv_b", "w_rg", "b_rg", "w_ig", "b_ig", "lru_lambda",
             "ret_norm_gain", "w_out", "final_norm_gain"]
    shapes = {"w_in": w_in.shape, "conv_w": conv_w.shape, "w_rg": w_rg.shape, "w_ig": w_ig.shape,
              "w_out": w_out.shape, "final_norm_gain": final_norm_gain.shape}
    outs = [loss, grad_x.reshape(x.shape)]
    for kind in range(4):
        for name in order:
            a = res[name][kind]
            outs.append(a.reshape(shapes[name]) if name in shapes else a)
    return tuple(outs)
```

```python
import functools

import jax
import jax.numpy as jnp
from jax import lax
from jax.experimental import pallas as pl
from jax.experimental.pallas import tpu as pltpu

F32 = jnp.float32
BF16 = jnp.bfloat16

N_META = 16
CHUNK = 128
PAD_ROWS = CHUNK - N_META
HEADS = 8
QK_DIM = 64
LANES = 128
SUBLANES = 8
LRU_C = 8.0
EPS = 1e-6
ROPE_BASE = 10000.0
ADAM_LR = 0.001
ADAM_B1 = 0.9
ADAM_B2 = 0.999
ADAM_EPS = 1e-08
ADAM_WD = 0.01
ADAM_STEP = 10
N_CHIPS = 4
N_DEV = 8
TOKEN_TILE = 384
VMEM_LIMIT = 58 * 1024 * 1024
MESH = pl.DeviceIdType.MESH

VMEM_SPEC = pl.BlockSpec(memory_space=pltpu.VMEM)
SMEM_SPEC = pl.BlockSpec(memory_space=pltpu.SMEM)
ANY_SPEC = pl.BlockSpec(memory_space=pl.ANY)

ROW_WR, ROW_WI, ROW_META, ROW_CONV, ROW_VEC, UNIT_ROWS = 0, 128, 256, 272, 276, 288
N_VEC = 7
ROW_LOSS = ROW_VEC + N_VEC
TAIL_ROWS = 24


def _dot(a, b):
    return jnp.dot(a, b, preferred_element_type=F32)


def _dot_nt(a, b):
    return lax.dot_general(a, b, (((1,), (1,)), ((), ())), preferred_element_type=F32)


def _dot_tn(a, b):
    return lax.dot_general(a, b, (((0,), (0,)), ((), ())), preferred_element_type=F32)


def _sigmoid(x):
    return 0.5 * jnp.tanh(0.5 * x) + 0.5


def _shift_down(x, prev8, s):
    rolled = pltpu.roll(x, s, 0)
    rows = lax.broadcasted_iota(jnp.int32, (SUBLANES, x.shape[1]), 0)
    top = jnp.where(rows < s, pltpu.roll(prev8, s, 0), rolled[0:SUBLANES])
    return jnp.concatenate([top, rolled[SUBLANES:]], axis=0)


def _shift_up(x, next8, s):
    n = x.shape[0]
    rolled = pltpu.roll(x, n - s, 0)
    rows = lax.broadcasted_iota(jnp.int32, (SUBLANES, x.shape[1]), 0)
    bot = jnp.where(rows >= SUBLANES - s, pltpu.roll(next8, SUBLANES - s, 0), rolled[n - SUBLANES:n])
    return jnp.concatenate([rolled[:n - SUBLANES], bot], axis=0)


def _rot_partner(t):
    w = t.shape[1]
    lane = lax.broadcasted_iota(jnp.int32, t.shape, 1)
    first = (lane % QK_DIM) < (QK_DIM // 2)
    return jnp.where(first, pltpu.roll(t, w - QK_DIM // 2, 1), pltpu.roll(t, QK_DIM // 2, 1))


def _tile_lanes(t, reps):
    return jnp.concatenate([t] * reps, axis=1)


class _Ride:
    def __init__(self, srcs, dst_shapes, n_copies, make):
        self.srcs, self.dst_shapes, self.n_copies, self.make = list(srcs), list(dst_shapes), n_copies, make


def _join_rides(a, b):
    def make(src, dst, send_sems, recv_sems, base):
        na, da = len(a.srcs), len(a.dst_shapes)
        return (a.make(src[:na], dst[:da], send_sems, recv_sems, base)
                + b.make(src[na:], dst[da:], send_sems, recv_sems, base + a.n_copies))

    return _Ride(a.srcs + b.srcs, a.dst_shapes + b.dst_shapes, a.n_copies + b.n_copies, make)


def _position():
    x, y, c = lax.axis_index("x"), lax.axis_index("y"), lax.axis_index("c")
    return x, y, c, [(1 - x, y), (x, 1 - y), (1 - x, 1 - y)]


def _remote(src, dst, send_sems, recv_sems, k, to):
    return pltpu.make_async_remote_copy(src_ref=src, dst_ref=dst, send_sem=send_sems.at[k], recv_sem=recv_sems.at[k],
                                        device_id=to, device_id_type=MESH)


def _pair_ride(bufs):
    def make(src, dst, send_sems, recv_sems, base):
        x, y, c, _ = _position()
        return [_remote(src[b].at[2 * jj + 1 - c], dst[b].at[jj], send_sems, recv_sems, base + b * N_CHIPS + jj,
                        (x, y, 1 - c)) for b in range(len(bufs)) for jj in range(N_CHIPS)]

    shapes = [jax.ShapeDtypeStruct((N_CHIPS,) + b.shape[1:], b.dtype) for b in bufs]
    return _Ride(bufs, shapes, N_CHIPS * len(bufs), make)


def _chip_ride(bufs):
    def make(src, dst, send_sems, recv_sems, base):
        x, y, c, chips = _position()
        return [_remote(src[b].at[2 * cx + cy], dst[b].at[2 * x + y], send_sems, recv_sems, base + b * 3 + k,
                        (cx, cy, c)) for b in range(len(bufs)) for k, (cx, cy) in enumerate(chips)]

    shapes = [jax.ShapeDtypeStruct(b.shape, b.dtype) for b in bufs]
    return _Ride(bufs, shapes, 3 * len(bufs), make)


def _sibling_ride(bufs):
    def make(src, dst, send_sems, recv_sems, base):
        x, y, c, _ = _position()
        return [_remote(src[b], dst[b], send_sems, recv_sems, base + b, (x, y, 1 - c)) for b in range(len(bufs))]

    shapes = [jax.ShapeDtypeStruct(b.shape, b.dtype) for b in bufs]
    return _Ride(bufs, shapes, len(bufs), make)


def _hosted_call(body, ride, n_steps, *, name, in_specs, out_specs, out_shape, scratch_shapes, args):
    params = pltpu.CompilerParams(dimension_semantics=("arbitrary",), vmem_limit_bytes=VMEM_LIMIT)
    if ride is None:
        res = pl.pallas_call(body, name=name, grid=(n_steps,), in_specs=list(in_specs), out_specs=list(out_specs),
                             out_shape=list(out_shape), scratch_shapes=list(scratch_shapes),
                             compiler_params=params)(*args)
        return list(res), []
    sizes = [len(in_specs), len(ride.srcs), len(out_specs), len(ride.dst_shapes), len(scratch_shapes), 2]

    def hosted(*refs):
        groups, pos = [], 0
        for n in sizes:
            groups.append(refs[pos:pos + n])
            pos += n
        ins, rin, outs, rout, scr, (send_sems, recv_sems) = groups
        i = pl.program_id(0)

        @pl.when(i == 0)
        def _():
            for cp in ride.make(rin, rout, send_sems, recv_sems, 0):
                cp.start()

        body(*ins, *outs, *scr)

        @pl.when(i == n_steps - 1)
        def _():
            for cp in ride.make(rin, rout, send_sems, recv_sems, 0):
                cp.wait()

    n_out = len(out_specs)
    res = pl.pallas_call(
        hosted,
        name=name,
        grid=(n_steps,),
        in_specs=list(in_specs) + [ANY_SPEC] * len(ride.srcs),
        out_specs=list(out_specs) + [ANY_SPEC] * len(ride.dst_shapes),
        out_shape=list(out_shape) + ride.dst_shapes,
        scratch_shapes=list(scratch_shapes) + [pltpu.SemaphoreType.DMA((ride.n_copies,)),
                                               pltpu.SemaphoreType.DMA((ride.n_copies,))],
        compiler_params=params,
    )(*args, *ride.srcs)
    return list(res[:n_out]), list(res[n_out:])


def _gather_weights(w_in, small):
    r_in, c_in = w_in.shape
    h_in = r_in // 2
    q_in = h_in // 2

    def body(win_ref, small_ref, wg_ref, sg_ref, send_sems, recv_sems):
        x, y, c, chips = _position()
        j = 2 * x + y
        sibling = (x, y, 1 - c)
        xn, yn, dg = chips
        jx, jy, jd = (2 * cx + cy for cx, cy in chips)

        def half(jj, cc):
            return wg_ref.at[jj, pl.ds(cc * h_in, h_in), :]

        def quarter(jj, qq):
            return wg_ref.at[jj, pl.ds(c * h_in + qq * q_in, q_in), :]

        def copy(k, ref, to):
            return _remote(ref, ref, send_sems, recv_sems, k, to)

        def cast_rows(start, rows):
            start = pl.multiple_of(start, q_in)
            wg_ref[j, pl.ds(start, rows), :] = win_ref[pl.ds(start, rows), :].astype(BF16)

        first = [copy(0, quarter(j, 0), (*xn, c)), copy(2, quarter(j, 1), (*yn, c)),
                 copy(1, quarter(j, 1), (*xn, c)), copy(3, quarter(j, 0), (*yn, c))]
        sg_ref[j] = small_ref[...]
        cast_rows(c * h_in, q_in)
        first[0].start()
        cast_rows(c * h_in + q_in, q_in)
        for cp in first[1:]:
            cp.start()
        small_copies = [copy(9 + k, sg_ref.at[j], (cx, cy, c)) for k, (cx, cy) in enumerate(chips)]
        for cp in small_copies:
            cp.start()
        first += small_copies
        cast_rows((1 - c) * h_in, h_in)
        copy(0, quarter(jx, 0), sibling).wait_recv()
        along_y = copy(4, quarter(jx, 0), (*yn, c))
        along_y.start()
        copy(2, quarter(jy, 1), sibling).wait_recv()
        along_x = copy(5, quarter(jy, 1), (*xn, c))
        along_x.start()
        copy(1, quarter(jx, 1), sibling).wait_recv()
        to_sib = [copy(6, half(jx, c), sibling)]
        to_sib[-1].start()
        copy(3, quarter(jy, 0), sibling).wait_recv()
        to_sib.append(copy(7, half(jy, c), sibling))
        to_sib[-1].start()
        copy(4, quarter(jd, 0), sibling).wait_recv()
        copy(5, quarter(jd, 1), sibling).wait_recv()
        to_sib.append(copy(8, half(jd, c), sibling))
        to_sib[-1].start()
        for k, jk in enumerate((jx, jy, jd)):
            copy(6 + k, half(jk, 1 - c), sibling).wait_recv()
            copy(9 + k, sg_ref.at[jk], sibling).wait_recv()
        for cp in first + [along_y, along_x] + to_sib:
            cp.wait_send()

    return pl.pallas_call(
        body,
        name="gather_weights",
        out_shape=(jax.ShapeDtypeStruct((N_CHIPS, r_in, c_in), BF16),
                   jax.ShapeDtypeStruct((N_CHIPS,) + small.shape, F32)),
        in_specs=[VMEM_SPEC, VMEM_SPEC],
        out_specs=(VMEM_SPEC, VMEM_SPEC),
        scratch_shapes=[pltpu.SemaphoreType.DMA((12,)), pltpu.SemaphoreType.DMA((12,))],
        compiler_params=pltpu.CompilerParams(vmem_limit_bytes=VMEM_LIMIT),
    )(w_in, small)


def _proj_segments(d_lru, d_qk, d_ret, chunk_w):
    widths = [d_lru, d_lru, d_qk, d_qk, d_ret, d_ret]
    segs, col = [], 0
    for w in widths:
        parts, off = [], 0
        while off < w:
            jj, inner = divmod(col + off, chunk_w)
            take = min(w - off, chunk_w - inner)
            parts.append((jj, inner, off, take))
            off += take
        segs.append(parts)
        col += w
    return segs


def _in_proj(x2, meta, gain, wg, cos_t, sin_t, w_out, qdec, kdec, tm, d_lru, d_qk, d_ret):
    s_len, d = x2.shape
    tp = s_len + CHUNK
    nt, nb = tp // tm, tm // CHUNK
    segs = _proj_segments(d_lru, d_qk, d_ret, wg.shape[2])
    outs = [(d, F32), (d_lru, F32), (d_lru, F32)] + [(d_qk, BF16)] * 4 + [(d_ret, BF16), (d_ret, F32)]
    r_out, c_out = w_out.shape
    h_out = r_out // 2
    fwd_step = min(6, nt - 1)

    def gather_w_out(i, wout_ref, wo_ref, wob, send_sems, recv_sems, local_sem):
        x, y, c, chips = _position()
        j = 2 * x + y
        sibling = (x, y, 1 - c)

        def half(jj, cc):
            return wo_ref.at[jj, pl.ds(cc * h_out, h_out), :]

        local = pltpu.make_async_copy(wob, wo_ref.at[j], local_sem)
        first = [_remote(wob.at[pl.ds(c * h_out, h_out), :], half(j, c), send_sems, recv_sems, k, (cx, cy, c))
                 for k, (cx, cy) in enumerate(chips)]
        passed = [_remote(half(2 * cx + cy, c), half(2 * cx + cy, c), send_sems, recv_sems, 3 + k, sibling)
                  for k, (cx, cy) in enumerate(chips)]

        @pl.when(i == 0)
        def _():
            wob[...] = wout_ref[...].astype(BF16)
            local.start()
            for cp in first:
                cp.start()

        @pl.when(i == fwd_step)
        def _():
            for k, (cx, cy) in enumerate(chips):
                _remote(half(2 * cx + cy, c), half(2 * cx + cy, c), send_sems, recv_sems, k, sibling).wait_recv()
                passed[k].start()

        @pl.when(i == nt - 1)
        def _():
            for k, (cx, cy) in enumerate(chips):
                jk = 2 * cx + cy
                _remote(half(jk, 1 - c), half(jk, 1 - c), send_sems, recv_sems, 3 + k, sibling).wait_recv()
            for cp in first + passed:
                cp.wait_send()
            local.wait()

    def body(*refs):
        xb = refs[:nb]
        meta_ref, g_ref, w_ref, cos_ref, sin_ref, wout_ref, qdec_ref, kdec_ref = refs[nb:nb + 8]
        hp_ref, lx_ref, lg_ref, qb_ref, kb_ref, qd_ref, kd_ref, vb_ref, rg_ref = refs[nb + 8:nb + 17]
        wo_ref, q_s, k_s, wob, send_sems, recv_sems, local_sem = refs[nb + 17:]
        i = pl.program_id(0)
        gather_w_out(i, wout_ref, wo_ref, wob, send_sems, recv_sems, local_sem)
        blocks = [r[...] for r in xb]
        head = jnp.concatenate([jnp.zeros((PAD_ROWS, d), F32), meta_ref[...]], axis=0)
        blocks[0] = jnp.where(i == 0, head, blocks[0])
        h = jnp.concatenate(blocks, axis=0)
        hp_ref[...] = h
        rinv = lax.rsqrt(jnp.mean(h * h, axis=-1, keepdims=True) + EPS)
        u = ((h * rinv) * g_ref[...]).astype(BF16)
        for out_ref, parts in zip([lx_ref, lg_ref, q_s, k_s, vb_ref, rg_ref], segs):
            for jj, inner, off, take in parts:
                out_ref[:, off:off + take] = _dot(u, w_ref[jj, :, inner:inner + take]).astype(out_ref.dtype)
        cos = _tile_lanes(cos_ref[...], d_qk // LANES)
        sin = _tile_lanes(sin_ref[...], d_qk // LANES)
        q = q_s[...]
        q = q * cos + _rot_partner(q) * sin
        k = k_s[...]
        k = (k * cos + _rot_partner(k) * sin) * (QK_DIM ** -0.5)
        qb_ref[...] = q.astype(BF16)
        kb_ref[...] = k.astype(BF16)
        qd_ref[...] = (q * jnp.concatenate([qdec_ref[...]] * nb, axis=0)).astype(BF16)
        kd_ref[...] = (k * jnp.concatenate([kdec_ref[...]] * nb, axis=0)).astype(BF16)

    x_specs = [pl.BlockSpec((CHUNK, d), functools.partial(lambda i, b: (jnp.maximum(i * nb + b - 1, 0), 0), b=b))
               for b in range(nb)]
    tile = lambda w: pl.BlockSpec((tm, w), lambda i: (i, 0))
    return pl.pallas_call(
        body,
        name="in_proj",
        grid=(nt,),
        in_specs=x_specs + [pl.BlockSpec(meta.shape, lambda i: (0, 0)),
                            pl.BlockSpec(gain.shape, lambda i: (0, 0)),
                            pl.BlockSpec(wg.shape, lambda i: (0, 0, 0)),
                            tile(LANES), tile(LANES),
                            pl.BlockSpec(w_out.shape, lambda i: (0, 0)),
                            pl.BlockSpec(qdec.shape, lambda i: (0, 0)), pl.BlockSpec(kdec.shape, lambda i: (0, 0))],
        out_specs=[tile(w) for w, _ in outs] + [ANY_SPEC],
        out_shape=[jax.ShapeDtypeStruct((tp, w), dt) for w, dt in outs]
                  + [jax.ShapeDtypeStruct((N_CHIPS, r_out, c_out), BF16)],
        scratch_shapes=[pltpu.VMEM((tm, d_qk), F32), pltpu.VMEM((tm, d_qk), F32),
                        pltpu.VMEM((r_out, c_out), BF16), pltpu.SemaphoreType.DMA((6,)),
                        pltpu.SemaphoreType.DMA((6,)), pltpu.SemaphoreType.DMA],
        compiler_params=pltpu.CompilerParams(dimension_semantics=("arbitrary",), vmem_limit_bytes=VMEM_LIMIT),
    )(*([x2] * nb), meta, gain, wg, cos_t, sin_t, w_out, qdec, kdec)


def _to_groups(ref3, x):
    for g in range(ref3.shape[0]):
        ref3[g] = x[:, g * LANES:(g + 1) * LANES]


def _from_groups(ref3):
    return jnp.concatenate([ref3[g] for g in range(ref3.shape[0])], axis=1)


def _segment_scan(a3, u3, out3, p3, carry, tm, reverse):
    groups = a3.shape[0]
    seg = tm // SUBLANES

    def step(j, state):
        hs, ps = state
        rows = pl.ds((seg - 1 - j) if reverse else j, SUBLANES, stride=seg)
        new_h, new_p = [], []
        for g in range(groups):
            a = a3[g, rows, :]
            h = a * hs[g] + u3[g, rows, :]
            p = ps[g] * a
            out3[g, rows, :] = h
            p3[g, rows, :] = p
            new_h.append(h)
            new_p.append(p)
        return tuple(new_h), tuple(new_p)

    zeros = tuple(jnp.zeros((SUBLANES, LANES), F32) for _ in range(groups))
    ones = tuple(jnp.ones((SUBLANES, LANES), F32) for _ in range(groups))
    lax.fori_loop(0, seg, step, (zeros, ones))
    carries = [carry[:, g * LANES:(g + 1) * LANES] for g in range(groups)]
    for s in (reversed(range(SUBLANES)) if reverse else range(SUBLANES)):
        rows = slice(s * seg, (s + 1) * seg)
        edge = s * seg if reverse else (s + 1) * seg - 1
        for g in range(groups):
            out3[g, rows, :] = out3[g, rows, :] + p3[g, rows, :] * carries[g]
            carries[g] = out3[g, edge:edge + 1, :]
    return jnp.concatenate(carries, axis=1)


def _softplus_neg(lam):
    z = -lam
    e = jnp.exp(-jnp.abs(z))
    e1 = 1.0 + e
    log1p_e = jnp.where(e1 == 1.0, e, jnp.log(e1) * (e / (e1 - 1.0)))
    return jnp.maximum(z, 0.0) + log1p_e


def _lru_fwd(lx, lg, cw, cb, wr, br, wi, bi, lam, tm):
    tp, w = lx.shape
    nt = tp // tm
    per8 = tm // SUBLANES
    n_heads = wr.shape[0]

    def body(lx_ref, lxp_ref, lg_ref, cw_ref, cb_ref, wr_ref, br_ref, wi_ref, bi_ref, lam_ref,
             hl_ref, y_ref, xc_ref, r_ref, ig_ref, a_ref, beta_ref, w4_ref, a_s, u_s, h_s, p_s, carry):
        i = pl.program_id(0)

        @pl.when(i == 0)
        def _():
            carry[...] = jnp.zeros_like(carry)

        lxv = lx_ref[...]
        prev8 = jnp.where(i == 0, 0.0, lxp_ref[...])
        xc = cb_ref[...] + _shift_down(lxv, prev8, 3) * cw_ref[0:1, :]
        xc = xc + _shift_down(lxv, prev8, 2) * cw_ref[1:2, :]
        xc = xc + _shift_down(lxv, prev8, 1) * cw_ref[2:3, :]
        xc = xc + lxv * cw_ref[3:4, :]
        xc_ref[...] = xc
        pre_r, pre_i = [], []
        for hd in range(n_heads):
            xh = xc[:, hd * LANES:(hd + 1) * LANES].astype(BF16)
            pre_r.append(_dot(xh, wr_ref[hd].astype(BF16)))
            pre_i.append(_dot(xh, wi_ref[hd].astype(BF16)))
        r = _sigmoid(jnp.concatenate(pre_r, axis=1) + br_ref[...])
        ig = _sigmoid(jnp.concatenate(pre_i, axis=1) + bi_ref[...])
        r_ref[...] = r
        ig_ref[...] = ig
        log_a = (-LRU_C * r) * _softplus_neg(lam_ref[...])
        a = jnp.exp(log_a)
        a_ref[...] = a
        a2 = a * a
        beta2 = jnp.maximum((1.0 + a2) * jnp.tanh(-log_a), 1e-37)
        rsb = lax.rsqrt(beta2)
        beta = beta2 * rsb
        beta_ref[...] = beta
        w4_ref[...] = a2 * rsb
        row = lax.broadcasted_iota(jnp.int32, (tm, 1), 0) + i * tm
        _to_groups(a_s, a)
        _to_groups(u_s, jnp.where(row >= PAD_ROWS, beta * ig * xc, 0.0))
        carry[0:1, :] = _segment_scan(a_s, u_s, h_s, p_s, carry[0:1, :], tm, reverse=False)
        hl = _from_groups(h_s)
        hl_ref[...] = hl
        g = lg_ref[...]
        y_ref[...] = (hl * (g * _sigmoid(g))).astype(BF16)

    tile = pl.BlockSpec((tm, w), lambda i: (i, 0))
    prev = pl.BlockSpec((SUBLANES, w), lambda i: (jnp.maximum(i * per8 - 1, 0), 0))
    vec = pl.BlockSpec((1, w), lambda i: (0, 0))
    mat = pl.BlockSpec(wr.shape, lambda i: (0, 0, 0))
    f32_out = jax.ShapeDtypeStruct((tp, w), F32)
    return pl.pallas_call(
        body,
        name="lru_fwd",
        grid=(nt,),
        in_specs=[tile, prev, tile, pl.BlockSpec(cw.shape, lambda i: (0, 0)), vec, mat, vec, mat, vec, vec],
        out_specs=[tile] * 8,
        out_shape=[f32_out, jax.ShapeDtypeStruct((tp, w), BF16)] + [f32_out] * 6,
        scratch_shapes=[pltpu.VMEM((w // LANES, tm, LANES), F32)] * 4 + [pltpu.VMEM((SUBLANES, w), F32)],
        compiler_params=pltpu.CompilerParams(dimension_semantics=("arbitrary",), vmem_limit_bytes=VMEM_LIMIT),
    )(lx, lx, lg, cw, cb, wr, br, wi, bi, lam)


def _ret_tables():
    log_g = jnp.log1p(-jnp.exp2(-5.0 - jnp.arange(HEADS, dtype=F32)))
    idx = jnp.arange(CHUNK, dtype=F32)
    diff = idx[:, None] - idx[None, :]
    dmask = jnp.where(diff[None] >= 0.0, jnp.exp(jnp.maximum(diff, 0.0)[None] * log_g[:, None, None]), 0.0)
    kdec = jnp.repeat(jnp.exp((CHUNK - 1.0 - idx)[:, None] * log_g[None, :]), QK_DIM, axis=1)
    qdec = jnp.repeat(jnp.exp((idx + 1.0)[:, None] * log_g[None, :]), QK_DIM, axis=1)
    g_chunk = jnp.exp(CHUNK * log_g)
    g_rows = jnp.repeat(g_chunk, QK_DIM).reshape(HEADS // 2, 2 * QK_DIM, 1)
    g_state = jnp.broadcast_to(g_rows, (HEADS // 2, 2 * QK_DIM, 2 * LANES))
    r_head = jnp.arange(2 * QK_DIM)[:, None] // QK_DIM
    c_head = jnp.arange(2 * LANES)[None, :] // LANES
    block_diag = (r_head == c_head).astype(F32)
    return dmask, qdec, kdec, g_state, block_diag


def _head_norm(o_h):
    mu = jnp.mean(o_h, axis=-1, keepdims=True)
    oc = o_h - mu
    var = jnp.mean(oc * oc, axis=-1, keepdims=True)
    rstd = lax.rsqrt(var + EPS)
    return oc * rstd, rstd


def _ret_fwd(qb, kb, qd, kd, vb, rg, gain, tables, tm):
    tp, d_qk = qb.shape
    d_ret = vb.shape[1]
    n_ch = tp // CHUNK
    cps = tm // CHUNK
    n_pairs = HEADS // 2
    dmask, _, _, g_state, block_diag = tables

    def body(q_ref, k_ref, qd_ref, kd_ref, v_ref, rg_ref, gain_ref, dm_ref, gs_ref, bd_ref,
             o_ref, y_ref, rp_ref, state):
        n = pl.program_id(0)

        @pl.when(n == 0)
        def _():
            state[...] = jnp.zeros_like(state)

        lane = lax.broadcasted_iota(jnp.int32, (CHUNK, LANES), 1)
        for ci in range(cps):
            rs = slice(ci * CHUNK, (ci + 1) * CHUNK)
            for p in range(n_pairs):
                qs = slice(p * LANES, (p + 1) * LANES)
                vs = slice(p * 2 * LANES, (p + 1) * 2 * LANES)
                qp, kb = q_ref[rs, qs], k_ref[rs, qs]
                vb = v_ref[rs, vs]
                qd, kd = qd_ref[rs, qs], kd_ref[rs, qs]
                st = state[p]
                st_b = st.astype(BF16)
                rp_ref[ci, p] = st_b
                cross = _dot(qd, st_b)
                for e in range(2):
                    hd = 2 * p + e
                    hs = slice(hd * LANES, (hd + 1) * LANES)
                    es = slice(e * LANES, (e + 1) * LANES)
                    qm = jnp.where((lane // QK_DIM) == e, qp, jnp.zeros_like(qp))
                    s = _dot_nt(qm, kb) * dm_ref[hd]
                    o_h = _dot(s.astype(BF16), vb[:, es]) + cross[:, es]
                    o_ref[rs, hs] = o_h
                    xhat, _ = _head_norm(o_h)
                    g = rg_ref[rs, hs]
                    y_ref[rs, hs] = ((xhat * gain_ref[:, hs]) * (g * _sigmoid(g))).astype(BF16)
                state[p] = gs_ref[p] * st + bd_ref[...] * _dot_tn(kd, vb)

    ch = lambda w: pl.BlockSpec((tm, w), lambda n: (n, 0))
    const2 = lambda a: pl.BlockSpec(a.shape, lambda n: (0, 0))
    const3 = lambda a: pl.BlockSpec(a.shape, lambda n: (0, 0, 0))
    return pl.pallas_call(
        body,
        name="ret_fwd",
        grid=(n_ch // cps,),
        in_specs=[ch(d_qk)] * 4 + [ch(d_ret), ch(d_ret), const2(gain), const3(dmask), const3(g_state),
                                   const2(block_diag)],
        out_specs=[ch(d_ret), ch(d_ret),
                   pl.BlockSpec((cps, n_pairs, 2 * QK_DIM, 2 * LANES), lambda n: (n, 0, 0, 0))],
        out_shape=[jax.ShapeDtypeStruct((tp, d_ret), F32), jax.ShapeDtypeStruct((tp, d_ret), BF16),
                   jax.ShapeDtypeStruct((n_ch, n_pairs, 2 * QK_DIM, 2 * LANES), BF16)],
        scratch_shapes=[pltpu.VMEM((n_pairs, 2 * QK_DIM, 2 * LANES), F32)],
        compiler_params=pltpu.CompilerParams(dimension_semantics=("arbitrary",), vmem_limit_bytes=VMEM_LIMIT),
    )(qb, kb, qd, kd, vb, rg, gain, dmask, g_state, block_diag)


def _out_proj_loss(y_lru, y_ret, hp, tgt, wo, gain_f, tm):
    tp, d = hp.shape
    w_lru = y_lru.shape[1]
    w_mix = wo.shape[0]
    nt, nb = tp // tm, tm // CHUNK

    def body(*refs):
        yl_ref, yr_ref, hp_ref = refs[:3]
        tb = refs[3:3 + nb]
        wo_ref, gf_ref = refs[3 + nb:5 + nb]
        dh2_ref, dyl_ref, dyr_ref, dwo_ref, dgf_ref, loss_ref = refs[5 + nb:]
        i = pl.program_id(0)

        @pl.when(i == 0)
        def _():
            dwo_ref[...] = jnp.zeros_like(dwo_ref)
            dgf_ref[...] = jnp.zeros_like(dgf_ref)
            loss_ref[...] = jnp.zeros_like(loss_ref)

        yl, yr = yl_ref[...], yr_ref[...]
        h2 = hp_ref[...] + _dot(yl, wo_ref[0:w_lru, :]) + _dot(yr, wo_ref[w_lru:w_mix, :])
        rinv = lax.rsqrt(jnp.mean(h2 * h2, axis=-1, keepdims=True) + EPS)
        nrm = h2 * rinv
        gf = gf_ref[...]
        tgt_v = jnp.concatenate([r[...] for r in tb], axis=0)
        row = lax.broadcasted_iota(jnp.int32, (tm, 1), 0) + i * tm
        err = jnp.where(row >= CHUNK, nrm * gf - tgt_v, 0.0)
        loss_ref[...] += 0.5 * jnp.sum(jnp.mean(err * err, axis=-1, keepdims=True))
        dout = err * (1.0 / d)
        dgf_ref[...] += jnp.sum(dout * nrm, axis=0, keepdims=True)
        dn = dout * gf
        dh2 = rinv * (dn - nrm * jnp.mean(dn * nrm, axis=-1, keepdims=True))
        dh2_ref[...] = dh2
        dh2b = dh2.astype(BF16)
        dyl_ref[...] = _dot_nt(dh2b, wo_ref[0:w_lru, :])
        dyr_ref[...] = _dot_nt(dh2b, wo_ref[w_lru:w_mix, :])
        dwo_ref[0:w_lru, :] += _dot_tn(yl, dh2b)
        dwo_ref[w_lru:w_mix, :] += _dot_tn(yr, dh2b)

    tile = lambda w: pl.BlockSpec((tm, w), lambda i: (i, 0))
    t_specs = [pl.BlockSpec((CHUNK, d), functools.partial(lambda i, b: (jnp.maximum(i * nb + b - 1, 0), 0), b=b))
               for b in range(nb)]
    return pl.pallas_call(
        body,
        name="out_proj_loss",
        grid=(nt,),
        in_specs=[tile(w_lru), tile(w_mix - w_lru), tile(d)] + t_specs +
                 [pl.BlockSpec(wo.shape, lambda i: (0, 0)), pl.BlockSpec(gain_f.shape, lambda i: (0, 0))],
        out_specs=[tile(d), tile(w_lru), tile(w_mix - w_lru), pl.BlockSpec(wo.shape, lambda i: (0, 0)),
                   pl.BlockSpec((1, d), lambda i: (0, 0)), pl.BlockSpec((SUBLANES, LANES), lambda i: (0, 0))],
        out_shape=[jax.ShapeDtypeStruct((tp, d), F32), jax.ShapeDtypeStruct((tp, w_lru), F32),
                   jax.ShapeDtypeStruct((tp, w_mix - w_lru), F32), jax.ShapeDtypeStruct(wo.shape, F32),
                   jax.ShapeDtypeStruct((1, d), F32), jax.ShapeDtypeStruct((SUBLANES, LANES), F32)],
        compiler_params=pltpu.CompilerParams(dimension_semantics=("arbitrary",), vmem_limit_bytes=VMEM_LIMIT),
    )(y_lru, y_ret, hp, *([tgt] * nb), wo, gain_f)


def _ret_bwd(qb, kb, qd, kd, vb, rg, o, rprev, dy, gain, cos_t, sin_t, tables, tm, ride=None):
    tp, d_qk = qb.shape
    d_ret = vb.shape[1]
    n_ch = tp // CHUNK
    cps = tm // CHUNK
    n_pairs = HEADS // 2
    dmask, qdec, kdec, g_state, block_diag = tables

    dmask_t = jnp.swapaxes(dmask, 1, 2)

    def body(q_ref, k_ref, qdb_ref, kdb_ref, v_ref, rg_ref, o_ref, rp_ref, dy_ref, gain_ref, cos_ref, sin_ref,
             dm_ref, dmt_ref, qd_ref, kd_ref, gs_ref, bd_ref, dq_ref, dk_ref, dv_ref, drg_ref, dgain_ref, dstate):
        n = pl.program_id(0)

        @pl.when(n == 0)
        def _():
            dstate[...] = jnp.zeros_like(dstate)
            dgain_ref[...] = jnp.zeros_like(dgain_ref)

        lane = lax.broadcasted_iota(jnp.int32, (CHUNK, LANES), 1)
        for ci in reversed(range(cps)):
            rs = slice(ci * CHUNK, (ci + 1) * CHUNK)
            dq_parts, dk_parts = [], []
            for p in range(n_pairs):
                qs = slice(p * LANES, (p + 1) * LANES)
                vs = slice(p * 2 * LANES, (p + 1) * 2 * LANES)
                do_parts = []
                for e in range(2):
                    hd = 2 * p + e
                    hs = slice(hd * LANES, (hd + 1) * LANES)
                    xhat, rstd = _head_norm(o_ref[rs, hs])
                    g = rg_ref[rs, hs]
                    sg = _sigmoid(g)
                    dyh = dy_ref[rs, hs]
                    gn = gain_ref[:, hs]
                    d_on = dyh * (g * sg)
                    drg_ref[rs, hs] = (dyh * (xhat * gn) * (sg * (1.0 + g * (1.0 - sg)))).astype(BF16)
                    dgain_ref[:, hs] += jnp.sum(d_on * xhat, axis=0, keepdims=True)
                    dxh = d_on * gn
                    do_parts.append(rstd * (dxh - jnp.mean(dxh, axis=-1, keepdims=True)
                                            - xhat * jnp.mean(dxh * xhat, axis=-1, keepdims=True)))
                do_b = jnp.concatenate(do_parts, axis=1).astype(BF16)
                qp, kb = q_ref[rs, qs], k_ref[rs, qs]
                vb = v_ref[rs, vs]
                qd, kd = qdb_ref[rs, qs], kdb_ref[rs, qs]
                dst = dstate[p]
                dst_b = dst.astype(BF16)
                dqp = _dot_nt(do_b, rp_ref[ci, p]) * qd_ref[:, qs]
                dkp = _dot_nt(vb, dst_b) * kd_ref[:, qs]
                dvp = _dot(kd, dst_b)
                dv_parts = []
                for e in range(2):
                    hd = 2 * p + e
                    es = slice(e * LANES, (e + 1) * LANES)
                    mine = (lane // QK_DIM) == e
                    qm = jnp.where(mine, qp, jnp.zeros_like(qp))
                    km = jnp.where(mine, kb, jnp.zeros_like(kb))
                    ds = (_dot_nt(do_b[:, es], vb[:, es]) * dm_ref[hd]).astype(BF16)
                    s_t = (_dot_nt(kb, qm) * dmt_ref[hd]).astype(BF16)
                    ds_t = (_dot_nt(vb[:, es], do_b[:, es]) * dmt_ref[hd]).astype(BF16)
                    dv_parts.append(dvp[:, es] + _dot(s_t, do_b[:, es]))
                    dqp = dqp + _dot(ds, km)
                    dkp = dkp + _dot(ds_t, qm)
                dv_ref[rs, vs] = jnp.concatenate(dv_parts, axis=1).astype(BF16)
                dstate[p] = gs_ref[p] * dst + bd_ref[...] * _dot_tn(qd, do_b)
                dq_parts.append(dqp)
                dk_parts.append(dkp)
            cos = _tile_lanes(cos_ref[rs, :], d_qk // LANES)
            sin = _tile_lanes(sin_ref[rs, :], d_qk // LANES)
            dq = jnp.concatenate(dq_parts, axis=1)
            dk = jnp.concatenate(dk_parts, axis=1) * (QK_DIM ** -0.5)
            dq_ref[rs, :] = (dq * cos + _rot_partner(dq * sin)).astype(BF16)
            dk_ref[rs, :] = (dk * cos + _rot_partner(dk * sin)).astype(BF16)

    last = n_ch // cps - 1
    ch = lambda w: pl.BlockSpec((tm, w), lambda n: (last - n, 0))
    const2 = lambda a: pl.BlockSpec(a.shape, lambda n: (0, 0))
    const3 = lambda a: pl.BlockSpec(a.shape, lambda n: (0, 0, 0))
    return _hosted_call(
        body, ride, n_ch // cps,
        name="ret_bwd",
        in_specs=[ch(d_qk)] * 4 + [ch(d_ret), ch(d_ret), ch(d_ret),
                  pl.BlockSpec((cps, n_pairs, 2 * QK_DIM, 2 * LANES), lambda n: (last - n, 0, 0, 0)),
                  ch(d_ret), const2(gain), ch(LANES), ch(LANES),
                  const3(dmask), const3(dmask_t), const2(qdec), const2(kdec), const3(g_state), const2(block_diag)],
        out_specs=[ch(d_qk), ch(d_qk), ch(d_ret), ch(d_ret), pl.BlockSpec((1, d_ret), lambda n: (0, 0))],
        out_shape=[jax.ShapeDtypeStruct((tp, d_qk), BF16), jax.ShapeDtypeStruct((tp, d_qk), BF16),
                   jax.ShapeDtypeStruct((tp, d_ret), BF16), jax.ShapeDtypeStruct((tp, d_ret), BF16),
                   jax.ShapeDtypeStruct((1, d_ret), F32)],
        scratch_shapes=[pltpu.VMEM((n_pairs, 2 * QK_DIM, 2 * LANES), F32)],
        args=(qb, kb, qd, kd, vb, rg, o, rprev, dy, gain, cos_t, sin_t, dmask, dmask_t, qdec, kdec, g_state,
              block_diag),
    )


def _lru_bwd(lx, lg, hl, dy, saved, cw, wr, wi, lam, tm, ride=None):
    tp, w = lx.shape
    nt = tp // tm
    per8 = tm // SUBLANES
    n_heads = wr.shape[0]

    def body(lx_ref, lg_ref, hl_ref, hlp_ref, dy_ref, xc_ref, r_ref, ig_ref, a_ref, beta_ref, w4_ref,
             cw_ref, wr_ref, wi_ref, lam_ref,
             dlx_ref, dlg_ref, dcw_ref, dcb_ref, dwr_ref, dbr_ref, dwi_ref, dbi_ref, dlam_ref,
             g_s, b_s, carry, dxc_next, a_next):
        i = pl.program_id(0)
        first_tile = i == nt - 1

        @pl.when(i == 0)
        def _():
            carry[...] = jnp.zeros_like(carry)
            dxc_next[...] = jnp.zeros_like(dxc_next)
            a_next[...] = jnp.zeros_like(a_next)
            for r in (dcw_ref, dcb_ref, dwr_ref, dbr_ref, dwi_ref, dbi_ref, dlam_ref):
                r[...] = jnp.zeros_like(r)

        lxv = lx_ref[...]
        a, beta, r, ig, xc = a_ref[...], beta_ref[...], r_ref[...], ig_ref[...], xc_ref[...]
        g = lg_ref[...]
        sg = _sigmoid(g)
        dyv = dy_ref[...]
        hlv = hl_ref[...]
        dlg_ref[...] = (dyv * hlv * (sg * (1.0 + g * (1.0 - sg)))).astype(BF16)
        _to_groups(g_s, dyv * (g * sg))
        _to_groups(b_s, _shift_up(a, a_next[...], 1))
        carry[0:1, :] = _segment_scan(b_s, g_s, g_s, b_s, carry[0:1, :], tm, reverse=True)
        a_next[...] = a[0:SUBLANES]
        dh = _from_groups(g_s)
        hprev = _shift_down(hlv, jnp.where(first_tile, 0.0, hlp_ref[...]), 1)
        row = lax.broadcasted_iota(jnp.int32, (tm, 1), 0) + (nt - 1 - i) * tm
        du = jnp.where(row >= PAD_ROWS, dh, 0.0)
        da = dh * hprev
        dbeta = du * ig * xc
        d_ig = du * beta * xc
        dxc = du * beta * ig
        dloga = da * a - dbeta * w4_ref[...]
        lam_v = lam_ref[...]
        dlam_ref[...] += jnp.sum(dloga * r, axis=0, keepdims=True) * (LRU_C * _sigmoid(-lam_v))
        dpr = (dloga * (-LRU_C * _softplus_neg(lam_v))) * r * (1.0 - r)
        dpi = d_ig * ig * (1.0 - ig)
        dbr_ref[...] += jnp.sum(dpr, axis=0, keepdims=True)
        dbi_ref[...] += jnp.sum(dpi, axis=0, keepdims=True)
        dxc_parts = []
        for hd in range(n_heads):
            hs = slice(hd * LANES, (hd + 1) * LANES)
            xh = xc[:, hs].astype(BF16)
            dprh = dpr[:, hs].astype(BF16)
            dpih = dpi[:, hs].astype(BF16)
            dwr_ref[hd] += _dot_tn(xh, dprh)
            dwi_ref[hd] += _dot_tn(xh, dpih)
            dxc_parts.append(_dot_nt(dprh, wr_ref[hd].astype(BF16)) + _dot_nt(dpih, wi_ref[hd].astype(BF16)))
        dxc = dxc + jnp.concatenate(dxc_parts, axis=1)
        nxt = dxc_next[...]
        up1, up2, up3 = _shift_up(dxc, nxt, 1), _shift_up(dxc, nxt, 2), _shift_up(dxc, nxt, 3)
        dlx = dxc * cw_ref[3:4, :]
        dlx = dlx + up1 * cw_ref[2:3, :]
        dlx = dlx + up2 * cw_ref[1:2, :]
        dlx = dlx + up3 * cw_ref[0:1, :]
        dlx_ref[...] = dlx.astype(BF16)
        dxc_next[...] = dxc[0:SUBLANES]
        dcb_ref[...] += jnp.sum(dxc, axis=0, keepdims=True)
        dcw_ref[0:1, :] += jnp.sum(up3 * lxv, axis=0, keepdims=True)
        dcw_ref[1:2, :] += jnp.sum(up2 * lxv, axis=0, keepdims=True)
        dcw_ref[2:3, :] += jnp.sum(up1 * lxv, axis=0, keepdims=True)
        dcw_ref[3:4, :] += jnp.sum(dxc * lxv, axis=0, keepdims=True)

    last = nt - 1
    tile = pl.BlockSpec((tm, w), lambda i: (last - i, 0))
    prev = pl.BlockSpec((SUBLANES, w), lambda i: (jnp.maximum((last - i) * per8 - 1, 0), 0))
    vec = pl.BlockSpec((1, w), lambda i: (0, 0))
    mat = pl.BlockSpec(wr.shape, lambda i: (0, 0, 0))
    cwb = pl.BlockSpec(cw.shape, lambda i: (0, 0))
    return _hosted_call(
        body, ride, nt,
        name="lru_bwd",
        in_specs=[tile, tile, tile, prev, tile] + [tile] * 6 + [cwb, mat, mat, vec],
        out_specs=[tile, tile, cwb, vec, mat, vec, mat, vec, vec],
        out_shape=[jax.ShapeDtypeStruct((tp, w), BF16), jax.ShapeDtypeStruct((tp, w), BF16),
                   jax.ShapeDtypeStruct(cw.shape, F32), jax.ShapeDtypeStruct((1, w), F32),
                   jax.ShapeDtypeStruct(wr.shape, F32), jax.ShapeDtypeStruct((1, w), F32),
                   jax.ShapeDtypeStruct(wr.shape, F32), jax.ShapeDtypeStruct((1, w), F32),
                   jax.ShapeDtypeStruct((1, w), F32)],
        scratch_shapes=[pltpu.VMEM((w // LANES, tm, LANES), F32)] * 2 + [pltpu.VMEM((SUBLANES, w), F32)] * 3,
        args=(lx, lg, hl, hl, dy, *saved, cw, wr, wi, lam),
    )


def _in_proj_dw(dparts, hp, gain, wg_shape, ride=None):
    tp, d = hp.shape
    n_ch = tp // CHUNK
    per = next(p for p in (4, 2, 5, 3, 1) if (n_ch - 1) % p == 0)
    n_steps = 1 + (n_ch - 1) // per
    widths = [p.shape[1] for p in dparts]
    segs = _proj_segments(widths[0], widths[2], widths[4], wg_shape[2])

    def body(*refs):
        dp = [refs[p * per:(p + 1) * per] for p in range(6)]
        hp_b = refs[6 * per:7 * per]
        g_ref, dwg_ref, acc, sem = refs[7 * per:]
        i = pl.program_id(0)

        def accumulate(blocks):
            h = jnp.concatenate([hp_b[b][...] for b in blocks], axis=0)
            rinv = lax.rsqrt(jnp.mean(h * h, axis=-1, keepdims=True) + EPS)
            u = ((h * rinv) * g_ref[...]).astype(BF16)
            for p_refs, parts in zip(dp, segs):
                for jj, inner, off, take in parts:
                    seg = jnp.concatenate([p_refs[b][:, off:off + take] for b in blocks], axis=0)
                    acc[jj, :, inner:inner + take] += _dot_tn(u, seg)

        @pl.when(i == 0)
        def _():
            acc[...] = jnp.zeros_like(acc)
            accumulate([0])

        @pl.when(i > 0)
        def _():
            accumulate(list(range(per)))

        @pl.when(i == n_steps - 1)
        def _():
            cp = pltpu.make_async_copy(acc, dwg_ref, sem)
            cp.start()
            cp.wait()

    def blocks(w):
        return [pl.BlockSpec((CHUNK, w), functools.partial(
            lambda i, b: (jnp.where(i == 0, b, per * (i - 1) + 1 + b), 0), b=b)) for b in range(per)]

    in_specs, args = [], []
    for a, w in list(zip(dparts, widths)) + [(hp, d)]:
        in_specs += blocks(w)
        args += [a] * per
    outs, rides = _hosted_call(
        body, ride, n_steps,
        name="in_proj_dw",
        in_specs=in_specs + [pl.BlockSpec(gain.shape, lambda i: (0, 0))],
        out_specs=[ANY_SPEC],
        out_shape=[jax.ShapeDtypeStruct(wg_shape, F32)],
        scratch_shapes=[pltpu.VMEM(wg_shape, F32), pltpu.SemaphoreType.DMA],
        args=(*args, gain),
    )
    return outs[0], rides


def _in_proj_dx(dparts, hp, dh2, gain, wg, s_len, tm, ride=None):
    tp, d = hp.shape
    nt = tp // tm
    widths = [p.shape[1] for p in dparts]
    segs = _proj_segments(widths[0], widths[2], widths[4], wg.shape[2])

    def body(*refs):
        dp = refs[:6]
        hp_ref, dh2_ref, g_ref, w_ref = refs[6:10]
        gx_ref, dmeta_ref, dg_ref = refs[10:13]
        stage, sems = refs[13:]
        i = pl.program_id(0)

        @pl.when(i == 0)
        def _():
            dg_ref[...] = jnp.zeros_like(dg_ref)

        h = hp_ref[...]
        rinv = lax.rsqrt(jnp.mean(h * h, axis=-1, keepdims=True) + EPS)
        nrm = h * rinv
        gv = g_ref[...]
        du = jnp.zeros((tm, d), F32)
        for p_ref, parts in zip(dp, segs):
            for jj, inner, off, take in parts:
                du = du + _dot_nt(p_ref[:, off:off + take], w_ref[jj, :, inner:inner + take])
        dg_ref[...] += jnp.sum(du * nrm, axis=0, keepdims=True)
        dn = du * gv
        dh = dh2_ref[...] + rinv * (dn - nrm * jnp.mean(dn * nrm, axis=-1, keepdims=True))

        def first_copy():
            return pltpu.make_async_copy(stage.at[0, pl.ds(CHUNK, tm - CHUNK), :],
                                         gx_ref.at[pl.ds(0, tm - CHUNK), :], sems.at[0])

        def tile_copy(slot, start):
            return pltpu.make_async_copy(stage.at[slot], gx_ref.at[pl.ds(start, tm), :], sems.at[slot])

        @pl.when(i == 0)
        def _():
            dmeta_ref[...] = dh[PAD_ROWS:CHUNK]
            stage[0] = dh
            first_copy().start()

        @pl.when(i > 0)
        def _():
            slot = 1 + i % 2

            @pl.when(i >= 3)
            def _():
                tile_copy(slot, 0).wait()

            stage[slot] = dh
            tile_copy(slot, pl.multiple_of(i * tm - CHUNK, CHUNK)).start()

        @pl.when(i == nt - 1)
        def _():
            first_copy().wait()
            for step in (nt - 2, nt - 1):
                if step >= 1:
                    tile_copy(1 + step % 2, 0).wait()

    tile = lambda w: pl.BlockSpec((tm, w), lambda i: (i, 0))
    return _hosted_call(
        body, ride, nt,
        name="in_proj_dx",
        in_specs=[tile(w) for w in widths] + [tile(d), tile(d), pl.BlockSpec(gain.shape, lambda i: (0, 0)),
                                              pl.BlockSpec(wg.shape, lambda i: (0, 0, 0))],
        out_specs=[ANY_SPEC, pl.BlockSpec((N_META, d), lambda i: (0, 0)), pl.BlockSpec((1, d), lambda i: (0, 0))],
        out_shape=[jax.ShapeDtypeStruct((s_len, d), F32), jax.ShapeDtypeStruct((N_META, d), F32),
                   jax.ShapeDtypeStruct((1, d), F32)],
        scratch_shapes=[pltpu.VMEM((3, tm, d), F32), pltpu.SemaphoreType.DMA((3,))],
        args=(*dparts, hp, dh2, gain, wg),
    )


def _pair_sum(buf, recv, c_arr, tr, name):
    _, rows, cols = buf.shape

    def body(c_ref, mine_ref, got_ref, out_ref):
        out_ref[...] = (mine_ref[...] + got_ref[...]).astype(BF16)

    grid_spec = pltpu.PrefetchScalarGridSpec(
        num_scalar_prefetch=1,
        grid=(N_CHIPS, rows // tr),
        in_specs=[pl.BlockSpec((1, tr, cols), lambda jj, r, c_ref: (2 * jj + c_ref[0], r, 0)),
                  pl.BlockSpec((1, tr, cols), lambda jj, r, c_ref: (jj, r, 0))],
        out_specs=pl.BlockSpec((1, tr, cols), lambda jj, r, c_ref: (jj, r, 0)),
    )
    return pl.pallas_call(
        body,
        name=name,
        grid_spec=grid_spec,
        out_shape=jax.ShapeDtypeStruct((N_CHIPS, rows, cols), BF16),
    )(c_arr, buf, recv)


def _pair_exchange_sum(buf, c_arr, tr, name):
    _, rows, cols = buf.shape
    per = rows // tr

    def body(c_ref, src_ref, mine_ref, out_ref, got, send_sems, recv_sems):
        jj, r = pl.program_id(0), pl.program_id(1)
        x, y, c, _ = _position()
        copies = [_remote(src_ref.at[2 * k + 1 - c], got.at[k], send_sems, recv_sems, k, (x, y, 1 - c))
                  for k in range(N_CHIPS)]

        @pl.when((jj == 0) & (r == 0))
        def _():
            for cp in copies:
                cp.start()

        for k in range(N_CHIPS):
            @pl.when((jj == k) & (r == 0))
            def _():
                copies[k].wait_recv()

        rows_r = pl.ds(pl.multiple_of(r * tr, tr), tr)
        out_ref[0] = (mine_ref[0] + got[jj, rows_r, :]).astype(BF16)

        @pl.when((jj == N_CHIPS - 1) & (r == per - 1))
        def _():
            for cp in copies:
                cp.wait_send()

    grid_spec = pltpu.PrefetchScalarGridSpec(
        num_scalar_prefetch=1,
        grid=(N_CHIPS, per),
        in_specs=[ANY_SPEC, pl.BlockSpec((1, tr, cols), lambda jj, r, c_ref: (2 * jj + c_ref[0], r, 0))],
        out_specs=pl.BlockSpec((1, tr, cols), lambda jj, r, c_ref: (jj, r, 0)),
        scratch_shapes=[pltpu.VMEM((N_CHIPS, rows, cols), F32), pltpu.SemaphoreType.DMA((N_CHIPS,)),
                        pltpu.SemaphoreType.DMA((N_CHIPS,))],
    )
    return pl.pallas_call(
        body,
        name=name,
        grid_spec=grid_spec,
        out_shape=jax.ShapeDtypeStruct((N_CHIPS, rows, cols), BF16),
        compiler_params=pltpu.CompilerParams(dimension_semantics=("arbitrary", "arbitrary"),
                                             vmem_limit_bytes=VMEM_LIMIT),
    )(c_arr, buf, buf)


def _chip_sum(mine, got, j_arr, tr, name, loss_part=None):
    _, rows, cols = got.shape
    extra = [] if loss_part is None else [loss_part]

    def body(j_ref, mine_ref, got_ref, *rest):
        out_ref = rest[-1]
        j = j_ref[0]
        acc = None
        for jj in range(N_CHIPS):
            term = jnp.where(j == jj, mine_ref[0], got_ref[jj]).astype(F32)
            acc = term if acc is None else acc + term
        out_ref[...] = acc
        if loss_part is not None:
            out_ref[ROW_LOSS:ROW_LOSS + 1, :] = rest[0][0:1, :]

    grid_spec = pltpu.PrefetchScalarGridSpec(
        num_scalar_prefetch=1,
        grid=(rows // tr,),
        in_specs=[pl.BlockSpec((1, tr, cols), lambda r, j_ref: (j_ref[0], r, 0)),
                  pl.BlockSpec((N_CHIPS, tr, cols), lambda r, j_ref: (0, r, 0))] +
                 [pl.BlockSpec(e.shape, lambda r, j_ref: (0, 0)) for e in extra],
        out_specs=pl.BlockSpec((tr, cols), lambda r, j_ref: (r, 0)),
    )
    return pl.pallas_call(
        body,
        name=name,
        grid_spec=grid_spec,
        out_shape=jax.ShapeDtypeStruct((rows, cols), F32),
    )(j_arr, mine, got, *extra)


def _finish_exchange(f_in, f_small):
    def body(fin_ref, fs_ref, rin_ref, os_ref, send_sems, recv_sems, local_sem):
        x, y, c, chips = _position()
        j = 2 * x + y
        me = 2 * j + c
        sibling = (x, y, 1 - c)
        local = pltpu.make_async_copy(fs_ref, os_ref.at[me], local_sem)
        local.start()

        def copy(k, src, dst, to):
            return _remote(src, dst, send_sems, recv_sems, k, to)

        first = [copy(0, fin_ref, rin_ref, sibling), copy(1, fs_ref, os_ref.at[me], sibling)]
        first += [copy(2 + k, fs_ref, os_ref.at[me], (cx, cy, c)) for k, (cx, cy) in enumerate(chips)]
        for cp in first:
            cp.start()
        passed = []
        for k, (cx, cy) in enumerate(chips):
            unit = 2 * (2 * cx + cy) + c
            copy(2 + k, fs_ref, os_ref.at[unit], sibling).wait_recv()
            fwd = copy(5 + k, os_ref.at[unit], os_ref.at[unit], sibling)
            fwd.start()
            passed.append(fwd)
        copy(0, fin_ref, rin_ref, sibling).wait_recv()
        copy(1, fs_ref, os_ref.at[2 * j + 1 - c], sibling).wait_recv()
        for k, (cx, cy) in enumerate(chips):
            unit = 2 * (2 * cx + cy) + 1 - c
            copy(5 + k, fs_ref, os_ref.at[unit], sibling).wait_recv()
        for cp in first + passed:
            cp.wait_send()
        local.wait()

    return pl.pallas_call(
        body,
        name="grad_finish_exchange",
        in_specs=[ANY_SPEC] * 2,
        out_specs=[ANY_SPEC] * 2,
        out_shape=[jax.ShapeDtypeStruct(f_in.shape, F32), jax.ShapeDtypeStruct((N_DEV,) + f_small.shape, F32)],
        scratch_shapes=[pltpu.SemaphoreType.DMA((8,)), pltpu.SemaphoreType.DMA((8,)), pltpu.SemaphoreType.DMA],
    )(f_in, f_small)


def _adamw_math(w, g, m, v):
    m = ADAM_B1 * m + (1.0 - ADAM_B1) * g
    v = ADAM_B2 * v + (1.0 - ADAM_B2) * (g * g)
    m_hat = m / (1.0 - ADAM_B1 ** ADAM_STEP)
    v_hat = v / (1.0 - ADAM_B2 ** ADAM_STEP)
    delta = -ADAM_LR * (m_hat / (jnp.sqrt(v_hat) + ADAM_EPS) + ADAM_WD * w)
    return delta, m, v


def _adamw_big(w, g_mine, g_sib, m, v, c_arr, tr, name):
    rows, cols = w.shape
    half = rows // 2
    per = half // tr

    def body(c_ref, w_ref, gm_ref, gs_ref, m_ref, v_ref, g_ref, d_ref, mo_ref, vo_ref):
        g = jnp.where(pl.program_id(0) == c_ref[0], gm_ref[...], gs_ref[...])
        g_ref[...] = g
        d_ref[...], mo_ref[...], vo_ref[...] = _adamw_math(w_ref[...], g, m_ref[...], v_ref[...])

    full = pl.BlockSpec((tr, cols), lambda h, r, c_ref: (h * per + r, 0))
    unit = pl.BlockSpec((tr, cols), lambda h, r, c_ref: (r, 0))
    grid_spec = pltpu.PrefetchScalarGridSpec(
        num_scalar_prefetch=1,
        grid=(2, per),
        in_specs=[full, unit, unit, full, full],
        out_specs=[full] * 4,
    )
    return pl.pallas_call(
        body,
        name=name,
        grid_spec=grid_spec,
        out_shape=[jax.ShapeDtypeStruct(w.shape, F32)] * 4,
    )(c_arr, w, g_mine, g_sib, m, v)


def _adamw_small(j_arr, packed, params):
    names = list(params)
    n = len(names)
    vec_names = ["norm_gain", "conv_b", "b_rg", "b_ig", "lru_lambda", "ret_norm_gain", "final_norm_gain"]

    def body(j_ref, pk_ref, *refs):
        ins = refs[:3 * n]
        outs = refs[3 * n:]
        j = j_ref[0]

        def shard(row, rows):
            return jnp.concatenate([pk_ref[2 * j, row:row + rows, :], pk_ref[2 * j + 1, row:row + rows, :]], axis=1)

        def tail_sum(unit, row, rows):
            start = pl.multiple_of(UNIT_ROWS + TAIL_ROWS * unit + row, SUBLANES)
            total = pk_ref[0, pl.ds(start, rows), :]
            for dev in range(1, N_DEV):
                total = total + pk_ref[dev, pl.ds(start, rows), :]
            return total

        for idx, name in enumerate(names):
            if name == "w_rg":
                g = pk_ref[:, ROW_WR:ROW_WR + LANES, :]
            elif name == "w_ig":
                g = pk_ref[:, ROW_WI:ROW_WI + LANES, :]
            elif name == "meta_tokens":
                g = jnp.concatenate([tail_sum(2 * j, 0, N_META), tail_sum(2 * j + 1, 0, N_META)], axis=1)
            elif name == "norm_gain":
                g = jnp.concatenate([tail_sum(u, N_META, SUBLANES)[0:1] for u in range(N_DEV)], axis=1)
            elif name == "conv_w":
                g = shard(ROW_CONV, 4)
            else:
                row = ROW_VEC + vec_names.index(name)
                g = jnp.concatenate([pk_ref[u, row:row + 1, :] for u in range(N_DEV)], axis=1)
            w_ref, m_ref, v_ref = ins[3 * idx:3 * idx + 3]
            delta, m, v = _adamw_math(w_ref[...], g, m_ref[...], v_ref[...])
            g_ref, d_ref, mo_ref, vo_ref = outs[4 * idx:4 * idx + 4]
            g_ref[...], d_ref[...], mo_ref[...], vo_ref[...] = g, delta, m, v
        total = pk_ref[0, ROW_LOSS:ROW_LOSS + 1, :]
        for u in range(1, N_DEV):
            total = total + pk_ref[u, ROW_LOSS:ROW_LOSS + 1, :]
        outs[4 * n][...] = jnp.broadcast_to(total, (SUBLANES, LANES))

    flat_in, out_shape = [], []
    for name in names:
        w, m, v = params[name]
        flat_in += [w, m, v]
        out_shape += [jax.ShapeDtypeStruct(w.shape, F32)] * 4
    out_shape.append(jax.ShapeDtypeStruct((SUBLANES, LANES), F32))
    res = pl.pallas_call(
        body,
        name="adamw_small",
        in_specs=[SMEM_SPEC, VMEM_SPEC] + [VMEM_SPEC] * (3 * n),
        out_specs=[VMEM_SPEC] * (4 * n + 1),
        out_shape=out_shape,
    )(j_arr, packed, *flat_in)
    return {name: tuple(res[4 * idx:4 * idx + 4]) for idx, name in enumerate(names)}, res[4 * n][0, 0]


def _units(a):
    rows = a.shape[0]
    return jnp.transpose(a.reshape(rows, N_DEV, LANES), (1, 0, 2))


def kernel(x, meta_tokens, norm_gain, w_in, conv_w, conv_b, w_rg, b_rg, w_ig, b_ig, lru_lambda, ret_norm_gain, w_out, final_norm_gain, loss_target, m_meta_tokens, m_norm_gain, m_w_in, m_conv_w, m_conv_b, m_w_rg, m_b_rg, m_w_ig, m_b_ig, m_lru_lambda, m_ret_norm_gain, m_w_out, m_final_norm_gain, v_meta_tokens, v_norm_gain, v_w_in, v_conv_w, v_conv_b, v_w_rg, v_b_rg, v_w_ig, v_b_ig, v_lru_lambda, v_ret_norm_gain, v_w_out, v_final_norm_gain):
    s_len, d = x.shape[1], x.shape[2]
    d_lru = w_rg.shape[1] * w_rg.shape[2]
    d_ret = ret_norm_gain.shape[1]
    d_qk = HEADS * QK_DIM
    tp = s_len + CHUNK
    tm = TOKEN_TILE
    assert tp % tm == 0 and d_lru == HEADS * LANES and d_ret == HEADS * LANES
    ax, ay, ac = lax.axis_index("x"), lax.axis_index("y"), lax.axis_index("c")
    c_arr = jnp.reshape(ac, (1,)).astype(jnp.int32)
    j_arr = jnp.reshape(2 * ax + ay, (1,)).astype(jnp.int32)

    small = jnp.concatenate([meta_tokens, conv_w[0], jnp.zeros((4, meta_tokens.shape[1]), F32)], axis=0)
    wg, sg = _gather_weights(w_in[0], small)
    cols = sg.shape[2]
    meta_full = jnp.transpose(sg[:, :N_META, :], (1, 0, 2)).reshape(N_META, N_CHIPS * cols)
    cw_full = jnp.transpose(sg[:, N_META:N_META + 4, :], (1, 0, 2)).reshape(4, N_CHIPS * cols)
    cw8 = jnp.concatenate([cw_full, jnp.zeros((4, cw_full.shape[1]), F32)], axis=0)

    half = QK_DIM // 2
    inv = ROPE_BASE ** (-jnp.arange(half, dtype=F32) / half)
    pos = (jnp.arange(tp) - PAD_ROWS).astype(F32)
    ang = pos[:, None] * inv[None, :]
    cos_t = jnp.tile(jnp.cos(ang), (1, LANES // half))
    sign = jnp.where((jnp.arange(LANES) % QK_DIM) < half, -1.0, 1.0).astype(F32)
    sin_t = jnp.tile(jnp.sin(ang), (1, LANES // half)) * sign[None, :]
    tables = _ret_tables()
    gain_f = final_norm_gain.reshape(1, d)

    hp, lx, lg, *qkv, rg, wo4 = _in_proj(x[0], meta_full, norm_gain, wg, cos_t, sin_t, w_out[0], tables[1], tables[2],
                                         tm, d_lru, d_qk, d_ret)
    wo = wo4.reshape(N_CHIPS * wo4.shape[1], wo4.shape[2])
    hl, y_lru, *lru_saved = _lru_fwd(lx, lg, cw8, conv_b, w_rg[0], b_rg, w_ig[0], b_ig, lru_lambda, tm)
    o, y_ret, rprev = _ret_fwd(*qkv, rg, ret_norm_gain, tables, tm)
    dh2, dy_lru, dy_ret, dwo, dgf, loss_acc = _out_proj_loss(y_lru, y_ret, hp, loss_target[0], wo, gain_f, tm)

    g_out = dwo.reshape(N_DEV, dwo.shape[0] // N_DEV, dwo.shape[1])
    (dq, dk, dv, drg, dgain), (r_out,) = _ret_bwd(*qkv, rg, o, rprev, dy_ret, ret_norm_gain, cos_t, sin_t, tables,
                                                 tm, ride=_pair_ride([g_out]))
    q_out = _pair_sum(g_out, r_out, c_arr, 128, "grad_pair_sum_out")
    (dlx, dlg, dcw, dcb, dwr, dbr, dwi, dbi, dlam), (e_out,) = _lru_bwd(
        lx, lg, hl, dy_lru, lru_saved, cw8, w_rg[0], w_ig[0], lru_lambda, tm, ride=_chip_ride([q_out]))
    f_out = _chip_sum(q_out, e_out, j_arr, 128, "grad_chip_sum_out")
    zero_row = jnp.zeros((1, d), F32)
    vecs = [zero_row, dcb, dbr, dbi, dlam, dgain, dgf]
    g_small = jnp.concatenate([dwr, dwi, jnp.zeros((N_DEV, N_META, LANES), F32), _units(dcw[0:4])]
                              + [_units(a) for a in vecs]
                              + [jnp.zeros((N_DEV, UNIT_ROWS - ROW_VEC - N_VEC, LANES), F32)], axis=1)
    dparts = [dlx, dlg, dq, dk, dv, drg]
    dwg, (s_out, r_small) = _in_proj_dw(dparts, hp, norm_gain, wg.shape,
                                        ride=_join_rides(_sibling_ride([f_out]), _pair_ride([g_small])))
    g_in = dwg.reshape(N_DEV, dwg.shape[1] // 2, dwg.shape[2])
    q_in = _pair_exchange_sum(g_in, c_arr, 128, "grad_pair_exchange_sum_in")
    q_small = _pair_sum(g_small, r_small, c_arr, UNIT_ROWS, "grad_pair_sum_small")
    (grad_x, dmeta, dg1), (e_in, e_small) = _in_proj_dx(dparts, hp, dh2, norm_gain, wg, s_len, tm,
                                                        ride=_chip_ride([q_in, q_small]))
    f_in = _chip_sum(q_in, e_in, j_arr, 128, "grad_chip_sum_in")
    f_small = _chip_sum(q_small, e_small, j_arr, UNIT_ROWS, "grad_chip_sum_small", loss_part=loss_acc)
    tail = jnp.concatenate([_units(dmeta), _units(dg1), jnp.zeros((N_DEV, TAIL_ROWS - N_META - 1, LANES), F32)],
                           axis=1).reshape(N_DEV * TAIL_ROWS, LANES)
    s_in, o_small = _finish_exchange(f_in, jnp.concatenate([f_small, tail], axis=0))

    res_in = _adamw_big(w_in[0], f_in, s_in, m_w_in[0], v_w_in[0], c_arr, 256, "adamw_w_in")
    res_out = _adamw_big(w_out[0], f_out, s_out, m_w_out[0], v_w_out[0], c_arr, 256, "adamw_w_out")
    small_params = {
        "meta_tokens": (meta_tokens, m_meta_tokens, v_meta_tokens),
        "norm_gain": (norm_gain, m_norm_gain, v_norm_gain),
        "conv_w": (conv_w[0], m_conv_w[0], v_conv_w[0]),
        "conv_b": (conv_b, m_conv_b, v_conv_b),
        "w_rg": (w_rg[0], m_w_rg[0], v_w_rg[0]),
        "b_rg": (b_rg, m_b_rg, v_b_rg),
        "w_ig": (w_ig[0], m_w_ig[0], v_w_ig[0]),
        "b_ig": (b_ig, m_b_ig, v_b_ig),
        "lru_lambda": (lru_lambda, m_lru_lambda, v_lru_lambda),
        "ret_norm_gain": (ret_norm_gain, m_ret_norm_gain, v_ret_norm_gain),
        "final_norm_gain": (gain_f, m_final_norm_gain.reshape(1, d), v_final_norm_gain.reshape(1, d)),
    }
    res, loss = _adamw_small(j_arr, o_small, small_params)
    res["w_in"] = tuple(res_in)
    res["w_out"] = tuple(res_out)

    order = ["meta_tokens", "norm_gain", "w_in", "conv_w", "conv_b", "w_rg", "b_rg", "w_ig", "b_ig", "lru_lambda",
             "ret_norm_gain", "w_out", "final_norm_gain"]
    shapes = {"w_in": w_in.shape, "conv_w": conv_w.shape, "w_rg": w_rg.shape, "w_ig": w_ig.shape,
              "w_out": w_out.shape, "final_norm_gain": final_norm_gain.shape}
    outs = [loss, grad_x.reshape(x.shape)]
    for kind in range(4):
        for name in order:
            a = res[name][kind]
            outs.append(a.reshape(shapes[name]) if name in shapes else a)
    return tuple(outs)
```

```python
import functools

import jax
import jax.numpy as jnp
from jax import lax
from jax.experimental import pallas as pl
from jax.experimental.pallas import tpu as pltpu

F32 = jnp.float32
BF16 = jnp.bfloat16

N_META = 16
CHUNK = 128
PAD_ROWS = CHUNK - N_META
HEADS = 8
QK_DIM = 64
LANES = 128
SUBLANES = 8
LRU_C = 8.0
EPS = 1e-6
ROPE_BASE = 10000.0
ADAM_LR = 0.001
ADAM_B1 = 0.9
ADAM_B2 = 0.999
ADAM_EPS = 1e-08
ADAM_WD = 0.01
ADAM_STEP = 10
N_CHIPS = 4
N_DEV = 8
TOKEN_TILE = 384
VMEM_LIMIT = 58 * 1024 * 1024
MESH = pl.DeviceIdType.MESH

VMEM_SPEC = pl.BlockSpec(memory_space=pltpu.VMEM)
SMEM_SPEC = pl.BlockSpec(memory_space=pltpu.SMEM)
ANY_SPEC = pl.BlockSpec(memory_space=pl.ANY)

ROW_WR, ROW_WI, ROW_META, ROW_CONV, ROW_VEC, UNIT_ROWS = 0, 128, 256, 272, 276, 288
N_VEC = 7
ROW_LOSS = ROW_VEC + N_VEC
TAIL_ROWS = 24


def _dot(a, b):
    return jnp.dot(a, b, preferred_element_type=F32)


def _dot_nt(a, b):
    return lax.dot_general(a, b, (((1,), (1,)), ((), ())), preferred_element_type=F32)


def _dot_tn(a, b):
    return lax.dot_general(a, b, (((0,), (0,)), ((), ())), preferred_element_type=F32)


def _sigmoid(x):
    return 0.5 * jnp.tanh(0.5 * x) + 0.5


def _shift_down(x, prev8, s):
    rolled = pltpu.roll(x, s, 0)
    rows = lax.broadcasted_iota(jnp.int32, (SUBLANES, x.shape[1]), 0)
    top = jnp.where(rows < s, pltpu.roll(prev8, s, 0), rolled[0:SUBLANES])
    return jnp.concatenate([top, rolled[SUBLANES:]], axis=0)


def _shift_up(x, next8, s):
    n = x.shape[0]
    rolled = pltpu.roll(x, n - s, 0)
    rows = lax.broadcasted_iota(jnp.int32, (SUBLANES, x.shape[1]), 0)
    bot = jnp.where(rows >= SUBLANES - s, pltpu.roll(next8, SUBLANES - s, 0), rolled[n - SUBLANES:n])
    return jnp.concatenate([rolled[:n - SUBLANES], bot], axis=0)


def _rot_partner(t):
    w = t.shape[1]
    lane = lax.broadcasted_iota(jnp.int32, t.shape, 1)
    first = (lane % QK_DIM) < (QK_DIM // 2)
    return jnp.where(first, pltpu.roll(t, w - QK_DIM // 2, 1), pltpu.roll(t, QK_DIM // 2, 1))


def _tile_lanes(t, reps):
    return jnp.concatenate([t] * reps, axis=1)


class _Ride:
    def __init__(self, srcs, dst_shapes, n_copies, make):
        self.srcs, self.dst_shapes, self.n_copies, self.make = list(srcs), list(dst_shapes), n_copies, make


def _join_rides(a, b):
    def make(src, dst, send_sems, recv_sems, base):
        na, da = len(a.srcs), len(a.dst_shapes)
        return (a.make(src[:na], dst[:da], send_sems, recv_sems, base)
                + b.make(src[na:], dst[da:], send_sems, recv_sems, base + a.n_copies))

    return _Ride(a.srcs + b.srcs, a.dst_shapes + b.dst_shapes, a.n_copies + b.n_copies, make)


def _position():
    x, y, c = lax.axis_index("x"), lax.axis_index("y"), lax.axis_index("c")
    return x, y, c, [(1 - x, y), (x, 1 - y), (1 - x, 1 - y)]


def _remote(src, dst, send_sems, recv_sems, k, to):
    return pltpu.make_async_remote_copy(src_ref=src, dst_ref=dst, send_sem=send_sems.at[k], recv_sem=recv_sems.at[k],
                                        device_id=to, device_id_type=MESH)


def _pair_ride(bufs):
    def make(src, dst, send_sems, recv_sems, base):
        x, y, c, _ = _position()
        return [_remote(src[b].at[2 * jj + 1 - c], dst[b].at[jj], send_sems, recv_sems, base + b * N_CHIPS + jj,
                        (x, y, 1 - c)) for b in range(len(bufs)) for jj in range(N_CHIPS)]

    shapes = [jax.ShapeDtypeStruct((N_CHIPS,) + b.shape[1:], b.dtype) for b in bufs]
    return _Ride(bufs, shapes, N_CHIPS * len(bufs), make)


def _chip_ride(bufs):
    def make(src, dst, send_sems, recv_sems, base):
        x, y, c, chips = _position()
        return [_remote(src[b].at[2 * cx + cy], dst[b].at[2 * x + y], send_sems, recv_sems, base + b * 3 + k,
                        (cx, cy, c)) for b in range(len(bufs)) for k, (cx, cy) in enumerate(chips)]

    shapes = [jax.ShapeDtypeStruct(b.shape, b.dtype) for b in bufs]
    return _Ride(bufs, shapes, 3 * len(bufs), make)


def _sibling_ride(bufs):
    def make(src, dst, send_sems, recv_sems, base):
        x, y, c, _ = _position()
        return [_remote(src[b], dst[b], send_sems, recv_sems, base + b, (x, y, 1 - c)) for b in range(len(bufs))]

    shapes = [jax.ShapeDtypeStruct(b.shape, b.dtype) for b in bufs]
    return _Ride(bufs, shapes, len(bufs), make)


def _hosted_call(body, ride, n_steps, *, name, in_specs, out_specs, out_shape, scratch_shapes, args):
    params = pltpu.CompilerParams(dimension_semantics=("arbitrary",), vmem_limit_bytes=VMEM_LIMIT)
    if ride is None:
        res = pl.pallas_call(body, name=name, grid=(n_steps,), in_specs=list(in_specs), out_specs=list(out_specs),
                             out_shape=list(out_shape), scratch_shapes=list(scratch_shapes),
                             compiler_params=params)(*args)
        return list(res), []
    sizes = [len(in_specs), len(ride.srcs), len(out_specs), len(ride.dst_shapes), len(scratch_shapes), 2]

    def hosted(*refs):
        groups, pos = [], 0
        for n in sizes:
            groups.append(refs[pos:pos + n])
            pos += n
        ins, rin, outs, rout, scr, (send_sems, recv_sems) = groups
        i = pl.program_id(0)

        @pl.when(i == 0)
        def _():
            for cp in ride.make(rin, rout, send_sems, recv_sems, 0):
                cp.start()

        body(*ins, *outs, *scr)

        @pl.when(i == n_steps - 1)
        def _():
            for cp in ride.make(rin, rout, send_sems, recv_sems, 0):
                cp.wait()

    n_out = len(out_specs)
    res = pl.pallas_call(
        hosted,
        name=name,
        grid=(n_steps,),
        in_specs=list(in_specs) + [ANY_SPEC] * len(ride.srcs),
        out_specs=list(out_specs) + [ANY_SPEC] * len(ride.dst_shapes),
        out_shape=list(out_shape) + ride.dst_shapes,
        scratch_shapes=list(scratch_shapes) + [pltpu.SemaphoreType.DMA((ride.n_copies,)),
                                               pltpu.SemaphoreType.DMA((ride.n_copies,))],
        compiler_params=params,
    )(*args, *ride.srcs)
    return list(res[:n_out]), list(res[n_out:])


def _gather_weights(w_in, small):
    r_in, c_in = w_in.shape
    h_in = r_in // 2
    q_in = h_in // 2

    def body(win_ref, small_ref, wg_ref, sg_ref, send_sems, recv_sems):
        x, y, c, chips = _position()
        j = 2 * x + y
        sibling = (x, y, 1 - c)
        xn, yn, dg = chips
        jx, jy, jd = (2 * cx + cy for cx, cy in chips)

        def half(jj, cc):
            return wg_ref.at[jj, pl.ds(cc * h_in, h_in), :]

        def quarter(jj, qq):
            return wg_ref.at[jj, pl.ds(c * h_in + qq * q_in, q_in), :]

        def copy(k, ref, to):
            return _remote(ref, ref, send_sems, recv_sems, k, to)

        def cast_rows(start, rows):
            start = pl.multiple_of(start, q_in)
            wg_ref[j, pl.ds(start, rows), :] = win_ref[pl.ds(start, rows), :].astype(BF16)

        first = [copy(0, quarter(j, 0), (*xn, c)), copy(2, quarter(j, 1), (*yn, c)),
                 copy(1, quarter(j, 1), (*xn, c)), copy(3, quarter(j, 0), (*yn, c))]
        sg_ref[j] = small_ref[...]
        cast_rows(c * h_in, q_in)
        first[0].start()
        cast_rows(c * h_in + q_in, q_in)
        for cp in first[1:]:
            cp.start()
        small_copies = [copy(9 + k, sg_ref.at[j], (cx, cy, c)) for k, (cx, cy) in enumerate(chips)]
        for cp in small_copies:
            cp.start()
        first += small_copies
        cast_rows((1 - c) * h_in, h_in)
        copy(0, quarter(jx, 0), sibling).wait_recv()
        along_y = copy(4, quarter(jx, 0), (*yn, c))
        along_y.start()
        copy(2, quarter(jy, 1), sibling).wait_recv()
        along_x = copy(5, quarter(jy, 1), (*xn, c))
        along_x.start()
        copy(1, quarter(jx, 1), sibling).wait_recv()
        to_sib = [copy(6, half(jx, c), sibling)]
        to_sib[-1].start()
        copy(3, quarter(jy, 0), sibling).wait_recv()
        to_sib.append(copy(7, half(jy, c), sibling))
        to_sib[-1].start()
        copy(4, quarter(jd, 0), sibling).wait_recv()
        copy(5, quarter(jd, 1), sibling).wait_recv()
        to_sib.append(copy(8, half(jd, c), sibling))
        to_sib[-1].start()
        for k, jk in enumerate((jx, jy, jd)):
            copy(6 + k, half(jk, 1 - c), sibling).wait_recv()
            copy(9 + k, sg_ref.at[jk], sibling).wait_recv()
        for cp in first + [along_y, along_x] + to_sib:
            cp.wait_send()

    return pl.pallas_call(
        body,
        name="gather_weights",
        out_shape=(jax.ShapeDtypeStruct((N_CHIPS, r_in, c_in), BF16),
                   jax.ShapeDtypeStruct((N_CHIPS,) + small.shape, F32)),
        in_specs=[VMEM_SPEC, VMEM_SPEC],
        out_specs=(VMEM_SPEC, VMEM_SPEC),
        scratch_shapes=[pltpu.SemaphoreType.DMA((12,)), pltpu.SemaphoreType.DMA((12,))],
        compiler_params=pltpu.CompilerParams(vmem_limit_bytes=VMEM_LIMIT),
    )(w_in, small)


def _proj_segments(d_lru, d_qk, d_ret, chunk_w):
    widths = [d_lru, d_lru, d_qk, d_qk, d_ret, d_ret]
    segs, col = [], 0
    for w in widths:
        parts, off = [], 0
        while off < w:
            jj, inner = divmod(col + off, chunk_w)
            take = min(w - off, chunk_w - inner)
            parts.append((jj, inner, off, take))
            off += take
        segs.append(parts)
        col += w
    return segs


def _in_proj(x2, meta, gain, wg, cos_t, sin_t, w_out, qdec, kdec, tm, d_lru, d_qk, d_ret):
    s_len, d = x2.shape
    tp = s_len + CHUNK
    nt, nb = tp // tm, tm // CHUNK
    segs = _proj_segments(d_lru, d_qk, d_ret, wg.shape[2])
    outs = [(d, F32), (d_lru, F32), (d_lru, F32)] + [(d_qk, BF16)] * 4 + [(d_ret, BF16), (d_ret, F32)]
    r_out, c_out = w_out.shape
    h_out = r_out // 2
    fwd_step = min(6, nt - 1)

    def gather_w_out(i, wout_ref, wo_ref, wob, send_sems, recv_sems, local_sem):
        x, y, c, chips = _position()
        j = 2 * x + y
        sibling = (x, y, 1 - c)

        def half(jj, cc):
            return wo_ref.at[jj, pl.ds(cc * h_out, h_out), :]

        local = pltpu.make_async_copy(wob, wo_ref.at[j], local_sem)
        first = [_remote(wob.at[pl.ds(c * h_out, h_out), :], half(j, c), send_sems, recv_sems, k, (cx, cy, c))
                 for k, (cx, cy) in enumerate(chips)]
        passed = [_remote(half(2 * cx + cy, c), half(2 * cx + cy, c), send_sems, recv_sems, 3 + k, sibling)
                  for k, (cx, cy) in enumerate(chips)]

        @pl.when(i == 0)
        def _():
            wob[...] = wout_ref[...].astype(BF16)
            local.start()
            for cp in first:
                cp.start()

        @pl.when(i == fwd_step)
        def _():
            for k, (cx, cy) in enumerate(chips):
                _remote(half(2 * cx + cy, c), half(2 * cx + cy, c), send_sems, recv_sems, k, sibling).wait_recv()
                passed[k].start()

        @pl.when(i == nt - 1)
        def _():
            for k, (cx, cy) in enumerate(chips):
                jk = 2 * cx + cy
                _remote(half(jk, 1 - c), half(jk, 1 - c), send_sems, recv_sems, 3 + k, sibling).wait_recv()
            for cp in first + passed:
                cp.wait_send()
            local.wait()

    def body(*refs):
        xb = refs[:nb]
        meta_ref, g_ref, w_ref, cos_ref, sin_ref, wout_ref, qdec_ref, kdec_ref = refs[nb:nb + 8]
        hp_ref, lx_ref, lg_ref, qb_ref, kb_ref, qd_ref, kd_ref, vb_ref, rg_ref = refs[nb + 8:nb + 17]
        wo_ref, q_s, k_s, wob, send_sems, recv_sems, local_sem = refs[nb + 17:]
        i = pl.program_id(0)
        gather_w_out(i, wout_ref, wo_ref, wob, send_sems, recv_sems, local_sem)
        blocks = [r[...] for r in xb]
        head = jnp.concatenate([jnp.zeros((PAD_ROWS, d), F32), meta_ref[...]], axis=0)
        blocks[0] = jnp.where(i == 0, head, blocks[0])
        h = jnp.concatenate(blocks, axis=0)
        hp_ref[...] = h
        rinv = lax.rsqrt(jnp.mean(h * h, axis=-1, keepdims=True) + EPS)
        u = ((h * rinv) * g_ref[...]).astype(BF16)
        for out_ref, parts in zip([lx_ref, lg_ref, q_s, k_s, vb_ref, rg_ref], segs):
            for jj, inner, off, take in parts:
                out_ref[:, off:off + take] = _dot(u, w_ref[jj, :, inner:inner + take]).astype(out_ref.dtype)
        cos = _tile_lanes(cos_ref[...], d_qk // LANES)
        sin = _tile_lanes(sin_ref[...], d_qk // LANES)
        q = q_s[...]
        q = q * cos + _rot_partner(q) * sin
        k = k_s[...]
        k = (k * cos + _rot_partner(k) * sin) * (QK_DIM ** -0.5)
        qb_ref[...] = q.astype(BF16)
        kb_ref[...] = k.astype(BF16)
        qd_ref[...] = (q * jnp.concatenate([qdec_ref[...]] * nb, axis=0)).astype(BF16)
        kd_ref[...] = (k * jnp.concatenate([kdec_ref[...]] * nb, axis=0)).astype(BF16)

    x_specs = [pl.BlockSpec((CHUNK, d), functools.partial(lambda i, b: (jnp.maximum(i * nb + b - 1, 0), 0), b=b))
               for b in range(nb)]
    tile = lambda w: pl.BlockSpec((tm, w), lambda i: (i, 0))
    return pl.pallas_call(
        body,
        name="in_proj",
        grid=(nt,),
        in_specs=x_specs + [pl.BlockSpec(meta.shape, lambda i: (0, 0)),
                            pl.BlockSpec(gain.shape, lambda i: (0, 0)),
                            pl.BlockSpec(wg.shape, lambda i: (0, 0, 0)),
                            tile(LANES), tile(LANES),
                            pl.BlockSpec(w_out.shape, lambda i: (0, 0)),
                            pl.BlockSpec(qdec.shape, lambda i: (0, 0)), pl.BlockSpec(kdec.shape, lambda i: (0, 0))],
        out_specs=[tile(w) for w, _ in outs] + [ANY_SPEC],
        out_shape=[jax.ShapeDtypeStruct((tp, w), dt) for w, dt in outs]
                  + [jax.ShapeDtypeStruct((N_CHIPS, r_out, c_out), BF16)],
        scratch_shapes=[pltpu.VMEM((tm, d_qk), F32), pltpu.VMEM((tm, d_qk), F32),
                        pltpu.VMEM((r_out, c_out), BF16), pltpu.SemaphoreType.DMA((6,)),
                        pltpu.SemaphoreType.DMA((6,)), pltpu.SemaphoreType.DMA],
        compiler_params=pltpu.CompilerParams(dimension_semantics=("arbitrary",), vmem_limit_bytes=VMEM_LIMIT),
    )(*([x2] * nb), meta, gain, wg, cos_t, sin_t, w_out, qdec, kdec)


def _segment_scan(a3, u3, out3, p3, carry, tm, reverse):
    groups = a3.shape[0]
    seg = tm // SUBLANES

    def step(j, state):
        hs, ps = state
        rows = pl.ds((seg - 1 - j) if reverse else j, SUBLANES, stride=seg)
        new_h, new_p = [], []
        for g in range(groups):
            a = a3[g, rows, :]
            h = a * hs[g] + u3[g, rows, :]
            p = ps[g] * a
            out3[g, rows, :] = h
            p3[g, rows, :] = p
            new_h.append(h)
            new_p.append(p)
        return tuple(new_h), tuple(new_p)

    zeros = tuple(jnp.zeros((SUBLANES, LANES), F32) for _ in range(groups))
    ones = tuple(jnp.ones((SUBLANES, LANES), F32) for _ in range(groups))
    lax.fori_loop(0, seg, step, (zeros, ones))
    carries = [carry[:, g * LANES:(g + 1) * LANES] for g in range(groups)]
    for s in (reversed(range(SUBLANES)) if reverse else range(SUBLANES)):
        rows = slice(s * seg, (s + 1) * seg)
        edge = s * seg if reverse else (s + 1) * seg - 1
        for g in range(groups):
            out3[g, rows, :] = out3[g, rows, :] + p3[g, rows, :] * carries[g]
            carries[g] = out3[g, edge:edge + 1, :]
    return jnp.concatenate(carries, axis=1)


def _softplus_neg(lam):
    z = -lam
    e = jnp.exp(-jnp.abs(z))
    e1 = 1.0 + e
    log1p_e = jnp.where(e1 == 1.0, e, jnp.log(e1) * (e / (e1 - 1.0)))
    return jnp.maximum(z, 0.0) + log1p_e


def _lru_fwd(lx, lg, cw, cb, wr, br, wi, bi, lam, tm):
    tp, w = lx.shape
    nt = tp // tm
    per8 = tm // SUBLANES
    n_heads = wr.shape[0]

    def body(lx_ref, lxp_ref, lg_ref, cw_ref, cb_ref, wr_ref, br_ref, wi_ref, bi_ref, lam_ref,
             hl_ref, y_ref, xc_ref, r_ref, ig_ref, a_ref, beta_ref, w4_ref, a_s, u_s, h_s, p_s, carry):
        i = pl.program_id(0)

        @pl.when(i == 0)
        def _():
            carry[...] = jnp.zeros_like(carry)

        sp = _softplus_neg(lam_ref[...])
        row = lax.broadcasted_iota(jnp.int32, (tm, 1), 0) + i * tm
        for hd in range(n_heads):
            hs = slice(hd * LANES, (hd + 1) * LANES)
            lxv = lx_ref[:, hs]
            prev8 = jnp.where(i == 0, 0.0, lxp_ref[:, hs])
            xc = cb_ref[:, hs] + _shift_down(lxv, prev8, 3) * cw_ref[0:1, hs]
            xc = xc + _shift_down(lxv, prev8, 2) * cw_ref[1:2, hs]
            xc = xc + _shift_down(lxv, prev8, 1) * cw_ref[2:3, hs]
            xc = xc + lxv * cw_ref[3:4, hs]
            xc_ref[:, hs] = xc
            xh = xc.astype(BF16)
            r = _sigmoid(_dot(xh, wr_ref[hd].astype(BF16)) + br_ref[:, hs])
            ig = _sigmoid(_dot(xh, wi_ref[hd].astype(BF16)) + bi_ref[:, hs])
            r_ref[:, hs] = r
            ig_ref[:, hs] = ig
            log_a = (-LRU_C * r) * sp[:, hs]
            a = jnp.exp(log_a)
            a_ref[:, hs] = a
            a2 = a * a
            beta2 = jnp.maximum((1.0 + a2) * jnp.tanh(-log_a), 1e-37)
            rsb = lax.rsqrt(beta2)
            beta = beta2 * rsb
            beta_ref[:, hs] = beta
            w4_ref[:, hs] = a2 * rsb
            a_s[hd] = a
            u_s[hd] = jnp.where(row >= PAD_ROWS, beta * ig * xc, 0.0)
        carry[0:1, :] = _segment_scan(a_s, u_s, h_s, p_s, carry[0:1, :], tm, reverse=False)
        for hd in range(n_heads):
            hs = slice(hd * LANES, (hd + 1) * LANES)
            hl = h_s[hd]
            hl_ref[:, hs] = hl
            g = lg_ref[:, hs]
            y_ref[:, hs] = (hl * (g * _sigmoid(g))).astype(BF16)

    tile = pl.BlockSpec((tm, w), lambda i: (i, 0))
    prev = pl.BlockSpec((SUBLANES, w), lambda i: (jnp.maximum(i * per8 - 1, 0), 0))
    vec = pl.BlockSpec((1, w), lambda i: (0, 0))
    mat = pl.BlockSpec(wr.shape, lambda i: (0, 0, 0))
    f32_out = jax.ShapeDtypeStruct((tp, w), F32)
    return pl.pallas_call(
        body,
        name="lru_fwd",
        grid=(nt,),
        in_specs=[tile, prev, tile, pl.BlockSpec(cw.shape, lambda i: (0, 0)), vec, mat, vec, mat, vec, vec],
        out_specs=[tile] * 8,
        out_shape=[f32_out, jax.ShapeDtypeStruct((tp, w), BF16)] + [f32_out] * 6,
        scratch_shapes=[pltpu.VMEM((w // LANES, tm, LANES), F32)] * 4 + [pltpu.VMEM((SUBLANES, w), F32)],
        compiler_params=pltpu.CompilerParams(dimension_semantics=("arbitrary",), vmem_limit_bytes=VMEM_LIMIT),
    )(lx, lx, lg, cw, cb, wr, br, wi, bi, lam)


def _ret_tables():
    log_g = jnp.log1p(-jnp.exp2(-5.0 - jnp.arange(HEADS, dtype=F32)))
    idx = jnp.arange(CHUNK, dtype=F32)
    diff = idx[:, None] - idx[None, :]
    dmask = jnp.where(diff[None] >= 0.0, jnp.exp(jnp.maximum(diff, 0.0)[None] * log_g[:, None, None]), 0.0)
    kdec = jnp.repeat(jnp.exp((CHUNK - 1.0 - idx)[:, None] * log_g[None, :]), QK_DIM, axis=1)
    qdec = jnp.repeat(jnp.exp((idx + 1.0)[:, None] * log_g[None, :]), QK_DIM, axis=1)
    g_chunk = jnp.exp(CHUNK * log_g)
    g_rows = jnp.repeat(g_chunk, QK_DIM).reshape(HEADS // 2, 2 * QK_DIM, 1)
    g_state = jnp.broadcast_to(g_rows, (HEADS // 2, 2 * QK_DIM, 2 * LANES))
    r_head = jnp.arange(2 * QK_DIM)[:, None] // QK_DIM
    c_head = jnp.arange(2 * LANES)[None, :] // LANES
    block_diag = (r_head == c_head).astype(F32)
    return dmask, qdec, kdec, g_state, block_diag


def _head_norm(o_h):
    mu = jnp.mean(o_h, axis=-1, keepdims=True)
    oc = o_h - mu
    var = jnp.mean(oc * oc, axis=-1, keepdims=True)
    rstd = lax.rsqrt(var + EPS)
    return oc * rstd, rstd


def _ret_fwd(qb, kb, qd, kd, vb, rg, gain, tables, tm):
    tp, d_qk = qb.shape
    d_ret = vb.shape[1]
    n_ch = tp // CHUNK
    cps = tm // CHUNK
    n_pairs = HEADS // 2
    dmask, _, _, g_state, block_diag = tables

    def body(q_ref, k_ref, qd_ref, kd_ref, v_ref, rg_ref, gain_ref, dm_ref, gs_ref, bd_ref,
             o_ref, y_ref, rp_ref, state):
        n = pl.program_id(0)

        @pl.when(n == 0)
        def _():
            state[...] = jnp.zeros_like(state)

        lane = lax.broadcasted_iota(jnp.int32, (CHUNK, LANES), 1)
        for ci in range(cps):
            rs = slice(ci * CHUNK, (ci + 1) * CHUNK)
            for p in range(n_pairs):
                qs = slice(p * LANES, (p + 1) * LANES)
                vs = slice(p * 2 * LANES, (p + 1) * 2 * LANES)
                qp, kb = q_ref[rs, qs], k_ref[rs, qs]
                vb = v_ref[rs, vs]
                qd, kd = qd_ref[rs, qs], kd_ref[rs, qs]
                st = state[p]
                st_b = st.astype(BF16)
                rp_ref[ci, p] = st_b
                cross = _dot(qd, st_b)
                for e in range(2):
                    hd = 2 * p + e
                    hs = slice(hd * LANES, (hd + 1) * LANES)
                    es = slice(e * LANES, (e + 1) * LANES)
                    qm = jnp.where((lane // QK_DIM) == e, qp, jnp.zeros_like(qp))
                    s = _dot_nt(qm, kb) * dm_ref[hd]
                    o_h = _dot(s.astype(BF16), vb[:, es]) + cross[:, es]
                    o_ref[rs, hs] = o_h
                    xhat, _ = _head_norm(o_h)
                    g = rg_ref[rs, hs]
                    y_ref[rs, hs] = ((xhat * gain_ref[:, hs]) * (g * _sigmoid(g))).astype(BF16)
                state[p] = gs_ref[p] * st + bd_ref[...] * _dot_tn(kd, vb)

    ch = lambda w: pl.BlockSpec((tm, w), lambda n: (n, 0))
    const2 = lambda a: pl.BlockSpec(a.shape, lambda n: (0, 0))
    const3 = lambda a: pl.BlockSpec(a.shape, lambda n: (0, 0, 0))
    return pl.pallas_call(
        body,
        name="ret_fwd",
        grid=(n_ch // cps,),
        in_specs=[ch(d_qk)] * 4 + [ch(d_ret), ch(d_ret), const2(gain), const3(dmask), const3(g_state),
                                   const2(block_diag)],
        out_specs=[ch(d_ret), ch(d_ret),
                   pl.BlockSpec((cps, n_pairs, 2 * QK_DIM, 2 * LANES), lambda n: (n, 0, 0, 0))],
        out_shape=[jax.ShapeDtypeStruct((tp, d_ret), F32), jax.ShapeDtypeStruct((tp, d_ret), BF16),
                   jax.ShapeDtypeStruct((n_ch, n_pairs, 2 * QK_DIM, 2 * LANES), BF16)],
        scratch_shapes=[pltpu.VMEM((n_pairs, 2 * QK_DIM, 2 * LANES), F32)],
        compiler_params=pltpu.CompilerParams(dimension_semantics=("arbitrary",), vmem_limit_bytes=VMEM_LIMIT),
    )(qb, kb, qd, kd, vb, rg, gain, dmask, g_state, block_diag)


def _out_proj_loss(y_lru, y_ret, hp, tgt, wo, gain_f, tm):
    tp, d = hp.shape
    w_lru = y_lru.shape[1]
    w_mix = wo.shape[0]
    nt, nb = tp // tm, tm // CHUNK

    def body(*refs):
        yl_ref, yr_ref, hp_ref = refs[:3]
        tb = refs[3:3 + nb]
        wo_ref, gf_ref = refs[3 + nb:5 + nb]
        dh2_ref, dyl_ref, dyr_ref, dwo_ref, dgf_ref, loss_ref = refs[5 + nb:]
        i = pl.program_id(0)

        @pl.when(i == 0)
        def _():
            dwo_ref[...] = jnp.zeros_like(dwo_ref)
            dgf_ref[...] = jnp.zeros_like(dgf_ref)
            loss_ref[...] = jnp.zeros_like(loss_ref)

        yl, yr = yl_ref[...], yr_ref[...]
        h2 = hp_ref[...] + _dot(yl, wo_ref[0:w_lru, :]) + _dot(yr, wo_ref[w_lru:w_mix, :])
        rinv = lax.rsqrt(jnp.mean(h2 * h2, axis=-1, keepdims=True) + EPS)
        nrm = h2 * rinv
        gf = gf_ref[...]
        tgt_v = jnp.concatenate([r[...] for r in tb], axis=0)
        row = lax.broadcasted_iota(jnp.int32, (tm, 1), 0) + i * tm
        err = jnp.where(row >= CHUNK, nrm * gf - tgt_v, 0.0)
        loss_ref[...] += 0.5 * jnp.sum(jnp.mean(err * err, axis=-1, keepdims=True))
        dout = err * (1.0 / d)
        dgf_ref[...] += jnp.sum(dout * nrm, axis=0, keepdims=True)
        dn = dout * gf
        dh2 = rinv * (dn - nrm * jnp.mean(dn * nrm, axis=-1, keepdims=True))
        dh2_ref[...] = dh2
        dh2b = dh2.astype(BF16)
        dyl_ref[...] = _dot_nt(dh2b, wo_ref[0:w_lru, :])
        dyr_ref[...] = _dot_nt(dh2b, wo_ref[w_lru:w_mix, :])
        dwo_ref[0:w_lru, :] += _dot_tn(yl, dh2b)
        dwo_ref[w_lru:w_mix, :] += _dot_tn(yr, dh2b)

    tile = lambda w: pl.BlockSpec((tm, w), lambda i: (i, 0))
    t_specs = [pl.BlockSpec((CHUNK, d), functools.partial(lambda i, b: (jnp.maximum(i * nb + b - 1, 0), 0), b=b))
               for b in range(nb)]
    return pl.pallas_call(
        body,
        name="out_proj_loss",
        grid=(nt,),
        in_specs=[tile(w_lru), tile(w_mix - w_lru), tile(d)] + t_specs +
                 [pl.BlockSpec(wo.shape, lambda i: (0, 0)), pl.BlockSpec(gain_f.shape, lambda i: (0, 0))],
        out_specs=[tile(d), tile(w_lru), tile(w_mix - w_lru), pl.BlockSpec(wo.shape, lambda i: (0, 0)),
                   pl.BlockSpec((1, d), lambda i: (0, 0)), pl.BlockSpec((SUBLANES, LANES), lambda i: (0, 0))],
        out_shape=[jax.ShapeDtypeStruct((tp, d), F32), jax.ShapeDtypeStruct((tp, w_lru), F32),
                   jax.ShapeDtypeStruct((tp, w_mix - w_lru), F32), jax.ShapeDtypeStruct(wo.shape, F32),
                   jax.ShapeDtypeStruct((1, d), F32), jax.ShapeDtypeStruct((SUBLANES, LANES), F32)],
        compiler_params=pltpu.CompilerParams(dimension_semantics=("arbitrary",), vmem_limit_bytes=VMEM_LIMIT),
    )(y_lru, y_ret, hp, *([tgt] * nb), wo, gain_f)


def _ret_bwd(qb, kb, qd, kd, vb, rg, o, rprev, dy, gain, cos_t, sin_t, tables, tm, ride=None):
    tp, d_qk = qb.shape
    d_ret = vb.shape[1]
    n_ch = tp // CHUNK
    cps = tm // CHUNK
    n_pairs = HEADS // 2
    dmask, qdec, kdec, g_state, block_diag = tables

    dmask_t = jnp.swapaxes(dmask, 1, 2)

    def body(q_ref, k_ref, qdb_ref, kdb_ref, v_ref, rg_ref, o_ref, rp_ref, dy_ref, gain_ref, cos_ref, sin_ref,
             dm_ref, dmt_ref, qd_ref, kd_ref, gs_ref, bd_ref, dq_ref, dk_ref, dv_ref, drg_ref, dgain_ref, dstate):
        n = pl.program_id(0)

        @pl.when(n == 0)
        def _():
            dstate[...] = jnp.zeros_like(dstate)
            dgain_ref[...] = jnp.zeros_like(dgain_ref)

        lane = lax.broadcasted_iota(jnp.int32, (CHUNK, LANES), 1)
        for ci in reversed(range(cps)):
            rs = slice(ci * CHUNK, (ci + 1) * CHUNK)
            dq_parts, dk_parts = [], []
            for p in range(n_pairs):
                qs = slice(p * LANES, (p + 1) * LANES)
                vs = slice(p * 2 * LANES, (p + 1) * 2 * LANES)
                do_parts = []
                for e in range(2):
                    hd = 2 * p + e
                    hs = slice(hd * LANES, (hd + 1) * LANES)
                    xhat, rstd = _head_norm(o_ref[rs, hs])
                    g = rg_ref[rs, hs]
                    sg = _sigmoid(g)
                    dyh = dy_ref[rs, hs]
                    gn = gain_ref[:, hs]
                    d_on = dyh * (g * sg)
                    drg_ref[rs, hs] = (dyh * (xhat * gn) * (sg * (1.0 + g * (1.0 - sg)))).astype(BF16)
                    dgain_ref[:, hs] += jnp.sum(d_on * xhat, axis=0, keepdims=True)
                    dxh = d_on * gn
                    do_parts.append(rstd * (dxh - jnp.mean(dxh, axis=-1, keepdims=True)
                                            - xhat * jnp.mean(dxh * xhat, axis=-1, keepdims=True)))
                do_b = jnp.concatenate(do_parts, axis=1).astype(BF16)
                qp, kb = q_ref[rs, qs], k_ref[rs, qs]
                vb = v_ref[rs, vs]
                qd, kd = qdb_ref[rs, qs], kdb_ref[rs, qs]
                dst = dstate[p]
                dst_b = dst.astype(BF16)
                dqp = _dot_nt(do_b, rp_ref[ci, p]) * qd_ref[:, qs]
                dkp = _dot_nt(vb, dst_b) * kd_ref[:, qs]
                dvp = _dot(kd, dst_b)
                dv_parts = []
                for e in range(2):
                    hd = 2 * p + e
                    es = slice(e * LANES, (e + 1) * LANES)
                    mine = (lane // QK_DIM) == e
                    qm = jnp.where(mine, qp, jnp.zeros_like(qp))
                    km = jnp.where(mine, kb, jnp.zeros_like(kb))
                    ds = (_dot_nt(do_b[:, es], vb[:, es]) * dm_ref[hd]).astype(BF16)
                    s_t = (_dot_nt(kb, qm) * dmt_ref[hd]).astype(BF16)
                    ds_t = (_dot_nt(vb[:, es], do_b[:, es]) * dmt_ref[hd]).astype(BF16)
                    dv_parts.append(dvp[:, es] + _dot(s_t, do_b[:, es]))
                    dqp = dqp + _dot(ds, km)
                    dkp = dkp + _dot(ds_t, qm)
                dv_ref[rs, vs] = jnp.concatenate(dv_parts, axis=1).astype(BF16)
                dstate[p] = gs_ref[p] * dst + bd_ref[...] * _dot_tn(qd, do_b)
                dq_parts.append(dqp)
                dk_parts.append(dkp)
            cos = _tile_lanes(cos_ref[rs, :], d_qk // LANES)
            sin = _tile_lanes(sin_ref[rs, :], d_qk // LANES)
            dq = jnp.concatenate(dq_parts, axis=1)
            dk = jnp.concatenate(dk_parts, axis=1) * (QK_DIM ** -0.5)
            dq_ref[rs, :] = (dq * cos + _rot_partner(dq * sin)).astype(BF16)
            dk_ref[rs, :] = (dk * cos + _rot_partner(dk * sin)).astype(BF16)

    last = n_ch // cps - 1
    ch = lambda w: pl.BlockSpec((tm, w), lambda n: (last - n, 0))
    const2 = lambda a: pl.BlockSpec(a.shape, lambda n: (0, 0))
    const3 = lambda a: pl.BlockSpec(a.shape, lambda n: (0, 0, 0))
    return _hosted_call(
        body, ride, n_ch // cps,
        name="ret_bwd",
        in_specs=[ch(d_qk)] * 4 + [ch(d_ret), ch(d_ret), ch(d_ret),
                  pl.BlockSpec((cps, n_pairs, 2 * QK_DIM, 2 * LANES), lambda n: (last - n, 0, 0, 0)),
                  ch(d_ret), const2(gain), ch(LANES), ch(LANES),
                  const3(dmask), const3(dmask_t), const2(qdec), const2(kdec), const3(g_state), const2(block_diag)],
        out_specs=[ch(d_qk), ch(d_qk), ch(d_ret), ch(d_ret), pl.BlockSpec((1, d_ret), lambda n: (0, 0))],
        out_shape=[jax.ShapeDtypeStruct((tp, d_qk), BF16), jax.ShapeDtypeStruct((tp, d_qk), BF16),
                   jax.ShapeDtypeStruct((tp, d_ret), BF16), jax.ShapeDtypeStruct((tp, d_ret), BF16),
                   jax.ShapeDtypeStruct((1, d_ret), F32)],
        scratch_shapes=[pltpu.VMEM((n_pairs, 2 * QK_DIM, 2 * LANES), F32)],
        args=(qb, kb, qd, kd, vb, rg, o, rprev, dy, gain, cos_t, sin_t, dmask, dmask_t, qdec, kdec, g_state,
              block_diag),
    )


def _lru_bwd(lx, lg, hl, dy, saved, cw, wr, wi, lam, tm, ride=None):
    tp, w = lx.shape
    nt = tp // tm
    per8 = tm // SUBLANES
    n_heads = wr.shape[0]

    def body(lx_ref, lg_ref, hl_ref, hlp_ref, dy_ref, xc_ref, r_ref, ig_ref, a_ref, beta_ref, w4_ref,
             cw_ref, wr_ref, wi_ref, lam_ref,
             dlx_ref, dlg_ref, dcw_ref, dcb_ref, dwr_ref, dbr_ref, dwi_ref, dbi_ref, dlam_ref,
             g_s, b_s, carry, dxc_next, a_next):
        i = pl.program_id(0)
        first_tile = i == nt - 1

        @pl.when(i == 0)
        def _():
            carry[...] = jnp.zeros_like(carry)
            dxc_next[...] = jnp.zeros_like(dxc_next)
            a_next[...] = jnp.zeros_like(a_next)
            for r in (dcw_ref, dcb_ref, dwr_ref, dbr_ref, dwi_ref, dbi_ref, dlam_ref):
                r[...] = jnp.zeros_like(r)

        heads = [slice(hd * LANES, (hd + 1) * LANES) for hd in range(n_heads)]
        for hd, hs in enumerate(heads):
            g = lg_ref[:, hs]
            sg = _sigmoid(g)
            dyv = dy_ref[:, hs]
            dlg_ref[:, hs] = (dyv * hl_ref[:, hs] * (sg * (1.0 + g * (1.0 - sg)))).astype(BF16)
            g_s[hd] = dyv * (g * sg)
            b_s[hd] = _shift_up(a_ref[:, hs], a_next[:, hs], 1)
        carry[0:1, :] = _segment_scan(b_s, g_s, g_s, b_s, carry[0:1, :], tm, reverse=True)
        a_next[...] = a_ref[0:SUBLANES, :]
        row = lax.broadcasted_iota(jnp.int32, (tm, 1), 0) + (nt - 1 - i) * tm
        lam_v = lam_ref[...]
        dlam_scale = LRU_C * _sigmoid(-lam_v)
        dr_scale = -LRU_C * _softplus_neg(lam_v)
        for hd, hs in enumerate(heads):
            a, beta, r, ig, xc = a_ref[:, hs], beta_ref[:, hs], r_ref[:, hs], ig_ref[:, hs], xc_ref[:, hs]
            dh = g_s[hd]
            hprev = _shift_down(hl_ref[:, hs], jnp.where(first_tile, 0.0, hlp_ref[:, hs]), 1)
            du = jnp.where(row >= PAD_ROWS, dh, 0.0)
            dbeta = du * ig * xc
            d_ig = du * beta * xc
            dxc = du * beta * ig
            dloga = (dh * hprev) * a - dbeta * w4_ref[:, hs]
            dlam_ref[:, hs] += jnp.sum(dloga * r, axis=0, keepdims=True) * dlam_scale[:, hs]
            dpr = (dloga * dr_scale[:, hs]) * r * (1.0 - r)
            dpi = d_ig * ig * (1.0 - ig)
            dbr_ref[:, hs] += jnp.sum(dpr, axis=0, keepdims=True)
            dbi_ref[:, hs] += jnp.sum(dpi, axis=0, keepdims=True)
            xh, dprh, dpih = xc.astype(BF16), dpr.astype(BF16), dpi.astype(BF16)
            dwr_ref[hd] += _dot_tn(xh, dprh)
            dwi_ref[hd] += _dot_tn(xh, dpih)
            dxc = dxc + _dot_nt(dprh, wr_ref[hd].astype(BF16)) + _dot_nt(dpih, wi_ref[hd].astype(BF16))
            nxt = dxc_next[:, hs]
            up1, up2, up3 = _shift_up(dxc, nxt, 1), _shift_up(dxc, nxt, 2), _shift_up(dxc, nxt, 3)
            dlx = dxc * cw_ref[3:4, hs]
            dlx = dlx + up1 * cw_ref[2:3, hs]
            dlx = dlx + up2 * cw_ref[1:2, hs]
            dlx = dlx + up3 * cw_ref[0:1, hs]
            dlx_ref[:, hs] = dlx.astype(BF16)
            dxc_next[:, hs] = dxc[0:SUBLANES]
            lxv = lx_ref[:, hs]
            dcb_ref[:, hs] += jnp.sum(dxc, axis=0, keepdims=True)
            dcw_ref[0:1, hs] += jnp.sum(up3 * lxv, axis=0, keepdims=True)
            dcw_ref[1:2, hs] += jnp.sum(up2 * lxv, axis=0, keepdims=True)
            dcw_ref[2:3, hs] += jnp.sum(up1 * lxv, axis=0, keepdims=True)
            dcw_ref[3:4, hs] += jnp.sum(dxc * lxv, axis=0, keepdims=True)

    last = nt - 1
    tile = pl.BlockSpec((tm, w), lambda i: (last - i, 0))
    prev = pl.BlockSpec((SUBLANES, w), lambda i: (jnp.maximum((last - i) * per8 - 1, 0), 0))
    vec = pl.BlockSpec((1, w), lambda i: (0, 0))
    mat = pl.BlockSpec(wr.shape, lambda i: (0, 0, 0))
    cwb = pl.BlockSpec(cw.shape, lambda i: (0, 0))
    return _hosted_call(
        body, ride, nt,
        name="lru_bwd",
        in_specs=[tile, tile, tile, prev, tile] + [tile] * 6 + [cwb, mat, mat, vec],
        out_specs=[tile, tile, cwb, vec, mat, vec, mat, vec, vec],
        out_shape=[jax.ShapeDtypeStruct((tp, w), BF16), jax.ShapeDtypeStruct((tp, w), BF16),
                   jax.ShapeDtypeStruct(cw.shape, F32), jax.ShapeDtypeStruct((1, w), F32),
                   jax.ShapeDtypeStruct(wr.shape, F32), jax.ShapeDtypeStruct((1, w), F32),
                   jax.ShapeDtypeStruct(wr.shape, F32), jax.ShapeDtypeStruct((1, w), F32),
                   jax.ShapeDtypeStruct((1, w), F32)],
        scratch_shapes=[pltpu.VMEM((w // LANES, tm, LANES), F32)] * 2 + [pltpu.VMEM((SUBLANES, w), F32)] * 3,
        args=(lx, lg, hl, hl, dy, *saved, cw, wr, wi, lam),
    )


def _in_proj_dw(dparts, hp, gain, wg_shape, ride=None):
    tp, d = hp.shape
    n_ch = tp // CHUNK
    per = next(p for p in (4, 2, 5, 3, 1) if (n_ch - 1) % p == 0)
    n_steps = 1 + (n_ch - 1) // per
    widths = [p.shape[1] for p in dparts]
    segs = _proj_segments(widths[0], widths[2], widths[4], wg_shape[2])

    def body(*refs):
        dp = [refs[p * per:(p + 1) * per] for p in range(6)]
        hp_b = refs[6 * per:7 * per]
        g_ref, dwg_ref, acc, sem = refs[7 * per:]
        i = pl.program_id(0)

        def accumulate(blocks):
            h = jnp.concatenate([hp_b[b][...] for b in blocks], axis=0)
            rinv = lax.rsqrt(jnp.mean(h * h, axis=-1, keepdims=True) + EPS)
            u = ((h * rinv) * g_ref[...]).astype(BF16)
            for p_refs, parts in zip(dp, segs):
                for jj, inner, off, take in parts:
                    seg = jnp.concatenate([p_refs[b][:, off:off + take] for b in blocks], axis=0)
                    acc[jj, :, inner:inner + take] += _dot_tn(u, seg)

        @pl.when(i == 0)
        def _():
            acc[...] = jnp.zeros_like(acc)
            accumulate([0])

        @pl.when(i > 0)
        def _():
            accumulate(list(range(per)))

        @pl.when(i == n_steps - 1)
        def _():
            cp = pltpu.make_async_copy(acc, dwg_ref, sem)
            cp.start()
            cp.wait()

    def blocks(w):
        return [pl.BlockSpec((CHUNK, w), functools.partial(
            lambda i, b: (jnp.where(i == 0, b, per * (i - 1) + 1 + b), 0), b=b)) for b in range(per)]

    in_specs, args = [], []
    for a, w in list(zip(dparts, widths)) + [(hp, d)]:
        in_specs += blocks(w)
        args += [a] * per
    outs, rides = _hosted_call(
        body, ride, n_steps,
        name="in_proj_dw",
        in_specs=in_specs + [pl.BlockSpec(gain.shape, lambda i: (0, 0))],
        out_specs=[ANY_SPEC],
        out_shape=[jax.ShapeDtypeStruct(wg_shape, F32)],
        scratch_shapes=[pltpu.VMEM(wg_shape, F32), pltpu.SemaphoreType.DMA],
        args=(*args, gain),
    )
    return outs[0], rides


def _in_proj_dx(dparts, hp, dh2, gain, wg, s_len, tm, ride=None):
    tp, d = hp.shape
    nt = tp // tm
    widths = [p.shape[1] for p in dparts]
    segs = _proj_segments(widths[0], widths[2], widths[4], wg.shape[2])

    def body(*refs):
        dp = refs[:6]
        hp_ref, dh2_ref, g_ref, w_ref = refs[6:10]
        gx_ref, dmeta_ref, dg_ref = refs[10:13]
        stage, sems = refs[13:]
        i = pl.program_id(0)

        @pl.when(i == 0)
        def _():
            dg_ref[...] = jnp.zeros_like(dg_ref)

        h = hp_ref[...]
        rinv = lax.rsqrt(jnp.mean(h * h, axis=-1, keepdims=True) + EPS)
        nrm = h * rinv
        gv = g_ref[...]
        du = jnp.zeros((tm, d), F32)
        for p_ref, parts in zip(dp, segs):
            for jj, inner, off, take in parts:
                du = du + _dot_nt(p_ref[:, off:off + take], w_ref[jj, :, inner:inner + take])
        dg_ref[...] += jnp.sum(du * nrm, axis=0, keepdims=True)
        dn = du * gv
        dh = dh2_ref[...] + rinv * (dn - nrm * jnp.mean(dn * nrm, axis=-1, keepdims=True))

        def first_copy():
            return pltpu.make_async_copy(stage.at[0, pl.ds(CHUNK, tm - CHUNK), :],
                                         gx_ref.at[pl.ds(0, tm - CHUNK), :], sems.at[0])

        def tile_copy(slot, start):
            return pltpu.make_async_copy(stage.at[slot], gx_ref.at[pl.ds(start, tm), :], sems.at[slot])

        @pl.when(i == 0)
        def _():
            dmeta_ref[...] = dh[PAD_ROWS:CHUNK]
            stage[0] = dh
            first_copy().start()

        @pl.when(i > 0)
        def _():
            slot = 1 + i % 2

            @pl.when(i >= 3)
            def _():
                tile_copy(slot, 0).wait()

            stage[slot] = dh
            tile_copy(slot, pl.multiple_of(i * tm - CHUNK, CHUNK)).start()

        @pl.when(i == nt - 1)
        def _():
            first_copy().wait()
            for step in (nt - 2, nt - 1):
                if step >= 1:
                    tile_copy(1 + step % 2, 0).wait()

    tile = lambda w: pl.BlockSpec((tm, w), lambda i: (i, 0))
    return _hosted_call(
        body, ride, nt,
        name="in_proj_dx",
        in_specs=[tile(w) for w in widths] + [tile(d), tile(d), pl.BlockSpec(gain.shape, lambda i: (0, 0)),
                                              pl.BlockSpec(wg.shape, lambda i: (0, 0, 0))],
        out_specs=[ANY_SPEC, pl.BlockSpec((N_META, d), lambda i: (0, 0)), pl.BlockSpec((1, d), lambda i: (0, 0))],
        out_shape=[jax.ShapeDtypeStruct((s_len, d), F32), jax.ShapeDtypeStruct((N_META, d), F32),
                   jax.ShapeDtypeStruct((1, d), F32)],
        scratch_shapes=[pltpu.VMEM((3, tm, d), F32), pltpu.SemaphoreType.DMA((3,))],
        args=(*dparts, hp, dh2, gain, wg),
    )


def _pair_sum(buf, recv, c_arr, tr, name):
    _, rows, cols = buf.shape

    def body(c_ref, mine_ref, got_ref, out_ref):
        out_ref[...] = (mine_ref[...] + got_ref[...]).astype(BF16)

    grid_spec = pltpu.PrefetchScalarGridSpec(
        num_scalar_prefetch=1,
        grid=(N_CHIPS, rows // tr),
        in_specs=[pl.BlockSpec((1, tr, cols), lambda jj, r, c_ref: (2 * jj + c_ref[0], r, 0)),
                  pl.BlockSpec((1, tr, cols), lambda jj, r, c_ref: (jj, r, 0))],
        out_specs=pl.BlockSpec((1, tr, cols), lambda jj, r, c_ref: (jj, r, 0)),
    )
    return pl.pallas_call(
        body,
        name=name,
        grid_spec=grid_spec,
        out_shape=jax.ShapeDtypeStruct((N_CHIPS, rows, cols), BF16),
    )(c_arr, buf, recv)


def _pair_exchange_sum(buf, c_arr, tr, name):
    _, rows, cols = buf.shape
    per = rows // tr

    def body(c_ref, src_ref, mine_ref, out_ref, got, send_sems, recv_sems):
        jj, r = pl.program_id(0), pl.program_id(1)
        x, y, c, _ = _position()
        copies = [_remote(src_ref.at[2 * k + 1 - c], got.at[k], send_sems, recv_sems, k, (x, y, 1 - c))
                  for k in range(N_CHIPS)]

        @pl.when((jj == 0) & (r == 0))
        def _():
            for cp in copies:
                cp.start()

        for k in range(N_CHIPS):
            @pl.when((jj == k) & (r == 0))
            def _():
                copies[k].wait_recv()

        rows_r = pl.ds(pl.multiple_of(r * tr, tr), tr)
        out_ref[0] = (mine_ref[0] + got[jj, rows_r, :]).astype(BF16)

        @pl.when((jj == N_CHIPS - 1) & (r == per - 1))
        def _():
            for cp in copies:
                cp.wait_send()

    grid_spec = pltpu.PrefetchScalarGridSpec(
        num_scalar_prefetch=1,
        grid=(N_CHIPS, per),
        in_specs=[ANY_SPEC, pl.BlockSpec((1, tr, cols), lambda jj, r, c_ref: (2 * jj + c_ref[0], r, 0))],
        out_specs=pl.BlockSpec((1, tr, cols), lambda jj, r, c_ref: (jj, r, 0)),
        scratch_shapes=[pltpu.VMEM((N_CHIPS, rows, cols), F32), pltpu.SemaphoreType.DMA((N_CHIPS,)),
                        pltpu.SemaphoreType.DMA((N_CHIPS,))],
    )
    return pl.pallas_call(
        body,
        name=name,
        grid_spec=grid_spec,
        out_shape=jax.ShapeDtypeStruct((N_CHIPS, rows, cols), BF16),
        compiler_params=pltpu.CompilerParams(dimension_semantics=("arbitrary", "arbitrary"),
                                             vmem_limit_bytes=VMEM_LIMIT),
    )(c_arr, buf, buf)


def _chip_sum(mine, got, j_arr, tr, name, loss_part=None):
    _, rows, cols = got.shape
    extra = [] if loss_part is None else [loss_part]

    def body(j_ref, mine_ref, got_ref, *rest):
        out_ref = rest[-1]
        j = j_ref[0]
        acc = None
        for jj in range(N_CHIPS):
            term = jnp.where(j == jj, mine_ref[0], got_ref[jj]).astype(F32)
            acc = term if acc is None else acc + term
        out_ref[...] = acc
        if loss_part is not None:
            out_ref[ROW_LOSS:ROW_LOSS + 1, :] = rest[0][0:1, :]

    grid_spec = pltpu.PrefetchScalarGridSpec(
        num_scalar_prefetch=1,
        grid=(rows // tr,),
        in_specs=[pl.BlockSpec((1, tr, cols), lambda r, j_ref: (j_ref[0], r, 0)),
                  pl.BlockSpec((N_CHIPS, tr, cols), lambda r, j_ref: (0, r, 0))] +
                 [pl.BlockSpec(e.shape, lambda r, j_ref: (0, 0)) for e in extra],
        out_specs=pl.BlockSpec((tr, cols), lambda r, j_ref: (r, 0)),
    )
    return pl.pallas_call(
        body,
        name=name,
        grid_spec=grid_spec,
        out_shape=jax.ShapeDtypeStruct((rows, cols), F32),
    )(j_arr, mine, got, *extra)


def _finish_exchange(f_in, f_small):
    def body(fin_ref, fs_ref, rin_ref, os_ref, send_sems, recv_sems, local_sem):
        x, y, c, chips = _position()
        j = 2 * x + y
        me = 2 * j + c
        sibling = (x, y, 1 - c)
        local = pltpu.make_async_copy(fs_ref, os_ref.at[me], local_sem)
        local.start()

        def copy(k, src, dst, to):
            return _remote(src, dst, send_sems, recv_sems, k, to)

        first = [copy(0, fin_ref, rin_ref, sibling), copy(1, fs_ref, os_ref.at[me], sibling)]
        first += [copy(2 + k, fs_ref, os_ref.at[me], (cx, cy, c)) for k, (cx, cy) in enumerate(chips)]
        for cp in first:
            cp.start()
        passed = []
        for k, (cx, cy) in enumerate(chips):
            unit = 2 * (2 * cx + cy) + c
            copy(2 + k, fs_ref, os_ref.at[unit], sibling).wait_recv()
            fwd = copy(5 + k, os_ref.at[unit], os_ref.at[unit], sibling)
            fwd.start()
            passed.append(fwd)
        copy(0, fin_ref, rin_ref, sibling).wait_recv()
        copy(1, fs_ref, os_ref.at[2 * j + 1 - c], sibling).wait_recv()
        for k, (cx, cy) in enumerate(chips):
            unit = 2 * (2 * cx + cy) + 1 - c
            copy(5 + k, fs_ref, os_ref.at[unit], sibling).wait_recv()
        for cp in first + passed:
            cp.wait_send()
        local.wait()

    return pl.pallas_call(
        body,
        name="grad_finish_exchange",
        in_specs=[ANY_SPEC] * 2,
        out_specs=[ANY_SPEC] * 2,
        out_shape=[jax.ShapeDtypeStruct(f_in.shape, F32), jax.ShapeDtypeStruct((N_DEV,) + f_small.shape, F32)],
        scratch_shapes=[pltpu.SemaphoreType.DMA((8,)), pltpu.SemaphoreType.DMA((8,)), pltpu.SemaphoreType.DMA],
    )(f_in, f_small)


def _adamw_math(w, g, m, v):
    m = ADAM_B1 * m + (1.0 - ADAM_B1) * g
    v = ADAM_B2 * v + (1.0 - ADAM_B2) * (g * g)
    m_hat = m / (1.0 - ADAM_B1 ** ADAM_STEP)
    v_hat = v / (1.0 - ADAM_B2 ** ADAM_STEP)
    delta = -ADAM_LR * (m_hat / (jnp.sqrt(v_hat) + ADAM_EPS) + ADAM_WD * w)
    return delta, m, v


def _adamw_big(w, g_mine, g_sib, m, v, c_arr, tr, name):
    rows, cols = w.shape
    half = rows // 2
    per = half // tr

    def body(c_ref, w_ref, gm_ref, gs_ref, m_ref, v_ref, g_ref, d_ref, mo_ref, vo_ref):
        g = jnp.where(pl.program_id(0) == c_ref[0], gm_ref[...], gs_ref[...])
        g_ref[...] = g
        d_ref[...], mo_ref[...], vo_ref[...] = _adamw_math(w_ref[...], g, m_ref[...], v_ref[...])

    full = pl.BlockSpec((tr, cols), lambda h, r, c_ref: (h * per + r, 0))
    unit = pl.BlockSpec((tr, cols), lambda h, r, c_ref: (r, 0))
    grid_spec = pltpu.PrefetchScalarGridSpec(
        num_scalar_prefetch=1,
        grid=(2, per),
        in_specs=[full, unit, unit, full, full],
        out_specs=[full] * 4,
    )
    return pl.pallas_call(
        body,
        name=name,
        grid_spec=grid_spec,
        out_shape=[jax.ShapeDtypeStruct(w.shape, F32)] * 4,
    )(c_arr, w, g_mine, g_sib, m, v)


def _adamw_small(j_arr, packed, params):
    names = list(params)
    n = len(names)
    vec_names = ["norm_gain", "conv_b", "b_rg", "b_ig", "lru_lambda", "ret_norm_gain", "final_norm_gain"]

    def body(j_ref, pk_ref, *refs):
        ins = refs[:3 * n]
        outs = refs[3 * n:]
        j = j_ref[0]

        def shard(row, rows):
            return jnp.concatenate([pk_ref[2 * j, row:row + rows, :], pk_ref[2 * j + 1, row:row + rows, :]], axis=1)

        def tail_sum(unit, row, rows):
            start = pl.multiple_of(UNIT_ROWS + TAIL_ROWS * unit + row, SUBLANES)
            total = pk_ref[0, pl.ds(start, rows), :]
            for dev in range(1, N_DEV):
                total = total + pk_ref[dev, pl.ds(start, rows), :]
            return total

        for idx, name in enumerate(names):
            if name == "w_rg":
                g = pk_ref[:, ROW_WR:ROW_WR + LANES, :]
            elif name == "w_ig":
                g = pk_ref[:, ROW_WI:ROW_WI + LANES, :]
            elif name == "meta_tokens":
                g = jnp.concatenate([tail_sum(2 * j, 0, N_META), tail_sum(2 * j + 1, 0, N_META)], axis=1)
            elif name == "norm_gain":
                g = jnp.concatenate([tail_sum(u, N_META, SUBLANES)[0:1] for u in range(N_DEV)], axis=1)
            elif name == "conv_w":
                g = shard(ROW_CONV, 4)
            else:
                row = ROW_VEC + vec_names.index(name)
                g = jnp.concatenate([pk_ref[u, row:row + 1, :] for u in range(N_DEV)], axis=1)
            w_ref, m_ref, v_ref = ins[3 * idx:3 * idx + 3]
            delta, m, v = _adamw_math(w_ref[...], g, m_ref[...], v_ref[...])
            g_ref, d_ref, mo_ref, vo_ref = outs[4 * idx:4 * idx + 4]
            g_ref[...], d_ref[...], mo_ref[...], vo_ref[...] = g, delta, m, v
        total = pk_ref[0, ROW_LOSS:ROW_LOSS + 1, :]
        for u in range(1, N_DEV):
            total = total + pk_ref[u, ROW_LOSS:ROW_LOSS + 1, :]
        outs[4 * n][...] = jnp.broadcast_to(total, (SUBLANES, LANES))

    flat_in, out_shape = [], []
    for name in names:
        w, m, v = params[name]
        flat_in += [w, m, v]
        out_shape += [jax.ShapeDtypeStruct(w.shape, F32)] * 4
    out_shape.append(jax.ShapeDtypeStruct((SUBLANES, LANES), F32))
    res = pl.pallas_call(
        body,
        name="adamw_small",
        in_specs=[SMEM_SPEC, VMEM_SPEC] + [VMEM_SPEC] * (3 * n),
        out_specs=[VMEM_SPEC] * (4 * n + 1),
        out_shape=out_shape,
    )(j_arr, packed, *flat_in)
    return {name: tuple(res[4 * idx:4 * idx + 4]) for idx, name in enumerate(names)}, res[4 * n][0, 0]


def _units(a):
    rows = a.shape[0]
    return jnp.transpose(a.reshape(rows, N_DEV, LANES), (1, 0, 2))


def kernel(x, meta_tokens, norm_gain, w_in, conv_w, conv_b, w_rg, b_rg, w_ig, b_ig, lru_lambda, ret_norm_gain, w_out, final_norm_gain, loss_target, m_meta_tokens, m_norm_gain, m_w_in, m_conv_w, m_conv_b, m_w_rg, m_b_rg, m_w_ig, m_b_ig, m_lru_lambda, m_ret_norm_gain, m_w_out, m_final_norm_gain, v_meta_tokens, v_norm_gain, v_w_in, v_conv_w, v_conv_b, v_w_rg, v_b_rg, v_w_ig, v_b_ig, v_lru_lambda, v_ret_norm_gain, v_w_out, v_final_norm_gain):
    s_len, d = x.shape[1], x.shape[2]
    d_lru = w_rg.shape[1] * w_rg.shape[2]
    d_ret = ret_norm_gain.shape[1]
    d_qk = HEADS * QK_DIM
    tp = s_len + CHUNK
    tm = TOKEN_TILE
    assert tp % tm == 0 and d_lru == HEADS * LANES and d_ret == HEADS * LANES
    ax, ay, ac = lax.axis_index("x"), lax.axis_index("y"), lax.axis_index("c")
    c_arr = jnp.reshape(ac, (1,)).astype(jnp.int32)
    j_arr = jnp.reshape(2 * ax + ay, (1,)).astype(jnp.int32)

    small = jnp.concatenate([meta_tokens, conv_w[0], jnp.zeros((4, meta_tokens.shape[1]), F32)], axis=0)
    wg, sg = _gather_weights(w_in[0], small)
    cols = sg.shape[2]
    meta_full = jnp.transpose(sg[:, :N_META, :], (1, 0, 2)).reshape(N_META, N_CHIPS * cols)
    cw_full = jnp.transpose(sg[:, N_META:N_META + 4, :], (1, 0, 2)).reshape(4, N_CHIPS * cols)
    cw8 = jnp.concatenate([cw_full, jnp.zeros((4, cw_full.shape[1]), F32)], axis=0)

    half = QK_DIM // 2
    inv = ROPE_BASE ** (-jnp.arange(half, dtype=F32) / half)
    pos = (jnp.arange(tp) - PAD_ROWS).astype(F32)
    ang = pos[:, None] * inv[None, :]
    cos_t = jnp.tile(jnp.cos(ang), (1, LANES // half))
    sign = jnp.where((jnp.arange(LANES) % QK_DIM) < half, -1.0, 1.0).astype(F32)
    sin_t = jnp.tile(jnp.sin(ang), (1, LANES // half)) * sign[None, :]
    tables = _ret_tables()
    gain_f = final_norm_gain.reshape(1, d)

    hp, lx, lg, *qkv, rg, wo4 = _in_proj(x[0], meta_full, norm_gain, wg, cos_t, sin_t, w_out[0], tables[1], tables[2],
                                         tm, d_lru, d_qk, d_ret)
    wo = wo4.reshape(N_CHIPS * wo4.shape[1], wo4.shape[2])
    hl, y_lru, *lru_saved = _lru_fwd(lx, lg, cw8, conv_b, w_rg[0], b_rg, w_ig[0], b_ig, lru_lambda, tm)
    o, y_ret, rprev = _ret_fwd(*qkv, rg, ret_norm_gain, tables, tm)
    dh2, dy_lru, dy_ret, dwo, dgf, loss_acc = _out_proj_loss(y_lru, y_ret, hp, loss_target[0], wo, gain_f, tm)

    g_out = dwo.reshape(N_DEV, dwo.shape[0] // N_DEV, dwo.shape[1])
    (dq, dk, dv, drg, dgain), (r_out,) = _ret_bwd(*qkv, rg, o, rprev, dy_ret, ret_norm_gain, cos_t, sin_t, tables,
                                                 tm, ride=_pair_ride([g_out]))
    q_out = _pair_sum(g_out, r_out, c_arr, 128, "grad_pair_sum_out")
    (dlx, dlg, dcw, dcb, dwr, dbr, dwi, dbi, dlam), (e_out,) = _lru_bwd(
        lx, lg, hl, dy_lru, lru_saved, cw8, w_rg[0], w_ig[0], lru_lambda, tm, ride=_chip_ride([q_out]))
    f_out = _chip_sum(q_out, e_out, j_arr, 128, "grad_chip_sum_out")
    zero_row = jnp.zeros((1, d), F32)
    vecs = [zero_row, dcb, dbr, dbi, dlam, dgain, dgf]
    g_small = jnp.concatenate([dwr, dwi, jnp.zeros((N_DEV, N_META, LANES), F32), _units(dcw[0:4])]
                              + [_units(a) for a in vecs]
                              + [jnp.zeros((N_DEV, UNIT_ROWS - ROW_VEC - N_VEC, LANES), F32)], axis=1)
    dparts = [dlx, dlg, dq, dk, dv, drg]
    dwg, (s_out, r_small) = _in_proj_dw(dparts, hp, norm_gain, wg.shape,
                                        ride=_join_rides(_sibling_ride([f_out]), _pair_ride([g_small])))
    g_in = dwg.reshape(N_DEV, dwg.shape[1] // 2, dwg.shape[2])
    q_in = _pair_exchange_sum(g_in, c_arr, 128, "grad_pair_exchange_sum_in")
    q_small = _pair_sum(g_small, r_small, c_arr, UNIT_ROWS, "grad_pair_sum_small")
    (grad_x, dmeta, dg1), (e_in, e_small) = _in_proj_dx(dparts, hp, dh2, norm_gain, wg, s_len, tm,
                                                        ride=_chip_ride([q_in, q_small]))
    f_in = _chip_sum(q_in, e_in, j_arr, 128, "grad_chip_sum_in")
    f_small = _chip_sum(q_small, e_small, j_arr, UNIT_ROWS, "grad_chip_sum_small", loss_part=loss_acc)
    tail = jnp.concatenate([_units(dmeta), _units(dg1), jnp.zeros((N_DEV, TAIL_ROWS - N_META - 1, LANES), F32)],
                           axis=1).reshape(N_DEV * TAIL_ROWS, LANES)
    s_in, o_small = _finish_exchange(f_in, jnp.concatenate([f_small, tail], axis=0))

    res_in = _adamw_big(w_in[0], f_in, s_in, m_w_in[0], v_w_in[0], c_arr, 256, "adamw_w_in")
    res_out = _adamw_big(w_out[0], f_out, s_out, m_w_out[0], v_w_out[0], c_arr, 256, "adamw_w_out")
    small_params = {
        "meta_tokens": (meta_tokens, m_meta_tokens, v_meta_tokens),
        "norm_gain": (norm_gain, m_norm_gain, v_norm_gain),
        "conv_w": (conv_w[0], m_conv_w[0], v_conv_w[0]),
        "conv_b": (conv_b, m_conv_b, v_conv_b),
        "w_rg": (w_rg[0], m_w_rg[0], v_w_rg[0]),
        "b_rg": (b_rg, m_b_rg, v_b_rg),
        "w_ig": (w_ig[0], m_w_ig[0], v_w_ig[0]),
        "b_ig": (b_ig, m_b_ig, v_b_ig),
        "lru_lambda": (lru_lambda, m_lru_lambda, v_lru_lambda),
        "ret_norm_gain": (ret_norm_gain, m_ret_norm_gain, v_ret_norm_gain),
        "final_norm_gain": (gain_f, m_final_norm_gain.reshape(1, d), v_final_norm_gain.reshape(1, d)),
    }
    res, loss = _adamw_small(j_arr, o_small, small_params)
    res["w_in"] = tuple(res_in)
    res["w_out"] = tuple(res_out)

    order = ["meta_tokens", "norm_gain", "w_in", "conv_w", "conv_b", "w_rg", "b_rg", "w_ig", "b_ig", "lru_lambda",
             "ret_norm_gain", "w_out", "final_norm_gain"]
    shapes = {"w_in": w_in.shape, "conv_w": conv_w.shape, "w_rg": w_rg.shape, "w_ig": w_ig.shape,
              "w_out": w_out.shape, "final_norm_gain": final_norm_gain.shape}
    outs = [loss, grad_x.reshape(x.shape)]
    for kind in range(4):
        for name in order:
            a = res[name][kind]
            outs.append(a.reshape(shapes[name]) if name in shapes else a)
    return tuple(outs)
```

```python
import functools

import jax
import jax.numpy as jnp
from jax import lax
from jax.experimental import pallas as pl
from jax.experimental.pallas import tpu as pltpu

F32 = jnp.float32
BF16 = jnp.bfloat16

N_META = 16
CHUNK = 128
PAD_ROWS = CHUNK - N_META
HEADS = 8
QK_DIM = 64
LANES = 128
SUBLANES = 8
LRU_C = 8.0
EPS = 1e-6
ROPE_BASE = 10000.0
ADAM_LR = 0.001
ADAM_B1 = 0.9
ADAM_B2 = 0.999
ADAM_EPS = 1e-08
ADAM_WD = 0.01
ADAM_STEP = 10
N_CHIPS = 4
N_DEV = 8
TOKEN_TILE = 384
VMEM_LIMIT = 58 * 1024 * 1024
MESH = pl.DeviceIdType.MESH

VMEM_SPEC = pl.BlockSpec(memory_space=pltpu.VMEM)
SMEM_SPEC = pl.BlockSpec(memory_space=pltpu.SMEM)
ANY_SPEC = pl.BlockSpec(memory_space=pl.ANY)

ROW_WR, ROW_WI, ROW_META, ROW_CONV, ROW_VEC, UNIT_ROWS = 0, 128, 256, 272, 276, 288
N_VEC = 7
ROW_LOSS = ROW_VEC + N_VEC
TAIL_ROWS = 24


def _dot(a, b):
    return jnp.dot(a, b, preferred_element_type=F32)


def _dot_nt(a, b):
    return lax.dot_general(a, b, (((1,), (1,)), ((), ())), preferred_element_type=F32)


def _dot_tn(a, b):
    return lax.dot_general(a, b, (((0,), (0,)), ((), ())), preferred_element_type=F32)


def _sigmoid(x):
    return 0.5 * jnp.tanh(0.5 * x) + 0.5


def _shift_down(x, prev8, s):
    rolled = pltpu.roll(x, s, 0)
    rows = lax.broadcasted_iota(jnp.int32, (SUBLANES, x.shape[1]), 0)
    top = jnp.where(rows < s, pltpu.roll(prev8, s, 0), rolled[0:SUBLANES])
    return jnp.concatenate([top, rolled[SUBLANES:]], axis=0)


def _shift_up(x, next8, s):
    n = x.shape[0]
    rolled = pltpu.roll(x, n - s, 0)
    rows = lax.broadcasted_iota(jnp.int32, (SUBLANES, x.shape[1]), 0)
    bot = jnp.where(rows >= SUBLANES - s, pltpu.roll(next8, SUBLANES - s, 0), rolled[n - SUBLANES:n])
    return jnp.concatenate([rolled[:n - SUBLANES], bot], axis=0)


def _rot_partner(t):
    w = t.shape[1]
    lane = lax.broadcasted_iota(jnp.int32, t.shape, 1)
    first = (lane % QK_DIM) < (QK_DIM // 2)
    return jnp.where(first, pltpu.roll(t, w - QK_DIM // 2, 1), pltpu.roll(t, QK_DIM // 2, 1))


def _tile_lanes(t, reps):
    return jnp.concatenate([t] * reps, axis=1)


class _Ride:
    def __init__(self, srcs, dst_shapes, n_copies, make):
        self.srcs, self.dst_shapes, self.n_copies, self.make = list(srcs), list(dst_shapes), n_copies, make


def _join_rides(a, b):
    def make(src, dst, send_sems, recv_sems, base):
        na, da = len(a.srcs), len(a.dst_shapes)
        return (a.make(src[:na], dst[:da], send_sems, recv_sems, base)
                + b.make(src[na:], dst[da:], send_sems, recv_sems, base + a.n_copies))

    return _Ride(a.srcs + b.srcs, a.dst_shapes + b.dst_shapes, a.n_copies + b.n_copies, make)


def _position():
    x, y, c = lax.axis_index("x"), lax.axis_index("y"), lax.axis_index("c")
    return x, y, c, [(1 - x, y), (x, 1 - y), (1 - x, 1 - y)]


def _remote(src, dst, send_sems, recv_sems, k, to):
    return pltpu.make_async_remote_copy(src_ref=src, dst_ref=dst, send_sem=send_sems.at[k], recv_sem=recv_sems.at[k],
                                        device_id=to, device_id_type=MESH)


def _pair_ride(bufs):
    def make(src, dst, send_sems, recv_sems, base):
        x, y, c, _ = _position()
        return [_remote(src[b].at[2 * jj + 1 - c], dst[b].at[jj], send_sems, recv_sems, base + b * N_CHIPS + jj,
                        (x, y, 1 - c)) for b in range(len(bufs)) for jj in range(N_CHIPS)]

    shapes = [jax.ShapeDtypeStruct((N_CHIPS,) + b.shape[1:], b.dtype) for b in bufs]
    return _Ride(bufs, shapes, N_CHIPS * len(bufs), make)


def _chip_ride(bufs):
    def make(src, dst, send_sems, recv_sems, base):
        x, y, c, chips = _position()
        return [_remote(src[b].at[2 * cx + cy], dst[b].at[2 * x + y], send_sems, recv_sems, base + b * 3 + k,
                        (cx, cy, c)) for b in range(len(bufs)) for k, (cx, cy) in enumerate(chips)]

    shapes = [jax.ShapeDtypeStruct(b.shape, b.dtype) for b in bufs]
    return _Ride(bufs, shapes, 3 * len(bufs), make)


def _sibling_ride(bufs):
    def make(src, dst, send_sems, recv_sems, base):
        x, y, c, _ = _position()
        return [_remote(src[b], dst[b], send_sems, recv_sems, base + b, (x, y, 1 - c)) for b in range(len(bufs))]

    shapes = [jax.ShapeDtypeStruct(b.shape, b.dtype) for b in bufs]
    return _Ride(bufs, shapes, len(bufs), make)


def _hosted_call(body, ride, n_steps, *, name, in_specs, out_specs, out_shape, scratch_shapes, args):
    params = pltpu.CompilerParams(dimension_semantics=("arbitrary",), vmem_limit_bytes=VMEM_LIMIT)
    if ride is None:
        res = pl.pallas_call(body, name=name, grid=(n_steps,), in_specs=list(in_specs), out_specs=list(out_specs),
                             out_shape=list(out_shape), scratch_shapes=list(scratch_shapes),
                             compiler_params=params)(*args)
        return list(res), []
    sizes = [len(in_specs), len(ride.srcs), len(out_specs), len(ride.dst_shapes), len(scratch_shapes), 2]

    def hosted(*refs):
        groups, pos = [], 0
        for n in sizes:
            groups.append(refs[pos:pos + n])
            pos += n
        ins, rin, outs, rout, scr, (send_sems, recv_sems) = groups
        i = pl.program_id(0)

        @pl.when(i == 0)
        def _():
            for cp in ride.make(rin, rout, send_sems, recv_sems, 0):
                cp.start()

        body(*ins, *outs, *scr)

        @pl.when(i == n_steps - 1)
        def _():
            for cp in ride.make(rin, rout, send_sems, recv_sems, 0):
                cp.wait()

    n_out = len(out_specs)
    res = pl.pallas_call(
        hosted,
        name=name,
        grid=(n_steps,),
        in_specs=list(in_specs) + [ANY_SPEC] * len(ride.srcs),
        out_specs=list(out_specs) + [ANY_SPEC] * len(ride.dst_shapes),
        out_shape=list(out_shape) + ride.dst_shapes,
        scratch_shapes=list(scratch_shapes) + [pltpu.SemaphoreType.DMA((ride.n_copies,)),
                                               pltpu.SemaphoreType.DMA((ride.n_copies,))],
        compiler_params=params,
    )(*args, *ride.srcs)
    return list(res[:n_out]), list(res[n_out:])


def _gather_weights(w_in, small):
    r_in, c_in = w_in.shape
    h_in = r_in // 2
    q_in = h_in // 2

    def body(win_ref, small_ref, wg_ref, sg_ref, send_sems, recv_sems):
        x, y, c, chips = _position()
        j = 2 * x + y
        sibling = (x, y, 1 - c)
        xn, yn, dg = chips
        jx, jy, jd = (2 * cx + cy for cx, cy in chips)

        def half(jj, cc):
            return wg_ref.at[jj, pl.ds(cc * h_in, h_in), :]

        def quarter(jj, qq):
            return wg_ref.at[jj, pl.ds(c * h_in + qq * q_in, q_in), :]

        def copy(k, ref, to):
            return _remote(ref, ref, send_sems, recv_sems, k, to)

        def cast_rows(start, rows):
            start = pl.multiple_of(start, q_in)
            wg_ref[j, pl.ds(start, rows), :] = win_ref[pl.ds(start, rows), :].astype(BF16)

        first = [copy(0, quarter(j, 0), (*xn, c)), copy(2, quarter(j, 1), (*yn, c)),
                 copy(1, quarter(j, 1), (*xn, c)), copy(3, quarter(j, 0), (*yn, c))]
        sg_ref[j] = small_ref[...]
        cast_rows(c * h_in, q_in)
        first[0].start()
        cast_rows(c * h_in + q_in, q_in)
        for cp in first[1:]:
            cp.start()
        small_copies = [copy(9 + k, sg_ref.at[j], (cx, cy, c)) for k, (cx, cy) in enumerate(chips)]
        for cp in small_copies:
            cp.start()
        first += small_copies
        cast_rows((1 - c) * h_in, h_in)
        copy(0, quarter(jx, 0), sibling).wait_recv()
        along_y = copy(4, quarter(jx, 0), (*yn, c))
        along_y.start()
        copy(2, quarter(jy, 1), sibling).wait_recv()
        along_x = copy(5, quarter(jy, 1), (*xn, c))
        along_x.start()
        copy(1, quarter(jx, 1), sibling).wait_recv()
        to_sib = [copy(6, half(jx, c), sibling)]
        to_sib[-1].start()
        copy(3, quarter(jy, 0), sibling).wait_recv()
        to_sib.append(copy(7, half(jy, c), sibling))
        to_sib[-1].start()
        copy(4, quarter(jd, 0), sibling).wait_recv()
        copy(5, quarter(jd, 1), sibling).wait_recv()
        to_sib.append(copy(8, half(jd, c), sibling))
        to_sib[-1].start()
        for k, jk in enumerate((jx, jy, jd)):
            copy(6 + k, half(jk, 1 - c), sibling).wait_recv()
            copy(9 + k, sg_ref.at[jk], sibling).wait_recv()
        for cp in first + [along_y, along_x] + to_sib:
            cp.wait_send()

    return pl.pallas_call(
        body,
        name="gather_weights",
        out_shape=(jax.ShapeDtypeStruct((N_CHIPS, r_in, c_in), BF16),
                   jax.ShapeDtypeStruct((N_CHIPS,) + small.shape, F32)),
        in_specs=[VMEM_SPEC, VMEM_SPEC],
        out_specs=(VMEM_SPEC, VMEM_SPEC),
        scratch_shapes=[pltpu.SemaphoreType.DMA((12,)), pltpu.SemaphoreType.DMA((12,))],
        compiler_params=pltpu.CompilerParams(vmem_limit_bytes=VMEM_LIMIT),
    )(w_in, small)


def _proj_segments(d_lru, d_qk, d_ret, chunk_w):
    widths = [d_lru, d_lru, d_qk, d_qk, d_ret, d_ret]
    segs, col = [], 0
    for w in widths:
        parts, off = [], 0
        while off < w:
            jj, inner = divmod(col + off, chunk_w)
            take = min(w - off, chunk_w - inner)
            parts.append((jj, inner, off, take))
            off += take
        segs.append(parts)
        col += w
    return segs


def _in_proj(x2, meta, gain, wg, cos_t, sin_t, w_out, qdec, kdec, tm, d_lru, d_qk, d_ret):
    s_len, d = x2.shape
    tp = s_len + CHUNK
    nt, nb = tp // tm, tm // CHUNK
    segs = _proj_segments(d_lru, d_qk, d_ret, wg.shape[2])
    outs = [(d, F32), (d_lru, F32), (d_lru, F32)] + [(d_qk, BF16)] * 4 + [(d_ret, BF16), (d_ret, F32)]
    r_out, c_out = w_out.shape
    h_out = r_out // 2
    fwd_step = min(6, nt - 1)

    def gather_w_out(i, wout_ref, wo_ref, wob, send_sems, recv_sems, local_sem):
        x, y, c, chips = _position()
        j = 2 * x + y
        sibling = (x, y, 1 - c)

        def half(jj, cc):
            return wo_ref.at[jj, pl.ds(cc * h_out, h_out), :]

        local = pltpu.make_async_copy(wob, wo_ref.at[j], local_sem)
        first = [_remote(wob.at[pl.ds(c * h_out, h_out), :], half(j, c), send_sems, recv_sems, k, (cx, cy, c))
                 for k, (cx, cy) in enumerate(chips)]
        passed = [_remote(half(2 * cx + cy, c), half(2 * cx + cy, c), send_sems, recv_sems, 3 + k, sibling)
                  for k, (cx, cy) in enumerate(chips)]

        @pl.when(i == 0)
        def _():
            wob[...] = wout_ref[...].astype(BF16)
            local.start()
            for cp in first:
                cp.start()

        @pl.when(i == fwd_step)
        def _():
            for k, (cx, cy) in enumerate(chips):
                _remote(half(2 * cx + cy, c), half(2 * cx + cy, c), send_sems, recv_sems, k, sibling).wait_recv()
                passed[k].start()

        @pl.when(i == nt - 1)
        def _():
            for k, (cx, cy) in enumerate(chips):
                jk = 2 * cx + cy
                _remote(half(jk, 1 - c), half(jk, 1 - c), send_sems, recv_sems, 3 + k, sibling).wait_recv()
            for cp in first + passed:
                cp.wait_send()
            local.wait()

    def body(*refs):
        xb = refs[:nb]
        meta_ref, g_ref, w_ref, cos_ref, sin_ref, wout_ref, qdec_ref, kdec_ref = refs[nb:nb + 8]
        hp_ref, lx_ref, lg_ref, qb_ref, kb_ref, qd_ref, kd_ref, vb_ref, rg_ref = refs[nb + 8:nb + 17]
        wo_ref, q_s, k_s, wob, send_sems, recv_sems, local_sem = refs[nb + 17:]
        i = pl.program_id(0)
        gather_w_out(i, wout_ref, wo_ref, wob, send_sems, recv_sems, local_sem)
        blocks = [r[...] for r in xb]
        head = jnp.concatenate([jnp.zeros((PAD_ROWS, d), F32), meta_ref[...]], axis=0)
        blocks[0] = jnp.where(i == 0, head, blocks[0])
        h = jnp.concatenate(blocks, axis=0)
        hp_ref[...] = h
        rinv = lax.rsqrt(jnp.mean(h * h, axis=-1, keepdims=True) + EPS)
        u = ((h * rinv) * g_ref[...]).astype(BF16)
        for out_ref, parts in zip([lx_ref, lg_ref, q_s, k_s, vb_ref, rg_ref], segs):
            for jj, inner, off, take in parts:
                out_ref[:, off:off + take] = _dot(u, w_ref[jj, :, inner:inner + take]).astype(out_ref.dtype)
        cos = _tile_lanes(cos_ref[...], d_qk // LANES)
        sin = _tile_lanes(sin_ref[...], d_qk // LANES)
        q = q_s[...]
        q = q * cos + _rot_partner(q) * sin
        k = k_s[...]
        k = (k * cos + _rot_partner(k) * sin) * (QK_DIM ** -0.5)
        qb_ref[...] = q.astype(BF16)
        kb_ref[...] = k.astype(BF16)
        qd_ref[...] = (q * jnp.concatenate([qdec_ref[...]] * nb, axis=0)).astype(BF16)
        kd_ref[...] = (k * jnp.concatenate([kdec_ref[...]] * nb, axis=0)).astype(BF16)

    x_specs = [pl.BlockSpec((CHUNK, d), functools.partial(lambda i, b: (jnp.maximum(i * nb + b - 1, 0), 0), b=b))
               for b in range(nb)]
    tile = lambda w: pl.BlockSpec((tm, w), lambda i: (i, 0))
    return pl.pallas_call(
        body,
        name="in_proj",
        grid=(nt,),
        in_specs=x_specs + [pl.BlockSpec(meta.shape, lambda i: (0, 0)),
                            pl.BlockSpec(gain.shape, lambda i: (0, 0)),
                            pl.BlockSpec(wg.shape, lambda i: (0, 0, 0)),
                            tile(LANES), tile(LANES),
                            pl.BlockSpec(w_out.shape, lambda i: (0, 0)),
                            pl.BlockSpec(qdec.shape, lambda i: (0, 0)), pl.BlockSpec(kdec.shape, lambda i: (0, 0))],
        out_specs=[tile(w) for w, _ in outs] + [ANY_SPEC],
        out_shape=[jax.ShapeDtypeStruct((tp, w), dt) for w, dt in outs]
                  + [jax.ShapeDtypeStruct((N_CHIPS, r_out, c_out), BF16)],
        scratch_shapes=[pltpu.VMEM((tm, d_qk), F32), pltpu.VMEM((tm, d_qk), F32),
                        pltpu.VMEM((r_out, c_out), BF16), pltpu.SemaphoreType.DMA((6,)),
                        pltpu.SemaphoreType.DMA((6,)), pltpu.SemaphoreType.DMA],
        compiler_params=pltpu.CompilerParams(dimension_semantics=("arbitrary",), vmem_limit_bytes=VMEM_LIMIT),
    )(*([x2] * nb), meta, gain, wg, cos_t, sin_t, w_out, qdec, kdec)


def _segment_scan(a3, u3, out3, p3, carry, tm, reverse):
    groups = a3.shape[0]
    seg = tm // SUBLANES

    def step(j, state):
        hs, ps = state
        rows = pl.ds((seg - 1 - j) if reverse else j, SUBLANES, stride=seg)
        new_h, new_p = [], []
        for g in range(groups):
            a = a3[g, rows, :]
            h = a * hs[g] + u3[g, rows, :]
            p = ps[g] * a
            out3[g, rows, :] = h
            p3[g, rows, :] = p
            new_h.append(h)
            new_p.append(p)
        return tuple(new_h), tuple(new_p)

    zeros = tuple(jnp.zeros((SUBLANES, LANES), F32) for _ in range(groups))
    ones = tuple(jnp.ones((SUBLANES, LANES), F32) for _ in range(groups))
    lax.fori_loop(0, seg, step, (zeros, ones))
    carries = [carry[:, g * LANES:(g + 1) * LANES] for g in range(groups)]
    for s in (reversed(range(SUBLANES)) if reverse else range(SUBLANES)):
        rows = slice(s * seg, (s + 1) * seg)
        edge = s * seg if reverse else (s + 1) * seg - 1
        for g in range(groups):
            out3[g, rows, :] = out3[g, rows, :] + p3[g, rows, :] * carries[g]
            carries[g] = out3[g, edge:edge + 1, :]
    return jnp.concatenate(carries, axis=1)


def _softplus_neg(lam):
    z = -lam
    e = jnp.exp(-jnp.abs(z))
    e1 = 1.0 + e
    log1p_e = jnp.where(e1 == 1.0, e, jnp.log(e1) * (e / (e1 - 1.0)))
    return jnp.maximum(z, 0.0) + log1p_e


def _lru_fwd(lx, lg, cw, cb, wr, br, wi, bi, lam, tm):
    tp, w = lx.shape
    nt = tp // tm
    per8 = tm // SUBLANES
    n_heads = wr.shape[0]

    def body(lx_ref, lxp_ref, lg_ref, cw_ref, cb_ref, wr_ref, br_ref, wi_ref, bi_ref, lam_ref,
             hl_ref, y_ref, xc_ref, r_ref, ig_ref, a_ref, beta_ref, w4_ref, a_s, u_s, h_s, p_s, carry):
        i = pl.program_id(0)

        @pl.when(i == 0)
        def _():
            carry[...] = jnp.zeros_like(carry)

        sp = _softplus_neg(lam_ref[...])
        row = lax.broadcasted_iota(jnp.int32, (tm, 1), 0) + i * tm
        for hd in range(n_heads):
            hs = slice(hd * LANES, (hd + 1) * LANES)
            lxv = lx_ref[:, hs]
            prev8 = jnp.where(i == 0, 0.0, lxp_ref[:, hs])
            xc = cb_ref[:, hs] + _shift_down(lxv, prev8, 3) * cw_ref[0:1, hs]
            xc = xc + _shift_down(lxv, prev8, 2) * cw_ref[1:2, hs]
            xc = xc + _shift_down(lxv, prev8, 1) * cw_ref[2:3, hs]
            xc = xc + lxv * cw_ref[3:4, hs]
            xh = xc.astype(BF16)
            xc_ref[:, hs] = xh
            r = _sigmoid(_dot(xh, wr_ref[hd].astype(BF16)) + br_ref[:, hs])
            ig = _sigmoid(_dot(xh, wi_ref[hd].astype(BF16)) + bi_ref[:, hs])
            r_ref[:, hs] = r.astype(BF16)
            ig_ref[:, hs] = ig.astype(BF16)
            log_a = (-LRU_C * r) * sp[:, hs]
            a = jnp.exp(log_a)
            a_ref[:, hs] = a
            a2 = a * a
            beta2 = jnp.maximum((1.0 + a2) * jnp.tanh(-log_a), 1e-37)
            rsb = lax.rsqrt(beta2)
            beta = beta2 * rsb
            beta_ref[:, hs] = beta.astype(BF16)
            w4_ref[:, hs] = (a2 * rsb).astype(BF16)
            a_s[hd] = a
            u_s[hd] = jnp.where(row >= PAD_ROWS, beta * ig * xc, 0.0)
        carry[0:1, :] = _segment_scan(a_s, u_s, h_s, p_s, carry[0:1, :], tm, reverse=False)
        for hd in range(n_heads):
            hs = slice(hd * LANES, (hd + 1) * LANES)
            hl = h_s[hd]
            hl_ref[:, hs] = hl
            g = lg_ref[:, hs]
            y_ref[:, hs] = (hl * (g * _sigmoid(g))).astype(BF16)

    tile = pl.BlockSpec((tm, w), lambda i: (i, 0))
    prev = pl.BlockSpec((SUBLANES, w), lambda i: (jnp.maximum(i * per8 - 1, 0), 0))
    vec = pl.BlockSpec((1, w), lambda i: (0, 0))
    mat = pl.BlockSpec(wr.shape, lambda i: (0, 0, 0))
    f32_out = jax.ShapeDtypeStruct((tp, w), F32)
    bf16_out = jax.ShapeDtypeStruct((tp, w), BF16)
    return pl.pallas_call(
        body,
        name="lru_fwd",
        grid=(nt,),
        in_specs=[tile, prev, tile, pl.BlockSpec(cw.shape, lambda i: (0, 0)), vec, mat, vec, mat, vec, vec],
        out_specs=[tile] * 8,
        out_shape=[f32_out, bf16_out, bf16_out, bf16_out, bf16_out, f32_out, bf16_out, bf16_out],
        scratch_shapes=[pltpu.VMEM((w // LANES, tm, LANES), F32)] * 4 + [pltpu.VMEM((SUBLANES, w), F32)],
        compiler_params=pltpu.CompilerParams(dimension_semantics=("arbitrary",), vmem_limit_bytes=VMEM_LIMIT),
    )(lx, lx, lg, cw, cb, wr, br, wi, bi, lam)


def _ret_tables():
    log_g = jnp.log1p(-jnp.exp2(-5.0 - jnp.arange(HEADS, dtype=F32)))
    idx = jnp.arange(CHUNK, dtype=F32)
    diff = idx[:, None] - idx[None, :]
    dmask = jnp.where(diff[None] >= 0.0, jnp.exp(jnp.maximum(diff, 0.0)[None] * log_g[:, None, None]), 0.0)
    kdec = jnp.repeat(jnp.exp((CHUNK - 1.0 - idx)[:, None] * log_g[None, :]), QK_DIM, axis=1)
    qdec = jnp.repeat(jnp.exp((idx + 1.0)[:, None] * log_g[None, :]), QK_DIM, axis=1)
    g_chunk = jnp.exp(CHUNK * log_g)
    g_rows = jnp.repeat(g_chunk, QK_DIM).reshape(HEADS // 2, 2 * QK_DIM, 1)
    g_state = jnp.broadcast_to(g_rows, (HEADS // 2, 2 * QK_DIM, 2 * LANES))
    r_head = jnp.arange(2 * QK_DIM)[:, None] // QK_DIM
    c_head = jnp.arange(2 * LANES)[None, :] // LANES
    block_diag = (r_head == c_head).astype(F32)
    return dmask, qdec, kdec, g_state, block_diag


def _head_norm(o_h):
    mu = jnp.mean(o_h, axis=-1, keepdims=True)
    oc = o_h - mu
    var = jnp.mean(oc * oc, axis=-1, keepdims=True)
    rstd = lax.rsqrt(var + EPS)
    return oc * rstd, rstd


def _ret_fwd(qb, kb, qd, kd, vb, rg, gain, tables, tm):
    tp, d_qk = qb.shape
    d_ret = vb.shape[1]
    n_ch = tp // CHUNK
    cps = tm // CHUNK
    n_pairs = HEADS // 2
    dmask, _, _, g_state, block_diag = tables

    def body(q_ref, k_ref, qd_ref, kd_ref, v_ref, rg_ref, gain_ref, dm_ref, gs_ref, bd_ref,
             o_ref, y_ref, rp_ref, state):
        n = pl.program_id(0)

        @pl.when(n == 0)
        def _():
            state[...] = jnp.zeros_like(state)

        lane = lax.broadcasted_iota(jnp.int32, (CHUNK, LANES), 1)
        for ci in range(cps):
            rs = slice(ci * CHUNK, (ci + 1) * CHUNK)
            for p in range(n_pairs):
                qs = slice(p * LANES, (p + 1) * LANES)
                vs = slice(p * 2 * LANES, (p + 1) * 2 * LANES)
                qp, kb = q_ref[rs, qs], k_ref[rs, qs]
                vb = v_ref[rs, vs]
                qd, kd = qd_ref[rs, qs], kd_ref[rs, qs]
                st = state[p]
                st_b = st.astype(BF16)
                rp_ref[ci, p] = st_b
                cross = _dot(qd, st_b)
                for e in range(2):
                    hd = 2 * p + e
                    hs = slice(hd * LANES, (hd + 1) * LANES)
                    es = slice(e * LANES, (e + 1) * LANES)
                    qm = jnp.where((lane // QK_DIM) == e, qp, jnp.zeros_like(qp))
                    s = _dot_nt(qm, kb) * dm_ref[hd]
                    o_h = _dot(s.astype(BF16), vb[:, es]) + cross[:, es]
                    o_ref[rs, hs] = o_h
                    xhat, _ = _head_norm(o_h)
                    g = rg_ref[rs, hs]
                    y_ref[rs, hs] = ((xhat * gain_ref[:, hs]) * (g * _sigmoid(g))).astype(BF16)
                state[p] = gs_ref[p] * st + bd_ref[...] * _dot_tn(kd, vb)

    ch = lambda w: pl.BlockSpec((tm, w), lambda n: (n, 0))
    const2 = lambda a: pl.BlockSpec(a.shape, lambda n: (0, 0))
    const3 = lambda a: pl.BlockSpec(a.shape, lambda n: (0, 0, 0))
    return pl.pallas_call(
        body,
        name="ret_fwd",
        grid=(n_ch // cps,),
        in_specs=[ch(d_qk)] * 4 + [ch(d_ret), ch(d_ret), const2(gain), const3(dmask), const3(g_state),
                                   const2(block_diag)],
        out_specs=[ch(d_ret), ch(d_ret),
                   pl.BlockSpec((cps, n_pairs, 2 * QK_DIM, 2 * LANES), lambda n: (n, 0, 0, 0))],
        out_shape=[jax.ShapeDtypeStruct((tp, d_ret), F32), jax.ShapeDtypeStruct((tp, d_ret), BF16),
                   jax.ShapeDtypeStruct((n_ch, n_pairs, 2 * QK_DIM, 2 * LANES), BF16)],
        scratch_shapes=[pltpu.VMEM((n_pairs, 2 * QK_DIM, 2 * LANES), F32)],
        compiler_params=pltpu.CompilerParams(dimension_semantics=("arbitrary",), vmem_limit_bytes=VMEM_LIMIT),
    )(qb, kb, qd, kd, vb, rg, gain, dmask, g_state, block_diag)


def _out_proj_loss(y_lru, y_ret, hp, tgt, wo, gain_f, tm):
    tp, d = hp.shape
    w_lru = y_lru.shape[1]
    w_mix = wo.shape[0]
    nt, nb = tp // tm, tm // CHUNK

    def body(*refs):
        yl_ref, yr_ref, hp_ref = refs[:3]
        tb = refs[3:3 + nb]
        wo_ref, gf_ref = refs[3 + nb:5 + nb]
        dh2_ref, dyl_ref, dyr_ref, dwo_ref, dgf_ref, loss_ref = refs[5 + nb:]
        i = pl.program_id(0)

        @pl.when(i == 0)
        def _():
            dwo_ref[...] = jnp.zeros_like(dwo_ref)
            dgf_ref[...] = jnp.zeros_like(dgf_ref)
            loss_ref[...] = jnp.zeros_like(loss_ref)

        yl, yr = yl_ref[...], yr_ref[...]
        h2 = hp_ref[...] + _dot(yl, wo_ref[0:w_lru, :]) + _dot(yr, wo_ref[w_lru:w_mix, :])
        rinv = lax.rsqrt(jnp.mean(h2 * h2, axis=-1, keepdims=True) + EPS)
        nrm = h2 * rinv
        gf = gf_ref[...]
        tgt_v = jnp.concatenate([r[...] for r in tb], axis=0)
        row = lax.broadcasted_iota(jnp.int32, (tm, 1), 0) + i * tm
        err = jnp.where(row >= CHUNK, nrm * gf - tgt_v, 0.0)
        loss_ref[...] += 0.5 * jnp.sum(jnp.mean(err * err, axis=-1, keepdims=True))
        dout = err * (1.0 / d)
        dgf_ref[...] += jnp.sum(dout * nrm, axis=0, keepdims=True)
        dn = dout * gf
        dh2 = rinv * (dn - nrm * jnp.mean(dn * nrm, axis=-1, keepdims=True))
        dh2_ref[...] = dh2
        dh2b = dh2.astype(BF16)
        dyl_ref[...] = _dot_nt(dh2b, wo_ref[0:w_lru, :])
        dyr_ref[...] = _dot_nt(dh2b, wo_ref[w_lru:w_mix, :])
        dwo_ref[0:w_lru, :] += _dot_tn(yl, dh2b)
        dwo_ref[w_lru:w_mix, :] += _dot_tn(yr, dh2b)

    tile = lambda w: pl.BlockSpec((tm, w), lambda i: (i, 0))
    t_specs = [pl.BlockSpec((CHUNK, d), functools.partial(lambda i, b: (jnp.maximum(i * nb + b - 1, 0), 0), b=b))
               for b in range(nb)]
    return pl.pallas_call(
        body,
        name="out_proj_loss",
        grid=(nt,),
        in_specs=[tile(w_lru), tile(w_mix - w_lru), tile(d)] + t_specs +
                 [pl.BlockSpec(wo.shape, lambda i: (0, 0)), pl.BlockSpec(gain_f.shape, lambda i: (0, 0))],
        out_specs=[tile(d), tile(w_lru), tile(w_mix - w_lru), pl.BlockSpec(wo.shape, lambda i: (0, 0)),
                   pl.BlockSpec((1, d), lambda i: (0, 0)), pl.BlockSpec((SUBLANES, LANES), lambda i: (0, 0))],
        out_shape=[jax.ShapeDtypeStruct((tp, d), F32), jax.ShapeDtypeStruct((tp, w_lru), F32),
                   jax.ShapeDtypeStruct((tp, w_mix - w_lru), F32), jax.ShapeDtypeStruct(wo.shape, F32),
                   jax.ShapeDtypeStruct((1, d), F32), jax.ShapeDtypeStruct((SUBLANES, LANES), F32)],
        compiler_params=pltpu.CompilerParams(dimension_semantics=("arbitrary",), vmem_limit_bytes=VMEM_LIMIT),
    )(y_lru, y_ret, hp, *([tgt] * nb), wo, gain_f)


def _ret_bwd(qb, kb, qd, kd, vb, rg, o, rprev, dy, gain, cos_t, sin_t, tables, tm, ride=None):
    tp, d_qk = qb.shape
    d_ret = vb.shape[1]
    n_ch = tp // CHUNK
    cps = tm // CHUNK
    n_pairs = HEADS // 2
    dmask, qdec, kdec, g_state, block_diag = tables

    dmask_t = jnp.swapaxes(dmask, 1, 2)

    def body(q_ref, k_ref, qdb_ref, kdb_ref, v_ref, rg_ref, o_ref, rp_ref, dy_ref, gain_ref, cos_ref, sin_ref,
             dm_ref, dmt_ref, qd_ref, kd_ref, gs_ref, bd_ref, dq_ref, dk_ref, dv_ref, drg_ref, dgain_ref, dstate):
        n = pl.program_id(0)

        @pl.when(n == 0)
        def _():
            dstate[...] = jnp.zeros_like(dstate)
            dgain_ref[...] = jnp.zeros_like(dgain_ref)

        lane = lax.broadcasted_iota(jnp.int32, (CHUNK, LANES), 1)
        for ci in reversed(range(cps)):
            rs = slice(ci * CHUNK, (ci + 1) * CHUNK)
            dq_parts, dk_parts = [], []
            for p in range(n_pairs):
                qs = slice(p * LANES, (p + 1) * LANES)
                vs = slice(p * 2 * LANES, (p + 1) * 2 * LANES)
                do_parts = []
                for e in range(2):
                    hd = 2 * p + e
                    hs = slice(hd * LANES, (hd + 1) * LANES)
                    xhat, rstd = _head_norm(o_ref[rs, hs])
                    g = rg_ref[rs, hs]
                    sg = _sigmoid(g)
                    dyh = dy_ref[rs, hs]
                    gn = gain_ref[:, hs]
                    d_on = dyh * (g * sg)
                    drg_ref[rs, hs] = (dyh * (xhat * gn) * (sg * (1.0 + g * (1.0 - sg)))).astype(BF16)
                    dgain_ref[:, hs] += jnp.sum(d_on * xhat, axis=0, keepdims=True)
                    dxh = d_on * gn
                    do_parts.append(rstd * (dxh - jnp.mean(dxh, axis=-1, keepdims=True)
                                            - xhat * jnp.mean(dxh * xhat, axis=-1, keepdims=True)))
                do_b = jnp.concatenate(do_parts, axis=1).astype(BF16)
                qp, kb = q_ref[rs, qs], k_ref[rs, qs]
                vb = v_ref[rs, vs]
                qd, kd = qdb_ref[rs, qs], kdb_ref[rs, qs]
                dst = dstate[p]
                dst_b = dst.astype(BF16)
                dqp = _dot_nt(do_b, rp_ref[ci, p]) * qd_ref[:, qs]
                dkp = _dot_nt(vb, dst_b) * kd_ref[:, qs]
                dvp = _dot(kd, dst_b)
                dv_parts = []
                for e in range(2):
                    hd = 2 * p + e
                    es = slice(e * LANES, (e + 1) * LANES)
                    mine = (lane // QK_DIM) == e
                    qm = jnp.where(mine, qp, jnp.zeros_like(qp))
                    km = jnp.where(mine, kb, jnp.zeros_like(kb))
                    ds = (_dot_nt(do_b[:, es], vb[:, es]) * dm_ref[hd]).astype(BF16)
                    s_t = (_dot_nt(kb, qm) * dmt_ref[hd]).astype(BF16)
                    ds_t = (_dot_nt(vb[:, es], do_b[:, es]) * dmt_ref[hd]).astype(BF16)
                    dv_parts.append(dvp[:, es] + _dot(s_t, do_b[:, es]))
                    dqp = dqp + _dot(ds, km)
                    dkp = dkp + _dot(ds_t, qm)
                dv_ref[rs, vs] = jnp.concatenate(dv_parts, axis=1).astype(BF16)
                dstate[p] = gs_ref[p] * dst + bd_ref[...] * _dot_tn(qd, do_b)
                dq_parts.append(dqp)
                dk_parts.append(dkp)
            cos = _tile_lanes(cos_ref[rs, :], d_qk // LANES)
            sin = _tile_lanes(sin_ref[rs, :], d_qk // LANES)
            dq = jnp.concatenate(dq_parts, axis=1)
            dk = jnp.concatenate(dk_parts, axis=1) * (QK_DIM ** -0.5)
            dq_ref[rs, :] = (dq * cos + _rot_partner(dq * sin)).astype(BF16)
            dk_ref[rs, :] = (dk * cos + _rot_partner(dk * sin)).astype(BF16)

    last = n_ch // cps - 1
    ch = lambda w: pl.BlockSpec((tm, w), lambda n: (last - n, 0))
    const2 = lambda a: pl.BlockSpec(a.shape, lambda n: (0, 0))
    const3 = lambda a: pl.BlockSpec(a.shape, lambda n: (0, 0, 0))
    return _hosted_call(
        body, ride, n_ch // cps,
        name="ret_bwd",
        in_specs=[ch(d_qk)] * 4 + [ch(d_ret), ch(d_ret), ch(d_ret),
                  pl.BlockSpec((cps, n_pairs, 2 * QK_DIM, 2 * LANES), lambda n: (last - n, 0, 0, 0)),
                  ch(d_ret), const2(gain), ch(LANES), ch(LANES),
                  const3(dmask), const3(dmask_t), const2(qdec), const2(kdec), const3(g_state), const2(block_diag)],
        out_specs=[ch(d_qk), ch(d_qk), ch(d_ret), ch(d_ret), pl.BlockSpec((1, d_ret), lambda n: (0, 0))],
        out_shape=[jax.ShapeDtypeStruct((tp, d_qk), BF16), jax.ShapeDtypeStruct((tp, d_qk), BF16),
                   jax.ShapeDtypeStruct((tp, d_ret), BF16), jax.ShapeDtypeStruct((tp, d_ret), BF16),
                   jax.ShapeDtypeStruct((1, d_ret), F32)],
        scratch_shapes=[pltpu.VMEM((n_pairs, 2 * QK_DIM, 2 * LANES), F32)],
        args=(qb, kb, qd, kd, vb, rg, o, rprev, dy, gain, cos_t, sin_t, dmask, dmask_t, qdec, kdec, g_state,
              block_diag),
    )


def _lru_bwd(lx, lg, hl, dy, saved, cw, wr, wi, lam, tm, ride=None):
    tp, w = lx.shape
    nt = tp // tm
    per8 = tm // SUBLANES
    n_heads = wr.shape[0]

    def body(lx_ref, lg_ref, hl_ref, hlp_ref, dy_ref, xc_ref, r_ref, ig_ref, a_ref, beta_ref, w4_ref,
             cw_ref, wr_ref, wi_ref, lam_ref,
             dlx_ref, dlg_ref, dcw_ref, dcb_ref, dwr_ref, dbr_ref, dwi_ref, dbi_ref, dlam_ref,
             g_s, b_s, carry, dxc_next, a_next):
        i = pl.program_id(0)
        first_tile = i == nt - 1

        @pl.when(i == 0)
        def _():
            carry[...] = jnp.zeros_like(carry)
            dxc_next[...] = jnp.zeros_like(dxc_next)
            a_next[...] = jnp.zeros_like(a_next)
            for r in (dcw_ref, dcb_ref, dwr_ref, dbr_ref, dwi_ref, dbi_ref, dlam_ref):
                r[...] = jnp.zeros_like(r)

        heads = [slice(hd * LANES, (hd + 1) * LANES) for hd in range(n_heads)]
        for hd, hs in enumerate(heads):
            g = lg_ref[:, hs]
            sg = _sigmoid(g)
            dyv = dy_ref[:, hs]
            dlg_ref[:, hs] = (dyv * hl_ref[:, hs] * (sg * (1.0 + g * (1.0 - sg)))).astype(BF16)
            g_s[hd] = dyv * (g * sg)
            b_s[hd] = _shift_up(a_ref[:, hs], a_next[:, hs], 1)
        carry[0:1, :] = _segment_scan(b_s, g_s, g_s, b_s, carry[0:1, :], tm, reverse=True)
        a_next[...] = a_ref[0:SUBLANES, :]
        row = lax.broadcasted_iota(jnp.int32, (tm, 1), 0) + (nt - 1 - i) * tm
        lam_v = lam_ref[...]
        dlam_scale = LRU_C * _sigmoid(-lam_v)
        dr_scale = -LRU_C * _softplus_neg(lam_v)
        for hd, hs in enumerate(heads):
            a, xh = a_ref[:, hs], xc_ref[:, hs]
            beta, r, ig, xc = (ref[:, hs].astype(F32) for ref in (beta_ref, r_ref, ig_ref, xc_ref))
            dh = g_s[hd]
            hprev = _shift_down(hl_ref[:, hs], jnp.where(first_tile, 0.0, hlp_ref[:, hs]), 1)
            du = jnp.where(row >= PAD_ROWS, dh, 0.0)
            dbeta = du * ig * xc
            d_ig = du * beta * xc
            dxc = du * beta * ig
            dloga = (dh * hprev) * a - dbeta * w4_ref[:, hs].astype(F32)
            dlam_ref[:, hs] += jnp.sum(dloga * r, axis=0, keepdims=True) * dlam_scale[:, hs]
            dpr = (dloga * dr_scale[:, hs]) * r * (1.0 - r)
            dpi = d_ig * ig * (1.0 - ig)
            dbr_ref[:, hs] += jnp.sum(dpr, axis=0, keepdims=True)
            dbi_ref[:, hs] += jnp.sum(dpi, axis=0, keepdims=True)
            dprh, dpih = dpr.astype(BF16), dpi.astype(BF16)
            dwr_ref[hd] += _dot_tn(xh, dprh)
            dwi_ref[hd] += _dot_tn(xh, dpih)
            dxc = dxc + _dot_nt(dprh, wr_ref[hd].astype(BF16)) + _dot_nt(dpih, wi_ref[hd].astype(BF16))
            nxt = dxc_next[:, hs]
            up1, up2, up3 = _shift_up(dxc, nxt, 1), _shift_up(dxc, nxt, 2), _shift_up(dxc, nxt, 3)
            dlx = dxc * cw_ref[3:4, hs]
            dlx = dlx + up1 * cw_ref[2:3, hs]
            dlx = dlx + up2 * cw_ref[1:2, hs]
            dlx = dlx + up3 * cw_ref[0:1, hs]
            dlx_ref[:, hs] = dlx.astype(BF16)
            dxc_next[:, hs] = dxc[0:SUBLANES]
            lxv = lx_ref[:, hs]
            dcb_ref[:, hs] += jnp.sum(dxc, axis=0, keepdims=True)
            dcw_ref[0:1, hs] += jnp.sum(up3 * lxv, axis=0, keepdims=True)
            dcw_ref[1:2, hs] += jnp.sum(up2 * lxv, axis=0, keepdims=True)
            dcw_ref[2:3, hs] += jnp.sum(up1 * lxv, axis=0, keepdims=True)
            dcw_ref[3:4, hs] += jnp.sum(dxc * lxv, axis=0, keepdims=True)

    last = nt - 1
    tile = pl.BlockSpec((tm, w), lambda i: (last - i, 0))
    prev = pl.BlockSpec((SUBLANES, w), lambda i: (jnp.maximum((last - i) * per8 - 1, 0), 0))
    vec = pl.BlockSpec((1, w), lambda i: (0, 0))
    mat = pl.BlockSpec(wr.shape, lambda i: (0, 0, 0))
    cwb = pl.BlockSpec(cw.shape, lambda i: (0, 0))
    return _hosted_call(
        body, ride, nt,
        name="lru_bwd",
        in_specs=[tile, tile, tile, prev, tile] + [tile] * 6 + [cwb, mat, mat, vec],
        out_specs=[tile, tile, cwb, vec, mat, vec, mat, vec, vec],
        out_shape=[jax.ShapeDtypeStruct((tp, w), BF16), jax.ShapeDtypeStruct((tp, w), BF16),
                   jax.ShapeDtypeStruct(cw.shape, F32), jax.ShapeDtypeStruct((1, w), F32),
                   jax.ShapeDtypeStruct(wr.shape, F32), jax.ShapeDtypeStruct((1, w), F32),
                   jax.ShapeDtypeStruct(wr.shape, F32), jax.ShapeDtypeStruct((1, w), F32),
                   jax.ShapeDtypeStruct((1, w), F32)],
        scratch_shapes=[pltpu.VMEM((w // LANES, tm, LANES), F32)] * 2 + [pltpu.VMEM((SUBLANES, w), F32)] * 3,
        args=(lx, lg, hl, hl, dy, *saved, cw, wr, wi, lam),
    )


def _in_proj_dw(dparts, hp, gain, wg_shape, ride=None):
    tp, d = hp.shape
    n_ch = tp // CHUNK
    per = next(p for p in (4, 2, 5, 3, 1) if (n_ch - 1) % p == 0)
    n_steps = 1 + (n_ch - 1) // per
    widths = [p.shape[1] for p in dparts]
    segs = _proj_segments(widths[0], widths[2], widths[4], wg_shape[2])

    def body(*refs):
        dp = [refs[p * per:(p + 1) * per] for p in range(6)]
        hp_b = refs[6 * per:7 * per]
        g_ref, dwg_ref, acc, sem = refs[7 * per:]
        i = pl.program_id(0)

        def accumulate(blocks):
            h = jnp.concatenate([hp_b[b][...] for b in blocks], axis=0)
            rinv = lax.rsqrt(jnp.mean(h * h, axis=-1, keepdims=True) + EPS)
            u = ((h * rinv) * g_ref[...]).astype(BF16)
            for p_refs, parts in zip(dp, segs):
                for jj, inner, off, take in parts:
                    seg = jnp.concatenate([p_refs[b][:, off:off + take] for b in blocks], axis=0)
                    acc[jj, :, inner:inner + take] += _dot_tn(u, seg)

        @pl.when(i == 0)
        def _():
            acc[...] = jnp.zeros_like(acc)
            accumulate([0])

        @pl.when(i > 0)
        def _():
            accumulate(list(range(per)))

        @pl.when(i == n_steps - 1)
        def _():
            cp = pltpu.make_async_copy(acc, dwg_ref, sem)
            cp.start()
            cp.wait()

    def blocks(w):
        return [pl.BlockSpec((CHUNK, w), functools.partial(
            lambda i, b: (jnp.where(i == 0, b, per * (i - 1) + 1 + b), 0), b=b)) for b in range(per)]

    in_specs, args = [], []
    for a, w in list(zip(dparts, widths)) + [(hp, d)]:
        in_specs += blocks(w)
        args += [a] * per
    outs, rides = _hosted_call(
        body, ride, n_steps,
        name="in_proj_dw",
        in_specs=in_specs + [pl.BlockSpec(gain.shape, lambda i: (0, 0))],
        out_specs=[ANY_SPEC],
        out_shape=[jax.ShapeDtypeStruct(wg_shape, F32)],
        scratch_shapes=[pltpu.VMEM(wg_shape, F32), pltpu.SemaphoreType.DMA],
        args=(*args, gain),
    )
    return outs[0], rides


def _in_proj_dx(dparts, hp, dh2, gain, wg, s_len, tm, ride=None):
    tp, d = hp.shape
    nt = tp // tm
    widths = [p.shape[1] for p in dparts]
    segs = _proj_segments(widths[0], widths[2], widths[4], wg.shape[2])

    def body(*refs):
        dp = refs[:6]
        hp_ref, dh2_ref, g_ref, w_ref = refs[6:10]
        gx_ref, dmeta_ref, dg_ref = refs[10:13]
        stage, sems = refs[13:]
        i = pl.program_id(0)

        @pl.when(i == 0)
        def _():
            dg_ref[...] = jnp.zeros_like(dg_ref)

        h = hp_ref[...]
        rinv = lax.rsqrt(jnp.mean(h * h, axis=-1, keepdims=True) + EPS)
        nrm = h * rinv
        gv = g_ref[...]
        du = jnp.zeros((tm, d), F32)
        for p_ref, parts in zip(dp, segs):
            for jj, inner, off, take in parts:
                du = du + _dot_nt(p_ref[:, off:off + take], w_ref[jj, :, inner:inner + take])
        dg_ref[...] += jnp.sum(du * nrm, axis=0, keepdims=True)
        dn = du * gv
        dh = dh2_ref[...] + rinv * (dn - nrm * jnp.mean(dn * nrm, axis=-1, keepdims=True))

        def first_copy():
            return pltpu.make_async_copy(stage.at[0, pl.ds(CHUNK, tm - CHUNK), :],
                                         gx_ref.at[pl.ds(0, tm - CHUNK), :], sems.at[0])

        def tile_copy(slot, start):
            return pltpu.make_async_copy(stage.at[slot], gx_ref.at[pl.ds(start, tm), :], sems.at[slot])

        @pl.when(i == 0)
        def _():
            dmeta_ref[...] = dh[PAD_ROWS:CHUNK]
            stage[0] = dh
            first_copy().start()

        @pl.when(i > 0)
        def _():
            slot = 1 + i % 2

            @pl.when(i >= 3)
            def _():
                tile_copy(slot, 0).wait()

            stage[slot] = dh
            tile_copy(slot, pl.multiple_of(i * tm - CHUNK, CHUNK)).start()

        @pl.when(i == nt - 1)
        def _():
            first_copy().wait()
            for step in (nt - 2, nt - 1):
                if step >= 1:
                    tile_copy(1 + step % 2, 0).wait()

    tile = lambda w: pl.BlockSpec((tm, w), lambda i: (i, 0))
    return _hosted_call(
        body, ride, nt,
        name="in_proj_dx",
        in_specs=[tile(w) for w in widths] + [tile(d), tile(d), pl.BlockSpec(gain.shape, lambda i: (0, 0)),
                                              pl.BlockSpec(wg.shape, lambda i: (0, 0, 0))],
        out_specs=[ANY_SPEC, pl.BlockSpec((N_META, d), lambda i: (0, 0)), pl.BlockSpec((1, d), lambda i: (0, 0))],
        out_shape=[jax.ShapeDtypeStruct((s_len, d), F32), jax.ShapeDtypeStruct((N_META, d), F32),
                   jax.ShapeDtypeStruct((1, d), F32)],
        scratch_shapes=[pltpu.VMEM((3, tm, d), F32), pltpu.SemaphoreType.DMA((3,))],
        args=(*dparts, hp, dh2, gain, wg),
    )


def _pair_sum(buf, recv, c_arr, tr, name):
    _, rows, cols = buf.shape

    def body(c_ref, mine_ref, got_ref, out_ref):
        out_ref[...] = (mine_ref[...] + got_ref[...]).astype(BF16)

    grid_spec = pltpu.PrefetchScalarGridSpec(
        num_scalar_prefetch=1,
        grid=(N_CHIPS, rows // tr),
        in_specs=[pl.BlockSpec((1, tr, cols), lambda jj, r, c_ref: (2 * jj + c_ref[0], r, 0)),
                  pl.BlockSpec((1, tr, cols), lambda jj, r, c_ref: (jj, r, 0))],
        out_specs=pl.BlockSpec((1, tr, cols), lambda jj, r, c_ref: (jj, r, 0)),
    )
    return pl.pallas_call(
        body,
        name=name,
        grid_spec=grid_spec,
        out_shape=jax.ShapeDtypeStruct((N_CHIPS, rows, cols), BF16),
    )(c_arr, buf, recv)


def _pair_exchange_sum(buf, c_arr, tr, name):
    _, rows, cols = buf.shape
    per = rows // tr

    def body(c_ref, src_ref, mine_ref, out_ref, got, send_sems, recv_sems):
        jj, r = pl.program_id(0), pl.program_id(1)
        x, y, c, _ = _position()
        copies = [_remote(src_ref.at[2 * k + 1 - c], got.at[k], send_sems, recv_sems, k, (x, y, 1 - c))
                  for k in range(N_CHIPS)]

        @pl.when((jj == 0) & (r == 0))
        def _():
            for cp in copies:
                cp.start()

        for k in range(N_CHIPS):
            @pl.when((jj == k) & (r == 0))
            def _():
                copies[k].wait_recv()

        rows_r = pl.ds(pl.multiple_of(r * tr, tr), tr)
        out_ref[0] = (mine_ref[0] + got[jj, rows_r, :]).astype(BF16)

        @pl.when((jj == N_CHIPS - 1) & (r == per - 1))
        def _():
            for cp in copies:
                cp.wait_send()

    grid_spec = pltpu.PrefetchScalarGridSpec(
        num_scalar_prefetch=1,
        grid=(N_CHIPS, per),
        in_specs=[ANY_SPEC, pl.BlockSpec((1, tr, cols), lambda jj, r, c_ref: (2 * jj + c_ref[0], r, 0))],
        out_specs=pl.BlockSpec((1, tr, cols), lambda jj, r, c_ref: (jj, r, 0)),
        scratch_shapes=[pltpu.VMEM((N_CHIPS, rows, cols), F32), pltpu.SemaphoreType.DMA((N_CHIPS,)),
                        pltpu.SemaphoreType.DMA((N_CHIPS,))],
    )
    return pl.pallas_call(
        body,
        name=name,
        grid_spec=grid_spec,
        out_shape=jax.ShapeDtypeStruct((N_CHIPS, rows, cols), BF16),
        compiler_params=pltpu.CompilerParams(dimension_semantics=("arbitrary", "arbitrary"),
                                             vmem_limit_bytes=VMEM_LIMIT),
    )(c_arr, buf, buf)


def _chip_sum(mine, got, j_arr, tr, name, loss_part=None):
    _, rows, cols = got.shape
    extra = [] if loss_part is None else [loss_part]

    def body(j_ref, mine_ref, got_ref, *rest):
        out_ref = rest[-1]
        j = j_ref[0]
        acc = None
        for jj in range(N_CHIPS):
            term = jnp.where(j == jj, mine_ref[0], got_ref[jj]).astype(F32)
            acc = term if acc is None else acc + term
        out_ref[...] = acc
        if loss_part is not None:
            out_ref[ROW_LOSS:ROW_LOSS + 1, :] = rest[0][0:1, :]

    grid_spec = pltpu.PrefetchScalarGridSpec(
        num_scalar_prefetch=1,
        grid=(rows // tr,),
        in_specs=[pl.BlockSpec((1, tr, cols), lambda r, j_ref: (j_ref[0], r, 0)),
                  pl.BlockSpec((N_CHIPS, tr, cols), lambda r, j_ref: (0, r, 0))] +
                 [pl.BlockSpec(e.shape, lambda r, j_ref: (0, 0)) for e in extra],
        out_specs=pl.BlockSpec((tr, cols), lambda r, j_ref: (r, 0)),
    )
    return pl.pallas_call(
        body,
        name=name,
        grid_spec=grid_spec,
        out_shape=jax.ShapeDtypeStruct((rows, cols), F32),
    )(j_arr, mine, got, *extra)


def _finish_exchange(f_in, f_small):
    def body(fin_ref, fs_ref, rin_ref, os_ref, send_sems, recv_sems, local_sem):
        x, y, c, chips = _position()
        j = 2 * x + y
        me = 2 * j + c
        sibling = (x, y, 1 - c)
        local = pltpu.make_async_copy(fs_ref, os_ref.at[me], local_sem)
        local.start()

        def copy(k, src, dst, to):
            return _remote(src, dst, send_sems, recv_sems, k, to)

        first = [copy(0, fin_ref, rin_ref, sibling), copy(1, fs_ref, os_ref.at[me], sibling)]
        first += [copy(2 + k, fs_ref, os_ref.at[me], (cx, cy, c)) for k, (cx, cy) in enumerate(chips)]
        for cp in first:
            cp.start()
        passed = []
        for k, (cx, cy) in enumerate(chips):
            unit = 2 * (2 * cx + cy) + c
            copy(2 + k, fs_ref, os_ref.at[unit], sibling).wait_recv()
            fwd = copy(5 + k, os_ref.at[unit], os_ref.at[unit], sibling)
            fwd.start()
            passed.append(fwd)
        copy(0, fin_ref, rin_ref, sibling).wait_recv()
        copy(1, fs_ref, os_ref.at[2 * j + 1 - c], sibling).wait_recv()
        for k, (cx, cy) in enumerate(chips):
            unit = 2 * (2 * cx + cy) + 1 - c
            copy(5 + k, fs_ref, os_ref.at[unit], sibling).wait_recv()
        for cp in first + passed:
            cp.wait_send()
        local.wait()

    return pl.pallas_call(
        body,
        name="grad_finish_exchange",
        in_specs=[ANY_SPEC] * 2,
        out_specs=[ANY_SPEC] * 2,
        out_shape=[jax.ShapeDtypeStruct(f_in.shape, F32), jax.ShapeDtypeStruct((N_DEV,) + f_small.shape, F32)],
        scratch_shapes=[pltpu.SemaphoreType.DMA((8,)), pltpu.SemaphoreType.DMA((8,)), pltpu.SemaphoreType.DMA],
    )(f_in, f_small)


def _adamw_math(w, g, m, v):
    m = ADAM_B1 * m + (1.0 - ADAM_B1) * g
    v = ADAM_B2 * v + (1.0 - ADAM_B2) * (g * g)
    m_hat = m / (1.0 - ADAM_B1 ** ADAM_STEP)
    v_hat = v / (1.0 - ADAM_B2 ** ADAM_STEP)
    delta = -ADAM_LR * (m_hat / (jnp.sqrt(v_hat) + ADAM_EPS) + ADAM_WD * w)
    return delta, m, v


def _adamw_big(w, g_mine, g_sib, m, v, c_arr, tr, name):
    rows, cols = w.shape
    half = rows // 2
    per = half // tr

    def body(c_ref, w_ref, gm_ref, gs_ref, m_ref, v_ref, g_ref, d_ref, mo_ref, vo_ref):
        g = jnp.where(pl.program_id(0) == c_ref[0], gm_ref[...], gs_ref[...])
        g_ref[...] = g
        d_ref[...], mo_ref[...], vo_ref[...] = _adamw_math(w_ref[...], g, m_ref[...], v_ref[...])

    full = pl.BlockSpec((tr, cols), lambda h, r, c_ref: (h * per + r, 0))
    unit = pl.BlockSpec((tr, cols), lambda h, r, c_ref: (r, 0))
    grid_spec = pltpu.PrefetchScalarGridSpec(
        num_scalar_prefetch=1,
        grid=(2, per),
        in_specs=[full, unit, unit, full, full],
        out_specs=[full] * 4,
    )
    return pl.pallas_call(
        body,
        name=name,
        grid_spec=grid_spec,
        out_shape=[jax.ShapeDtypeStruct(w.shape, F32)] * 4,
    )(c_arr, w, g_mine, g_sib, m, v)


def _adamw_small(j_arr, packed, params):
    names = list(params)
    n = len(names)
    vec_names = ["norm_gain", "conv_b", "b_rg", "b_ig", "lru_lambda", "ret_norm_gain", "final_norm_gain"]

    def body(j_ref, pk_ref, *refs):
        ins = refs[:3 * n]
        outs = refs[3 * n:]
        j = j_ref[0]

        def shard(row, rows):
            return jnp.concatenate([pk_ref[2 * j, row:row + rows, :], pk_ref[2 * j + 1, row:row + rows, :]], axis=1)

        def tail_sum(unit, row, rows):
            start = pl.multiple_of(UNIT_ROWS + TAIL_ROWS * unit + row, SUBLANES)
            total = pk_ref[0, pl.ds(start, rows), :]
            for dev in range(1, N_DEV):
                total = total + pk_ref[dev, pl.ds(start, rows), :]
            return total

        for idx, name in enumerate(names):
            if name == "w_rg":
                g = pk_ref[:, ROW_WR:ROW_WR + LANES, :]
            elif name == "w_ig":
                g = pk_ref[:, ROW_WI:ROW_WI + LANES, :]
            elif name == "meta_tokens":
                g = jnp.concatenate([tail_sum(2 * j, 0, N_META), tail_sum(2 * j + 1, 0, N_META)], axis=1)
            elif name == "norm_gain":
                g = jnp.concatenate([tail_sum(u, N_META, SUBLANES)[0:1] for u in range(N_DEV)], axis=1)
            elif name == "conv_w":
                g = shard(ROW_CONV, 4)
            else:
                row = ROW_VEC + vec_names.index(name)
                g = jnp.concatenate([pk_ref[u, row:row + 1, :] for u in range(N_DEV)], axis=1)
            w_ref, m_ref, v_ref = ins[3 * idx:3 * idx + 3]
            delta, m, v = _adamw_math(w_ref[...], g, m_ref[...], v_ref[...])
            g_ref, d_ref, mo_ref, vo_ref = outs[4 * idx:4 * idx + 4]
            g_ref[...], d_ref[...], mo_ref[...], vo_ref[...] = g, delta, m, v
        total = pk_ref[0, ROW_LOSS:ROW_LOSS + 1, :]
        for u in range(1, N_DEV):
            total = total + pk_ref[u, ROW_LOSS:ROW_LOSS + 1, :]
        outs[4 * n][...] = jnp.broadcast_to(total, (SUBLANES, LANES))

    flat_in, out_shape = [], []
    for name in names:
        w, m, v = params[name]
        flat_in += [w, m, v]
        out_shape += [jax.ShapeDtypeStruct(w.shape, F32)] * 4
    out_shape.append(jax.ShapeDtypeStruct((SUBLANES, LANES), F32))
    res = pl.pallas_call(
        body,
        name="adamw_small",
        in_specs=[SMEM_SPEC, VMEM_SPEC] + [VMEM_SPEC] * (3 * n),
        out_specs=[VMEM_SPEC] * (4 * n + 1),
        out_shape=out_shape,
    )(j_arr, packed, *flat_in)
    return {name: tuple(res[4 * idx:4 * idx + 4]) for idx, name in enumerate(names)}, res[4 * n][0, 0]


def _units(a):
    rows = a.shape[0]
    return jnp.transpose(a.reshape(rows, N_DEV, LANES), (1, 0, 2))


def kernel(x, meta_tokens, norm_gain, w_in, conv_w, conv_b, w_rg, b_rg, w_ig, b_ig, lru_lambda, ret_norm_gain, w_out, final_norm_gain, loss_target, m_meta_tokens, m_norm_gain, m_w_in, m_conv_w, m_conv_b, m_w_rg, m_b_rg, m_w_ig, m_b_ig, m_lru_lambda, m_ret_norm_gain, m_w_out, m_final_norm_gain, v_meta_tokens, v_norm_gain, v_w_in, v_conv_w, v_conv_b, v_w_rg, v_b_rg, v_w_ig, v_b_ig, v_lru_lambda, v_ret_norm_gain, v_w_out, v_final_norm_gain):
    s_len, d = x.shape[1], x.shape[2]
    d_lru = w_rg.shape[1] * w_rg.shape[2]
    d_ret = ret_norm_gain.shape[1]
    d_qk = HEADS * QK_DIM
    tp = s_len + CHUNK
    tm = TOKEN_TILE
    assert tp % tm == 0 and d_lru == HEADS * LANES and d_ret == HEADS * LANES
    ax, ay, ac = lax.axis_index("x"), lax.axis_index("y"), lax.axis_index("c")
    c_arr = jnp.reshape(ac, (1,)).astype(jnp.int32)
    j_arr = jnp.reshape(2 * ax + ay, (1,)).astype(jnp.int32)

    small = jnp.concatenate([meta_tokens, conv_w[0], jnp.zeros((4, meta_tokens.shape[1]), F32)], axis=0)
    wg, sg = _gather_weights(w_in[0], small)
    cols = sg.shape[2]
    meta_full = jnp.transpose(sg[:, :N_META, :], (1, 0, 2)).reshape(N_META, N_CHIPS * cols)
    cw_full = jnp.transpose(sg[:, N_META:N_META + 4, :], (1, 0, 2)).reshape(4, N_CHIPS * cols)
    cw8 = jnp.concatenate([cw_full, jnp.zeros((4, cw_full.shape[1]), F32)], axis=0)

    half = QK_DIM // 2
    inv = ROPE_BASE ** (-jnp.arange(half, dtype=F32) / half)
    pos = (jnp.arange(tp) - PAD_ROWS).astype(F32)
    ang = pos[:, None] * inv[None, :]
    cos_t = jnp.tile(jnp.cos(ang), (1, LANES // half))
    sign = jnp.where((jnp.arange(LANES) % QK_DIM) < half, -1.0, 1.0).astype(F32)
    sin_t = jnp.tile(jnp.sin(ang), (1, LANES // half)) * sign[None, :]
    tables = _ret_tables()
    gain_f = final_norm_gain.reshape(1, d)

    hp, lx, lg, *qkv, rg, wo4 = _in_proj(x[0], meta_full, norm_gain, wg, cos_t, sin_t, w_out[0], tables[1], tables[2],
                                         tm, d_lru, d_qk, d_ret)
    wo = wo4.reshape(N_CHIPS * wo4.shape[1], wo4.shape[2])
    hl, y_lru, *lru_saved = _lru_fwd(lx, lg, cw8, conv_b, w_rg[0], b_rg, w_ig[0], b_ig, lru_lambda, tm)
    o, y_ret, rprev = _ret_fwd(*qkv, rg, ret_norm_gain, tables, tm)
    dh2, dy_lru, dy_ret, dwo, dgf, loss_acc = _out_proj_loss(y_lru, y_ret, hp, loss_target[0], wo, gain_f, tm)

    g_out = dwo.reshape(N_DEV, dwo.shape[0] // N_DEV, dwo.shape[1])
    (dq, dk, dv, drg, dgain), (r_out,) = _ret_bwd(*qkv, rg, o, rprev, dy_ret, ret_norm_gain, cos_t, sin_t, tables,
                                                 tm, ride=_pair_ride([g_out]))
    q_out = _pair_sum(g_out, r_out, c_arr, 128, "grad_pair_sum_out")
    (dlx, dlg, dcw, dcb, dwr, dbr, dwi, dbi, dlam), (e_out,) = _lru_bwd(
        lx, lg, hl, dy_lru, lru_saved, cw8, w_rg[0], w_ig[0], lru_lambda, tm, ride=_chip_ride([q_out]))
    f_out = _chip_sum(q_out, e_out, j_arr, 128, "grad_chip_sum_out")
    zero_row = jnp.zeros((1, d), F32)
    vecs = [zero_row, dcb, dbr, dbi, dlam, dgain, dgf]
    g_small = jnp.concatenate([dwr, dwi, jnp.zeros((N_DEV, N_META, LANES), F32), _units(dcw[0:4])]
                              + [_units(a) for a in vecs]
                              + [jnp.zeros((N_DEV, UNIT_ROWS - ROW_VEC - N_VEC, LANES), F32)], axis=1)
    dparts = [dlx, dlg, dq, dk, dv, drg]
    dwg, (s_out, r_small) = _in_proj_dw(dparts, hp, norm_gain, wg.shape,
                                        ride=_join_rides(_sibling_ride([f_out]), _pair_ride([g_small])))
    g_in = dwg.reshape(N_DEV, dwg.shape[1] // 2, dwg.shape[2])
    q_in = _pair_exchange_sum(g_in, c_arr, 128, "grad_pair_exchange_sum_in")
    q_small = _pair_sum(g_small, r_small, c_arr, UNIT_ROWS, "grad_pair_sum_small")
    (grad_x, dmeta, dg1), (e_in, e_small) = _in_proj_dx(dparts, hp, dh2, norm_gain, wg, s_len, tm,
                                                        ride=_chip_ride([q_in, q_small]))
    f_in = _chip_sum(q_in, e_in, j_arr, 128, "grad_chip_sum_in")
    f_small = _chip_sum(q_small, e_small, j_arr, UNIT_ROWS, "grad_chip_sum_small", loss_part=loss_acc)
    tail = jnp.concatenate([_units(dmeta), _units(dg1), jnp.zeros((N_DEV, TAIL_ROWS - N_META - 1, LANES), F32)],
                           axis=1).reshape(N_DEV * TAIL_ROWS, LANES)
    s_in, o_small = _finish_exchange(f_in, jnp.concatenate([f_small, tail], axis=0))

    res_in = _adamw_big(w_in[0], f_in, s_in, m_w_in[0], v_w_in[0], c_arr, 256, "adamw_w_in")
    res_out = _adamw_big(w_out[0], f_out, s_out, m_w_out[0], v_w_out[0], c_arr, 256, "adamw_w_out")
    small_params = {
        "meta_tokens": (meta_tokens, m_meta_tokens, v_meta_tokens),
        "norm_gain": (norm_gain, m_norm_gain, v_norm_gain),
        "conv_w": (conv_w[0], m_conv_w[0], v_conv_w[0]),
        "conv_b": (conv_b, m_conv_b, v_conv_b),
        "w_rg": (w_rg[0], m_w_rg[0], v_w_rg[0]),
        "b_rg": (b_rg, m_b_rg, v_b_rg),
        "w_ig": (w_ig[0], m_w_ig[0], v_w_ig[0]),
        "b_ig": (b_ig, m_b_ig, v_b_ig),
        "lru_lambda": (lru_lambda, m_lru_lambda, v_lru_lambda),
        "ret_norm_gain": (ret_norm_gain, m_ret_norm_gain, v_ret_norm_gain),
        "final_norm_gain": (gain_f, m_final_norm_gain.reshape(1, d), v_final_norm_gain.reshape(1, d)),
    }
    res, loss = _adamw_small(j_arr, o_small, small_params)
    res["w_in"] = tuple(res_in)
    res["w_out"] = tuple(res_out)

    order = ["meta_tokens", "norm_gain", "w_in", "conv_w", "conv_b", "w_rg", "b_rg", "w_ig", "b_ig", "lru_lambda",
             "ret_norm_gain", "w_out", "final_norm_gain"]
    shapes = {"w_in": w_in.shape, "conv_w": conv_w.shape, "w_rg": w_rg.shape, "w_ig": w_ig.shape,
              "w_out": w_out.shape, "final_norm_gain": final_norm_gain.shape}
    outs = [loss, grad_x.reshape(x.shape)]
    for kind in range(4):
        for name in order:
            a = res[name][kind]
            outs.append(a.reshape(shapes[name]) if name in shapes else a)
    return tuple(outs)
```

```python
import functools

import jax
import jax.numpy as jnp
from jax import lax
from jax.experimental import pallas as pl
from jax.experimental.pallas import tpu as pltpu

F32 = jnp.float32
BF16 = jnp.bfloat16

N_META = 16
CHUNK = 128
PAD_ROWS = CHUNK - N_META
HEADS = 8
QK_DIM = 64
LANES = 128
SUBLANES = 8
LRU_C = 8.0
EPS = 1e-6
ROPE_BASE = 10000.0
ADAM_LR = 0.001
ADAM_B1 = 0.9
ADAM_B2 = 0.999
ADAM_EPS = 1e-08
ADAM_WD = 0.01
ADAM_STEP = 10
N_CHIPS = 4
N_DEV = 8
TOKEN_TILE = 384
VMEM_LIMIT = 58 * 1024 * 1024
MESH = pl.DeviceIdType.MESH

VMEM_SPEC = pl.BlockSpec(memory_space=pltpu.VMEM)
SMEM_SPEC = pl.BlockSpec(memory_space=pltpu.SMEM)
ANY_SPEC = pl.BlockSpec(memory_space=pl.ANY)

ROW_WR, ROW_WI, ROW_META, ROW_CONV, ROW_VEC, UNIT_ROWS = 0, 128, 256, 272, 276, 288
N_VEC = 7
VEC_NAMES = ["norm_gain", "conv_b", "b_rg", "b_ig", "lru_lambda", "ret_norm_gain", "final_norm_gain"]
ROW_LOSS = ROW_VEC + N_VEC
TAIL_ROWS = 24


def _dot(a, b):
    return jnp.dot(a, b, preferred_element_type=F32)


def _dot_nt(a, b):
    return lax.dot_general(a, b, (((1,), (1,)), ((), ())), preferred_element_type=F32)


def _dot_tn(a, b):
    return lax.dot_general(a, b, (((0,), (0,)), ((), ())), preferred_element_type=F32)


def _sigmoid(x):
    return 0.5 * jnp.tanh(0.5 * x) + 0.5


def _shift_down(x, prev8, s):
    rolled = pltpu.roll(x, s, 0)
    rows = lax.broadcasted_iota(jnp.int32, (SUBLANES, x.shape[1]), 0)
    top = jnp.where(rows < s, pltpu.roll(prev8, s, 0), rolled[0:SUBLANES])
    return jnp.concatenate([top, rolled[SUBLANES:]], axis=0)


def _shift_up(x, next8, s):
    n = x.shape[0]
    rolled = pltpu.roll(x, n - s, 0)
    rows = lax.broadcasted_iota(jnp.int32, (SUBLANES, x.shape[1]), 0)
    bot = jnp.where(rows >= SUBLANES - s, pltpu.roll(next8, SUBLANES - s, 0), rolled[n - SUBLANES:n])
    return jnp.concatenate([rolled[:n - SUBLANES], bot], axis=0)


def _rot_partner(t):
    w = t.shape[1]
    lane = lax.broadcasted_iota(jnp.int32, t.shape, 1)
    first = (lane % QK_DIM) < (QK_DIM // 2)
    return jnp.where(first, pltpu.roll(t, w - QK_DIM // 2, 1), pltpu.roll(t, QK_DIM // 2, 1))


def _tile_lanes(t, reps):
    return jnp.concatenate([t] * reps, axis=1)


class _Ride:
    def __init__(self, srcs, dst_shapes, n_copies, make):
        self.srcs, self.dst_shapes, self.n_copies, self.make = list(srcs), list(dst_shapes), n_copies, make


def _join_rides(a, b):
    def make(src, dst, send_sems, recv_sems, base):
        na, da = len(a.srcs), len(a.dst_shapes)
        return (a.make(src[:na], dst[:da], send_sems, recv_sems, base)
                + b.make(src[na:], dst[da:], send_sems, recv_sems, base + a.n_copies))

    return _Ride(a.srcs + b.srcs, a.dst_shapes + b.dst_shapes, a.n_copies + b.n_copies, make)


def _position():
    x, y, c = lax.axis_index("x"), lax.axis_index("y"), lax.axis_index("c")
    return x, y, c, [(1 - x, y), (x, 1 - y), (1 - x, 1 - y)]


def _remote(src, dst, send_sems, recv_sems, k, to):
    return pltpu.make_async_remote_copy(src_ref=src, dst_ref=dst, send_sem=send_sems.at[k], recv_sem=recv_sems.at[k],
                                        device_id=to, device_id_type=MESH)


def _pair_ride(bufs):
    def make(src, dst, send_sems, recv_sems, base):
        x, y, c, _ = _position()
        return [_remote(src[b].at[2 * jj + 1 - c], dst[b].at[jj], send_sems, recv_sems, base + b * N_CHIPS + jj,
                        (x, y, 1 - c)) for b in range(len(bufs)) for jj in range(N_CHIPS)]

    shapes = [jax.ShapeDtypeStruct((N_CHIPS,) + b.shape[1:], b.dtype) for b in bufs]
    return _Ride(bufs, shapes, N_CHIPS * len(bufs), make)


def _chip_ride(bufs):
    def make(src, dst, send_sems, recv_sems, base):
        x, y, c, chips = _position()
        return [_remote(src[b].at[2 * cx + cy], dst[b].at[2 * x + y], send_sems, recv_sems, base + b * 3 + k,
                        (cx, cy, c)) for b in range(len(bufs)) for k, (cx, cy) in enumerate(chips)]

    shapes = [jax.ShapeDtypeStruct(b.shape, b.dtype) for b in bufs]
    return _Ride(bufs, shapes, 3 * len(bufs), make)


def _sibling_ride(bufs):
    def make(src, dst, send_sems, recv_sems, base):
        x, y, c, _ = _position()
        return [_remote(src[b], dst[b], send_sems, recv_sems, base + b, (x, y, 1 - c)) for b in range(len(bufs))]

    shapes = [jax.ShapeDtypeStruct(b.shape, b.dtype) for b in bufs]
    return _Ride(bufs, shapes, len(bufs), make)


def _hosted_call(body, ride, n_steps, *, name, in_specs, out_specs, out_shape, scratch_shapes, args):
    params = pltpu.CompilerParams(dimension_semantics=("arbitrary",), vmem_limit_bytes=VMEM_LIMIT)
    if ride is None:
        res = pl.pallas_call(body, name=name, grid=(n_steps,), in_specs=list(in_specs), out_specs=list(out_specs),
                             out_shape=list(out_shape), scratch_shapes=list(scratch_shapes),
                             compiler_params=params)(*args)
        return list(res), []
    sizes = [len(in_specs), len(ride.srcs), len(out_specs), len(ride.dst_shapes), len(scratch_shapes), 2]

    def hosted(*refs):
        groups, pos = [], 0
        for n in sizes:
            groups.append(refs[pos:pos + n])
            pos += n
        ins, rin, outs, rout, scr, (send_sems, recv_sems) = groups
        i = pl.program_id(0)

        @pl.when(i == 0)
        def _():
            for cp in ride.make(rin, rout, send_sems, recv_sems, 0):
                cp.start()

        body(*ins, *outs, *scr)

        @pl.when(i == n_steps - 1)
        def _():
            for cp in ride.make(rin, rout, send_sems, recv_sems, 0):
                cp.wait()

    n_out = len(out_specs)
    res = pl.pallas_call(
        hosted,
        name=name,
        grid=(n_steps,),
        in_specs=list(in_specs) + [ANY_SPEC] * len(ride.srcs),
        out_specs=list(out_specs) + [ANY_SPEC] * len(ride.dst_shapes),
        out_shape=list(out_shape) + ride.dst_shapes,
        scratch_shapes=list(scratch_shapes) + [pltpu.SemaphoreType.DMA((ride.n_copies,)),
                                               pltpu.SemaphoreType.DMA((ride.n_copies,))],
        compiler_params=params,
    )(*args, *ride.srcs)
    return list(res[:n_out]), list(res[n_out:])


def _gather_weights(w_in, small):
    r_in, c_in = w_in.shape
    h_in = r_in // 2
    q_in = h_in // 2

    def body(win_ref, small_ref, wg_ref, sg_ref, send_sems, recv_sems):
        x, y, c, chips = _position()
        j = 2 * x + y
        sibling = (x, y, 1 - c)
        xn, yn, dg = chips
        jx, jy, jd = (2 * cx + cy for cx, cy in chips)

        def half(jj, cc):
            return wg_ref.at[jj, pl.ds(cc * h_in, h_in), :]

        def quarter(jj, qq):
            return wg_ref.at[jj, pl.ds(c * h_in + qq * q_in, q_in), :]

        def copy(k, ref, to):
            return _remote(ref, ref, send_sems, recv_sems, k, to)

        def cast_rows(start, rows):
            start = pl.multiple_of(start, q_in)
            wg_ref[j, pl.ds(start, rows), :] = win_ref[pl.ds(start, rows), :].astype(BF16)

        first = [copy(0, quarter(j, 0), (*xn, c)), copy(2, quarter(j, 1), (*yn, c)),
                 copy(1, quarter(j, 1), (*xn, c)), copy(3, quarter(j, 0), (*yn, c))]
        sg_ref[j] = small_ref[...]
        cast_rows(c * h_in, q_in)
        first[0].start()
        cast_rows(c * h_in + q_in, q_in)
        for cp in first[1:]:
            cp.start()
        small_copies = [copy(9 + k, sg_ref.at[j], (cx, cy, c)) for k, (cx, cy) in enumerate(chips)]
        for cp in small_copies:
            cp.start()
        first += small_copies
        cast_rows((1 - c) * h_in, h_in)
        copy(0, quarter(jx, 0), sibling).wait_recv()
        along_y = copy(4, quarter(jx, 0), (*yn, c))
        along_y.start()
        copy(2, quarter(jy, 1), sibling).wait_recv()
        along_x = copy(5, quarter(jy, 1), (*xn, c))
        along_x.start()
        copy(1, quarter(jx, 1), sibling).wait_recv()
        to_sib = [copy(6, half(jx, c), sibling)]
        to_sib[-1].start()
        copy(3, quarter(jy, 0), sibling).wait_recv()
        to_sib.append(copy(7, half(jy, c), sibling))
        to_sib[-1].start()
        copy(4, quarter(jd, 0), sibling).wait_recv()
        copy(5, quarter(jd, 1), sibling).wait_recv()
        to_sib.append(copy(8, half(jd, c), sibling))
        to_sib[-1].start()
        for k, jk in enumerate((jx, jy, jd)):
            copy(6 + k, half(jk, 1 - c), sibling).wait_recv()
            copy(9 + k, sg_ref.at[jk], sibling).wait_recv()
        for cp in first + [along_y, along_x] + to_sib:
            cp.wait_send()

    return pl.pallas_call(
        body,
        name="gather_weights",
        out_shape=(jax.ShapeDtypeStruct((N_CHIPS, r_in, c_in), BF16),
                   jax.ShapeDtypeStruct((N_CHIPS,) + small.shape, F32)),
        in_specs=[VMEM_SPEC, VMEM_SPEC],
        out_specs=(VMEM_SPEC, VMEM_SPEC),
        scratch_shapes=[pltpu.SemaphoreType.DMA((12,)), pltpu.SemaphoreType.DMA((12,))],
        compiler_params=pltpu.CompilerParams(vmem_limit_bytes=VMEM_LIMIT),
    )(w_in, small)


def _proj_segments(d_lru, d_qk, d_ret, chunk_w):
    widths = [d_lru, d_lru, d_qk, d_qk, d_ret, d_ret]
    segs, col = [], 0
    for w in widths:
        parts, off = [], 0
        while off < w:
            jj, inner = divmod(col + off, chunk_w)
            take = min(w - off, chunk_w - inner)
            parts.append((jj, inner, off, take))
            off += take
        segs.append(parts)
        col += w
    return segs


def _in_proj(x2, meta, gain, wg, cos_t, sin_t, w_out, qdec, kdec, tm, d_lru, d_qk, d_ret):
    s_len, d = x2.shape
    tp = s_len + CHUNK
    nt, nb = tp // tm, tm // CHUNK
    segs = _proj_segments(d_lru, d_qk, d_ret, wg.shape[2])
    outs = [(d, F32), (d_lru, F32), (d_lru, F32)] + [(d_qk, BF16)] * 4 + [(d_ret, BF16), (d_ret, F32)]
    r_out, c_out = w_out.shape
    h_out = r_out // 2
    fwd_step = min(6, nt - 1)

    def gather_w_out(i, wout_ref, wo_ref, wob, send_sems, recv_sems, local_sem):
        x, y, c, chips = _position()
        j = 2 * x + y
        sibling = (x, y, 1 - c)

        def half(jj, cc):
            return wo_ref.at[jj, pl.ds(cc * h_out, h_out), :]

        local = pltpu.make_async_copy(wob, wo_ref.at[j], local_sem)
        first = [_remote(wob.at[pl.ds(c * h_out, h_out), :], half(j, c), send_sems, recv_sems, k, (cx, cy, c))
                 for k, (cx, cy) in enumerate(chips)]
        passed = [_remote(half(2 * cx + cy, c), half(2 * cx + cy, c), send_sems, recv_sems, 3 + k, sibling)
                  for k, (cx, cy) in enumerate(chips)]

        @pl.when(i == 0)
        def _():
            wob[...] = wout_ref[...].astype(BF16)
            local.start()
            for cp in first:
                cp.start()

        @pl.when(i == fwd_step)
        def _():
            for k, (cx, cy) in enumerate(chips):
                _remote(half(2 * cx + cy, c), half(2 * cx + cy, c), send_sems, recv_sems, k, sibling).wait_recv()
                passed[k].start()

        @pl.when(i == nt - 1)
        def _():
            for k, (cx, cy) in enumerate(chips):
                jk = 2 * cx + cy
                _remote(half(jk, 1 - c), half(jk, 1 - c), send_sems, recv_sems, 3 + k, sibling).wait_recv()
            for cp in first + passed:
                cp.wait_send()
            local.wait()

    def body(*refs):
        xb = refs[:nb]
        meta_ref, g_ref, w_ref, cos_ref, sin_ref, wout_ref, qdec_ref, kdec_ref = refs[nb:nb + 8]
        hp_ref, lx_ref, lg_ref, qb_ref, kb_ref, qd_ref, kd_ref, vb_ref, rg_ref = refs[nb + 8:nb + 17]
        wo_ref, q_s, k_s, wob, send_sems, recv_sems, local_sem = refs[nb + 17:]
        i = pl.program_id(0)
        gather_w_out(i, wout_ref, wo_ref, wob, send_sems, recv_sems, local_sem)
        blocks = [r[...] for r in xb]
        head = jnp.concatenate([jnp.zeros((PAD_ROWS, d), F32), meta_ref[...]], axis=0)
        blocks[0] = jnp.where(i == 0, head, blocks[0])
        h = jnp.concatenate(blocks, axis=0)
        hp_ref[...] = h
        rinv = lax.rsqrt(jnp.mean(h * h, axis=-1, keepdims=True) + EPS)
        u = ((h * rinv) * g_ref[...]).astype(BF16)
        for out_ref, parts in zip([lx_ref, lg_ref, q_s, k_s, vb_ref, rg_ref], segs):
            for jj, inner, off, take in parts:
                out_ref[:, off:off + take] = _dot(u, w_ref[jj, :, inner:inner + take]).astype(out_ref.dtype)
        cos = _tile_lanes(cos_ref[...], d_qk // LANES)
        sin = _tile_lanes(sin_ref[...], d_qk // LANES)
        q = q_s[...]
        q = q * cos + _rot_partner(q) * sin
        k = k_s[...]
        k = (k * cos + _rot_partner(k) * sin) * (QK_DIM ** -0.5)
        qb_ref[...] = q.astype(BF16)
        kb_ref[...] = k.astype(BF16)
        qd_ref[...] = (q * jnp.concatenate([qdec_ref[...]] * nb, axis=0)).astype(BF16)
        kd_ref[...] = (k * jnp.concatenate([kdec_ref[...]] * nb, axis=0)).astype(BF16)

    x_specs = [pl.BlockSpec((CHUNK, d), functools.partial(lambda i, b: (jnp.maximum(i * nb + b - 1, 0), 0), b=b))
               for b in range(nb)]
    tile = lambda w: pl.BlockSpec((tm, w), lambda i: (i, 0))
    return pl.pallas_call(
        body,
        name="in_proj",
        grid=(nt,),
        in_specs=x_specs + [pl.BlockSpec(meta.shape, lambda i: (0, 0)),
                            pl.BlockSpec(gain.shape, lambda i: (0, 0)),
                            pl.BlockSpec(wg.shape, lambda i: (0, 0, 0)),
                            tile(LANES), tile(LANES),
                            pl.BlockSpec(w_out.shape, lambda i: (0, 0)),
                            pl.BlockSpec(qdec.shape, lambda i: (0, 0)), pl.BlockSpec(kdec.shape, lambda i: (0, 0))],
        out_specs=[tile(w) for w, _ in outs] + [ANY_SPEC],
        out_shape=[jax.ShapeDtypeStruct((tp, w), dt) for w, dt in outs]
                  + [jax.ShapeDtypeStruct((N_CHIPS, r_out, c_out), BF16)],
        scratch_shapes=[pltpu.VMEM((tm, d_qk), F32), pltpu.VMEM((tm, d_qk), F32),
                        pltpu.VMEM((r_out, c_out), BF16), pltpu.SemaphoreType.DMA((6,)),
                        pltpu.SemaphoreType.DMA((6,)), pltpu.SemaphoreType.DMA],
        compiler_params=pltpu.CompilerParams(dimension_semantics=("arbitrary",), vmem_limit_bytes=VMEM_LIMIT),
    )(*([x2] * nb), meta, gain, wg, cos_t, sin_t, w_out, qdec, kdec)


def _segment_scan(a3, u3, out3, p3, carry, tm, reverse):
    groups = a3.shape[0]
    seg = tm // SUBLANES

    def step(j, state):
        hs, ps = state
        rows = pl.ds((seg - 1 - j) if reverse else j, SUBLANES, stride=seg)
        new_h, new_p = [], []
        for g in range(groups):
            a = a3[g, rows, :]
            h = a * hs[g] + u3[g, rows, :]
            p = ps[g] * a
            out3[g, rows, :] = h
            p3[g, rows, :] = p
            new_h.append(h)
            new_p.append(p)
        return tuple(new_h), tuple(new_p)

    zeros = tuple(jnp.zeros((SUBLANES, LANES), F32) for _ in range(groups))
    ones = tuple(jnp.ones((SUBLANES, LANES), F32) for _ in range(groups))
    lax.fori_loop(0, seg, step, (zeros, ones))
    carries = [carry[:, g * LANES:(g + 1) * LANES] for g in range(groups)]
    for s in (reversed(range(SUBLANES)) if reverse else range(SUBLANES)):
        rows = slice(s * seg, (s + 1) * seg)
        edge = s * seg if reverse else (s + 1) * seg - 1
        for g in range(groups):
            out3[g, rows, :] = out3[g, rows, :] + p3[g, rows, :] * carries[g]
            carries[g] = out3[g, edge:edge + 1, :]
    return jnp.concatenate(carries, axis=1)


def _softplus_neg(lam):
    z = -lam
    e = jnp.exp(-jnp.abs(z))
    e1 = 1.0 + e
    log1p_e = jnp.where(e1 == 1.0, e, jnp.log(e1) * (e / (e1 - 1.0)))
    return jnp.maximum(z, 0.0) + log1p_e


def _lru_fwd(lx, lg, cw, cb, wr, br, wi, bi, lam, tm):
    tp, w = lx.shape
    nt = tp // tm
    per8 = tm // SUBLANES
    n_heads = wr.shape[0]

    def body(lx_ref, lxp_ref, lg_ref, cw_ref, cb_ref, wr_ref, br_ref, wi_ref, bi_ref, lam_ref,
             hl_ref, y_ref, xc_ref, r_ref, ig_ref, a_ref, beta_ref, w4_ref, a_s, u_s, h_s, p_s, carry):
        i = pl.program_id(0)

        @pl.when(i == 0)
        def _():
            carry[...] = jnp.zeros_like(carry)

        sp = _softplus_neg(lam_ref[...])
        row = lax.broadcasted_iota(jnp.int32, (tm, 1), 0) + i * tm
        for hd in range(n_heads):
            hs = slice(hd * LANES, (hd + 1) * LANES)
            lxv = lx_ref[:, hs]
            prev8 = jnp.where(i == 0, 0.0, lxp_ref[:, hs])
            xc = cb_ref[:, hs] + _shift_down(lxv, prev8, 3) * cw_ref[0:1, hs]
            xc = xc + _shift_down(lxv, prev8, 2) * cw_ref[1:2, hs]
            xc = xc + _shift_down(lxv, prev8, 1) * cw_ref[2:3, hs]
            xc = xc + lxv * cw_ref[3:4, hs]
            xc_ref[:, hs] = xc
            xh = xc.astype(BF16)
            r = _sigmoid(_dot(xh, wr_ref[hd].astype(BF16)) + br_ref[:, hs])
            ig = _sigmoid(_dot(xh, wi_ref[hd].astype(BF16)) + bi_ref[:, hs])
            r_ref[:, hs] = r
            ig_ref[:, hs] = ig
            log_a = (-LRU_C * r) * sp[:, hs]
            a = jnp.exp(log_a)
            a_ref[:, hs] = a
            a2 = a * a
            beta2 = jnp.maximum((1.0 + a2) * jnp.tanh(-log_a), 1e-37)
            rsb = lax.rsqrt(beta2)
            beta = beta2 * rsb
            beta_ref[:, hs] = beta
            w4_ref[:, hs] = a2 * rsb
            a_s[hd] = a
            u_s[hd] = jnp.where(row >= PAD_ROWS, beta * ig * xc, 0.0)
        carry[0:1, :] = _segment_scan(a_s, u_s, h_s, p_s, carry[0:1, :], tm, reverse=False)
        for hd in range(n_heads):
            hs = slice(hd * LANES, (hd + 1) * LANES)
            hl = h_s[hd]
            hl_ref[:, hs] = hl
            g = lg_ref[:, hs]
            y_ref[:, hs] = (hl * (g * _sigmoid(g))).astype(BF16)

    tile = pl.BlockSpec((tm, w), lambda i: (i, 0))
    prev = pl.BlockSpec((SUBLANES, w), lambda i: (jnp.maximum(i * per8 - 1, 0), 0))
    vec = pl.BlockSpec((1, w), lambda i: (0, 0))
    mat = pl.BlockSpec(wr.shape, lambda i: (0, 0, 0))
    f32_out = jax.ShapeDtypeStruct((tp, w), F32)
    return pl.pallas_call(
        body,
        name="lru_fwd",
        grid=(nt,),
        in_specs=[tile, prev, tile, pl.BlockSpec(cw.shape, lambda i: (0, 0)), vec, mat, vec, mat, vec, vec],
        out_specs=[tile] * 8,
        out_shape=[f32_out, jax.ShapeDtypeStruct((tp, w), BF16)] + [f32_out] * 6,
        scratch_shapes=[pltpu.VMEM((w // LANES, tm, LANES), F32)] * 4 + [pltpu.VMEM((SUBLANES, w), F32)],
        compiler_params=pltpu.CompilerParams(dimension_semantics=("arbitrary",), vmem_limit_bytes=VMEM_LIMIT),
    )(lx, lx, lg, cw, cb, wr, br, wi, bi, lam)


def _ret_tables():
    log_g = jnp.log1p(-jnp.exp2(-5.0 - jnp.arange(HEADS, dtype=F32)))
    idx = jnp.arange(CHUNK, dtype=F32)
    diff = idx[:, None] - idx[None, :]
    dmask = jnp.where(diff[None] >= 0.0, jnp.exp(jnp.maximum(diff, 0.0)[None] * log_g[:, None, None]), 0.0)
    kdec = jnp.repeat(jnp.exp((CHUNK - 1.0 - idx)[:, None] * log_g[None, :]), QK_DIM, axis=1)
    qdec = jnp.repeat(jnp.exp((idx + 1.0)[:, None] * log_g[None, :]), QK_DIM, axis=1)
    g_chunk = jnp.exp(CHUNK * log_g)
    g_rows = jnp.repeat(g_chunk, QK_DIM).reshape(HEADS // 2, 2 * QK_DIM, 1)
    g_state = jnp.broadcast_to(g_rows, (HEADS // 2, 2 * QK_DIM, 2 * LANES))
    r_head = jnp.arange(2 * QK_DIM)[:, None] // QK_DIM
    c_head = jnp.arange(2 * LANES)[None, :] // LANES
    block_diag = (r_head == c_head).astype(F32)
    return dmask, qdec, kdec, g_state, block_diag


def _head_norm(o_h):
    mu = jnp.mean(o_h, axis=-1, keepdims=True)
    oc = o_h - mu
    var = jnp.mean(oc * oc, axis=-1, keepdims=True)
    rstd = lax.rsqrt(var + EPS)
    return oc * rstd, rstd


def _ret_fwd(qb, kb, qd, kd, vb, rg, gain, tables, tm):
    tp, d_qk = qb.shape
    d_ret = vb.shape[1]
    n_ch = tp // CHUNK
    cps = tm // CHUNK
    n_pairs = HEADS // 2
    dmask, _, _, g_state, block_diag = tables

    def body(q_ref, k_ref, qd_ref, kd_ref, v_ref, rg_ref, gain_ref, dm_ref, gs_ref, bd_ref,
             o_ref, y_ref, rp_ref, state):
        n = pl.program_id(0)

        @pl.when(n == 0)
        def _():
            state[...] = jnp.zeros_like(state)

        lane = lax.broadcasted_iota(jnp.int32, (CHUNK, LANES), 1)
        for ci in range(cps):
            rs = slice(ci * CHUNK, (ci + 1) * CHUNK)
            for p in range(n_pairs):
                qs = slice(p * LANES, (p + 1) * LANES)
                vs = slice(p * 2 * LANES, (p + 1) * 2 * LANES)
                qp, kb = q_ref[rs, qs], k_ref[rs, qs]
                vb = v_ref[rs, vs]
                qd, kd = qd_ref[rs, qs], kd_ref[rs, qs]
                st = state[p]
                st_b = st.astype(BF16)
                rp_ref[ci, p] = st_b
                cross = _dot(qd, st_b)
                for e in range(2):
                    hd = 2 * p + e
                    hs = slice(hd * LANES, (hd + 1) * LANES)
                    es = slice(e * LANES, (e + 1) * LANES)
                    qm = jnp.where((lane // QK_DIM) == e, qp, jnp.zeros_like(qp))
                    s = _dot_nt(qm, kb) * dm_ref[hd]
                    o_h = _dot(s.astype(BF16), vb[:, es]) + cross[:, es]
                    o_ref[rs, hs] = o_h
                    xhat, _ = _head_norm(o_h)
                    g = rg_ref[rs, hs]
                    y_ref[rs, hs] = ((xhat * gain_ref[:, hs]) * (g * _sigmoid(g))).astype(BF16)
                state[p] = gs_ref[p] * st + bd_ref[...] * _dot_tn(kd, vb)

    ch = lambda w: pl.BlockSpec((tm, w), lambda n: (n, 0))
    const2 = lambda a: pl.BlockSpec(a.shape, lambda n: (0, 0))
    const3 = lambda a: pl.BlockSpec(a.shape, lambda n: (0, 0, 0))
    return pl.pallas_call(
        body,
        name="ret_fwd",
        grid=(n_ch // cps,),
        in_specs=[ch(d_qk)] * 4 + [ch(d_ret), ch(d_ret), const2(gain), const3(dmask), const3(g_state),
                                   const2(block_diag)],
        out_specs=[ch(d_ret), ch(d_ret),
                   pl.BlockSpec((cps, n_pairs, 2 * QK_DIM, 2 * LANES), lambda n: (n, 0, 0, 0))],
        out_shape=[jax.ShapeDtypeStruct((tp, d_ret), F32), jax.ShapeDtypeStruct((tp, d_ret), BF16),
                   jax.ShapeDtypeStruct((n_ch, n_pairs, 2 * QK_DIM, 2 * LANES), BF16)],
        scratch_shapes=[pltpu.VMEM((n_pairs, 2 * QK_DIM, 2 * LANES), F32)],
        compiler_params=pltpu.CompilerParams(dimension_semantics=("arbitrary",), vmem_limit_bytes=VMEM_LIMIT),
    )(qb, kb, qd, kd, vb, rg, gain, dmask, g_state, block_diag)


def _out_proj_loss(y_lru, y_ret, hp, tgt, wo, gain_f, tm):
    tp, d = hp.shape
    w_lru = y_lru.shape[1]
    w_mix = wo.shape[0]
    nt, nb = tp // tm, tm // CHUNK

    def body(*refs):
        yl_ref, yr_ref, hp_ref = refs[:3]
        tb = refs[3:3 + nb]
        wo_ref, gf_ref = refs[3 + nb:5 + nb]
        dh2_ref, dyl_ref, dyr_ref, dwo_ref, dgf_ref, loss_ref = refs[5 + nb:]
        i = pl.program_id(0)

        @pl.when(i == 0)
        def _():
            dwo_ref[...] = jnp.zeros_like(dwo_ref)
            dgf_ref[...] = jnp.zeros_like(dgf_ref)
            loss_ref[...] = jnp.zeros_like(loss_ref)

        yl, yr = yl_ref[...], yr_ref[...]
        h2 = hp_ref[...] + _dot(yl, wo_ref[0:w_lru, :]) + _dot(yr, wo_ref[w_lru:w_mix, :])
        rinv = lax.rsqrt(jnp.mean(h2 * h2, axis=-1, keepdims=True) + EPS)
        nrm = h2 * rinv
        gf = gf_ref[...]
        tgt_v = jnp.concatenate([r[...] for r in tb], axis=0)
        row = lax.broadcasted_iota(jnp.int32, (tm, 1), 0) + i * tm
        err = jnp.where(row >= CHUNK, nrm * gf - tgt_v, 0.0)
        loss_ref[...] += 0.5 * jnp.sum(jnp.mean(err * err, axis=-1, keepdims=True))
        dout = err * (1.0 / d)
        dgf_ref[...] += jnp.sum(dout * nrm, axis=0, keepdims=True)
        dn = dout * gf
        dh2 = rinv * (dn - nrm * jnp.mean(dn * nrm, axis=-1, keepdims=True))
        dh2_ref[...] = dh2
        dh2b = dh2.astype(BF16)
        dyl_ref[...] = _dot_nt(dh2b, wo_ref[0:w_lru, :])
        dyr_ref[...] = _dot_nt(dh2b, wo_ref[w_lru:w_mix, :])
        dwo_ref[0:w_lru, :] += _dot_tn(yl, dh2b)
        dwo_ref[w_lru:w_mix, :] += _dot_tn(yr, dh2b)

    tile = lambda w: pl.BlockSpec((tm, w), lambda i: (i, 0))
    t_specs = [pl.BlockSpec((CHUNK, d), functools.partial(lambda i, b: (jnp.maximum(i * nb + b - 1, 0), 0), b=b))
               for b in range(nb)]
    return pl.pallas_call(
        body,
        name="out_proj_loss",
        grid=(nt,),
        in_specs=[tile(w_lru), tile(w_mix - w_lru), tile(d)] + t_specs +
                 [pl.BlockSpec(wo.shape, lambda i: (0, 0)), pl.BlockSpec(gain_f.shape, lambda i: (0, 0))],
        out_specs=[tile(d), tile(w_lru), tile(w_mix - w_lru), pl.BlockSpec(wo.shape, lambda i: (0, 0)),
                   pl.BlockSpec((1, d), lambda i: (0, 0)), pl.BlockSpec((SUBLANES, LANES), lambda i: (0, 0))],
        out_shape=[jax.ShapeDtypeStruct((tp, d), F32), jax.ShapeDtypeStruct((tp, w_lru), F32),
                   jax.ShapeDtypeStruct((tp, w_mix - w_lru), F32), jax.ShapeDtypeStruct(wo.shape, F32),
                   jax.ShapeDtypeStruct((1, d), F32), jax.ShapeDtypeStruct((SUBLANES, LANES), F32)],
        compiler_params=pltpu.CompilerParams(dimension_semantics=("arbitrary",), vmem_limit_bytes=VMEM_LIMIT),
    )(y_lru, y_ret, hp, *([tgt] * nb), wo, gain_f)


def _ret_bwd(qb, kb, qd, kd, vb, rg, o, rprev, dy, gain, cos_t, sin_t, tables, tm, ride=None):
    tp, d_qk = qb.shape
    d_ret = vb.shape[1]
    n_ch = tp // CHUNK
    cps = tm // CHUNK
    n_pairs = HEADS // 2
    dmask, qdec, kdec, g_state, block_diag = tables

    dmask_t = jnp.swapaxes(dmask, 1, 2)

    def body(q_ref, k_ref, qdb_ref, kdb_ref, v_ref, rg_ref, o_ref, rp_ref, dy_ref, gain_ref, cos_ref, sin_ref,
             dm_ref, dmt_ref, qd_ref, kd_ref, gs_ref, bd_ref, dq_ref, dk_ref, dv_ref, drg_ref, dgain_ref, dstate):
        n = pl.program_id(0)

        @pl.when(n == 0)
        def _():
            dstate[...] = jnp.zeros_like(dstate)
            dgain_ref[...] = jnp.zeros_like(dgain_ref)

        lane = lax.broadcasted_iota(jnp.int32, (CHUNK, LANES), 1)
        for ci in reversed(range(cps)):
            rs = slice(ci * CHUNK, (ci + 1) * CHUNK)
            dq_parts, dk_parts = [], []
            for p in range(n_pairs):
                qs = slice(p * LANES, (p + 1) * LANES)
                vs = slice(p * 2 * LANES, (p + 1) * 2 * LANES)
                do_parts = []
                for e in range(2):
                    hd = 2 * p + e
                    hs = slice(hd * LANES, (hd + 1) * LANES)
                    xhat, rstd = _head_norm(o_ref[rs, hs])
                    g = rg_ref[rs, hs]
                    sg = _sigmoid(g)
                    dyh = dy_ref[rs, hs]
                    gn = gain_ref[:, hs]
                    d_on = dyh * (g * sg)
                    drg_ref[rs, hs] = (dyh * (xhat * gn) * (sg * (1.0 + g * (1.0 - sg)))).astype(BF16)
                    dgain_ref[:, hs] += jnp.sum(d_on * xhat, axis=0, keepdims=True)
                    dxh = d_on * gn
                    do_parts.append(rstd * (dxh - jnp.mean(dxh, axis=-1, keepdims=True)
                                            - xhat * jnp.mean(dxh * xhat, axis=-1, keepdims=True)))
                do_b = jnp.concatenate(do_parts, axis=1).astype(BF16)
                qp, kb = q_ref[rs, qs], k_ref[rs, qs]
                vb = v_ref[rs, vs]
                qd, kd = qdb_ref[rs, qs], kdb_ref[rs, qs]
                dst = dstate[p]
                dst_b = dst.astype(BF16)
                dqp = _dot_nt(do_b, rp_ref[ci, p]) * qd_ref[:, qs]
                dkp = _dot_nt(vb, dst_b) * kd_ref[:, qs]
                dvp = _dot(kd, dst_b)
                dv_parts = []
                for e in range(2):
                    hd = 2 * p + e
                    es = slice(e * LANES, (e + 1) * LANES)
                    mine = (lane // QK_DIM) == e
                    qm = jnp.where(mine, qp, jnp.zeros_like(qp))
                    km = jnp.where(mine, kb, jnp.zeros_like(kb))
                    ds = (_dot_nt(do_b[:, es], vb[:, es]) * dm_ref[hd]).astype(BF16)
                    s_t = (_dot_nt(kb, qm) * dmt_ref[hd]).astype(BF16)
                    ds_t = (_dot_nt(vb[:, es], do_b[:, es]) * dmt_ref[hd]).astype(BF16)
                    dv_parts.append(dvp[:, es] + _dot(s_t, do_b[:, es]))
                    dqp = dqp + _dot(ds, km)
                    dkp = dkp + _dot(ds_t, qm)
                dv_ref[rs, vs] = jnp.concatenate(dv_parts, axis=1).astype(BF16)
                dstate[p] = gs_ref[p] * dst + bd_ref[...] * _dot_tn(qd, do_b)
                dq_parts.append(dqp)
                dk_parts.append(dkp)
            cos = _tile_lanes(cos_ref[rs, :], d_qk // LANES)
            sin = _tile_lanes(sin_ref[rs, :], d_qk // LANES)
            dq = jnp.concatenate(dq_parts, axis=1)
            dk = jnp.concatenate(dk_parts, axis=1) * (QK_DIM ** -0.5)
            dq_ref[rs, :] = (dq * cos + _rot_partner(dq * sin)).astype(BF16)
            dk_ref[rs, :] = (dk * cos + _rot_partner(dk * sin)).astype(BF16)

    last = n_ch // cps - 1
    ch = lambda w: pl.BlockSpec((tm, w), lambda n: (last - n, 0))
    const2 = lambda a: pl.BlockSpec(a.shape, lambda n: (0, 0))
    const3 = lambda a: pl.BlockSpec(a.shape, lambda n: (0, 0, 0))
    return _hosted_call(
        body, ride, n_ch // cps,
        name="ret_bwd",
        in_specs=[ch(d_qk)] * 4 + [ch(d_ret), ch(d_ret), ch(d_ret),
                  pl.BlockSpec((cps, n_pairs, 2 * QK_DIM, 2 * LANES), lambda n: (last - n, 0, 0, 0)),
                  ch(d_ret), const2(gain), ch(LANES), ch(LANES),
                  const3(dmask), const3(dmask_t), const2(qdec), const2(kdec), const3(g_state), const2(block_diag)],
        out_specs=[ch(d_qk), ch(d_qk), ch(d_ret), ch(d_ret), pl.BlockSpec((1, d_ret), lambda n: (0, 0))],
        out_shape=[jax.ShapeDtypeStruct((tp, d_qk), BF16), jax.ShapeDtypeStruct((tp, d_qk), BF16),
                   jax.ShapeDtypeStruct((tp, d_ret), BF16), jax.ShapeDtypeStruct((tp, d_ret), BF16),
                   jax.ShapeDtypeStruct((1, d_ret), F32)],
        scratch_shapes=[pltpu.VMEM((n_pairs, 2 * QK_DIM, 2 * LANES), F32)],
        args=(qb, kb, qd, kd, vb, rg, o, rprev, dy, gain, cos_t, sin_t, dmask, dmask_t, qdec, kdec, g_state,
              block_diag),
    )


def _lru_bwd(lx, lg, hl, dy, saved, cw, wr, wi, lam, dgain, dgf, tm, ride=None):
    tp, w = lx.shape
    nt = tp // tm
    per8 = tm // SUBLANES
    n_heads = wr.shape[0]
    vec_row = {name: ROW_VEC + VEC_NAMES.index(name) for name in VEC_NAMES}

    def body(lx_ref, lg_ref, hl_ref, hlp_ref, dy_ref, xc_ref, r_ref, ig_ref, a_ref, beta_ref, w4_ref,
             cw_ref, wr_ref, wi_ref, lam_ref, dgain_ref, dgf_ref,
             dlx_ref, dlg_ref, pk_ref,
             g_s, b_s, carry, dxc_next, a_next):
        i = pl.program_id(0)
        first_tile = i == nt - 1
        heads = [slice(hd * LANES, (hd + 1) * LANES) for hd in range(n_heads)]

        def add_row(hd, row, value):
            pk_ref[hd, row:row + 1, :] += value

        @pl.when(i == 0)
        def _():
            carry[...] = jnp.zeros_like(carry)
            dxc_next[...] = jnp.zeros_like(dxc_next)
            a_next[...] = jnp.zeros_like(a_next)
            pk_ref[...] = jnp.zeros_like(pk_ref)
            for hd, hs in enumerate(heads):
                add_row(hd, vec_row["ret_norm_gain"], dgain_ref[:, hs])
                add_row(hd, vec_row["final_norm_gain"], dgf_ref[:, hs])

        for hd, hs in enumerate(heads):
            g = lg_ref[:, hs]
            sg = _sigmoid(g)
            dyv = dy_ref[:, hs]
            dlg_ref[:, hs] = (dyv * hl_ref[:, hs] * (sg * (1.0 + g * (1.0 - sg)))).astype(BF16)
            g_s[hd] = dyv * (g * sg)
            b_s[hd] = _shift_up(a_ref[:, hs], a_next[:, hs], 1)
        carry[0:1, :] = _segment_scan(b_s, g_s, g_s, b_s, carry[0:1, :], tm, reverse=True)
        a_next[...] = a_ref[0:SUBLANES, :]
        row = lax.broadcasted_iota(jnp.int32, (tm, 1), 0) + (nt - 1 - i) * tm
        lam_v = lam_ref[...]
        dlam_scale = LRU_C * _sigmoid(-lam_v)
        dr_scale = -LRU_C * _softplus_neg(lam_v)
        for hd, hs in enumerate(heads):
            a, beta, r, ig, xc = a_ref[:, hs], beta_ref[:, hs], r_ref[:, hs], ig_ref[:, hs], xc_ref[:, hs]
            dh = g_s[hd]
            hprev = _shift_down(hl_ref[:, hs], jnp.where(first_tile, 0.0, hlp_ref[:, hs]), 1)
            du = jnp.where(row >= PAD_ROWS, dh, 0.0)
            dbeta = du * ig * xc
            d_ig = du * beta * xc
            dxc = du * beta * ig
            dloga = (dh * hprev) * a - dbeta * w4_ref[:, hs]
            add_row(hd, vec_row["lru_lambda"], jnp.sum(dloga * r, axis=0, keepdims=True) * dlam_scale[:, hs])
            dpr = (dloga * dr_scale[:, hs]) * r * (1.0 - r)
            dpi = d_ig * ig * (1.0 - ig)
            add_row(hd, vec_row["b_rg"], jnp.sum(dpr, axis=0, keepdims=True))
            add_row(hd, vec_row["b_ig"], jnp.sum(dpi, axis=0, keepdims=True))
            xh, dprh, dpih = xc.astype(BF16), dpr.astype(BF16), dpi.astype(BF16)
            pk_ref[hd, ROW_WR:ROW_WR + LANES, :] += _dot_tn(xh, dprh)
            pk_ref[hd, ROW_WI:ROW_WI + LANES, :] += _dot_tn(xh, dpih)
            dxc = dxc + _dot_nt(dprh, wr_ref[hd].astype(BF16)) + _dot_nt(dpih, wi_ref[hd].astype(BF16))
            nxt = dxc_next[:, hs]
            up1, up2, up3 = _shift_up(dxc, nxt, 1), _shift_up(dxc, nxt, 2), _shift_up(dxc, nxt, 3)
            dlx = dxc * cw_ref[3:4, hs]
            dlx = dlx + up1 * cw_ref[2:3, hs]
            dlx = dlx + up2 * cw_ref[1:2, hs]
            dlx = dlx + up3 * cw_ref[0:1, hs]
            dlx_ref[:, hs] = dlx.astype(BF16)
            dxc_next[:, hs] = dxc[0:SUBLANES]
            lxv = lx_ref[:, hs]
            add_row(hd, vec_row["conv_b"], jnp.sum(dxc, axis=0, keepdims=True))
            for kk, shifted in enumerate((up3, up2, up1, dxc)):
                add_row(hd, ROW_CONV + kk, jnp.sum(shifted * lxv, axis=0, keepdims=True))

    last = nt - 1
    tile = pl.BlockSpec((tm, w), lambda i: (last - i, 0))
    prev = pl.BlockSpec((SUBLANES, w), lambda i: (jnp.maximum((last - i) * per8 - 1, 0), 0))
    vec = pl.BlockSpec((1, w), lambda i: (0, 0))
    mat = pl.BlockSpec(wr.shape, lambda i: (0, 0, 0))
    cwb = pl.BlockSpec(cw.shape, lambda i: (0, 0))
    packed = (n_heads, UNIT_ROWS, LANES)
    return _hosted_call(
        body, ride, nt,
        name="lru_bwd",
        in_specs=[tile, tile, tile, prev, tile] + [tile] * 6 + [cwb, mat, mat, vec, vec, vec],
        out_specs=[tile, tile, pl.BlockSpec(packed, lambda i: (0, 0, 0))],
        out_shape=[jax.ShapeDtypeStruct((tp, w), BF16), jax.ShapeDtypeStruct((tp, w), BF16),
                   jax.ShapeDtypeStruct(packed, F32)],
        scratch_shapes=[pltpu.VMEM((w // LANES, tm, LANES), F32)] * 2 + [pltpu.VMEM((SUBLANES, w), F32)] * 3,
        args=(lx, lg, hl, hl, dy, *saved, cw, wr, wi, lam, dgain, dgf),
    )


def _in_proj_dw(dparts, hp, gain, wg_shape, ride=None):
    tp, d = hp.shape
    n_ch = tp // CHUNK
    per = next(p for p in (4, 2, 5, 3, 1) if (n_ch - 1) % p == 0)
    n_steps = 1 + (n_ch - 1) // per
    widths = [p.shape[1] for p in dparts]
    segs = _proj_segments(widths[0], widths[2], widths[4], wg_shape[2])

    def body(*refs):
        dp = [refs[p * per:(p + 1) * per] for p in range(6)]
        hp_b = refs[6 * per:7 * per]
        g_ref, dwg_ref, acc, sem = refs[7 * per:]
        i = pl.program_id(0)

        def accumulate(blocks):
            h = jnp.concatenate([hp_b[b][...] for b in blocks], axis=0)
            rinv = lax.rsqrt(jnp.mean(h * h, axis=-1, keepdims=True) + EPS)
            u = ((h * rinv) * g_ref[...]).astype(BF16)
            for p_refs, parts in zip(dp, segs):
                for jj, inner, off, take in parts:
                    seg = jnp.concatenate([p_refs[b][:, off:off + take] for b in blocks], axis=0)
                    acc[jj, :, inner:inner + take] += _dot_tn(u, seg)

        @pl.when(i == 0)
        def _():
            acc[...] = jnp.zeros_like(acc)
            accumulate([0])

        @pl.when(i > 0)
        def _():
            accumulate(list(range(per)))

        @pl.when(i == n_steps - 1)
        def _():
            cp = pltpu.make_async_copy(acc, dwg_ref, sem)
            cp.start()
            cp.wait()

    def blocks(w):
        return [pl.BlockSpec((CHUNK, w), functools.partial(
            lambda i, b: (jnp.where(i == 0, b, per * (i - 1) + 1 + b), 0), b=b)) for b in range(per)]

    in_specs, args = [], []
    for a, w in list(zip(dparts, widths)) + [(hp, d)]:
        in_specs += blocks(w)
        args += [a] * per
    outs, rides = _hosted_call(
        body, ride, n_steps,
        name="in_proj_dw",
        in_specs=in_specs + [pl.BlockSpec(gain.shape, lambda i: (0, 0))],
        out_specs=[ANY_SPEC],
        out_shape=[jax.ShapeDtypeStruct(wg_shape, F32)],
        scratch_shapes=[pltpu.VMEM(wg_shape, F32), pltpu.SemaphoreType.DMA],
        args=(*args, gain),
    )
    return outs[0], rides


def _in_proj_dx(dparts, hp, dh2, gain, wg, s_len, tm, ride=None):
    tp, d = hp.shape
    nt = tp // tm
    widths = [p.shape[1] for p in dparts]
    segs = _proj_segments(widths[0], widths[2], widths[4], wg.shape[2])

    def body(*refs):
        dp = refs[:6]
        hp_ref, dh2_ref, g_ref, w_ref = refs[6:10]
        gx_ref, dmeta_ref, dg_ref = refs[10:13]
        stage, sems = refs[13:]
        i = pl.program_id(0)

        @pl.when(i == 0)
        def _():
            dg_ref[...] = jnp.zeros_like(dg_ref)

        h = hp_ref[...]
        rinv = lax.rsqrt(jnp.mean(h * h, axis=-1, keepdims=True) + EPS)
        nrm = h * rinv
        gv = g_ref[...]
        du = jnp.zeros((tm, d), F32)
        for p_ref, parts in zip(dp, segs):
            for jj, inner, off, take in parts:
                du = du + _dot_nt(p_ref[:, off:off + take], w_ref[jj, :, inner:inner + take])
        dg_ref[...] += jnp.sum(du * nrm, axis=0, keepdims=True)
        dn = du * gv
        dh = dh2_ref[...] + rinv * (dn - nrm * jnp.mean(dn * nrm, axis=-1, keepdims=True))

        def first_copy():
            return pltpu.make_async_copy(stage.at[0, pl.ds(CHUNK, tm - CHUNK), :],
                                         gx_ref.at[pl.ds(0, tm - CHUNK), :], sems.at[0])

        def tile_copy(slot, start):
            return pltpu.make_async_copy(stage.at[slot], gx_ref.at[pl.ds(start, tm), :], sems.at[slot])

        @pl.when(i == 0)
        def _():
            dmeta_ref[...] = dh[PAD_ROWS:CHUNK]
            stage[0] = dh
            first_copy().start()

        @pl.when(i > 0)
        def _():
            slot = 1 + i % 2

            @pl.when(i >= 3)
            def _():
                tile_copy(slot, 0).wait()

            stage[slot] = dh
            tile_copy(slot, pl.multiple_of(i * tm - CHUNK, CHUNK)).start()

        @pl.when(i == nt - 1)
        def _():
            first_copy().wait()
            for step in (nt - 2, nt - 1):
                if step >= 1:
                    tile_copy(1 + step % 2, 0).wait()

    tile = lambda w: pl.BlockSpec((tm, w), lambda i: (i, 0))
    return _hosted_call(
        body, ride, nt,
        name="in_proj_dx",
        in_specs=[tile(w) for w in widths] + [tile(d), tile(d), pl.BlockSpec(gain.shape, lambda i: (0, 0)),
                                              pl.BlockSpec(wg.shape, lambda i: (0, 0, 0))],
        out_specs=[ANY_SPEC, pl.BlockSpec((N_META, d), lambda i: (0, 0)), pl.BlockSpec((1, d), lambda i: (0, 0))],
        out_shape=[jax.ShapeDtypeStruct((s_len, d), F32), jax.ShapeDtypeStruct((N_META, d), F32),
                   jax.ShapeDtypeStruct((1, d), F32)],
        scratch_shapes=[pltpu.VMEM((3, tm, d), F32), pltpu.SemaphoreType.DMA((3,))],
        args=(*dparts, hp, dh2, gain, wg),
    )


def _pair_sum(buf, recv, c_arr, tr, name):
    _, rows, cols = buf.shape

    def body(c_ref, mine_ref, got_ref, out_ref):
        out_ref[...] = (mine_ref[...] + got_ref[...]).astype(BF16)

    grid_spec = pltpu.PrefetchScalarGridSpec(
        num_scalar_prefetch=1,
        grid=(N_CHIPS, rows // tr),
        in_specs=[pl.BlockSpec((1, tr, cols), lambda jj, r, c_ref: (2 * jj + c_ref[0], r, 0)),
                  pl.BlockSpec((1, tr, cols), lambda jj, r, c_ref: (jj, r, 0))],
        out_specs=pl.BlockSpec((1, tr, cols), lambda jj, r, c_ref: (jj, r, 0)),
    )
    return pl.pallas_call(
        body,
        name=name,
        grid_spec=grid_spec,
        out_shape=jax.ShapeDtypeStruct((N_CHIPS, rows, cols), BF16),
    )(c_arr, buf, recv)


def _pair_exchange_sum(buf, c_arr, tr, name):
    _, rows, cols = buf.shape
    per = rows // tr

    def body(c_ref, src_ref, mine_ref, out_ref, got, send_sems, recv_sems):
        jj, r = pl.program_id(0), pl.program_id(1)
        x, y, c, _ = _position()
        copies = [_remote(src_ref.at[2 * k + 1 - c], got.at[k], send_sems, recv_sems, k, (x, y, 1 - c))
                  for k in range(N_CHIPS)]

        @pl.when((jj == 0) & (r == 0))
        def _():
            for cp in copies:
                cp.start()

        for k in range(N_CHIPS):
            @pl.when((jj == k) & (r == 0))
            def _():
                copies[k].wait_recv()

        rows_r = pl.ds(pl.multiple_of(r * tr, tr), tr)
        out_ref[0] = (mine_ref[0] + got[jj, rows_r, :]).astype(BF16)

        @pl.when((jj == N_CHIPS - 1) & (r == per - 1))
        def _():
            for cp in copies:
                cp.wait_send()

    grid_spec = pltpu.PrefetchScalarGridSpec(
        num_scalar_prefetch=1,
        grid=(N_CHIPS, per),
        in_specs=[ANY_SPEC, pl.BlockSpec((1, tr, cols), lambda jj, r, c_ref: (2 * jj + c_ref[0], r, 0))],
        out_specs=pl.BlockSpec((1, tr, cols), lambda jj, r, c_ref: (jj, r, 0)),
        scratch_shapes=[pltpu.VMEM((N_CHIPS, rows, cols), F32), pltpu.SemaphoreType.DMA((N_CHIPS,)),
                        pltpu.SemaphoreType.DMA((N_CHIPS,))],
    )
    return pl.pallas_call(
        body,
        name=name,
        grid_spec=grid_spec,
        out_shape=jax.ShapeDtypeStruct((N_CHIPS, rows, cols), BF16),
        compiler_params=pltpu.CompilerParams(dimension_semantics=("arbitrary", "arbitrary"),
                                             vmem_limit_bytes=VMEM_LIMIT),
    )(c_arr, buf, buf)


def _chip_sum(mine, got, j_arr, tr, name, loss_part=None):
    _, rows, cols = got.shape
    extra = [] if loss_part is None else [loss_part]

    def body(j_ref, mine_ref, got_ref, *rest):
        out_ref = rest[-1]
        j = j_ref[0]
        acc = None
        for jj in range(N_CHIPS):
            term = jnp.where(j == jj, mine_ref[0], got_ref[jj]).astype(F32)
            acc = term if acc is None else acc + term
        out_ref[...] = acc
        if loss_part is not None:
            out_ref[ROW_LOSS:ROW_LOSS + 1, :] = rest[0][0:1, :]

    grid_spec = pltpu.PrefetchScalarGridSpec(
        num_scalar_prefetch=1,
        grid=(rows // tr,),
        in_specs=[pl.BlockSpec((1, tr, cols), lambda r, j_ref: (j_ref[0], r, 0)),
                  pl.BlockSpec((N_CHIPS, tr, cols), lambda r, j_ref: (0, r, 0))] +
                 [pl.BlockSpec(e.shape, lambda r, j_ref: (0, 0)) for e in extra],
        out_specs=pl.BlockSpec((tr, cols), lambda r, j_ref: (r, 0)),
    )
    return pl.pallas_call(
        body,
        name=name,
        grid_spec=grid_spec,
        out_shape=jax.ShapeDtypeStruct((rows, cols), F32),
    )(j_arr, mine, got, *extra)


def _finish_exchange(f_in, f_small):
    def body(fin_ref, fs_ref, rin_ref, os_ref, send_sems, recv_sems, local_sem):
        x, y, c, chips = _position()
        j = 2 * x + y
        me = 2 * j + c
        sibling = (x, y, 1 - c)
        local = pltpu.make_async_copy(fs_ref, os_ref.at[me], local_sem)
        local.start()

        def copy(k, src, dst, to):
            return _remote(src, dst, send_sems, recv_sems, k, to)

        first = [copy(0, fin_ref, rin_ref, sibling), copy(1, fs_ref, os_ref.at[me], sibling)]
        first += [copy(2 + k, fs_ref, os_ref.at[me], (cx, cy, c)) for k, (cx, cy) in enumerate(chips)]
        for cp in first:
            cp.start()
        passed = []
        for k, (cx, cy) in enumerate(chips):
            unit = 2 * (2 * cx + cy) + c
            copy(2 + k, fs_ref, os_ref.at[unit], sibling).wait_recv()
            fwd = copy(5 + k, os_ref.at[unit], os_ref.at[unit], sibling)
            fwd.start()
            passed.append(fwd)
        copy(0, fin_ref, rin_ref, sibling).wait_recv()
        copy(1, fs_ref, os_ref.at[2 * j + 1 - c], sibling).wait_recv()
        for k, (cx, cy) in enumerate(chips):
            unit = 2 * (2 * cx + cy) + 1 - c
            copy(5 + k, fs_ref, os_ref.at[unit], sibling).wait_recv()
        for cp in first + passed:
            cp.wait_send()
        local.wait()

    return pl.pallas_call(
        body,
        name="grad_finish_exchange",
        in_specs=[ANY_SPEC] * 2,
        out_specs=[ANY_SPEC] * 2,
        out_shape=[jax.ShapeDtypeStruct(f_in.shape, F32), jax.ShapeDtypeStruct((N_DEV,) + f_small.shape, F32)],
        scratch_shapes=[pltpu.SemaphoreType.DMA((8,)), pltpu.SemaphoreType.DMA((8,)), pltpu.SemaphoreType.DMA],
    )(f_in, f_small)


def _adamw_math(w, g, m, v):
    m = ADAM_B1 * m + (1.0 - ADAM_B1) * g
    v = ADAM_B2 * v + (1.0 - ADAM_B2) * (g * g)
    m_hat = m / (1.0 - ADAM_B1 ** ADAM_STEP)
    v_hat = v / (1.0 - ADAM_B2 ** ADAM_STEP)
    delta = -ADAM_LR * (m_hat / (jnp.sqrt(v_hat) + ADAM_EPS) + ADAM_WD * w)
    return delta, m, v


def _adamw_big(w, g_mine, g_sib, m, v, c_arr, tr, name):
    rows, cols = w.shape
    half = rows // 2
    per = half // tr

    def body(c_ref, w_ref, gm_ref, gs_ref, m_ref, v_ref, g_ref, d_ref, mo_ref, vo_ref):
        g = jnp.where(pl.program_id(0) == c_ref[0], gm_ref[...], gs_ref[...])
        g_ref[...] = g
        d_ref[...], mo_ref[...], vo_ref[...] = _adamw_math(w_ref[...], g, m_ref[...], v_ref[...])

    full = pl.BlockSpec((tr, cols), lambda h, r, c_ref: (h * per + r, 0))
    unit = pl.BlockSpec((tr, cols), lambda h, r, c_ref: (r, 0))
    grid_spec = pltpu.PrefetchScalarGridSpec(
        num_scalar_prefetch=1,
        grid=(2, per),
        in_specs=[full, unit, unit, full, full],
        out_specs=[full] * 4,
    )
    return pl.pallas_call(
        body,
        name=name,
        grid_spec=grid_spec,
        out_shape=[jax.ShapeDtypeStruct(w.shape, F32)] * 4,
    )(c_arr, w, g_mine, g_sib, m, v)


def _adamw_small(j_arr, packed, params):
    names = list(params)
    n = len(names)

    def body(j_ref, pk_ref, *refs):
        ins = refs[:3 * n]
        outs = refs[3 * n:]
        j = j_ref[0]

        def shard(row, rows):
            return jnp.concatenate([pk_ref[2 * j, row:row + rows, :], pk_ref[2 * j + 1, row:row + rows, :]], axis=1)

        def tail_sum(unit, row, rows):
            start = pl.multiple_of(UNIT_ROWS + TAIL_ROWS * unit + row, SUBLANES)
            total = pk_ref[0, pl.ds(start, rows), :]
            for dev in range(1, N_DEV):
                total = total + pk_ref[dev, pl.ds(start, rows), :]
            return total

        for idx, name in enumerate(names):
            if name == "w_rg":
                g = pk_ref[:, ROW_WR:ROW_WR + LANES, :]
            elif name == "w_ig":
                g = pk_ref[:, ROW_WI:ROW_WI + LANES, :]
            elif name == "meta_tokens":
                g = jnp.concatenate([tail_sum(2 * j, 0, N_META), tail_sum(2 * j + 1, 0, N_META)], axis=1)
            elif name == "norm_gain":
                g = jnp.concatenate([tail_sum(u, N_META, SUBLANES)[0:1] for u in range(N_DEV)], axis=1)
            elif name == "conv_w":
                g = shard(ROW_CONV, 4)
            else:
                row = ROW_VEC + VEC_NAMES.index(name)
                g = jnp.concatenate([pk_ref[u, row:row + 1, :] for u in range(N_DEV)], axis=1)
            w_ref, m_ref, v_ref = ins[3 * idx:3 * idx + 3]
            delta, m, v = _adamw_math(w_ref[...], g, m_ref[...], v_ref[...])
            g_ref, d_ref, mo_ref, vo_ref = outs[4 * idx:4 * idx + 4]
            g_ref[...], d_ref[...], mo_ref[...], vo_ref[...] = g, delta, m, v
        total = pk_ref[0, ROW_LOSS:ROW_LOSS + 1, :]
        for u in range(1, N_DEV):
            total = total + pk_ref[u, ROW_LOSS:ROW_LOSS + 1, :]
        outs[4 * n][...] = jnp.broadcast_to(total, (SUBLANES, LANES))

    flat_in, out_shape = [], []
    for name in names:
        w, m, v = params[name]
        flat_in += [w, m, v]
        out_shape += [jax.ShapeDtypeStruct(w.shape, F32)] * 4
    out_shape.append(jax.ShapeDtypeStruct((SUBLANES, LANES), F32))
    res = pl.pallas_call(
        body,
        name="adamw_small",
        in_specs=[SMEM_SPEC, VMEM_SPEC] + [VMEM_SPEC] * (3 * n),
        out_specs=[VMEM_SPEC] * (4 * n + 1),
        out_shape=out_shape,
    )(j_arr, packed, *flat_in)
    return {name: tuple(res[4 * idx:4 * idx + 4]) for idx, name in enumerate(names)}, res[4 * n][0, 0]


def _units(a):
    rows = a.shape[0]
    return jnp.transpose(a.reshape(rows, N_DEV, LANES), (1, 0, 2))


def kernel(x, meta_tokens, norm_gain, w_in, conv_w, conv_b, w_rg, b_rg, w_ig, b_ig, lru_lambda, ret_norm_gain, w_out, final_norm_gain, loss_target, m_meta_tokens, m_norm_gain, m_w_in, m_conv_w, m_conv_b, m_w_rg, m_b_rg, m_w_ig, m_b_ig, m_lru_lambda, m_ret_norm_gain, m_w_out, m_final_norm_gain, v_meta_tokens, v_norm_gain, v_w_in, v_conv_w, v_conv_b, v_w_rg, v_b_rg, v_w_ig, v_b_ig, v_lru_lambda, v_ret_norm_gain, v_w_out, v_final_norm_gain):
    s_len, d = x.shape[1], x.shape[2]
    d_lru = w_rg.shape[1] * w_rg.shape[2]
    d_ret = ret_norm_gain.shape[1]
    d_qk = HEADS * QK_DIM
    tp = s_len + CHUNK
    tm = TOKEN_TILE
    assert tp % tm == 0 and d_lru == HEADS * LANES and d_ret == HEADS * LANES
    ax, ay, ac = lax.axis_index("x"), lax.axis_index("y"), lax.axis_index("c")
    c_arr = jnp.reshape(ac, (1,)).astype(jnp.int32)
    j_arr = jnp.reshape(2 * ax + ay, (1,)).astype(jnp.int32)

    small = jnp.concatenate([meta_tokens, conv_w[0], jnp.zeros((4, meta_tokens.shape[1]), F32)], axis=0)
    wg, sg = _gather_weights(w_in[0], small)
    cols = sg.shape[2]
    meta_full = jnp.transpose(sg[:, :N_META, :], (1, 0, 2)).reshape(N_META, N_CHIPS * cols)
    cw_full = jnp.transpose(sg[:, N_META:N_META + 4, :], (1, 0, 2)).reshape(4, N_CHIPS * cols)
    cw8 = jnp.concatenate([cw_full, jnp.zeros((4, cw_full.shape[1]), F32)], axis=0)

    half = QK_DIM // 2
    inv = ROPE_BASE ** (-jnp.arange(half, dtype=F32) / half)
    pos = (jnp.arange(tp) - PAD_ROWS).astype(F32)
    ang = pos[:, None] * inv[None, :]
    cos_t = jnp.tile(jnp.cos(ang), (1, LANES // half))
    sign = jnp.where((jnp.arange(LANES) % QK_DIM) < half, -1.0, 1.0).astype(F32)
    sin_t = jnp.tile(jnp.sin(ang), (1, LANES // half)) * sign[None, :]
    tables = _ret_tables()
    gain_f = final_norm_gain.reshape(1, d)

    hp, lx, lg, *qkv, rg, wo4 = _in_proj(x[0], meta_full, norm_gain, wg, cos_t, sin_t, w_out[0], tables[1], tables[2],
                                         tm, d_lru, d_qk, d_ret)
    wo = wo4.reshape(N_CHIPS * wo4.shape[1], wo4.shape[2])
    hl, y_lru, *lru_saved = _lru_fwd(lx, lg, cw8, conv_b, w_rg[0], b_rg, w_ig[0], b_ig, lru_lambda, tm)
    o, y_ret, rprev = _ret_fwd(*qkv, rg, ret_norm_gain, tables, tm)
    dh2, dy_lru, dy_ret, dwo, dgf, loss_acc = _out_proj_loss(y_lru, y_ret, hp, loss_target[0], wo, gain_f, tm)

    g_out = dwo.reshape(N_DEV, dwo.shape[0] // N_DEV, dwo.shape[1])
    (dq, dk, dv, drg, dgain), (r_out,) = _ret_bwd(*qkv, rg, o, rprev, dy_ret, ret_norm_gain, cos_t, sin_t, tables,
                                                 tm, ride=_pair_ride([g_out]))
    q_out = _pair_sum(g_out, r_out, c_arr, 128, "grad_pair_sum_out")
    (dlx, dlg, g_small), (e_out,) = _lru_bwd(lx, lg, hl, dy_lru, lru_saved, cw8, w_rg[0], w_ig[0], lru_lambda,
                                            dgain, dgf, tm, ride=_chip_ride([q_out]))
    f_out = _chip_sum(q_out, e_out, j_arr, 128, "grad_chip_sum_out")
    dparts = [dlx, dlg, dq, dk, dv, drg]
    dwg, (s_out, r_small) = _in_proj_dw(dparts, hp, norm_gain, wg.shape,
                                        ride=_join_rides(_sibling_ride([f_out]), _pair_ride([g_small])))
    g_in = dwg.reshape(N_DEV, dwg.shape[1] // 2, dwg.shape[2])
    q_in = _pair_exchange_sum(g_in, c_arr, 128, "grad_pair_exchange_sum_in")
    q_small = _pair_sum(g_small, r_small, c_arr, UNIT_ROWS, "grad_pair_sum_small")
    (grad_x, dmeta, dg1), (e_in, e_small) = _in_proj_dx(dparts, hp, dh2, norm_gain, wg, s_len, tm,
                                                        ride=_chip_ride([q_in, q_small]))
    f_in = _chip_sum(q_in, e_in, j_arr, 128, "grad_chip_sum_in")
    f_small = _chip_sum(q_small, e_small, j_arr, UNIT_ROWS, "grad_chip_sum_small", loss_part=loss_acc)
    tail = jnp.concatenate([_units(dmeta), _units(dg1), jnp.zeros((N_DEV, TAIL_ROWS - N_META - 1, LANES), F32)],
                           axis=1).reshape(N_DEV * TAIL_ROWS, LANES)
    s_in, o_small = _finish_exchange(f_in, jnp.concatenate([f_small, tail], axis=0))

    res_in = _adamw_big(w_in[0], f_in, s_in, m_w_in[0], v_w_in[0], c_arr, 256, "adamw_w_in")
    res_out = _adamw_big(w_out[0], f_out, s_out, m_w_out[0], v_w_out[0], c_arr, 256, "adamw_w_out")
    small_params = {
        "meta_tokens": (meta_tokens, m_meta_tokens, v_meta_tokens),
        "norm_gain": (norm_gain, m_norm_gain, v_norm_gain),
        "conv_w": (conv_w[0], m_conv_w[0], v_conv_w[0]),
        "conv_b": (conv_b, m_conv_b, v_conv_b),
        "w_rg": (w_rg[0], m_w_rg[0], v_w_rg[0]),
        "b_rg": (b_rg, m_b_rg, v_b_rg),
        "w_ig": (w_ig[0], m_w_ig[0], v_w_ig[0]),
        "b_ig": (b_ig, m_b_ig, v_b_ig),
        "lru_lambda": (lru_lambda, m_lru_lambda, v_lru_lambda),
        "ret_norm_gain": (ret_norm_gain, m_ret_norm_gain, v_ret_norm_gain),
        "final_norm_gain": (gain_f, m_final_norm_gain.reshape(1, d), v_final_norm_gain.reshape(1, d)),
    }
    res, loss = _adamw_small(j_arr, o_small, small_params)
    res["w_in"] = tuple(res_in)
    res["w_out"] = tuple(res_out)

    order = ["meta_tokens", "norm_gain", "w_in", "conv_w", "conv_b", "w_rg", "b_rg", "w_ig", "b_ig", "lru_lambda",
             "ret_norm_gain", "w_out", "final_norm_gain"]
    shapes = {"w_in": w_in.shape, "conv_w": conv_w.shape, "w_rg": w_rg.shape, "w_ig": w_ig.shape,
              "w_out": w_out.shape, "final_norm_gain": final_norm_gain.shape}
    outs = [loss, grad_x.reshape(x.shape)]
    for kind in range(4):
        for name in order:
            a = res[name][kind]
            outs.append(a.reshape(shapes[name]) if name in shapes else a)
    return tuple(outs)
```

```python
import functools

import jax
import jax.numpy as jnp
from jax import lax
from jax.experimental import pallas as pl
from jax.experimental.pallas import tpu as pltpu

F32 = jnp.float32
BF16 = jnp.bfloat16

N_META = 16
CHUNK = 128
PAD_ROWS = CHUNK - N_META
HEADS = 8
QK_DIM = 64
LANES = 128
SUBLANES = 8
LRU_C = 8.0
EPS = 1e-6
ROPE_BASE = 10000.0
ADAM_LR = 0.001
ADAM_B1 = 0.9
ADAM_B2 = 0.999
ADAM_EPS = 1e-08
ADAM_WD = 0.01
ADAM_STEP = 10
N_CHIPS = 4
N_DEV = 8
TOKEN_TILE = 384
VMEM_LIMIT = 58 * 1024 * 1024
MESH = pl.DeviceIdType.MESH

VMEM_SPEC = pl.BlockSpec(memory_space=pltpu.VMEM)
SMEM_SPEC = pl.BlockSpec(memory_space=pltpu.SMEM)
ANY_SPEC = pl.BlockSpec(memory_space=pl.ANY)

ROW_WR, ROW_WI, ROW_META, ROW_CONV, ROW_VEC, UNIT_ROWS = 0, 128, 256, 272, 276, 288
N_VEC = 7
VEC_NAMES = ["norm_gain", "conv_b", "b_rg", "b_ig", "lru_lambda", "ret_norm_gain", "final_norm_gain"]
ROW_LOSS = ROW_VEC + N_VEC
TAIL_ROWS = 24


def _dot(a, b):
    return jnp.dot(a, b, preferred_element_type=F32)


def _dot_nt(a, b):
    return lax.dot_general(a, b, (((1,), (1,)), ((), ())), preferred_element_type=F32)


def _dot_tn(a, b):
    return lax.dot_general(a, b, (((0,), (0,)), ((), ())), preferred_element_type=F32)


def _sigmoid(x):
    return 0.5 * jnp.tanh(0.5 * x) + 0.5


def _shift_down(x, prev8, s):
    rolled = pltpu.roll(x, s, 0)
    rows = lax.broadcasted_iota(jnp.int32, (SUBLANES, x.shape[1]), 0)
    top = jnp.where(rows < s, pltpu.roll(prev8, s, 0), rolled[0:SUBLANES])
    return jnp.concatenate([top, rolled[SUBLANES:]], axis=0)


def _shift_up(x, next8, s):
    n = x.shape[0]
    rolled = pltpu.roll(x, n - s, 0)
    rows = lax.broadcasted_iota(jnp.int32, (SUBLANES, x.shape[1]), 0)
    bot = jnp.where(rows >= SUBLANES - s, pltpu.roll(next8, SUBLANES - s, 0), rolled[n - SUBLANES:n])
    return jnp.concatenate([rolled[:n - SUBLANES], bot], axis=0)


def _rot_partner(t):
    w = t.shape[1]
    lane = lax.broadcasted_iota(jnp.int32, t.shape, 1)
    first = (lane % QK_DIM) < (QK_DIM // 2)
    return jnp.where(first, pltpu.roll(t, w - QK_DIM // 2, 1), pltpu.roll(t, QK_DIM // 2, 1))


def _tile_lanes(t, reps):
    return jnp.concatenate([t] * reps, axis=1)


class _Ride:
    def __init__(self, srcs, dst_shapes, n_copies, make, to_sibling=False, to_chips=False):
        self.srcs, self.dst_shapes, self.n_copies, self.make = list(srcs), list(dst_shapes), n_copies, make
        self.to_sibling, self.to_chips = to_sibling, to_chips

    def peers(self):
        x, y, c, chips = _position()
        return ([(x, y, 1 - c)] if self.to_sibling else []) + ([(cx, cy, c) for cx, cy in chips] if self.to_chips else [])


def _join_rides(a, b):
    def make(src, dst, send_sems, recv_sems, base):
        na, da = len(a.srcs), len(a.dst_shapes)
        return (a.make(src[:na], dst[:da], send_sems, recv_sems, base)
                + b.make(src[na:], dst[da:], send_sems, recv_sems, base + a.n_copies))

    return _Ride(a.srcs + b.srcs, a.dst_shapes + b.dst_shapes, a.n_copies + b.n_copies, make,
                 a.to_sibling or b.to_sibling, a.to_chips or b.to_chips)


def _position():
    x, y, c = lax.axis_index("x"), lax.axis_index("y"), lax.axis_index("c")
    return x, y, c, [(1 - x, y), (x, 1 - y), (1 - x, 1 - y)]


def _remote(src, dst, send_sems, recv_sems, k, to):
    return pltpu.make_async_remote_copy(src_ref=src, dst_ref=dst, send_sem=send_sems.at[k], recv_sem=recv_sems.at[k],
                                        device_id=to, device_id_type=MESH)


def _pair_ride(bufs):
    def make(src, dst, send_sems, recv_sems, base):
        x, y, c, _ = _position()
        return [_remote(src[b].at[2 * jj + 1 - c], dst[b].at[jj], send_sems, recv_sems, base + b * N_CHIPS + jj,
                        (x, y, 1 - c)) for b in range(len(bufs)) for jj in range(N_CHIPS)]

    shapes = [jax.ShapeDtypeStruct((N_CHIPS,) + b.shape[1:], b.dtype) for b in bufs]
    return _Ride(bufs, shapes, N_CHIPS * len(bufs), make, to_sibling=True)


def _chip_ride(bufs):
    def make(src, dst, send_sems, recv_sems, base):
        x, y, c, chips = _position()
        return [_remote(src[b].at[2 * cx + cy], dst[b].at[2 * x + y], send_sems, recv_sems, base + b * 3 + k,
                        (cx, cy, c)) for b in range(len(bufs)) for k, (cx, cy) in enumerate(chips)]

    shapes = [jax.ShapeDtypeStruct(b.shape, b.dtype) for b in bufs]
    return _Ride(bufs, shapes, 3 * len(bufs), make, to_chips=True)


def _sibling_ride(bufs):
    def make(src, dst, send_sems, recv_sems, base):
        x, y, c, _ = _position()
        return [_remote(src[b], dst[b], send_sems, recv_sems, base + b, (x, y, 1 - c)) for b in range(len(bufs))]

    shapes = [jax.ShapeDtypeStruct(b.shape, b.dtype) for b in bufs]
    return _Ride(bufs, shapes, len(bufs), make, to_sibling=True)


def _hosted_call(body, ride, n_steps, *, name, in_specs, out_specs, out_shape, scratch_shapes, args, barrier_id=None):
    params = pltpu.CompilerParams(dimension_semantics=("arbitrary",), vmem_limit_bytes=VMEM_LIMIT,
                                  collective_id=barrier_id if ride is not None else None)
    if ride is None:
        res = pl.pallas_call(body, name=name, grid=(n_steps,), in_specs=list(in_specs), out_specs=list(out_specs),
                             out_shape=list(out_shape), scratch_shapes=list(scratch_shapes),
                             compiler_params=params)(*args)
        return list(res), []
    sizes = [len(in_specs), len(ride.srcs), len(out_specs), len(ride.dst_shapes), len(scratch_shapes), 2]

    def hosted(*refs):
        groups, pos = [], 0
        for n in sizes:
            groups.append(refs[pos:pos + n])
            pos += n
        ins, rin, outs, rout, scr, (send_sems, recv_sems) = groups
        i = pl.program_id(0)

        @pl.when(i == 0)
        def _():
            if barrier_id is not None:
                barrier = pltpu.get_barrier_semaphore()
                peers = ride.peers()
                for peer in peers:
                    pl.semaphore_signal(barrier, inc=1, device_id=peer, device_id_type=MESH)
                pl.semaphore_wait(barrier, len(peers))
            for cp in ride.make(rin, rout, send_sems, recv_sems, 0):
                cp.start()

        body(*ins, *outs, *scr)

        @pl.when(i == n_steps - 1)
        def _():
            for cp in ride.make(rin, rout, send_sems, recv_sems, 0):
                cp.wait()

    n_out = len(out_specs)
    res = pl.pallas_call(
        hosted,
        name=name,
        grid=(n_steps,),
        in_specs=list(in_specs) + [ANY_SPEC] * len(ride.srcs),
        out_specs=list(out_specs) + [ANY_SPEC] * len(ride.dst_shapes),
        out_shape=list(out_shape) + ride.dst_shapes,
        scratch_shapes=list(scratch_shapes) + [pltpu.SemaphoreType.DMA((ride.n_copies,)),
                                               pltpu.SemaphoreType.DMA((ride.n_copies,))],
        compiler_params=params,
    )(*args, *ride.srcs)
    return list(res[:n_out]), list(res[n_out:])


def _gather_weights(w_in, small):
    r_in, c_in = w_in.shape
    h_in = r_in // 2
    q_in = h_in // 2

    def body(win_ref, small_ref, wg_ref, sg_ref, send_sems, recv_sems):
        x, y, c, chips = _position()
        j = 2 * x + y
        sibling = (x, y, 1 - c)
        xn, yn, dg = chips
        jx, jy, jd = (2 * cx + cy for cx, cy in chips)

        def half(jj, cc):
            return wg_ref.at[jj, pl.ds(cc * h_in, h_in), :]

        def quarter(jj, qq):
            return wg_ref.at[jj, pl.ds(c * h_in + qq * q_in, q_in), :]

        def copy(k, ref, to):
            return _remote(ref, ref, send_sems, recv_sems, k, to)

        def cast_rows(start, rows):
            start = pl.multiple_of(start, q_in)
            wg_ref[j, pl.ds(start, rows), :] = win_ref[pl.ds(start, rows), :].astype(BF16)

        first = [copy(0, quarter(j, 0), (*xn, c)), copy(2, quarter(j, 1), (*yn, c)),
                 copy(1, quarter(j, 1), (*xn, c)), copy(3, quarter(j, 0), (*yn, c))]
        sg_ref[j] = small_ref[...]
        cast_rows(c * h_in, q_in)
        first[0].start()
        cast_rows(c * h_in + q_in, q_in)
        for cp in first[1:]:
            cp.start()
        small_copies = [copy(9 + k, sg_ref.at[j], (cx, cy, c)) for k, (cx, cy) in enumerate(chips)]
        for cp in small_copies:
            cp.start()
        first += small_copies
        cast_rows((1 - c) * h_in, h_in)
        copy(0, quarter(jx, 0), sibling).wait_recv()
        along_y = copy(4, quarter(jx, 0), (*yn, c))
        along_y.start()
        copy(2, quarter(jy, 1), sibling).wait_recv()
        along_x = copy(5, quarter(jy, 1), (*xn, c))
        along_x.start()
        copy(1, quarter(jx, 1), sibling).wait_recv()
        to_sib = [copy(6, half(jx, c), sibling)]
        to_sib[-1].start()
        copy(3, quarter(jy, 0), sibling).wait_recv()
        to_sib.append(copy(7, half(jy, c), sibling))
        to_sib[-1].start()
        copy(4, quarter(jd, 0), sibling).wait_recv()
        copy(5, quarter(jd, 1), sibling).wait_recv()
        to_sib.append(copy(8, half(jd, c), sibling))
        to_sib[-1].start()
        for k, jk in enumerate((jx, jy, jd)):
            copy(6 + k, half(jk, 1 - c), sibling).wait_recv()
            copy(9 + k, sg_ref.at[jk], sibling).wait_recv()
        for cp in first + [along_y, along_x] + to_sib:
            cp.wait_send()

    return pl.pallas_call(
        body,
        name="gather_weights",
        out_shape=(jax.ShapeDtypeStruct((N_CHIPS, r_in, c_in), BF16),
                   jax.ShapeDtypeStruct((N_CHIPS,) + small.shape, F32)),
        in_specs=[VMEM_SPEC, VMEM_SPEC],
        out_specs=(VMEM_SPEC, VMEM_SPEC),
        scratch_shapes=[pltpu.SemaphoreType.DMA((12,)), pltpu.SemaphoreType.DMA((12,))],
        compiler_params=pltpu.CompilerParams(vmem_limit_bytes=VMEM_LIMIT),
    )(w_in, small)


def _proj_segments(d_lru, d_qk, d_ret, chunk_w):
    widths = [d_lru, d_lru, d_qk, d_qk, d_ret, d_ret]
    segs, col = [], 0
    for w in widths:
        parts, off = [], 0
        while off < w:
            jj, inner = divmod(col + off, chunk_w)
            take = min(w - off, chunk_w - inner)
            parts.append((jj, inner, off, take))
            off += take
        segs.append(parts)
        col += w
    return segs


def _in_proj(x2, meta, gain, wg, cos_t, sin_t, w_out, qdec, kdec, tm, d_lru, d_qk, d_ret):
    s_len, d = x2.shape
    tp = s_len + CHUNK
    nt, nb = tp // tm, tm // CHUNK
    segs = _proj_segments(d_lru, d_qk, d_ret, wg.shape[2])
    outs = [(d, F32), (d_lru, F32), (d_lru, F32)] + [(d_qk, BF16)] * 4 + [(d_ret, BF16), (d_ret, F32)]
    r_out, c_out = w_out.shape
    h_out = r_out // 2
    fwd_step = min(6, nt - 1)

    def gather_w_out(i, wout_ref, wo_ref, wob, send_sems, recv_sems, local_sem):
        x, y, c, chips = _position()
        j = 2 * x + y
        sibling = (x, y, 1 - c)

        def half(jj, cc):
            return wo_ref.at[jj, pl.ds(cc * h_out, h_out), :]

        local = pltpu.make_async_copy(wob, wo_ref.at[j], local_sem)
        first = [_remote(wob.at[pl.ds(c * h_out, h_out), :], half(j, c), send_sems, recv_sems, k, (cx, cy, c))
                 for k, (cx, cy) in enumerate(chips)]
        passed = [_remote(half(2 * cx + cy, c), half(2 * cx + cy, c), send_sems, recv_sems, 3 + k, sibling)
                  for k, (cx, cy) in enumerate(chips)]

        @pl.when(i == 0)
        def _():
            wob[...] = wout_ref[...].astype(BF16)
            local.start()
            for cp in first:
                cp.start()

        @pl.when(i == fwd_step)
        def _():
            for k, (cx, cy) in enumerate(chips):
                _remote(half(2 * cx + cy, c), half(2 * cx + cy, c), send_sems, recv_sems, k, sibling).wait_recv()
                passed[k].start()

        @pl.when(i == nt - 1)
        def _():
            for k, (cx, cy) in enumerate(chips):
                jk = 2 * cx + cy
                _remote(half(jk, 1 - c), half(jk, 1 - c), send_sems, recv_sems, 3 + k, sibling).wait_recv()
            for cp in first + passed:
                cp.wait_send()
            local.wait()

    def body(*refs):
        xb = refs[:nb]
        meta_ref, g_ref, w_ref, cos_ref, sin_ref, wout_ref, qdec_ref, kdec_ref = refs[nb:nb + 8]
        hp_ref, lx_ref, lg_ref, qb_ref, kb_ref, qd_ref, kd_ref, vb_ref, rg_ref = refs[nb + 8:nb + 17]
        wo_ref, q_s, k_s, wob, send_sems, recv_sems, local_sem = refs[nb + 17:]
        i = pl.program_id(0)
        gather_w_out(i, wout_ref, wo_ref, wob, send_sems, recv_sems, local_sem)
        blocks = [r[...] for r in xb]
        head = jnp.concatenate([jnp.zeros((PAD_ROWS, d), F32), meta_ref[...]], axis=0)
        blocks[0] = jnp.where(i == 0, head, blocks[0])
        h = jnp.concatenate(blocks, axis=0)
        hp_ref[...] = h
        rinv = lax.rsqrt(jnp.mean(h * h, axis=-1, keepdims=True) + EPS)
        u = ((h * rinv) * g_ref[...]).astype(BF16)
        for out_ref, parts in zip([lx_ref, lg_ref, q_s, k_s, vb_ref, rg_ref], segs):
            for jj, inner, off, take in parts:
                out_ref[:, off:off + take] = _dot(u, w_ref[jj, :, inner:inner + take]).astype(out_ref.dtype)
        cos = _tile_lanes(cos_ref[...], d_qk // LANES)
        sin = _tile_lanes(sin_ref[...], d_qk // LANES)
        q = q_s[...]
        q = q * cos + _rot_partner(q) * sin
        k = k_s[...]
        k = (k * cos + _rot_partner(k) * sin) * (QK_DIM ** -0.5)
        qb_ref[...] = q.astype(BF16)
        kb_ref[...] = k.astype(BF16)
        qd_ref[...] = (q * jnp.concatenate([qdec_ref[...]] * nb, axis=0)).astype(BF16)
        kd_ref[...] = (k * jnp.concatenate([kdec_ref[...]] * nb, axis=0)).astype(BF16)

    x_specs = [pl.BlockSpec((CHUNK, d), functools.partial(lambda i, b: (jnp.maximum(i * nb + b - 1, 0), 0), b=b))
               for b in range(nb)]
    tile = lambda w: pl.BlockSpec((tm, w), lambda i: (i, 0))
    return pl.pallas_call(
        body,
        name="in_proj",
        grid=(nt,),
        in_specs=x_specs + [pl.BlockSpec(meta.shape, lambda i: (0, 0)),
                            pl.BlockSpec(gain.shape, lambda i: (0, 0)),
                            pl.BlockSpec(wg.shape, lambda i: (0, 0, 0)),
                            tile(LANES), tile(LANES),
                            pl.BlockSpec(w_out.shape, lambda i: (0, 0)),
                            pl.BlockSpec(qdec.shape, lambda i: (0, 0)), pl.BlockSpec(kdec.shape, lambda i: (0, 0))],
        out_specs=[tile(w) for w, _ in outs] + [ANY_SPEC],
        out_shape=[jax.ShapeDtypeStruct((tp, w), dt) for w, dt in outs]
                  + [jax.ShapeDtypeStruct((N_CHIPS, r_out, c_out), BF16)],
        scratch_shapes=[pltpu.VMEM((tm, d_qk), F32), pltpu.VMEM((tm, d_qk), F32),
                        pltpu.VMEM((r_out, c_out), BF16), pltpu.SemaphoreType.DMA((6,)),
                        pltpu.SemaphoreType.DMA((6,)), pltpu.SemaphoreType.DMA],
        compiler_params=pltpu.CompilerParams(dimension_semantics=("arbitrary",), vmem_limit_bytes=VMEM_LIMIT),
    )(*([x2] * nb), meta, gain, wg, cos_t, sin_t, w_out, qdec, kdec)


def _segment_scan(a3, u3, out3, p3, carry, tm, reverse):
    groups = a3.shape[0]
    seg = tm // SUBLANES

    def step(j, state):
        hs, ps = state
        rows = pl.ds((seg - 1 - j) if reverse else j, SUBLANES, stride=seg)
        new_h, new_p = [], []
        for g in range(groups):
            a = a3[g, rows, :]
            h = a * hs[g] + u3[g, rows, :]
            p = ps[g] * a
            out3[g, rows, :] = h
            p3[g, rows, :] = p
            new_h.append(h)
            new_p.append(p)
        return tuple(new_h), tuple(new_p)

    zeros = tuple(jnp.zeros((SUBLANES, LANES), F32) for _ in range(groups))
    ones = tuple(jnp.ones((SUBLANES, LANES), F32) for _ in range(groups))
    lax.fori_loop(0, seg, step, (zeros, ones))
    carries = [carry[:, g * LANES:(g + 1) * LANES] for g in range(groups)]
    for s in (reversed(range(SUBLANES)) if reverse else range(SUBLANES)):
        rows = slice(s * seg, (s + 1) * seg)
        edge = s * seg if reverse else (s + 1) * seg - 1
        for g in range(groups):
            out3[g, rows, :] = out3[g, rows, :] + p3[g, rows, :] * carries[g]
            carries[g] = out3[g, edge:edge + 1, :]
    return jnp.concatenate(carries, axis=1)


def _softplus_neg(lam):
    z = -lam
    e = jnp.exp(-jnp.abs(z))
    e1 = 1.0 + e
    log1p_e = jnp.where(e1 == 1.0, e, jnp.log(e1) * (e / (e1 - 1.0)))
    return jnp.maximum(z, 0.0) + log1p_e


def _lru_fwd(lx, lg, cw, cb, wr, br, wi, bi, lam, tm):
    tp, w = lx.shape
    nt = tp // tm
    per8 = tm // SUBLANES
    n_heads = wr.shape[0]

    def body(lx_ref, lxp_ref, lg_ref, cw_ref, cb_ref, wr_ref, br_ref, wi_ref, bi_ref, lam_ref,
             hl_ref, y_ref, xc_ref, r_ref, ig_ref, a_ref, beta_ref, w4_ref, a_s, u_s, h_s, p_s, carry):
        i = pl.program_id(0)

        @pl.when(i == 0)
        def _():
            carry[...] = jnp.zeros_like(carry)

        sp = _softplus_neg(lam_ref[...])
        row = lax.broadcasted_iota(jnp.int32, (tm, 1), 0) + i * tm
        for hd in range(n_heads):
            hs = slice(hd * LANES, (hd + 1) * LANES)
            lxv = lx_ref[:, hs]
            prev8 = jnp.where(i == 0, 0.0, lxp_ref[:, hs])
            xc = cb_ref[:, hs] + _shift_down(lxv, prev8, 3) * cw_ref[0:1, hs]
            xc = xc + _shift_down(lxv, prev8, 2) * cw_ref[1:2, hs]
            xc = xc + _shift_down(lxv, prev8, 1) * cw_ref[2:3, hs]
            xc = xc + lxv * cw_ref[3:4, hs]
            xc_ref[:, hs] = xc
            xh = xc.astype(BF16)
            r = _sigmoid(_dot(xh, wr_ref[hd].astype(BF16)) + br_ref[:, hs])
            ig = _sigmoid(_dot(xh, wi_ref[hd].astype(BF16)) + bi_ref[:, hs])
            r_ref[:, hs] = r
            ig_ref[:, hs] = ig
            log_a = (-LRU_C * r) * sp[:, hs]
            a = jnp.exp(log_a)
            a_ref[:, hs] = a
            a2 = a * a
            beta2 = jnp.maximum((1.0 + a2) * jnp.tanh(-log_a), 1e-37)
            rsb = lax.rsqrt(beta2)
            beta = beta2 * rsb
            beta_ref[:, hs] = beta
            w4_ref[:, hs] = a2 * rsb
            a_s[hd] = a
            u_s[hd] = jnp.where(row >= PAD_ROWS, beta * ig * xc, 0.0)
        carry[0:1, :] = _segment_scan(a_s, u_s, h_s, p_s, carry[0:1, :], tm, reverse=False)
        for hd in range(n_heads):
            hs = slice(hd * LANES, (hd + 1) * LANES)
            hl = h_s[hd]
            hl_ref[:, hs] = hl
            g = lg_ref[:, hs]
            y_ref[:, hs] = (hl * (g * _sigmoid(g))).astype(BF16)

    tile = pl.BlockSpec((tm, w), lambda i: (i, 0))
    prev = pl.BlockSpec((SUBLANES, w), lambda i: (jnp.maximum(i * per8 - 1, 0), 0))
    vec = pl.BlockSpec((1, w), lambda i: (0, 0))
    mat = pl.BlockSpec(wr.shape, lambda i: (0, 0, 0))
    f32_out = jax.ShapeDtypeStruct((tp, w), F32)
    return pl.pallas_call(
        body,
        name="lru_fwd",
        grid=(nt,),
        in_specs=[tile, prev, tile, pl.BlockSpec(cw.shape, lambda i: (0, 0)), vec, mat, vec, mat, vec, vec],
        out_specs=[tile] * 8,
        out_shape=[f32_out, jax.ShapeDtypeStruct((tp, w), BF16)] + [f32_out] * 6,
        scratch_shapes=[pltpu.VMEM((w // LANES, tm, LANES), F32)] * 4 + [pltpu.VMEM((SUBLANES, w), F32)],
        compiler_params=pltpu.CompilerParams(dimension_semantics=("arbitrary",), vmem_limit_bytes=VMEM_LIMIT),
    )(lx, lx, lg, cw, cb, wr, br, wi, bi, lam)


def _ret_tables():
    log_g = jnp.log1p(-jnp.exp2(-5.0 - jnp.arange(HEADS, dtype=F32)))
    idx = jnp.arange(CHUNK, dtype=F32)
    diff = idx[:, None] - idx[None, :]
    dmask = jnp.where(diff[None] >= 0.0, jnp.exp(jnp.maximum(diff, 0.0)[None] * log_g[:, None, None]), 0.0)
    kdec = jnp.repeat(jnp.exp((CHUNK - 1.0 - idx)[:, None] * log_g[None, :]), QK_DIM, axis=1)
    qdec = jnp.repeat(jnp.exp((idx + 1.0)[:, None] * log_g[None, :]), QK_DIM, axis=1)
    g_chunk = jnp.exp(CHUNK * log_g)
    g_rows = jnp.repeat(g_chunk, QK_DIM).reshape(HEADS // 2, 2 * QK_DIM, 1)
    g_state = jnp.broadcast_to(g_rows, (HEADS // 2, 2 * QK_DIM, 2 * LANES))
    r_head = jnp.arange(2 * QK_DIM)[:, None] // QK_DIM
    c_head = jnp.arange(2 * LANES)[None, :] // LANES
    block_diag = (r_head == c_head).astype(F32)
    return dmask, qdec, kdec, g_state, block_diag


def _head_norm(o_h):
    mu = jnp.mean(o_h, axis=-1, keepdims=True)
    oc = o_h - mu
    var = jnp.mean(oc * oc, axis=-1, keepdims=True)
    rstd = lax.rsqrt(var + EPS)
    return oc * rstd, rstd


def _ret_fwd(qb, kb, qd, kd, vb, rg, gain, tables, tm):
    tp, d_qk = qb.shape
    d_ret = vb.shape[1]
    n_ch = tp // CHUNK
    cps = tm // CHUNK
    n_pairs = HEADS // 2
    dmask, _, _, g_state, block_diag = tables

    def body(q_ref, k_ref, qd_ref, kd_ref, v_ref, rg_ref, gain_ref, dm_ref, gs_ref, bd_ref,
             o_ref, y_ref, rp_ref, state):
        n = pl.program_id(0)

        @pl.when(n == 0)
        def _():
            state[...] = jnp.zeros_like(state)

        lane = lax.broadcasted_iota(jnp.int32, (CHUNK, LANES), 1)
        for ci in range(cps):
            rs = slice(ci * CHUNK, (ci + 1) * CHUNK)
            for p in range(n_pairs):
                qs = slice(p * LANES, (p + 1) * LANES)
                vs = slice(p * 2 * LANES, (p + 1) * 2 * LANES)
                qp, kb = q_ref[rs, qs], k_ref[rs, qs]
                vb = v_ref[rs, vs]
                qd, kd = qd_ref[rs, qs], kd_ref[rs, qs]
                st = state[p]
                st_b = st.astype(BF16)
                rp_ref[ci, p] = st_b
                cross = _dot(qd, st_b)
                for e in range(2):
                    hd = 2 * p + e
                    hs = slice(hd * LANES, (hd + 1) * LANES)
                    es = slice(e * LANES, (e + 1) * LANES)
                    qm = jnp.where((lane // QK_DIM) == e, qp, jnp.zeros_like(qp))
                    s = _dot_nt(qm, kb) * dm_ref[hd]
                    o_h = _dot(s.astype(BF16), vb[:, es]) + cross[:, es]
                    o_ref[rs, hs] = o_h
                    xhat, _ = _head_norm(o_h)
                    g = rg_ref[rs, hs]
                    y_ref[rs, hs] = ((xhat * gain_ref[:, hs]) * (g * _sigmoid(g))).astype(BF16)
                state[p] = gs_ref[p] * st + bd_ref[...] * _dot_tn(kd, vb)

    ch = lambda w: pl.BlockSpec((tm, w), lambda n: (n, 0))
    const2 = lambda a: pl.BlockSpec(a.shape, lambda n: (0, 0))
    const3 = lambda a: pl.BlockSpec(a.shape, lambda n: (0, 0, 0))
    return pl.pallas_call(
        body,
        name="ret_fwd",
        grid=(n_ch // cps,),
        in_specs=[ch(d_qk)] * 4 + [ch(d_ret), ch(d_ret), const2(gain), const3(dmask), const3(g_state),
                                   const2(block_diag)],
        out_specs=[ch(d_ret), ch(d_ret),
                   pl.BlockSpec((cps, n_pairs, 2 * QK_DIM, 2 * LANES), lambda n: (n, 0, 0, 0))],
        out_shape=[jax.ShapeDtypeStruct((tp, d_ret), F32), jax.ShapeDtypeStruct((tp, d_ret), BF16),
                   jax.ShapeDtypeStruct((n_ch, n_pairs, 2 * QK_DIM, 2 * LANES), BF16)],
        scratch_shapes=[pltpu.VMEM((n_pairs, 2 * QK_DIM, 2 * LANES), F32)],
        compiler_params=pltpu.CompilerParams(dimension_semantics=("arbitrary",), vmem_limit_bytes=VMEM_LIMIT),
    )(qb, kb, qd, kd, vb, rg, gain, dmask, g_state, block_diag)


def _out_proj_loss(y_lru, y_ret, hp, tgt, wo, gain_f, tm):
    tp, d = hp.shape
    w_lru = y_lru.shape[1]
    w_mix = wo.shape[0]
    nt, nb = tp // tm, tm // CHUNK

    def body(*refs):
        yl_ref, yr_ref, hp_ref = refs[:3]
        tb = refs[3:3 + nb]
        wo_ref, gf_ref = refs[3 + nb:5 + nb]
        dh2_ref, dyl_ref, dyr_ref, dwo_ref, dgf_ref, loss_ref = refs[5 + nb:]
        i = pl.program_id(0)

        @pl.when(i == 0)
        def _():
            dwo_ref[...] = jnp.zeros_like(dwo_ref)
            dgf_ref[...] = jnp.zeros_like(dgf_ref)
            loss_ref[...] = jnp.zeros_like(loss_ref)

        yl, yr = yl_ref[...], yr_ref[...]
        h2 = hp_ref[...] + _dot(yl, wo_ref[0:w_lru, :]) + _dot(yr, wo_ref[w_lru:w_mix, :])
        rinv = lax.rsqrt(jnp.mean(h2 * h2, axis=-1, keepdims=True) + EPS)
        nrm = h2 * rinv
        gf = gf_ref[...]
        tgt_v = jnp.concatenate([r[...] for r in tb], axis=0)
        row = lax.broadcasted_iota(jnp.int32, (tm, 1), 0) + i * tm
        err = jnp.where(row >= CHUNK, nrm * gf - tgt_v, 0.0)
        loss_ref[...] += 0.5 * jnp.sum(jnp.mean(err * err, axis=-1, keepdims=True))
        dout = err * (1.0 / d)
        dgf_ref[...] += jnp.sum(dout * nrm, axis=0, keepdims=True)
        dn = dout * gf
        dh2 = rinv * (dn - nrm * jnp.mean(dn * nrm, axis=-1, keepdims=True))
        dh2_ref[...] = dh2
        dh2b = dh2.astype(BF16)
        dyl_ref[...] = _dot_nt(dh2b, wo_ref[0:w_lru, :])
        dyr_ref[...] = _dot_nt(dh2b, wo_ref[w_lru:w_mix, :])
        dwo_ref[0:w_lru, :] += _dot_tn(yl, dh2b)
        dwo_ref[w_lru:w_mix, :] += _dot_tn(yr, dh2b)

    tile = lambda w: pl.BlockSpec((tm, w), lambda i: (i, 0))
    t_specs = [pl.BlockSpec((CHUNK, d), functools.partial(lambda i, b: (jnp.maximum(i * nb + b - 1, 0), 0), b=b))
               for b in range(nb)]
    return pl.pallas_call(
        body,
        name="out_proj_loss",
        grid=(nt,),
        in_specs=[tile(w_lru), tile(w_mix - w_lru), tile(d)] + t_specs +
                 [pl.BlockSpec(wo.shape, lambda i: (0, 0)), pl.BlockSpec(gain_f.shape, lambda i: (0, 0))],
        out_specs=[tile(d), tile(w_lru), tile(w_mix - w_lru), pl.BlockSpec(wo.shape, lambda i: (0, 0)),
                   pl.BlockSpec((1, d), lambda i: (0, 0)), pl.BlockSpec((SUBLANES, LANES), lambda i: (0, 0))],
        out_shape=[jax.ShapeDtypeStruct((tp, d), F32), jax.ShapeDtypeStruct((tp, w_lru), F32),
                   jax.ShapeDtypeStruct((tp, w_mix - w_lru), F32), jax.ShapeDtypeStruct(wo.shape, F32),
                   jax.ShapeDtypeStruct((1, d), F32), jax.ShapeDtypeStruct((SUBLANES, LANES), F32)],
        compiler_params=pltpu.CompilerParams(dimension_semantics=("arbitrary",), vmem_limit_bytes=VMEM_LIMIT),
    )(y_lru, y_ret, hp, *([tgt] * nb), wo, gain_f)


def _ret_bwd(qb, kb, qd, kd, vb, rg, o, rprev, dy, gain, cos_t, sin_t, tables, tm, ride=None):
    tp, d_qk = qb.shape
    d_ret = vb.shape[1]
    n_ch = tp // CHUNK
    cps = tm // CHUNK
    n_pairs = HEADS // 2
    dmask, qdec, kdec, g_state, block_diag = tables

    dmask_t = jnp.swapaxes(dmask, 1, 2)

    def body(q_ref, k_ref, qdb_ref, kdb_ref, v_ref, rg_ref, o_ref, rp_ref, dy_ref, gain_ref, cos_ref, sin_ref,
             dm_ref, dmt_ref, qd_ref, kd_ref, gs_ref, bd_ref, dq_ref, dk_ref, dv_ref, drg_ref, dgain_ref, dstate):
        n = pl.program_id(0)

        @pl.when(n == 0)
        def _():
            dstate[...] = jnp.zeros_like(dstate)
            dgain_ref[...] = jnp.zeros_like(dgain_ref)

        lane = lax.broadcasted_iota(jnp.int32, (CHUNK, LANES), 1)
        for ci in reversed(range(cps)):
            rs = slice(ci * CHUNK, (ci + 1) * CHUNK)
            dq_parts, dk_parts = [], []
            for p in range(n_pairs):
                qs = slice(p * LANES, (p + 1) * LANES)
                vs = slice(p * 2 * LANES, (p + 1) * 2 * LANES)
                do_parts = []
                for e in range(2):
                    hd = 2 * p + e
                    hs = slice(hd * LANES, (hd + 1) * LANES)
                    xhat, rstd = _head_norm(o_ref[rs, hs])
                    g = rg_ref[rs, hs]
                    sg = _sigmoid(g)
                    dyh = dy_ref[rs, hs]
                    gn = gain_ref[:, hs]
                    d_on = dyh * (g * sg)
                    drg_ref[rs, hs] = (dyh * (xhat * gn) * (sg * (1.0 + g * (1.0 - sg)))).astype(BF16)
                    dgain_ref[:, hs] += jnp.sum(d_on * xhat, axis=0, keepdims=True)
                    dxh = d_on * gn
                    do_parts.append(rstd * (dxh - jnp.mean(dxh, axis=-1, keepdims=True)
                                            - xhat * jnp.mean(dxh * xhat, axis=-1, keepdims=True)))
                do_b = jnp.concatenate(do_parts, axis=1).astype(BF16)
                qp, kb = q_ref[rs, qs], k_ref[rs, qs]
                vb = v_ref[rs, vs]
                qd, kd = qdb_ref[rs, qs], kdb_ref[rs, qs]
                dst = dstate[p]
                dst_b = dst.astype(BF16)
                dqp = _dot_nt(do_b, rp_ref[ci, p]) * qd_ref[:, qs]
                dkp = _dot_nt(vb, dst_b) * kd_ref[:, qs]
                dvp = _dot(kd, dst_b)
                dv_parts = []
                for e in range(2):
                    hd = 2 * p + e
                    es = slice(e * LANES, (e + 1) * LANES)
                    mine = (lane // QK_DIM) == e
                    qm = jnp.where(mine, qp, jnp.zeros_like(qp))
                    km = jnp.where(mine, kb, jnp.zeros_like(kb))
                    ds = (_dot_nt(do_b[:, es], vb[:, es]) * dm_ref[hd]).astype(BF16)
                    s_t = (_dot_nt(kb, qm) * dmt_ref[hd]).astype(BF16)
                    ds_t = (_dot_nt(vb[:, es], do_b[:, es]) * dmt_ref[hd]).astype(BF16)
                    dv_parts.append(dvp[:, es] + _dot(s_t, do_b[:, es]))
                    dqp = dqp + _dot(ds, km)
                    dkp = dkp + _dot(ds_t, qm)
                dv_ref[rs, vs] = jnp.concatenate(dv_parts, axis=1).astype(BF16)
                dstate[p] = gs_ref[p] * dst + bd_ref[...] * _dot_tn(qd, do_b)
                dq_parts.append(dqp)
                dk_parts.append(dkp)
            cos = _tile_lanes(cos_ref[rs, :], d_qk // LANES)
            sin = _tile_lanes(sin_ref[rs, :], d_qk // LANES)
            dq = jnp.concatenate(dq_parts, axis=1)
            dk = jnp.concatenate(dk_parts, axis=1) * (QK_DIM ** -0.5)
            dq_ref[rs, :] = (dq * cos + _rot_partner(dq * sin)).astype(BF16)
            dk_ref[rs, :] = (dk * cos + _rot_partner(dk * sin)).astype(BF16)

    last = n_ch // cps - 1
    ch = lambda w: pl.BlockSpec((tm, w), lambda n: (last - n, 0))
    const2 = lambda a: pl.BlockSpec(a.shape, lambda n: (0, 0))
    const3 = lambda a: pl.BlockSpec(a.shape, lambda n: (0, 0, 0))
    return _hosted_call(
        body, ride, n_ch // cps,
        name="ret_bwd", barrier_id=1,
        in_specs=[ch(d_qk)] * 4 + [ch(d_ret), ch(d_ret), ch(d_ret),
                  pl.BlockSpec((cps, n_pairs, 2 * QK_DIM, 2 * LANES), lambda n: (last - n, 0, 0, 0)),
                  ch(d_ret), const2(gain), ch(LANES), ch(LANES),
                  const3(dmask), const3(dmask_t), const2(qdec), const2(kdec), const3(g_state), const2(block_diag)],
        out_specs=[ch(d_qk), ch(d_qk), ch(d_ret), ch(d_ret), pl.BlockSpec((1, d_ret), lambda n: (0, 0))],
        out_shape=[jax.ShapeDtypeStruct((tp, d_qk), BF16), jax.ShapeDtypeStruct((tp, d_qk), BF16),
                   jax.ShapeDtypeStruct((tp, d_ret), BF16), jax.ShapeDtypeStruct((tp, d_ret), BF16),
                   jax.ShapeDtypeStruct((1, d_ret), F32)],
        scratch_shapes=[pltpu.VMEM((n_pairs, 2 * QK_DIM, 2 * LANES), F32)],
        args=(qb, kb, qd, kd, vb, rg, o, rprev, dy, gain, cos_t, sin_t, dmask, dmask_t, qdec, kdec, g_state,
              block_diag),
    )


def _lru_bwd(lx, lg, hl, dy, saved, cw, wr, wi, lam, dgain, dgf, tm, ride=None):
    tp, w = lx.shape
    nt = tp // tm
    per8 = tm // SUBLANES
    n_heads = wr.shape[0]
    vec_row = {name: ROW_VEC + VEC_NAMES.index(name) for name in VEC_NAMES}

    def body(lx_ref, lg_ref, hl_ref, hlp_ref, dy_ref, xc_ref, r_ref, ig_ref, a_ref, beta_ref, w4_ref,
             cw_ref, wr_ref, wi_ref, lam_ref, dgain_ref, dgf_ref,
             dlx_ref, dlg_ref, pk_ref,
             g_s, b_s, carry, dxc_next, a_next):
        i = pl.program_id(0)
        first_tile = i == nt - 1
        heads = [slice(hd * LANES, (hd + 1) * LANES) for hd in range(n_heads)]

        def add_row(hd, row, value):
            pk_ref[hd, row:row + 1, :] += value

        @pl.when(i == 0)
        def _():
            carry[...] = jnp.zeros_like(carry)
            dxc_next[...] = jnp.zeros_like(dxc_next)
            a_next[...] = jnp.zeros_like(a_next)
            pk_ref[...] = jnp.zeros_like(pk_ref)
            for hd, hs in enumerate(heads):
                add_row(hd, vec_row["ret_norm_gain"], dgain_ref[:, hs])
                add_row(hd, vec_row["final_norm_gain"], dgf_ref[:, hs])

        for hd, hs in enumerate(heads):
            g = lg_ref[:, hs]
            sg = _sigmoid(g)
            dyv = dy_ref[:, hs]
            dlg_ref[:, hs] = (dyv * hl_ref[:, hs] * (sg * (1.0 + g * (1.0 - sg)))).astype(BF16)
            g_s[hd] = dyv * (g * sg)
            b_s[hd] = _shift_up(a_ref[:, hs], a_next[:, hs], 1)
        carry[0:1, :] = _segment_scan(b_s, g_s, g_s, b_s, carry[0:1, :], tm, reverse=True)
        a_next[...] = a_ref[0:SUBLANES, :]
        row = lax.broadcasted_iota(jnp.int32, (tm, 1), 0) + (nt - 1 - i) * tm
        lam_v = lam_ref[...]
        dlam_scale = LRU_C * _sigmoid(-lam_v)
        dr_scale = -LRU_C * _softplus_neg(lam_v)
        for hd, hs in enumerate(heads):
            a, beta, r, ig, xc = a_ref[:, hs], beta_ref[:, hs], r_ref[:, hs], ig_ref[:, hs], xc_ref[:, hs]
            dh = g_s[hd]
            hprev = _shift_down(hl_ref[:, hs], jnp.where(first_tile, 0.0, hlp_ref[:, hs]), 1)
            du = jnp.where(row >= PAD_ROWS, dh, 0.0)
            dbeta = du * ig * xc
            d_ig = du * beta * xc
            dxc = du * beta * ig
            dloga = (dh * hprev) * a - dbeta * w4_ref[:, hs]
            add_row(hd, vec_row["lru_lambda"], jnp.sum(dloga * r, axis=0, keepdims=True) * dlam_scale[:, hs])
            dpr = (dloga * dr_scale[:, hs]) * r * (1.0 - r)
            dpi = d_ig * ig * (1.0 - ig)
            add_row(hd, vec_row["b_rg"], jnp.sum(dpr, axis=0, keepdims=True))
            add_row(hd, vec_row["b_ig"], jnp.sum(dpi, axis=0, keepdims=True))
            xh, dprh, dpih = xc.astype(BF16), dpr.astype(BF16), dpi.astype(BF16)
            pk_ref[hd, ROW_WR:ROW_WR + LANES, :] += _dot_tn(xh, dprh)
            pk_ref[hd, ROW_WI:ROW_WI + LANES, :] += _dot_tn(xh, dpih)
            dxc = dxc + _dot_nt(dprh, wr_ref[hd].astype(BF16)) + _dot_nt(dpih, wi_ref[hd].astype(BF16))
            nxt = dxc_next[:, hs]
            up1, up2, up3 = _shift_up(dxc, nxt, 1), _shift_up(dxc, nxt, 2), _shift_up(dxc, nxt, 3)
            dlx = dxc * cw_ref[3:4, hs]
            dlx = dlx + up1 * cw_ref[2:3, hs]
            dlx = dlx + up2 * cw_ref[1:2, hs]
            dlx = dlx + up3 * cw_ref[0:1, hs]
            dlx_ref[:, hs] = dlx.astype(BF16)
            dxc_next[:, hs] = dxc[0:SUBLANES]
            lxv = lx_ref[:, hs]
            add_row(hd, vec_row["conv_b"], jnp.sum(dxc, axis=0, keepdims=True))
            for kk, shifted in enumerate((up3, up2, up1, dxc)):
                add_row(hd, ROW_CONV + kk, jnp.sum(shifted * lxv, axis=0, keepdims=True))

    last = nt - 1
    tile = pl.BlockSpec((tm, w), lambda i: (last - i, 0))
    prev = pl.BlockSpec((SUBLANES, w), lambda i: (jnp.maximum((last - i) * per8 - 1, 0), 0))
    vec = pl.BlockSpec((1, w), lambda i: (0, 0))
    mat = pl.BlockSpec(wr.shape, lambda i: (0, 0, 0))
    cwb = pl.BlockSpec(cw.shape, lambda i: (0, 0))
    packed = (n_heads, UNIT_ROWS, LANES)
    return _hosted_call(
        body, ride, nt,
        name="lru_bwd", barrier_id=2,
        in_specs=[tile, tile, tile, prev, tile] + [tile] * 6 + [cwb, mat, mat, vec, vec, vec],
        out_specs=[tile, tile, pl.BlockSpec(packed, lambda i: (0, 0, 0))],
        out_shape=[jax.ShapeDtypeStruct((tp, w), BF16), jax.ShapeDtypeStruct((tp, w), BF16),
                   jax.ShapeDtypeStruct(packed, F32)],
        scratch_shapes=[pltpu.VMEM((w // LANES, tm, LANES), F32)] * 2 + [pltpu.VMEM((SUBLANES, w), F32)] * 3,
        args=(lx, lg, hl, hl, dy, *saved, cw, wr, wi, lam, dgain, dgf),
    )


def _in_proj_dw(dparts, hp, gain, wg_shape, ride=None):
    tp, d = hp.shape
    n_ch = tp // CHUNK
    per = next(p for p in (4, 2, 5, 3, 1) if (n_ch - 1) % p == 0)
    n_steps = 1 + (n_ch - 1) // per
    widths = [p.shape[1] for p in dparts]
    segs = _proj_segments(widths[0], widths[2], widths[4], wg_shape[2])

    def body(*refs):
        dp = [refs[p * per:(p + 1) * per] for p in range(6)]
        hp_b = refs[6 * per:7 * per]
        g_ref, dwg_ref, acc, sem = refs[7 * per:]
        i = pl.program_id(0)

        def accumulate(blocks):
            h = jnp.concatenate([hp_b[b][...] for b in blocks], axis=0)
            rinv = lax.rsqrt(jnp.mean(h * h, axis=-1, keepdims=True) + EPS)
            u = ((h * rinv) * g_ref[...]).astype(BF16)
            for p_refs, parts in zip(dp, segs):
                for jj, inner, off, take in parts:
                    seg = jnp.concatenate([p_refs[b][:, off:off + take] for b in blocks], axis=0)
                    acc[jj, :, inner:inner + take] += _dot_tn(u, seg)

        @pl.when(i == 0)
        def _():
            acc[...] = jnp.zeros_like(acc)
            accumulate([0])

        @pl.when(i > 0)
        def _():
            accumulate(list(range(per)))

        @pl.when(i == n_steps - 1)
        def _():
            cp = pltpu.make_async_copy(acc, dwg_ref, sem)
            cp.start()
            cp.wait()

    def blocks(w):
        return [pl.BlockSpec((CHUNK, w), functools.partial(
            lambda i, b: (jnp.where(i == 0, b, per * (i - 1) + 1 + b), 0), b=b)) for b in range(per)]

    in_specs, args = [], []
    for a, w in list(zip(dparts, widths)) + [(hp, d)]:
        in_specs += blocks(w)
        args += [a] * per
    outs, rides = _hosted_call(
        body, ride, n_steps,
        name="in_proj_dw", barrier_id=3,
        in_specs=in_specs + [pl.BlockSpec(gain.shape, lambda i: (0, 0))],
        out_specs=[ANY_SPEC],
        out_shape=[jax.ShapeDtypeStruct(wg_shape, F32)],
        scratch_shapes=[pltpu.VMEM(wg_shape, F32), pltpu.SemaphoreType.DMA],
        args=(*args, gain),
    )
    return outs[0], rides


def _in_proj_dx(dparts, hp, dh2, gain, wg, s_len, tm, ride=None):
    tp, d = hp.shape
    nt = tp // tm
    widths = [p.shape[1] for p in dparts]
    segs = _proj_segments(widths[0], widths[2], widths[4], wg.shape[2])

    def body(*refs):
        dp = refs[:6]
        hp_ref, dh2_ref, g_ref, w_ref = refs[6:10]
        gx_ref, dmeta_ref, dg_ref = refs[10:13]
        stage, sems = refs[13:]
        i = pl.program_id(0)

        @pl.when(i == 0)
        def _():
            dg_ref[...] = jnp.zeros_like(dg_ref)

        h = hp_ref[...]
        rinv = lax.rsqrt(jnp.mean(h * h, axis=-1, keepdims=True) + EPS)
        nrm = h * rinv
        gv = g_ref[...]
        du = jnp.zeros((tm, d), F32)
        for p_ref, parts in zip(dp, segs):
            for jj, inner, off, take in parts:
                du = du + _dot_nt(p_ref[:, off:off + take], w_ref[jj, :, inner:inner + take])
        dg_ref[...] += jnp.sum(du * nrm, axis=0, keepdims=True)
        dn = du * gv
        dh = dh2_ref[...] + rinv * (dn - nrm * jnp.mean(dn * nrm, axis=-1, keepdims=True))

        def first_copy():
            return pltpu.make_async_copy(stage.at[0, pl.ds(CHUNK, tm - CHUNK), :],
                                         gx_ref.at[pl.ds(0, tm - CHUNK), :], sems.at[0])

        def tile_copy(slot, start):
            return pltpu.make_async_copy(stage.at[slot], gx_ref.at[pl.ds(start, tm), :], sems.at[slot])

        @pl.when(i == 0)
        def _():
            dmeta_ref[...] = dh[PAD_ROWS:CHUNK]
            stage[0] = dh
            first_copy().start()

        @pl.when(i > 0)
        def _():
            slot = 1 + i % 2

            @pl.when(i >= 3)
            def _():
                tile_copy(slot, 0).wait()

            stage[slot] = dh
            tile_copy(slot, pl.multiple_of(i * tm - CHUNK, CHUNK)).start()

        @pl.when(i == nt - 1)
        def _():
            first_copy().wait()
            for step in (nt - 2, nt - 1):
                if step >= 1:
                    tile_copy(1 + step % 2, 0).wait()

    tile = lambda w: pl.BlockSpec((tm, w), lambda i: (i, 0))
    return _hosted_call(
        body, ride, nt,
        name="in_proj_dx", barrier_id=4,
        in_specs=[tile(w) for w in widths] + [tile(d), tile(d), pl.BlockSpec(gain.shape, lambda i: (0, 0)),
                                              pl.BlockSpec(wg.shape, lambda i: (0, 0, 0))],
        out_specs=[ANY_SPEC, pl.BlockSpec((N_META, d), lambda i: (0, 0)), pl.BlockSpec((1, d), lambda i: (0, 0))],
        out_shape=[jax.ShapeDtypeStruct((s_len, d), F32), jax.ShapeDtypeStruct((N_META, d), F32),
                   jax.ShapeDtypeStruct((1, d), F32)],
        scratch_shapes=[pltpu.VMEM((3, tm, d), F32), pltpu.SemaphoreType.DMA((3,))],
        args=(*dparts, hp, dh2, gain, wg),
    )


def _pair_sum(buf, recv, c_arr, tr, name):
    _, rows, cols = buf.shape

    def body(c_ref, mine_ref, got_ref, out_ref):
        out_ref[...] = (mine_ref[...] + got_ref[...]).astype(BF16)

    grid_spec = pltpu.PrefetchScalarGridSpec(
        num_scalar_prefetch=1,
        grid=(N_CHIPS, rows // tr),
        in_specs=[pl.BlockSpec((1, tr, cols), lambda jj, r, c_ref: (2 * jj + c_ref[0], r, 0)),
                  pl.BlockSpec((1, tr, cols), lambda jj, r, c_ref: (jj, r, 0))],
        out_specs=pl.BlockSpec((1, tr, cols), lambda jj, r, c_ref: (jj, r, 0)),
    )
    return pl.pallas_call(
        body,
        name=name,
        grid_spec=grid_spec,
        out_shape=jax.ShapeDtypeStruct((N_CHIPS, rows, cols), BF16),
    )(c_arr, buf, recv)


def _pair_exchange_sum(buf, c_arr, tr, name):
    _, rows, cols = buf.shape
    per = rows // tr

    def body(c_ref, src_ref, mine_ref, out_ref, got, send_sems, recv_sems):
        jj, r = pl.program_id(0), pl.program_id(1)
        x, y, c, _ = _position()
        copies = [_remote(src_ref.at[2 * k + 1 - c], got.at[k], send_sems, recv_sems, k, (x, y, 1 - c))
                  for k in range(N_CHIPS)]

        @pl.when((jj == 0) & (r == 0))
        def _():
            barrier = pltpu.get_barrier_semaphore()
            pl.semaphore_signal(barrier, inc=1, device_id=(x, y, 1 - c), device_id_type=MESH)
            pl.semaphore_wait(barrier, 1)
            for cp in copies:
                cp.start()

        for k in range(N_CHIPS):
            @pl.when((jj == k) & (r == 0))
            def _():
                copies[k].wait_recv()

        rows_r = pl.ds(pl.multiple_of(r * tr, tr), tr)
        out_ref[0] = (mine_ref[0] + got[jj, rows_r, :]).astype(BF16)

        @pl.when((jj == N_CHIPS - 1) & (r == per - 1))
        def _():
            for cp in copies:
                cp.wait_send()

    grid_spec = pltpu.PrefetchScalarGridSpec(
        num_scalar_prefetch=1,
        grid=(N_CHIPS, per),
        in_specs=[ANY_SPEC, pl.BlockSpec((1, tr, cols), lambda jj, r, c_ref: (2 * jj + c_ref[0], r, 0))],
        out_specs=pl.BlockSpec((1, tr, cols), lambda jj, r, c_ref: (jj, r, 0)),
        scratch_shapes=[pltpu.VMEM((N_CHIPS, rows, cols), F32), pltpu.SemaphoreType.DMA((N_CHIPS,)),
                        pltpu.SemaphoreType.DMA((N_CHIPS,))],
    )
    return pl.pallas_call(
        body,
        name=name,
        grid_spec=grid_spec,
        out_shape=jax.ShapeDtypeStruct((N_CHIPS, rows, cols), BF16),
        compiler_params=pltpu.CompilerParams(dimension_semantics=("arbitrary", "arbitrary"),
                                             vmem_limit_bytes=VMEM_LIMIT, collective_id=5),
    )(c_arr, buf, buf)


def _chip_sum(mine, got, j_arr, tr, name, loss_part=None):
    _, rows, cols = got.shape
    extra = [] if loss_part is None else [loss_part]

    def body(j_ref, mine_ref, got_ref, *rest):
        out_ref = rest[-1]
        j = j_ref[0]
        acc = None
        for jj in range(N_CHIPS):
            term = jnp.where(j == jj, mine_ref[0], got_ref[jj]).astype(F32)
            acc = term if acc is None else acc + term
        out_ref[...] = acc
        if loss_part is not None:
            out_ref[ROW_LOSS:ROW_LOSS + 1, :] = rest[0][0:1, :]

    grid_spec = pltpu.PrefetchScalarGridSpec(
        num_scalar_prefetch=1,
        grid=(rows // tr,),
        in_specs=[pl.BlockSpec((1, tr, cols), lambda r, j_ref: (j_ref[0], r, 0)),
                  pl.BlockSpec((N_CHIPS, tr, cols), lambda r, j_ref: (0, r, 0))] +
                 [pl.BlockSpec(e.shape, lambda r, j_ref: (0, 0)) for e in extra],
        out_specs=pl.BlockSpec((tr, cols), lambda r, j_ref: (r, 0)),
    )
    return pl.pallas_call(
        body,
        name=name,
        grid_spec=grid_spec,
        out_shape=jax.ShapeDtypeStruct((rows, cols), F32),
    )(j_arr, mine, got, *extra)


def _finish_exchange(f_in, f_small):
    def body(fin_ref, fs_ref, rin_ref, os_ref, send_sems, recv_sems, local_sem):
        x, y, c, chips = _position()
        j = 2 * x + y
        me = 2 * j + c
        sibling = (x, y, 1 - c)
        local = pltpu.make_async_copy(fs_ref, os_ref.at[me], local_sem)
        local.start()

        def copy(k, src, dst, to):
            return _remote(src, dst, send_sems, recv_sems, k, to)

        first = [copy(0, fin_ref, rin_ref, sibling), copy(1, fs_ref, os_ref.at[me], sibling)]
        first += [copy(2 + k, fs_ref, os_ref.at[me], (cx, cy, c)) for k, (cx, cy) in enumerate(chips)]
        for cp in first:
            cp.start()
        passed = []
        for k, (cx, cy) in enumerate(chips):
            unit = 2 * (2 * cx + cy) + c
            copy(2 + k, fs_ref, os_ref.at[unit], sibling).wait_recv()
            fwd = copy(5 + k, os_ref.at[unit], os_ref.at[unit], sibling)
            fwd.start()
            passed.append(fwd)
        copy(0, fin_ref, rin_ref, sibling).wait_recv()
        copy(1, fs_ref, os_ref.at[2 * j + 1 - c], sibling).wait_recv()
        for k, (cx, cy) in enumerate(chips):
            unit = 2 * (2 * cx + cy) + 1 - c
            copy(5 + k, fs_ref, os_ref.at[unit], sibling).wait_recv()
        for cp in first + passed:
            cp.wait_send()
        local.wait()

    return pl.pallas_call(
        body,
        name="grad_finish_exchange",
        in_specs=[ANY_SPEC] * 2,
        out_specs=[ANY_SPEC] * 2,
        out_shape=[jax.ShapeDtypeStruct(f_in.shape, F32), jax.ShapeDtypeStruct((N_DEV,) + f_small.shape, F32)],
        scratch_shapes=[pltpu.SemaphoreType.DMA((8,)), pltpu.SemaphoreType.DMA((8,)), pltpu.SemaphoreType.DMA],
    )(f_in, f_small)


def _adamw_math(w, g, m, v):
    m = ADAM_B1 * m + (1.0 - ADAM_B1) * g
    v = ADAM_B2 * v + (1.0 - ADAM_B2) * (g * g)
    m_hat = m / (1.0 - ADAM_B1 ** ADAM_STEP)
    v_hat = v / (1.0 - ADAM_B2 ** ADAM_STEP)
    delta = -ADAM_LR * (m_hat / (jnp.sqrt(v_hat) + ADAM_EPS) + ADAM_WD * w)
    return delta, m, v


def _adamw_big(w, g_mine, g_sib, m, v, c_arr, tr, name):
    rows, cols = w.shape
    half = rows // 2
    per = half // tr

    def body(c_ref, w_ref, gm_ref, gs_ref, m_ref, v_ref, g_ref, d_ref, mo_ref, vo_ref):
        g = jnp.where(pl.program_id(0) == c_ref[0], gm_ref[...], gs_ref[...])
        g_ref[...] = g
        d_ref[...], mo_ref[...], vo_ref[...] = _adamw_math(w_ref[...], g, m_ref[...], v_ref[...])

    full = pl.BlockSpec((tr, cols), lambda h, r, c_ref: (h * per + r, 0))
    unit = pl.BlockSpec((tr, cols), lambda h, r, c_ref: (r, 0))
    grid_spec = pltpu.PrefetchScalarGridSpec(
        num_scalar_prefetch=1,
        grid=(2, per),
        in_specs=[full, unit, unit, full, full],
        out_specs=[full] * 4,
    )
    return pl.pallas_call(
        body,
        name=name,
        grid_spec=grid_spec,
        out_shape=[jax.ShapeDtypeStruct(w.shape, F32)] * 4,
    )(c_arr, w, g_mine, g_sib, m, v)


def _adamw_small(j_arr, packed, params):
    names = list(params)
    n = len(names)

    def body(j_ref, pk_ref, *refs):
        ins = refs[:3 * n]
        outs = refs[3 * n:]
        j = j_ref[0]

        def shard(row, rows):
            return jnp.concatenate([pk_ref[2 * j, row:row + rows, :], pk_ref[2 * j + 1, row:row + rows, :]], axis=1)

        def tail_sum(unit, row, rows):
            start = pl.multiple_of(UNIT_ROWS + TAIL_ROWS * unit + row, SUBLANES)
            total = pk_ref[0, pl.ds(start, rows), :]
            for dev in range(1, N_DEV):
                total = total + pk_ref[dev, pl.ds(start, rows), :]
            return total

        for idx, name in enumerate(names):
            if name == "w_rg":
                g = pk_ref[:, ROW_WR:ROW_WR + LANES, :]
            elif name == "w_ig":
                g = pk_ref[:, ROW_WI:ROW_WI + LANES, :]
            elif name == "meta_tokens":
                g = jnp.concatenate([tail_sum(2 * j, 0, N_META), tail_sum(2 * j + 1, 0, N_META)], axis=1)
            elif name == "norm_gain":
                g = jnp.concatenate([tail_sum(u, N_META, SUBLANES)[0:1] for u in range(N_DEV)], axis=1)
            elif name == "conv_w":
                g = shard(ROW_CONV, 4)
            else:
                row = ROW_VEC + VEC_NAMES.index(name)
                g = jnp.concatenate([pk_ref[u, row:row + 1, :] for u in range(N_DEV)], axis=1)
            w_ref, m_ref, v_ref = ins[3 * idx:3 * idx + 3]
            delta, m, v = _adamw_math(w_ref[...], g, m_ref[...], v_ref[...])
            g_ref, d_ref, mo_ref, vo_ref = outs[4 * idx:4 * idx + 4]
            g_ref[...], d_ref[...], mo_ref[...], vo_ref[...] = g, delta, m, v
        total = pk_ref[0, ROW_LOSS:ROW_LOSS + 1, :]
        for u in range(1, N_DEV):
            total = total + pk_ref[u, ROW_LOSS:ROW_LOSS + 1, :]
        outs[4 * n][...] = jnp.broadcast_to(total, (SUBLANES, LANES))

    flat_in, out_shape = [], []
    for name in names:
        w, m, v = params[name]
        flat_in += [w, m, v]
        out_shape += [jax.ShapeDtypeStruct(w.shape, F32)] * 4
    out_shape.append(jax.ShapeDtypeStruct((SUBLANES, LANES), F32))
    res = pl.pallas_call(
        body,
        name="adamw_small",
        in_specs=[SMEM_SPEC, VMEM_SPEC] + [VMEM_SPEC] * (3 * n),
        out_specs=[VMEM_SPEC] * (4 * n + 1),
        out_shape=out_shape,
    )(j_arr, packed, *flat_in)
    return {name: tuple(res[4 * idx:4 * idx + 4]) for idx, name in enumerate(names)}, res[4 * n][0, 0]


def _units(a):
    rows = a.shape[0]
    return jnp.transpose(a.reshape(rows, N_DEV, LANES), (1, 0, 2))


def kernel(x, meta_tokens, norm_gain, w_in, conv_w, conv_b, w_rg, b_rg, w_ig, b_ig, lru_lambda, ret_norm_gain, w_out, final_norm_gain, loss_target, m_meta_tokens, m_norm_gain, m_w_in, m_conv_w, m_conv_b, m_w_rg, m_b_rg, m_w_ig, m_b_ig, m_lru_lambda, m_ret_norm_gain, m_w_out, m_final_norm_gain, v_meta_tokens, v_norm_gain, v_w_in, v_conv_w, v_conv_b, v_w_rg, v_b_rg, v_w_ig, v_b_ig, v_lru_lambda, v_ret_norm_gain, v_w_out, v_final_norm_gain):
    s_len, d = x.shape[1], x.shape[2]
    d_lru = w_rg.shape[1] * w_rg.shape[2]
    d_ret = ret_norm_gain.shape[1]
    d_qk = HEADS * QK_DIM
    tp = s_len + CHUNK
    tm = TOKEN_TILE
    assert tp % tm == 0 and d_lru == HEADS * LANES and d_ret == HEADS * LANES
    ax, ay, ac = lax.axis_index("x"), lax.axis_index("y"), lax.axis_index("c")
    c_arr = jnp.reshape(ac, (1,)).astype(jnp.int32)
    j_arr = jnp.reshape(2 * ax + ay, (1,)).astype(jnp.int32)

    small = jnp.concatenate([meta_tokens, conv_w[0], jnp.zeros((4, meta_tokens.shape[1]), F32)], axis=0)
    wg, sg = _gather_weights(w_in[0], small)
    cols = sg.shape[2]
    meta_full = jnp.transpose(sg[:, :N_META, :], (1, 0, 2)).reshape(N_META, N_CHIPS * cols)
    cw_full = jnp.transpose(sg[:, N_META:N_META + 4, :], (1, 0, 2)).reshape(4, N_CHIPS * cols)
    cw8 = jnp.concatenate([cw_full, jnp.zeros((4, cw_full.shape[1]), F32)], axis=0)

    half = QK_DIM // 2
    inv = ROPE_BASE ** (-jnp.arange(half, dtype=F32) / half)
    pos = (jnp.arange(tp) - PAD_ROWS).astype(F32)
    ang = pos[:, None] * inv[None, :]
    cos_t = jnp.tile(jnp.cos(ang), (1, LANES // half))
    sign = jnp.where((jnp.arange(LANES) % QK_DIM) < half, -1.0, 1.0).astype(F32)
    sin_t = jnp.tile(jnp.sin(ang), (1, LANES // half)) * sign[None, :]
    tables = _ret_tables()
    gain_f = final_norm_gain.reshape(1, d)

    hp, lx, lg, *qkv, rg, wo4 = _in_proj(x[0], meta_full, norm_gain, wg, cos_t, sin_t, w_out[0], tables[1], tables[2],
                                         tm, d_lru, d_qk, d_ret)
    wo = wo4.reshape(N_CHIPS * wo4.shape[1], wo4.shape[2])
    hl, y_lru, *lru_saved = _lru_fwd(lx, lg, cw8, conv_b, w_rg[0], b_rg, w_ig[0], b_ig, lru_lambda, tm)
    o, y_ret, rprev = _ret_fwd(*qkv, rg, ret_norm_gain, tables, tm)
    dh2, dy_lru, dy_ret, dwo, dgf, loss_acc = _out_proj_loss(y_lru, y_ret, hp, loss_target[0], wo, gain_f, tm)

    g_out = dwo.reshape(N_DEV, dwo.shape[0] // N_DEV, dwo.shape[1])
    (dq, dk, dv, drg, dgain), (r_out,) = _ret_bwd(*qkv, rg, o, rprev, dy_ret, ret_norm_gain, cos_t, sin_t, tables,
                                                 tm, ride=_pair_ride([g_out]))
    q_out = _pair_sum(g_out, r_out, c_arr, 128, "grad_pair_sum_out")
    (dlx, dlg, g_small), (e_out,) = _lru_bwd(lx, lg, hl, dy_lru, lru_saved, cw8, w_rg[0], w_ig[0], lru_lambda,
                                            dgain, dgf, tm, ride=_chip_ride([q_out]))
    f_out = _chip_sum(q_out, e_out, j_arr, 128, "grad_chip_sum_out")
    dparts = [dlx, dlg, dq, dk, dv, drg]
    dwg, (s_out, r_small) = _in_proj_dw(dparts, hp, norm_gain, wg.shape,
                                        ride=_join_rides(_sibling_ride([f_out]), _pair_ride([g_small])))
    g_in = dwg.reshape(N_DEV, dwg.shape[1] // 2, dwg.shape[2])
    q_in = _pair_exchange_sum(g_in, c_arr, 128, "grad_pair_exchange_sum_in")
    q_small = _pair_sum(g_small, r_small, c_arr, UNIT_ROWS, "grad_pair_sum_small")
    (grad_x, dmeta, dg1), (e_in, e_small) = _in_proj_dx(dparts, hp, dh2, norm_gain, wg, s_len, tm,
                                                        ride=_chip_ride([q_in, q_small]))
    f_in = _chip_sum(q_in, e_in, j_arr, 128, "grad_chip_sum_in")
    f_small = _chip_sum(q_small, e_small, j_arr, UNIT_ROWS, "grad_chip_sum_small", loss_part=loss_acc)
    tail = jnp.concatenate([_units(dmeta), _units(dg1), jnp.zeros((N_DEV, TAIL_ROWS - N_META - 1, LANES), F32)],
                           axis=1).reshape(N_DEV * TAIL_ROWS, LANES)
    s_in, o_small = _finish_exchange(f_in, jnp.concatenate([f_small, tail], axis=0))

    res_in = _adamw_big(w_in[0], f_in, s_in, m_w_in[0], v_w_in[0], c_arr, 256, "adamw_w_in")
    res_out = _adamw_big(w_out[0], f_out, s_out, m_w_out[0], v_w_out[0], c_arr, 256, "adamw_w_out")
    small_params = {
        "meta_tokens": (meta_tokens, m_meta_tokens, v_meta_tokens),
        "norm_gain": (norm_gain, m_norm_gain, v_norm_gain),
        "conv_w": (conv_w[0], m_conv_w[0], v_conv_w[0]),
        "conv_b": (conv_b, m_conv_b, v_conv_b),
        "w_rg": (w_rg[0], m_w_rg[0], v_w_rg[0]),
        "b_rg": (b_rg, m_b_rg, v_b_rg),
        "w_ig": (w_ig[0], m_w_ig[0], v_w_ig[0]),
        "b_ig": (b_ig, m_b_ig, v_b_ig),
        "lru_lambda": (lru_lambda, m_lru_lambda, v_lru_lambda),
        "ret_norm_gain": (ret_norm_gain, m_ret_norm_gain, v_ret_norm_gain),
        "final_norm_gain": (gain_f, m_final_norm_gain.reshape(1, d), v_final_norm_gain.reshape(1, d)),
    }
    res, loss = _adamw_small(j_arr, o_small, small_params)
    res["w_in"] = tuple(res_in)
    res["w_out"] = tuple(res_out)

    order = ["meta_tokens", "norm_gain", "w_in", "conv_w", "conv_b", "w_rg", "b_rg", "w_ig", "b_ig", "lru_lambda",
             "ret_norm_gain", "w_out", "final_norm_gain"]
    shapes = {"w_in": w_in.shape, "conv_w": conv_w.shape, "w_rg": w_rg.shape, "w_ig": w_ig.shape,
              "w_out": w_out.shape, "final_norm_gain": final_norm_gain.shape}
    outs = [loss, grad_x.reshape(x.shape)]
    for kind in range(4):
        for name in order:
            a = res[name][kind]
            outs.append(a.reshape(shapes[name]) if name in shapes else a)
    return tuple(outs)
```

```python
import functools

import jax
import jax.numpy as jnp
from jax import lax
from jax.experimental import pallas as pl
from jax.experimental.pallas import tpu as pltpu

F32 = jnp.float32
BF16 = jnp.bfloat16

N_META = 16
CHUNK = 128
PAD_ROWS = CHUNK - N_META
HEADS = 8
QK_DIM = 64
LANES = 128
SUBLANES = 8
LRU_C = 8.0
EPS = 1e-6
ROPE_BASE = 10000.0
ADAM_LR = 0.001
ADAM_B1 = 0.9
ADAM_B2 = 0.999
ADAM_EPS = 1e-08
ADAM_WD = 0.01
ADAM_STEP = 10
N_CHIPS = 4
N_DEV = 8
TOKEN_TILE = 384
VMEM_LIMIT = 58 * 1024 * 1024
MESH = pl.DeviceIdType.MESH

VMEM_SPEC = pl.BlockSpec(memory_space=pltpu.VMEM)
SMEM_SPEC = pl.BlockSpec(memory_space=pltpu.SMEM)
ANY_SPEC = pl.BlockSpec(memory_space=pl.ANY)

ROW_WR, ROW_WI, ROW_META, ROW_CONV, ROW_VEC, UNIT_ROWS = 0, 128, 256, 272, 276, 288
N_VEC = 7
VEC_NAMES = ["norm_gain", "conv_b", "b_rg", "b_ig", "lru_lambda", "ret_norm_gain", "final_norm_gain"]
ROW_LOSS = ROW_VEC + N_VEC
TAIL_ROWS = 24


def _dot(a, b):
    return jnp.dot(a, b, preferred_element_type=F32)


def _dot_nt(a, b):
    return lax.dot_general(a, b, (((1,), (1,)), ((), ())), preferred_element_type=F32)


def _dot_tn(a, b):
    return lax.dot_general(a, b, (((0,), (0,)), ((), ())), preferred_element_type=F32)


def _sigmoid(x):
    return 0.5 * jnp.tanh(0.5 * x) + 0.5


def _shift_down(x, prev8, s):
    rolled = pltpu.roll(x, s, 0)
    rows = lax.broadcasted_iota(jnp.int32, (SUBLANES, x.shape[1]), 0)
    top = jnp.where(rows < s, pltpu.roll(prev8, s, 0), rolled[0:SUBLANES])
    return jnp.concatenate([top, rolled[SUBLANES:]], axis=0)


def _shift_up(x, next8, s):
    n = x.shape[0]
    rolled = pltpu.roll(x, n - s, 0)
    rows = lax.broadcasted_iota(jnp.int32, (SUBLANES, x.shape[1]), 0)
    bot = jnp.where(rows >= SUBLANES - s, pltpu.roll(next8, SUBLANES - s, 0), rolled[n - SUBLANES:n])
    return jnp.concatenate([rolled[:n - SUBLANES], bot], axis=0)


def _rot_partner(t):
    w = t.shape[1]
    lane = lax.broadcasted_iota(jnp.int32, t.shape, 1)
    first = (lane % QK_DIM) < (QK_DIM // 2)
    return jnp.where(first, pltpu.roll(t, w - QK_DIM // 2, 1), pltpu.roll(t, QK_DIM // 2, 1))


def _tile_lanes(t, reps):
    return jnp.concatenate([t] * reps, axis=1)


class _Ride:
    def __init__(self, srcs, dst_shapes, n_copies, make, to_sibling=False, to_chips=False):
        self.srcs, self.dst_shapes, self.n_copies, self.make = list(srcs), list(dst_shapes), n_copies, make
        self.to_sibling, self.to_chips = to_sibling, to_chips

    def peers(self):
        x, y, c, chips = _position()
        return ([(x, y, 1 - c)] if self.to_sibling else []) + ([(cx, cy, c) for cx, cy in chips] if self.to_chips else [])


def _join_rides(a, b):
    def make(src, dst, send_sems, recv_sems, base):
        na, da = len(a.srcs), len(a.dst_shapes)
        return (a.make(src[:na], dst[:da], send_sems, recv_sems, base)
                + b.make(src[na:], dst[da:], send_sems, recv_sems, base + a.n_copies))

    return _Ride(a.srcs + b.srcs, a.dst_shapes + b.dst_shapes, a.n_copies + b.n_copies, make,
                 a.to_sibling or b.to_sibling, a.to_chips or b.to_chips)


def _position():
    x, y, c = lax.axis_index("x"), lax.axis_index("y"), lax.axis_index("c")
    return x, y, c, [(1 - x, y), (x, 1 - y), (1 - x, 1 - y)]


def _peer_barrier(peers):
    barrier = pltpu.get_barrier_semaphore()
    for peer in peers:
        pl.semaphore_signal(barrier, inc=1, device_id=peer, device_id_type=MESH)
    pl.semaphore_wait(barrier, len(peers))


def _remote(src, dst, send_sems, recv_sems, k, to):
    return pltpu.make_async_remote_copy(src_ref=src, dst_ref=dst, send_sem=send_sems.at[k], recv_sem=recv_sems.at[k],
                                        device_id=to, device_id_type=MESH)


def _pair_ride(bufs):
    def make(src, dst, send_sems, recv_sems, base):
        x, y, c, _ = _position()
        return [_remote(src[b].at[2 * jj + 1 - c], dst[b].at[jj], send_sems, recv_sems, base + b * N_CHIPS + jj,
                        (x, y, 1 - c)) for b in range(len(bufs)) for jj in range(N_CHIPS)]

    shapes = [jax.ShapeDtypeStruct((N_CHIPS,) + b.shape[1:], b.dtype) for b in bufs]
    return _Ride(bufs, shapes, N_CHIPS * len(bufs), make, to_sibling=True)


def _chip_ride(bufs):
    def make(src, dst, send_sems, recv_sems, base):
        x, y, c, chips = _position()
        return [_remote(src[b].at[2 * cx + cy], dst[b].at[2 * x + y], send_sems, recv_sems, base + b * 3 + k,
                        (cx, cy, c)) for b in range(len(bufs)) for k, (cx, cy) in enumerate(chips)]

    shapes = [jax.ShapeDtypeStruct(b.shape, b.dtype) for b in bufs]
    return _Ride(bufs, shapes, 3 * len(bufs), make, to_chips=True)


def _sibling_ride(bufs):
    def make(src, dst, send_sems, recv_sems, base):
        x, y, c, _ = _position()
        return [_remote(src[b], dst[b], send_sems, recv_sems, base + b, (x, y, 1 - c)) for b in range(len(bufs))]

    shapes = [jax.ShapeDtypeStruct(b.shape, b.dtype) for b in bufs]
    return _Ride(bufs, shapes, len(bufs), make, to_sibling=True)


def _hosted_call(body, ride, n_steps, *, name, in_specs, out_specs, out_shape, scratch_shapes, args, barrier_id=None):
    params = pltpu.CompilerParams(dimension_semantics=("arbitrary",), vmem_limit_bytes=VMEM_LIMIT,
                                  collective_id=barrier_id if ride is not None else None)
    if ride is None:
        res = pl.pallas_call(body, name=name, grid=(n_steps,), in_specs=list(in_specs), out_specs=list(out_specs),
                             out_shape=list(out_shape), scratch_shapes=list(scratch_shapes),
                             compiler_params=params)(*args)
        return list(res), []
    sizes = [len(in_specs), len(ride.srcs), len(out_specs), len(ride.dst_shapes), len(scratch_shapes), 2]

    def hosted(*refs):
        groups, pos = [], 0
        for n in sizes:
            groups.append(refs[pos:pos + n])
            pos += n
        ins, rin, outs, rout, scr, (send_sems, recv_sems) = groups
        i = pl.program_id(0)

        @pl.when(i == 0)
        def _():
            if barrier_id is not None:
                _peer_barrier(ride.peers())
            for cp in ride.make(rin, rout, send_sems, recv_sems, 0):
                cp.start()

        body(*ins, *outs, *scr)

        @pl.when(i == n_steps - 1)
        def _():
            for cp in ride.make(rin, rout, send_sems, recv_sems, 0):
                cp.wait()

    n_out = len(out_specs)
    res = pl.pallas_call(
        hosted,
        name=name,
        grid=(n_steps,),
        in_specs=list(in_specs) + [ANY_SPEC] * len(ride.srcs),
        out_specs=list(out_specs) + [ANY_SPEC] * len(ride.dst_shapes),
        out_shape=list(out_shape) + ride.dst_shapes,
        scratch_shapes=list(scratch_shapes) + [pltpu.SemaphoreType.DMA((ride.n_copies,)),
                                               pltpu.SemaphoreType.DMA((ride.n_copies,))],
        compiler_params=params,
    )(*args, *ride.srcs)
    return list(res[:n_out]), list(res[n_out:])


def _gather_weights(w_in, small):
    r_in, c_in = w_in.shape
    h_in = r_in // 2
    q_in = h_in // 2

    def body(win_ref, small_ref, wg_ref, sg_ref, send_sems, recv_sems):
        x, y, c, chips = _position()
        j = 2 * x + y
        sibling = (x, y, 1 - c)
        xn, yn, dg = chips
        jx, jy, jd = (2 * cx + cy for cx, cy in chips)

        def half(jj, cc):
            return wg_ref.at[jj, pl.ds(cc * h_in, h_in), :]

        def quarter(jj, qq):
            return wg_ref.at[jj, pl.ds(c * h_in + qq * q_in, q_in), :]

        def copy(k, ref, to):
            return _remote(ref, ref, send_sems, recv_sems, k, to)

        def cast_rows(start, rows):
            start = pl.multiple_of(start, q_in)
            wg_ref[j, pl.ds(start, rows), :] = win_ref[pl.ds(start, rows), :].astype(BF16)

        first = [copy(0, quarter(j, 0), (*xn, c)), copy(2, quarter(j, 1), (*yn, c)),
                 copy(1, quarter(j, 1), (*xn, c)), copy(3, quarter(j, 0), (*yn, c))]
        sg_ref[j] = small_ref[...]
        cast_rows(c * h_in, q_in)
        _peer_barrier([sibling] + [(cx, cy, c) for cx, cy in chips])
        first[0].start()
        cast_rows(c * h_in + q_in, q_in)
        for cp in first[1:]:
            cp.start()
        small_copies = [copy(9 + k, sg_ref.at[j], (cx, cy, c)) for k, (cx, cy) in enumerate(chips)]
        for cp in small_copies:
            cp.start()
        first += small_copies
        cast_rows((1 - c) * h_in, h_in)
        copy(0, quarter(jx, 0), sibling).wait_recv()
        along_y = copy(4, quarter(jx, 0), (*yn, c))
        along_y.start()
        copy(2, quarter(jy, 1), sibling).wait_recv()
        along_x = copy(5, quarter(jy, 1), (*xn, c))
        along_x.start()
        copy(1, quarter(jx, 1), sibling).wait_recv()
        to_sib = [copy(6, half(jx, c), sibling)]
        to_sib[-1].start()
        copy(3, quarter(jy, 0), sibling).wait_recv()
        to_sib.append(copy(7, half(jy, c), sibling))
        to_sib[-1].start()
        copy(4, quarter(jd, 0), sibling).wait_recv()
        copy(5, quarter(jd, 1), sibling).wait_recv()
        to_sib.append(copy(8, half(jd, c), sibling))
        to_sib[-1].start()
        for k, jk in enumerate((jx, jy, jd)):
            copy(6 + k, half(jk, 1 - c), sibling).wait_recv()
            copy(9 + k, sg_ref.at[jk], sibling).wait_recv()
        for cp in first + [along_y, along_x] + to_sib:
            cp.wait_send()

    return pl.pallas_call(
        body,
        name="gather_weights",
        out_shape=(jax.ShapeDtypeStruct((N_CHIPS, r_in, c_in), BF16),
                   jax.ShapeDtypeStruct((N_CHIPS,) + small.shape, F32)),
        in_specs=[VMEM_SPEC, VMEM_SPEC],
        out_specs=(VMEM_SPEC, VMEM_SPEC),
        scratch_shapes=[pltpu.SemaphoreType.DMA((12,)), pltpu.SemaphoreType.DMA((12,))],
        compiler_params=pltpu.CompilerParams(vmem_limit_bytes=VMEM_LIMIT, collective_id=6),
    )(w_in, small)


def _proj_segments(d_lru, d_qk, d_ret, chunk_w):
    widths = [d_lru, d_lru, d_qk, d_qk, d_ret, d_ret]
    segs, col = [], 0
    for w in widths:
        parts, off = [], 0
        while off < w:
            jj, inner = divmod(col + off, chunk_w)
            take = min(w - off, chunk_w - inner)
            parts.append((jj, inner, off, take))
            off += take
        segs.append(parts)
        col += w
    return segs


def _in_proj(x2, meta, gain, wg, cos_t, sin_t, w_out, qdec, kdec, tm, d_lru, d_qk, d_ret):
    s_len, d = x2.shape
    tp = s_len + CHUNK
    nt, nb = tp // tm, tm // CHUNK
    segs = _proj_segments(d_lru, d_qk, d_ret, wg.shape[2])
    outs = [(d, F32), (d_lru, F32), (d_lru, F32)] + [(d_qk, BF16)] * 4 + [(d_ret, BF16), (d_ret, F32)]
    r_out, c_out = w_out.shape
    h_out = r_out // 2
    fwd_step = min(6, nt - 1)

    def gather_w_out(i, wout_ref, wo_ref, wob, send_sems, recv_sems, local_sem):
        x, y, c, chips = _position()
        j = 2 * x + y
        sibling = (x, y, 1 - c)

        def half(jj, cc):
            return wo_ref.at[jj, pl.ds(cc * h_out, h_out), :]

        local = pltpu.make_async_copy(wob, wo_ref.at[j], local_sem)
        first = [_remote(wob.at[pl.ds(c * h_out, h_out), :], half(j, c), send_sems, recv_sems, k, (cx, cy, c))
                 for k, (cx, cy) in enumerate(chips)]
        passed = [_remote(half(2 * cx + cy, c), half(2 * cx + cy, c), send_sems, recv_sems, 3 + k, sibling)
                  for k, (cx, cy) in enumerate(chips)]

        @pl.when(i == 0)
        def _():
            wob[...] = wout_ref[...].astype(BF16)
            _peer_barrier([sibling] + [(cx, cy, c) for cx, cy in chips])
            local.start()
            for cp in first:
                cp.start()

        @pl.when(i == fwd_step)
        def _():
            for k, (cx, cy) in enumerate(chips):
                _remote(half(2 * cx + cy, c), half(2 * cx + cy, c), send_sems, recv_sems, k, sibling).wait_recv()
                passed[k].start()

        @pl.when(i == nt - 1)
        def _():
            for k, (cx, cy) in enumerate(chips):
                jk = 2 * cx + cy
                _remote(half(jk, 1 - c), half(jk, 1 - c), send_sems, recv_sems, 3 + k, sibling).wait_recv()
            for cp in first + passed:
                cp.wait_send()
            local.wait()

    def body(*refs):
        xb = refs[:nb]
        meta_ref, g_ref, w_ref, cos_ref, sin_ref, wout_ref, qdec_ref, kdec_ref = refs[nb:nb + 8]
        hp_ref, lx_ref, lg_ref, qb_ref, kb_ref, qd_ref, kd_ref, vb_ref, rg_ref = refs[nb + 8:nb + 17]
        wo_ref, q_s, k_s, wob, send_sems, recv_sems, local_sem = refs[nb + 17:]
        i = pl.program_id(0)
        gather_w_out(i, wout_ref, wo_ref, wob, send_sems, recv_sems, local_sem)
        blocks = [r[...] for r in xb]
        head = jnp.concatenate([jnp.zeros((PAD_ROWS, d), F32), meta_ref[...]], axis=0)
        blocks[0] = jnp.where(i == 0, head, blocks[0])
        h = jnp.concatenate(blocks, axis=0)
        hp_ref[...] = h
        rinv = lax.rsqrt(jnp.mean(h * h, axis=-1, keepdims=True) + EPS)
        u = ((h * rinv) * g_ref[...]).astype(BF16)
        for out_ref, parts in zip([lx_ref, lg_ref, q_s, k_s, vb_ref, rg_ref], segs):
            for jj, inner, off, take in parts:
                out_ref[:, off:off + take] = _dot(u, w_ref[jj, :, inner:inner + take]).astype(out_ref.dtype)
        cos = _tile_lanes(cos_ref[...], d_qk // LANES)
        sin = _tile_lanes(sin_ref[...], d_qk // LANES)
        q = q_s[...]
        q = q * cos + _rot_partner(q) * sin
        k = k_s[...]
        k = (k * cos + _rot_partner(k) * sin) * (QK_DIM ** -0.5)
        qb_ref[...] = q.astype(BF16)
        kb_ref[...] = k.astype(BF16)
        qd_ref[...] = (q * jnp.concatenate([qdec_ref[...]] * nb, axis=0)).astype(BF16)
        kd_ref[...] = (k * jnp.concatenate([kdec_ref[...]] * nb, axis=0)).astype(BF16)

    x_specs = [pl.BlockSpec((CHUNK, d), functools.partial(lambda i, b: (jnp.maximum(i * nb + b - 1, 0), 0), b=b))
               for b in range(nb)]
    tile = lambda w: pl.BlockSpec((tm, w), lambda i: (i, 0))
    return pl.pallas_call(
        body,
        name="in_proj",
        grid=(nt,),
        in_specs=x_specs + [pl.BlockSpec(meta.shape, lambda i: (0, 0)),
                            pl.BlockSpec(gain.shape, lambda i: (0, 0)),
                            pl.BlockSpec(wg.shape, lambda i: (0, 0, 0)),
                            tile(LANES), tile(LANES),
                            pl.BlockSpec(w_out.shape, lambda i: (0, 0)),
                            pl.BlockSpec(qdec.shape, lambda i: (0, 0)), pl.BlockSpec(kdec.shape, lambda i: (0, 0))],
        out_specs=[tile(w) for w, _ in outs] + [ANY_SPEC],
        out_shape=[jax.ShapeDtypeStruct((tp, w), dt) for w, dt in outs]
                  + [jax.ShapeDtypeStruct((N_CHIPS, r_out, c_out), BF16)],
        scratch_shapes=[pltpu.VMEM((tm, d_qk), F32), pltpu.VMEM((tm, d_qk), F32),
                        pltpu.VMEM((r_out, c_out), BF16), pltpu.SemaphoreType.DMA((6,)),
                        pltpu.SemaphoreType.DMA((6,)), pltpu.SemaphoreType.DMA],
        compiler_params=pltpu.CompilerParams(dimension_semantics=("arbitrary",), vmem_limit_bytes=VMEM_LIMIT,
                                             collective_id=7),
    )(*([x2] * nb), meta, gain, wg, cos_t, sin_t, w_out, qdec, kdec)


def _segment_scan(a3, u3, out3, p3, carry, tm, reverse):
    groups = a3.shape[0]
    seg = tm // SUBLANES

    def step(j, state):
        hs, ps = state
        rows = pl.ds((seg - 1 - j) if reverse else j, SUBLANES, stride=seg)
        new_h, new_p = [], []
        for g in range(groups):
            a = a3[g, rows, :]
            h = a * hs[g] + u3[g, rows, :]
            p = ps[g] * a
            out3[g, rows, :] = h
            p3[g, rows, :] = p
            new_h.append(h)
            new_p.append(p)
        return tuple(new_h), tuple(new_p)

    zeros = tuple(jnp.zeros((SUBLANES, LANES), F32) for _ in range(groups))
    ones = tuple(jnp.ones((SUBLANES, LANES), F32) for _ in range(groups))
    lax.fori_loop(0, seg, step, (zeros, ones))
    carries = [carry[:, g * LANES:(g + 1) * LANES] for g in range(groups)]
    for s in (reversed(range(SUBLANES)) if reverse else range(SUBLANES)):
        rows = slice(s * seg, (s + 1) * seg)
        edge = s * seg if reverse else (s + 1) * seg - 1
        for g in range(groups):
            out3[g, rows, :] = out3[g, rows, :] + p3[g, rows, :] * carries[g]
            carries[g] = out3[g, edge:edge + 1, :]
    return jnp.concatenate(carries, axis=1)


def _softplus_neg(lam):
    z = -lam
    e = jnp.exp(-jnp.abs(z))
    e1 = 1.0 + e
    log1p_e = jnp.where(e1 == 1.0, e, jnp.log(e1) * (e / (e1 - 1.0)))
    return jnp.maximum(z, 0.0) + log1p_e


def _lru_fwd(lx, lg, cw, cb, wr, br, wi, bi, lam, tm):
    tp, w = lx.shape
    nt = tp // tm
    per8 = tm // SUBLANES
    n_heads = wr.shape[0]

    def body(lx_ref, lxp_ref, lg_ref, cw_ref, cb_ref, wr_ref, br_ref, wi_ref, bi_ref, lam_ref,
             hl_ref, y_ref, xc_ref, r_ref, ig_ref, a_ref, beta_ref, w4_ref, a_s, u_s, h_s, p_s, carry):
        i = pl.program_id(0)

        @pl.when(i == 0)
        def _():
            carry[...] = jnp.zeros_like(carry)

        sp = _softplus_neg(lam_ref[...])
        row = lax.broadcasted_iota(jnp.int32, (tm, 1), 0) + i * tm
        for hd in range(n_heads):
            hs = slice(hd * LANES, (hd + 1) * LANES)
            lxv = lx_ref[:, hs]
            prev8 = jnp.where(i == 0, 0.0, lxp_ref[:, hs])
            xc = cb_ref[:, hs] + _shift_down(lxv, prev8, 3) * cw_ref[0:1, hs]
            xc = xc + _shift_down(lxv, prev8, 2) * cw_ref[1:2, hs]
            xc = xc + _shift_down(lxv, prev8, 1) * cw_ref[2:3, hs]
            xc = xc + lxv * cw_ref[3:4, hs]
            xc_ref[:, hs] = xc
            xh = xc.astype(BF16)
            r = _sigmoid(_dot(xh, wr_ref[hd].astype(BF16)) + br_ref[:, hs])
            ig = _sigmoid(_dot(xh, wi_ref[hd].astype(BF16)) + bi_ref[:, hs])
            r_ref[:, hs] = r
            ig_ref[:, hs] = ig
            log_a = (-LRU_C * r) * sp[:, hs]
            a = jnp.exp(log_a)
            a_ref[:, hs] = a
            a2 = a * a
            beta2 = jnp.maximum((1.0 + a2) * jnp.tanh(-log_a), 1e-37)
            rsb = lax.rsqrt(beta2)
            beta = beta2 * rsb
            beta_ref[:, hs] = beta
            w4_ref[:, hs] = a2 * rsb
            a_s[hd] = a
            u_s[hd] = jnp.where(row >= PAD_ROWS, beta * ig * xc, 0.0)
        carry[0:1, :] = _segment_scan(a_s, u_s, h_s, p_s, carry[0:1, :], tm, reverse=False)
        for hd in range(n_heads):
            hs = slice(hd * LANES, (hd + 1) * LANES)
            hl = h_s[hd]
            hl_ref[:, hs] = hl
            g = lg_ref[:, hs]
            y_ref[:, hs] = (hl * (g * _sigmoid(g))).astype(BF16)

    tile = pl.BlockSpec((tm, w), lambda i: (i, 0))
    prev = pl.BlockSpec((SUBLANES, w), lambda i: (jnp.maximum(i * per8 - 1, 0), 0))
    vec = pl.BlockSpec((1, w), lambda i: (0, 0))
    mat = pl.BlockSpec(wr.shape, lambda i: (0, 0, 0))
    f32_out = jax.ShapeDtypeStruct((tp, w), F32)
    return pl.pallas_call(
        body,
        name="lru_fwd",
        grid=(nt,),
        in_specs=[tile, prev, tile, pl.BlockSpec(cw.shape, lambda i: (0, 0)), vec, mat, vec, mat, vec, vec],
        out_specs=[tile] * 8,
        out_shape=[f32_out, jax.ShapeDtypeStruct((tp, w), BF16)] + [f32_out] * 6,
        scratch_shapes=[pltpu.VMEM((w // LANES, tm, LANES), F32)] * 4 + [pltpu.VMEM((SUBLANES, w), F32)],
        compiler_params=pltpu.CompilerParams(dimension_semantics=("arbitrary",), vmem_limit_bytes=VMEM_LIMIT),
    )(lx, lx, lg, cw, cb, wr, br, wi, bi, lam)


def _ret_tables():
    log_g = jnp.log1p(-jnp.exp2(-5.0 - jnp.arange(HEADS, dtype=F32)))
    idx = jnp.arange(CHUNK, dtype=F32)
    diff = idx[:, None] - idx[None, :]
    dmask = jnp.where(diff[None] >= 0.0, jnp.exp(jnp.maximum(diff, 0.0)[None] * log_g[:, None, None]), 0.0)
    kdec = jnp.repeat(jnp.exp((CHUNK - 1.0 - idx)[:, None] * log_g[None, :]), QK_DIM, axis=1)
    qdec = jnp.repeat(jnp.exp((idx + 1.0)[:, None] * log_g[None, :]), QK_DIM, axis=1)
    g_chunk = jnp.exp(CHUNK * log_g)
    g_rows = jnp.repeat(g_chunk, QK_DIM).reshape(HEADS // 2, 2 * QK_DIM, 1)
    g_state = jnp.broadcast_to(g_rows, (HEADS // 2, 2 * QK_DIM, 2 * LANES))
    r_head = jnp.arange(2 * QK_DIM)[:, None] // QK_DIM
    c_head = jnp.arange(2 * LANES)[None, :] // LANES
    block_diag = (r_head == c_head).astype(F32)
    return dmask, qdec, kdec, g_state, block_diag


def _head_norm(o_h):
    mu = jnp.mean(o_h, axis=-1, keepdims=True)
    oc = o_h - mu
    var = jnp.mean(oc * oc, axis=-1, keepdims=True)
    rstd = lax.rsqrt(var + EPS)
    return oc * rstd, rstd


def _ret_fwd(qb, kb, qd, kd, vb, rg, gain, tables, tm):
    tp, d_qk = qb.shape
    d_ret = vb.shape[1]
    n_ch = tp // CHUNK
    cps = tm // CHUNK
    n_pairs = HEADS // 2
    dmask, _, _, g_state, block_diag = tables

    def body(q_ref, k_ref, qd_ref, kd_ref, v_ref, rg_ref, gain_ref, dm_ref, gs_ref, bd_ref,
             o_ref, y_ref, rp_ref, state):
        n = pl.program_id(0)

        @pl.when(n == 0)
        def _():
            state[...] = jnp.zeros_like(state)

        lane = lax.broadcasted_iota(jnp.int32, (CHUNK, LANES), 1)
        for ci in range(cps):
            rs = slice(ci * CHUNK, (ci + 1) * CHUNK)
            for p in range(n_pairs):
                qs = slice(p * LANES, (p + 1) * LANES)
                vs = slice(p * 2 * LANES, (p + 1) * 2 * LANES)
                qp, kb = q_ref[rs, qs], k_ref[rs, qs]
                vb = v_ref[rs, vs]
                qd, kd = qd_ref[rs, qs], kd_ref[rs, qs]
                st = state[p]
                st_b = st.astype(BF16)
                rp_ref[ci, p] = st_b
                cross = _dot(qd, st_b)
                for e in range(2):
                    hd = 2 * p + e
                    hs = slice(hd * LANES, (hd + 1) * LANES)
                    es = slice(e * LANES, (e + 1) * LANES)
                    qm = jnp.where((lane // QK_DIM) == e, qp, jnp.zeros_like(qp))
                    s = _dot_nt(qm, kb) * dm_ref[hd]
                    o_h = _dot(s.astype(BF16), vb[:, es]) + cross[:, es]
                    o_ref[rs, hs] = o_h
                    xhat, _ = _head_norm(o_h)
                    g = rg_ref[rs, hs]
                    y_ref[rs, hs] = ((xhat * gain_ref[:, hs]) * (g * _sigmoid(g))).astype(BF16)
                state[p] = gs_ref[p] * st + bd_ref[...] * _dot_tn(kd, vb)

    ch = lambda w: pl.BlockSpec((tm, w), lambda n: (n, 0))
    const2 = lambda a: pl.BlockSpec(a.shape, lambda n: (0, 0))
    const3 = lambda a: pl.BlockSpec(a.shape, lambda n: (0, 0, 0))
    return pl.pallas_call(
        body,
        name="ret_fwd",
        grid=(n_ch // cps,),
        in_specs=[ch(d_qk)] * 4 + [ch(d_ret), ch(d_ret), const2(gain), const3(dmask), const3(g_state),
                                   const2(block_diag)],
        out_specs=[ch(d_ret), ch(d_ret),
                   pl.BlockSpec((cps, n_pairs, 2 * QK_DIM, 2 * LANES), lambda n: (n, 0, 0, 0))],
        out_shape=[jax.ShapeDtypeStruct((tp, d_ret), F32), jax.ShapeDtypeStruct((tp, d_ret), BF16),
                   jax.ShapeDtypeStruct((n_ch, n_pairs, 2 * QK_DIM, 2 * LANES), BF16)],
        scratch_shapes=[pltpu.VMEM((n_pairs, 2 * QK_DIM, 2 * LANES), F32)],
        compiler_params=pltpu.CompilerParams(dimension_semantics=("arbitrary",), vmem_limit_bytes=VMEM_LIMIT),
    )(qb, kb, qd, kd, vb, rg, gain, dmask, g_state, block_diag)


def _out_proj_loss(y_lru, y_ret, hp, tgt, wo, gain_f, tm):
    tp, d = hp.shape
    w_lru = y_lru.shape[1]
    w_mix = wo.shape[0]
    nt, nb = tp // tm, tm // CHUNK

    def body(*refs):
        yl_ref, yr_ref, hp_ref = refs[:3]
        tb = refs[3:3 + nb]
        wo_ref, gf_ref = refs[3 + nb:5 + nb]
        dh2_ref, dyl_ref, dyr_ref, dwo_ref, dgf_ref, loss_ref = refs[5 + nb:]
        i = pl.program_id(0)

        @pl.when(i == 0)
        def _():
            dwo_ref[...] = jnp.zeros_like(dwo_ref)
            dgf_ref[...] = jnp.zeros_like(dgf_ref)
            loss_ref[...] = jnp.zeros_like(loss_ref)

        yl, yr = yl_ref[...], yr_ref[...]
        h2 = hp_ref[...] + _dot(yl, wo_ref[0:w_lru, :]) + _dot(yr, wo_ref[w_lru:w_mix, :])
        rinv = lax.rsqrt(jnp.mean(h2 * h2, axis=-1, keepdims=True) + EPS)
        nrm = h2 * rinv
        gf = gf_ref[...]
        tgt_v = jnp.concatenate([r[...] for r in tb], axis=0)
        row = lax.broadcasted_iota(jnp.int32, (tm, 1), 0) + i * tm
        err = jnp.where(row >= CHUNK, nrm * gf - tgt_v, 0.0)
        loss_ref[...] += 0.5 * jnp.sum(jnp.mean(err * err, axis=-1, keepdims=True))
        dout = err * (1.0 / d)
        dgf_ref[...] += jnp.sum(dout * nrm, axis=0, keepdims=True)
        dn = dout * gf
        dh2 = rinv * (dn - nrm * jnp.mean(dn * nrm, axis=-1, keepdims=True))
        dh2_ref[...] = dh2
        dh2b = dh2.astype(BF16)
        dyl_ref[...] = _dot_nt(dh2b, wo_ref[0:w_lru, :])
        dyr_ref[...] = _dot_nt(dh2b, wo_ref[w_lru:w_mix, :])
        dwo_ref[0:w_lru, :] += _dot_tn(yl, dh2b)
        dwo_ref[w_lru:w_mix, :] += _dot_tn(yr, dh2b)

    tile = lambda w: pl.BlockSpec((tm, w), lambda i: (i, 0))
    t_specs = [pl.BlockSpec((CHUNK, d), functools.partial(lambda i, b: (jnp.maximum(i * nb + b - 1, 0), 0), b=b))
               for b in range(nb)]
    return pl.pallas_call(
        body,
        name="out_proj_loss",
        grid=(nt,),
        in_specs=[tile(w_lru), tile(w_mix - w_lru), tile(d)] + t_specs +
                 [pl.BlockSpec(wo.shape, lambda i: (0, 0)), pl.BlockSpec(gain_f.shape, lambda i: (0, 0))],
        out_specs=[tile(d), tile(w_lru), tile(w_mix - w_lru), pl.BlockSpec(wo.shape, lambda i: (0, 0)),
                   pl.BlockSpec((1, d), lambda i: (0, 0)), pl.BlockSpec((SUBLANES, LANES), lambda i: (0, 0))],
        out_shape=[jax.ShapeDtypeStruct((tp, d), F32), jax.ShapeDtypeStruct((tp, w_lru), F32),
                   jax.ShapeDtypeStruct((tp, w_mix - w_lru), F32), jax.ShapeDtypeStruct(wo.shape, F32),
                   jax.ShapeDtypeStruct((1, d), F32), jax.ShapeDtypeStruct((SUBLANES, LANES), F32)],
        compiler_params=pltpu.CompilerParams(dimension_semantics=("arbitrary",), vmem_limit_bytes=VMEM_LIMIT),
    )(y_lru, y_ret, hp, *([tgt] * nb), wo, gain_f)


def _ret_bwd(qb, kb, qd, kd, vb, rg, o, rprev, dy, gain, cos_t, sin_t, tables, tm, ride=None):
    tp, d_qk = qb.shape
    d_ret = vb.shape[1]
    n_ch = tp // CHUNK
    cps = tm // CHUNK
    n_pairs = HEADS // 2
    dmask, qdec, kdec, g_state, block_diag = tables

    dmask_t = jnp.swapaxes(dmask, 1, 2)

    def body(q_ref, k_ref, qdb_ref, kdb_ref, v_ref, rg_ref, o_ref, rp_ref, dy_ref, gain_ref, cos_ref, sin_ref,
             dm_ref, dmt_ref, qd_ref, kd_ref, gs_ref, bd_ref, dq_ref, dk_ref, dv_ref, drg_ref, dgain_ref, dstate):
        n = pl.program_id(0)

        @pl.when(n == 0)
        def _():
            dstate[...] = jnp.zeros_like(dstate)
            dgain_ref[...] = jnp.zeros_like(dgain_ref)

        lane = lax.broadcasted_iota(jnp.int32, (CHUNK, LANES), 1)
        for ci in reversed(range(cps)):
            rs = slice(ci * CHUNK, (ci + 1) * CHUNK)
            dq_parts, dk_parts = [], []
            for p in range(n_pairs):
                qs = slice(p * LANES, (p + 1) * LANES)
                vs = slice(p * 2 * LANES, (p + 1) * 2 * LANES)
                do_parts = []
                for e in range(2):
                    hd = 2 * p + e
                    hs = slice(hd * LANES, (hd + 1) * LANES)
                    xhat, rstd = _head_norm(o_ref[rs, hs])
                    g = rg_ref[rs, hs]
                    sg = _sigmoid(g)
                    dyh = dy_ref[rs, hs]
                    gn = gain_ref[:, hs]
                    d_on = dyh * (g * sg)
                    drg_ref[rs, hs] = (dyh * (xhat * gn) * (sg * (1.0 + g * (1.0 - sg)))).astype(BF16)
                    dgain_ref[:, hs] += jnp.sum(d_on * xhat, axis=0, keepdims=True)
                    dxh = d_on * gn
                    do_parts.append(rstd * (dxh - jnp.mean(dxh, axis=-1, keepdims=True)
                                            - xhat * jnp.mean(dxh * xhat, axis=-1, keepdims=True)))
                do_b = jnp.concatenate(do_parts, axis=1).astype(BF16)
                qp, kb = q_ref[rs, qs], k_ref[rs, qs]
                vb = v_ref[rs, vs]
                qd, kd = qdb_ref[rs, qs], kdb_ref[rs, qs]
                dst = dstate[p]
                dst_b = dst.astype(BF16)
                dqp = _dot_nt(do_b, rp_ref[ci, p]) * qd_ref[:, qs]
                dkp = _dot_nt(vb, dst_b) * kd_ref[:, qs]
                dvp = _dot(kd, dst_b)
                dv_parts = []
                for e in range(2):
                    hd = 2 * p + e
                    es = slice(e * LANES, (e + 1) * LANES)
                    mine = (lane // QK_DIM) == e
                    qm = jnp.where(mine, qp, jnp.zeros_like(qp))
                    km = jnp.where(mine, kb, jnp.zeros_like(kb))
                    ds = (_dot_nt(do_b[:, es], vb[:, es]) * dm_ref[hd]).astype(BF16)
                    s_t = (_dot_nt(kb, qm) * dmt_ref[hd]).astype(BF16)
                    ds_t = (_dot_nt(vb[:, es], do_b[:, es]) * dmt_ref[hd]).astype(BF16)
                    dv_parts.append(dvp[:, es] + _dot(s_t, do_b[:, es]))
                    dqp = dqp + _dot(ds, km)
                    dkp = dkp + _dot(ds_t, qm)
                dv_ref[rs, vs] = jnp.concatenate(dv_parts, axis=1).astype(BF16)
                dstate[p] = gs_ref[p] * dst + bd_ref[...] * _dot_tn(qd, do_b)
                dq_parts.append(dqp)
                dk_parts.append(dkp)
            cos = _tile_lanes(cos_ref[rs, :], d_qk // LANES)
            sin = _tile_lanes(sin_ref[rs, :], d_qk // LANES)
            dq = jnp.concatenate(dq_parts, axis=1)
            dk = jnp.concatenate(dk_parts, axis=1) * (QK_DIM ** -0.5)
            dq_ref[rs, :] = (dq * cos + _rot_partner(dq * sin)).astype(BF16)
            dk_ref[rs, :] = (dk * cos + _rot_partner(dk * sin)).astype(BF16)

    last = n_ch // cps - 1
    ch = lambda w: pl.BlockSpec((tm, w), lambda n: (last - n, 0))
    const2 = lambda a: pl.BlockSpec(a.shape, lambda n: (0, 0))
    const3 = lambda a: pl.BlockSpec(a.shape, lambda n: (0, 0, 0))
    return _hosted_call(
        body, ride, n_ch // cps,
        name="ret_bwd", barrier_id=1,
        in_specs=[ch(d_qk)] * 4 + [ch(d_ret), ch(d_ret), ch(d_ret),
                  pl.BlockSpec((cps, n_pairs, 2 * QK_DIM, 2 * LANES), lambda n: (last - n, 0, 0, 0)),
                  ch(d_ret), const2(gain), ch(LANES), ch(LANES),
                  const3(dmask), const3(dmask_t), const2(qdec), const2(kdec), const3(g_state), const2(block_diag)],
        out_specs=[ch(d_qk), ch(d_qk), ch(d_ret), ch(d_ret), pl.BlockSpec((1, d_ret), lambda n: (0, 0))],
        out_shape=[jax.ShapeDtypeStruct((tp, d_qk), BF16), jax.ShapeDtypeStruct((tp, d_qk), BF16),
                   jax.ShapeDtypeStruct((tp, d_ret), BF16), jax.ShapeDtypeStruct((tp, d_ret), BF16),
                   jax.ShapeDtypeStruct((1, d_ret), F32)],
        scratch_shapes=[pltpu.VMEM((n_pairs, 2 * QK_DIM, 2 * LANES), F32)],
        args=(qb, kb, qd, kd, vb, rg, o, rprev, dy, gain, cos_t, sin_t, dmask, dmask_t, qdec, kdec, g_state,
              block_diag),
    )


def _lru_bwd(lx, lg, hl, dy, saved, cw, wr, wi, lam, dgain, dgf, tm, ride=None):
    tp, w = lx.shape
    nt = tp // tm
    per8 = tm // SUBLANES
    n_heads = wr.shape[0]
    vec_row = {name: ROW_VEC + VEC_NAMES.index(name) for name in VEC_NAMES}

    def body(lx_ref, lg_ref, hl_ref, hlp_ref, dy_ref, xc_ref, r_ref, ig_ref, a_ref, beta_ref, w4_ref,
             cw_ref, wr_ref, wi_ref, lam_ref, dgain_ref, dgf_ref,
             dlx_ref, dlg_ref, pk_ref,
             g_s, b_s, carry, dxc_next, a_next):
        i = pl.program_id(0)
        first_tile = i == nt - 1
        heads = [slice(hd * LANES, (hd + 1) * LANES) for hd in range(n_heads)]

        def add_row(hd, row, value):
            pk_ref[hd, row:row + 1, :] += value

        @pl.when(i == 0)
        def _():
            carry[...] = jnp.zeros_like(carry)
            dxc_next[...] = jnp.zeros_like(dxc_next)
            a_next[...] = jnp.zeros_like(a_next)
            pk_ref[...] = jnp.zeros_like(pk_ref)
            for hd, hs in enumerate(heads):
                add_row(hd, vec_row["ret_norm_gain"], dgain_ref[:, hs])
                add_row(hd, vec_row["final_norm_gain"], dgf_ref[:, hs])

        for hd, hs in enumerate(heads):
            g = lg_ref[:, hs]
            sg = _sigmoid(g)
            dyv = dy_ref[:, hs]
            dlg_ref[:, hs] = (dyv * hl_ref[:, hs] * (sg * (1.0 + g * (1.0 - sg)))).astype(BF16)
            g_s[hd] = dyv * (g * sg)
            b_s[hd] = _shift_up(a_ref[:, hs], a_next[:, hs], 1)
        carry[0:1, :] = _segment_scan(b_s, g_s, g_s, b_s, carry[0:1, :], tm, reverse=True)
        a_next[...] = a_ref[0:SUBLANES, :]
        row = lax.broadcasted_iota(jnp.int32, (tm, 1), 0) + (nt - 1 - i) * tm
        lam_v = lam_ref[...]
        dlam_scale = LRU_C * _sigmoid(-lam_v)
        dr_scale = -LRU_C * _softplus_neg(lam_v)
        for hd, hs in enumerate(heads):
            a, beta, r, ig, xc = a_ref[:, hs], beta_ref[:, hs], r_ref[:, hs], ig_ref[:, hs], xc_ref[:, hs]
            dh = g_s[hd]
            hprev = _shift_down(hl_ref[:, hs], jnp.where(first_tile, 0.0, hlp_ref[:, hs]), 1)
            du = jnp.where(row >= PAD_ROWS, dh, 0.0)
            dbeta = du * ig * xc
            d_ig = du * beta * xc
            dxc = du * beta * ig
            dloga = (dh * hprev) * a - dbeta * w4_ref[:, hs]
            add_row(hd, vec_row["lru_lambda"], jnp.sum(dloga * r, axis=0, keepdims=True) * dlam_scale[:, hs])
            dpr = (dloga * dr_scale[:, hs]) * r * (1.0 - r)
            dpi = d_ig * ig * (1.0 - ig)
            add_row(hd, vec_row["b_rg"], jnp.sum(dpr, axis=0, keepdims=True))
            add_row(hd, vec_row["b_ig"], jnp.sum(dpi, axis=0, keepdims=True))
            xh, dprh, dpih = xc.astype(BF16), dpr.astype(BF16), dpi.astype(BF16)
            pk_ref[hd, ROW_WR:ROW_WR + LANES, :] += _dot_tn(xh, dprh)
            pk_ref[hd, ROW_WI:ROW_WI + LANES, :] += _dot_tn(xh, dpih)
            dxc = dxc + _dot_nt(dprh, wr_ref[hd].astype(BF16)) + _dot_nt(dpih, wi_ref[hd].astype(BF16))
            nxt = dxc_next[:, hs]
            up1, up2, up3 = _shift_up(dxc, nxt, 1), _shift_up(dxc, nxt, 2), _shift_up(dxc, nxt, 3)
            dlx = dxc * cw_ref[3:4, hs]
            dlx = dlx + up1 * cw_ref[2:3, hs]
            dlx = dlx + up2 * cw_ref[1:2, hs]
            dlx = dlx + up3 * cw_ref[0:1, hs]
            dlx_ref[:, hs] = dlx.astype(BF16)
            dxc_next[:, hs] = dxc[0:SUBLANES]
            lxv = lx_ref[:, hs]
            add_row(hd, vec_row["conv_b"], jnp.sum(dxc, axis=0, keepdims=True))
            for kk, shifted in enumerate((up3, up2, up1, dxc)):
                add_row(hd, ROW_CONV + kk, jnp.sum(shifted * lxv, axis=0, keepdims=True))

    last = nt - 1
    tile = pl.BlockSpec((tm, w), lambda i: (last - i, 0))
    prev = pl.BlockSpec((SUBLANES, w), lambda i: (jnp.maximum((last - i) * per8 - 1, 0), 0))
    vec = pl.BlockSpec((1, w), lambda i: (0, 0))
    mat = pl.BlockSpec(wr.shape, lambda i: (0, 0, 0))
    cwb = pl.BlockSpec(cw.shape, lambda i: (0, 0))
    packed = (n_heads, UNIT_ROWS, LANES)
    return _hosted_call(
        body, ride, nt,
        name="lru_bwd", barrier_id=2,
        in_specs=[tile, tile, tile, prev, tile] + [tile] * 6 + [cwb, mat, mat, vec, vec, vec],
        out_specs=[tile, tile, pl.BlockSpec(packed, lambda i: (0, 0, 0))],
        out_shape=[jax.ShapeDtypeStruct((tp, w), BF16), jax.ShapeDtypeStruct((tp, w), BF16),
                   jax.ShapeDtypeStruct(packed, F32)],
        scratch_shapes=[pltpu.VMEM((w // LANES, tm, LANES), F32)] * 2 + [pltpu.VMEM((SUBLANES, w), F32)] * 3,
        args=(lx, lg, hl, hl, dy, *saved, cw, wr, wi, lam, dgain, dgf),
    )


def _in_proj_dw(dparts, hp, gain, wg_shape, ride=None):
    tp, d = hp.shape
    n_ch = tp // CHUNK
    per = next(p for p in (4, 2, 5, 3, 1) if (n_ch - 1) % p == 0)
    n_steps = 1 + (n_ch - 1) // per
    widths = [p.shape[1] for p in dparts]
    segs = _proj_segments(widths[0], widths[2], widths[4], wg_shape[2])

    def body(*refs):
        dp = [refs[p * per:(p + 1) * per] for p in range(6)]
        hp_b = refs[6 * per:7 * per]
        g_ref, dwg_ref, acc, sem = refs[7 * per:]
        i = pl.program_id(0)

        def accumulate(blocks):
            h = jnp.concatenate([hp_b[b][...] for b in blocks], axis=0)
            rinv = lax.rsqrt(jnp.mean(h * h, axis=-1, keepdims=True) + EPS)
            u = ((h * rinv) * g_ref[...]).astype(BF16)
            for p_refs, parts in zip(dp, segs):
                for jj, inner, off, take in parts:
                    seg = jnp.concatenate([p_refs[b][:, off:off + take] for b in blocks], axis=0)
                    acc[jj, :, inner:inner + take] += _dot_tn(u, seg)

        @pl.when(i == 0)
        def _():
            acc[...] = jnp.zeros_like(acc)
            accumulate([0])

        @pl.when(i > 0)
        def _():
            accumulate(list(range(per)))

        @pl.when(i == n_steps - 1)
        def _():
            cp = pltpu.make_async_copy(acc, dwg_ref, sem)
            cp.start()
            cp.wait()

    def blocks(w):
        return [pl.BlockSpec((CHUNK, w), functools.partial(
            lambda i, b: (jnp.where(i == 0, b, per * (i - 1) + 1 + b), 0), b=b)) for b in range(per)]

    in_specs, args = [], []
    for a, w in list(zip(dparts, widths)) + [(hp, d)]:
        in_specs += blocks(w)
        args += [a] * per
    outs, rides = _hosted_call(
        body, ride, n_steps,
        name="in_proj_dw", barrier_id=3,
        in_specs=in_specs + [pl.BlockSpec(gain.shape, lambda i: (0, 0))],
        out_specs=[ANY_SPEC],
        out_shape=[jax.ShapeDtypeStruct(wg_shape, F32)],
        scratch_shapes=[pltpu.VMEM(wg_shape, F32), pltpu.SemaphoreType.DMA],
        args=(*args, gain),
    )
    return outs[0], rides


def _in_proj_dx(dparts, hp, dh2, gain, wg, s_len, tm, ride=None):
    tp, d = hp.shape
    nt = tp // tm
    widths = [p.shape[1] for p in dparts]
    segs = _proj_segments(widths[0], widths[2], widths[4], wg.shape[2])

    def body(*refs):
        dp = refs[:6]
        hp_ref, dh2_ref, g_ref, w_ref = refs[6:10]
        gx_ref, dmeta_ref, dg_ref = refs[10:13]
        stage, sems = refs[13:]
        i = pl.program_id(0)

        @pl.when(i == 0)
        def _():
            dg_ref[...] = jnp.zeros_like(dg_ref)

        h = hp_ref[...]
        rinv = lax.rsqrt(jnp.mean(h * h, axis=-1, keepdims=True) + EPS)
        nrm = h * rinv
        gv = g_ref[...]
        du = jnp.zeros((tm, d), F32)
        for p_ref, parts in zip(dp, segs):
            for jj, inner, off, take in parts:
                du = du + _dot_nt(p_ref[:, off:off + take], w_ref[jj, :, inner:inner + take])
        dg_ref[...] += jnp.sum(du * nrm, axis=0, keepdims=True)
        dn = du * gv
        dh = dh2_ref[...] + rinv * (dn - nrm * jnp.mean(dn * nrm, axis=-1, keepdims=True))

        def first_copy():
            return pltpu.make_async_copy(stage.at[0, pl.ds(CHUNK, tm - CHUNK), :],
                                         gx_ref.at[pl.ds(0, tm - CHUNK), :], sems.at[0])

        def tile_copy(slot, start):
            return pltpu.make_async_copy(stage.at[slot], gx_ref.at[pl.ds(start, tm), :], sems.at[slot])

        @pl.when(i == 0)
        def _():
            dmeta_ref[...] = dh[PAD_ROWS:CHUNK]
            stage[0] = dh
            first_copy().start()

        @pl.when(i > 0)
        def _():
            slot = 1 + i % 2

            @pl.when(i >= 3)
            def _():
                tile_copy(slot, 0).wait()

            stage[slot] = dh
            tile_copy(slot, pl.multiple_of(i * tm - CHUNK, CHUNK)).start()

        @pl.when(i == nt - 1)
        def _():
            first_copy().wait()
            for step in (nt - 2, nt - 1):
                if step >= 1:
                    tile_copy(1 + step % 2, 0).wait()

    tile = lambda w: pl.BlockSpec((tm, w), lambda i: (i, 0))
    return _hosted_call(
        body, ride, nt,
        name="in_proj_dx", barrier_id=4,
        in_specs=[tile(w) for w in widths] + [tile(d), tile(d), pl.BlockSpec(gain.shape, lambda i: (0, 0)),
                                              pl.BlockSpec(wg.shape, lambda i: (0, 0, 0))],
        out_specs=[ANY_SPEC, pl.BlockSpec((N_META, d), lambda i: (0, 0)), pl.BlockSpec((1, d), lambda i: (0, 0))],
        out_shape=[jax.ShapeDtypeStruct((s_len, d), F32), jax.ShapeDtypeStruct((N_META, d), F32),
                   jax.ShapeDtypeStruct((1, d), F32)],
        scratch_shapes=[pltpu.VMEM((3, tm, d), F32), pltpu.SemaphoreType.DMA((3,))],
        args=(*dparts, hp, dh2, gain, wg),
    )


def _pair_sum(buf, recv, c_arr, tr, name):
    _, rows, cols = buf.shape

    def body(c_ref, mine_ref, got_ref, out_ref):
        out_ref[...] = (mine_ref[...] + got_ref[...]).astype(BF16)

    grid_spec = pltpu.PrefetchScalarGridSpec(
        num_scalar_prefetch=1,
        grid=(N_CHIPS, rows // tr),
        in_specs=[pl.BlockSpec((1, tr, cols), lambda jj, r, c_ref: (2 * jj + c_ref[0], r, 0)),
                  pl.BlockSpec((1, tr, cols), lambda jj, r, c_ref: (jj, r, 0))],
        out_specs=pl.BlockSpec((1, tr, cols), lambda jj, r, c_ref: (jj, r, 0)),
    )
    return pl.pallas_call(
        body,
        name=name,
        grid_spec=grid_spec,
        out_shape=jax.ShapeDtypeStruct((N_CHIPS, rows, cols), BF16),
    )(c_arr, buf, recv)


def _pair_exchange_sum(buf, c_arr, tr, name):
    _, rows, cols = buf.shape
    per = rows // tr

    def body(c_ref, src_ref, mine_ref, out_ref, got, send_sems, recv_sems):
        jj, r = pl.program_id(0), pl.program_id(1)
        x, y, c, _ = _position()
        copies = [_remote(src_ref.at[2 * k + 1 - c], got.at[k], send_sems, recv_sems, k, (x, y, 1 - c))
                  for k in range(N_CHIPS)]

        @pl.when((jj == 0) & (r == 0))
        def _():
            _peer_barrier([(x, y, 1 - c)])
            for cp in copies:
                cp.start()

        for k in range(N_CHIPS):
            @pl.when((jj == k) & (r == 0))
            def _():
                copies[k].wait_recv()

        rows_r = pl.ds(pl.multiple_of(r * tr, tr), tr)
        out_ref[0] = (mine_ref[0] + got[jj, rows_r, :]).astype(BF16)

        @pl.when((jj == N_CHIPS - 1) & (r == per - 1))
        def _():
            for cp in copies:
                cp.wait_send()

    grid_spec = pltpu.PrefetchScalarGridSpec(
        num_scalar_prefetch=1,
        grid=(N_CHIPS, per),
        in_specs=[ANY_SPEC, pl.BlockSpec((1, tr, cols), lambda jj, r, c_ref: (2 * jj + c_ref[0], r, 0))],
        out_specs=pl.BlockSpec((1, tr, cols), lambda jj, r, c_ref: (jj, r, 0)),
        scratch_shapes=[pltpu.VMEM((N_CHIPS, rows, cols), F32), pltpu.SemaphoreType.DMA((N_CHIPS,)),
                        pltpu.SemaphoreType.DMA((N_CHIPS,))],
    )
    return pl.pallas_call(
        body,
        name=name,
        grid_spec=grid_spec,
        out_shape=jax.ShapeDtypeStruct((N_CHIPS, rows, cols), BF16),
        compiler_params=pltpu.CompilerParams(dimension_semantics=("arbitrary", "arbitrary"),
                                             vmem_limit_bytes=VMEM_LIMIT, collective_id=5),
    )(c_arr, buf, buf)


def _chip_sum(mine, got, j_arr, tr, name, loss_part=None):
    _, rows, cols = got.shape
    extra = [] if loss_part is None else [loss_part]

    def body(j_ref, mine_ref, got_ref, *rest):
        out_ref = rest[-1]
        j = j_ref[0]
        acc = None
        for jj in range(N_CHIPS):
            term = jnp.where(j == jj, mine_ref[0], got_ref[jj]).astype(F32)
            acc = term if acc is None else acc + term
        out_ref[...] = acc
        if loss_part is not None:
            out_ref[ROW_LOSS:ROW_LOSS + 1, :] = rest[0][0:1, :]

    grid_spec = pltpu.PrefetchScalarGridSpec(
        num_scalar_prefetch=1,
        grid=(rows // tr,),
        in_specs=[pl.BlockSpec((1, tr, cols), lambda r, j_ref: (j_ref[0], r, 0)),
                  pl.BlockSpec((N_CHIPS, tr, cols), lambda r, j_ref: (0, r, 0))] +
                 [pl.BlockSpec(e.shape, lambda r, j_ref: (0, 0)) for e in extra],
        out_specs=pl.BlockSpec((tr, cols), lambda r, j_ref: (r, 0)),
    )
    return pl.pallas_call(
        body,
        name=name,
        grid_spec=grid_spec,
        out_shape=jax.ShapeDtypeStruct((rows, cols), F32),
    )(j_arr, mine, got, *extra)


def _finish_exchange(f_in, f_small):
    def body(fin_ref, fs_ref, rin_ref, os_ref, send_sems, recv_sems, local_sem):
        x, y, c, chips = _position()
        j = 2 * x + y
        me = 2 * j + c
        sibling = (x, y, 1 - c)
        _peer_barrier([sibling] + [(cx, cy, c) for cx, cy in chips])
        local = pltpu.make_async_copy(fs_ref, os_ref.at[me], local_sem)
        local.start()

        def copy(k, src, dst, to):
            return _remote(src, dst, send_sems, recv_sems, k, to)

        first = [copy(0, fin_ref, rin_ref, sibling), copy(1, fs_ref, os_ref.at[me], sibling)]
        first += [copy(2 + k, fs_ref, os_ref.at[me], (cx, cy, c)) for k, (cx, cy) in enumerate(chips)]
        for cp in first:
            cp.start()
        passed = []
        for k, (cx, cy) in enumerate(chips):
            unit = 2 * (2 * cx + cy) + c
            copy(2 + k, fs_ref, os_ref.at[unit], sibling).wait_recv()
            fwd = copy(5 + k, os_ref.at[unit], os_ref.at[unit], sibling)
            fwd.start()
            passed.append(fwd)
        copy(0, fin_ref, rin_ref, sibling).wait_recv()
        copy(1, fs_ref, os_ref.at[2 * j + 1 - c], sibling).wait_recv()
        for k, (cx, cy) in enumerate(chips):
            unit = 2 * (2 * cx + cy) + 1 - c
            copy(5 + k, fs_ref, os_ref.at[unit], sibling).wait_recv()
        for cp in first + passed:
            cp.wait_send()
        local.wait()

    return pl.pallas_call(
        body,
        name="grad_finish_exchange",
        in_specs=[ANY_SPEC] * 2,
        out_specs=[ANY_SPEC] * 2,
        out_shape=[jax.ShapeDtypeStruct(f_in.shape, F32), jax.ShapeDtypeStruct((N_DEV,) + f_small.shape, F32)],
        scratch_shapes=[pltpu.SemaphoreType.DMA((8,)), pltpu.SemaphoreType.DMA((8,)), pltpu.SemaphoreType.DMA],
        compiler_params=pltpu.CompilerParams(collective_id=8),
    )(f_in, f_small)


def _adamw_math(w, g, m, v):
    m = ADAM_B1 * m + (1.0 - ADAM_B1) * g
    v = ADAM_B2 * v + (1.0 - ADAM_B2) * (g * g)
    m_hat = m / (1.0 - ADAM_B1 ** ADAM_STEP)
    v_hat = v / (1.0 - ADAM_B2 ** ADAM_STEP)
    delta = -ADAM_LR * (m_hat / (jnp.sqrt(v_hat) + ADAM_EPS) + ADAM_WD * w)
    return delta, m, v


def _adamw_big(w, g_mine, g_sib, m, v, c_arr, tr, name):
    rows, cols = w.shape
    half = rows // 2
    per = half // tr

    def body(c_ref, w_ref, gm_ref, gs_ref, m_ref, v_ref, g_ref, d_ref, mo_ref, vo_ref):
        g = jnp.where(pl.program_id(0) == c_ref[0], gm_ref[...], gs_ref[...])
        g_ref[...] = g
        d_ref[...], mo_ref[...], vo_ref[...] = _adamw_math(w_ref[...], g, m_ref[...], v_ref[...])

    full = pl.BlockSpec((tr, cols), lambda h, r, c_ref: (h * per + r, 0))
    unit = pl.BlockSpec((tr, cols), lambda h, r, c_ref: (r, 0))
    grid_spec = pltpu.PrefetchScalarGridSpec(
        num_scalar_prefetch=1,
        grid=(2, per),
        in_specs=[full, unit, unit, full, full],
        out_specs=[full] * 4,
    )
    return pl.pallas_call(
        body,
        name=name,
        grid_spec=grid_spec,
        out_shape=[jax.ShapeDtypeStruct(w.shape, F32)] * 4,
    )(c_arr, w, g_mine, g_sib, m, v)


def _adamw_small(j_arr, packed, params):
    names = list(params)
    n = len(names)

    def body(j_ref, pk_ref, *refs):
        ins = refs[:3 * n]
        outs = refs[3 * n:]
        j = j_ref[0]

        def shard(row, rows):
            return jnp.concatenate([pk_ref[2 * j, row:row + rows, :], pk_ref[2 * j + 1, row:row + rows, :]], axis=1)

        def tail_sum(unit, row, rows):
            start = pl.multiple_of(UNIT_ROWS + TAIL_ROWS * unit + row, SUBLANES)
            total = pk_ref[0, pl.ds(start, rows), :]
            for dev in range(1, N_DEV):
                total = total + pk_ref[dev, pl.ds(start, rows), :]
            return total

        for idx, name in enumerate(names):
            if name == "w_rg":
                g = pk_ref[:, ROW_WR:ROW_WR + LANES, :]
            elif name == "w_ig":
                g = pk_ref[:, ROW_WI:ROW_WI + LANES, :]
            elif name == "meta_tokens":
                g = jnp.concatenate([tail_sum(2 * j, 0, N_META), tail_sum(2 * j + 1, 0, N_META)], axis=1)
            elif name == "norm_gain":
                g = jnp.concatenate([tail_sum(u, N_META, SUBLANES)[0:1] for u in range(N_DEV)], axis=1)
            elif name == "conv_w":
                g = shard(ROW_CONV, 4)
            else:
                row = ROW_VEC + VEC_NAMES.index(name)
                g = jnp.concatenate([pk_ref[u, row:row + 1, :] for u in range(N_DEV)], axis=1)
            w_ref, m_ref, v_ref = ins[3 * idx:3 * idx + 3]
            delta, m, v = _adamw_math(w_ref[...], g, m_ref[...], v_ref[...])
            g_ref, d_ref, mo_ref, vo_ref = outs[4 * idx:4 * idx + 4]
            g_ref[...], d_ref[...], mo_ref[...], vo_ref[...] = g, delta, m, v
        total = pk_ref[0, ROW_LOSS:ROW_LOSS + 1, :]
        for u in range(1, N_DEV):
            total = total + pk_ref[u, ROW_LOSS:ROW_LOSS + 1, :]
        outs[4 * n][...] = jnp.broadcast_to(total, (SUBLANES, LANES))

    flat_in, out_shape = [], []
    for name in names:
        w, m, v = params[name]
        flat_in += [w, m, v]
        out_shape += [jax.ShapeDtypeStruct(w.shape, F32)] * 4
    out_shape.append(jax.ShapeDtypeStruct((SUBLANES, LANES), F32))
    res = pl.pallas_call(
        body,
        name="adamw_small",
        in_specs=[SMEM_SPEC, VMEM_SPEC] + [VMEM_SPEC] * (3 * n),
        out_specs=[VMEM_SPEC] * (4 * n + 1),
        out_shape=out_shape,
    )(j_arr, packed, *flat_in)
    return {name: tuple(res[4 * idx:4 * idx + 4]) for idx, name in enumerate(names)}, res[4 * n][0, 0]


def _units(a):
    rows = a.shape[0]
    return jnp.transpose(a.reshape(rows, N_DEV, LANES), (1, 0, 2))


def kernel(x, meta_tokens, norm_gain, w_in, conv_w, conv_b, w_rg, b_rg, w_ig, b_ig, lru_lambda, ret_norm_gain, w_out, final_norm_gain, loss_target, m_meta_tokens, m_norm_gain, m_w_in, m_conv_w, m_conv_b, m_w_rg, m_b_rg, m_w_ig, m_b_ig, m_lru_lambda, m_ret_norm_gain, m_w_out, m_final_norm_gain, v_meta_tokens, v_norm_gain, v_w_in, v_conv_w, v_conv_b, v_w_rg, v_b_rg, v_w_ig, v_b_ig, v_lru_lambda, v_ret_norm_gain, v_w_out, v_final_norm_gain):
    s_len, d = x.shape[1], x.shape[2]
    d_lru = w_rg.shape[1] * w_rg.shape[2]
    d_ret = ret_norm_gain.shape[1]
    d_qk = HEADS * QK_DIM
    tp = s_len + CHUNK
    tm = TOKEN_TILE
    assert tp % tm == 0 and d_lru == HEADS * LANES and d_ret == HEADS * LANES
    ax, ay, ac = lax.axis_index("x"), lax.axis_index("y"), lax.axis_index("c")
    c_arr = jnp.reshape(ac, (1,)).astype(jnp.int32)
    j_arr = jnp.reshape(2 * ax + ay, (1,)).astype(jnp.int32)

    small = jnp.concatenate([meta_tokens, conv_w[0], jnp.zeros((4, meta_tokens.shape[1]), F32)], axis=0)
    wg, sg = _gather_weights(w_in[0], small)
    cols = sg.shape[2]
    meta_full = jnp.transpose(sg[:, :N_META, :], (1, 0, 2)).reshape(N_META, N_CHIPS * cols)
    cw_full = jnp.transpose(sg[:, N_META:N_META + 4, :], (1, 0, 2)).reshape(4, N_CHIPS * cols)
    cw8 = jnp.concatenate([cw_full, jnp.zeros((4, cw_full.shape[1]), F32)], axis=0)

    half = QK_DIM // 2
    inv = ROPE_BASE ** (-jnp.arange(half, dtype=F32) / half)
    pos = (jnp.arange(tp) - PAD_ROWS).astype(F32)
    ang = pos[:, None] * inv[None, :]
    cos_t = jnp.tile(jnp.cos(ang), (1, LANES // half))
    sign = jnp.where((jnp.arange(LANES) % QK_DIM) < half, -1.0, 1.0).astype(F32)
    sin_t = jnp.tile(jnp.sin(ang), (1, LANES // half)) * sign[None, :]
    tables = _ret_tables()
    gain_f = final_norm_gain.reshape(1, d)

    hp, lx, lg, *qkv, rg, wo4 = _in_proj(x[0], meta_full, norm_gain, wg, cos_t, sin_t, w_out[0], tables[1], tables[2],
                                         tm, d_lru, d_qk, d_ret)
    wo = wo4.reshape(N_CHIPS * wo4.shape[1], wo4.shape[2])
    hl, y_lru, *lru_saved = _lru_fwd(lx, lg, cw8, conv_b, w_rg[0], b_rg, w_ig[0], b_ig, lru_lambda, tm)
    o, y_ret, rprev = _ret_fwd(*qkv, rg, ret_norm_gain, tables, tm)
    dh2, dy_lru, dy_ret, dwo, dgf, loss_acc = _out_proj_loss(y_lru, y_ret, hp, loss_target[0], wo, gain_f, tm)

    g_out = dwo.reshape(N_DEV, dwo.shape[0] // N_DEV, dwo.shape[1])
    (dq, dk, dv, drg, dgain), (r_out,) = _ret_bwd(*qkv, rg, o, rprev, dy_ret, ret_norm_gain, cos_t, sin_t, tables,
                                                 tm, ride=_pair_ride([g_out]))
    q_out = _pair_sum(g_out, r_out, c_arr, 128, "grad_pair_sum_out")
    (dlx, dlg, g_small), (e_out,) = _lru_bwd(lx, lg, hl, dy_lru, lru_saved, cw8, w_rg[0], w_ig[0], lru_lambda,
                                            dgain, dgf, tm, ride=_chip_ride([q_out]))
    f_out = _chip_sum(q_out, e_out, j_arr, 128, "grad_chip_sum_out")
    dparts = [dlx, dlg, dq, dk, dv, drg]
    dwg, (s_out, r_small) = _in_proj_dw(dparts, hp, norm_gain, wg.shape,
                                        ride=_join_rides(_sibling_ride([f_out]), _pair_ride([g_small])))
    g_in = dwg.reshape(N_DEV, dwg.shape[1] // 2, dwg.shape[2])
    q_in = _pair_exchange_sum(g_in, c_arr, 128, "grad_pair_exchange_sum_in")
    q_small = _pair_sum(g_small, r_small, c_arr, UNIT_ROWS, "grad_pair_sum_small")
    (grad_x, dmeta, dg1), (e_in, e_small) = _in_proj_dx(dparts, hp, dh2, norm_gain, wg, s_len, tm,
                                                        ride=_chip_ride([q_in, q_small]))
    f_in = _chip_sum(q_in, e_in, j_arr, 128, "grad_chip_sum_in")
    f_small = _chip_sum(q_small, e_small, j_arr, UNIT_ROWS, "grad_chip_sum_small", loss_part=loss_acc)
    tail = jnp.concatenate([_units(dmeta), _units(dg1), jnp.zeros((N_DEV, TAIL_ROWS - N_META - 1, LANES), F32)],
                           axis=1).reshape(N_DEV * TAIL_ROWS, LANES)
    s_in, o_small = _finish_exchange(f_in, jnp.concatenate([f_small, tail], axis=0))

    res_in = _adamw_big(w_in[0], f_in, s_in, m_w_in[0], v_w_in[0], c_arr, 256, "adamw_w_in")
    res_out = _adamw_big(w_out[0], f_out, s_out, m_w_out[0], v_w_out[0], c_arr, 256, "adamw_w_out")
    small_params = {
        "meta_tokens": (meta_tokens, m_meta_tokens, v_meta_tokens),
        "norm_gain": (norm_gain, m_norm_gain, v_norm_gain),
        "conv_w": (conv_w[0], m_conv_w[0], v_conv_w[0]),
        "conv_b": (conv_b, m_conv_b, v_conv_b),
        "w_rg": (w_rg[0], m_w_rg[0], v_w_rg[0]),
        "b_rg": (b_rg, m_b_rg, v_b_rg),
        "w_ig": (w_ig[0], m_w_ig[0], v_w_ig[0]),
        "b_ig": (b_ig, m_b_ig, v_b_ig),
        "lru_lambda": (lru_lambda, m_lru_lambda, v_lru_lambda),
        "ret_norm_gain": (ret_norm_gain, m_ret_norm_gain, v_ret_norm_gain),
        "final_norm_gain": (gain_f, m_final_norm_gain.reshape(1, d), v_final_norm_gain.reshape(1, d)),
    }
    res, loss = _adamw_small(j_arr, o_small, small_params)
    res["w_in"] = tuple(res_in)
    res["w_out"] = tuple(res_out)

    order = ["meta_tokens", "norm_gain", "w_in", "conv_w", "conv_b", "w_rg", "b_rg", "w_ig", "b_ig", "lru_lambda",
             "ret_norm_gain", "w_out", "final_norm_gain"]
    shapes = {"w_in": w_in.shape, "conv_w": conv_w.shape, "w_rg": w_rg.shape, "w_ig": w_ig.shape,
              "w_out": w_out.shape, "final_norm_gain": final_norm_gain.shape}
    outs = [loss, grad_x.reshape(x.shape)]
    for kind in range(4):
        for name in order:
            a = res[name][kind]
            outs.append(a.reshape(shapes[name]) if name in shapes else a)
    return tuple(outs)
```

```python
import functools

import jax
import jax.numpy as jnp
from jax import lax
from jax.experimental import pallas as pl
from jax.experimental.pallas import tpu as pltpu

F32 = jnp.float32
BF16 = jnp.bfloat16

N_META = 16
CHUNK = 128
PAD_ROWS = CHUNK - N_META
HEADS = 8
QK_DIM = 64
LANES = 128
SUBLANES = 8
LRU_C = 8.0
EPS = 1e-6
ROPE_BASE = 10000.0
ADAM_LR = 0.001
ADAM_B1 = 0.9
ADAM_B2 = 0.999
ADAM_EPS = 1e-08
ADAM_WD = 0.01
ADAM_STEP = 10
N_CHIPS = 4
N_DEV = 8
TOKEN_TILE = 384
REDUCE_TILE = 256
VMEM_LIMIT = 58 * 1024 * 1024
MESH = pl.DeviceIdType.MESH

VMEM_SPEC = pl.BlockSpec(memory_space=pltpu.VMEM)
SMEM_SPEC = pl.BlockSpec(memory_space=pltpu.SMEM)
ANY_SPEC = pl.BlockSpec(memory_space=pl.ANY)

ROW_WR, ROW_WI, ROW_META, ROW_CONV, ROW_VEC, UNIT_ROWS = 0, 128, 256, 272, 276, 288
VEC_NAMES = ["norm_gain", "conv_b", "b_rg", "b_ig", "lru_lambda", "ret_norm_gain", "final_norm_gain"]
N_VEC = len(VEC_NAMES)
ROW_LOSS = ROW_VEC + N_VEC
TAIL_ROWS = 24


def _dot(a, b):
    return jnp.dot(a, b, preferred_element_type=F32)


def _dot_nt(a, b):
    return lax.dot_general(a, b, (((1,), (1,)), ((), ())), preferred_element_type=F32)


def _dot_tn(a, b):
    return lax.dot_general(a, b, (((0,), (0,)), ((), ())), preferred_element_type=F32)


def _sigmoid(x):
    return 0.5 * jnp.tanh(0.5 * x) + 0.5


def _shift_down(x, prev8, s):
    rolled = pltpu.roll(x, s, 0)
    rows = lax.broadcasted_iota(jnp.int32, (SUBLANES, x.shape[1]), 0)
    top = jnp.where(rows < s, pltpu.roll(prev8, s, 0), rolled[0:SUBLANES])
    return jnp.concatenate([top, rolled[SUBLANES:]], axis=0)


def _shift_up(x, next8, s):
    n = x.shape[0]
    rolled = pltpu.roll(x, n - s, 0)
    rows = lax.broadcasted_iota(jnp.int32, (SUBLANES, x.shape[1]), 0)
    bot = jnp.where(rows >= SUBLANES - s, pltpu.roll(next8, SUBLANES - s, 0), rolled[n - SUBLANES:n])
    return jnp.concatenate([rolled[:n - SUBLANES], bot], axis=0)


def _rot_partner(t):
    w = t.shape[1]
    lane = lax.broadcasted_iota(jnp.int32, t.shape, 1)
    first = (lane % QK_DIM) < (QK_DIM // 2)
    return jnp.where(first, pltpu.roll(t, w - QK_DIM // 2, 1), pltpu.roll(t, QK_DIM // 2, 1))


def _tile_lanes(t, reps):
    return jnp.concatenate([t] * reps, axis=1)


class _Ride:
    def __init__(self, srcs, dst_shapes, n_copies, make, to_sibling=False, to_chips=False):
        self.srcs, self.dst_shapes, self.n_copies, self.make = list(srcs), list(dst_shapes), n_copies, make
        self.to_sibling, self.to_chips = to_sibling, to_chips

    def peers(self):
        x, y, c, chips = _position()
        return ([(x, y, 1 - c)] if self.to_sibling else []) + ([(cx, cy, c) for cx, cy in chips] if self.to_chips else [])


def _join_rides(a, b):
    def make(src, dst, send_sems, recv_sems, base):
        na, da = len(a.srcs), len(a.dst_shapes)
        return (a.make(src[:na], dst[:da], send_sems, recv_sems, base)
                + b.make(src[na:], dst[da:], send_sems, recv_sems, base + a.n_copies))

    return _Ride(a.srcs + b.srcs, a.dst_shapes + b.dst_shapes, a.n_copies + b.n_copies, make,
                 a.to_sibling or b.to_sibling, a.to_chips or b.to_chips)


def _position():
    x, y, c = lax.axis_index("x"), lax.axis_index("y"), lax.axis_index("c")
    return x, y, c, [(1 - x, y), (x, 1 - y), (1 - x, 1 - y)]


def _peer_barrier(peers):
    barrier = pltpu.get_barrier_semaphore()
    for peer in peers:
        pl.semaphore_signal(barrier, inc=1, device_id=peer, device_id_type=MESH)
    pl.semaphore_wait(barrier, len(peers))


def _remote(src, dst, send_sems, recv_sems, k, to):
    return pltpu.make_async_remote_copy(src_ref=src, dst_ref=dst, send_sem=send_sems.at[k], recv_sem=recv_sems.at[k],
                                        device_id=to, device_id_type=MESH)


def _pair_ride(bufs):
    def make(src, dst, send_sems, recv_sems, base):
        x, y, c, _ = _position()
        return [_remote(src[b].at[2 * jj + 1 - c], dst[b].at[jj], send_sems, recv_sems, base + b * N_CHIPS + jj,
                        (x, y, 1 - c)) for b in range(len(bufs)) for jj in range(N_CHIPS)]

    shapes = [jax.ShapeDtypeStruct((N_CHIPS,) + b.shape[1:], b.dtype) for b in bufs]
    return _Ride(bufs, shapes, N_CHIPS * len(bufs), make, to_sibling=True)


def _chip_ride(bufs):
    def make(src, dst, send_sems, recv_sems, base):
        x, y, c, chips = _position()
        return [_remote(src[b].at[2 * cx + cy], dst[b].at[2 * x + y], send_sems, recv_sems, base + b * 3 + k,
                        (cx, cy, c)) for b in range(len(bufs)) for k, (cx, cy) in enumerate(chips)]

    shapes = [jax.ShapeDtypeStruct(b.shape, b.dtype) for b in bufs]
    return _Ride(bufs, shapes, 3 * len(bufs), make, to_chips=True)


def _sibling_ride(bufs):
    def make(src, dst, send_sems, recv_sems, base):
        x, y, c, _ = _position()
        return [_remote(src[b], dst[b], send_sems, recv_sems, base + b, (x, y, 1 - c)) for b in range(len(bufs))]

    shapes = [jax.ShapeDtypeStruct(b.shape, b.dtype) for b in bufs]
    return _Ride(bufs, shapes, len(bufs), make, to_sibling=True)


def _hosted_call(body, ride, n_steps, *, name, in_specs, out_specs, out_shape, scratch_shapes, args, barrier_id=None):
    params = pltpu.CompilerParams(dimension_semantics=("arbitrary",), vmem_limit_bytes=VMEM_LIMIT,
                                  collective_id=barrier_id if ride is not None else None)
    if ride is None:
        res = pl.pallas_call(body, name=name, grid=(n_steps,), in_specs=list(in_specs), out_specs=list(out_specs),
                             out_shape=list(out_shape), scratch_shapes=list(scratch_shapes),
                             compiler_params=params)(*args)
        return list(res), []
    sizes = [len(in_specs), len(ride.srcs), len(out_specs), len(ride.dst_shapes), len(scratch_shapes), 2]

    def hosted(*refs):
        groups, pos = [], 0
        for n in sizes:
            groups.append(refs[pos:pos + n])
            pos += n
        ins, rin, outs, rout, scr, (send_sems, recv_sems) = groups
        i = pl.program_id(0)

        @pl.when(i == 0)
        def _():
            if barrier_id is not None:
                _peer_barrier(ride.peers())
            for cp in ride.make(rin, rout, send_sems, recv_sems, 0):
                cp.start()

        body(*ins, *outs, *scr)

        @pl.when(i == n_steps - 1)
        def _():
            for cp in ride.make(rin, rout, send_sems, recv_sems, 0):
                cp.wait()

    n_out = len(out_specs)
    res = pl.pallas_call(
        hosted,
        name=name,
        grid=(n_steps,),
        in_specs=list(in_specs) + [ANY_SPEC] * len(ride.srcs),
        out_specs=list(out_specs) + [ANY_SPEC] * len(ride.dst_shapes),
        out_shape=list(out_shape) + ride.dst_shapes,
        scratch_shapes=list(scratch_shapes) + [pltpu.SemaphoreType.DMA((ride.n_copies,)),
                                               pltpu.SemaphoreType.DMA((ride.n_copies,))],
        compiler_params=params,
    )(*args, *ride.srcs)
    return list(res[:n_out]), list(res[n_out:])


def _gather_weights(w_in, small):
    r_in, c_in = w_in.shape
    h_in = r_in // 2
    q_in = h_in // 2

    def body(win_ref, small_ref, wg_ref, sg_ref, send_sems, recv_sems):
        x, y, c, chips = _position()
        j = 2 * x + y
        sibling = (x, y, 1 - c)
        xn, yn, dg = chips
        jx, jy, jd = (2 * cx + cy for cx, cy in chips)

        def half(jj, cc):
            return wg_ref.at[jj, pl.ds(cc * h_in, h_in), :]

        def quarter(jj, qq):
            return wg_ref.at[jj, pl.ds(c * h_in + qq * q_in, q_in), :]

        def copy(k, ref, to):
            return _remote(ref, ref, send_sems, recv_sems, k, to)

        def cast_rows(start, rows):
            start = pl.multiple_of(start, q_in)
            wg_ref[j, pl.ds(start, rows), :] = win_ref[pl.ds(start, rows), :].astype(BF16)

        first = [copy(0, quarter(j, 0), (*xn, c)), copy(2, quarter(j, 1), (*yn, c)),
                 copy(1, quarter(j, 1), (*xn, c)), copy(3, quarter(j, 0), (*yn, c))]
        sg_ref[j] = small_ref[...]
        cast_rows(c * h_in, q_in)
        _peer_barrier([sibling] + [(cx, cy, c) for cx, cy in chips])
        first[0].start()
        cast_rows(c * h_in + q_in, q_in)
        for cp in first[1:]:
            cp.start()
        small_copies = [copy(9 + k, sg_ref.at[j], (cx, cy, c)) for k, (cx, cy) in enumerate(chips)]
        for cp in small_copies:
            cp.start()
        first += small_copies
        cast_rows((1 - c) * h_in, h_in)
        copy(0, quarter(jx, 0), sibling).wait_recv()
        along_y = copy(4, quarter(jx, 0), (*yn, c))
        along_y.start()
        copy(2, quarter(jy, 1), sibling).wait_recv()
        along_x = copy(5, quarter(jy, 1), (*xn, c))
        along_x.start()
        copy(1, quarter(jx, 1), sibling).wait_recv()
        to_sib = [copy(6, half(jx, c), sibling)]
        to_sib[-1].start()
        copy(3, quarter(jy, 0), sibling).wait_recv()
        to_sib.append(copy(7, half(jy, c), sibling))
        to_sib[-1].start()
        copy(4, quarter(jd, 0), sibling).wait_recv()
        copy(5, quarter(jd, 1), sibling).wait_recv()
        to_sib.append(copy(8, half(jd, c), sibling))
        to_sib[-1].start()
        for k, jk in enumerate((jx, jy, jd)):
            copy(6 + k, half(jk, 1 - c), sibling).wait_recv()
            copy(9 + k, sg_ref.at[jk], sibling).wait_recv()
        for cp in first + [along_y, along_x] + to_sib:
            cp.wait_send()

    return pl.pallas_call(
        body,
        name="gather_weights",
        out_shape=(jax.ShapeDtypeStruct((N_CHIPS, r_in, c_in), BF16),
                   jax.ShapeDtypeStruct((N_CHIPS,) + small.shape, F32)),
        in_specs=[VMEM_SPEC, VMEM_SPEC],
        out_specs=(VMEM_SPEC, VMEM_SPEC),
        scratch_shapes=[pltpu.SemaphoreType.DMA((12,)), pltpu.SemaphoreType.DMA((12,))],
        compiler_params=pltpu.CompilerParams(vmem_limit_bytes=VMEM_LIMIT, collective_id=6),
    )(w_in, small)


def _proj_segments(d_lru, d_qk, d_ret, chunk_w):
    widths = [d_lru, d_lru, d_qk, d_qk, d_ret, d_ret]
    segs, col = [], 0
    for w in widths:
        parts, off = [], 0
        while off < w:
            jj, inner = divmod(col + off, chunk_w)
            take = min(w - off, chunk_w - inner)
            parts.append((jj, inner, off, take))
            off += take
        segs.append(parts)
        col += w
    return segs


def _in_proj(x2, meta, gain, wg, cos_t, sin_t, w_out, qdec, kdec, tm, d_lru, d_qk, d_ret):
    s_len, d = x2.shape
    tp = s_len + CHUNK
    nt, nb = tp // tm, tm // CHUNK
    segs = _proj_segments(d_lru, d_qk, d_ret, wg.shape[2])
    outs = [(d, F32), (d_lru, F32), (d_lru, F32)] + [(d_qk, BF16)] * 4 + [(d_ret, BF16), (d_ret, F32)]
    r_out, c_out = w_out.shape
    h_out = r_out // 2
    fwd_step = min(6, nt - 1)

    def gather_w_out(i, wout_ref, wo_ref, wob, send_sems, recv_sems, local_sem):
        x, y, c, chips = _position()
        j = 2 * x + y
        sibling = (x, y, 1 - c)

        def half(jj, cc):
            return wo_ref.at[jj, pl.ds(cc * h_out, h_out), :]

        local = pltpu.make_async_copy(wob, wo_ref.at[j], local_sem)
        first = [_remote(wob.at[pl.ds(c * h_out, h_out), :], half(j, c), send_sems, recv_sems, k, (cx, cy, c))
                 for k, (cx, cy) in enumerate(chips)]
        passed = [_remote(half(2 * cx + cy, c), half(2 * cx + cy, c), send_sems, recv_sems, 3 + k, sibling)
                  for k, (cx, cy) in enumerate(chips)]

        @pl.when(i == 0)
        def _():
            wob[...] = wout_ref[...].astype(BF16)
            _peer_barrier([sibling] + [(cx, cy, c) for cx, cy in chips])
            local.start()
            for cp in first:
                cp.start()

        @pl.when(i == fwd_step)
        def _():
            for k, (cx, cy) in enumerate(chips):
                _remote(half(2 * cx + cy, c), half(2 * cx + cy, c), send_sems, recv_sems, k, sibling).wait_recv()
                passed[k].start()

        @pl.when(i == nt - 1)
        def _():
            for k, (cx, cy) in enumerate(chips):
                jk = 2 * cx + cy
                _remote(half(jk, 1 - c), half(jk, 1 - c), send_sems, recv_sems, 3 + k, sibling).wait_recv()
            for cp in first + passed:
                cp.wait_send()
            local.wait()

    def body(*refs):
        xb = refs[:nb]
        meta_ref, g_ref, w_ref, cos_ref, sin_ref, wout_ref, qdec_ref, kdec_ref = refs[nb:nb + 8]
        hp_ref, lx_ref, lg_ref, qb_ref, kb_ref, qd_ref, kd_ref, vb_ref, rg_ref = refs[nb + 8:nb + 17]
        wo_ref, q_s, k_s, wob, send_sems, recv_sems, local_sem = refs[nb + 17:]
        i = pl.program_id(0)
        gather_w_out(i, wout_ref, wo_ref, wob, send_sems, recv_sems, local_sem)
        blocks = [r[...] for r in xb]
        head = jnp.concatenate([jnp.zeros((PAD_ROWS, d), F32), meta_ref[...]], axis=0)
        blocks[0] = jnp.where(i == 0, head, blocks[0])
        h = jnp.concatenate(blocks, axis=0)
        hp_ref[...] = h
        rinv = lax.rsqrt(jnp.mean(h * h, axis=-1, keepdims=True) + EPS)
        u = ((h * rinv) * g_ref[...]).astype(BF16)
        for out_ref, parts in zip([lx_ref, lg_ref, q_s, k_s, vb_ref, rg_ref], segs):
            for jj, inner, off, take in parts:
                out_ref[:, off:off + take] = _dot(u, w_ref[jj, :, inner:inner + take]).astype(out_ref.dtype)
        cos = _tile_lanes(cos_ref[...], d_qk // LANES)
        sin = _tile_lanes(sin_ref[...], d_qk // LANES)
        q = q_s[...]
        q = q * cos + _rot_partner(q) * sin
        k = k_s[...]
        k = (k * cos + _rot_partner(k) * sin) * (QK_DIM ** -0.5)
        qb_ref[...] = q.astype(BF16)
        kb_ref[...] = k.astype(BF16)
        qd_ref[...] = (q * jnp.concatenate([qdec_ref[...]] * nb, axis=0)).astype(BF16)
        kd_ref[...] = (k * jnp.concatenate([kdec_ref[...]] * nb, axis=0)).astype(BF16)

    x_specs = [pl.BlockSpec((CHUNK, d), functools.partial(lambda i, b: (jnp.maximum(i * nb + b - 1, 0), 0), b=b))
               for b in range(nb)]
    tile = lambda w: pl.BlockSpec((tm, w), lambda i: (i, 0))
    return pl.pallas_call(
        body,
        name="in_proj",
        grid=(nt,),
        in_specs=x_specs + [pl.BlockSpec(meta.shape, lambda i: (0, 0)),
                            pl.BlockSpec(gain.shape, lambda i: (0, 0)),
                            pl.BlockSpec(wg.shape, lambda i: (0, 0, 0)),
                            tile(LANES), tile(LANES),
                            pl.BlockSpec(w_out.shape, lambda i: (0, 0)),
                            pl.BlockSpec(qdec.shape, lambda i: (0, 0)), pl.BlockSpec(kdec.shape, lambda i: (0, 0))],
        out_specs=[tile(w) for w, _ in outs] + [ANY_SPEC],
        out_shape=[jax.ShapeDtypeStruct((tp, w), dt) for w, dt in outs]
                  + [jax.ShapeDtypeStruct((N_CHIPS, r_out, c_out), BF16)],
        scratch_shapes=[pltpu.VMEM((tm, d_qk), F32), pltpu.VMEM((tm, d_qk), F32),
                        pltpu.VMEM((r_out, c_out), BF16), pltpu.SemaphoreType.DMA((6,)),
                        pltpu.SemaphoreType.DMA((6,)), pltpu.SemaphoreType.DMA],
        compiler_params=pltpu.CompilerParams(dimension_semantics=("arbitrary",), vmem_limit_bytes=VMEM_LIMIT,
                                             collective_id=7),
    )(*([x2] * nb), meta, gain, wg, cos_t, sin_t, w_out, qdec, kdec)


def _segment_scan(a3, u3, out3, p3, carry, tm, reverse):
    groups = a3.shape[0]
    seg = tm // SUBLANES

    def step(j, state):
        hs, ps = state
        rows = pl.ds((seg - 1 - j) if reverse else j, SUBLANES, stride=seg)
        new_h, new_p = [], []
        for g in range(groups):
            a = a3[g, rows, :]
            h = a * hs[g] + u3[g, rows, :]
            p = ps[g] * a
            out3[g, rows, :] = h
            p3[g, rows, :] = p
            new_h.append(h)
            new_p.append(p)
        return tuple(new_h), tuple(new_p)

    zeros = tuple(jnp.zeros((SUBLANES, LANES), F32) for _ in range(groups))
    ones = tuple(jnp.ones((SUBLANES, LANES), F32) for _ in range(groups))
    lax.fori_loop(0, seg, step, (zeros, ones))
    carries = [carry[:, g * LANES:(g + 1) * LANES] for g in range(groups)]
    for s in (reversed(range(SUBLANES)) if reverse else range(SUBLANES)):
        rows = slice(s * seg, (s + 1) * seg)
        edge = s * seg if reverse else (s + 1) * seg - 1
        for g in range(groups):
            out3[g, rows, :] = out3[g, rows, :] + p3[g, rows, :] * carries[g]
            carries[g] = out3[g, edge:edge + 1, :]
    return jnp.concatenate(carries, axis=1)


def _softplus_neg(lam):
    z = -lam
    e = jnp.exp(-jnp.abs(z))
    e1 = 1.0 + e
    log1p_e = jnp.where(e1 == 1.0, e, jnp.log(e1) * (e / (e1 - 1.0)))
    return jnp.maximum(z, 0.0) + log1p_e


def _lru_fwd(lx, lg, cw, cb, wr, br, wi, bi, lam, tm):
    tp, w = lx.shape
    nt = tp // tm
    per8 = tm // SUBLANES
    n_heads = wr.shape[0]

    def body(lx_ref, lxp_ref, lg_ref, cw_ref, cb_ref, wr_ref, br_ref, wi_ref, bi_ref, lam_ref,
             hl_ref, y_ref, xc_ref, r_ref, ig_ref, a_ref, beta_ref, w4_ref, a_s, u_s, h_s, p_s, carry):
        i = pl.program_id(0)

        @pl.when(i == 0)
        def _():
            carry[...] = jnp.zeros_like(carry)

        sp = _softplus_neg(lam_ref[...])
        row = lax.broadcasted_iota(jnp.int32, (tm, 1), 0) + i * tm
        for hd in range(n_heads):
            hs = slice(hd * LANES, (hd + 1) * LANES)
            lxv = lx_ref[:, hs]
            prev8 = jnp.where(i == 0, 0.0, lxp_ref[:, hs])
            xc = cb_ref[:, hs] + _shift_down(lxv, prev8, 3) * cw_ref[0:1, hs]
            xc = xc + _shift_down(lxv, prev8, 2) * cw_ref[1:2, hs]
            xc = xc + _shift_down(lxv, prev8, 1) * cw_ref[2:3, hs]
            xc = xc + lxv * cw_ref[3:4, hs]
            xc_ref[:, hs] = xc
            xh = xc.astype(BF16)
            r = _sigmoid(_dot(xh, wr_ref[hd].astype(BF16)) + br_ref[:, hs])
            ig = _sigmoid(_dot(xh, wi_ref[hd].astype(BF16)) + bi_ref[:, hs])
            r_ref[:, hs] = r
            ig_ref[:, hs] = ig
            log_a = (-LRU_C * r) * sp[:, hs]
            a = jnp.exp(log_a)
            a_ref[:, hs] = a
            a2 = a * a
            beta2 = jnp.maximum((1.0 + a2) * jnp.tanh(-log_a), 1e-37)
            rsb = lax.rsqrt(beta2)
            beta = beta2 * rsb
            beta_ref[:, hs] = beta
            w4_ref[:, hs] = a2 * rsb
            a_s[hd] = a
            u_s[hd] = jnp.where(row >= PAD_ROWS, beta * ig * xc, 0.0)
        carry[0:1, :] = _segment_scan(a_s, u_s, h_s, p_s, carry[0:1, :], tm, reverse=False)
        for hd in range(n_heads):
            hs = slice(hd * LANES, (hd + 1) * LANES)
            hl = h_s[hd]
            hl_ref[:, hs] = hl
            g = lg_ref[:, hs]
            y_ref[:, hs] = (hl * (g * _sigmoid(g))).astype(BF16)

    tile = pl.BlockSpec((tm, w), lambda i: (i, 0))
    prev = pl.BlockSpec((SUBLANES, w), lambda i: (jnp.maximum(i * per8 - 1, 0), 0))
    vec = pl.BlockSpec((1, w), lambda i: (0, 0))
    mat = pl.BlockSpec(wr.shape, lambda i: (0, 0, 0))
    f32_out = jax.ShapeDtypeStruct((tp, w), F32)
    return pl.pallas_call(
        body,
        name="lru_fwd",
        grid=(nt,),
        in_specs=[tile, prev, tile, pl.BlockSpec(cw.shape, lambda i: (0, 0)), vec, mat, vec, mat, vec, vec],
        out_specs=[tile] * 8,
        out_shape=[f32_out, jax.ShapeDtypeStruct((tp, w), BF16)] + [f32_out] * 6,
        scratch_shapes=[pltpu.VMEM((w // LANES, tm, LANES), F32)] * 4 + [pltpu.VMEM((SUBLANES, w), F32)],
        compiler_params=pltpu.CompilerParams(dimension_semantics=("arbitrary",), vmem_limit_bytes=VMEM_LIMIT),
    )(lx, lx, lg, cw, cb, wr, br, wi, bi, lam)


def _ret_tables():
    log_g = jnp.log1p(-jnp.exp2(-5.0 - jnp.arange(HEADS, dtype=F32)))
    idx = jnp.arange(CHUNK, dtype=F32)
    diff = idx[:, None] - idx[None, :]
    dmask = jnp.where(diff[None] >= 0.0, jnp.exp(jnp.maximum(diff, 0.0)[None] * log_g[:, None, None]), 0.0)
    kdec = jnp.repeat(jnp.exp((CHUNK - 1.0 - idx)[:, None] * log_g[None, :]), QK_DIM, axis=1)
    qdec = jnp.repeat(jnp.exp((idx + 1.0)[:, None] * log_g[None, :]), QK_DIM, axis=1)
    g_chunk = jnp.exp(CHUNK * log_g)
    g_rows = jnp.repeat(g_chunk, QK_DIM).reshape(HEADS // 2, 2 * QK_DIM, 1)
    g_state = jnp.broadcast_to(g_rows, (HEADS // 2, 2 * QK_DIM, 2 * LANES))
    r_head = jnp.arange(2 * QK_DIM)[:, None] // QK_DIM
    c_head = jnp.arange(2 * LANES)[None, :] // LANES
    block_diag = (r_head == c_head).astype(F32)
    return dmask, qdec, kdec, g_state, block_diag


def _head_norm(o_h):
    mu = jnp.mean(o_h, axis=-1, keepdims=True)
    oc = o_h - mu
    var = jnp.mean(oc * oc, axis=-1, keepdims=True)
    rstd = lax.rsqrt(var + EPS)
    return oc * rstd, rstd


def _ret_fwd(qb, kb, qd, kd, vb, rg, gain, tables, tm):
    tp, d_qk = qb.shape
    d_ret = vb.shape[1]
    n_ch = tp // CHUNK
    cps = tm // CHUNK
    n_pairs = HEADS // 2
    dmask, _, _, g_state, block_diag = tables

    def body(q_ref, k_ref, qd_ref, kd_ref, v_ref, rg_ref, gain_ref, dm_ref, gs_ref, bd_ref,
             o_ref, y_ref, rp_ref, state):
        n = pl.program_id(0)

        @pl.when(n == 0)
        def _():
            state[...] = jnp.zeros_like(state)

        lane = lax.broadcasted_iota(jnp.int32, (CHUNK, LANES), 1)
        for ci in range(cps):
            rs = slice(ci * CHUNK, (ci + 1) * CHUNK)
            for p in range(n_pairs):
                qs = slice(p * LANES, (p + 1) * LANES)
                vs = slice(p * 2 * LANES, (p + 1) * 2 * LANES)
                qp, kb = q_ref[rs, qs], k_ref[rs, qs]
                vb = v_ref[rs, vs]
                qd, kd = qd_ref[rs, qs], kd_ref[rs, qs]
                st = state[p]
                st_b = st.astype(BF16)
                rp_ref[ci, p] = st_b
                cross = _dot(qd, st_b)
                for e in range(2):
                    hd = 2 * p + e
                    hs = slice(hd * LANES, (hd + 1) * LANES)
                    es = slice(e * LANES, (e + 1) * LANES)
                    qm = jnp.where((lane // QK_DIM) == e, qp, jnp.zeros_like(qp))
                    s = _dot_nt(qm, kb) * dm_ref[hd]
                    o_h = _dot(s.astype(BF16), vb[:, es]) + cross[:, es]
                    o_ref[rs, hs] = o_h
                    xhat, _ = _head_norm(o_h)
                    g = rg_ref[rs, hs]
                    y_ref[rs, hs] = ((xhat * gain_ref[:, hs]) * (g * _sigmoid(g))).astype(BF16)
                state[p] = gs_ref[p] * st + bd_ref[...] * _dot_tn(kd, vb)

    ch = lambda w: pl.BlockSpec((tm, w), lambda n: (n, 0))
    const2 = lambda a: pl.BlockSpec(a.shape, lambda n: (0, 0))
    const3 = lambda a: pl.BlockSpec(a.shape, lambda n: (0, 0, 0))
    return pl.pallas_call(
        body,
        name="ret_fwd",
        grid=(n_ch // cps,),
        in_specs=[ch(d_qk)] * 4 + [ch(d_ret), ch(d_ret), const2(gain), const3(dmask), const3(g_state),
                                   const2(block_diag)],
        out_specs=[ch(d_ret), ch(d_ret),
                   pl.BlockSpec((cps, n_pairs, 2 * QK_DIM, 2 * LANES), lambda n: (n, 0, 0, 0))],
        out_shape=[jax.ShapeDtypeStruct((tp, d_ret), F32), jax.ShapeDtypeStruct((tp, d_ret), BF16),
                   jax.ShapeDtypeStruct((n_ch, n_pairs, 2 * QK_DIM, 2 * LANES), BF16)],
        scratch_shapes=[pltpu.VMEM((n_pairs, 2 * QK_DIM, 2 * LANES), F32)],
        compiler_params=pltpu.CompilerParams(dimension_semantics=("arbitrary",), vmem_limit_bytes=VMEM_LIMIT),
    )(qb, kb, qd, kd, vb, rg, gain, dmask, g_state, block_diag)


def _out_proj_loss(y_lru, y_ret, hp, tgt, wo, gain_f, tm):
    tp, d = hp.shape
    w_lru = y_lru.shape[1]
    w_mix = wo.shape[0]
    nt, nb = tp // tm, tm // CHUNK

    def body(*refs):
        yl_ref, yr_ref, hp_ref = refs[:3]
        tb = refs[3:3 + nb]
        wo_ref, gf_ref = refs[3 + nb:5 + nb]
        dh2_ref, dyl_ref, dyr_ref, dwo_ref, dgf_ref, loss_ref = refs[5 + nb:]
        i = pl.program_id(0)

        @pl.when(i == 0)
        def _():
            dwo_ref[...] = jnp.zeros_like(dwo_ref)
            dgf_ref[...] = jnp.zeros_like(dgf_ref)
            loss_ref[...] = jnp.zeros_like(loss_ref)

        yl, yr = yl_ref[...], yr_ref[...]
        h2 = hp_ref[...] + _dot(yl, wo_ref[0:w_lru, :]) + _dot(yr, wo_ref[w_lru:w_mix, :])
        rinv = lax.rsqrt(jnp.mean(h2 * h2, axis=-1, keepdims=True) + EPS)
        nrm = h2 * rinv
        gf = gf_ref[...]
        tgt_v = jnp.concatenate([r[...] for r in tb], axis=0)
        row = lax.broadcasted_iota(jnp.int32, (tm, 1), 0) + i * tm
        err = jnp.where(row >= CHUNK, nrm * gf - tgt_v, 0.0)
        loss_ref[...] += 0.5 * jnp.sum(jnp.mean(err * err, axis=-1, keepdims=True))
        dout = err * (1.0 / d)
        dgf_ref[...] += jnp.sum(dout * nrm, axis=0, keepdims=True)
        dn = dout * gf
        dh2 = rinv * (dn - nrm * jnp.mean(dn * nrm, axis=-1, keepdims=True))
        dh2_ref[...] = dh2
        dh2b = dh2.astype(BF16)
        dyl_ref[...] = _dot_nt(dh2b, wo_ref[0:w_lru, :])
        dyr_ref[...] = _dot_nt(dh2b, wo_ref[w_lru:w_mix, :])
        dwo_ref[0:w_lru, :] += _dot_tn(yl, dh2b)
        dwo_ref[w_lru:w_mix, :] += _dot_tn(yr, dh2b)

    tile = lambda w: pl.BlockSpec((tm, w), lambda i: (i, 0))
    t_specs = [pl.BlockSpec((CHUNK, d), functools.partial(lambda i, b: (jnp.maximum(i * nb + b - 1, 0), 0), b=b))
               for b in range(nb)]
    return pl.pallas_call(
        body,
        name="out_proj_loss",
        grid=(nt,),
        in_specs=[tile(w_lru), tile(w_mix - w_lru), tile(d)] + t_specs +
                 [pl.BlockSpec(wo.shape, lambda i: (0, 0)), pl.BlockSpec(gain_f.shape, lambda i: (0, 0))],
        out_specs=[tile(d), tile(w_lru), tile(w_mix - w_lru), pl.BlockSpec(wo.shape, lambda i: (0, 0)),
                   pl.BlockSpec((1, d), lambda i: (0, 0)), pl.BlockSpec((SUBLANES, LANES), lambda i: (0, 0))],
        out_shape=[jax.ShapeDtypeStruct((tp, d), F32), jax.ShapeDtypeStruct((tp, w_lru), F32),
                   jax.ShapeDtypeStruct((tp, w_mix - w_lru), F32), jax.ShapeDtypeStruct(wo.shape, F32),
                   jax.ShapeDtypeStruct((1, d), F32), jax.ShapeDtypeStruct((SUBLANES, LANES), F32)],
        compiler_params=pltpu.CompilerParams(dimension_semantics=("arbitrary",), vmem_limit_bytes=VMEM_LIMIT),
    )(y_lru, y_ret, hp, *([tgt] * nb), wo, gain_f)


def _ret_bwd(qb, kb, qd, kd, vb, rg, o, rprev, dy, gain, cos_t, sin_t, tables, tm, ride=None):
    tp, d_qk = qb.shape
    d_ret = vb.shape[1]
    n_ch = tp // CHUNK
    cps = tm // CHUNK
    n_pairs = HEADS // 2
    dmask, qdec, kdec, g_state, block_diag = tables

    dmask_t = jnp.swapaxes(dmask, 1, 2)

    def body(q_ref, k_ref, qdb_ref, kdb_ref, v_ref, rg_ref, o_ref, rp_ref, dy_ref, gain_ref, cos_ref, sin_ref,
             dm_ref, dmt_ref, qd_ref, kd_ref, gs_ref, bd_ref, dq_ref, dk_ref, dv_ref, drg_ref, dgain_ref, dstate):
        n = pl.program_id(0)

        @pl.when(n == 0)
        def _():
            dstate[...] = jnp.zeros_like(dstate)
            dgain_ref[...] = jnp.zeros_like(dgain_ref)

        lane = lax.broadcasted_iota(jnp.int32, (CHUNK, LANES), 1)
        for ci in reversed(range(cps)):
            rs = slice(ci * CHUNK, (ci + 1) * CHUNK)
            dq_parts, dk_parts = [], []
            for p in range(n_pairs):
                qs = slice(p * LANES, (p + 1) * LANES)
                vs = slice(p * 2 * LANES, (p + 1) * 2 * LANES)
                do_parts = []
                for e in range(2):
                    hd = 2 * p + e
                    hs = slice(hd * LANES, (hd + 1) * LANES)
                    xhat, rstd = _head_norm(o_ref[rs, hs])
                    g = rg_ref[rs, hs]
                    sg = _sigmoid(g)
                    dyh = dy_ref[rs, hs]
                    gn = gain_ref[:, hs]
                    d_on = dyh * (g * sg)
                    drg_ref[rs, hs] = (dyh * (xhat * gn) * (sg * (1.0 + g * (1.0 - sg)))).astype(BF16)
                    dgain_ref[:, hs] += jnp.sum(d_on * xhat, axis=0, keepdims=True)
                    dxh = d_on * gn
                    do_parts.append(rstd * (dxh - jnp.mean(dxh, axis=-1, keepdims=True)
                                            - xhat * jnp.mean(dxh * xhat, axis=-1, keepdims=True)))
                do_b = jnp.concatenate(do_parts, axis=1).astype(BF16)
                qp, kb = q_ref[rs, qs], k_ref[rs, qs]
                vb = v_ref[rs, vs]
                qd, kd = qdb_ref[rs, qs], kdb_ref[rs, qs]
                dst = dstate[p]
                dst_b = dst.astype(BF16)
                dqp = _dot_nt(do_b, rp_ref[ci, p]) * qd_ref[:, qs]
                dkp = _dot_nt(vb, dst_b) * kd_ref[:, qs]
                dvp = _dot(kd, dst_b)
                dv_parts = []
                for e in range(2):
                    hd = 2 * p + e
                    es = slice(e * LANES, (e + 1) * LANES)
                    mine = (lane // QK_DIM) == e
                    qm = jnp.where(mine, qp, jnp.zeros_like(qp))
                    km = jnp.where(mine, kb, jnp.zeros_like(kb))
                    ds = (_dot_nt(do_b[:, es], vb[:, es]) * dm_ref[hd]).astype(BF16)
                    s_t = (_dot_nt(kb, qm) * dmt_ref[hd]).astype(BF16)
                    ds_t = (_dot_nt(vb[:, es], do_b[:, es]) * dmt_ref[hd]).astype(BF16)
                    dv_parts.append(dvp[:, es] + _dot(s_t, do_b[:, es]))
                    dqp = dqp + _dot(ds, km)
                    dkp = dkp + _dot(ds_t, qm)
                dv_ref[rs, vs] = jnp.concatenate(dv_parts, axis=1).astype(BF16)
                dstate[p] = gs_ref[p] * dst + bd_ref[...] * _dot_tn(qd, do_b)
                dq_parts.append(dqp)
                dk_parts.append(dkp)
            cos = _tile_lanes(cos_ref[rs, :], d_qk // LANES)
            sin = _tile_lanes(sin_ref[rs, :], d_qk // LANES)
            dq = jnp.concatenate(dq_parts, axis=1)
            dk = jnp.concatenate(dk_parts, axis=1) * (QK_DIM ** -0.5)
            dq_ref[rs, :] = (dq * cos + _rot_partner(dq * sin)).astype(BF16)
            dk_ref[rs, :] = (dk * cos + _rot_partner(dk * sin)).astype(BF16)

    last = n_ch // cps - 1
    ch = lambda w: pl.BlockSpec((tm, w), lambda n: (last - n, 0))
    const2 = lambda a: pl.BlockSpec(a.shape, lambda n: (0, 0))
    const3 = lambda a: pl.BlockSpec(a.shape, lambda n: (0, 0, 0))
    return _hosted_call(
        body, ride, n_ch // cps,
        name="ret_bwd", barrier_id=1,
        in_specs=[ch(d_qk)] * 4 + [ch(d_ret), ch(d_ret), ch(d_ret),
                  pl.BlockSpec((cps, n_pairs, 2 * QK_DIM, 2 * LANES), lambda n: (last - n, 0, 0, 0)),
                  ch(d_ret), const2(gain), ch(LANES), ch(LANES),
                  const3(dmask), const3(dmask_t), const2(qdec), const2(kdec), const3(g_state), const2(block_diag)],
        out_specs=[ch(d_qk), ch(d_qk), ch(d_ret), ch(d_ret), pl.BlockSpec((1, d_ret), lambda n: (0, 0))],
        out_shape=[jax.ShapeDtypeStruct((tp, d_qk), BF16), jax.ShapeDtypeStruct((tp, d_qk), BF16),
                   jax.ShapeDtypeStruct((tp, d_ret), BF16), jax.ShapeDtypeStruct((tp, d_ret), BF16),
                   jax.ShapeDtypeStruct((1, d_ret), F32)],
        scratch_shapes=[pltpu.VMEM((n_pairs, 2 * QK_DIM, 2 * LANES), F32)],
        args=(qb, kb, qd, kd, vb, rg, o, rprev, dy, gain, cos_t, sin_t, dmask, dmask_t, qdec, kdec, g_state,
              block_diag),
    )


def _lru_bwd(lx, lg, hl, dy, saved, cw, wr, wi, lam, dgain, dgf, tm, ride=None):
    tp, w = lx.shape
    nt = tp // tm
    per8 = tm // SUBLANES
    n_heads = wr.shape[0]
    vec_row = {name: ROW_VEC + VEC_NAMES.index(name) for name in VEC_NAMES}

    def body(lx_ref, lg_ref, hl_ref, hlp_ref, dy_ref, xc_ref, r_ref, ig_ref, a_ref, beta_ref, w4_ref,
             cw_ref, wr_ref, wi_ref, lam_ref, dgain_ref, dgf_ref,
             dlx_ref, dlg_ref, pk_ref,
             g_s, b_s, carry, dxc_next, a_next):
        i = pl.program_id(0)
        first_tile = i == nt - 1
        heads = [slice(hd * LANES, (hd + 1) * LANES) for hd in range(n_heads)]

        def add_row(hd, row, value):
            pk_ref[hd, row:row + 1, :] += value

        @pl.when(i == 0)
        def _():
            carry[...] = jnp.zeros_like(carry)
            dxc_next[...] = jnp.zeros_like(dxc_next)
            a_next[...] = jnp.zeros_like(a_next)
            pk_ref[...] = jnp.zeros_like(pk_ref)
            for hd, hs in enumerate(heads):
                add_row(hd, vec_row["ret_norm_gain"], dgain_ref[:, hs])
                add_row(hd, vec_row["final_norm_gain"], dgf_ref[:, hs])

        for hd, hs in enumerate(heads):
            g = lg_ref[:, hs]
            sg = _sigmoid(g)
            dyv = dy_ref[:, hs]
            dlg_ref[:, hs] = (dyv * hl_ref[:, hs] * (sg * (1.0 + g * (1.0 - sg)))).astype(BF16)
            g_s[hd] = dyv * (g * sg)
            b_s[hd] = _shift_up(a_ref[:, hs], a_next[:, hs], 1)
        carry[0:1, :] = _segment_scan(b_s, g_s, g_s, b_s, carry[0:1, :], tm, reverse=True)
        a_next[...] = a_ref[0:SUBLANES, :]
        row = lax.broadcasted_iota(jnp.int32, (tm, 1), 0) + (nt - 1 - i) * tm
        lam_v = lam_ref[...]
        dlam_scale = LRU_C * _sigmoid(-lam_v)
        dr_scale = -LRU_C * _softplus_neg(lam_v)
        for hd, hs in enumerate(heads):
            a, beta, r, ig, xc = a_ref[:, hs], beta_ref[:, hs], r_ref[:, hs], ig_ref[:, hs], xc_ref[:, hs]
            dh = g_s[hd]
            hprev = _shift_down(hl_ref[:, hs], jnp.where(first_tile, 0.0, hlp_ref[:, hs]), 1)
            du = jnp.where(row >= PAD_ROWS, dh, 0.0)
            dbeta = du * ig * xc
            d_ig = du * beta * xc
            dxc = du * beta * ig
            dloga = (dh * hprev) * a - dbeta * w4_ref[:, hs]
            add_row(hd, vec_row["lru_lambda"], jnp.sum(dloga * r, axis=0, keepdims=True) * dlam_scale[:, hs])
            dpr = (dloga * dr_scale[:, hs]) * r * (1.0 - r)
            dpi = d_ig * ig * (1.0 - ig)
            add_row(hd, vec_row["b_rg"], jnp.sum(dpr, axis=0, keepdims=True))
            add_row(hd, vec_row["b_ig"], jnp.sum(dpi, axis=0, keepdims=True))
            xh, dprh, dpih = xc.astype(BF16), dpr.astype(BF16), dpi.astype(BF16)
            pk_ref[hd, ROW_WR:ROW_WR + LANES, :] += _dot_tn(xh, dprh)
            pk_ref[hd, ROW_WI:ROW_WI + LANES, :] += _dot_tn(xh, dpih)
            dxc = dxc + _dot_nt(dprh, wr_ref[hd].astype(BF16)) + _dot_nt(dpih, wi_ref[hd].astype(BF16))
            nxt = dxc_next[:, hs]
            up1, up2, up3 = _shift_up(dxc, nxt, 1), _shift_up(dxc, nxt, 2), _shift_up(dxc, nxt, 3)
            dlx = dxc * cw_ref[3:4, hs]
            dlx = dlx + up1 * cw_ref[2:3, hs]
            dlx = dlx + up2 * cw_ref[1:2, hs]
            dlx = dlx + up3 * cw_ref[0:1, hs]
            dlx_ref[:, hs] = dlx.astype(BF16)
            dxc_next[:, hs] = dxc[0:SUBLANES]
            lxv = lx_ref[:, hs]
            add_row(hd, vec_row["conv_b"], jnp.sum(dxc, axis=0, keepdims=True))
            for kk, shifted in enumerate((up3, up2, up1, dxc)):
                add_row(hd, ROW_CONV + kk, jnp.sum(shifted * lxv, axis=0, keepdims=True))

    last = nt - 1
    tile = pl.BlockSpec((tm, w), lambda i: (last - i, 0))
    prev = pl.BlockSpec((SUBLANES, w), lambda i: (jnp.maximum((last - i) * per8 - 1, 0), 0))
    vec = pl.BlockSpec((1, w), lambda i: (0, 0))
    mat = pl.BlockSpec(wr.shape, lambda i: (0, 0, 0))
    cwb = pl.BlockSpec(cw.shape, lambda i: (0, 0))
    packed = (n_heads, UNIT_ROWS, LANES)
    return _hosted_call(
        body, ride, nt,
        name="lru_bwd", barrier_id=2,
        in_specs=[tile, tile, tile, prev, tile] + [tile] * 6 + [cwb, mat, mat, vec, vec, vec],
        out_specs=[tile, tile, pl.BlockSpec(packed, lambda i: (0, 0, 0))],
        out_shape=[jax.ShapeDtypeStruct((tp, w), BF16), jax.ShapeDtypeStruct((tp, w), BF16),
                   jax.ShapeDtypeStruct(packed, F32)],
        scratch_shapes=[pltpu.VMEM((w // LANES, tm, LANES), F32)] * 2 + [pltpu.VMEM((SUBLANES, w), F32)] * 3,
        args=(lx, lg, hl, hl, dy, *saved, cw, wr, wi, lam, dgain, dgf),
    )


def _in_proj_dw(dparts, hp, gain, wg_shape, ride=None):
    tp, d = hp.shape
    n_ch = tp // CHUNK
    per = next(p for p in (4, 2, 5, 3, 1) if (n_ch - 1) % p == 0)
    n_steps = 1 + (n_ch - 1) // per
    widths = [p.shape[1] for p in dparts]
    segs = _proj_segments(widths[0], widths[2], widths[4], wg_shape[2])

    def body(*refs):
        dp = [refs[p * per:(p + 1) * per] for p in range(6)]
        hp_b = refs[6 * per:7 * per]
        g_ref, dwg_ref, acc, sem = refs[7 * per:]
        i = pl.program_id(0)

        def accumulate(blocks):
            h = jnp.concatenate([hp_b[b][...] for b in blocks], axis=0)
            rinv = lax.rsqrt(jnp.mean(h * h, axis=-1, keepdims=True) + EPS)
            u = ((h * rinv) * g_ref[...]).astype(BF16)
            for p_refs, parts in zip(dp, segs):
                for jj, inner, off, take in parts:
                    seg = jnp.concatenate([p_refs[b][:, off:off + take] for b in blocks], axis=0)
                    acc[jj, :, inner:inner + take] += _dot_tn(u, seg)

        @pl.when(i == 0)
        def _():
            acc[...] = jnp.zeros_like(acc)
            accumulate([0])

        @pl.when(i > 0)
        def _():
            accumulate(list(range(per)))

        @pl.when(i == n_steps - 1)
        def _():
            cp = pltpu.make_async_copy(acc, dwg_ref, sem)
            cp.start()
            cp.wait()

    def blocks(w):
        return [pl.BlockSpec((CHUNK, w), functools.partial(
            lambda i, b: (jnp.where(i == 0, b, per * (i - 1) + 1 + b), 0), b=b)) for b in range(per)]

    in_specs, args = [], []
    for a, w in list(zip(dparts, widths)) + [(hp, d)]:
        in_specs += blocks(w)
        args += [a] * per
    outs, rides = _hosted_call(
        body, ride, n_steps,
        name="in_proj_dw", barrier_id=3,
        in_specs=in_specs + [pl.BlockSpec(gain.shape, lambda i: (0, 0))],
        out_specs=[ANY_SPEC],
        out_shape=[jax.ShapeDtypeStruct(wg_shape, F32)],
        scratch_shapes=[pltpu.VMEM(wg_shape, F32), pltpu.SemaphoreType.DMA],
        args=(*args, gain),
    )
    return outs[0], rides


def _in_proj_dx(dparts, hp, dh2, gain, wg, s_len, tm, ride=None):
    tp, d = hp.shape
    nt = tp // tm
    widths = [p.shape[1] for p in dparts]
    segs = _proj_segments(widths[0], widths[2], widths[4], wg.shape[2])

    def body(*refs):
        dp = refs[:6]
        hp_ref, dh2_ref, g_ref, w_ref = refs[6:10]
        gx_ref, dmeta_ref, dg_ref = refs[10:13]
        stage, sems = refs[13:]
        i = pl.program_id(0)

        @pl.when(i == 0)
        def _():
            dg_ref[...] = jnp.zeros_like(dg_ref)

        h = hp_ref[...]
        rinv = lax.rsqrt(jnp.mean(h * h, axis=-1, keepdims=True) + EPS)
        nrm = h * rinv
        gv = g_ref[...]
        du = jnp.zeros((tm, d), F32)
        for p_ref, parts in zip(dp, segs):
            for jj, inner, off, take in parts:
                du = du + _dot_nt(p_ref[:, off:off + take], w_ref[jj, :, inner:inner + take])
        dg_ref[...] += jnp.sum(du * nrm, axis=0, keepdims=True)
        dn = du * gv
        dh = dh2_ref[...] + rinv * (dn - nrm * jnp.mean(dn * nrm, axis=-1, keepdims=True))

        def first_copy():
            return pltpu.make_async_copy(stage.at[0, pl.ds(CHUNK, tm - CHUNK), :],
                                         gx_ref.at[pl.ds(0, tm - CHUNK), :], sems.at[0])

        def tile_copy(slot, start):
            return pltpu.make_async_copy(stage.at[slot], gx_ref.at[pl.ds(start, tm), :], sems.at[slot])

        @pl.when(i == 0)
        def _():
            dmeta_ref[...] = dh[PAD_ROWS:CHUNK]
            stage[0] = dh
            first_copy().start()

        @pl.when(i > 0)
        def _():
            slot = 1 + i % 2

            @pl.when(i >= 3)
            def _():
                tile_copy(slot, 0).wait()

            stage[slot] = dh
            tile_copy(slot, pl.multiple_of(i * tm - CHUNK, CHUNK)).start()

        @pl.when(i == nt - 1)
        def _():
            first_copy().wait()
            for step in (nt - 2, nt - 1):
                if step >= 1:
                    tile_copy(1 + step % 2, 0).wait()

    tile = lambda w: pl.BlockSpec((tm, w), lambda i: (i, 0))
    return _hosted_call(
        body, ride, nt,
        name="in_proj_dx", barrier_id=4,
        in_specs=[tile(w) for w in widths] + [tile(d), tile(d), pl.BlockSpec(gain.shape, lambda i: (0, 0)),
                                              pl.BlockSpec(wg.shape, lambda i: (0, 0, 0))],
        out_specs=[ANY_SPEC, pl.BlockSpec((N_META, d), lambda i: (0, 0)), pl.BlockSpec((1, d), lambda i: (0, 0))],
        out_shape=[jax.ShapeDtypeStruct((s_len, d), F32), jax.ShapeDtypeStruct((N_META, d), F32),
                   jax.ShapeDtypeStruct((1, d), F32)],
        scratch_shapes=[pltpu.VMEM((3, tm, d), F32), pltpu.SemaphoreType.DMA((3,))],
        args=(*dparts, hp, dh2, gain, wg),
    )


def _pair_sum(buf, recv, c_arr, tr, name):
    _, rows, cols = buf.shape

    def body(c_ref, mine_ref, got_ref, out_ref):
        out_ref[...] = (mine_ref[...] + got_ref[...]).astype(BF16)

    grid_spec = pltpu.PrefetchScalarGridSpec(
        num_scalar_prefetch=1,
        grid=(N_CHIPS, rows // tr),
        in_specs=[pl.BlockSpec((1, tr, cols), lambda jj, r, c_ref: (2 * jj + c_ref[0], r, 0)),
                  pl.BlockSpec((1, tr, cols), lambda jj, r, c_ref: (jj, r, 0))],
        out_specs=pl.BlockSpec((1, tr, cols), lambda jj, r, c_ref: (jj, r, 0)),
    )
    return pl.pallas_call(
        body,
        name=name,
        grid_spec=grid_spec,
        out_shape=jax.ShapeDtypeStruct((N_CHIPS, rows, cols), BF16),
    )(c_arr, buf, recv)


def _pair_exchange_sum(buf, c_arr, tr, name):
    _, rows, cols = buf.shape
    per = rows // tr

    def body(c_ref, src_ref, mine_ref, out_ref, got, send_sems, recv_sems):
        jj, r = pl.program_id(0), pl.program_id(1)
        x, y, c, _ = _position()
        copies = [_remote(src_ref.at[2 * k + 1 - c], got.at[k], send_sems, recv_sems, k, (x, y, 1 - c))
                  for k in range(N_CHIPS)]

        @pl.when((jj == 0) & (r == 0))
        def _():
            _peer_barrier([(x, y, 1 - c)])
            for cp in copies:
                cp.start()

        for k in range(N_CHIPS):
            @pl.when((jj == k) & (r == 0))
            def _():
                copies[k].wait_recv()

        rows_r = pl.ds(pl.multiple_of(r * tr, tr), tr)
        out_ref[0] = (mine_ref[0] + got[jj, rows_r, :]).astype(BF16)

        @pl.when((jj == N_CHIPS - 1) & (r == per - 1))
        def _():
            for cp in copies:
                cp.wait_send()

    grid_spec = pltpu.PrefetchScalarGridSpec(
        num_scalar_prefetch=1,
        grid=(N_CHIPS, per),
        in_specs=[ANY_SPEC, pl.BlockSpec((1, tr, cols), lambda jj, r, c_ref: (2 * jj + c_ref[0], r, 0))],
        out_specs=pl.BlockSpec((1, tr, cols), lambda jj, r, c_ref: (jj, r, 0)),
        scratch_shapes=[pltpu.VMEM((N_CHIPS, rows, cols), F32), pltpu.SemaphoreType.DMA((N_CHIPS,)),
                        pltpu.SemaphoreType.DMA((N_CHIPS,))],
    )
    return pl.pallas_call(
        body,
        name=name,
        grid_spec=grid_spec,
        out_shape=jax.ShapeDtypeStruct((N_CHIPS, rows, cols), BF16),
        compiler_params=pltpu.CompilerParams(dimension_semantics=("arbitrary", "arbitrary"),
                                             vmem_limit_bytes=VMEM_LIMIT, collective_id=5),
    )(c_arr, buf, buf)


def _chip_sum(mine, got, j_arr, tr, name, loss_part=None):
    _, rows, cols = got.shape
    extra = [] if loss_part is None else [loss_part]

    def body(j_ref, mine_ref, got_ref, *rest):
        out_ref = rest[-1]
        j = j_ref[0]
        acc = None
        for jj in range(N_CHIPS):
            term = jnp.where(j == jj, mine_ref[0], got_ref[jj]).astype(F32)
            acc = term if acc is None else acc + term
        out_ref[...] = acc
        if loss_part is not None:
            out_ref[ROW_LOSS:ROW_LOSS + 1, :] = rest[0][0:1, :]

    grid_spec = pltpu.PrefetchScalarGridSpec(
        num_scalar_prefetch=1,
        grid=(rows // tr,),
        in_specs=[pl.BlockSpec((1, tr, cols), lambda r, j_ref: (j_ref[0], r, 0)),
                  pl.BlockSpec((N_CHIPS, tr, cols), lambda r, j_ref: (0, r, 0))] +
                 [pl.BlockSpec(e.shape, lambda r, j_ref: (0, 0)) for e in extra],
        out_specs=pl.BlockSpec((tr, cols), lambda r, j_ref: (r, 0)),
    )
    return pl.pallas_call(
        body,
        name=name,
        grid_spec=grid_spec,
        out_shape=jax.ShapeDtypeStruct((rows, cols), F32),
    )(j_arr, mine, got, *extra)


def _finish_exchange(f_in, f_small):
    def body(fin_ref, fs_ref, rin_ref, os_ref, send_sems, recv_sems, local_sem):
        x, y, c, chips = _position()
        j = 2 * x + y
        me = 2 * j + c
        sibling = (x, y, 1 - c)
        _peer_barrier([sibling] + [(cx, cy, c) for cx, cy in chips])
        local = pltpu.make_async_copy(fs_ref, os_ref.at[me], local_sem)
        local.start()

        def copy(k, src, dst, to):
            return _remote(src, dst, send_sems, recv_sems, k, to)

        first = [copy(0, fin_ref, rin_ref, sibling), copy(1, fs_ref, os_ref.at[me], sibling)]
        first += [copy(2 + k, fs_ref, os_ref.at[me], (cx, cy, c)) for k, (cx, cy) in enumerate(chips)]
        for cp in first:
            cp.start()
        passed = []
        for k, (cx, cy) in enumerate(chips):
            unit = 2 * (2 * cx + cy) + c
            copy(2 + k, fs_ref, os_ref.at[unit], sibling).wait_recv()
            fwd = copy(5 + k, os_ref.at[unit], os_ref.at[unit], sibling)
            fwd.start()
            passed.append(fwd)
        copy(0, fin_ref, rin_ref, sibling).wait_recv()
        copy(1, fs_ref, os_ref.at[2 * j + 1 - c], sibling).wait_recv()
        for k, (cx, cy) in enumerate(chips):
            unit = 2 * (2 * cx + cy) + 1 - c
            copy(5 + k, fs_ref, os_ref.at[unit], sibling).wait_recv()
        for cp in first + passed:
            cp.wait_send()
        local.wait()

    return pl.pallas_call(
        body,
        name="grad_finish_exchange",
        in_specs=[ANY_SPEC] * 2,
        out_specs=[ANY_SPEC] * 2,
        out_shape=[jax.ShapeDtypeStruct(f_in.shape, F32), jax.ShapeDtypeStruct((N_DEV,) + f_small.shape, F32)],
        scratch_shapes=[pltpu.SemaphoreType.DMA((8,)), pltpu.SemaphoreType.DMA((8,)), pltpu.SemaphoreType.DMA],
        compiler_params=pltpu.CompilerParams(collective_id=8),
    )(f_in, f_small)


def _adamw_math(w, g, m, v):
    m = ADAM_B1 * m + (1.0 - ADAM_B1) * g
    v = ADAM_B2 * v + (1.0 - ADAM_B2) * (g * g)
    m_hat = m / (1.0 - ADAM_B1 ** ADAM_STEP)
    v_hat = v / (1.0 - ADAM_B2 ** ADAM_STEP)
    delta = -ADAM_LR * (m_hat / (jnp.sqrt(v_hat) + ADAM_EPS) + ADAM_WD * w)
    return delta, m, v


def _adamw_big(w, g_mine, g_sib, m, v, c_arr, tr, name):
    rows, cols = w.shape
    half = rows // 2
    per = half // tr

    def body(c_ref, w_ref, gm_ref, gs_ref, m_ref, v_ref, g_ref, d_ref, mo_ref, vo_ref):
        g = jnp.where(pl.program_id(0) == c_ref[0], gm_ref[...], gs_ref[...])
        g_ref[...] = g
        d_ref[...], mo_ref[...], vo_ref[...] = _adamw_math(w_ref[...], g, m_ref[...], v_ref[...])

    full = pl.BlockSpec((tr, cols), lambda h, r, c_ref: (h * per + r, 0))
    unit = pl.BlockSpec((tr, cols), lambda h, r, c_ref: (r, 0))
    grid_spec = pltpu.PrefetchScalarGridSpec(
        num_scalar_prefetch=1,
        grid=(2, per),
        in_specs=[full, unit, unit, full, full],
        out_specs=[full] * 4,
    )
    return pl.pallas_call(
        body,
        name=name,
        grid_spec=grid_spec,
        out_shape=[jax.ShapeDtypeStruct(w.shape, F32)] * 4,
    )(c_arr, w, g_mine, g_sib, m, v)


def _adamw_small(j_arr, packed, params):
    names = list(params)
    n = len(names)

    def body(j_ref, pk_ref, *refs):
        ins = refs[:3 * n]
        outs = refs[3 * n:]
        j = j_ref[0]

        def shard(row, rows):
            return jnp.concatenate([pk_ref[2 * j, row:row + rows, :], pk_ref[2 * j + 1, row:row + rows, :]], axis=1)

        def tail_sum(unit, row, rows):
            start = pl.multiple_of(UNIT_ROWS + TAIL_ROWS * unit + row, SUBLANES)
            total = pk_ref[0, pl.ds(start, rows), :]
            for dev in range(1, N_DEV):
                total = total + pk_ref[dev, pl.ds(start, rows), :]
            return total

        for idx, name in enumerate(names):
            if name == "w_rg":
                g = pk_ref[:, ROW_WR:ROW_WR + LANES, :]
            elif name == "w_ig":
                g = pk_ref[:, ROW_WI:ROW_WI + LANES, :]
            elif name == "meta_tokens":
                g = jnp.concatenate([tail_sum(2 * j, 0, N_META), tail_sum(2 * j + 1, 0, N_META)], axis=1)
            elif name == "norm_gain":
                g = jnp.concatenate([tail_sum(u, N_META, SUBLANES)[0:1] for u in range(N_DEV)], axis=1)
            elif name == "conv_w":
                g = shard(ROW_CONV, 4)
            else:
                row = ROW_VEC + VEC_NAMES.index(name)
                g = jnp.concatenate([pk_ref[u, row:row + 1, :] for u in range(N_DEV)], axis=1)
            w_ref, m_ref, v_ref = ins[3 * idx:3 * idx + 3]
            delta, m, v = _adamw_math(w_ref[...], g, m_ref[...], v_ref[...])
            g_ref, d_ref, mo_ref, vo_ref = outs[4 * idx:4 * idx + 4]
            g_ref[...], d_ref[...], mo_ref[...], vo_ref[...] = g, delta, m, v
        total = pk_ref[0, ROW_LOSS:ROW_LOSS + 1, :]
        for u in range(1, N_DEV):
            total = total + pk_ref[u, ROW_LOSS:ROW_LOSS + 1, :]
        outs[4 * n][...] = jnp.broadcast_to(total, (SUBLANES, LANES))

    flat_in, out_shape = [], []
    for name in names:
        w, m, v = params[name]
        flat_in += [w, m, v]
        out_shape += [jax.ShapeDtypeStruct(w.shape, F32)] * 4
    out_shape.append(jax.ShapeDtypeStruct((SUBLANES, LANES), F32))
    res = pl.pallas_call(
        body,
        name="adamw_small",
        in_specs=[SMEM_SPEC, VMEM_SPEC] + [VMEM_SPEC] * (3 * n),
        out_specs=[VMEM_SPEC] * (4 * n + 1),
        out_shape=out_shape,
    )(j_arr, packed, *flat_in)
    return {name: tuple(res[4 * idx:4 * idx + 4]) for idx, name in enumerate(names)}, res[4 * n][0, 0]


def _units(a):
    rows = a.shape[0]
    return jnp.transpose(a.reshape(rows, N_DEV, LANES), (1, 0, 2))


def kernel(x, meta_tokens, norm_gain, w_in, conv_w, conv_b, w_rg, b_rg, w_ig, b_ig, lru_lambda, ret_norm_gain, w_out, final_norm_gain, loss_target, m_meta_tokens, m_norm_gain, m_w_in, m_conv_w, m_conv_b, m_w_rg, m_b_rg, m_w_ig, m_b_ig, m_lru_lambda, m_ret_norm_gain, m_w_out, m_final_norm_gain, v_meta_tokens, v_norm_gain, v_w_in, v_conv_w, v_conv_b, v_w_rg, v_b_rg, v_w_ig, v_b_ig, v_lru_lambda, v_ret_norm_gain, v_w_out, v_final_norm_gain):
    s_len, d = x.shape[1], x.shape[2]
    d_lru = w_rg.shape[1] * w_rg.shape[2]
    d_ret = ret_norm_gain.shape[1]
    d_qk = HEADS * QK_DIM
    tp = s_len + CHUNK
    tm = TOKEN_TILE
    assert tp % tm == 0 and d_lru == HEADS * LANES and d_ret == HEADS * LANES
    ax, ay, ac = lax.axis_index("x"), lax.axis_index("y"), lax.axis_index("c")
    c_arr = jnp.reshape(ac, (1,)).astype(jnp.int32)
    j_arr = jnp.reshape(2 * ax + ay, (1,)).astype(jnp.int32)

    small = jnp.concatenate([meta_tokens, conv_w[0], jnp.zeros((4, meta_tokens.shape[1]), F32)], axis=0)
    wg, sg = _gather_weights(w_in[0], small)
    cols = sg.shape[2]
    meta_full = jnp.transpose(sg[:, :N_META, :], (1, 0, 2)).reshape(N_META, N_CHIPS * cols)
    cw_full = jnp.transpose(sg[:, N_META:N_META + 4, :], (1, 0, 2)).reshape(4, N_CHIPS * cols)
    cw8 = jnp.concatenate([cw_full, jnp.zeros((4, cw_full.shape[1]), F32)], axis=0)

    half = QK_DIM // 2
    inv = ROPE_BASE ** (-jnp.arange(half, dtype=F32) / half)
    pos = (jnp.arange(tp) - PAD_ROWS).astype(F32)
    ang = pos[:, None] * inv[None, :]
    cos_t = jnp.tile(jnp.cos(ang), (1, LANES // half))
    sign = jnp.where((jnp.arange(LANES) % QK_DIM) < half, -1.0, 1.0).astype(F32)
    sin_t = jnp.tile(jnp.sin(ang), (1, LANES // half)) * sign[None, :]
    tables = _ret_tables()
    gain_f = final_norm_gain.reshape(1, d)

    hp, lx, lg, *qkv, rg, wo4 = _in_proj(x[0], meta_full, norm_gain, wg, cos_t, sin_t, w_out[0], tables[1], tables[2],
                                         tm, d_lru, d_qk, d_ret)
    wo = wo4.reshape(N_CHIPS * wo4.shape[1], wo4.shape[2])
    hl, y_lru, *lru_saved = _lru_fwd(lx, lg, cw8, conv_b, w_rg[0], b_rg, w_ig[0], b_ig, lru_lambda, tm)
    o, y_ret, rprev = _ret_fwd(*qkv, rg, ret_norm_gain, tables, tm)
    dh2, dy_lru, dy_ret, dwo, dgf, loss_acc = _out_proj_loss(y_lru, y_ret, hp, loss_target[0], wo, gain_f, tm)

    g_out = dwo.reshape(N_DEV, dwo.shape[0] // N_DEV, dwo.shape[1])
    (dq, dk, dv, drg, dgain), (r_out,) = _ret_bwd(*qkv, rg, o, rprev, dy_ret, ret_norm_gain, cos_t, sin_t, tables,
                                                 tm, ride=_pair_ride([g_out]))
    q_out = _pair_sum(g_out, r_out, c_arr, REDUCE_TILE, "grad_pair_sum_out")
    (dlx, dlg, g_small), (e_out,) = _lru_bwd(lx, lg, hl, dy_lru, lru_saved, cw8, w_rg[0], w_ig[0], lru_lambda,
                                            dgain, dgf, tm, ride=_chip_ride([q_out]))
    f_out = _chip_sum(q_out, e_out, j_arr, REDUCE_TILE, "grad_chip_sum_out")
    dparts = [dlx, dlg, dq, dk, dv, drg]
    dwg, (s_out, r_small) = _in_proj_dw(dparts, hp, norm_gain, wg.shape,
                                        ride=_join_rides(_sibling_ride([f_out]), _pair_ride([g_small])))
    g_in = dwg.reshape(N_DEV, dwg.shape[1] // 2, dwg.shape[2])
    q_in = _pair_exchange_sum(g_in, c_arr, REDUCE_TILE, "grad_pair_exchange_sum_in")
    q_small = _pair_sum(g_small, r_small, c_arr, UNIT_ROWS, "grad_pair_sum_small")
    (grad_x, dmeta, dg1), (e_in, e_small) = _in_proj_dx(dparts, hp, dh2, norm_gain, wg, s_len, tm,
                                                        ride=_chip_ride([q_in, q_small]))
    f_in = _chip_sum(q_in, e_in, j_arr, REDUCE_TILE, "grad_chip_sum_in")
    f_small = _chip_sum(q_small, e_small, j_arr, UNIT_ROWS, "grad_chip_sum_small", loss_part=loss_acc)
    tail = jnp.concatenate([_units(dmeta), _units(dg1), jnp.zeros((N_DEV, TAIL_ROWS - N_META - 1, LANES), F32)],
                           axis=1).reshape(N_DEV * TAIL_ROWS, LANES)
    s_in, o_small = _finish_exchange(f_in, jnp.concatenate([f_small, tail], axis=0))

    res_in = _adamw_big(w_in[0], f_in, s_in, m_w_in[0], v_w_in[0], c_arr, REDUCE_TILE, "adamw_w_in")
    res_out = _adamw_big(w_out[0], f_out, s_out, m_w_out[0], v_w_out[0], c_arr, REDUCE_TILE, "adamw_w_out")
    small_params = {
        "meta_tokens": (meta_tokens, m_meta_tokens, v_meta_tokens),
        "norm_gain": (norm_gain, m_norm_gain, v_norm_gain),
        "conv_w": (conv_w[0], m_conv_w[0], v_conv_w[0]),
        "conv_b": (conv_b, m_conv_b, v_conv_b),
        "w_rg": (w_rg[0], m_w_rg[0], v_w_rg[0]),
        "b_rg": (b_rg, m_b_rg, v_b_rg),
        "w_ig": (w_ig[0], m_w_ig[0], v_w_ig[0]),
        "b_ig": (b_ig, m_b_ig, v_b_ig),
        "lru_lambda": (lru_lambda, m_lru_lambda, v_lru_lambda),
        "ret_norm_gain": (ret_norm_gain, m_ret_norm_gain, v_ret_norm_gain),
        "final_norm_gain": (gain_f, m_final_norm_gain.reshape(1, d), v_final_norm_gain.reshape(1, d)),
    }
    res, loss = _adamw_small(j_arr, o_small, small_params)
    res["w_in"] = tuple(res_in)
    res["w_out"] = tuple(res_out)

    order = ["meta_tokens", "norm_gain", "w_in", "conv_w", "conv_b", "w_rg", "b_rg", "w_ig", "b_ig", "lru_lambda",
             "ret_norm_gain", "w_out", "final_norm_gain"]
    shapes = {"w_in": w_in.shape, "conv_w": conv_w.shape, "w_rg": w_rg.shape, "w_ig": w_ig.shape,
              "w_out": w_out.shape, "final_norm_gain": final_norm_gain.shape}
    outs = [loss, grad_x.reshape(x.shape)]
    for kind in range(4):
        for name in order:
            a = res[name][kind]
            outs.append(a.reshape(shapes[name]) if name in shapes else a)
    return tuple(outs)
```

```python
import functools

import jax
import jax.numpy as jnp
from jax import lax
from jax.experimental import pallas as pl
from jax.experimental.pallas import tpu as pltpu

F32 = jnp.float32
BF16 = jnp.bfloat16

N_META = 16
CHUNK = 128
PAD_ROWS = CHUNK - N_META
HEADS = 8
QK_DIM = 64
LANES = 128
SUBLANES = 8
LRU_C = 8.0
EPS = 1e-6
ROPE_BASE = 10000.0
ADAM_LR = 0.001
ADAM_B1 = 0.9
ADAM_B2 = 0.999
ADAM_EPS = 1e-08
ADAM_WD = 0.01
ADAM_STEP = 10
N_CHIPS = 4
N_DEV = 8
TOKEN_TILE = 384
REDUCE_TILE = 256
VMEM_LIMIT = 58 * 1024 * 1024
MESH = pl.DeviceIdType.MESH

VMEM_SPEC = pl.BlockSpec(memory_space=pltpu.VMEM)
SMEM_SPEC = pl.BlockSpec(memory_space=pltpu.SMEM)
ANY_SPEC = pl.BlockSpec(memory_space=pl.ANY)

ROW_WR, ROW_WI, ROW_META, ROW_CONV, ROW_VEC, UNIT_ROWS = 0, 128, 256, 272, 276, 288
VEC_NAMES = ["norm_gain", "conv_b", "b_rg", "b_ig", "lru_lambda", "ret_norm_gain", "final_norm_gain"]
N_VEC = len(VEC_NAMES)
ROW_LOSS = ROW_VEC + N_VEC
TAIL_ROWS = 24


def _dot(a, b):
    return jnp.dot(a, b, preferred_element_type=F32)


def _dot_nt(a, b):
    return lax.dot_general(a, b, (((1,), (1,)), ((), ())), preferred_element_type=F32)


def _dot_tn(a, b):
    return lax.dot_general(a, b, (((0,), (0,)), ((), ())), preferred_element_type=F32)


def _sigmoid(x):
    return 0.5 * jnp.tanh(0.5 * x) + 0.5


def _shift_down(x, prev8, s):
    rolled = pltpu.roll(x, s, 0)
    rows = lax.broadcasted_iota(jnp.int32, (SUBLANES, x.shape[1]), 0)
    top = jnp.where(rows < s, pltpu.roll(prev8, s, 0), rolled[0:SUBLANES])
    return jnp.concatenate([top, rolled[SUBLANES:]], axis=0)


def _shift_up(x, next8, s):
    n = x.shape[0]
    rolled = pltpu.roll(x, n - s, 0)
    rows = lax.broadcasted_iota(jnp.int32, (SUBLANES, x.shape[1]), 0)
    bot = jnp.where(rows >= SUBLANES - s, pltpu.roll(next8, SUBLANES - s, 0), rolled[n - SUBLANES:n])
    return jnp.concatenate([rolled[:n - SUBLANES], bot], axis=0)


def _rot_partner(t):
    w = t.shape[1]
    lane = lax.broadcasted_iota(jnp.int32, t.shape, 1)
    first = (lane % QK_DIM) < (QK_DIM // 2)
    return jnp.where(first, pltpu.roll(t, w - QK_DIM // 2, 1), pltpu.roll(t, QK_DIM // 2, 1))


def _tile_lanes(t, reps):
    return jnp.concatenate([t] * reps, axis=1)


class _Ride:
    def __init__(self, srcs, dst_shapes, n_copies, make, to_sibling=False, to_chips=False):
        self.srcs, self.dst_shapes, self.n_copies, self.make = list(srcs), list(dst_shapes), n_copies, make
        self.to_sibling, self.to_chips = to_sibling, to_chips

    def peers(self):
        x, y, c, chips = _position()
        return ([(x, y, 1 - c)] if self.to_sibling else []) + ([(cx, cy, c) for cx, cy in chips] if self.to_chips else [])


def _join_rides(a, b):
    def make(src, dst, send_sems, recv_sems, base):
        na, da = len(a.srcs), len(a.dst_shapes)
        return (a.make(src[:na], dst[:da], send_sems, recv_sems, base)
                + b.make(src[na:], dst[da:], send_sems, recv_sems, base + a.n_copies))

    return _Ride(a.srcs + b.srcs, a.dst_shapes + b.dst_shapes, a.n_copies + b.n_copies, make,
                 a.to_sibling or b.to_sibling, a.to_chips or b.to_chips)


def _position():
    x, y, c = lax.axis_index("x"), lax.axis_index("y"), lax.axis_index("c")
    return x, y, c, [(1 - x, y), (x, 1 - y), (1 - x, 1 - y)]


def _peer_barrier(peers):
    barrier = pltpu.get_barrier_semaphore()
    for peer in peers:
        pl.semaphore_signal(barrier, inc=1, device_id=peer, device_id_type=MESH)
    pl.semaphore_wait(barrier, len(peers))


def _remote(src, dst, send_sems, recv_sems, k, to):
    return pltpu.make_async_remote_copy(src_ref=src, dst_ref=dst, send_sem=send_sems.at[k], recv_sem=recv_sems.at[k],
                                        device_id=to, device_id_type=MESH)


def _pair_ride(bufs):
    def make(src, dst, send_sems, recv_sems, base):
        x, y, c, _ = _position()
        return [_remote(src[b].at[2 * jj + 1 - c], dst[b].at[jj], send_sems, recv_sems, base + b * N_CHIPS + jj,
                        (x, y, 1 - c)) for b in range(len(bufs)) for jj in range(N_CHIPS)]

    shapes = [jax.ShapeDtypeStruct((N_CHIPS,) + b.shape[1:], b.dtype) for b in bufs]
    return _Ride(bufs, shapes, N_CHIPS * len(bufs), make, to_sibling=True)


def _chip_ride(bufs):
    def make(src, dst, send_sems, recv_sems, base):
        x, y, c, chips = _position()
        return [_remote(src[b].at[2 * cx + cy], dst[b].at[2 * x + y], send_sems, recv_sems, base + b * 3 + k,
                        (cx, cy, c)) for b in range(len(bufs)) for k, (cx, cy) in enumerate(chips)]

    shapes = [jax.ShapeDtypeStruct(b.shape, b.dtype) for b in bufs]
    return _Ride(bufs, shapes, 3 * len(bufs), make, to_chips=True)


def _sibling_ride(bufs):
    def make(src, dst, send_sems, recv_sems, base):
        x, y, c, _ = _position()
        return [_remote(src[b], dst[b], send_sems, recv_sems, base + b, (x, y, 1 - c)) for b in range(len(bufs))]

    shapes = [jax.ShapeDtypeStruct(b.shape, b.dtype) for b in bufs]
    return _Ride(bufs, shapes, len(bufs), make, to_sibling=True)


def _hosted_call(body, ride, n_steps, *, name, in_specs, out_specs, out_shape, scratch_shapes, args, barrier_id=None):
    params = pltpu.CompilerParams(dimension_semantics=("arbitrary",), vmem_limit_bytes=VMEM_LIMIT,
                                  collective_id=barrier_id if ride is not None else None)
    if ride is None:
        res = pl.pallas_call(body, name=name, grid=(n_steps,), in_specs=list(in_specs), out_specs=list(out_specs),
                             out_shape=list(out_shape), scratch_shapes=list(scratch_shapes),
                             compiler_params=params)(*args)
        return list(res), []
    sizes = [len(in_specs), len(ride.srcs), len(out_specs), len(ride.dst_shapes), len(scratch_shapes), 2]

    def hosted(*refs):
        groups, pos = [], 0
        for n in sizes:
            groups.append(refs[pos:pos + n])
            pos += n
        ins, rin, outs, rout, scr, (send_sems, recv_sems) = groups
        i = pl.program_id(0)

        @pl.when(i == 0)
        def _():
            if barrier_id is not None:
                _peer_barrier(ride.peers())
            for cp in ride.make(rin, rout, send_sems, recv_sems, 0):
                cp.start()

        body(*ins, *outs, *scr)

        @pl.when(i == n_steps - 1)
        def _():
            for cp in ride.make(rin, rout, send_sems, recv_sems, 0):
                cp.wait()

    n_out = len(out_specs)
    res = pl.pallas_call(
        hosted,
        name=name,
        grid=(n_steps,),
        in_specs=list(in_specs) + [ANY_SPEC] * len(ride.srcs),
        out_specs=list(out_specs) + [ANY_SPEC] * len(ride.dst_shapes),
        out_shape=list(out_shape) + ride.dst_shapes,
        scratch_shapes=list(scratch_shapes) + [pltpu.SemaphoreType.DMA((ride.n_copies,)),
                                               pltpu.SemaphoreType.DMA((ride.n_copies,))],
        compiler_params=params,
    )(*args, *ride.srcs)
    return list(res[:n_out]), list(res[n_out:])


def _gather_weights(w_in, small):
    r_in, c_in = w_in.shape
    h_in = r_in // 2
    q_in = h_in // 2

    def body(win_ref, small_ref, wg_ref, sg_ref, mine, send_sems, recv_sems, local_sem):
        x, y, c, chips = _position()
        j = 2 * x + y
        sibling = (x, y, 1 - c)
        xn, yn, dg = chips
        jx, jy, jd = (2 * cx + cy for cx, cy in chips)

        def half(jj, cc):
            return wg_ref.at[jj, pl.ds(cc * h_in, h_in), :]

        def quarter(jj, qq):
            return wg_ref.at[jj, pl.ds(c * h_in + qq * q_in, q_in), :]

        def copy(k, ref, to):
            return _remote(ref, ref, send_sems, recv_sems, k, to)

        def send_mine(k, qq, to):
            rows = pl.ds(c * h_in + qq * q_in, q_in)
            return _remote(mine.at[rows, :], wg_ref.at[j, rows, :], send_sems, recv_sems, k, to)

        def cast_rows(start, rows):
            start = pl.multiple_of(start, q_in)
            mine[pl.ds(start, rows), :] = win_ref[pl.ds(start, rows), :].astype(BF16)

        first = [send_mine(0, 0, (*xn, c)), send_mine(2, 1, (*yn, c)), send_mine(1, 1, (*xn, c)),
                 send_mine(3, 0, (*yn, c))]
        sg_ref[j] = small_ref[...]
        cast_rows(c * h_in, q_in)
        _peer_barrier([sibling] + [(cx, cy, c) for cx, cy in chips])
        first[0].start()
        cast_rows(c * h_in + q_in, q_in)
        for cp in first[1:]:
            cp.start()
        small_copies = [copy(9 + k, sg_ref.at[j], (cx, cy, c)) for k, (cx, cy) in enumerate(chips)]
        for cp in small_copies:
            cp.start()
        first += small_copies
        cast_rows((1 - c) * h_in, h_in)
        keep = pltpu.make_async_copy(mine, wg_ref.at[j], local_sem)
        keep.start()
        copy(0, quarter(jx, 0), sibling).wait_recv()
        along_y = copy(4, quarter(jx, 0), (*yn, c))
        along_y.start()
        copy(2, quarter(jy, 1), sibling).wait_recv()
        along_x = copy(5, quarter(jy, 1), (*xn, c))
        along_x.start()
        copy(1, quarter(jx, 1), sibling).wait_recv()
        to_sib = [copy(6, half(jx, c), sibling)]
        to_sib[-1].start()
        copy(3, quarter(jy, 0), sibling).wait_recv()
        to_sib.append(copy(7, half(jy, c), sibling))
        to_sib[-1].start()
        copy(4, quarter(jd, 0), sibling).wait_recv()
        copy(5, quarter(jd, 1), sibling).wait_recv()
        to_sib.append(copy(8, half(jd, c), sibling))
        to_sib[-1].start()
        for k, jk in enumerate((jx, jy, jd)):
            copy(6 + k, half(jk, 1 - c), sibling).wait_recv()
            copy(9 + k, sg_ref.at[jk], sibling).wait_recv()
        for cp in first + [along_y, along_x] + to_sib:
            cp.wait_send()
        keep.wait()

    return pl.pallas_call(
        body,
        name="gather_weights",
        out_shape=(jax.ShapeDtypeStruct((N_CHIPS, r_in, c_in), BF16),
                   jax.ShapeDtypeStruct((N_CHIPS,) + small.shape, F32)),
        in_specs=[VMEM_SPEC, VMEM_SPEC],
        out_specs=(ANY_SPEC, VMEM_SPEC),
        scratch_shapes=[pltpu.VMEM((r_in, c_in), BF16), pltpu.SemaphoreType.DMA((12,)),
                        pltpu.SemaphoreType.DMA((12,)), pltpu.SemaphoreType.DMA],
        compiler_params=pltpu.CompilerParams(vmem_limit_bytes=VMEM_LIMIT, collective_id=6),
    )(w_in, small)


def _proj_segments(d_lru, d_qk, d_ret, chunk_w):
    widths = [d_lru, d_lru, d_qk, d_qk, d_ret, d_ret]
    segs, col = [], 0
    for w in widths:
        parts, off = [], 0
        while off < w:
            jj, inner = divmod(col + off, chunk_w)
            take = min(w - off, chunk_w - inner)
            parts.append((jj, inner, off, take))
            off += take
        segs.append(parts)
        col += w
    return segs


def _in_proj(x2, meta, gain, wg, cos_t, sin_t, w_out, qdec, kdec, tm, d_lru, d_qk, d_ret):
    s_len, d = x2.shape
    tp = s_len + CHUNK
    nt, nb = tp // tm, tm // CHUNK
    segs = _proj_segments(d_lru, d_qk, d_ret, wg.shape[2])
    outs = [(d, F32), (d_lru, F32), (d_lru, F32)] + [(d_qk, BF16)] * 4 + [(d_ret, BF16), (d_ret, F32)]
    r_out, c_out = w_out.shape
    h_out = r_out // 2
    fwd_step = min(6, nt - 1)

    def gather_w_out(i, wout_ref, wo_ref, wob, send_sems, recv_sems, local_sem):
        x, y, c, chips = _position()
        j = 2 * x + y
        sibling = (x, y, 1 - c)

        def half(jj, cc):
            return wo_ref.at[jj, pl.ds(cc * h_out, h_out), :]

        local = pltpu.make_async_copy(wob, wo_ref.at[j], local_sem)
        first = [_remote(wob.at[pl.ds(c * h_out, h_out), :], half(j, c), send_sems, recv_sems, k, (cx, cy, c))
                 for k, (cx, cy) in enumerate(chips)]
        passed = [_remote(half(2 * cx + cy, c), half(2 * cx + cy, c), send_sems, recv_sems, 3 + k, sibling)
                  for k, (cx, cy) in enumerate(chips)]

        @pl.when(i == 0)
        def _():
            wob[...] = wout_ref[...].astype(BF16)
            _peer_barrier([sibling] + [(cx, cy, c) for cx, cy in chips])
            local.start()
            for cp in first:
                cp.start()

        @pl.when(i == fwd_step)
        def _():
            for k, (cx, cy) in enumerate(chips):
                _remote(half(2 * cx + cy, c), half(2 * cx + cy, c), send_sems, recv_sems, k, sibling).wait_recv()
                passed[k].start()

        @pl.when(i == nt - 1)
        def _():
            for k, (cx, cy) in enumerate(chips):
                jk = 2 * cx + cy
                _remote(half(jk, 1 - c), half(jk, 1 - c), send_sems, recv_sems, 3 + k, sibling).wait_recv()
            for cp in first + passed:
                cp.wait_send()
            local.wait()

    def body(*refs):
        xb = refs[:nb]
        meta_ref, g_ref, w_ref, cos_ref, sin_ref, wout_ref, qdec_ref, kdec_ref = refs[nb:nb + 8]
        hp_ref, lx_ref, lg_ref, qb_ref, kb_ref, qd_ref, kd_ref, vb_ref, rg_ref = refs[nb + 8:nb + 17]
        wo_ref, q_s, k_s, wob, send_sems, recv_sems, local_sem = refs[nb + 17:]
        i = pl.program_id(0)
        gather_w_out(i, wout_ref, wo_ref, wob, send_sems, recv_sems, local_sem)
        blocks = [r[...] for r in xb]
        head = jnp.concatenate([jnp.zeros((PAD_ROWS, d), F32), meta_ref[...]], axis=0)
        blocks[0] = jnp.where(i == 0, head, blocks[0])
        h = jnp.concatenate(blocks, axis=0)
        hp_ref[...] = h
        rinv = lax.rsqrt(jnp.mean(h * h, axis=-1, keepdims=True) + EPS)
        u = ((h * rinv) * g_ref[...]).astype(BF16)
        for out_ref, parts in zip([lx_ref, lg_ref, q_s, k_s, vb_ref, rg_ref], segs):
            for jj, inner, off, take in parts:
                out_ref[:, off:off + take] = _dot(u, w_ref[jj, :, inner:inner + take]).astype(out_ref.dtype)
        cos = _tile_lanes(cos_ref[...], d_qk // LANES)
        sin = _tile_lanes(sin_ref[...], d_qk // LANES)
        q = q_s[...]
        q = q * cos + _rot_partner(q) * sin
        k = k_s[...]
        k = (k * cos + _rot_partner(k) * sin) * (QK_DIM ** -0.5)
        qb_ref[...] = q.astype(BF16)
        kb_ref[...] = k.astype(BF16)
        qd_ref[...] = (q * jnp.concatenate([qdec_ref[...]] * nb, axis=0)).astype(BF16)
        kd_ref[...] = (k * jnp.concatenate([kdec_ref[...]] * nb, axis=0)).astype(BF16)

    x_specs = [pl.BlockSpec((CHUNK, d), functools.partial(lambda i, b: (jnp.maximum(i * nb + b - 1, 0), 0), b=b))
               for b in range(nb)]
    tile = lambda w: pl.BlockSpec((tm, w), lambda i: (i, 0))
    return pl.pallas_call(
        body,
        name="in_proj",
        grid=(nt,),
        in_specs=x_specs + [pl.BlockSpec(meta.shape, lambda i: (0, 0)),
                            pl.BlockSpec(gain.shape, lambda i: (0, 0)),
                            pl.BlockSpec(wg.shape, lambda i: (0, 0, 0)),
                            tile(LANES), tile(LANES),
                            pl.BlockSpec(w_out.shape, lambda i: (0, 0)),
                            pl.BlockSpec(qdec.shape, lambda i: (0, 0)), pl.BlockSpec(kdec.shape, lambda i: (0, 0))],
        out_specs=[tile(w) for w, _ in outs] + [ANY_SPEC],
        out_shape=[jax.ShapeDtypeStruct((tp, w), dt) for w, dt in outs]
                  + [jax.ShapeDtypeStruct((N_CHIPS, r_out, c_out), BF16)],
        scratch_shapes=[pltpu.VMEM((tm, d_qk), F32), pltpu.VMEM((tm, d_qk), F32),
                        pltpu.VMEM((r_out, c_out), BF16), pltpu.SemaphoreType.DMA((6,)),
                        pltpu.SemaphoreType.DMA((6,)), pltpu.SemaphoreType.DMA],
        compiler_params=pltpu.CompilerParams(dimension_semantics=("arbitrary",), vmem_limit_bytes=VMEM_LIMIT,
                                             collective_id=7),
    )(*([x2] * nb), meta, gain, wg, cos_t, sin_t, w_out, qdec, kdec)


def _segment_scan(a3, u3, out3, p3, carry, tm, reverse):
    groups = a3.shape[0]
    seg = tm // SUBLANES

    def step(j, state):
        hs, ps = state
        rows = pl.ds((seg - 1 - j) if reverse else j, SUBLANES, stride=seg)
        new_h, new_p = [], []
        for g in range(groups):
            a = a3[g, rows, :]
            h = a * hs[g] + u3[g, rows, :]
            p = ps[g] * a
            out3[g, rows, :] = h
            p3[g, rows, :] = p
            new_h.append(h)
            new_p.append(p)
        return tuple(new_h), tuple(new_p)

    zeros = tuple(jnp.zeros((SUBLANES, LANES), F32) for _ in range(groups))
    ones = tuple(jnp.ones((SUBLANES, LANES), F32) for _ in range(groups))
    lax.fori_loop(0, seg, step, (zeros, ones))
    carries = [carry[:, g * LANES:(g + 1) * LANES] for g in range(groups)]
    for s in (reversed(range(SUBLANES)) if reverse else range(SUBLANES)):
        rows = slice(s * seg, (s + 1) * seg)
        edge = s * seg if reverse else (s + 1) * seg - 1
        for g in range(groups):
            out3[g, rows, :] = out3[g, rows, :] + p3[g, rows, :] * carries[g]
            carries[g] = out3[g, edge:edge + 1, :]
    return jnp.concatenate(carries, axis=1)


def _softplus_neg(lam):
    z = -lam
    e = jnp.exp(-jnp.abs(z))
    e1 = 1.0 + e
    log1p_e = jnp.where(e1 == 1.0, e, jnp.log(e1) * (e / (e1 - 1.0)))
    return jnp.maximum(z, 0.0) + log1p_e


def _lru_fwd(lx, lg, cw, cb, wr, br, wi, bi, lam, tm):
    tp, w = lx.shape
    nt = tp // tm
    per8 = tm // SUBLANES
    n_heads = wr.shape[0]

    def body(lx_ref, lxp_ref, lg_ref, cw_ref, cb_ref, wr_ref, br_ref, wi_ref, bi_ref, lam_ref,
             hl_ref, y_ref, xc_ref, r_ref, ig_ref, a_ref, beta_ref, w4_ref, a_s, u_s, h_s, p_s, carry):
        i = pl.program_id(0)

        @pl.when(i == 0)
        def _():
            carry[...] = jnp.zeros_like(carry)

        sp = _softplus_neg(lam_ref[...])
        row = lax.broadcasted_iota(jnp.int32, (tm, 1), 0) + i * tm
        for hd in range(n_heads):
            hs = slice(hd * LANES, (hd + 1) * LANES)
            lxv = lx_ref[:, hs]
            prev8 = jnp.where(i == 0, 0.0, lxp_ref[:, hs])
            xc = cb_ref[:, hs] + _shift_down(lxv, prev8, 3) * cw_ref[0:1, hs]
            xc = xc + _shift_down(lxv, prev8, 2) * cw_ref[1:2, hs]
            xc = xc + _shift_down(lxv, prev8, 1) * cw_ref[2:3, hs]
            xc = xc + lxv * cw_ref[3:4, hs]
            xc_ref[:, hs] = xc
            xh = xc.astype(BF16)
            r = _sigmoid(_dot(xh, wr_ref[hd].astype(BF16)) + br_ref[:, hs])
            ig = _sigmoid(_dot(xh, wi_ref[hd].astype(BF16)) + bi_ref[:, hs])
            r_ref[:, hs] = r
            ig_ref[:, hs] = ig
            log_a = (-LRU_C * r) * sp[:, hs]
            a = jnp.exp(log_a)
            a_ref[:, hs] = a
            a2 = a * a
            beta2 = jnp.maximum((1.0 + a2) * jnp.tanh(-log_a), 1e-37)
            rsb = lax.rsqrt(beta2)
            beta = beta2 * rsb
            beta_ref[:, hs] = beta
            w4_ref[:, hs] = a2 * rsb
            a_s[hd] = a
            u_s[hd] = jnp.where(row >= PAD_ROWS, beta * ig * xc, 0.0)
        carry[0:1, :] = _segment_scan(a_s, u_s, h_s, p_s, carry[0:1, :], tm, reverse=False)
        for hd in range(n_heads):
            hs = slice(hd * LANES, (hd + 1) * LANES)
            hl = h_s[hd]
            hl_ref[:, hs] = hl
            g = lg_ref[:, hs]
            y_ref[:, hs] = (hl * (g * _sigmoid(g))).astype(BF16)

    tile = pl.BlockSpec((tm, w), lambda i: (i, 0))
    prev = pl.BlockSpec((SUBLANES, w), lambda i: (jnp.maximum(i * per8 - 1, 0), 0))
    vec = pl.BlockSpec((1, w), lambda i: (0, 0))
    mat = pl.BlockSpec(wr.shape, lambda i: (0, 0, 0))
    f32_out = jax.ShapeDtypeStruct((tp, w), F32)
    return pl.pallas_call(
        body,
        name="lru_fwd",
        grid=(nt,),
        in_specs=[tile, prev, tile, pl.BlockSpec(cw.shape, lambda i: (0, 0)), vec, mat, vec, mat, vec, vec],
        out_specs=[tile] * 8,
        out_shape=[f32_out, jax.ShapeDtypeStruct((tp, w), BF16)] + [f32_out] * 6,
        scratch_shapes=[pltpu.VMEM((w // LANES, tm, LANES), F32)] * 4 + [pltpu.VMEM((SUBLANES, w), F32)],
        compiler_params=pltpu.CompilerParams(dimension_semantics=("arbitrary",), vmem_limit_bytes=VMEM_LIMIT),
    )(lx, lx, lg, cw, cb, wr, br, wi, bi, lam)


def _ret_tables():
    log_g = jnp.log1p(-jnp.exp2(-5.0 - jnp.arange(HEADS, dtype=F32)))
    idx = jnp.arange(CHUNK, dtype=F32)
    diff = idx[:, None] - idx[None, :]
    dmask = jnp.where(diff[None] >= 0.0, jnp.exp(jnp.maximum(diff, 0.0)[None] * log_g[:, None, None]), 0.0)
    kdec = jnp.repeat(jnp.exp((CHUNK - 1.0 - idx)[:, None] * log_g[None, :]), QK_DIM, axis=1)
    qdec = jnp.repeat(jnp.exp((idx + 1.0)[:, None] * log_g[None, :]), QK_DIM, axis=1)
    g_chunk = jnp.exp(CHUNK * log_g)
    g_rows = jnp.repeat(g_chunk, QK_DIM).reshape(HEADS // 2, 2 * QK_DIM, 1)
    g_state = jnp.broadcast_to(g_rows, (HEADS // 2, 2 * QK_DIM, 2 * LANES))
    r_head = jnp.arange(2 * QK_DIM)[:, None] // QK_DIM
    c_head = jnp.arange(2 * LANES)[None, :] // LANES
    block_diag = (r_head == c_head).astype(F32)
    return dmask, qdec, kdec, g_state, block_diag


def _head_norm(o_h):
    mu = jnp.mean(o_h, axis=-1, keepdims=True)
    oc = o_h - mu
    var = jnp.mean(oc * oc, axis=-1, keepdims=True)
    rstd = lax.rsqrt(var + EPS)
    return oc * rstd, rstd


def _ret_fwd(qb, kb, qd, kd, vb, rg, gain, tables, tm):
    tp, d_qk = qb.shape
    d_ret = vb.shape[1]
    n_ch = tp // CHUNK
    cps = tm // CHUNK
    n_pairs = HEADS // 2
    dmask, _, _, g_state, block_diag = tables

    def body(q_ref, k_ref, qd_ref, kd_ref, v_ref, rg_ref, gain_ref, dm_ref, gs_ref, bd_ref,
             o_ref, y_ref, rp_ref, state):
        n = pl.program_id(0)

        @pl.when(n == 0)
        def _():
            state[...] = jnp.zeros_like(state)

        lane = lax.broadcasted_iota(jnp.int32, (CHUNK, LANES), 1)
        for ci in range(cps):
            rs = slice(ci * CHUNK, (ci + 1) * CHUNK)
            for p in range(n_pairs):
                qs = slice(p * LANES, (p + 1) * LANES)
                vs = slice(p * 2 * LANES, (p + 1) * 2 * LANES)
                qp, kb = q_ref[rs, qs], k_ref[rs, qs]
                vb = v_ref[rs, vs]
                qd, kd = qd_ref[rs, qs], kd_ref[rs, qs]
                st = state[p]
                st_b = st.astype(BF16)
                rp_ref[ci, p] = st_b
                cross = _dot(qd, st_b)
                for e in range(2):
                    hd = 2 * p + e
                    hs = slice(hd * LANES, (hd + 1) * LANES)
                    es = slice(e * LANES, (e + 1) * LANES)
                    qm = jnp.where((lane // QK_DIM) == e, qp, jnp.zeros_like(qp))
                    s = _dot_nt(qm, kb) * dm_ref[hd]
                    o_h = _dot(s.astype(BF16), vb[:, es]) + cross[:, es]
                    o_ref[rs, hs] = o_h
                    xhat, _ = _head_norm(o_h)
                    g = rg_ref[rs, hs]
                    y_ref[rs, hs] = ((xhat * gain_ref[:, hs]) * (g * _sigmoid(g))).astype(BF16)
                state[p] = gs_ref[p] * st + bd_ref[...] * _dot_tn(kd, vb)

    ch = lambda w: pl.BlockSpec((tm, w), lambda n: (n, 0))
    const2 = lambda a: pl.BlockSpec(a.shape, lambda n: (0, 0))
    const3 = lambda a: pl.BlockSpec(a.shape, lambda n: (0, 0, 0))
    return pl.pallas_call(
        body,
        name="ret_fwd",
        grid=(n_ch // cps,),
        in_specs=[ch(d_qk)] * 4 + [ch(d_ret), ch(d_ret), const2(gain), const3(dmask), const3(g_state),
                                   const2(block_diag)],
        out_specs=[ch(d_ret), ch(d_ret),
                   pl.BlockSpec((cps, n_pairs, 2 * QK_DIM, 2 * LANES), lambda n: (n, 0, 0, 0))],
        out_shape=[jax.ShapeDtypeStruct((tp, d_ret), F32), jax.ShapeDtypeStruct((tp, d_ret), BF16),
                   jax.ShapeDtypeStruct((n_ch, n_pairs, 2 * QK_DIM, 2 * LANES), BF16)],
        scratch_shapes=[pltpu.VMEM((n_pairs, 2 * QK_DIM, 2 * LANES), F32)],
        compiler_params=pltpu.CompilerParams(dimension_semantics=("arbitrary",), vmem_limit_bytes=VMEM_LIMIT),
    )(qb, kb, qd, kd, vb, rg, gain, dmask, g_state, block_diag)


def _out_proj_loss(y_lru, y_ret, hp, tgt, wo, gain_f, tm):
    tp, d = hp.shape
    w_lru = y_lru.shape[1]
    w_mix = wo.shape[0]
    nt, nb = tp // tm, tm // CHUNK

    def body(*refs):
        yl_ref, yr_ref, hp_ref = refs[:3]
        tb = refs[3:3 + nb]
        wo_ref, gf_ref = refs[3 + nb:5 + nb]
        dh2_ref, dyl_ref, dyr_ref, dwo_ref, dgf_ref, loss_ref = refs[5 + nb:]
        i = pl.program_id(0)

        @pl.when(i == 0)
        def _():
            dwo_ref[...] = jnp.zeros_like(dwo_ref)
            dgf_ref[...] = jnp.zeros_like(dgf_ref)
            loss_ref[...] = jnp.zeros_like(loss_ref)

        yl, yr = yl_ref[...], yr_ref[...]
        h2 = hp_ref[...] + _dot(yl, wo_ref[0:w_lru, :]) + _dot(yr, wo_ref[w_lru:w_mix, :])
        rinv = lax.rsqrt(jnp.mean(h2 * h2, axis=-1, keepdims=True) + EPS)
        nrm = h2 * rinv
        gf = gf_ref[...]
        tgt_v = jnp.concatenate([r[...] for r in tb], axis=0)
        row = lax.broadcasted_iota(jnp.int32, (tm, 1), 0) + i * tm
        err = jnp.where(row >= CHUNK, nrm * gf - tgt_v, 0.0)
        loss_ref[...] += 0.5 * jnp.sum(jnp.mean(err * err, axis=-1, keepdims=True))
        dout = err * (1.0 / d)
        dgf_ref[...] += jnp.sum(dout * nrm, axis=0, keepdims=True)
        dn = dout * gf
        dh2 = rinv * (dn - nrm * jnp.mean(dn * nrm, axis=-1, keepdims=True))
        dh2_ref[...] = dh2
        dh2b = dh2.astype(BF16)
        dyl_ref[...] = _dot_nt(dh2b, wo_ref[0:w_lru, :])
        dyr_ref[...] = _dot_nt(dh2b, wo_ref[w_lru:w_mix, :])
        dwo_ref[0:w_lru, :] += _dot_tn(yl, dh2b)
        dwo_ref[w_lru:w_mix, :] += _dot_tn(yr, dh2b)

    tile = lambda w: pl.BlockSpec((tm, w), lambda i: (i, 0))
    t_specs = [pl.BlockSpec((CHUNK, d), functools.partial(lambda i, b: (jnp.maximum(i * nb + b - 1, 0), 0), b=b))
               for b in range(nb)]
    return pl.pallas_call(
        body,
        name="out_proj_loss",
        grid=(nt,),
        in_specs=[tile(w_lru), tile(w_mix - w_lru), tile(d)] + t_specs +
                 [pl.BlockSpec(wo.shape, lambda i: (0, 0)), pl.BlockSpec(gain_f.shape, lambda i: (0, 0))],
        out_specs=[tile(d), tile(w_lru), tile(w_mix - w_lru), pl.BlockSpec(wo.shape, lambda i: (0, 0)),
                   pl.BlockSpec((1, d), lambda i: (0, 0)), pl.BlockSpec((SUBLANES, LANES), lambda i: (0, 0))],
        out_shape=[jax.ShapeDtypeStruct((tp, d), F32), jax.ShapeDtypeStruct((tp, w_lru), F32),
                   jax.ShapeDtypeStruct((tp, w_mix - w_lru), F32), jax.ShapeDtypeStruct(wo.shape, F32),
                   jax.ShapeDtypeStruct((1, d), F32), jax.ShapeDtypeStruct((SUBLANES, LANES), F32)],
        compiler_params=pltpu.CompilerParams(dimension_semantics=("arbitrary",), vmem_limit_bytes=VMEM_LIMIT),
    )(y_lru, y_ret, hp, *([tgt] * nb), wo, gain_f)


def _ret_bwd(qb, kb, qd, kd, vb, rg, o, rprev, dy, gain, cos_t, sin_t, tables, tm, ride=None):
    tp, d_qk = qb.shape
    d_ret = vb.shape[1]
    n_ch = tp // CHUNK
    cps = tm // CHUNK
    n_pairs = HEADS // 2
    dmask, qdec, kdec, g_state, block_diag = tables

    dmask_t = jnp.swapaxes(dmask, 1, 2)

    def body(q_ref, k_ref, qdb_ref, kdb_ref, v_ref, rg_ref, o_ref, rp_ref, dy_ref, gain_ref, cos_ref, sin_ref,
             dm_ref, dmt_ref, qd_ref, kd_ref, gs_ref, bd_ref, dq_ref, dk_ref, dv_ref, drg_ref, dgain_ref, dstate):
        n = pl.program_id(0)

        @pl.when(n == 0)
        def _():
            dstate[...] = jnp.zeros_like(dstate)
            dgain_ref[...] = jnp.zeros_like(dgain_ref)

        lane = lax.broadcasted_iota(jnp.int32, (CHUNK, LANES), 1)
        for ci in reversed(range(cps)):
            rs = slice(ci * CHUNK, (ci + 1) * CHUNK)
            dq_parts, dk_parts = [], []
            for p in range(n_pairs):
                qs = slice(p * LANES, (p + 1) * LANES)
                vs = slice(p * 2 * LANES, (p + 1) * 2 * LANES)
                do_parts = []
                for e in range(2):
                    hd = 2 * p + e
                    hs = slice(hd * LANES, (hd + 1) * LANES)
                    xhat, rstd = _head_norm(o_ref[rs, hs])
                    g = rg_ref[rs, hs]
                    sg = _sigmoid(g)
                    dyh = dy_ref[rs, hs]
                    gn = gain_ref[:, hs]
                    d_on = dyh * (g * sg)
                    drg_ref[rs, hs] = (dyh * (xhat * gn) * (sg * (1.0 + g * (1.0 - sg)))).astype(BF16)
                    dgain_ref[:, hs] += jnp.sum(d_on * xhat, axis=0, keepdims=True)
                    dxh = d_on * gn
                    do_parts.append(rstd * (dxh - jnp.mean(dxh, axis=-1, keepdims=True)
                                            - xhat * jnp.mean(dxh * xhat, axis=-1, keepdims=True)))
                do_b = jnp.concatenate(do_parts, axis=1).astype(BF16)
                qp, kb = q_ref[rs, qs], k_ref[rs, qs]
                vb = v_ref[rs, vs]
                qd, kd = qdb_ref[rs, qs], kdb_ref[rs, qs]
                dst = dstate[p]
                dst_b = dst.astype(BF16)
                dqp = _dot_nt(do_b, rp_ref[ci, p]) * qd_ref[:, qs]
                dkp = _dot_nt(vb, dst_b) * kd_ref[:, qs]
                dvp = _dot(kd, dst_b)
                dv_parts = []
                for e in range(2):
                    hd = 2 * p + e
                    es = slice(e * LANES, (e + 1) * LANES)
                    mine = (lane // QK_DIM) == e
                    qm = jnp.where(mine, qp, jnp.zeros_like(qp))
                    km = jnp.where(mine, kb, jnp.zeros_like(kb))
                    ds = (_dot_nt(do_b[:, es], vb[:, es]) * dm_ref[hd]).astype(BF16)
                    s_t = (_dot_nt(kb, qm) * dmt_ref[hd]).astype(BF16)
                    ds_t = (_dot_nt(vb[:, es], do_b[:, es]) * dmt_ref[hd]).astype(BF16)
                    dv_parts.append(dvp[:, es] + _dot(s_t, do_b[:, es]))
                    dqp = dqp + _dot(ds, km)
                    dkp = dkp + _dot(ds_t, qm)
                dv_ref[rs, vs] = jnp.concatenate(dv_parts, axis=1).astype(BF16)
                dstate[p] = gs_ref[p] * dst + bd_ref[...] * _dot_tn(qd, do_b)
                dq_parts.append(dqp)
                dk_parts.append(dkp)
            cos = _tile_lanes(cos_ref[rs, :], d_qk // LANES)
            sin = _tile_lanes(sin_ref[rs, :], d_qk // LANES)
            dq = jnp.concatenate(dq_parts, axis=1)
            dk = jnp.concatenate(dk_parts, axis=1) * (QK_DIM ** -0.5)
            dq_ref[rs, :] = (dq * cos + _rot_partner(dq * sin)).astype(BF16)
            dk_ref[rs, :] = (dk * cos + _rot_partner(dk * sin)).astype(BF16)

    last = n_ch // cps - 1
    ch = lambda w: pl.BlockSpec((tm, w), lambda n: (last - n, 0))
    const2 = lambda a: pl.BlockSpec(a.shape, lambda n: (0, 0))
    const3 = lambda a: pl.BlockSpec(a.shape, lambda n: (0, 0, 0))
    return _hosted_call(
        body, ride, n_ch // cps,
        name="ret_bwd", barrier_id=1,
        in_specs=[ch(d_qk)] * 4 + [ch(d_ret), ch(d_ret), ch(d_ret),
                  pl.BlockSpec((cps, n_pairs, 2 * QK_DIM, 2 * LANES), lambda n: (last - n, 0, 0, 0)),
                  ch(d_ret), const2(gain), ch(LANES), ch(LANES),
                  const3(dmask), const3(dmask_t), const2(qdec), const2(kdec), const3(g_state), const2(block_diag)],
        out_specs=[ch(d_qk), ch(d_qk), ch(d_ret), ch(d_ret), pl.BlockSpec((1, d_ret), lambda n: (0, 0))],
        out_shape=[jax.ShapeDtypeStruct((tp, d_qk), BF16), jax.ShapeDtypeStruct((tp, d_qk), BF16),
                   jax.ShapeDtypeStruct((tp, d_ret), BF16), jax.ShapeDtypeStruct((tp, d_ret), BF16),
                   jax.ShapeDtypeStruct((1, d_ret), F32)],
        scratch_shapes=[pltpu.VMEM((n_pairs, 2 * QK_DIM, 2 * LANES), F32)],
        args=(qb, kb, qd, kd, vb, rg, o, rprev, dy, gain, cos_t, sin_t, dmask, dmask_t, qdec, kdec, g_state,
              block_diag),
    )


def _lru_bwd(lx, lg, hl, dy, saved, cw, wr, wi, lam, dgain, dgf, tm, ride=None):
    tp, w = lx.shape
    nt = tp // tm
    per8 = tm // SUBLANES
    n_heads = wr.shape[0]
    vec_row = {name: ROW_VEC + VEC_NAMES.index(name) for name in VEC_NAMES}

    def body(lx_ref, lg_ref, hl_ref, hlp_ref, dy_ref, xc_ref, r_ref, ig_ref, a_ref, beta_ref, w4_ref,
             cw_ref, wr_ref, wi_ref, lam_ref, dgain_ref, dgf_ref,
             dlx_ref, dlg_ref, pk_ref,
             g_s, b_s, carry, dxc_next, a_next):
        i = pl.program_id(0)
        first_tile = i == nt - 1
        heads = [slice(hd * LANES, (hd + 1) * LANES) for hd in range(n_heads)]

        def add_row(hd, row, value):
            pk_ref[hd, row:row + 1, :] += value

        @pl.when(i == 0)
        def _():
            carry[...] = jnp.zeros_like(carry)
            dxc_next[...] = jnp.zeros_like(dxc_next)
            a_next[...] = jnp.zeros_like(a_next)
            pk_ref[...] = jnp.zeros_like(pk_ref)
            for hd, hs in enumerate(heads):
                add_row(hd, vec_row["ret_norm_gain"], dgain_ref[:, hs])
                add_row(hd, vec_row["final_norm_gain"], dgf_ref[:, hs])

        for hd, hs in enumerate(heads):
            g = lg_ref[:, hs]
            sg = _sigmoid(g)
            dyv = dy_ref[:, hs]
            dlg_ref[:, hs] = (dyv * hl_ref[:, hs] * (sg * (1.0 + g * (1.0 - sg)))).astype(BF16)
            g_s[hd] = dyv * (g * sg)
            b_s[hd] = _shift_up(a_ref[:, hs], a_next[:, hs], 1)
        carry[0:1, :] = _segment_scan(b_s, g_s, g_s, b_s, carry[0:1, :], tm, reverse=True)
        a_next[...] = a_ref[0:SUBLANES, :]
        row = lax.broadcasted_iota(jnp.int32, (tm, 1), 0) + (nt - 1 - i) * tm
        lam_v = lam_ref[...]
        dlam_scale = LRU_C * _sigmoid(-lam_v)
        dr_scale = -LRU_C * _softplus_neg(lam_v)
        for hd, hs in enumerate(heads):
            a, beta, r, ig, xc = a_ref[:, hs], beta_ref[:, hs], r_ref[:, hs], ig_ref[:, hs], xc_ref[:, hs]
            dh = g_s[hd]
            hprev = _shift_down(hl_ref[:, hs], jnp.where(first_tile, 0.0, hlp_ref[:, hs]), 1)
            du = jnp.where(row >= PAD_ROWS, dh, 0.0)
            dbeta = du * ig * xc
            d_ig = du * beta * xc
            dxc = du * beta * ig
            dloga = (dh * hprev) * a - dbeta * w4_ref[:, hs]
            add_row(hd, vec_row["lru_lambda"], jnp.sum(dloga * r, axis=0, keepdims=True) * dlam_scale[:, hs])
            dpr = (dloga * dr_scale[:, hs]) * r * (1.0 - r)
            dpi = d_ig * ig * (1.0 - ig)
            add_row(hd, vec_row["b_rg"], jnp.sum(dpr, axis=0, keepdims=True))
            add_row(hd, vec_row["b_ig"], jnp.sum(dpi, axis=0, keepdims=True))
            xh, dprh, dpih = xc.astype(BF16), dpr.astype(BF16), dpi.astype(BF16)
            pk_ref[hd, ROW_WR:ROW_WR + LANES, :] += _dot_tn(xh, dprh)
            pk_ref[hd, ROW_WI:ROW_WI + LANES, :] += _dot_tn(xh, dpih)
            dxc = dxc + _dot_nt(dprh, wr_ref[hd].astype(BF16)) + _dot_nt(dpih, wi_ref[hd].astype(BF16))
            nxt = dxc_next[:, hs]
            up1, up2, up3 = _shift_up(dxc, nxt, 1), _shift_up(dxc, nxt, 2), _shift_up(dxc, nxt, 3)
            dlx = dxc * cw_ref[3:4, hs]
            dlx = dlx + up1 * cw_ref[2:3, hs]
            dlx = dlx + up2 * cw_ref[1:2, hs]
            dlx = dlx + up3 * cw_ref[0:1, hs]
            dlx_ref[:, hs] = dlx.astype(BF16)
            dxc_next[:, hs] = dxc[0:SUBLANES]
            lxv = lx_ref[:, hs]
            add_row(hd, vec_row["conv_b"], jnp.sum(dxc, axis=0, keepdims=True))
            for kk, shifted in enumerate((up3, up2, up1, dxc)):
                add_row(hd, ROW_CONV + kk, jnp.sum(shifted * lxv, axis=0, keepdims=True))

    last = nt - 1
    tile = pl.BlockSpec((tm, w), lambda i: (last - i, 0))
    prev = pl.BlockSpec((SUBLANES, w), lambda i: (jnp.maximum((last - i) * per8 - 1, 0), 0))
    vec = pl.BlockSpec((1, w), lambda i: (0, 0))
    mat = pl.BlockSpec(wr.shape, lambda i: (0, 0, 0))
    cwb = pl.BlockSpec(cw.shape, lambda i: (0, 0))
    packed = (n_heads, UNIT_ROWS, LANES)
    return _hosted_call(
        body, ride, nt,
        name="lru_bwd", barrier_id=2,
        in_specs=[tile, tile, tile, prev, tile] + [tile] * 6 + [cwb, mat, mat, vec, vec, vec],
        out_specs=[tile, tile, pl.BlockSpec(packed, lambda i: (0, 0, 0))],
        out_shape=[jax.ShapeDtypeStruct((tp, w), BF16), jax.ShapeDtypeStruct((tp, w), BF16),
                   jax.ShapeDtypeStruct(packed, F32)],
        scratch_shapes=[pltpu.VMEM((w // LANES, tm, LANES), F32)] * 2 + [pltpu.VMEM((SUBLANES, w), F32)] * 3,
        args=(lx, lg, hl, hl, dy, *saved, cw, wr, wi, lam, dgain, dgf),
    )


def _in_proj_dw(dparts, hp, gain, wg_shape, ride=None):
    tp, d = hp.shape
    n_ch = tp // CHUNK
    per = next(p for p in (4, 2, 5, 3, 1) if (n_ch - 1) % p == 0)
    n_steps = 1 + (n_ch - 1) // per
    widths = [p.shape[1] for p in dparts]
    segs = _proj_segments(widths[0], widths[2], widths[4], wg_shape[2])

    def body(*refs):
        dp = [refs[p * per:(p + 1) * per] for p in range(6)]
        hp_b = refs[6 * per:7 * per]
        g_ref, dwg_ref, acc, sem = refs[7 * per:]
        i = pl.program_id(0)

        def accumulate(blocks):
            h = jnp.concatenate([hp_b[b][...] for b in blocks], axis=0)
            rinv = lax.rsqrt(jnp.mean(h * h, axis=-1, keepdims=True) + EPS)
            u = ((h * rinv) * g_ref[...]).astype(BF16)
            for p_refs, parts in zip(dp, segs):
                for jj, inner, off, take in parts:
                    seg = jnp.concatenate([p_refs[b][:, off:off + take] for b in blocks], axis=0)
                    acc[jj, :, inner:inner + take] += _dot_tn(u, seg)

        @pl.when(i == 0)
        def _():
            acc[...] = jnp.zeros_like(acc)
            accumulate([0])

        @pl.when(i > 0)
        def _():
            accumulate(list(range(per)))

        @pl.when(i == n_steps - 1)
        def _():
            cp = pltpu.make_async_copy(acc, dwg_ref, sem)
            cp.start()
            cp.wait()

    def blocks(w):
        return [pl.BlockSpec((CHUNK, w), functools.partial(
            lambda i, b: (jnp.where(i == 0, b, per * (i - 1) + 1 + b), 0), b=b)) for b in range(per)]

    in_specs, args = [], []
    for a, w in list(zip(dparts, widths)) + [(hp, d)]:
        in_specs += blocks(w)
        args += [a] * per
    outs, rides = _hosted_call(
        body, ride, n_steps,
        name="in_proj_dw", barrier_id=3,
        in_specs=in_specs + [pl.BlockSpec(gain.shape, lambda i: (0, 0))],
        out_specs=[ANY_SPEC],
        out_shape=[jax.ShapeDtypeStruct(wg_shape, F32)],
        scratch_shapes=[pltpu.VMEM(wg_shape, F32), pltpu.SemaphoreType.DMA],
        args=(*args, gain),
    )
    return outs[0], rides


def _in_proj_dx(dparts, hp, dh2, gain, wg, s_len, tm, ride=None):
    tp, d = hp.shape
    nt = tp // tm
    widths = [p.shape[1] for p in dparts]
    segs = _proj_segments(widths[0], widths[2], widths[4], wg.shape[2])

    def body(*refs):
        dp = refs[:6]
        hp_ref, dh2_ref, g_ref, w_ref = refs[6:10]
        gx_ref, dmeta_ref, dg_ref = refs[10:13]
        stage, sems = refs[13:]
        i = pl.program_id(0)

        @pl.when(i == 0)
        def _():
            dg_ref[...] = jnp.zeros_like(dg_ref)

        h = hp_ref[...]
        rinv = lax.rsqrt(jnp.mean(h * h, axis=-1, keepdims=True) + EPS)
        nrm = h * rinv
        gv = g_ref[...]
        du = jnp.zeros((tm, d), F32)
        for p_ref, parts in zip(dp, segs):
            for jj, inner, off, take in parts:
                du = du + _dot_nt(p_ref[:, off:off + take], w_ref[jj, :, inner:inner + take])
        dg_ref[...] += jnp.sum(du * nrm, axis=0, keepdims=True)
        dn = du * gv
        dh = dh2_ref[...] + rinv * (dn - nrm * jnp.mean(dn * nrm, axis=-1, keepdims=True))

        def first_copy():
            return pltpu.make_async_copy(stage.at[0, pl.ds(CHUNK, tm - CHUNK), :],
                                         gx_ref.at[pl.ds(0, tm - CHUNK), :], sems.at[0])

        def tile_copy(slot, start):
            return pltpu.make_async_copy(stage.at[slot], gx_ref.at[pl.ds(start, tm), :], sems.at[slot])

        @pl.when(i == 0)
        def _():
            dmeta_ref[...] = dh[PAD_ROWS:CHUNK]
            stage[0] = dh
            first_copy().start()

        @pl.when(i > 0)
        def _():
            slot = 1 + i % 2

            @pl.when(i >= 3)
            def _():
                tile_copy(slot, 0).wait()

            stage[slot] = dh
            tile_copy(slot, pl.multiple_of(i * tm - CHUNK, CHUNK)).start()

        @pl.when(i == nt - 1)
        def _():
            first_copy().wait()
            for step in (nt - 2, nt - 1):
                if step >= 1:
                    tile_copy(1 + step % 2, 0).wait()

    tile = lambda w: pl.BlockSpec((tm, w), lambda i: (i, 0))
    return _hosted_call(
        body, ride, nt,
        name="in_proj_dx", barrier_id=4,
        in_specs=[tile(w) for w in widths] + [tile(d), tile(d), pl.BlockSpec(gain.shape, lambda i: (0, 0)),
                                              pl.BlockSpec(wg.shape, lambda i: (0, 0, 0))],
        out_specs=[ANY_SPEC, pl.BlockSpec((N_META, d), lambda i: (0, 0)), pl.BlockSpec((1, d), lambda i: (0, 0))],
        out_shape=[jax.ShapeDtypeStruct((s_len, d), F32), jax.ShapeDtypeStruct((N_META, d), F32),
                   jax.ShapeDtypeStruct((1, d), F32)],
        scratch_shapes=[pltpu.VMEM((3, tm, d), F32), pltpu.SemaphoreType.DMA((3,))],
        args=(*dparts, hp, dh2, gain, wg),
    )


def _pair_sum(buf, recv, c_arr, tr, name):
    _, rows, cols = buf.shape

    def body(c_ref, mine_ref, got_ref, out_ref):
        out_ref[...] = (mine_ref[...] + got_ref[...]).astype(BF16)

    grid_spec = pltpu.PrefetchScalarGridSpec(
        num_scalar_prefetch=1,
        grid=(N_CHIPS, rows // tr),
        in_specs=[pl.BlockSpec((1, tr, cols), lambda jj, r, c_ref: (2 * jj + c_ref[0], r, 0)),
                  pl.BlockSpec((1, tr, cols), lambda jj, r, c_ref: (jj, r, 0))],
        out_specs=pl.BlockSpec((1, tr, cols), lambda jj, r, c_ref: (jj, r, 0)),
    )
    return pl.pallas_call(
        body,
        name=name,
        grid_spec=grid_spec,
        out_shape=jax.ShapeDtypeStruct((N_CHIPS, rows, cols), BF16),
    )(c_arr, buf, recv)


def _pair_exchange_sum(buf, c_arr, tr, name):
    _, rows, cols = buf.shape
    per = rows // tr

    def body(c_ref, src_ref, mine_ref, out_ref, got, send_sems, recv_sems):
        jj, r = pl.program_id(0), pl.program_id(1)
        x, y, c, _ = _position()
        copies = [_remote(src_ref.at[2 * k + 1 - c], got.at[k], send_sems, recv_sems, k, (x, y, 1 - c))
                  for k in range(N_CHIPS)]

        @pl.when((jj == 0) & (r == 0))
        def _():
            _peer_barrier([(x, y, 1 - c)])
            for cp in copies:
                cp.start()

        for k in range(N_CHIPS):
            @pl.when((jj == k) & (r == 0))
            def _():
                copies[k].wait_recv()

        rows_r = pl.ds(pl.multiple_of(r * tr, tr), tr)
        out_ref[0] = (mine_ref[0] + got[jj, rows_r, :]).astype(BF16)

        @pl.when((jj == N_CHIPS - 1) & (r == per - 1))
        def _():
            for cp in copies:
                cp.wait_send()

    grid_spec = pltpu.PrefetchScalarGridSpec(
        num_scalar_prefetch=1,
        grid=(N_CHIPS, per),
        in_specs=[ANY_SPEC, pl.BlockSpec((1, tr, cols), lambda jj, r, c_ref: (2 * jj + c_ref[0], r, 0))],
        out_specs=pl.BlockSpec((1, tr, cols), lambda jj, r, c_ref: (jj, r, 0)),
        scratch_shapes=[pltpu.VMEM((N_CHIPS, rows, cols), F32), pltpu.SemaphoreType.DMA((N_CHIPS,)),
                        pltpu.SemaphoreType.DMA((N_CHIPS,))],
    )
    return pl.pallas_call(
        body,
        name=name,
        grid_spec=grid_spec,
        out_shape=jax.ShapeDtypeStruct((N_CHIPS, rows, cols), BF16),
        compiler_params=pltpu.CompilerParams(dimension_semantics=("arbitrary", "arbitrary"),
                                             vmem_limit_bytes=VMEM_LIMIT, collective_id=5),
    )(c_arr, buf, buf)


def _chip_sum(mine, got, j_arr, tr, name, loss_part=None):
    _, rows, cols = got.shape
    extra = [] if loss_part is None else [loss_part]

    def body(j_ref, mine_ref, got_ref, *rest):
        out_ref = rest[-1]
        j = j_ref[0]
        acc = None
        for jj in range(N_CHIPS):
            term = jnp.where(j == jj, mine_ref[0], got_ref[jj]).astype(F32)
            acc = term if acc is None else acc + term
        out_ref[...] = acc
        if loss_part is not None:
            out_ref[ROW_LOSS:ROW_LOSS + 1, :] = rest[0][0:1, :]

    grid_spec = pltpu.PrefetchScalarGridSpec(
        num_scalar_prefetch=1,
        grid=(rows // tr,),
        in_specs=[pl.BlockSpec((1, tr, cols), lambda r, j_ref: (j_ref[0], r, 0)),
                  pl.BlockSpec((N_CHIPS, tr, cols), lambda r, j_ref: (0, r, 0))] +
                 [pl.BlockSpec(e.shape, lambda r, j_ref: (0, 0)) for e in extra],
        out_specs=pl.BlockSpec((tr, cols), lambda r, j_ref: (r, 0)),
    )
    return pl.pallas_call(
        body,
        name=name,
        grid_spec=grid_spec,
        out_shape=jax.ShapeDtypeStruct((rows, cols), F32),
    )(j_arr, mine, got, *extra)


def _finish_exchange(f_in, f_small):
    def body(fin_ref, fs_ref, rin_ref, os_ref, send_sems, recv_sems, local_sem):
        x, y, c, chips = _position()
        j = 2 * x + y
        me = 2 * j + c
        sibling = (x, y, 1 - c)
        _peer_barrier([sibling] + [(cx, cy, c) for cx, cy in chips])
        local = pltpu.make_async_copy(fs_ref, os_ref.at[me], local_sem)
        local.start()

        def copy(k, src, dst, to):
            return _remote(src, dst, send_sems, recv_sems, k, to)

        first = [copy(0, fin_ref, rin_ref, sibling), copy(1, fs_ref, os_ref.at[me], sibling)]
        first += [copy(2 + k, fs_ref, os_ref.at[me], (cx, cy, c)) for k, (cx, cy) in enumerate(chips)]
        for cp in first:
            cp.start()
        passed = []
        for k, (cx, cy) in enumerate(chips):
            unit = 2 * (2 * cx + cy) + c
            copy(2 + k, fs_ref, os_ref.at[unit], sibling).wait_recv()
            fwd = copy(5 + k, os_ref.at[unit], os_ref.at[unit], sibling)
            fwd.start()
            passed.append(fwd)
        copy(0, fin_ref, rin_ref, sibling).wait_recv()
        copy(1, fs_ref, os_ref.at[2 * j + 1 - c], sibling).wait_recv()
        for k, (cx, cy) in enumerate(chips):
            unit = 2 * (2 * cx + cy) + 1 - c
            copy(5 + k, fs_ref, os_ref.at[unit], sibling).wait_recv()
        for cp in first + passed:
            cp.wait_send()
        local.wait()

    return pl.pallas_call(
        body,
        name="grad_finish_exchange",
        in_specs=[ANY_SPEC] * 2,
        out_specs=[ANY_SPEC] * 2,
        out_shape=[jax.ShapeDtypeStruct(f_in.shape, F32), jax.ShapeDtypeStruct((N_DEV,) + f_small.shape, F32)],
        scratch_shapes=[pltpu.SemaphoreType.DMA((8,)), pltpu.SemaphoreType.DMA((8,)), pltpu.SemaphoreType.DMA],
        compiler_params=pltpu.CompilerParams(collective_id=8),
    )(f_in, f_small)


def _adamw_math(w, g, m, v):
    m = ADAM_B1 * m + (1.0 - ADAM_B1) * g
    v = ADAM_B2 * v + (1.0 - ADAM_B2) * (g * g)
    m_hat = m / (1.0 - ADAM_B1 ** ADAM_STEP)
    v_hat = v / (1.0 - ADAM_B2 ** ADAM_STEP)
    delta = -ADAM_LR * (m_hat / (jnp.sqrt(v_hat) + ADAM_EPS) + ADAM_WD * w)
    return delta, m, v


def _adamw_big(w, g_mine, g_sib, m, v, c_arr, tr, name):
    rows, cols = w.shape
    half = rows // 2
    per = half // tr

    def body(c_ref, w_ref, gm_ref, gs_ref, m_ref, v_ref, g_ref, d_ref, mo_ref, vo_ref):
        g = jnp.where(pl.program_id(0) == c_ref[0], gm_ref[...], gs_ref[...])
        g_ref[...] = g
        d_ref[...], mo_ref[...], vo_ref[...] = _adamw_math(w_ref[...], g, m_ref[...], v_ref[...])

    full = pl.BlockSpec((tr, cols), lambda h, r, c_ref: (h * per + r, 0))
    unit = pl.BlockSpec((tr, cols), lambda h, r, c_ref: (r, 0))
    grid_spec = pltpu.PrefetchScalarGridSpec(
        num_scalar_prefetch=1,
        grid=(2, per),
        in_specs=[full, unit, unit, full, full],
        out_specs=[full] * 4,
    )
    return pl.pallas_call(
        body,
        name=name,
        grid_spec=grid_spec,
        out_shape=[jax.ShapeDtypeStruct(w.shape, F32)] * 4,
    )(c_arr, w, g_mine, g_sib, m, v)


def _adamw_small(j_arr, packed, params):
    names = list(params)
    n = len(names)

    def body(j_ref, pk_ref, *refs):
        ins = refs[:3 * n]
        outs = refs[3 * n:]
        j = j_ref[0]

        def shard(row, rows):
            return jnp.concatenate([pk_ref[2 * j, row:row + rows, :], pk_ref[2 * j + 1, row:row + rows, :]], axis=1)

        def tail_sum(unit, row, rows):
            start = pl.multiple_of(UNIT_ROWS + TAIL_ROWS * unit + row, SUBLANES)
            total = pk_ref[0, pl.ds(start, rows), :]
            for dev in range(1, N_DEV):
                total = total + pk_ref[dev, pl.ds(start, rows), :]
            return total

        for idx, name in enumerate(names):
            if name == "w_rg":
                g = pk_ref[:, ROW_WR:ROW_WR + LANES, :]
            elif name == "w_ig":
                g = pk_ref[:, ROW_WI:ROW_WI + LANES, :]
            elif name == "meta_tokens":
                g = jnp.concatenate([tail_sum(2 * j, 0, N_META), tail_sum(2 * j + 1, 0, N_META)], axis=1)
            elif name == "norm_gain":
                g = jnp.concatenate([tail_sum(u, N_META, SUBLANES)[0:1] for u in range(N_DEV)], axis=1)
            elif name == "conv_w":
                g = shard(ROW_CONV, 4)
            else:
                row = ROW_VEC + VEC_NAMES.index(name)
                g = jnp.concatenate([pk_ref[u, row:row + 1, :] for u in range(N_DEV)], axis=1)
            w_ref, m_ref, v_ref = ins[3 * idx:3 * idx + 3]
            delta, m, v = _adamw_math(w_ref[...], g, m_ref[...], v_ref[...])
            g_ref, d_ref, mo_ref, vo_ref = outs[4 * idx:4 * idx + 4]
            g_ref[...], d_ref[...], mo_ref[...], vo_ref[...] = g, delta, m, v
        total = pk_ref[0, ROW_LOSS:ROW_LOSS + 1, :]
        for u in range(1, N_DEV):
            total = total + pk_ref[u, ROW_LOSS:ROW_LOSS + 1, :]
        outs[4 * n][...] = jnp.broadcast_to(total, (SUBLANES, LANES))

    flat_in, out_shape = [], []
    for name in names:
        w, m, v = params[name]
        flat_in += [w, m, v]
        out_shape += [jax.ShapeDtypeStruct(w.shape, F32)] * 4
    out_shape.append(jax.ShapeDtypeStruct((SUBLANES, LANES), F32))
    res = pl.pallas_call(
        body,
        name="adamw_small",
        in_specs=[SMEM_SPEC, VMEM_SPEC] + [VMEM_SPEC] * (3 * n),
        out_specs=[VMEM_SPEC] * (4 * n + 1),
        out_shape=out_shape,
    )(j_arr, packed, *flat_in)
    return {name: tuple(res[4 * idx:4 * idx + 4]) for idx, name in enumerate(names)}, res[4 * n][0, 0]


def _units(a):
    rows = a.shape[0]
    return jnp.transpose(a.reshape(rows, N_DEV, LANES), (1, 0, 2))


def kernel(x, meta_tokens, norm_gain, w_in, conv_w, conv_b, w_rg, b_rg, w_ig, b_ig, lru_lambda, ret_norm_gain, w_out, final_norm_gain, loss_target, m_meta_tokens, m_norm_gain, m_w_in, m_conv_w, m_conv_b, m_w_rg, m_b_rg, m_w_ig, m_b_ig, m_lru_lambda, m_ret_norm_gain, m_w_out, m_final_norm_gain, v_meta_tokens, v_norm_gain, v_w_in, v_conv_w, v_conv_b, v_w_rg, v_b_rg, v_w_ig, v_b_ig, v_lru_lambda, v_ret_norm_gain, v_w_out, v_final_norm_gain):
    s_len, d = x.shape[1], x.shape[2]
    d_lru = w_rg.shape[1] * w_rg.shape[2]
    d_ret = ret_norm_gain.shape[1]
    d_qk = HEADS * QK_DIM
    tp = s_len + CHUNK
    tm = TOKEN_TILE
    assert tp % tm == 0 and d_lru == HEADS * LANES and d_ret == HEADS * LANES
    ax, ay, ac = lax.axis_index("x"), lax.axis_index("y"), lax.axis_index("c")
    c_arr = jnp.reshape(ac, (1,)).astype(jnp.int32)
    j_arr = jnp.reshape(2 * ax + ay, (1,)).astype(jnp.int32)

    small = jnp.concatenate([meta_tokens, conv_w[0], jnp.zeros((4, meta_tokens.shape[1]), F32)], axis=0)
    wg, sg = _gather_weights(w_in[0], small)
    cols = sg.shape[2]
    meta_full = jnp.transpose(sg[:, :N_META, :], (1, 0, 2)).reshape(N_META, N_CHIPS * cols)
    cw_full = jnp.transpose(sg[:, N_META:N_META + 4, :], (1, 0, 2)).reshape(4, N_CHIPS * cols)
    cw8 = jnp.concatenate([cw_full, jnp.zeros((4, cw_full.shape[1]), F32)], axis=0)

    half = QK_DIM // 2
    inv = ROPE_BASE ** (-jnp.arange(half, dtype=F32) / half)
    pos = (jnp.arange(tp) - PAD_ROWS).astype(F32)
    ang = pos[:, None] * inv[None, :]
    cos_t = jnp.tile(jnp.cos(ang), (1, LANES // half))
    sign = jnp.where((jnp.arange(LANES) % QK_DIM) < half, -1.0, 1.0).astype(F32)
    sin_t = jnp.tile(jnp.sin(ang), (1, LANES // half)) * sign[None, :]
    tables = _ret_tables()
    gain_f = final_norm_gain.reshape(1, d)

    hp, lx, lg, *qkv, rg, wo4 = _in_proj(x[0], meta_full, norm_gain, wg, cos_t, sin_t, w_out[0], tables[1], tables[2],
                                         tm, d_lru, d_qk, d_ret)
    wo = wo4.reshape(N_CHIPS * wo4.shape[1], wo4.shape[2])
    hl, y_lru, *lru_saved = _lru_fwd(lx, lg, cw8, conv_b, w_rg[0], b_rg, w_ig[0], b_ig, lru_lambda, tm)
    o, y_ret, rprev = _ret_fwd(*qkv, rg, ret_norm_gain, tables, tm)
    dh2, dy_lru, dy_ret, dwo, dgf, loss_acc = _out_proj_loss(y_lru, y_ret, hp, loss_target[0], wo, gain_f, tm)

    g_out = dwo.reshape(N_DEV, dwo.shape[0] // N_DEV, dwo.shape[1])
    (dq, dk, dv, drg, dgain), (r_out,) = _ret_bwd(*qkv, rg, o, rprev, dy_ret, ret_norm_gain, cos_t, sin_t, tables,
                                                 tm, ride=_pair_ride([g_out]))
    q_out = _pair_sum(g_out, r_out, c_arr, REDUCE_TILE, "grad_pair_sum_out")
    (dlx, dlg, g_small), (e_out,) = _lru_bwd(lx, lg, hl, dy_lru, lru_saved, cw8, w_rg[0], w_ig[0], lru_lambda,
                                            dgain, dgf, tm, ride=_chip_ride([q_out]))
    f_out = _chip_sum(q_out, e_out, j_arr, REDUCE_TILE, "grad_chip_sum_out")
    dparts = [dlx, dlg, dq, dk, dv, drg]
    dwg, (s_out, r_small) = _in_proj_dw(dparts, hp, norm_gain, wg.shape,
                                        ride=_join_rides(_sibling_ride([f_out]), _pair_ride([g_small])))
    g_in = dwg.reshape(N_DEV, dwg.shape[1] // 2, dwg.shape[2])
    q_in = _pair_exchange_sum(g_in, c_arr, REDUCE_TILE, "grad_pair_exchange_sum_in")
    q_small = _pair_sum(g_small, r_small, c_arr, UNIT_ROWS, "grad_pair_sum_small")
    (grad_x, dmeta, dg1), (e_in, e_small) = _in_proj_dx(dparts, hp, dh2, norm_gain, wg, s_len, tm,
                                                        ride=_chip_ride([q_in, q_small]))
    f_in = _chip_sum(q_in, e_in, j_arr, REDUCE_TILE, "grad_chip_sum_in")
    f_small = _chip_sum(q_small, e_small, j_arr, UNIT_ROWS, "grad_chip_sum_small", loss_part=loss_acc)
    tail = jnp.concatenate([_units(dmeta), _units(dg1), jnp.zeros((N_DEV, TAIL_ROWS - N_META - 1, LANES), F32)],
                           axis=1).reshape(N_DEV * TAIL_ROWS, LANES)
    s_in, o_small = _finish_exchange(f_in, jnp.concatenate([f_small, tail], axis=0))

    res_in = _adamw_big(w_in[0], f_in, s_in, m_w_in[0], v_w_in[0], c_arr, REDUCE_TILE, "adamw_w_in")
    res_out = _adamw_big(w_out[0], f_out, s_out, m_w_out[0], v_w_out[0], c_arr, REDUCE_TILE, "adamw_w_out")
    small_params = {
        "meta_tokens": (meta_tokens, m_meta_tokens, v_meta_tokens),
        "norm_gain": (norm_gain, m_norm_gain, v_norm_gain),
        "conv_w": (conv_w[0], m_conv_w[0], v_conv_w[0]),
        "conv_b": (conv_b, m_conv_b, v_conv_b),
        "w_rg": (w_rg[0], m_w_rg[0], v_w_rg[0]),
        "b_rg": (b_rg, m_b_rg, v_b_rg),
        "w_ig": (w_ig[0], m_w_ig[0], v_w_ig[0]),
        "b_ig": (b_ig, m_b_ig, v_b_ig),
        "lru_lambda": (lru_lambda, m_lru_lambda, v_lru_lambda),
        "ret_norm_gain": (ret_norm_gain, m_ret_norm_gain, v_ret_norm_gain),
        "final_norm_gain": (gain_f, m_final_norm_gain.reshape(1, d), v_final_norm_gain.reshape(1, d)),
    }
    res, loss = _adamw_small(j_arr, o_small, small_params)
    res["w_in"] = tuple(res_in)
    res["w_out"] = tuple(res_out)

    order = ["meta_tokens", "norm_gain", "w_in", "conv_w", "conv_b", "w_rg", "b_rg", "w_ig", "b_ig", "lru_lambda",
             "ret_norm_gain", "w_out", "final_norm_gain"]
    shapes = {"w_in": w_in.shape, "conv_w": conv_w.shape, "w_rg": w_rg.shape, "w_ig": w_ig.shape,
              "w_out": w_out.shape, "final_norm_gain": final_norm_gain.shape}
    outs = [loss, grad_x.reshape(x.shape)]
    for kind in range(4):
        for name in order:
            a = res[name][kind]
            outs.append(a.reshape(shapes[name]) if name in shapes else a)
    return tuple(outs)
```

```python
import functools

import jax
import jax.numpy as jnp
from jax import lax
from jax.experimental import pallas as pl
from jax.experimental.pallas import tpu as pltpu

F32 = jnp.float32
BF16 = jnp.bfloat16

N_META = 16
CHUNK = 128
PAD_ROWS = CHUNK - N_META
HEADS = 8
QK_DIM = 64
LANES = 128
SUBLANES = 8
LRU_C = 8.0
EPS = 1e-6
ROPE_BASE = 10000.0
ADAM_LR = 0.001
ADAM_B1 = 0.9
ADAM_B2 = 0.999
ADAM_EPS = 1e-08
ADAM_WD = 0.01
ADAM_STEP = 10
N_CHIPS = 4
N_DEV = 8
TOKEN_TILE = 384
REDUCE_TILE = 256
VMEM_LIMIT = 58 * 1024 * 1024
MESH = pl.DeviceIdType.MESH

VMEM_SPEC = pl.BlockSpec(memory_space=pltpu.VMEM)
SMEM_SPEC = pl.BlockSpec(memory_space=pltpu.SMEM)
ANY_SPEC = pl.BlockSpec(memory_space=pl.ANY)

ROW_WR, ROW_WI, ROW_META, ROW_CONV, ROW_VEC, UNIT_ROWS = 0, 128, 256, 272, 276, 288
VEC_NAMES = ["norm_gain", "conv_b", "b_rg", "b_ig", "lru_lambda", "ret_norm_gain", "final_norm_gain"]
N_VEC = len(VEC_NAMES)
ROW_LOSS = ROW_VEC + N_VEC
TAIL_ROWS = 24


def _dot(a, b):
    return jnp.dot(a, b, preferred_element_type=F32)


def _dot_nt(a, b):
    return lax.dot_general(a, b, (((1,), (1,)), ((), ())), preferred_element_type=F32)


def _dot_tn(a, b):
    return lax.dot_general(a, b, (((0,), (0,)), ((), ())), preferred_element_type=F32)


def _sigmoid(x):
    return 0.5 * jnp.tanh(0.5 * x) + 0.5


def _shift_down(x, prev8, s):
    rolled = pltpu.roll(x, s, 0)
    rows = lax.broadcasted_iota(jnp.int32, (SUBLANES, x.shape[1]), 0)
    top = jnp.where(rows < s, pltpu.roll(prev8, s, 0), rolled[0:SUBLANES])
    return jnp.concatenate([top, rolled[SUBLANES:]], axis=0)


def _shift_up(x, next8, s):
    n = x.shape[0]
    rolled = pltpu.roll(x, n - s, 0)
    rows = lax.broadcasted_iota(jnp.int32, (SUBLANES, x.shape[1]), 0)
    bot = jnp.where(rows >= SUBLANES - s, pltpu.roll(next8, SUBLANES - s, 0), rolled[n - SUBLANES:n])
    return jnp.concatenate([rolled[:n - SUBLANES], bot], axis=0)


def _rot_partner(t):
    w = t.shape[1]
    lane = lax.broadcasted_iota(jnp.int32, t.shape, 1)
    first = (lane % QK_DIM) < (QK_DIM // 2)
    return jnp.where(first, pltpu.roll(t, w - QK_DIM // 2, 1), pltpu.roll(t, QK_DIM // 2, 1))


def _tile_lanes(t, reps):
    return jnp.concatenate([t] * reps, axis=1)


class _Ride:
    def __init__(self, srcs, dst_shapes, n_copies, make, to_sibling=False, to_chips=False):
        self.srcs, self.dst_shapes, self.n_copies, self.make = list(srcs), list(dst_shapes), n_copies, make
        self.to_sibling, self.to_chips = to_sibling, to_chips

    def peers(self):
        x, y, c, chips = _position()
        return ([(x, y, 1 - c)] if self.to_sibling else []) + ([(cx, cy, c) for cx, cy in chips] if self.to_chips else [])


def _join_rides(a, b):
    def make(src, dst, send_sems, recv_sems, base):
        na, da = len(a.srcs), len(a.dst_shapes)
        return (a.make(src[:na], dst[:da], send_sems, recv_sems, base)
                + b.make(src[na:], dst[da:], send_sems, recv_sems, base + a.n_copies))

    return _Ride(a.srcs + b.srcs, a.dst_shapes + b.dst_shapes, a.n_copies + b.n_copies, make,
                 a.to_sibling or b.to_sibling, a.to_chips or b.to_chips)


def _position():
    x, y, c = lax.axis_index("x"), lax.axis_index("y"), lax.axis_index("c")
    return x, y, c, [(1 - x, y), (x, 1 - y), (1 - x, 1 - y)]


def _peer_barrier(peers):
    barrier = pltpu.get_barrier_semaphore()
    for peer in peers:
        pl.semaphore_signal(barrier, inc=1, device_id=peer, device_id_type=MESH)
    pl.semaphore_wait(barrier, len(peers))


def _remote(src, dst, send_sems, recv_sems, k, to):
    return pltpu.make_async_remote_copy(src_ref=src, dst_ref=dst, send_sem=send_sems.at[k], recv_sem=recv_sems.at[k],
                                        device_id=to, device_id_type=MESH)


def _pair_ride(bufs):
    def make(src, dst, send_sems, recv_sems, base):
        x, y, c, _ = _position()
        return [_remote(src[b].at[2 * jj + 1 - c], dst[b].at[jj], send_sems, recv_sems, base + b * N_CHIPS + jj,
                        (x, y, 1 - c)) for b in range(len(bufs)) for jj in range(N_CHIPS)]

    shapes = [jax.ShapeDtypeStruct((N_CHIPS,) + b.shape[1:], b.dtype) for b in bufs]
    return _Ride(bufs, shapes, N_CHIPS * len(bufs), make, to_sibling=True)


def _chip_ride(bufs):
    def make(src, dst, send_sems, recv_sems, base):
        x, y, c, chips = _position()
        return [_remote(src[b].at[2 * cx + cy], dst[b].at[2 * x + y], send_sems, recv_sems, base + b * 3 + k,
                        (cx, cy, c)) for b in range(len(bufs)) for k, (cx, cy) in enumerate(chips)]

    shapes = [jax.ShapeDtypeStruct(b.shape, b.dtype) for b in bufs]
    return _Ride(bufs, shapes, 3 * len(bufs), make, to_chips=True)


def _sibling_ride(bufs):
    def make(src, dst, send_sems, recv_sems, base):
        x, y, c, _ = _position()
        return [_remote(src[b], dst[b], send_sems, recv_sems, base + b, (x, y, 1 - c)) for b in range(len(bufs))]

    shapes = [jax.ShapeDtypeStruct(b.shape, b.dtype) for b in bufs]
    return _Ride(bufs, shapes, len(bufs), make, to_sibling=True)


def _hosted_call(body, ride, n_steps, *, name, in_specs, out_specs, out_shape, scratch_shapes, args, barrier_id=None):
    params = pltpu.CompilerParams(dimension_semantics=("arbitrary",), vmem_limit_bytes=VMEM_LIMIT,
                                  collective_id=barrier_id if ride is not None else None)
    if ride is None:
        res = pl.pallas_call(body, name=name, grid=(n_steps,), in_specs=list(in_specs), out_specs=list(out_specs),
                             out_shape=list(out_shape), scratch_shapes=list(scratch_shapes),
                             compiler_params=params)(*args)
        return list(res), []
    sizes = [len(in_specs), len(ride.srcs), len(out_specs), len(ride.dst_shapes), len(scratch_shapes), 2]

    def hosted(*refs):
        groups, pos = [], 0
        for n in sizes:
            groups.append(refs[pos:pos + n])
            pos += n
        ins, rin, outs, rout, scr, (send_sems, recv_sems) = groups
        i = pl.program_id(0)

        @pl.when(i == 0)
        def _():
            if barrier_id is not None:
                _peer_barrier(ride.peers())
            for cp in ride.make(rin, rout, send_sems, recv_sems, 0):
                cp.start()

        body(*ins, *outs, *scr)

        @pl.when(i == n_steps - 1)
        def _():
            for cp in ride.make(rin, rout, send_sems, recv_sems, 0):
                cp.wait()

    n_out = len(out_specs)
    res = pl.pallas_call(
        hosted,
        name=name,
        grid=(n_steps,),
        in_specs=list(in_specs) + [ANY_SPEC] * len(ride.srcs),
        out_specs=list(out_specs) + [ANY_SPEC] * len(ride.dst_shapes),
        out_shape=list(out_shape) + ride.dst_shapes,
        scratch_shapes=list(scratch_shapes) + [pltpu.SemaphoreType.DMA((ride.n_copies,)),
                                               pltpu.SemaphoreType.DMA((ride.n_copies,))],
        compiler_params=params,
    )(*args, *ride.srcs)
    return list(res[:n_out]), list(res[n_out:])


def _gather_weights(w_in, small):
    r_in, c_in = w_in.shape
    h_in = r_in // 2
    q_in = h_in // 2

    def body(win_ref, small_ref, wg_ref, sg_ref, mine, send_sems, recv_sems, local_sem):
        x, y, c, chips = _position()
        j = 2 * x + y
        sibling = (x, y, 1 - c)
        xn, yn, dg = chips
        jx, jy, jd = (2 * cx + cy for cx, cy in chips)

        def half(jj, cc):
            return wg_ref.at[jj, pl.ds(cc * h_in, h_in), :]

        def quarter(jj, qq):
            return wg_ref.at[jj, pl.ds(c * h_in + qq * q_in, q_in), :]

        def copy(k, ref, to):
            return _remote(ref, ref, send_sems, recv_sems, k, to)

        def send_mine(k, qq, to):
            rows = pl.ds(c * h_in + qq * q_in, q_in)
            return _remote(mine.at[rows, :], wg_ref.at[j, rows, :], send_sems, recv_sems, k, to)

        def cast_rows(start, rows):
            start = pl.multiple_of(start, q_in)
            mine[pl.ds(start, rows), :] = win_ref[pl.ds(start, rows), :].astype(BF16)

        first = [send_mine(0, 0, (*xn, c)), send_mine(2, 1, (*yn, c)), send_mine(1, 1, (*xn, c)),
                 send_mine(3, 0, (*yn, c))]
        sg_ref[j] = small_ref[...]
        cast_rows(c * h_in, q_in)
        _peer_barrier([sibling] + [(cx, cy, c) for cx, cy in chips])
        first[0].start()
        cast_rows(c * h_in + q_in, q_in)
        for cp in first[1:]:
            cp.start()
        small_copies = [copy(9 + k, sg_ref.at[j], (cx, cy, c)) for k, (cx, cy) in enumerate(chips)]
        for cp in small_copies:
            cp.start()
        first += small_copies
        cast_rows((1 - c) * h_in, h_in)
        keep = pltpu.make_async_copy(mine, wg_ref.at[j], local_sem)
        keep.start()
        copy(0, quarter(jx, 0), sibling).wait_recv()
        along_y = copy(4, quarter(jx, 0), (*yn, c))
        along_y.start()
        copy(2, quarter(jy, 1), sibling).wait_recv()
        along_x = copy(5, quarter(jy, 1), (*xn, c))
        along_x.start()
        copy(1, quarter(jx, 1), sibling).wait_recv()
        to_sib = [copy(6, half(jx, c), sibling)]
        to_sib[-1].start()
        copy(3, quarter(jy, 0), sibling).wait_recv()
        to_sib.append(copy(7, half(jy, c), sibling))
        to_sib[-1].start()
        copy(4, quarter(jd, 0), sibling).wait_recv()
        copy(5, quarter(jd, 1), sibling).wait_recv()
        to_sib.append(copy(8, half(jd, c), sibling))
        to_sib[-1].start()
        for k, jk in enumerate((jx, jy, jd)):
            copy(6 + k, half(jk, 1 - c), sibling).wait_recv()
            copy(9 + k, sg_ref.at[jk], sibling).wait_recv()
        for cp in first + [along_y, along_x] + to_sib:
            cp.wait_send()
        keep.wait()

    return pl.pallas_call(
        body,
        name="gather_weights",
        out_shape=(jax.ShapeDtypeStruct((N_CHIPS, r_in, c_in), BF16),
                   jax.ShapeDtypeStruct((N_CHIPS,) + small.shape, F32)),
        in_specs=[VMEM_SPEC, VMEM_SPEC],
        out_specs=(ANY_SPEC, VMEM_SPEC),
        scratch_shapes=[pltpu.VMEM((r_in, c_in), BF16), pltpu.SemaphoreType.DMA((12,)),
                        pltpu.SemaphoreType.DMA((12,)), pltpu.SemaphoreType.DMA],
        compiler_params=pltpu.CompilerParams(vmem_limit_bytes=VMEM_LIMIT, collective_id=6),
    )(w_in, small)


def _weights_on_demand(i, w_hbm, w_vmem, sems):
    copies = [pltpu.make_async_copy(w_hbm.at[jj], w_vmem.at[jj], sems.at[jj]) for jj in range(w_hbm.shape[0])]

    @pl.when(i == 0)
    def _():
        for cp in copies:
            cp.start()

    waited = set()

    def need(jj):
        if jj not in waited:
            waited.add(jj)

            @pl.when(i == 0)
            def _():
                copies[jj].wait()

    return need


def _proj_segments(d_lru, d_qk, d_ret, chunk_w):
    widths = [d_lru, d_lru, d_qk, d_qk, d_ret, d_ret]
    segs, col = [], 0
    for w in widths:
        parts, off = [], 0
        while off < w:
            jj, inner = divmod(col + off, chunk_w)
            take = min(w - off, chunk_w - inner)
            parts.append((jj, inner, off, take))
            off += take
        segs.append(parts)
        col += w
    return segs


def _in_proj(x2, meta, gain, wg, cos_t, sin_t, w_out, qdec, kdec, tm, d_lru, d_qk, d_ret):
    s_len, d = x2.shape
    tp = s_len + CHUNK
    nt, nb = tp // tm, tm // CHUNK
    segs = _proj_segments(d_lru, d_qk, d_ret, wg.shape[2])
    outs = [(d, F32), (d_lru, F32), (d_lru, F32)] + [(d_qk, BF16)] * 4 + [(d_ret, BF16), (d_ret, F32)]
    r_out, c_out = w_out.shape
    h_out = r_out // 2
    fwd_step = min(6, nt - 1)

    def gather_w_out(i, wout_ref, wo_ref, wob, send_sems, recv_sems, local_sem):
        x, y, c, chips = _position()
        j = 2 * x + y
        sibling = (x, y, 1 - c)

        def half(jj, cc):
            return wo_ref.at[jj, pl.ds(cc * h_out, h_out), :]

        local = pltpu.make_async_copy(wob, wo_ref.at[j], local_sem)
        first = [_remote(wob.at[pl.ds(c * h_out, h_out), :], half(j, c), send_sems, recv_sems, k, (cx, cy, c))
                 for k, (cx, cy) in enumerate(chips)]
        passed = [_remote(half(2 * cx + cy, c), half(2 * cx + cy, c), send_sems, recv_sems, 3 + k, sibling)
                  for k, (cx, cy) in enumerate(chips)]

        @pl.when(i == 0)
        def _():
            wob[...] = wout_ref[...].astype(BF16)
            _peer_barrier([sibling] + [(cx, cy, c) for cx, cy in chips])
            local.start()
            for cp in first:
                cp.start()

        @pl.when(i == fwd_step)
        def _():
            for k, (cx, cy) in enumerate(chips):
                _remote(half(2 * cx + cy, c), half(2 * cx + cy, c), send_sems, recv_sems, k, sibling).wait_recv()
                passed[k].start()

        @pl.when(i == nt - 1)
        def _():
            for k, (cx, cy) in enumerate(chips):
                jk = 2 * cx + cy
                _remote(half(jk, 1 - c), half(jk, 1 - c), send_sems, recv_sems, 3 + k, sibling).wait_recv()
            for cp in first + passed:
                cp.wait_send()
            local.wait()

    def body(*refs):
        xb = refs[:nb]
        meta_ref, g_ref, w_ref, cos_ref, sin_ref, wout_ref, qdec_ref, kdec_ref = refs[nb:nb + 8]
        hp_ref, lx_ref, lg_ref, qb_ref, kb_ref, qd_ref, kd_ref, vb_ref, rg_ref = refs[nb + 8:nb + 17]
        wo_ref, q_s, k_s, wob, send_sems, recv_sems, local_sem, wv, w_sems = refs[nb + 17:]
        i = pl.program_id(0)
        need = _weights_on_demand(i, w_ref, wv, w_sems)
        gather_w_out(i, wout_ref, wo_ref, wob, send_sems, recv_sems, local_sem)
        blocks = [r[...] for r in xb]
        head = jnp.concatenate([jnp.zeros((PAD_ROWS, d), F32), meta_ref[...]], axis=0)
        blocks[0] = jnp.where(i == 0, head, blocks[0])
        h = jnp.concatenate(blocks, axis=0)
        hp_ref[...] = h
        rinv = lax.rsqrt(jnp.mean(h * h, axis=-1, keepdims=True) + EPS)
        u = ((h * rinv) * g_ref[...]).astype(BF16)
        for out_ref, parts in zip([lx_ref, lg_ref, q_s, k_s, vb_ref, rg_ref], segs):
            for jj, inner, off, take in parts:
                need(jj)
                out_ref[:, off:off + take] = _dot(u, wv[jj, :, inner:inner + take]).astype(out_ref.dtype)
        cos = _tile_lanes(cos_ref[...], d_qk // LANES)
        sin = _tile_lanes(sin_ref[...], d_qk // LANES)
        q = q_s[...]
        q = q * cos + _rot_partner(q) * sin
        k = k_s[...]
        k = (k * cos + _rot_partner(k) * sin) * (QK_DIM ** -0.5)
        qb_ref[...] = q.astype(BF16)
        kb_ref[...] = k.astype(BF16)
        qd_ref[...] = (q * jnp.concatenate([qdec_ref[...]] * nb, axis=0)).astype(BF16)
        kd_ref[...] = (k * jnp.concatenate([kdec_ref[...]] * nb, axis=0)).astype(BF16)

    x_specs = [pl.BlockSpec((CHUNK, d), functools.partial(lambda i, b: (jnp.maximum(i * nb + b - 1, 0), 0), b=b))
               for b in range(nb)]
    tile = lambda w: pl.BlockSpec((tm, w), lambda i: (i, 0))
    return pl.pallas_call(
        body,
        name="in_proj",
        grid=(nt,),
        in_specs=x_specs + [pl.BlockSpec(meta.shape, lambda i: (0, 0)),
                            pl.BlockSpec(gain.shape, lambda i: (0, 0)),
                            ANY_SPEC,
                            tile(LANES), tile(LANES),
                            pl.BlockSpec(w_out.shape, lambda i: (0, 0)),
                            pl.BlockSpec(qdec.shape, lambda i: (0, 0)), pl.BlockSpec(kdec.shape, lambda i: (0, 0))],
        out_specs=[tile(w) for w, _ in outs] + [ANY_SPEC],
        out_shape=[jax.ShapeDtypeStruct((tp, w), dt) for w, dt in outs]
                  + [jax.ShapeDtypeStruct((N_CHIPS, r_out, c_out), BF16)],
        scratch_shapes=[pltpu.VMEM((tm, d_qk), F32), pltpu.VMEM((tm, d_qk), F32),
                        pltpu.VMEM((r_out, c_out), BF16), pltpu.SemaphoreType.DMA((6,)),
                        pltpu.SemaphoreType.DMA((6,)), pltpu.SemaphoreType.DMA,
                        pltpu.VMEM(wg.shape, BF16), pltpu.SemaphoreType.DMA((wg.shape[0],))],
        compiler_params=pltpu.CompilerParams(dimension_semantics=("arbitrary",), vmem_limit_bytes=VMEM_LIMIT,
                                             collective_id=7),
    )(*([x2] * nb), meta, gain, wg, cos_t, sin_t, w_out, qdec, kdec)


def _segment_scan(a3, u3, out3, p3, carry, tm, reverse):
    groups = a3.shape[0]
    seg = tm // SUBLANES

    def step(j, state):
        hs, ps = state
        rows = pl.ds((seg - 1 - j) if reverse else j, SUBLANES, stride=seg)
        new_h, new_p = [], []
        for g in range(groups):
            a = a3[g, rows, :]
            h = a * hs[g] + u3[g, rows, :]
            p = ps[g] * a
            out3[g, rows, :] = h
            p3[g, rows, :] = p
            new_h.append(h)
            new_p.append(p)
        return tuple(new_h), tuple(new_p)

    zeros = tuple(jnp.zeros((SUBLANES, LANES), F32) for _ in range(groups))
    ones = tuple(jnp.ones((SUBLANES, LANES), F32) for _ in range(groups))
    lax.fori_loop(0, seg, step, (zeros, ones))
    carries = [carry[:, g * LANES:(g + 1) * LANES] for g in range(groups)]
    for s in (reversed(range(SUBLANES)) if reverse else range(SUBLANES)):
        rows = slice(s * seg, (s + 1) * seg)
        edge = s * seg if reverse else (s + 1) * seg - 1
        for g in range(groups):
            out3[g, rows, :] = out3[g, rows, :] + p3[g, rows, :] * carries[g]
            carries[g] = out3[g, edge:edge + 1, :]
    return jnp.concatenate(carries, axis=1)


def _softplus_neg(lam):
    z = -lam
    e = jnp.exp(-jnp.abs(z))
    e1 = 1.0 + e
    log1p_e = jnp.where(e1 == 1.0, e, jnp.log(e1) * (e / (e1 - 1.0)))
    return jnp.maximum(z, 0.0) + log1p_e


def _lru_fwd(lx, lg, cw, cb, wr, br, wi, bi, lam, tm):
    tp, w = lx.shape
    nt = tp // tm
    per8 = tm // SUBLANES
    n_heads = wr.shape[0]

    def body(lx_ref, lxp_ref, lg_ref, cw_ref, cb_ref, wr_ref, br_ref, wi_ref, bi_ref, lam_ref,
             hl_ref, y_ref, xc_ref, r_ref, ig_ref, a_ref, beta_ref, w4_ref, a_s, u_s, h_s, p_s, carry):
        i = pl.program_id(0)

        @pl.when(i == 0)
        def _():
            carry[...] = jnp.zeros_like(carry)

        sp = _softplus_neg(lam_ref[...])
        row = lax.broadcasted_iota(jnp.int32, (tm, 1), 0) + i * tm
        for hd in range(n_heads):
            hs = slice(hd * LANES, (hd + 1) * LANES)
            lxv = lx_ref[:, hs]
            prev8 = jnp.where(i == 0, 0.0, lxp_ref[:, hs])
            xc = cb_ref[:, hs] + _shift_down(lxv, prev8, 3) * cw_ref[0:1, hs]
            xc = xc + _shift_down(lxv, prev8, 2) * cw_ref[1:2, hs]
            xc = xc + _shift_down(lxv, prev8, 1) * cw_ref[2:3, hs]
            xc = xc + lxv * cw_ref[3:4, hs]
            xc_ref[:, hs] = xc
            xh = xc.astype(BF16)
            r = _sigmoid(_dot(xh, wr_ref[hd].astype(BF16)) + br_ref[:, hs])
            ig = _sigmoid(_dot(xh, wi_ref[hd].astype(BF16)) + bi_ref[:, hs])
            r_ref[:, hs] = r
            ig_ref[:, hs] = ig
            log_a = (-LRU_C * r) * sp[:, hs]
            a = jnp.exp(log_a)
            a_ref[:, hs] = a
            a2 = a * a
            beta2 = jnp.maximum((1.0 + a2) * jnp.tanh(-log_a), 1e-37)
            rsb = lax.rsqrt(beta2)
            beta = beta2 * rsb
            beta_ref[:, hs] = beta
            w4_ref[:, hs] = a2 * rsb
            a_s[hd] = a
            u_s[hd] = jnp.where(row >= PAD_ROWS, beta * ig * xc, 0.0)
        carry[0:1, :] = _segment_scan(a_s, u_s, h_s, p_s, carry[0:1, :], tm, reverse=False)
        for hd in range(n_heads):
            hs = slice(hd * LANES, (hd + 1) * LANES)
            hl = h_s[hd]
            hl_ref[:, hs] = hl
            g = lg_ref[:, hs]
            y_ref[:, hs] = (hl * (g * _sigmoid(g))).astype(BF16)

    tile = pl.BlockSpec((tm, w), lambda i: (i, 0))
    prev = pl.BlockSpec((SUBLANES, w), lambda i: (jnp.maximum(i * per8 - 1, 0), 0))
    vec = pl.BlockSpec((1, w), lambda i: (0, 0))
    mat = pl.BlockSpec(wr.shape, lambda i: (0, 0, 0))
    f32_out = jax.ShapeDtypeStruct((tp, w), F32)
    return pl.pallas_call(
        body,
        name="lru_fwd",
        grid=(nt,),
        in_specs=[tile, prev, tile, pl.BlockSpec(cw.shape, lambda i: (0, 0)), vec, mat, vec, mat, vec, vec],
        out_specs=[tile] * 8,
        out_shape=[f32_out, jax.ShapeDtypeStruct((tp, w), BF16)] + [f32_out] * 6,
        scratch_shapes=[pltpu.VMEM((w // LANES, tm, LANES), F32)] * 4 + [pltpu.VMEM((SUBLANES, w), F32)],
        compiler_params=pltpu.CompilerParams(dimension_semantics=("arbitrary",), vmem_limit_bytes=VMEM_LIMIT),
    )(lx, lx, lg, cw, cb, wr, br, wi, bi, lam)


def _ret_tables():
    log_g = jnp.log1p(-jnp.exp2(-5.0 - jnp.arange(HEADS, dtype=F32)))
    idx = jnp.arange(CHUNK, dtype=F32)
    diff = idx[:, None] - idx[None, :]
    dmask = jnp.where(diff[None] >= 0.0, jnp.exp(jnp.maximum(diff, 0.0)[None] * log_g[:, None, None]), 0.0)
    kdec = jnp.repeat(jnp.exp((CHUNK - 1.0 - idx)[:, None] * log_g[None, :]), QK_DIM, axis=1)
    qdec = jnp.repeat(jnp.exp((idx + 1.0)[:, None] * log_g[None, :]), QK_DIM, axis=1)
    g_chunk = jnp.exp(CHUNK * log_g)
    g_rows = jnp.repeat(g_chunk, QK_DIM).reshape(HEADS // 2, 2 * QK_DIM, 1)
    g_state = jnp.broadcast_to(g_rows, (HEADS // 2, 2 * QK_DIM, 2 * LANES))
    r_head = jnp.arange(2 * QK_DIM)[:, None] // QK_DIM
    c_head = jnp.arange(2 * LANES)[None, :] // LANES
    block_diag = (r_head == c_head).astype(F32)
    return dmask, qdec, kdec, g_state, block_diag


def _head_norm(o_h):
    mu = jnp.mean(o_h, axis=-1, keepdims=True)
    oc = o_h - mu
    var = jnp.mean(oc * oc, axis=-1, keepdims=True)
    rstd = lax.rsqrt(var + EPS)
    return oc * rstd, rstd


def _ret_fwd(qb, kb, qd, kd, vb, rg, gain, tables, tm):
    tp, d_qk = qb.shape
    d_ret = vb.shape[1]
    n_ch = tp // CHUNK
    cps = tm // CHUNK
    n_pairs = HEADS // 2
    dmask, _, _, g_state, block_diag = tables

    def body(q_ref, k_ref, qd_ref, kd_ref, v_ref, rg_ref, gain_ref, dm_ref, gs_ref, bd_ref,
             o_ref, y_ref, rp_ref, state):
        n = pl.program_id(0)

        @pl.when(n == 0)
        def _():
            state[...] = jnp.zeros_like(state)

        lane = lax.broadcasted_iota(jnp.int32, (CHUNK, LANES), 1)
        for ci in range(cps):
            rs = slice(ci * CHUNK, (ci + 1) * CHUNK)
            for p in range(n_pairs):
                qs = slice(p * LANES, (p + 1) * LANES)
                vs = slice(p * 2 * LANES, (p + 1) * 2 * LANES)
                qp, kb = q_ref[rs, qs], k_ref[rs, qs]
                vb = v_ref[rs, vs]
                qd, kd = qd_ref[rs, qs], kd_ref[rs, qs]
                st = state[p]
                st_b = st.astype(BF16)
                rp_ref[ci, p] = st_b
                cross = _dot(qd, st_b)
                for e in range(2):
                    hd = 2 * p + e
                    hs = slice(hd * LANES, (hd + 1) * LANES)
                    es = slice(e * LANES, (e + 1) * LANES)
                    qm = jnp.where((lane // QK_DIM) == e, qp, jnp.zeros_like(qp))
                    s = _dot_nt(qm, kb) * dm_ref[hd]
                    o_h = _dot(s.astype(BF16), vb[:, es]) + cross[:, es]
                    o_ref[rs, hs] = o_h
                    xhat, _ = _head_norm(o_h)
                    g = rg_ref[rs, hs]
                    y_ref[rs, hs] = ((xhat * gain_ref[:, hs]) * (g * _sigmoid(g))).astype(BF16)
                state[p] = gs_ref[p] * st + bd_ref[...] * _dot_tn(kd, vb)

    ch = lambda w: pl.BlockSpec((tm, w), lambda n: (n, 0))
    const2 = lambda a: pl.BlockSpec(a.shape, lambda n: (0, 0))
    const3 = lambda a: pl.BlockSpec(a.shape, lambda n: (0, 0, 0))
    return pl.pallas_call(
        body,
        name="ret_fwd",
        grid=(n_ch // cps,),
        in_specs=[ch(d_qk)] * 4 + [ch(d_ret), ch(d_ret), const2(gain), const3(dmask), const3(g_state),
                                   const2(block_diag)],
        out_specs=[ch(d_ret), ch(d_ret),
                   pl.BlockSpec((cps, n_pairs, 2 * QK_DIM, 2 * LANES), lambda n: (n, 0, 0, 0))],
        out_shape=[jax.ShapeDtypeStruct((tp, d_ret), F32), jax.ShapeDtypeStruct((tp, d_ret), BF16),
                   jax.ShapeDtypeStruct((n_ch, n_pairs, 2 * QK_DIM, 2 * LANES), BF16)],
        scratch_shapes=[pltpu.VMEM((n_pairs, 2 * QK_DIM, 2 * LANES), F32)],
        compiler_params=pltpu.CompilerParams(dimension_semantics=("arbitrary",), vmem_limit_bytes=VMEM_LIMIT),
    )(qb, kb, qd, kd, vb, rg, gain, dmask, g_state, block_diag)


def _out_proj_loss(y_lru, y_ret, hp, tgt, wo, gain_f, tm):
    tp, d = hp.shape
    w_lru = y_lru.shape[1]
    w_mix = wo.shape[0]
    nt, nb = tp // tm, tm // CHUNK

    def body(*refs):
        yl_ref, yr_ref, hp_ref = refs[:3]
        tb = refs[3:3 + nb]
        wo_ref, gf_ref = refs[3 + nb:5 + nb]
        dh2_ref, dyl_ref, dyr_ref, dwo_ref, dgf_ref, loss_ref = refs[5 + nb:]
        i = pl.program_id(0)

        @pl.when(i == 0)
        def _():
            dwo_ref[...] = jnp.zeros_like(dwo_ref)
            dgf_ref[...] = jnp.zeros_like(dgf_ref)
            loss_ref[...] = jnp.zeros_like(loss_ref)

        yl, yr = yl_ref[...], yr_ref[...]
        h2 = hp_ref[...] + _dot(yl, wo_ref[0:w_lru, :]) + _dot(yr, wo_ref[w_lru:w_mix, :])
        rinv = lax.rsqrt(jnp.mean(h2 * h2, axis=-1, keepdims=True) + EPS)
        nrm = h2 * rinv
        gf = gf_ref[...]
        tgt_v = jnp.concatenate([r[...] for r in tb], axis=0)
        row = lax.broadcasted_iota(jnp.int32, (tm, 1), 0) + i * tm
        err = jnp.where(row >= CHUNK, nrm * gf - tgt_v, 0.0)
        loss_ref[...] += 0.5 * jnp.sum(jnp.mean(err * err, axis=-1, keepdims=True))
        dout = err * (1.0 / d)
        dgf_ref[...] += jnp.sum(dout * nrm, axis=0, keepdims=True)
        dn = dout * gf
        dh2 = rinv * (dn - nrm * jnp.mean(dn * nrm, axis=-1, keepdims=True))
        dh2_ref[...] = dh2
        dh2b = dh2.astype(BF16)
        dyl_ref[...] = _dot_nt(dh2b, wo_ref[0:w_lru, :])
        dyr_ref[...] = _dot_nt(dh2b, wo_ref[w_lru:w_mix, :])
        dwo_ref[0:w_lru, :] += _dot_tn(yl, dh2b)
        dwo_ref[w_lru:w_mix, :] += _dot_tn(yr, dh2b)

    tile = lambda w: pl.BlockSpec((tm, w), lambda i: (i, 0))
    t_specs = [pl.BlockSpec((CHUNK, d), functools.partial(lambda i, b: (jnp.maximum(i * nb + b - 1, 0), 0), b=b))
               for b in range(nb)]
    return pl.pallas_call(
        body,
        name="out_proj_loss",
        grid=(nt,),
        in_specs=[tile(w_lru), tile(w_mix - w_lru), tile(d)] + t_specs +
                 [pl.BlockSpec(wo.shape, lambda i: (0, 0)), pl.BlockSpec(gain_f.shape, lambda i: (0, 0))],
        out_specs=[tile(d), tile(w_lru), tile(w_mix - w_lru), pl.BlockSpec(wo.shape, lambda i: (0, 0)),
                   pl.BlockSpec((1, d), lambda i: (0, 0)), pl.BlockSpec((SUBLANES, LANES), lambda i: (0, 0))],
        out_shape=[jax.ShapeDtypeStruct((tp, d), F32), jax.ShapeDtypeStruct((tp, w_lru), F32),
                   jax.ShapeDtypeStruct((tp, w_mix - w_lru), F32), jax.ShapeDtypeStruct(wo.shape, F32),
                   jax.ShapeDtypeStruct((1, d), F32), jax.ShapeDtypeStruct((SUBLANES, LANES), F32)],
        compiler_params=pltpu.CompilerParams(dimension_semantics=("arbitrary",), vmem_limit_bytes=VMEM_LIMIT),
    )(y_lru, y_ret, hp, *([tgt] * nb), wo, gain_f)


def _ret_bwd(qb, kb, qd, kd, vb, rg, o, rprev, dy, gain, cos_t, sin_t, tables, tm, ride=None):
    tp, d_qk = qb.shape
    d_ret = vb.shape[1]
    n_ch = tp // CHUNK
    cps = tm // CHUNK
    n_pairs = HEADS // 2
    dmask, qdec, kdec, g_state, block_diag = tables

    dmask_t = jnp.swapaxes(dmask, 1, 2)

    def body(q_ref, k_ref, qdb_ref, kdb_ref, v_ref, rg_ref, o_ref, rp_ref, dy_ref, gain_ref, cos_ref, sin_ref,
             dm_ref, dmt_ref, qd_ref, kd_ref, gs_ref, bd_ref, dq_ref, dk_ref, dv_ref, drg_ref, dgain_ref, dstate):
        n = pl.program_id(0)

        @pl.when(n == 0)
        def _():
            dstate[...] = jnp.zeros_like(dstate)
            dgain_ref[...] = jnp.zeros_like(dgain_ref)

        lane = lax.broadcasted_iota(jnp.int32, (CHUNK, LANES), 1)
        for ci in reversed(range(cps)):
            rs = slice(ci * CHUNK, (ci + 1) * CHUNK)
            dq_parts, dk_parts = [], []
            for p in range(n_pairs):
                qs = slice(p * LANES, (p + 1) * LANES)
                vs = slice(p * 2 * LANES, (p + 1) * 2 * LANES)
                do_parts = []
                for e in range(2):
                    hd = 2 * p + e
                    hs = slice(hd * LANES, (hd + 1) * LANES)
                    xhat, rstd = _head_norm(o_ref[rs, hs])
                    g = rg_ref[rs, hs]
                    sg = _sigmoid(g)
                    dyh = dy_ref[rs, hs]
                    gn = gain_ref[:, hs]
                    d_on = dyh * (g * sg)
                    drg_ref[rs, hs] = (dyh * (xhat * gn) * (sg * (1.0 + g * (1.0 - sg)))).astype(BF16)
                    dgain_ref[:, hs] += jnp.sum(d_on * xhat, axis=0, keepdims=True)
                    dxh = d_on * gn
                    do_parts.append(rstd * (dxh - jnp.mean(dxh, axis=-1, keepdims=True)
                                            - xhat * jnp.mean(dxh * xhat, axis=-1, keepdims=True)))
                do_b = jnp.concatenate(do_parts, axis=1).astype(BF16)
                qp, kb = q_ref[rs, qs], k_ref[rs, qs]
                vb = v_ref[rs, vs]
                qd, kd = qdb_ref[rs, qs], kdb_ref[rs, qs]
                dst = dstate[p]
                dst_b = dst.astype(BF16)
                dqp = _dot_nt(do_b, rp_ref[ci, p]) * qd_ref[:, qs]
                dkp = _dot_nt(vb, dst_b) * kd_ref[:, qs]
                dvp = _dot(kd, dst_b)
                dv_parts = []
                for e in range(2):
                    hd = 2 * p + e
                    es = slice(e * LANES, (e + 1) * LANES)
                    mine = (lane // QK_DIM) == e
                    qm = jnp.where(mine, qp, jnp.zeros_like(qp))
                    km = jnp.where(mine, kb, jnp.zeros_like(kb))
                    ds = (_dot_nt(do_b[:, es], vb[:, es]) * dm_ref[hd]).astype(BF16)
                    s_t = (_dot_nt(kb, qm) * dmt_ref[hd]).astype(BF16)
                    ds_t = (_dot_nt(vb[:, es], do_b[:, es]) * dmt_ref[hd]).astype(BF16)
                    dv_parts.append(dvp[:, es] + _dot(s_t, do_b[:, es]))
                    dqp = dqp + _dot(ds, km)
                    dkp = dkp + _dot(ds_t, qm)
                dv_ref[rs, vs] = jnp.concatenate(dv_parts, axis=1).astype(BF16)
                dstate[p] = gs_ref[p] * dst + bd_ref[...] * _dot_tn(qd, do_b)
                dq_parts.append(dqp)
                dk_parts.append(dkp)
            cos = _tile_lanes(cos_ref[rs, :], d_qk // LANES)
            sin = _tile_lanes(sin_ref[rs, :], d_qk // LANES)
            dq = jnp.concatenate(dq_parts, axis=1)
            dk = jnp.concatenate(dk_parts, axis=1) * (QK_DIM ** -0.5)
            dq_ref[rs, :] = (dq * cos + _rot_partner(dq * sin)).astype(BF16)
            dk_ref[rs, :] = (dk * cos + _rot_partner(dk * sin)).astype(BF16)

    last = n_ch // cps - 1
    ch = lambda w: pl.BlockSpec((tm, w), lambda n: (last - n, 0))
    const2 = lambda a: pl.BlockSpec(a.shape, lambda n: (0, 0))
    const3 = lambda a: pl.BlockSpec(a.shape, lambda n: (0, 0, 0))
    return _hosted_call(
        body, ride, n_ch // cps,
        name="ret_bwd", barrier_id=1,
        in_specs=[ch(d_qk)] * 4 + [ch(d_ret), ch(d_ret), ch(d_ret),
                  pl.BlockSpec((cps, n_pairs, 2 * QK_DIM, 2 * LANES), lambda n: (last - n, 0, 0, 0)),
                  ch(d_ret), const2(gain), ch(LANES), ch(LANES),
                  const3(dmask), const3(dmask_t), const2(qdec), const2(kdec), const3(g_state), const2(block_diag)],
        out_specs=[ch(d_qk), ch(d_qk), ch(d_ret), ch(d_ret), pl.BlockSpec((1, d_ret), lambda n: (0, 0))],
        out_shape=[jax.ShapeDtypeStruct((tp, d_qk), BF16), jax.ShapeDtypeStruct((tp, d_qk), BF16),
                   jax.ShapeDtypeStruct((tp, d_ret), BF16), jax.ShapeDtypeStruct((tp, d_ret), BF16),
                   jax.ShapeDtypeStruct((1, d_ret), F32)],
        scratch_shapes=[pltpu.VMEM((n_pairs, 2 * QK_DIM, 2 * LANES), F32)],
        args=(qb, kb, qd, kd, vb, rg, o, rprev, dy, gain, cos_t, sin_t, dmask, dmask_t, qdec, kdec, g_state,
              block_diag),
    )


def _lru_bwd(lx, lg, hl, dy, saved, cw, wr, wi, lam, dgain, dgf, tm, ride=None):
    tp, w = lx.shape
    nt = tp // tm
    per8 = tm // SUBLANES
    n_heads = wr.shape[0]
    vec_row = {name: ROW_VEC + VEC_NAMES.index(name) for name in VEC_NAMES}

    def body(lx_ref, lg_ref, hl_ref, hlp_ref, dy_ref, xc_ref, r_ref, ig_ref, a_ref, beta_ref, w4_ref,
             cw_ref, wr_ref, wi_ref, lam_ref, dgain_ref, dgf_ref,
             dlx_ref, dlg_ref, pk_ref,
             g_s, b_s, carry, dxc_next, a_next):
        i = pl.program_id(0)
        first_tile = i == nt - 1
        heads = [slice(hd * LANES, (hd + 1) * LANES) for hd in range(n_heads)]

        def add_row(hd, row, value):
            pk_ref[hd, row:row + 1, :] += value

        @pl.when(i == 0)
        def _():
            carry[...] = jnp.zeros_like(carry)
            dxc_next[...] = jnp.zeros_like(dxc_next)
            a_next[...] = jnp.zeros_like(a_next)
            pk_ref[...] = jnp.zeros_like(pk_ref)
            for hd, hs in enumerate(heads):
                add_row(hd, vec_row["ret_norm_gain"], dgain_ref[:, hs])
                add_row(hd, vec_row["final_norm_gain"], dgf_ref[:, hs])

        for hd, hs in enumerate(heads):
            g = lg_ref[:, hs]
            sg = _sigmoid(g)
            dyv = dy_ref[:, hs]
            dlg_ref[:, hs] = (dyv * hl_ref[:, hs] * (sg * (1.0 + g * (1.0 - sg)))).astype(BF16)
            g_s[hd] = dyv * (g * sg)
            b_s[hd] = _shift_up(a_ref[:, hs], a_next[:, hs], 1)
        carry[0:1, :] = _segment_scan(b_s, g_s, g_s, b_s, carry[0:1, :], tm, reverse=True)
        a_next[...] = a_ref[0:SUBLANES, :]
        row = lax.broadcasted_iota(jnp.int32, (tm, 1), 0) + (nt - 1 - i) * tm
        lam_v = lam_ref[...]
        dlam_scale = LRU_C * _sigmoid(-lam_v)
        dr_scale = -LRU_C * _softplus_neg(lam_v)
        for hd, hs in enumerate(heads):
            a, beta, r, ig, xc = a_ref[:, hs], beta_ref[:, hs], r_ref[:, hs], ig_ref[:, hs], xc_ref[:, hs]
            dh = g_s[hd]
            hprev = _shift_down(hl_ref[:, hs], jnp.where(first_tile, 0.0, hlp_ref[:, hs]), 1)
            du = jnp.where(row >= PAD_ROWS, dh, 0.0)
            dbeta = du * ig * xc
            d_ig = du * beta * xc
            dxc = du * beta * ig
            dloga = (dh * hprev) * a - dbeta * w4_ref[:, hs]
            add_row(hd, vec_row["lru_lambda"], jnp.sum(dloga * r, axis=0, keepdims=True) * dlam_scale[:, hs])
            dpr = (dloga * dr_scale[:, hs]) * r * (1.0 - r)
            dpi = d_ig * ig * (1.0 - ig)
            add_row(hd, vec_row["b_rg"], jnp.sum(dpr, axis=0, keepdims=True))
            add_row(hd, vec_row["b_ig"], jnp.sum(dpi, axis=0, keepdims=True))
            xh, dprh, dpih = xc.astype(BF16), dpr.astype(BF16), dpi.astype(BF16)
            pk_ref[hd, ROW_WR:ROW_WR + LANES, :] += _dot_tn(xh, dprh)
            pk_ref[hd, ROW_WI:ROW_WI + LANES, :] += _dot_tn(xh, dpih)
            dxc = dxc + _dot_nt(dprh, wr_ref[hd].astype(BF16)) + _dot_nt(dpih, wi_ref[hd].astype(BF16))
            nxt = dxc_next[:, hs]
            up1, up2, up3 = _shift_up(dxc, nxt, 1), _shift_up(dxc, nxt, 2), _shift_up(dxc, nxt, 3)
            dlx = dxc * cw_ref[3:4, hs]
            dlx = dlx + up1 * cw_ref[2:3, hs]
            dlx = dlx + up2 * cw_ref[1:2, hs]
            dlx = dlx + up3 * cw_ref[0:1, hs]
            dlx_ref[:, hs] = dlx.astype(BF16)
            dxc_next[:, hs] = dxc[0:SUBLANES]
            lxv = lx_ref[:, hs]
            add_row(hd, vec_row["conv_b"], jnp.sum(dxc, axis=0, keepdims=True))
            for kk, shifted in enumerate((up3, up2, up1, dxc)):
                add_row(hd, ROW_CONV + kk, jnp.sum(shifted * lxv, axis=0, keepdims=True))

    last = nt - 1
    tile = pl.BlockSpec((tm, w), lambda i: (last - i, 0))
    prev = pl.BlockSpec((SUBLANES, w), lambda i: (jnp.maximum((last - i) * per8 - 1, 0), 0))
    vec = pl.BlockSpec((1, w), lambda i: (0, 0))
    mat = pl.BlockSpec(wr.shape, lambda i: (0, 0, 0))
    cwb = pl.BlockSpec(cw.shape, lambda i: (0, 0))
    packed = (n_heads, UNIT_ROWS, LANES)
    return _hosted_call(
        body, ride, nt,
        name="lru_bwd", barrier_id=2,
        in_specs=[tile, tile, tile, prev, tile] + [tile] * 6 + [cwb, mat, mat, vec, vec, vec],
        out_specs=[tile, tile, pl.BlockSpec(packed, lambda i: (0, 0, 0))],
        out_shape=[jax.ShapeDtypeStruct((tp, w), BF16), jax.ShapeDtypeStruct((tp, w), BF16),
                   jax.ShapeDtypeStruct(packed, F32)],
        scratch_shapes=[pltpu.VMEM((w // LANES, tm, LANES), F32)] * 2 + [pltpu.VMEM((SUBLANES, w), F32)] * 3,
        args=(lx, lg, hl, hl, dy, *saved, cw, wr, wi, lam, dgain, dgf),
    )


def _in_proj_dw(dparts, hp, gain, wg_shape, ride=None):
    tp, d = hp.shape
    n_ch = tp // CHUNK
    per = next(p for p in (4, 2, 5, 3, 1) if (n_ch - 1) % p == 0)
    n_steps = 1 + (n_ch - 1) // per
    widths = [p.shape[1] for p in dparts]
    segs = _proj_segments(widths[0], widths[2], widths[4], wg_shape[2])

    def body(*refs):
        dp = [refs[p * per:(p + 1) * per] for p in range(6)]
        hp_b = refs[6 * per:7 * per]
        g_ref, dwg_ref, acc, sem = refs[7 * per:]
        i = pl.program_id(0)

        def accumulate(blocks):
            h = jnp.concatenate([hp_b[b][...] for b in blocks], axis=0)
            rinv = lax.rsqrt(jnp.mean(h * h, axis=-1, keepdims=True) + EPS)
            u = ((h * rinv) * g_ref[...]).astype(BF16)
            for p_refs, parts in zip(dp, segs):
                for jj, inner, off, take in parts:
                    seg = jnp.concatenate([p_refs[b][:, off:off + take] for b in blocks], axis=0)
                    acc[jj, :, inner:inner + take] += _dot_tn(u, seg)

        @pl.when(i == 0)
        def _():
            acc[...] = jnp.zeros_like(acc)
            accumulate([0])

        @pl.when(i > 0)
        def _():
            accumulate(list(range(per)))

        @pl.when(i == n_steps - 1)
        def _():
            cp = pltpu.make_async_copy(acc, dwg_ref, sem)
            cp.start()
            cp.wait()

    def blocks(w):
        return [pl.BlockSpec((CHUNK, w), functools.partial(
            lambda i, b: (jnp.where(i == 0, b, per * (i - 1) + 1 + b), 0), b=b)) for b in range(per)]

    in_specs, args = [], []
    for a, w in list(zip(dparts, widths)) + [(hp, d)]:
        in_specs += blocks(w)
        args += [a] * per
    outs, rides = _hosted_call(
        body, ride, n_steps,
        name="in_proj_dw", barrier_id=3,
        in_specs=in_specs + [pl.BlockSpec(gain.shape, lambda i: (0, 0))],
        out_specs=[ANY_SPEC],
        out_shape=[jax.ShapeDtypeStruct(wg_shape, F32)],
        scratch_shapes=[pltpu.VMEM(wg_shape, F32), pltpu.SemaphoreType.DMA],
        args=(*args, gain),
    )
    return outs[0], rides


def _in_proj_dx(dparts, hp, dh2, gain, wg, s_len, tm, ride=None):
    tp, d = hp.shape
    nt = tp // tm
    widths = [p.shape[1] for p in dparts]
    segs = _proj_segments(widths[0], widths[2], widths[4], wg.shape[2])

    def body(*refs):
        dp = refs[:6]
        hp_ref, dh2_ref, g_ref, w_ref = refs[6:10]
        gx_ref, dmeta_ref, dg_ref = refs[10:13]
        stage, sems, wv, w_sems = refs[13:]
        i = pl.program_id(0)
        need = _weights_on_demand(i, w_ref, wv, w_sems)

        @pl.when(i == 0)
        def _():
            dg_ref[...] = jnp.zeros_like(dg_ref)

        h = hp_ref[...]
        rinv = lax.rsqrt(jnp.mean(h * h, axis=-1, keepdims=True) + EPS)
        nrm = h * rinv
        gv = g_ref[...]
        du = jnp.zeros((tm, d), F32)
        for p_ref, parts in zip(dp, segs):
            for jj, inner, off, take in parts:
                need(jj)
                du = du + _dot_nt(p_ref[:, off:off + take], wv[jj, :, inner:inner + take])
        dg_ref[...] += jnp.sum(du * nrm, axis=0, keepdims=True)
        dn = du * gv
        dh = dh2_ref[...] + rinv * (dn - nrm * jnp.mean(dn * nrm, axis=-1, keepdims=True))

        def first_copy():
            return pltpu.make_async_copy(stage.at[0, pl.ds(CHUNK, tm - CHUNK), :],
                                         gx_ref.at[pl.ds(0, tm - CHUNK), :], sems.at[0])

        def tile_copy(slot, start):
            return pltpu.make_async_copy(stage.at[slot], gx_ref.at[pl.ds(start, tm), :], sems.at[slot])

        @pl.when(i == 0)
        def _():
            dmeta_ref[...] = dh[PAD_ROWS:CHUNK]
            stage[0] = dh
            first_copy().start()

        @pl.when(i > 0)
        def _():
            slot = 1 + i % 2

            @pl.when(i >= 3)
            def _():
                tile_copy(slot, 0).wait()

            stage[slot] = dh
            tile_copy(slot, pl.multiple_of(i * tm - CHUNK, CHUNK)).start()

        @pl.when(i == nt - 1)
        def _():
            first_copy().wait()
            for step in (nt - 2, nt - 1):
                if step >= 1:
                    tile_copy(1 + step % 2, 0).wait()

    tile = lambda w: pl.BlockSpec((tm, w), lambda i: (i, 0))
    return _hosted_call(
        body, ride, nt,
        name="in_proj_dx", barrier_id=4,
        in_specs=[tile(w) for w in widths] + [tile(d), tile(d), pl.BlockSpec(gain.shape, lambda i: (0, 0)),
                                              ANY_SPEC],
        out_specs=[ANY_SPEC, pl.BlockSpec((N_META, d), lambda i: (0, 0)), pl.BlockSpec((1, d), lambda i: (0, 0))],
        out_shape=[jax.ShapeDtypeStruct((s_len, d), F32), jax.ShapeDtypeStruct((N_META, d), F32),
                   jax.ShapeDtypeStruct((1, d), F32)],
        scratch_shapes=[pltpu.VMEM((3, tm, d), F32), pltpu.SemaphoreType.DMA((3,)),
                        pltpu.VMEM(wg.shape, BF16), pltpu.SemaphoreType.DMA((wg.shape[0],))],
        args=(*dparts, hp, dh2, gain, wg),
    )


def _pair_sum(buf, recv, c_arr, tr, name):
    _, rows, cols = buf.shape

    def body(c_ref, mine_ref, got_ref, out_ref):
        out_ref[...] = (mine_ref[...] + got_ref[...]).astype(BF16)

    grid_spec = pltpu.PrefetchScalarGridSpec(
        num_scalar_prefetch=1,
        grid=(N_CHIPS, rows // tr),
        in_specs=[pl.BlockSpec((1, tr, cols), lambda jj, r, c_ref: (2 * jj + c_ref[0], r, 0)),
                  pl.BlockSpec((1, tr, cols), lambda jj, r, c_ref: (jj, r, 0))],
        out_specs=pl.BlockSpec((1, tr, cols), lambda jj, r, c_ref: (jj, r, 0)),
    )
    return pl.pallas_call(
        body,
        name=name,
        grid_spec=grid_spec,
        out_shape=jax.ShapeDtypeStruct((N_CHIPS, rows, cols), BF16),
    )(c_arr, buf, recv)


def _pair_exchange_sum(buf, c_arr, tr, name):
    _, rows, cols = buf.shape
    per = rows // tr

    def body(c_ref, src_ref, mine_ref, out_ref, got, send_sems, recv_sems):
        jj, r = pl.program_id(0), pl.program_id(1)
        x, y, c, _ = _position()
        copies = [_remote(src_ref.at[2 * k + 1 - c], got.at[k], send_sems, recv_sems, k, (x, y, 1 - c))
                  for k in range(N_CHIPS)]

        @pl.when((jj == 0) & (r == 0))
        def _():
            _peer_barrier([(x, y, 1 - c)])
            for cp in copies:
                cp.start()

        for k in range(N_CHIPS):
            @pl.when((jj == k) & (r == 0))
            def _():
                copies[k].wait_recv()

        rows_r = pl.ds(pl.multiple_of(r * tr, tr), tr)
        out_ref[0] = (mine_ref[0] + got[jj, rows_r, :]).astype(BF16)

        @pl.when((jj == N_CHIPS - 1) & (r == per - 1))
        def _():
            for cp in copies:
                cp.wait_send()

    grid_spec = pltpu.PrefetchScalarGridSpec(
        num_scalar_prefetch=1,
        grid=(N_CHIPS, per),
        in_specs=[ANY_SPEC, pl.BlockSpec((1, tr, cols), lambda jj, r, c_ref: (2 * jj + c_ref[0], r, 0))],
        out_specs=pl.BlockSpec((1, tr, cols), lambda jj, r, c_ref: (jj, r, 0)),
        scratch_shapes=[pltpu.VMEM((N_CHIPS, rows, cols), F32), pltpu.SemaphoreType.DMA((N_CHIPS,)),
                        pltpu.SemaphoreType.DMA((N_CHIPS,))],
    )
    return pl.pallas_call(
        body,
        name=name,
        grid_spec=grid_spec,
        out_shape=jax.ShapeDtypeStruct((N_CHIPS, rows, cols), BF16),
        compiler_params=pltpu.CompilerParams(dimension_semantics=("arbitrary", "arbitrary"),
                                             vmem_limit_bytes=VMEM_LIMIT, collective_id=5),
    )(c_arr, buf, buf)


def _chip_sum(mine, got, j_arr, tr, name, loss_part=None):
    _, rows, cols = got.shape
    extra = [] if loss_part is None else [loss_part]

    def body(j_ref, mine_ref, got_ref, *rest):
        out_ref = rest[-1]
        j = j_ref[0]
        acc = None
        for jj in range(N_CHIPS):
            term = jnp.where(j == jj, mine_ref[0], got_ref[jj]).astype(F32)
            acc = term if acc is None else acc + term
        out_ref[...] = acc
        if loss_part is not None:
            out_ref[ROW_LOSS:ROW_LOSS + 1, :] = rest[0][0:1, :]

    grid_spec = pltpu.PrefetchScalarGridSpec(
        num_scalar_prefetch=1,
        grid=(rows // tr,),
        in_specs=[pl.BlockSpec((1, tr, cols), lambda r, j_ref: (j_ref[0], r, 0)),
                  pl.BlockSpec((N_CHIPS, tr, cols), lambda r, j_ref: (0, r, 0))] +
                 [pl.BlockSpec(e.shape, lambda r, j_ref: (0, 0)) for e in extra],
        out_specs=pl.BlockSpec((tr, cols), lambda r, j_ref: (r, 0)),
    )
    return pl.pallas_call(
        body,
        name=name,
        grid_spec=grid_spec,
        out_shape=jax.ShapeDtypeStruct((rows, cols), F32),
    )(j_arr, mine, got, *extra)


def _finish_exchange(f_in, f_small):
    def body(fin_ref, fs_ref, rin_ref, os_ref, send_sems, recv_sems, local_sem):
        x, y, c, chips = _position()
        j = 2 * x + y
        me = 2 * j + c
        sibling = (x, y, 1 - c)
        _peer_barrier([sibling] + [(cx, cy, c) for cx, cy in chips])
        local = pltpu.make_async_copy(fs_ref, os_ref.at[me], local_sem)
        local.start()

        def copy(k, src, dst, to):
            return _remote(src, dst, send_sems, recv_sems, k, to)

        first = [copy(0, fin_ref, rin_ref, sibling), copy(1, fs_ref, os_ref.at[me], sibling)]
        first += [copy(2 + k, fs_ref, os_ref.at[me], (cx, cy, c)) for k, (cx, cy) in enumerate(chips)]
        for cp in first:
            cp.start()
        passed = []
        for k, (cx, cy) in enumerate(chips):
            unit = 2 * (2 * cx + cy) + c
            copy(2 + k, fs_ref, os_ref.at[unit], sibling).wait_recv()
            fwd = copy(5 + k, os_ref.at[unit], os_ref.at[unit], sibling)
            fwd.start()
            passed.append(fwd)
        copy(0, fin_ref, rin_ref, sibling).wait_recv()
        copy(1, fs_ref, os_ref.at[2 * j + 1 - c], sibling).wait_recv()
        for k, (cx, cy) in enumerate(chips):
            unit = 2 * (2 * cx + cy) + 1 - c
            copy(5 + k, fs_ref, os_ref.at[unit], sibling).wait_recv()
        for cp in first + passed:
            cp.wait_send()
        local.wait()

    return pl.pallas_call(
        body,
        name="grad_finish_exchange",
        in_specs=[ANY_SPEC] * 2,
        out_specs=[ANY_SPEC] * 2,
        out_shape=[jax.ShapeDtypeStruct(f_in.shape, F32), jax.ShapeDtypeStruct((N_DEV,) + f_small.shape, F32)],
        scratch_shapes=[pltpu.SemaphoreType.DMA((8,)), pltpu.SemaphoreType.DMA((8,)), pltpu.SemaphoreType.DMA],
        compiler_params=pltpu.CompilerParams(collective_id=8),
    )(f_in, f_small)


def _adamw_math(w, g, m, v):
    m = ADAM_B1 * m + (1.0 - ADAM_B1) * g
    v = ADAM_B2 * v + (1.0 - ADAM_B2) * (g * g)
    m_hat = m / (1.0 - ADAM_B1 ** ADAM_STEP)
    v_hat = v / (1.0 - ADAM_B2 ** ADAM_STEP)
    delta = -ADAM_LR * (m_hat / (jnp.sqrt(v_hat) + ADAM_EPS) + ADAM_WD * w)
    return delta, m, v


def _adamw_big(w, g_mine, g_sib, m, v, c_arr, tr, name):
    rows, cols = w.shape
    half = rows // 2
    per = half // tr

    def body(c_ref, w_ref, gm_ref, gs_ref, m_ref, v_ref, g_ref, d_ref, mo_ref, vo_ref):
        g = jnp.where(pl.program_id(0) == c_ref[0], gm_ref[...], gs_ref[...])
        g_ref[...] = g
        d_ref[...], mo_ref[...], vo_ref[...] = _adamw_math(w_ref[...], g, m_ref[...], v_ref[...])

    full = pl.BlockSpec((tr, cols), lambda h, r, c_ref: (h * per + r, 0))
    unit = pl.BlockSpec((tr, cols), lambda h, r, c_ref: (r, 0))
    grid_spec = pltpu.PrefetchScalarGridSpec(
        num_scalar_prefetch=1,
        grid=(2, per),
        in_specs=[full, unit, unit, full, full],
        out_specs=[full] * 4,
    )
    return pl.pallas_call(
        body,
        name=name,
        grid_spec=grid_spec,
        out_shape=[jax.ShapeDtypeStruct(w.shape, F32)] * 4,
    )(c_arr, w, g_mine, g_sib, m, v)


def _adamw_small(j_arr, packed, params):
    names = list(params)
    n = len(names)

    def body(j_ref, pk_ref, *refs):
        ins = refs[:3 * n]
        outs = refs[3 * n:]
        j = j_ref[0]

        def shard(row, rows):
            return jnp.concatenate([pk_ref[2 * j, row:row + rows, :], pk_ref[2 * j + 1, row:row + rows, :]], axis=1)

        def tail_sum(unit, row, rows):
            start = pl.multiple_of(UNIT_ROWS + TAIL_ROWS * unit + row, SUBLANES)
            total = pk_ref[0, pl.ds(start, rows), :]
            for dev in range(1, N_DEV):
                total = total + pk_ref[dev, pl.ds(start, rows), :]
            return total

        for idx, name in enumerate(names):
            if name == "w_rg":
                g = pk_ref[:, ROW_WR:ROW_WR + LANES, :]
            elif name == "w_ig":
                g = pk_ref[:, ROW_WI:ROW_WI + LANES, :]
            elif name == "meta_tokens":
                g = jnp.concatenate([tail_sum(2 * j, 0, N_META), tail_sum(2 * j + 1, 0, N_META)], axis=1)
            elif name == "norm_gain":
                g = jnp.concatenate([tail_sum(u, N_META, SUBLANES)[0:1] for u in range(N_DEV)], axis=1)
            elif name == "conv_w":
                g = shard(ROW_CONV, 4)
            else:
                row = ROW_VEC + VEC_NAMES.index(name)
                g = jnp.concatenate([pk_ref[u, row:row + 1, :] for u in range(N_DEV)], axis=1)
            w_ref, m_ref, v_ref = ins[3 * idx:3 * idx + 3]
            delta, m, v = _adamw_math(w_ref[...], g, m_ref[...], v_ref[...])
            g_ref, d_ref, mo_ref, vo_ref = outs[4 * idx:4 * idx + 4]
            g_ref[...], d_ref[...], mo_ref[...], vo_ref[...] = g, delta, m, v
        total = pk_ref[0, ROW_LOSS:ROW_LOSS + 1, :]
        for u in range(1, N_DEV):
            total = total + pk_ref[u, ROW_LOSS:ROW_LOSS + 1, :]
        outs[4 * n][...] = jnp.broadcast_to(total, (SUBLANES, LANES))

    flat_in, out_shape = [], []
    for name in names:
        w, m, v = params[name]
        flat_in += [w, m, v]
        out_shape += [jax.ShapeDtypeStruct(w.shape, F32)] * 4
    out_shape.append(jax.ShapeDtypeStruct((SUBLANES, LANES), F32))
    res = pl.pallas_call(
        body,
        name="adamw_small",
        in_specs=[SMEM_SPEC, VMEM_SPEC] + [VMEM_SPEC] * (3 * n),
        out_specs=[VMEM_SPEC] * (4 * n + 1),
        out_shape=out_shape,
    )(j_arr, packed, *flat_in)
    return {name: tuple(res[4 * idx:4 * idx + 4]) for idx, name in enumerate(names)}, res[4 * n][0, 0]


def _units(a):
    rows = a.shape[0]
    return jnp.transpose(a.reshape(rows, N_DEV, LANES), (1, 0, 2))


def kernel(x, meta_tokens, norm_gain, w_in, conv_w, conv_b, w_rg, b_rg, w_ig, b_ig, lru_lambda, ret_norm_gain, w_out, final_norm_gain, loss_target, m_meta_tokens, m_norm_gain, m_w_in, m_conv_w, m_conv_b, m_w_rg, m_b_rg, m_w_ig, m_b_ig, m_lru_lambda, m_ret_norm_gain, m_w_out, m_final_norm_gain, v_meta_tokens, v_norm_gain, v_w_in, v_conv_w, v_conv_b, v_w_rg, v_b_rg, v_w_ig, v_b_ig, v_lru_lambda, v_ret_norm_gain, v_w_out, v_final_norm_gain):
    s_len, d = x.shape[1], x.shape[2]
    d_lru = w_rg.shape[1] * w_rg.shape[2]
    d_ret = ret_norm_gain.shape[1]
    d_qk = HEADS * QK_DIM
    tp = s_len + CHUNK
    tm = TOKEN_TILE
    assert tp % tm == 0 and d_lru == HEADS * LANES and d_ret == HEADS * LANES
    ax, ay, ac = lax.axis_index("x"), lax.axis_index("y"), lax.axis_index("c")
    c_arr = jnp.reshape(ac, (1,)).astype(jnp.int32)
    j_arr = jnp.reshape(2 * ax + ay, (1,)).astype(jnp.int32)

    small = jnp.concatenate([meta_tokens, conv_w[0], jnp.zeros((4, meta_tokens.shape[1]), F32)], axis=0)
    wg, sg = _gather_weights(w_in[0], small)
    cols = sg.shape[2]
    meta_full = jnp.transpose(sg[:, :N_META, :], (1, 0, 2)).reshape(N_META, N_CHIPS * cols)
    cw_full = jnp.transpose(sg[:, N_META:N_META + 4, :], (1, 0, 2)).reshape(4, N_CHIPS * cols)
    cw8 = jnp.concatenate([cw_full, jnp.zeros((4, cw_full.shape[1]), F32)], axis=0)

    half = QK_DIM // 2
    inv = ROPE_BASE ** (-jnp.arange(half, dtype=F32) / half)
    pos = (jnp.arange(tp) - PAD_ROWS).astype(F32)
    ang = pos[:, None] * inv[None, :]
    cos_t = jnp.tile(jnp.cos(ang), (1, LANES // half))
    sign = jnp.where((jnp.arange(LANES) % QK_DIM) < half, -1.0, 1.0).astype(F32)
    sin_t = jnp.tile(jnp.sin(ang), (1, LANES // half)) * sign[None, :]
    tables = _ret_tables()
    gain_f = final_norm_gain.reshape(1, d)

    hp, lx, lg, *qkv, rg, wo4 = _in_proj(x[0], meta_full, norm_gain, wg, cos_t, sin_t, w_out[0], tables[1], tables[2],
                                         tm, d_lru, d_qk, d_ret)
    wo = wo4.reshape(N_CHIPS * wo4.shape[1], wo4.shape[2])
    hl, y_lru, *lru_saved = _lru_fwd(lx, lg, cw8, conv_b, w_rg[0], b_rg, w_ig[0], b_ig, lru_lambda, tm)
    o, y_ret, rprev = _ret_fwd(*qkv, rg, ret_norm_gain, tables, tm)
    dh2, dy_lru, dy_ret, dwo, dgf, loss_acc = _out_proj_loss(y_lru, y_ret, hp, loss_target[0], wo, gain_f, tm)

    g_out = dwo.reshape(N_DEV, dwo.shape[0] // N_DEV, dwo.shape[1])
    (dq, dk, dv, drg, dgain), (r_out,) = _ret_bwd(*qkv, rg, o, rprev, dy_ret, ret_norm_gain, cos_t, sin_t, tables,
                                                 tm, ride=_pair_ride([g_out]))
    q_out = _pair_sum(g_out, r_out, c_arr, REDUCE_TILE, "grad_pair_sum_out")
    (dlx, dlg, g_small), (e_out,) = _lru_bwd(lx, lg, hl, dy_lru, lru_saved, cw8, w_rg[0], w_ig[0], lru_lambda,
                                            dgain, dgf, tm, ride=_chip_ride([q_out]))
    f_out = _chip_sum(q_out, e_out, j_arr, REDUCE_TILE, "grad_chip_sum_out")
    dparts = [dlx, dlg, dq, dk, dv, drg]
    dwg, (s_out, r_small) = _in_proj_dw(dparts, hp, norm_gain, wg.shape,
                                        ride=_join_rides(_sibling_ride([f_out]), _pair_ride([g_small])))
    g_in = dwg.reshape(N_DEV, dwg.shape[1] // 2, dwg.shape[2])
    q_in = _pair_exchange_sum(g_in, c_arr, REDUCE_TILE, "grad_pair_exchange_sum_in")
    q_small = _pair_sum(g_small, r_small, c_arr, UNIT_ROWS, "grad_pair_sum_small")
    (grad_x, dmeta, dg1), (e_in, e_small) = _in_proj_dx(dparts, hp, dh2, norm_gain, wg, s_len, tm,
                                                        ride=_chip_ride([q_in, q_small]))
    f_in = _chip_sum(q_in, e_in, j_arr, REDUCE_TILE, "grad_chip_sum_in")
    f_small = _chip_sum(q_small, e_small, j_arr, UNIT_ROWS, "grad_chip_sum_small", loss_part=loss_acc)
    tail = jnp.concatenate([_units(dmeta), _units(dg1), jnp.zeros((N_DEV, TAIL_ROWS - N_META - 1, LANES), F32)],
                           axis=1).reshape(N_DEV * TAIL_ROWS, LANES)
    s_in, o_small = _finish_exchange(f_in, jnp.concatenate([f_small, tail], axis=0))

    res_in = _adamw_big(w_in[0], f_in, s_in, m_w_in[0], v_w_in[0], c_arr, REDUCE_TILE, "adamw_w_in")
    res_out = _adamw_big(w_out[0], f_out, s_out, m_w_out[0], v_w_out[0], c_arr, REDUCE_TILE, "adamw_w_out")
    small_params = {
        "meta_tokens": (meta_tokens, m_meta_tokens, v_meta_tokens),
        "norm_gain": (norm_gain, m_norm_gain, v_norm_gain),
        "conv_w": (conv_w[0], m_conv_w[0], v_conv_w[0]),
        "conv_b": (conv_b, m_conv_b, v_conv_b),
        "w_rg": (w_rg[0], m_w_rg[0], v_w_rg[0]),
        "b_rg": (b_rg, m_b_rg, v_b_rg),
        "w_ig": (w_ig[0], m_w_ig[0], v_w_ig[0]),
        "b_ig": (b_ig, m_b_ig, v_b_ig),
        "lru_lambda": (lru_lambda, m_lru_lambda, v_lru_lambda),
        "ret_norm_gain": (ret_norm_gain, m_ret_norm_gain, v_ret_norm_gain),
        "final_norm_gain": (gain_f, m_final_norm_gain.reshape(1, d), v_final_norm_gain.reshape(1, d)),
    }
    res, loss = _adamw_small(j_arr, o_small, small_params)
    res["w_in"] = tuple(res_in)
    res["w_out"] = tuple(res_out)

    order = ["meta_tokens", "norm_gain", "w_in", "conv_w", "conv_b", "w_rg", "b_rg", "w_ig", "b_ig", "lru_lambda",
             "ret_norm_gain", "w_out", "final_norm_gain"]
    shapes = {"w_in": w_in.shape, "conv_w": conv_w.shape, "w_rg": w_rg.shape, "w_ig": w_ig.shape,
              "w_out": w_out.shape, "final_norm_gain": final_norm_gain.shape}
    outs = [loss, grad_x.reshape(x.shape)]
    for kind in range(4):
        for name in order:
            a = res[name][kind]
            outs.append(a.reshape(shapes[name]) if name in shapes else a)
    return tuple(outs)
```

```python
import functools

import jax
import jax.numpy as jnp
from jax import lax
from jax.experimental import pallas as pl
from jax.experimental.pallas import tpu as pltpu

F32 = jnp.float32
BF16 = jnp.bfloat16

N_META = 16
CHUNK = 128
PAD_ROWS = CHUNK - N_META
HEADS = 8
QK_DIM = 64
LANES = 128
SUBLANES = 8
LRU_C = 8.0
EPS = 1e-6
ROPE_BASE = 10000.0
ADAM_LR = 0.001
ADAM_B1 = 0.9
ADAM_B2 = 0.999
ADAM_EPS = 1e-08
ADAM_WD = 0.01
ADAM_STEP = 10
N_CHIPS = 4
N_DEV = 8
TOKEN_TILE = 384
REDUCE_TILE = 256
VMEM_LIMIT = 58 * 1024 * 1024
MESH = pl.DeviceIdType.MESH

VMEM_SPEC = pl.BlockSpec(memory_space=pltpu.VMEM)
SMEM_SPEC = pl.BlockSpec(memory_space=pltpu.SMEM)
ANY_SPEC = pl.BlockSpec(memory_space=pl.ANY)

ROW_WR, ROW_WI, ROW_META, ROW_CONV, ROW_VEC, UNIT_ROWS = 0, 128, 256, 272, 276, 288
VEC_NAMES = ["norm_gain", "conv_b", "b_rg", "b_ig", "lru_lambda", "ret_norm_gain", "final_norm_gain"]
N_VEC = len(VEC_NAMES)
ROW_LOSS = ROW_VEC + N_VEC
TAIL_ROWS = 24


def _dot(a, b):
    return jnp.dot(a, b, preferred_element_type=F32)


def _dot_nt(a, b):
    return lax.dot_general(a, b, (((1,), (1,)), ((), ())), preferred_element_type=F32)


def _dot_tn(a, b):
    return lax.dot_general(a, b, (((0,), (0,)), ((), ())), preferred_element_type=F32)


def _sigmoid(x):
    return 0.5 * jnp.tanh(0.5 * x) + 0.5


def _shift_down(x, prev8, s):
    rolled = pltpu.roll(x, s, 0)
    rows = lax.broadcasted_iota(jnp.int32, (SUBLANES, x.shape[1]), 0)
    top = jnp.where(rows < s, pltpu.roll(prev8, s, 0), rolled[0:SUBLANES])
    return jnp.concatenate([top, rolled[SUBLANES:]], axis=0)


def _shift_up(x, next8, s):
    n = x.shape[0]
    rolled = pltpu.roll(x, n - s, 0)
    rows = lax.broadcasted_iota(jnp.int32, (SUBLANES, x.shape[1]), 0)
    bot = jnp.where(rows >= SUBLANES - s, pltpu.roll(next8, SUBLANES - s, 0), rolled[n - SUBLANES:n])
    return jnp.concatenate([rolled[:n - SUBLANES], bot], axis=0)


def _rot_partner(t):
    w = t.shape[1]
    lane = lax.broadcasted_iota(jnp.int32, t.shape, 1)
    first = (lane % QK_DIM) < (QK_DIM // 2)
    return jnp.where(first, pltpu.roll(t, w - QK_DIM // 2, 1), pltpu.roll(t, QK_DIM // 2, 1))


def _tile_lanes(t, reps):
    return jnp.concatenate([t] * reps, axis=1)


class _Ride:
    def __init__(self, srcs, dst_shapes, n_copies, make, to_sibling=False, to_chips=False):
        self.srcs, self.dst_shapes, self.n_copies, self.make = list(srcs), list(dst_shapes), n_copies, make
        self.to_sibling, self.to_chips = to_sibling, to_chips

    def peers(self):
        x, y, c, chips = _position()
        return ([(x, y, 1 - c)] if self.to_sibling else []) + ([(cx, cy, c) for cx, cy in chips] if self.to_chips else [])


def _join_rides(a, b):
    def make(src, dst, send_sems, recv_sems, base):
        na, da = len(a.srcs), len(a.dst_shapes)
        return (a.make(src[:na], dst[:da], send_sems, recv_sems, base)
                + b.make(src[na:], dst[da:], send_sems, recv_sems, base + a.n_copies))

    return _Ride(a.srcs + b.srcs, a.dst_shapes + b.dst_shapes, a.n_copies + b.n_copies, make,
                 a.to_sibling or b.to_sibling, a.to_chips or b.to_chips)


def _position():
    x, y, c = lax.axis_index("x"), lax.axis_index("y"), lax.axis_index("c")
    return x, y, c, [(1 - x, y), (x, 1 - y), (1 - x, 1 - y)]


def _peer_barrier(peers):
    barrier = pltpu.get_barrier_semaphore()
    for peer in peers:
        pl.semaphore_signal(barrier, inc=1, device_id=peer, device_id_type=MESH)
    pl.semaphore_wait(barrier, len(peers))


def _remote(src, dst, send_sems, recv_sems, k, to):
    return pltpu.make_async_remote_copy(src_ref=src, dst_ref=dst, send_sem=send_sems.at[k], recv_sem=recv_sems.at[k],
                                        device_id=to, device_id_type=MESH)


def _pair_ride(bufs):
    def make(src, dst, send_sems, recv_sems, base):
        x, y, c, _ = _position()
        return [_remote(src[b].at[2 * jj + 1 - c], dst[b].at[jj], send_sems, recv_sems, base + b * N_CHIPS + jj,
                        (x, y, 1 - c)) for b in range(len(bufs)) for jj in range(N_CHIPS)]

    shapes = [jax.ShapeDtypeStruct((N_CHIPS,) + b.shape[1:], b.dtype) for b in bufs]
    return _Ride(bufs, shapes, N_CHIPS * len(bufs), make, to_sibling=True)


def _chip_ride(bufs):
    def make(src, dst, send_sems, recv_sems, base):
        x, y, c, chips = _position()
        return [_remote(src[b].at[2 * cx + cy], dst[b].at[2 * x + y], send_sems, recv_sems, base + b * 3 + k,
                        (cx, cy, c)) for b in range(len(bufs)) for k, (cx, cy) in enumerate(chips)]

    shapes = [jax.ShapeDtypeStruct(b.shape, b.dtype) for b in bufs]
    return _Ride(bufs, shapes, 3 * len(bufs), make, to_chips=True)


def _sibling_ride(bufs):
    def make(src, dst, send_sems, recv_sems, base):
        x, y, c, _ = _position()
        return [_remote(src[b], dst[b], send_sems, recv_sems, base + b, (x, y, 1 - c)) for b in range(len(bufs))]

    shapes = [jax.ShapeDtypeStruct(b.shape, b.dtype) for b in bufs]
    return _Ride(bufs, shapes, len(bufs), make, to_sibling=True)


def _hosted_call(body, ride, n_steps, *, name, in_specs, out_specs, out_shape, scratch_shapes, args, barrier_id=None):
    params = pltpu.CompilerParams(dimension_semantics=("arbitrary",), vmem_limit_bytes=VMEM_LIMIT,
                                  collective_id=barrier_id if ride is not None else None)
    if ride is None:
        res = pl.pallas_call(body, name=name, grid=(n_steps,), in_specs=list(in_specs), out_specs=list(out_specs),
                             out_shape=list(out_shape), scratch_shapes=list(scratch_shapes),
                             compiler_params=params)(*args)
        return list(res), []
    sizes = [len(in_specs), len(ride.srcs), len(out_specs), len(ride.dst_shapes), len(scratch_shapes), 2]

    def hosted(*refs):
        groups, pos = [], 0
        for n in sizes:
            groups.append(refs[pos:pos + n])
            pos += n
        ins, rin, outs, rout, scr, (send_sems, recv_sems) = groups
        i = pl.program_id(0)

        @pl.when(i == 0)
        def _():
            if barrier_id is not None:
                _peer_barrier(ride.peers())
            for cp in ride.make(rin, rout, send_sems, recv_sems, 0):
                cp.start()

        body(*ins, *outs, *scr)

        @pl.when(i == n_steps - 1)
        def _():
            for cp in ride.make(rin, rout, send_sems, recv_sems, 0):
                cp.wait()

    n_out = len(out_specs)
    res = pl.pallas_call(
        hosted,
        name=name,
        grid=(n_steps,),
        in_specs=list(in_specs) + [ANY_SPEC] * len(ride.srcs),
        out_specs=list(out_specs) + [ANY_SPEC] * len(ride.dst_shapes),
        out_shape=list(out_shape) + ride.dst_shapes,
        scratch_shapes=list(scratch_shapes) + [pltpu.SemaphoreType.DMA((ride.n_copies,)),
                                               pltpu.SemaphoreType.DMA((ride.n_copies,))],
        compiler_params=params,
    )(*args, *ride.srcs)
    return list(res[:n_out]), list(res[n_out:])


def _gather_weights(w_in, small):
    r_in, c_in = w_in.shape
    h_in = r_in // 2
    q_in = h_in // 2

    def body(win_ref, small_ref, wg_ref, sg_ref, mine, send_sems, recv_sems, local_sem):
        x, y, c, chips = _position()
        j = 2 * x + y
        sibling = (x, y, 1 - c)
        xn, yn, dg = chips
        jx, jy, jd = (2 * cx + cy for cx, cy in chips)

        def half(jj, cc):
            return wg_ref.at[jj, pl.ds(cc * h_in, h_in), :]

        def quarter(jj, qq):
            return wg_ref.at[jj, pl.ds(c * h_in + qq * q_in, q_in), :]

        def copy(k, ref, to):
            return _remote(ref, ref, send_sems, recv_sems, k, to)

        def send_mine(k, qq, to):
            rows = pl.ds(c * h_in + qq * q_in, q_in)
            return _remote(mine.at[rows, :], wg_ref.at[j, rows, :], send_sems, recv_sems, k, to)

        def cast_rows(start, rows):
            start = pl.multiple_of(start, q_in)
            mine[pl.ds(start, rows), :] = win_ref[pl.ds(start, rows), :].astype(BF16)

        first = [send_mine(0, 0, (*xn, c)), send_mine(2, 1, (*yn, c)), send_mine(1, 1, (*xn, c)),
                 send_mine(3, 0, (*yn, c))]
        sg_ref[j] = small_ref[...]
        cast_rows(c * h_in, q_in)
        _peer_barrier([sibling] + [(cx, cy, c) for cx, cy in chips])
        first[0].start()
        cast_rows(c * h_in + q_in, q_in)
        for cp in first[1:]:
            cp.start()
        small_copies = [copy(9 + k, sg_ref.at[j], (cx, cy, c)) for k, (cx, cy) in enumerate(chips)]
        for cp in small_copies:
            cp.start()
        first += small_copies
        cast_rows((1 - c) * h_in, h_in)
        keep = pltpu.make_async_copy(mine, wg_ref.at[j], local_sem)
        keep.start()
        copy(0, quarter(jx, 0), sibling).wait_recv()
        along_y = copy(4, quarter(jx, 0), (*yn, c))
        along_y.start()
        copy(2, quarter(jy, 1), sibling).wait_recv()
        along_x = copy(5, quarter(jy, 1), (*xn, c))
        along_x.start()
        copy(1, quarter(jx, 1), sibling).wait_recv()
        to_sib = [copy(6, half(jx, c), sibling)]
        to_sib[-1].start()
        copy(3, quarter(jy, 0), sibling).wait_recv()
        to_sib.append(copy(7, half(jy, c), sibling))
        to_sib[-1].start()
        copy(4, quarter(jd, 0), sibling).wait_recv()
        copy(5, quarter(jd, 1), sibling).wait_recv()
        to_sib.append(copy(8, half(jd, c), sibling))
        to_sib[-1].start()
        for k, jk in enumerate((jx, jy, jd)):
            copy(6 + k, half(jk, 1 - c), sibling).wait_recv()
            copy(9 + k, sg_ref.at[jk], sibling).wait_recv()
        for cp in first + [along_y, along_x] + to_sib:
            cp.wait_send()
        keep.wait()

    return pl.pallas_call(
        body,
        name="gather_weights",
        out_shape=(jax.ShapeDtypeStruct((N_CHIPS, r_in, c_in), BF16),
                   jax.ShapeDtypeStruct((N_CHIPS,) + small.shape, F32)),
        in_specs=[VMEM_SPEC, VMEM_SPEC],
        out_specs=(ANY_SPEC, VMEM_SPEC),
        scratch_shapes=[pltpu.VMEM((r_in, c_in), BF16), pltpu.SemaphoreType.DMA((12,)),
                        pltpu.SemaphoreType.DMA((12,)), pltpu.SemaphoreType.DMA],
        compiler_params=pltpu.CompilerParams(vmem_limit_bytes=VMEM_LIMIT, collective_id=6),
    )(w_in, small)


def _proj_segments(d_lru, d_qk, d_ret, chunk_w):
    widths = [d_lru, d_lru, d_qk, d_qk, d_ret, d_ret]
    segs, col = [], 0
    for w in widths:
        parts, off = [], 0
        while off < w:
            jj, inner = divmod(col + off, chunk_w)
            take = min(w - off, chunk_w - inner)
            parts.append((jj, inner, off, take))
            off += take
        segs.append(parts)
        col += w
    return segs


def _in_proj(x2, meta, gain, wg, cos_t, sin_t, w_out, qdec, kdec, tm, d_lru, d_qk, d_ret):
    s_len, d = x2.shape
    tp = s_len + CHUNK
    nt, nb = tp // tm, tm // CHUNK
    segs = _proj_segments(d_lru, d_qk, d_ret, wg.shape[2])
    outs = [(d, F32), (d_lru, F32), (d_lru, F32)] + [(d_qk, BF16)] * 4 + [(d_ret, BF16), (d_ret, F32)]
    r_out, c_out = w_out.shape
    h_out = r_out // 2
    fwd_step = min(6, nt - 1)

    def gather_w_out(i, wout_ref, wo_ref, wob, send_sems, recv_sems, local_sem):
        x, y, c, chips = _position()
        j = 2 * x + y
        sibling = (x, y, 1 - c)

        def half(jj, cc):
            return wo_ref.at[jj, pl.ds(cc * h_out, h_out), :]

        local = pltpu.make_async_copy(wob, wo_ref.at[j], local_sem)
        first = [_remote(wob.at[pl.ds(c * h_out, h_out), :], half(j, c), send_sems, recv_sems, k, (cx, cy, c))
                 for k, (cx, cy) in enumerate(chips)]
        passed = [_remote(half(2 * cx + cy, c), half(2 * cx + cy, c), send_sems, recv_sems, 3 + k, sibling)
                  for k, (cx, cy) in enumerate(chips)]

        @pl.when(i == 0)
        def _():
            wob[...] = wout_ref[...].astype(BF16)
            _peer_barrier([sibling] + [(cx, cy, c) for cx, cy in chips])
            local.start()
            for cp in first:
                cp.start()

        @pl.when(i == fwd_step)
        def _():
            for k, (cx, cy) in enumerate(chips):
                _remote(half(2 * cx + cy, c), half(2 * cx + cy, c), send_sems, recv_sems, k, sibling).wait_recv()
                passed[k].start()

        @pl.when(i == nt - 1)
        def _():
            for k, (cx, cy) in enumerate(chips):
                jk = 2 * cx + cy
                _remote(half(jk, 1 - c), half(jk, 1 - c), send_sems, recv_sems, 3 + k, sibling).wait_recv()
            for cp in first + passed:
                cp.wait_send()
            local.wait()

    def body(*refs):
        xb = refs[:nb]
        meta_ref, g_ref, w_ref, cos_ref, sin_ref, wout_ref, qdec_ref, kdec_ref = refs[nb:nb + 8]
        hp_ref, lx_ref, lg_ref, qb_ref, kb_ref, qd_ref, kd_ref, vb_ref, rg_ref = refs[nb + 8:nb + 17]
        wo_ref, q_s, k_s, wob, send_sems, recv_sems, local_sem = refs[nb + 17:]
        i = pl.program_id(0)
        gather_w_out(i, wout_ref, wo_ref, wob, send_sems, recv_sems, local_sem)
        blocks = [r[...] for r in xb]
        head = jnp.concatenate([jnp.zeros((PAD_ROWS, d), F32), meta_ref[...]], axis=0)
        blocks[0] = jnp.where(i == 0, head, blocks[0])
        h = jnp.concatenate(blocks, axis=0)
        hp_ref[...] = h
        rinv = lax.rsqrt(jnp.mean(h * h, axis=-1, keepdims=True) + EPS)
        u = ((h * rinv) * g_ref[...]).astype(BF16)
        for out_ref, parts in zip([lx_ref, lg_ref, q_s, k_s, vb_ref, rg_ref], segs):
            for jj, inner, off, take in parts:
                out_ref[:, off:off + take] = _dot(u, w_ref[jj, :, inner:inner + take]).astype(out_ref.dtype)
        cos = _tile_lanes(cos_ref[...], d_qk // LANES)
        sin = _tile_lanes(sin_ref[...], d_qk // LANES)
        q = q_s[...]
        q = q * cos + _rot_partner(q) * sin
        k = k_s[...]
        k = (k * cos + _rot_partner(k) * sin) * (QK_DIM ** -0.5)
        qb_ref[...] = q.astype(BF16)
        kb_ref[...] = k.astype(BF16)
        qd_ref[...] = (q * jnp.concatenate([qdec_ref[...]] * nb, axis=0)).astype(BF16)
        kd_ref[...] = (k * jnp.concatenate([kdec_ref[...]] * nb, axis=0)).astype(BF16)

    x_specs = [pl.BlockSpec((CHUNK, d), functools.partial(lambda i, b: (jnp.maximum(i * nb + b - 1, 0), 0), b=b))
               for b in range(nb)]
    tile = lambda w: pl.BlockSpec((tm, w), lambda i: (i, 0))
    return pl.pallas_call(
        body,
        name="in_proj",
        grid=(nt,),
        in_specs=x_specs + [pl.BlockSpec(meta.shape, lambda i: (0, 0)),
                            pl.BlockSpec(gain.shape, lambda i: (0, 0)),
                            pl.BlockSpec(wg.shape, lambda i: (0, 0, 0)),
                            tile(LANES), tile(LANES),
                            pl.BlockSpec(w_out.shape, lambda i: (0, 0)),
                            pl.BlockSpec(qdec.shape, lambda i: (0, 0)), pl.BlockSpec(kdec.shape, lambda i: (0, 0))],
        out_specs=[tile(w) for w, _ in outs] + [ANY_SPEC],
        out_shape=[jax.ShapeDtypeStruct((tp, w), dt) for w, dt in outs]
                  + [jax.ShapeDtypeStruct((N_CHIPS, r_out, c_out), BF16)],
        scratch_shapes=[pltpu.VMEM((tm, d_qk), F32), pltpu.VMEM((tm, d_qk), F32),
                        pltpu.VMEM((r_out, c_out), BF16), pltpu.SemaphoreType.DMA((6,)),
                        pltpu.SemaphoreType.DMA((6,)), pltpu.SemaphoreType.DMA],
        compiler_params=pltpu.CompilerParams(dimension_semantics=("arbitrary",), vmem_limit_bytes=VMEM_LIMIT,
                                             collective_id=7),
    )(*([x2] * nb), meta, gain, wg, cos_t, sin_t, w_out, qdec, kdec)


def _segment_scan(a3, u3, out3, p3, carry, tm, reverse):
    groups = a3.shape[0]
    seg = tm // SUBLANES

    def step(j, state):
        hs, ps = state
        rows = pl.ds((seg - 1 - j) if reverse else j, SUBLANES, stride=seg)
        new_h, new_p = [], []
        for g in range(groups):
            a = a3[g, rows, :]
            h = a * hs[g] + u3[g, rows, :]
            p = ps[g] * a
            out3[g, rows, :] = h
            p3[g, rows, :] = p
            new_h.append(h)
            new_p.append(p)
        return tuple(new_h), tuple(new_p)

    zeros = tuple(jnp.zeros((SUBLANES, LANES), F32) for _ in range(groups))
    ones = tuple(jnp.ones((SUBLANES, LANES), F32) for _ in range(groups))
    lax.fori_loop(0, seg, step, (zeros, ones))
    carries = [carry[:, g * LANES:(g + 1) * LANES] for g in range(groups)]
    for s in (reversed(range(SUBLANES)) if reverse else range(SUBLANES)):
        rows = slice(s * seg, (s + 1) * seg)
        edge = s * seg if reverse else (s + 1) * seg - 1
        for g in range(groups):
            out3[g, rows, :] = out3[g, rows, :] + p3[g, rows, :] * carries[g]
            carries[g] = out3[g, edge:edge + 1, :]
    return jnp.concatenate(carries, axis=1)


def _softplus_neg(lam):
    z = -lam
    e = jnp.exp(-jnp.abs(z))
    e1 = 1.0 + e
    log1p_e = jnp.where(e1 == 1.0, e, jnp.log(e1) * (e / (e1 - 1.0)))
    return jnp.maximum(z, 0.0) + log1p_e


def _lru_fwd(lx, lg, cw, cb, wr, br, wi, bi, lam, tm):
    tp, w = lx.shape
    nt = tp // tm
    per8 = tm // SUBLANES
    n_heads = wr.shape[0]

    def body(lx_ref, lxp_ref, lg_ref, cw_ref, cb_ref, wr_ref, br_ref, wi_ref, bi_ref, lam_ref,
             hl_ref, y_ref, xc_ref, r_ref, ig_ref, a_ref, beta_ref, w4_ref, a_s, u_s, h_s, p_s, carry,
             lx_ring, lg_ring, ring_sems):
        i = pl.program_id(0)

        def fetch(step):
            slot = step % 3
            start = step * tm if isinstance(step, int) else pl.multiple_of(step * tm, tm)
            rows = pl.ds(start, tm)
            return (pltpu.make_async_copy(lx_ref.at[rows, :], lx_ring.at[slot], ring_sems.at[0, slot]),
                    pltpu.make_async_copy(lg_ref.at[rows, :], lg_ring.at[slot], ring_sems.at[1, slot]))

        @pl.when(i == 0)
        def _():
            carry[...] = jnp.zeros_like(carry)
            for step in range(min(2, nt)):
                for cp in fetch(step):
                    cp.start()

        @pl.when(i + 2 < nt)
        def _():
            for cp in fetch(i + 2):
                cp.start()

        for cp in fetch(i):
            cp.wait()
        slot = i % 3

        sp = _softplus_neg(lam_ref[...])
        row = lax.broadcasted_iota(jnp.int32, (tm, 1), 0) + i * tm
        for hd in range(n_heads):
            hs = slice(hd * LANES, (hd + 1) * LANES)
            lxv = lx_ring[slot, :, hs]
            prev8 = jnp.where(i == 0, 0.0, lxp_ref[:, hs])
            xc = cb_ref[:, hs] + _shift_down(lxv, prev8, 3) * cw_ref[0:1, hs]
            xc = xc + _shift_down(lxv, prev8, 2) * cw_ref[1:2, hs]
            xc = xc + _shift_down(lxv, prev8, 1) * cw_ref[2:3, hs]
            xc = xc + lxv * cw_ref[3:4, hs]
            xc_ref[:, hs] = xc
            xh = xc.astype(BF16)
            r = _sigmoid(_dot(xh, wr_ref[hd].astype(BF16)) + br_ref[:, hs])
            ig = _sigmoid(_dot(xh, wi_ref[hd].astype(BF16)) + bi_ref[:, hs])
            r_ref[:, hs] = r
            ig_ref[:, hs] = ig
            log_a = (-LRU_C * r) * sp[:, hs]
            a = jnp.exp(log_a)
            a_ref[:, hs] = a
            a2 = a * a
            beta2 = jnp.maximum((1.0 + a2) * jnp.tanh(-log_a), 1e-37)
            rsb = lax.rsqrt(beta2)
            beta = beta2 * rsb
            beta_ref[:, hs] = beta
            w4_ref[:, hs] = a2 * rsb
            a_s[hd] = a
            u_s[hd] = jnp.where(row >= PAD_ROWS, beta * ig * xc, 0.0)
        carry[0:1, :] = _segment_scan(a_s, u_s, h_s, p_s, carry[0:1, :], tm, reverse=False)
        for hd in range(n_heads):
            hs = slice(hd * LANES, (hd + 1) * LANES)
            hl = h_s[hd]
            hl_ref[:, hs] = hl
            g = lg_ring[slot, :, hs]
            y_ref[:, hs] = (hl * (g * _sigmoid(g))).astype(BF16)

    tile = pl.BlockSpec((tm, w), lambda i: (i, 0))
    prev = pl.BlockSpec((SUBLANES, w), lambda i: (jnp.maximum(i * per8 - 1, 0), 0))
    vec = pl.BlockSpec((1, w), lambda i: (0, 0))
    mat = pl.BlockSpec(wr.shape, lambda i: (0, 0, 0))
    f32_out = jax.ShapeDtypeStruct((tp, w), F32)
    return pl.pallas_call(
        body,
        name="lru_fwd",
        grid=(nt,),
        in_specs=[ANY_SPEC, prev, ANY_SPEC, pl.BlockSpec(cw.shape, lambda i: (0, 0)), vec, mat, vec, mat, vec, vec],
        out_specs=[tile] * 8,
        out_shape=[f32_out, jax.ShapeDtypeStruct((tp, w), BF16)] + [f32_out] * 6,
        scratch_shapes=[pltpu.VMEM((w // LANES, tm, LANES), F32)] * 4 + [pltpu.VMEM((SUBLANES, w), F32)]
                       + [pltpu.VMEM((3, tm, w), F32), pltpu.VMEM((3, tm, w), F32), pltpu.SemaphoreType.DMA((2, 3))],
        compiler_params=pltpu.CompilerParams(dimension_semantics=("arbitrary",), vmem_limit_bytes=VMEM_LIMIT),
    )(lx, lx, lg, cw, cb, wr, br, wi, bi, lam)


def _ret_tables():
    log_g = jnp.log1p(-jnp.exp2(-5.0 - jnp.arange(HEADS, dtype=F32)))
    idx = jnp.arange(CHUNK, dtype=F32)
    diff = idx[:, None] - idx[None, :]
    dmask = jnp.where(diff[None] >= 0.0, jnp.exp(jnp.maximum(diff, 0.0)[None] * log_g[:, None, None]), 0.0)
    kdec = jnp.repeat(jnp.exp((CHUNK - 1.0 - idx)[:, None] * log_g[None, :]), QK_DIM, axis=1)
    qdec = jnp.repeat(jnp.exp((idx + 1.0)[:, None] * log_g[None, :]), QK_DIM, axis=1)
    g_chunk = jnp.exp(CHUNK * log_g)
    g_rows = jnp.repeat(g_chunk, QK_DIM).reshape(HEADS // 2, 2 * QK_DIM, 1)
    g_state = jnp.broadcast_to(g_rows, (HEADS // 2, 2 * QK_DIM, 2 * LANES))
    r_head = jnp.arange(2 * QK_DIM)[:, None] // QK_DIM
    c_head = jnp.arange(2 * LANES)[None, :] // LANES
    block_diag = (r_head == c_head).astype(F32)
    return dmask, qdec, kdec, g_state, block_diag


def _head_norm(o_h):
    mu = jnp.mean(o_h, axis=-1, keepdims=True)
    oc = o_h - mu
    var = jnp.mean(oc * oc, axis=-1, keepdims=True)
    rstd = lax.rsqrt(var + EPS)
    return oc * rstd, rstd


def _ret_fwd(qb, kb, qd, kd, vb, rg, gain, tables, tm):
    tp, d_qk = qb.shape
    d_ret = vb.shape[1]
    n_ch = tp // CHUNK
    cps = tm // CHUNK
    n_pairs = HEADS // 2
    dmask, _, _, g_state, block_diag = tables

    def body(q_ref, k_ref, qd_ref, kd_ref, v_ref, rg_ref, gain_ref, dm_ref, gs_ref, bd_ref,
             o_ref, y_ref, rp_ref, state):
        n = pl.program_id(0)

        @pl.when(n == 0)
        def _():
            state[...] = jnp.zeros_like(state)

        lane = lax.broadcasted_iota(jnp.int32, (CHUNK, LANES), 1)
        for ci in range(cps):
            rs = slice(ci * CHUNK, (ci + 1) * CHUNK)
            for p in range(n_pairs):
                qs = slice(p * LANES, (p + 1) * LANES)
                vs = slice(p * 2 * LANES, (p + 1) * 2 * LANES)
                qp, kb = q_ref[rs, qs], k_ref[rs, qs]
                vb = v_ref[rs, vs]
                qd, kd = qd_ref[rs, qs], kd_ref[rs, qs]
                st = state[p]
                st_b = st.astype(BF16)
                rp_ref[ci, p] = st_b
                cross = _dot(qd, st_b)
                for e in range(2):
                    hd = 2 * p + e
                    hs = slice(hd * LANES, (hd + 1) * LANES)
                    es = slice(e * LANES, (e + 1) * LANES)
                    qm = jnp.where((lane // QK_DIM) == e, qp, jnp.zeros_like(qp))
                    s = _dot_nt(qm, kb) * dm_ref[hd]
                    o_h = _dot(s.astype(BF16), vb[:, es]) + cross[:, es]
                    o_ref[rs, hs] = o_h
                    xhat, _ = _head_norm(o_h)
                    g = rg_ref[rs, hs]
                    y_ref[rs, hs] = ((xhat * gain_ref[:, hs]) * (g * _sigmoid(g))).astype(BF16)
                state[p] = gs_ref[p] * st + bd_ref[...] * _dot_tn(kd, vb)

    ch = lambda w: pl.BlockSpec((tm, w), lambda n: (n, 0))
    const2 = lambda a: pl.BlockSpec(a.shape, lambda n: (0, 0))
    const3 = lambda a: pl.BlockSpec(a.shape, lambda n: (0, 0, 0))
    return pl.pallas_call(
        body,
        name="ret_fwd",
        grid=(n_ch // cps,),
        in_specs=[ch(d_qk)] * 4 + [ch(d_ret), ch(d_ret), const2(gain), const3(dmask), const3(g_state),
                                   const2(block_diag)],
        out_specs=[ch(d_ret), ch(d_ret),
                   pl.BlockSpec((cps, n_pairs, 2 * QK_DIM, 2 * LANES), lambda n: (n, 0, 0, 0))],
        out_shape=[jax.ShapeDtypeStruct((tp, d_ret), F32), jax.ShapeDtypeStruct((tp, d_ret), BF16),
                   jax.ShapeDtypeStruct((n_ch, n_pairs, 2 * QK_DIM, 2 * LANES), BF16)],
        scratch_shapes=[pltpu.VMEM((n_pairs, 2 * QK_DIM, 2 * LANES), F32)],
        compiler_params=pltpu.CompilerParams(dimension_semantics=("arbitrary",), vmem_limit_bytes=VMEM_LIMIT),
    )(qb, kb, qd, kd, vb, rg, gain, dmask, g_state, block_diag)


def _out_proj_loss(y_lru, y_ret, hp, tgt, wo, gain_f, tm):
    tp, d = hp.shape
    w_lru = y_lru.shape[1]
    w_mix = wo.shape[0]
    nt, nb = tp // tm, tm // CHUNK

    def body(*refs):
        yl_ref, yr_ref, hp_ref = refs[:3]
        tb = refs[3:3 + nb]
        wo_ref, gf_ref = refs[3 + nb:5 + nb]
        dh2_ref, dyl_ref, dyr_ref, dwo_ref, dgf_ref, loss_ref = refs[5 + nb:]
        i = pl.program_id(0)

        @pl.when(i == 0)
        def _():
            dwo_ref[...] = jnp.zeros_like(dwo_ref)
            dgf_ref[...] = jnp.zeros_like(dgf_ref)
            loss_ref[...] = jnp.zeros_like(loss_ref)

        yl, yr = yl_ref[...], yr_ref[...]
        h2 = hp_ref[...] + _dot(yl, wo_ref[0:w_lru, :]) + _dot(yr, wo_ref[w_lru:w_mix, :])
        rinv = lax.rsqrt(jnp.mean(h2 * h2, axis=-1, keepdims=True) + EPS)
        nrm = h2 * rinv
        gf = gf_ref[...]
        tgt_v = jnp.concatenate([r[...] for r in tb], axis=0)
        row = lax.broadcasted_iota(jnp.int32, (tm, 1), 0) + i * tm
        err = jnp.where(row >= CHUNK, nrm * gf - tgt_v, 0.0)
        loss_ref[...] += 0.5 * jnp.sum(jnp.mean(err * err, axis=-1, keepdims=True))
        dout = err * (1.0 / d)
        dgf_ref[...] += jnp.sum(dout * nrm, axis=0, keepdims=True)
        dn = dout * gf
        dh2 = rinv * (dn - nrm * jnp.mean(dn * nrm, axis=-1, keepdims=True))
        dh2_ref[...] = dh2
        dh2b = dh2.astype(BF16)
        dyl_ref[...] = _dot_nt(dh2b, wo_ref[0:w_lru, :])
        dyr_ref[...] = _dot_nt(dh2b, wo_ref[w_lru:w_mix, :])
        dwo_ref[0:w_lru, :] += _dot_tn(yl, dh2b)
        dwo_ref[w_lru:w_mix, :] += _dot_tn(yr, dh2b)

    tile = lambda w: pl.BlockSpec((tm, w), lambda i: (i, 0))
    t_specs = [pl.BlockSpec((CHUNK, d), functools.partial(lambda i, b: (jnp.maximum(i * nb + b - 1, 0), 0), b=b))
               for b in range(nb)]
    return pl.pallas_call(
        body,
        name="out_proj_loss",
        grid=(nt,),
        in_specs=[tile(w_lru), tile(w_mix - w_lru), tile(d)] + t_specs +
                 [pl.BlockSpec(wo.shape, lambda i: (0, 0)), pl.BlockSpec(gain_f.shape, lambda i: (0, 0))],
        out_specs=[tile(d), tile(w_lru), tile(w_mix - w_lru), pl.BlockSpec(wo.shape, lambda i: (0, 0)),
                   pl.BlockSpec((1, d), lambda i: (0, 0)), pl.BlockSpec((SUBLANES, LANES), lambda i: (0, 0))],
        out_shape=[jax.ShapeDtypeStruct((tp, d), F32), jax.ShapeDtypeStruct((tp, w_lru), F32),
                   jax.ShapeDtypeStruct((tp, w_mix - w_lru), F32), jax.ShapeDtypeStruct(wo.shape, F32),
                   jax.ShapeDtypeStruct((1, d), F32), jax.ShapeDtypeStruct((SUBLANES, LANES), F32)],
        compiler_params=pltpu.CompilerParams(dimension_semantics=("arbitrary",), vmem_limit_bytes=VMEM_LIMIT),
    )(y_lru, y_ret, hp, *([tgt] * nb), wo, gain_f)


def _ret_bwd(qb, kb, qd, kd, vb, rg, o, rprev, dy, gain, cos_t, sin_t, tables, tm, ride=None):
    tp, d_qk = qb.shape
    d_ret = vb.shape[1]
    n_ch = tp // CHUNK
    cps = tm // CHUNK
    n_pairs = HEADS // 2
    dmask, qdec, kdec, g_state, block_diag = tables

    dmask_t = jnp.swapaxes(dmask, 1, 2)

    def body(q_ref, k_ref, qdb_ref, kdb_ref, v_ref, rg_ref, o_ref, rp_ref, dy_ref, gain_ref, cos_ref, sin_ref,
             dm_ref, dmt_ref, qd_ref, kd_ref, gs_ref, bd_ref, dq_ref, dk_ref, dv_ref, drg_ref, dgain_ref, dstate):
        n = pl.program_id(0)

        @pl.when(n == 0)
        def _():
            dstate[...] = jnp.zeros_like(dstate)
            dgain_ref[...] = jnp.zeros_like(dgain_ref)

        lane = lax.broadcasted_iota(jnp.int32, (CHUNK, LANES), 1)
        for ci in reversed(range(cps)):
            rs = slice(ci * CHUNK, (ci + 1) * CHUNK)
            dq_parts, dk_parts = [], []
            for p in range(n_pairs):
                qs = slice(p * LANES, (p + 1) * LANES)
                vs = slice(p * 2 * LANES, (p + 1) * 2 * LANES)
                do_parts = []
                for e in range(2):
                    hd = 2 * p + e
                    hs = slice(hd * LANES, (hd + 1) * LANES)
                    xhat, rstd = _head_norm(o_ref[rs, hs])
                    g = rg_ref[rs, hs]
                    sg = _sigmoid(g)
                    dyh = dy_ref[rs, hs]
                    gn = gain_ref[:, hs]
                    d_on = dyh * (g * sg)
                    drg_ref[rs, hs] = (dyh * (xhat * gn) * (sg * (1.0 + g * (1.0 - sg)))).astype(BF16)
                    dgain_ref[:, hs] += jnp.sum(d_on * xhat, axis=0, keepdims=True)
                    dxh = d_on * gn
                    do_parts.append(rstd * (dxh - jnp.mean(dxh, axis=-1, keepdims=True)
                                            - xhat * jnp.mean(dxh * xhat, axis=-1, keepdims=True)))
                do_b = jnp.concatenate(do_parts, axis=1).astype(BF16)
                qp, kb = q_ref[rs, qs], k_ref[rs, qs]
                vb = v_ref[rs, vs]
                qd, kd = qdb_ref[rs, qs], kdb_ref[rs, qs]
                dst = dstate[p]
                dst_b = dst.astype(BF16)
                dqp = _dot_nt(do_b, rp_ref[ci, p]) * qd_ref[:, qs]
                dkp = _dot_nt(vb, dst_b) * kd_ref[:, qs]
                dvp = _dot(kd, dst_b)
                dv_parts = []
                for e in range(2):
                    hd = 2 * p + e
                    es = slice(e * LANES, (e + 1) * LANES)
                    mine = (lane // QK_DIM) == e
                    qm = jnp.where(mine, qp, jnp.zeros_like(qp))
                    km = jnp.where(mine, kb, jnp.zeros_like(kb))
                    ds = (_dot_nt(do_b[:, es], vb[:, es]) * dm_ref[hd]).astype(BF16)
                    s_t = (_dot_nt(kb, qm) * dmt_ref[hd]).astype(BF16)
                    ds_t = (_dot_nt(vb[:, es], do_b[:, es]) * dmt_ref[hd]).astype(BF16)
                    dv_parts.append(dvp[:, es] + _dot(s_t, do_b[:, es]))
                    dqp = dqp + _dot(ds, km)
                    dkp = dkp + _dot(ds_t, qm)
                dv_ref[rs, vs] = jnp.concatenate(dv_parts, axis=1).astype(BF16)
                dstate[p] = gs_ref[p] * dst + bd_ref[...] * _dot_tn(qd, do_b)
                dq_parts.append(dqp)
                dk_parts.append(dkp)
            cos = _tile_lanes(cos_ref[rs, :], d_qk // LANES)
            sin = _tile_lanes(sin_ref[rs, :], d_qk // LANES)
            dq = jnp.concatenate(dq_parts, axis=1)
            dk = jnp.concatenate(dk_parts, axis=1) * (QK_DIM ** -0.5)
            dq_ref[rs, :] = (dq * cos + _rot_partner(dq * sin)).astype(BF16)
            dk_ref[rs, :] = (dk * cos + _rot_partner(dk * sin)).astype(BF16)

    last = n_ch // cps - 1
    ch = lambda w: pl.BlockSpec((tm, w), lambda n: (last - n, 0))
    const2 = lambda a: pl.BlockSpec(a.shape, lambda n: (0, 0))
    const3 = lambda a: pl.BlockSpec(a.shape, lambda n: (0, 0, 0))
    return _hosted_call(
        body, ride, n_ch // cps,
        name="ret_bwd", barrier_id=1,
        in_specs=[ch(d_qk)] * 4 + [ch(d_ret), ch(d_ret), ch(d_ret),
                  pl.BlockSpec((cps, n_pairs, 2 * QK_DIM, 2 * LANES), lambda n: (last - n, 0, 0, 0)),
                  ch(d_ret), const2(gain), ch(LANES), ch(LANES),
                  const3(dmask), const3(dmask_t), const2(qdec), const2(kdec), const3(g_state), const2(block_diag)],
        out_specs=[ch(d_qk), ch(d_qk), ch(d_ret), ch(d_ret), pl.BlockSpec((1, d_ret), lambda n: (0, 0))],
        out_shape=[jax.ShapeDtypeStruct((tp, d_qk), BF16), jax.ShapeDtypeStruct((tp, d_qk), BF16),
                   jax.ShapeDtypeStruct((tp, d_ret), BF16), jax.ShapeDtypeStruct((tp, d_ret), BF16),
                   jax.ShapeDtypeStruct((1, d_ret), F32)],
        scratch_shapes=[pltpu.VMEM((n_pairs, 2 * QK_DIM, 2 * LANES), F32)],
        args=(qb, kb, qd, kd, vb, rg, o, rprev, dy, gain, cos_t, sin_t, dmask, dmask_t, qdec, kdec, g_state,
              block_diag),
    )


def _lru_bwd(lx, lg, hl, dy, saved, cw, wr, wi, lam, dgain, dgf, tm, ride=None):
    tp, w = lx.shape
    nt = tp // tm
    per8 = tm // SUBLANES
    n_heads = wr.shape[0]
    vec_row = {name: ROW_VEC + VEC_NAMES.index(name) for name in VEC_NAMES}

    def body(lx_ref, lg_ref, hl_ref, hlp_ref, dy_ref, xc_ref, r_ref, ig_ref, a_ref, beta_ref, w4_ref,
             cw_ref, wr_ref, wi_ref, lam_ref, dgain_ref, dgf_ref,
             dlx_ref, dlg_ref, pk_ref,
             g_s, b_s, carry, dxc_next, a_next):
        i = pl.program_id(0)
        first_tile = i == nt - 1
        heads = [slice(hd * LANES, (hd + 1) * LANES) for hd in range(n_heads)]

        def add_row(hd, row, value):
            pk_ref[hd, row:row + 1, :] += value

        @pl.when(i == 0)
        def _():
            carry[...] = jnp.zeros_like(carry)
            dxc_next[...] = jnp.zeros_like(dxc_next)
            a_next[...] = jnp.zeros_like(a_next)
            pk_ref[...] = jnp.zeros_like(pk_ref)
            for hd, hs in enumerate(heads):
                add_row(hd, vec_row["ret_norm_gain"], dgain_ref[:, hs])
                add_row(hd, vec_row["final_norm_gain"], dgf_ref[:, hs])

        for hd, hs in enumerate(heads):
            g = lg_ref[:, hs]
            sg = _sigmoid(g)
            dyv = dy_ref[:, hs]
            dlg_ref[:, hs] = (dyv * hl_ref[:, hs] * (sg * (1.0 + g * (1.0 - sg)))).astype(BF16)
            g_s[hd] = dyv * (g * sg)
            b_s[hd] = _shift_up(a_ref[:, hs], a_next[:, hs], 1)
        carry[0:1, :] = _segment_scan(b_s, g_s, g_s, b_s, carry[0:1, :], tm, reverse=True)
        a_next[...] = a_ref[0:SUBLANES, :]
        row = lax.broadcasted_iota(jnp.int32, (tm, 1), 0) + (nt - 1 - i) * tm
        lam_v = lam_ref[...]
        dlam_scale = LRU_C * _sigmoid(-lam_v)
        dr_scale = -LRU_C * _softplus_neg(lam_v)
        for hd, hs in enumerate(heads):
            a, beta, r, ig, xc = a_ref[:, hs], beta_ref[:, hs], r_ref[:, hs], ig_ref[:, hs], xc_ref[:, hs]
            dh = g_s[hd]
            hprev = _shift_down(hl_ref[:, hs], jnp.where(first_tile, 0.0, hlp_ref[:, hs]), 1)
            du = jnp.where(row >= PAD_ROWS, dh, 0.0)
            dbeta = du * ig * xc
            d_ig = du * beta * xc
            dxc = du * beta * ig
            dloga = (dh * hprev) * a - dbeta * w4_ref[:, hs]
            add_row(hd, vec_row["lru_lambda"], jnp.sum(dloga * r, axis=0, keepdims=True) * dlam_scale[:, hs])
            dpr = (dloga * dr_scale[:, hs]) * r * (1.0 - r)
            dpi = d_ig * ig * (1.0 - ig)
            add_row(hd, vec_row["b_rg"], jnp.sum(dpr, axis=0, keepdims=True))
            add_row(hd, vec_row["b_ig"], jnp.sum(dpi, axis=0, keepdims=True))
            xh, dprh, dpih = xc.astype(BF16), dpr.astype(BF16), dpi.astype(BF16)
            pk_ref[hd, ROW_WR:ROW_WR + LANES, :] += _dot_tn(xh, dprh)
            pk_ref[hd, ROW_WI:ROW_WI + LANES, :] += _dot_tn(xh, dpih)
            dxc = dxc + _dot_nt(dprh, wr_ref[hd].astype(BF16)) + _dot_nt(dpih, wi_ref[hd].astype(BF16))
            nxt = dxc_next[:, hs]
            up1, up2, up3 = _shift_up(dxc, nxt, 1), _shift_up(dxc, nxt, 2), _shift_up(dxc, nxt, 3)
            dlx = dxc * cw_ref[3:4, hs]
            dlx = dlx + up1 * cw_ref[2:3, hs]
            dlx = dlx + up2 * cw_ref[1:2, hs]
            dlx = dlx + up3 * cw_ref[0:1, hs]
            dlx_ref[:, hs] = dlx.astype(BF16)
            dxc_next[:, hs] = dxc[0:SUBLANES]
            lxv = lx_ref[:, hs]
            add_row(hd, vec_row["conv_b"], jnp.sum(dxc, axis=0, keepdims=True))
            for kk, shifted in enumerate((up3, up2, up1, dxc)):
                add_row(hd, ROW_CONV + kk, jnp.sum(shifted * lxv, axis=0, keepdims=True))

    last = nt - 1
    tile = pl.BlockSpec((tm, w), lambda i: (last - i, 0))
    prev = pl.BlockSpec((SUBLANES, w), lambda i: (jnp.maximum((last - i) * per8 - 1, 0), 0))
    vec = pl.BlockSpec((1, w), lambda i: (0, 0))
    mat = pl.BlockSpec(wr.shape, lambda i: (0, 0, 0))
    cwb = pl.BlockSpec(cw.shape, lambda i: (0, 0))
    packed = (n_heads, UNIT_ROWS, LANES)
    return _hosted_call(
        body, ride, nt,
        name="lru_bwd", barrier_id=2,
        in_specs=[tile, tile, tile, prev, tile] + [tile] * 6 + [cwb, mat, mat, vec, vec, vec],
        out_specs=[tile, tile, pl.BlockSpec(packed, lambda i: (0, 0, 0))],
        out_shape=[jax.ShapeDtypeStruct((tp, w), BF16), jax.ShapeDtypeStruct((tp, w), BF16),
                   jax.ShapeDtypeStruct(packed, F32)],
        scratch_shapes=[pltpu.VMEM((w // LANES, tm, LANES), F32)] * 2 + [pltpu.VMEM((SUBLANES, w), F32)] * 3,
        args=(lx, lg, hl, hl, dy, *saved, cw, wr, wi, lam, dgain, dgf),
    )


def _in_proj_dw(dparts, hp, gain, wg_shape, ride=None):
    tp, d = hp.shape
    n_ch = tp // CHUNK
    per = next(p for p in (4, 2, 5, 3, 1) if (n_ch - 1) % p == 0)
    n_steps = 1 + (n_ch - 1) // per
    widths = [p.shape[1] for p in dparts]
    segs = _proj_segments(widths[0], widths[2], widths[4], wg_shape[2])

    def body(*refs):
        dp = [refs[p * per:(p + 1) * per] for p in range(6)]
        hp_b = refs[6 * per:7 * per]
        g_ref, dwg_ref, acc, sem = refs[7 * per:]
        i = pl.program_id(0)

        def accumulate(blocks):
            h = jnp.concatenate([hp_b[b][...] for b in blocks], axis=0)
            rinv = lax.rsqrt(jnp.mean(h * h, axis=-1, keepdims=True) + EPS)
            u = ((h * rinv) * g_ref[...]).astype(BF16)
            for p_refs, parts in zip(dp, segs):
                for jj, inner, off, take in parts:
                    seg = jnp.concatenate([p_refs[b][:, off:off + take] for b in blocks], axis=0)
                    acc[jj, :, inner:inner + take] += _dot_tn(u, seg)

        @pl.when(i == 0)
        def _():
            acc[...] = jnp.zeros_like(acc)
            accumulate([0])

        @pl.when(i > 0)
        def _():
            accumulate(list(range(per)))

        @pl.when(i == n_steps - 1)
        def _():
            cp = pltpu.make_async_copy(acc, dwg_ref, sem)
            cp.start()
            cp.wait()

    def blocks(w):
        return [pl.BlockSpec((CHUNK, w), functools.partial(
            lambda i, b: (jnp.where(i == 0, b, per * (i - 1) + 1 + b), 0), b=b)) for b in range(per)]

    in_specs, args = [], []
    for a, w in list(zip(dparts, widths)) + [(hp, d)]:
        in_specs += blocks(w)
        args += [a] * per
    outs, rides = _hosted_call(
        body, ride, n_steps,
        name="in_proj_dw", barrier_id=3,
        in_specs=in_specs + [pl.BlockSpec(gain.shape, lambda i: (0, 0))],
        out_specs=[ANY_SPEC],
        out_shape=[jax.ShapeDtypeStruct(wg_shape, F32)],
        scratch_shapes=[pltpu.VMEM(wg_shape, F32), pltpu.SemaphoreType.DMA],
        args=(*args, gain),
    )
    return outs[0], rides


def _in_proj_dx(dparts, hp, dh2, gain, wg, s_len, tm, ride=None):
    tp, d = hp.shape
    nt = tp // tm
    widths = [p.shape[1] for p in dparts]
    segs = _proj_segments(widths[0], widths[2], widths[4], wg.shape[2])

    def body(*refs):
        dp = refs[:6]
        hp_ref, dh2_ref, g_ref, w_ref = refs[6:10]
        gx_ref, dmeta_ref, dg_ref = refs[10:13]
        stage, sems = refs[13:]
        i = pl.program_id(0)

        @pl.when(i == 0)
        def _():
            dg_ref[...] = jnp.zeros_like(dg_ref)

        h = hp_ref[...]
        rinv = lax.rsqrt(jnp.mean(h * h, axis=-1, keepdims=True) + EPS)
        nrm = h * rinv
        gv = g_ref[...]
        du = jnp.zeros((tm, d), F32)
        for p_ref, parts in zip(dp, segs):
            for jj, inner, off, take in parts:
                du = du + _dot_nt(p_ref[:, off:off + take], w_ref[jj, :, inner:inner + take])
        dg_ref[...] += jnp.sum(du * nrm, axis=0, keepdims=True)
        dn = du * gv
        dh = dh2_ref[...] + rinv * (dn - nrm * jnp.mean(dn * nrm, axis=-1, keepdims=True))

        def first_copy():
            return pltpu.make_async_copy(stage.at[0, pl.ds(CHUNK, tm - CHUNK), :],
                                         gx_ref.at[pl.ds(0, tm - CHUNK), :], sems.at[0])

        def tile_copy(slot, start):
            return pltpu.make_async_copy(stage.at[slot], gx_ref.at[pl.ds(start, tm), :], sems.at[slot])

        @pl.when(i == 0)
        def _():
            dmeta_ref[...] = dh[PAD_ROWS:CHUNK]
            stage[0] = dh
            first_copy().start()

        @pl.when(i > 0)
        def _():
            slot = 1 + i % 2

            @pl.when(i >= 3)
            def _():
                tile_copy(slot, 0).wait()

            stage[slot] = dh
            tile_copy(slot, pl.multiple_of(i * tm - CHUNK, CHUNK)).start()

        @pl.when(i == nt - 1)
        def _():
            first_copy().wait()
            for step in (nt - 2, nt - 1):
                if step >= 1:
                    tile_copy(1 + step % 2, 0).wait()

    tile = lambda w: pl.BlockSpec((tm, w), lambda i: (i, 0))
    return _hosted_call(
        body, ride, nt,
        name="in_proj_dx", barrier_id=4,
        in_specs=[tile(w) for w in widths] + [tile(d), tile(d), pl.BlockSpec(gain.shape, lambda i: (0, 0)),
                                              pl.BlockSpec(wg.shape, lambda i: (0, 0, 0))],
        out_specs=[ANY_SPEC, pl.BlockSpec((N_META, d), lambda i: (0, 0)), pl.BlockSpec((1, d), lambda i: (0, 0))],
        out_shape=[jax.ShapeDtypeStruct((s_len, d), F32), jax.ShapeDtypeStruct((N_META, d), F32),
                   jax.ShapeDtypeStruct((1, d), F32)],
        scratch_shapes=[pltpu.VMEM((3, tm, d), F32), pltpu.SemaphoreType.DMA((3,))],
        args=(*dparts, hp, dh2, gain, wg),
    )


def _pair_sum(buf, recv, c_arr, tr, name):
    _, rows, cols = buf.shape

    def body(c_ref, mine_ref, got_ref, out_ref):
        out_ref[...] = (mine_ref[...] + got_ref[...]).astype(BF16)

    grid_spec = pltpu.PrefetchScalarGridSpec(
        num_scalar_prefetch=1,
        grid=(N_CHIPS, rows // tr),
        in_specs=[pl.BlockSpec((1, tr, cols), lambda jj, r, c_ref: (2 * jj + c_ref[0], r, 0)),
                  pl.BlockSpec((1, tr, cols), lambda jj, r, c_ref: (jj, r, 0))],
        out_specs=pl.BlockSpec((1, tr, cols), lambda jj, r, c_ref: (jj, r, 0)),
    )
    return pl.pallas_call(
        body,
        name=name,
        grid_spec=grid_spec,
        out_shape=jax.ShapeDtypeStruct((N_CHIPS, rows, cols), BF16),
    )(c_arr, buf, recv)


def _pair_exchange_sum(buf, c_arr, tr, name):
    _, rows, cols = buf.shape
    per = rows // tr

    def body(c_ref, src_ref, mine_ref, out_ref, got, send_sems, recv_sems):
        jj, r = pl.program_id(0), pl.program_id(1)
        x, y, c, _ = _position()
        copies = [_remote(src_ref.at[2 * k + 1 - c], got.at[k], send_sems, recv_sems, k, (x, y, 1 - c))
                  for k in range(N_CHIPS)]

        @pl.when((jj == 0) & (r == 0))
        def _():
            _peer_barrier([(x, y, 1 - c)])
            for cp in copies:
                cp.start()

        for k in range(N_CHIPS):
            @pl.when((jj == k) & (r == 0))
            def _():
                copies[k].wait_recv()

        rows_r = pl.ds(pl.multiple_of(r * tr, tr), tr)
        out_ref[0] = (mine_ref[0] + got[jj, rows_r, :]).astype(BF16)

        @pl.when((jj == N_CHIPS - 1) & (r == per - 1))
        def _():
            for cp in copies:
                cp.wait_send()

    grid_spec = pltpu.PrefetchScalarGridSpec(
        num_scalar_prefetch=1,
        grid=(N_CHIPS, per),
        in_specs=[ANY_SPEC, pl.BlockSpec((1, tr, cols), lambda jj, r, c_ref: (2 * jj + c_ref[0], r, 0))],
        out_specs=pl.BlockSpec((1, tr, cols), lambda jj, r, c_ref: (jj, r, 0)),
        scratch_shapes=[pltpu.VMEM((N_CHIPS, rows, cols), F32), pltpu.SemaphoreType.DMA((N_CHIPS,)),
                        pltpu.SemaphoreType.DMA((N_CHIPS,))],
    )
    return pl.pallas_call(
        body,
        name=name,
        grid_spec=grid_spec,
        out_shape=jax.ShapeDtypeStruct((N_CHIPS, rows, cols), BF16),
        compiler_params=pltpu.CompilerParams(dimension_semantics=("arbitrary", "arbitrary"),
                                             vmem_limit_bytes=VMEM_LIMIT, collective_id=5),
    )(c_arr, buf, buf)


def _chip_sum(mine, got, j_arr, tr, name, loss_part=None):
    _, rows, cols = got.shape
    extra = [] if loss_part is None else [loss_part]

    def body(j_ref, mine_ref, got_ref, *rest):
        out_ref = rest[-1]
        j = j_ref[0]
        acc = None
        for jj in range(N_CHIPS):
            term = jnp.where(j == jj, mine_ref[0], got_ref[jj]).astype(F32)
            acc = term if acc is None else acc + term
        out_ref[...] = acc
        if loss_part is not None:
            out_ref[ROW_LOSS:ROW_LOSS + 1, :] = rest[0][0:1, :]

    grid_spec = pltpu.PrefetchScalarGridSpec(
        num_scalar_prefetch=1,
        grid=(rows // tr,),
        in_specs=[pl.BlockSpec((1, tr, cols), lambda r, j_ref: (j_ref[0], r, 0)),
                  pl.BlockSpec((N_CHIPS, tr, cols), lambda r, j_ref: (0, r, 0))] +
                 [pl.BlockSpec(e.shape, lambda r, j_ref: (0, 0)) for e in extra],
        out_specs=pl.BlockSpec((tr, cols), lambda r, j_ref: (r, 0)),
    )
    return pl.pallas_call(
        body,
        name=name,
        grid_spec=grid_spec,
        out_shape=jax.ShapeDtypeStruct((rows, cols), F32),
    )(j_arr, mine, got, *extra)


def _finish_exchange(f_in, f_small):
    def body(fin_ref, fs_ref, rin_ref, os_ref, send_sems, recv_sems, local_sem):
        x, y, c, chips = _position()
        j = 2 * x + y
        me = 2 * j + c
        sibling = (x, y, 1 - c)
        _peer_barrier([sibling] + [(cx, cy, c) for cx, cy in chips])
        local = pltpu.make_async_copy(fs_ref, os_ref.at[me], local_sem)
        local.start()

        def copy(k, src, dst, to):
            return _remote(src, dst, send_sems, recv_sems, k, to)

        first = [copy(0, fin_ref, rin_ref, sibling), copy(1, fs_ref, os_ref.at[me], sibling)]
        first += [copy(2 + k, fs_ref, os_ref.at[me], (cx, cy, c)) for k, (cx, cy) in enumerate(chips)]
        for cp in first:
            cp.start()
        passed = []
        for k, (cx, cy) in enumerate(chips):
            unit = 2 * (2 * cx + cy) + c
            copy(2 + k, fs_ref, os_ref.at[unit], sibling).wait_recv()
            fwd = copy(5 + k, os_ref.at[unit], os_ref.at[unit], sibling)
            fwd.start()
            passed.append(fwd)
        copy(0, fin_ref, rin_ref, sibling).wait_recv()
        copy(1, fs_ref, os_ref.at[2 * j + 1 - c], sibling).wait_recv()
        for k, (cx, cy) in enumerate(chips):
            unit = 2 * (2 * cx + cy) + 1 - c
            copy(5 + k, fs_ref, os_ref.at[unit], sibling).wait_recv()
        for cp in first + passed:
            cp.wait_send()
        local.wait()

    return pl.pallas_call(
        body,
        name="grad_finish_exchange",
        in_specs=[ANY_SPEC] * 2,
        out_specs=[ANY_SPEC] * 2,
        out_shape=[jax.ShapeDtypeStruct(f_in.shape, F32), jax.ShapeDtypeStruct((N_DEV,) + f_small.shape, F32)],
        scratch_shapes=[pltpu.SemaphoreType.DMA((8,)), pltpu.SemaphoreType.DMA((8,)), pltpu.SemaphoreType.DMA],
        compiler_params=pltpu.CompilerParams(collective_id=8),
    )(f_in, f_small)


def _adamw_math(w, g, m, v):
    m = ADAM_B1 * m + (1.0 - ADAM_B1) * g
    v = ADAM_B2 * v + (1.0 - ADAM_B2) * (g * g)
    m_hat = m / (1.0 - ADAM_B1 ** ADAM_STEP)
    v_hat = v / (1.0 - ADAM_B2 ** ADAM_STEP)
    delta = -ADAM_LR * (m_hat / (jnp.sqrt(v_hat) + ADAM_EPS) + ADAM_WD * w)
    return delta, m, v


def _adamw_big(w, g_mine, g_sib, m, v, c_arr, tr, name):
    rows, cols = w.shape
    half = rows // 2
    per = half // tr

    def body(c_ref, w_ref, gm_ref, gs_ref, m_ref, v_ref, g_ref, d_ref, mo_ref, vo_ref):
        g = jnp.where(pl.program_id(0) == c_ref[0], gm_ref[...], gs_ref[...])
        g_ref[...] = g
        d_ref[...], mo_ref[...], vo_ref[...] = _adamw_math(w_ref[...], g, m_ref[...], v_ref[...])

    full = pl.BlockSpec((tr, cols), lambda h, r, c_ref: (h * per + r, 0))
    unit = pl.BlockSpec((tr, cols), lambda h, r, c_ref: (r, 0))
    grid_spec = pltpu.PrefetchScalarGridSpec(
        num_scalar_prefetch=1,
        grid=(2, per),
        in_specs=[full, unit, unit, full, full],
        out_specs=[full] * 4,
    )
    return pl.pallas_call(
        body,
        name=name,
        grid_spec=grid_spec,
        out_shape=[jax.ShapeDtypeStruct(w.shape, F32)] * 4,
    )(c_arr, w, g_mine, g_sib, m, v)


def _adamw_small(j_arr, packed, params):
    names = list(params)
    n = len(names)

    def body(j_ref, pk_ref, *refs):
        ins = refs[:3 * n]
        outs = refs[3 * n:]
        j = j_ref[0]

        def shard(row, rows):
            return jnp.concatenate([pk_ref[2 * j, row:row + rows, :], pk_ref[2 * j + 1, row:row + rows, :]], axis=1)

        def tail_sum(unit, row, rows):
            start = pl.multiple_of(UNIT_ROWS + TAIL_ROWS * unit + row, SUBLANES)
            total = pk_ref[0, pl.ds(start, rows), :]
            for dev in range(1, N_DEV):
                total = total + pk_ref[dev, pl.ds(start, rows), :]
            return total

        for idx, name in enumerate(names):
            if name == "w_rg":
                g = pk_ref[:, ROW_WR:ROW_WR + LANES, :]
            elif name == "w_ig":
                g = pk_ref[:, ROW_WI:ROW_WI + LANES, :]
            elif name == "meta_tokens":
                g = jnp.concatenate([tail_sum(2 * j, 0, N_META), tail_sum(2 * j + 1, 0, N_META)], axis=1)
            elif name == "norm_gain":
                g = jnp.concatenate([tail_sum(u, N_META, SUBLANES)[0:1] for u in range(N_DEV)], axis=1)
            elif name == "conv_w":
                g = shard(ROW_CONV, 4)
            else:
                row = ROW_VEC + VEC_NAMES.index(name)
                g = jnp.concatenate([pk_ref[u, row:row + 1, :] for u in range(N_DEV)], axis=1)
            w_ref, m_ref, v_ref = ins[3 * idx:3 * idx + 3]
            delta, m, v = _adamw_math(w_ref[...], g, m_ref[...], v_ref[...])
            g_ref, d_ref, mo_ref, vo_ref = outs[4 * idx:4 * idx + 4]
            g_ref[...], d_ref[...], mo_ref[...], vo_ref[...] = g, delta, m, v
        total = pk_ref[0, ROW_LOSS:ROW_LOSS + 1, :]
        for u in range(1, N_DEV):
            total = total + pk_ref[u, ROW_LOSS:ROW_LOSS + 1, :]
        outs[4 * n][...] = jnp.broadcast_to(total, (SUBLANES, LANES))

    flat_in, out_shape = [], []
    for name in names:
        w, m, v = params[name]
        flat_in += [w, m, v]
        out_shape += [jax.ShapeDtypeStruct(w.shape, F32)] * 4
    out_shape.append(jax.ShapeDtypeStruct((SUBLANES, LANES), F32))
    res = pl.pallas_call(
        body,
        name="adamw_small",
        in_specs=[SMEM_SPEC, VMEM_SPEC] + [VMEM_SPEC] * (3 * n),
        out_specs=[VMEM_SPEC] * (4 * n + 1),
        out_shape=out_shape,
    )(j_arr, packed, *flat_in)
    return {name: tuple(res[4 * idx:4 * idx + 4]) for idx, name in enumerate(names)}, res[4 * n][0, 0]


def _units(a):
    rows = a.shape[0]
    return jnp.transpose(a.reshape(rows, N_DEV, LANES), (1, 0, 2))


def kernel(x, meta_tokens, norm_gain, w_in, conv_w, conv_b, w_rg, b_rg, w_ig, b_ig, lru_lambda, ret_norm_gain, w_out, final_norm_gain, loss_target, m_meta_tokens, m_norm_gain, m_w_in, m_conv_w, m_conv_b, m_w_rg, m_b_rg, m_w_ig, m_b_ig, m_lru_lambda, m_ret_norm_gain, m_w_out, m_final_norm_gain, v_meta_tokens, v_norm_gain, v_w_in, v_conv_w, v_conv_b, v_w_rg, v_b_rg, v_w_ig, v_b_ig, v_lru_lambda, v_ret_norm_gain, v_w_out, v_final_norm_gain):
    s_len, d = x.shape[1], x.shape[2]
    d_lru = w_rg.shape[1] * w_rg.shape[2]
    d_ret = ret_norm_gain.shape[1]
    d_qk = HEADS * QK_DIM
    tp = s_len + CHUNK
    tm = TOKEN_TILE
    assert tp % tm == 0 and d_lru == HEADS * LANES and d_ret == HEADS * LANES
    ax, ay, ac = lax.axis_index("x"), lax.axis_index("y"), lax.axis_index("c")
    c_arr = jnp.reshape(ac, (1,)).astype(jnp.int32)
    j_arr = jnp.reshape(2 * ax + ay, (1,)).astype(jnp.int32)

    small = jnp.concatenate([meta_tokens, conv_w[0], jnp.zeros((4, meta_tokens.shape[1]), F32)], axis=0)
    wg, sg = _gather_weights(w_in[0], small)
    cols = sg.shape[2]
    meta_full = jnp.transpose(sg[:, :N_META, :], (1, 0, 2)).reshape(N_META, N_CHIPS * cols)
    cw_full = jnp.transpose(sg[:, N_META:N_META + 4, :], (1, 0, 2)).reshape(4, N_CHIPS * cols)
    cw8 = jnp.concatenate([cw_full, jnp.zeros((4, cw_full.shape[1]), F32)], axis=0)

    half = QK_DIM // 2
    inv = ROPE_BASE ** (-jnp.arange(half, dtype=F32) / half)
    pos = (jnp.arange(tp) - PAD_ROWS).astype(F32)
    ang = pos[:, None] * inv[None, :]
    cos_t = jnp.tile(jnp.cos(ang), (1, LANES // half))
    sign = jnp.where((jnp.arange(LANES) % QK_DIM) < half, -1.0, 1.0).astype(F32)
    sin_t = jnp.tile(jnp.sin(ang), (1, LANES // half)) * sign[None, :]
    tables = _ret_tables()
    gain_f = final_norm_gain.reshape(1, d)

    hp, lx, lg, *qkv, rg, wo4 = _in_proj(x[0], meta_full, norm_gain, wg, cos_t, sin_t, w_out[0], tables[1], tables[2],
                                         tm, d_lru, d_qk, d_ret)
    wo = wo4.reshape(N_CHIPS * wo4.shape[1], wo4.shape[2])
    hl, y_lru, *lru_saved = _lru_fwd(lx, lg, cw8, conv_b, w_rg[0], b_rg, w_ig[0], b_ig, lru_lambda, tm)
    o, y_ret, rprev = _ret_fwd(*qkv, rg, ret_norm_gain, tables, tm)
    dh2, dy_lru, dy_ret, dwo, dgf, loss_acc = _out_proj_loss(y_lru, y_ret, hp, loss_target[0], wo, gain_f, tm)

    g_out = dwo.reshape(N_DEV, dwo.shape[0] // N_DEV, dwo.shape[1])
    (dq, dk, dv, drg, dgain), (r_out,) = _ret_bwd(*qkv, rg, o, rprev, dy_ret, ret_norm_gain, cos_t, sin_t, tables,
                                                 tm, ride=_pair_ride([g_out]))
    q_out = _pair_sum(g_out, r_out, c_arr, REDUCE_TILE, "grad_pair_sum_out")
    (dlx, dlg, g_small), (e_out,) = _lru_bwd(lx, lg, hl, dy_lru, lru_saved, cw8, w_rg[0], w_ig[0], lru_lambda,
                                            dgain, dgf, tm, ride=_chip_ride([q_out]))
    f_out = _chip_sum(q_out, e_out, j_arr, REDUCE_TILE, "grad_chip_sum_out")
    dparts = [dlx, dlg, dq, dk, dv, drg]
    dwg, (s_out, r_small) = _in_proj_dw(dparts, hp, norm_gain, wg.shape,
                                        ride=_join_rides(_sibling_ride([f_out]), _pair_ride([g_small])))
    g_in = dwg.reshape(N_DEV, dwg.shape[1] // 2, dwg.shape[2])
    q_in = _pair_exchange_sum(g_in, c_arr, REDUCE_TILE, "grad_pair_exchange_sum_in")
    q_small = _pair_sum(g_small, r_small, c_arr, UNIT_ROWS, "grad_pair_sum_small")
    (grad_x, dmeta, dg1), (e_in, e_small) = _in_proj_dx(dparts, hp, dh2, norm_gain, wg, s_len, tm,
                                                        ride=_chip_ride([q_in, q_small]))
    f_in = _chip_sum(q_in, e_in, j_arr, REDUCE_TILE, "grad_chip_sum_in")
    f_small = _chip_sum(q_small, e_small, j_arr, UNIT_ROWS, "grad_chip_sum_small", loss_part=loss_acc)
    tail = jnp.concatenate([_units(dmeta), _units(dg1), jnp.zeros((N_DEV, TAIL_ROWS - N_META - 1, LANES), F32)],
                           axis=1).reshape(N_DEV * TAIL_ROWS, LANES)
    s_in, o_small = _finish_exchange(f_in, jnp.concatenate([f_small, tail], axis=0))

    res_in = _adamw_big(w_in[0], f_in, s_in, m_w_in[0], v_w_in[0], c_arr, REDUCE_TILE, "adamw_w_in")
    res_out = _adamw_big(w_out[0], f_out, s_out, m_w_out[0], v_w_out[0], c_arr, REDUCE_TILE, "adamw_w_out")
    small_params = {
        "meta_tokens": (meta_tokens, m_meta_tokens, v_meta_tokens),
        "norm_gain": (norm_gain, m_norm_gain, v_norm_gain),
        "conv_w": (conv_w[0], m_conv_w[0], v_conv_w[0]),
        "conv_b": (conv_b, m_conv_b, v_conv_b),
        "w_rg": (w_rg[0], m_w_rg[0], v_w_rg[0]),
        "b_rg": (b_rg, m_b_rg, v_b_rg),
        "w_ig": (w_ig[0], m_w_ig[0], v_w_ig[0]),
        "b_ig": (b_ig, m_b_ig, v_b_ig),
        "lru_lambda": (lru_lambda, m_lru_lambda, v_lru_lambda),
        "ret_norm_gain": (ret_norm_gain, m_ret_norm_gain, v_ret_norm_gain),
        "final_norm_gain": (gain_f, m_final_norm_gain.reshape(1, d), v_final_norm_gain.reshape(1, d)),
    }
    res, loss = _adamw_small(j_arr, o_small, small_params)
    res["w_in"] = tuple(res_in)
    res["w_out"] = tuple(res_out)

    order = ["meta_tokens", "norm_gain", "w_in", "conv_w", "conv_b", "w_rg", "b_rg", "w_ig", "b_ig", "lru_lambda",
             "ret_norm_gain", "w_out", "final_norm_gain"]
    shapes = {"w_in": w_in.shape, "conv_w": conv_w.shape, "w_rg": w_rg.shape, "w_ig": w_ig.shape,
              "w_out": w_out.shape, "final_norm_gain": final_norm_gain.shape}
    outs = [loss, grad_x.reshape(x.shape)]
    for kind in range(4):
        for name in order:
            a = res[name][kind]
            outs.append(a.reshape(shapes[name]) if name in shapes else a)
    return tuple(outs)
```

```python
import functools

import jax
import jax.numpy as jnp
from jax import lax
from jax.experimental import pallas as pl
from jax.experimental.pallas import tpu as pltpu

F32 = jnp.float32
BF16 = jnp.bfloat16

N_META = 16
CHUNK = 128
PAD_ROWS = CHUNK - N_META
HEADS = 8
QK_DIM = 64
LANES = 128
SUBLANES = 8
LRU_C = 8.0
EPS = 1e-6
ROPE_BASE = 10000.0
ADAM_LR = 0.001
ADAM_B1 = 0.9
ADAM_B2 = 0.999
ADAM_EPS = 1e-08
ADAM_WD = 0.01
ADAM_STEP = 10
N_CHIPS = 4
N_DEV = 8
TOKEN_TILE = 384
REDUCE_TILE = 256
VMEM_LIMIT = 58 * 1024 * 1024
MESH = pl.DeviceIdType.MESH

VMEM_SPEC = pl.BlockSpec(memory_space=pltpu.VMEM)
SMEM_SPEC = pl.BlockSpec(memory_space=pltpu.SMEM)
ANY_SPEC = pl.BlockSpec(memory_space=pl.ANY)

ROW_WR, ROW_WI, ROW_META, ROW_CONV, ROW_VEC, UNIT_ROWS = 0, 128, 256, 272, 276, 288
VEC_NAMES = ["norm_gain", "conv_b", "b_rg", "b_ig", "lru_lambda", "ret_norm_gain", "final_norm_gain"]
N_VEC = len(VEC_NAMES)
ROW_LOSS = ROW_VEC + N_VEC
TAIL_ROWS = 24


def _dot(a, b):
    return jnp.dot(a, b, preferred_element_type=F32)


def _dot_nt(a, b):
    return lax.dot_general(a, b, (((1,), (1,)), ((), ())), preferred_element_type=F32)


def _dot_tn(a, b):
    return lax.dot_general(a, b, (((0,), (0,)), ((), ())), preferred_element_type=F32)


def _sigmoid(x):
    return 0.5 * jnp.tanh(0.5 * x) + 0.5


def _shift_down(x, prev8, s):
    rolled = pltpu.roll(x, s, 0)
    rows = lax.broadcasted_iota(jnp.int32, (SUBLANES, x.shape[1]), 0)
    top = jnp.where(rows < s, pltpu.roll(prev8, s, 0), rolled[0:SUBLANES])
    return jnp.concatenate([top, rolled[SUBLANES:]], axis=0)


def _shift_up(x, next8, s):
    n = x.shape[0]
    rolled = pltpu.roll(x, n - s, 0)
    rows = lax.broadcasted_iota(jnp.int32, (SUBLANES, x.shape[1]), 0)
    bot = jnp.where(rows >= SUBLANES - s, pltpu.roll(next8, SUBLANES - s, 0), rolled[n - SUBLANES:n])
    return jnp.concatenate([rolled[:n - SUBLANES], bot], axis=0)


def _rot_partner(t):
    w = t.shape[1]
    lane = lax.broadcasted_iota(jnp.int32, t.shape, 1)
    first = (lane % QK_DIM) < (QK_DIM // 2)
    return jnp.where(first, pltpu.roll(t, w - QK_DIM // 2, 1), pltpu.roll(t, QK_DIM // 2, 1))


def _tile_lanes(t, reps):
    return jnp.concatenate([t] * reps, axis=1)


class _Ride:
    def __init__(self, srcs, dst_shapes, n_copies, make, to_sibling=False, to_chips=False):
        self.srcs, self.dst_shapes, self.n_copies, self.make = list(srcs), list(dst_shapes), n_copies, make
        self.to_sibling, self.to_chips = to_sibling, to_chips

    def peers(self):
        x, y, c, chips = _position()
        return ([(x, y, 1 - c)] if self.to_sibling else []) + ([(cx, cy, c) for cx, cy in chips] if self.to_chips else [])


def _join_rides(a, b):
    def make(src, dst, send_sems, recv_sems, base):
        na, da = len(a.srcs), len(a.dst_shapes)
        return (a.make(src[:na], dst[:da], send_sems, recv_sems, base)
                + b.make(src[na:], dst[da:], send_sems, recv_sems, base + a.n_copies))

    return _Ride(a.srcs + b.srcs, a.dst_shapes + b.dst_shapes, a.n_copies + b.n_copies, make,
                 a.to_sibling or b.to_sibling, a.to_chips or b.to_chips)


def _position():
    x, y, c = lax.axis_index("x"), lax.axis_index("y"), lax.axis_index("c")
    return x, y, c, [(1 - x, y), (x, 1 - y), (1 - x, 1 - y)]


def _peer_barrier(peers):
    barrier = pltpu.get_barrier_semaphore()
    for peer in peers:
        pl.semaphore_signal(barrier, inc=1, device_id=peer, device_id_type=MESH)
    pl.semaphore_wait(barrier, len(peers))


def _remote(src, dst, send_sems, recv_sems, k, to):
    return pltpu.make_async_remote_copy(src_ref=src, dst_ref=dst, send_sem=send_sems.at[k], recv_sem=recv_sems.at[k],
                                        device_id=to, device_id_type=MESH)


def _pair_ride(bufs):
    def make(src, dst, send_sems, recv_sems, base):
        x, y, c, _ = _position()
        return [_remote(src[b].at[2 * jj + 1 - c], dst[b].at[jj], send_sems, recv_sems, base + b * N_CHIPS + jj,
                        (x, y, 1 - c)) for b in range(len(bufs)) for jj in range(N_CHIPS)]

    shapes = [jax.ShapeDtypeStruct((N_CHIPS,) + b.shape[1:], b.dtype) for b in bufs]
    return _Ride(bufs, shapes, N_CHIPS * len(bufs), make, to_sibling=True)


def _chip_ride(bufs):
    def make(src, dst, send_sems, recv_sems, base):
        x, y, c, chips = _position()
        return [_remote(src[b].at[2 * cx + cy], dst[b].at[2 * x + y], send_sems, recv_sems, base + b * 3 + k,
                        (cx, cy, c)) for b in range(len(bufs)) for k, (cx, cy) in enumerate(chips)]

    shapes = [jax.ShapeDtypeStruct(b.shape, b.dtype) for b in bufs]
    return _Ride(bufs, shapes, 3 * len(bufs), make, to_chips=True)


def _sibling_ride(bufs):
    def make(src, dst, send_sems, recv_sems, base):
        x, y, c, _ = _position()
        return [_remote(src[b], dst[b], send_sems, recv_sems, base + b, (x, y, 1 - c)) for b in range(len(bufs))]

    shapes = [jax.ShapeDtypeStruct(b.shape, b.dtype) for b in bufs]
    return _Ride(bufs, shapes, len(bufs), make, to_sibling=True)


def _hosted_call(body, ride, n_steps, *, name, in_specs, out_specs, out_shape, scratch_shapes, args, barrier_id=None):
    params = pltpu.CompilerParams(dimension_semantics=("arbitrary",), vmem_limit_bytes=VMEM_LIMIT,
                                  collective_id=barrier_id if ride is not None else None)
    if ride is None:
        res = pl.pallas_call(body, name=name, grid=(n_steps,), in_specs=list(in_specs), out_specs=list(out_specs),
                             out_shape=list(out_shape), scratch_shapes=list(scratch_shapes),
                             compiler_params=params)(*args)
        return list(res), []
    sizes = [len(in_specs), len(ride.srcs), len(out_specs), len(ride.dst_shapes), len(scratch_shapes), 2]

    def hosted(*refs):
        groups, pos = [], 0
        for n in sizes:
            groups.append(refs[pos:pos + n])
            pos += n
        ins, rin, outs, rout, scr, (send_sems, recv_sems) = groups
        i = pl.program_id(0)

        @pl.when(i == 0)
        def _():
            if barrier_id is not None:
                _peer_barrier(ride.peers())
            for cp in ride.make(rin, rout, send_sems, recv_sems, 0):
                cp.start()

        body(*ins, *outs, *scr)

        @pl.when(i == n_steps - 1)
        def _():
            for cp in ride.make(rin, rout, send_sems, recv_sems, 0):
                cp.wait()

    n_out = len(out_specs)
    res = pl.pallas_call(
        hosted,
        name=name,
        grid=(n_steps,),
        in_specs=list(in_specs) + [ANY_SPEC] * len(ride.srcs),
        out_specs=list(out_specs) + [ANY_SPEC] * len(ride.dst_shapes),
        out_shape=list(out_shape) + ride.dst_shapes,
        scratch_shapes=list(scratch_shapes) + [pltpu.SemaphoreType.DMA((ride.n_copies,)),
                                               pltpu.SemaphoreType.DMA((ride.n_copies,))],
        compiler_params=params,
    )(*args, *ride.srcs)
    return list(res[:n_out]), list(res[n_out:])


def _gather_weights(w_in, small):
    r_in, c_in = w_in.shape
    h_in = r_in // 2
    q_in = h_in // 2

    def body(win_ref, small_ref, wg_ref, sg_ref, mine, send_sems, recv_sems, local_sem):
        x, y, c, chips = _position()
        j = 2 * x + y
        sibling = (x, y, 1 - c)
        xn, yn, dg = chips
        jx, jy, jd = (2 * cx + cy for cx, cy in chips)

        def half(jj, cc):
            return wg_ref.at[jj, pl.ds(cc * h_in, h_in), :]

        def quarter(jj, qq):
            return wg_ref.at[jj, pl.ds(c * h_in + qq * q_in, q_in), :]

        def copy(k, ref, to):
            return _remote(ref, ref, send_sems, recv_sems, k, to)

        def send_mine(k, qq, to):
            rows = pl.ds(c * h_in + qq * q_in, q_in)
            return _remote(mine.at[rows, :], wg_ref.at[j, rows, :], send_sems, recv_sems, k, to)

        def cast_rows(start, rows):
            start = pl.multiple_of(start, q_in)
            mine[pl.ds(start, rows), :] = win_ref[pl.ds(start, rows), :].astype(BF16)

        first = [send_mine(0, 0, (*xn, c)), send_mine(2, 1, (*yn, c)), send_mine(1, 1, (*xn, c)),
                 send_mine(3, 0, (*yn, c))]
        sg_ref[j] = small_ref[...]
        cast_rows(c * h_in, q_in)
        _peer_barrier([sibling] + [(cx, cy, c) for cx, cy in chips])
        first[0].start()
        cast_rows(c * h_in + q_in, q_in)
        for cp in first[1:]:
            cp.start()
        small_copies = [copy(9 + k, sg_ref.at[j], (cx, cy, c)) for k, (cx, cy) in enumerate(chips)]
        for cp in small_copies:
            cp.start()
        first += small_copies
        cast_rows((1 - c) * h_in, h_in)
        keep = pltpu.make_async_copy(mine, wg_ref.at[j], local_sem)
        keep.start()
        copy(0, quarter(jx, 0), sibling).wait_recv()
        along_y = copy(4, quarter(jx, 0), (*yn, c))
        along_y.start()
        copy(2, quarter(jy, 1), sibling).wait_recv()
        along_x = copy(5, quarter(jy, 1), (*xn, c))
        along_x.start()
        copy(1, quarter(jx, 1), sibling).wait_recv()
        to_sib = [copy(6, half(jx, c), sibling)]
        to_sib[-1].start()
        copy(3, quarter(jy, 0), sibling).wait_recv()
        to_sib.append(copy(7, half(jy, c), sibling))
        to_sib[-1].start()
        copy(4, quarter(jd, 0), sibling).wait_recv()
        copy(5, quarter(jd, 1), sibling).wait_recv()
        to_sib.append(copy(8, half(jd, c), sibling))
        to_sib[-1].start()
        for k, jk in enumerate((jx, jy, jd)):
            copy(6 + k, half(jk, 1 - c), sibling).wait_recv()
            copy(9 + k, sg_ref.at[jk], sibling).wait_recv()
        for cp in first + [along_y, along_x] + to_sib:
            cp.wait_send()
        keep.wait()

    return pl.pallas_call(
        body,
        name="gather_weights",
        out_shape=(jax.ShapeDtypeStruct((N_CHIPS, r_in, c_in), BF16),
                   jax.ShapeDtypeStruct((N_CHIPS,) + small.shape, F32)),
        in_specs=[VMEM_SPEC, VMEM_SPEC],
        out_specs=(ANY_SPEC, VMEM_SPEC),
        scratch_shapes=[pltpu.VMEM((r_in, c_in), BF16), pltpu.SemaphoreType.DMA((12,)),
                        pltpu.SemaphoreType.DMA((12,)), pltpu.SemaphoreType.DMA],
        compiler_params=pltpu.CompilerParams(vmem_limit_bytes=VMEM_LIMIT, collective_id=6),
    )(w_in, small)


def _proj_segments(d_lru, d_qk, d_ret, chunk_w):
    widths = [d_lru, d_lru, d_qk, d_qk, d_ret, d_ret]
    segs, col = [], 0
    for w in widths:
        parts, off = [], 0
        while off < w:
            jj, inner = divmod(col + off, chunk_w)
            take = min(w - off, chunk_w - inner)
            parts.append((jj, inner, off, take))
            off += take
        segs.append(parts)
        col += w
    return segs


def _in_proj(x2, meta, gain, wg, cos_t, sin_t, w_out, qdec, kdec, tm, d_lru, d_qk, d_ret):
    s_len, d = x2.shape
    tp = s_len + CHUNK
    nt, nb = tp // tm, tm // CHUNK
    segs = _proj_segments(d_lru, d_qk, d_ret, wg.shape[2])
    outs = [(d, F32), (d_lru, F32), (d_lru, F32)] + [(d_qk, BF16)] * 4 + [(d_ret, BF16), (d_ret, F32)]
    r_out, c_out = w_out.shape
    h_out = r_out // 2
    fwd_step = min(6, nt - 1)

    def gather_w_out(i, wout_ref, wo_ref, wob, send_sems, recv_sems, local_sem):
        x, y, c, chips = _position()
        j = 2 * x + y
        sibling = (x, y, 1 - c)

        def half(jj, cc):
            return wo_ref.at[jj, pl.ds(cc * h_out, h_out), :]

        local = pltpu.make_async_copy(wob, wo_ref.at[j], local_sem)
        first = [_remote(wob.at[pl.ds(c * h_out, h_out), :], half(j, c), send_sems, recv_sems, k, (cx, cy, c))
                 for k, (cx, cy) in enumerate(chips)]
        passed = [_remote(half(2 * cx + cy, c), half(2 * cx + cy, c), send_sems, recv_sems, 3 + k, sibling)
                  for k, (cx, cy) in enumerate(chips)]

        @pl.when(i == 0)
        def _():
            wob[...] = wout_ref[...].astype(BF16)
            _peer_barrier([sibling] + [(cx, cy, c) for cx, cy in chips])
            local.start()
            for cp in first:
                cp.start()

        @pl.when(i == fwd_step)
        def _():
            for k, (cx, cy) in enumerate(chips):
                _remote(half(2 * cx + cy, c), half(2 * cx + cy, c), send_sems, recv_sems, k, sibling).wait_recv()
                passed[k].start()

        @pl.when(i == nt - 1)
        def _():
            for k, (cx, cy) in enumerate(chips):
                jk = 2 * cx + cy
                _remote(half(jk, 1 - c), half(jk, 1 - c), send_sems, recv_sems, 3 + k, sibling).wait_recv()
            for cp in first + passed:
                cp.wait_send()
            local.wait()

    def body(*refs):
        xb = refs[:nb]
        meta_ref, g_ref, w_ref, cos_ref, sin_ref, wout_ref, qdec_ref, kdec_ref = refs[nb:nb + 8]
        hp_ref, lx_ref, lg_ref, qb_ref, kb_ref, qd_ref, kd_ref, vb_ref, rg_ref = refs[nb + 8:nb + 17]
        wo_ref, q_s, k_s, wob, send_sems, recv_sems, local_sem = refs[nb + 17:]
        i = pl.program_id(0)
        gather_w_out(i, wout_ref, wo_ref, wob, send_sems, recv_sems, local_sem)
        blocks = [r[...] for r in xb]
        head = jnp.concatenate([jnp.zeros((PAD_ROWS, d), F32), meta_ref[...]], axis=0)
        blocks[0] = jnp.where(i == 0, head, blocks[0])
        h = jnp.concatenate(blocks, axis=0)
        hp_ref[...] = h
        rinv = lax.rsqrt(jnp.mean(h * h, axis=-1, keepdims=True) + EPS)
        u = ((h * rinv) * g_ref[...]).astype(BF16)
        for out_ref, parts in zip([lx_ref, lg_ref, q_s, k_s, vb_ref, rg_ref], segs):
            for jj, inner, off, take in parts:
                out_ref[:, off:off + take] = _dot(u, w_ref[jj, :, inner:inner + take]).astype(out_ref.dtype)
        cos = _tile_lanes(cos_ref[...], d_qk // LANES)
        sin = _tile_lanes(sin_ref[...], d_qk // LANES)
        q = q_s[...]
        q = q * cos + _rot_partner(q) * sin
        k = k_s[...]
        k = (k * cos + _rot_partner(k) * sin) * (QK_DIM ** -0.5)
        qb_ref[...] = q.astype(BF16)
        kb_ref[...] = k.astype(BF16)
        qd_ref[...] = (q * jnp.concatenate([qdec_ref[...]] * nb, axis=0)).astype(BF16)
        kd_ref[...] = (k * jnp.concatenate([kdec_ref[...]] * nb, axis=0)).astype(BF16)

    x_specs = [pl.BlockSpec((CHUNK, d), functools.partial(lambda i, b: (jnp.maximum(i * nb + b - 1, 0), 0), b=b))
               for b in range(nb)]
    tile = lambda w: pl.BlockSpec((tm, w), lambda i: (i, 0))
    return pl.pallas_call(
        body,
        name="in_proj",
        grid=(nt,),
        in_specs=x_specs + [pl.BlockSpec(meta.shape, lambda i: (0, 0)),
                            pl.BlockSpec(gain.shape, lambda i: (0, 0)),
                            pl.BlockSpec(wg.shape, lambda i: (0, 0, 0)),
                            tile(LANES), tile(LANES),
                            pl.BlockSpec(w_out.shape, lambda i: (0, 0)),
                            pl.BlockSpec(qdec.shape, lambda i: (0, 0)), pl.BlockSpec(kdec.shape, lambda i: (0, 0))],
        out_specs=[tile(w) for w, _ in outs] + [ANY_SPEC],
        out_shape=[jax.ShapeDtypeStruct((tp, w), dt) for w, dt in outs]
                  + [jax.ShapeDtypeStruct((N_CHIPS, r_out, c_out), BF16)],
        scratch_shapes=[pltpu.VMEM((tm, d_qk), F32), pltpu.VMEM((tm, d_qk), F32),
                        pltpu.VMEM((r_out, c_out), BF16), pltpu.SemaphoreType.DMA((6,)),
                        pltpu.SemaphoreType.DMA((6,)), pltpu.SemaphoreType.DMA],
        compiler_params=pltpu.CompilerParams(dimension_semantics=("arbitrary",), vmem_limit_bytes=VMEM_LIMIT,
                                             collective_id=7),
    )(*([x2] * nb), meta, gain, wg, cos_t, sin_t, w_out, qdec, kdec)


def _segment_scan(a3, u3, out3, p3, carry, tm, reverse):
    groups = a3.shape[0]
    seg = tm // SUBLANES

    def step(j, state):
        hs, ps = state
        pos = (seg - 1 - j) if reverse else j
        rows = pl.ds(pos, SUBLANES, stride=seg)
        together = pl.ds(pl.multiple_of(pos * SUBLANES, SUBLANES), SUBLANES)
        new_h, new_p = [], []
        for g in range(groups):
            a = a3[g, rows, :]
            h = a * hs[g] + u3[g, rows, :]
            p = ps[g] * a
            out3[g, rows, :] = h
            p3[g, together, :] = p
            new_h.append(h)
            new_p.append(p)
        return tuple(new_h), tuple(new_p)

    zeros = tuple(jnp.zeros((SUBLANES, LANES), F32) for _ in range(groups))
    ones = tuple(jnp.ones((SUBLANES, LANES), F32) for _ in range(groups))
    lax.fori_loop(0, seg, step, (zeros, ones))
    carries = [carry[:, g * LANES:(g + 1) * LANES] for g in range(groups)]
    for s in (reversed(range(SUBLANES)) if reverse else range(SUBLANES)):
        rows = slice(s * seg, (s + 1) * seg)
        edge = s * seg if reverse else (s + 1) * seg - 1
        for g in range(groups):
            products = p3[g, pl.ds(s, seg, stride=SUBLANES), :]
            out3[g, rows, :] = out3[g, rows, :] + products * carries[g]
            carries[g] = out3[g, edge:edge + 1, :]
    return jnp.concatenate(carries, axis=1)


def _softplus_neg(lam):
    z = -lam
    e = jnp.exp(-jnp.abs(z))
    e1 = 1.0 + e
    log1p_e = jnp.where(e1 == 1.0, e, jnp.log(e1) * (e / (e1 - 1.0)))
    return jnp.maximum(z, 0.0) + log1p_e


def _lru_fwd(lx, lg, cw, cb, wr, br, wi, bi, lam, tm):
    tp, w = lx.shape
    nt = tp // tm
    per8 = tm // SUBLANES
    n_heads = wr.shape[0]

    def body(lx_ref, lxp_ref, lg_ref, cw_ref, cb_ref, wr_ref, br_ref, wi_ref, bi_ref, lam_ref,
             hl_ref, y_ref, xc_ref, r_ref, ig_ref, a_ref, beta_ref, w4_ref, a_s, u_s, h_s, p_s, carry):
        i = pl.program_id(0)

        @pl.when(i == 0)
        def _():
            carry[...] = jnp.zeros_like(carry)

        sp = _softplus_neg(lam_ref[...])
        row = lax.broadcasted_iota(jnp.int32, (tm, 1), 0) + i * tm
        for hd in range(n_heads):
            hs = slice(hd * LANES, (hd + 1) * LANES)
            lxv = lx_ref[:, hs]
            prev8 = jnp.where(i == 0, 0.0, lxp_ref[:, hs])
            xc = cb_ref[:, hs] + _shift_down(lxv, prev8, 3) * cw_ref[0:1, hs]
            xc = xc + _shift_down(lxv, prev8, 2) * cw_ref[1:2, hs]
            xc = xc + _shift_down(lxv, prev8, 1) * cw_ref[2:3, hs]
            xc = xc + lxv * cw_ref[3:4, hs]
            xc_ref[:, hs] = xc
            xh = xc.astype(BF16)
            r = _sigmoid(_dot(xh, wr_ref[hd].astype(BF16)) + br_ref[:, hs])
            ig = _sigmoid(_dot(xh, wi_ref[hd].astype(BF16)) + bi_ref[:, hs])
            r_ref[:, hs] = r
            ig_ref[:, hs] = ig
            log_a = (-LRU_C * r) * sp[:, hs]
            a = jnp.exp(log_a)
            a_ref[:, hs] = a
            a2 = a * a
            beta2 = jnp.maximum((1.0 + a2) * jnp.tanh(-log_a), 1e-37)
            rsb = lax.rsqrt(beta2)
            beta = beta2 * rsb
            beta_ref[:, hs] = beta
            w4_ref[:, hs] = a2 * rsb
            a_s[hd] = a
            u_s[hd] = jnp.where(row >= PAD_ROWS, beta * ig * xc, 0.0)
        carry[0:1, :] = _segment_scan(a_s, u_s, h_s, p_s, carry[0:1, :], tm, reverse=False)
        for hd in range(n_heads):
            hs = slice(hd * LANES, (hd + 1) * LANES)
            hl = h_s[hd]
            hl_ref[:, hs] = hl
            g = lg_ref[:, hs]
            y_ref[:, hs] = (hl * (g * _sigmoid(g))).astype(BF16)

    tile = pl.BlockSpec((tm, w), lambda i: (i, 0))
    prev = pl.BlockSpec((SUBLANES, w), lambda i: (jnp.maximum(i * per8 - 1, 0), 0))
    vec = pl.BlockSpec((1, w), lambda i: (0, 0))
    mat = pl.BlockSpec(wr.shape, lambda i: (0, 0, 0))
    f32_out = jax.ShapeDtypeStruct((tp, w), F32)
    return pl.pallas_call(
        body,
        name="lru_fwd",
        grid=(nt,),
        in_specs=[tile, prev, tile, pl.BlockSpec(cw.shape, lambda i: (0, 0)), vec, mat, vec, mat, vec, vec],
        out_specs=[tile] * 8,
        out_shape=[f32_out, jax.ShapeDtypeStruct((tp, w), BF16)] + [f32_out] * 6,
        scratch_shapes=[pltpu.VMEM((w // LANES, tm, LANES), F32)] * 4 + [pltpu.VMEM((SUBLANES, w), F32)],
        compiler_params=pltpu.CompilerParams(dimension_semantics=("arbitrary",), vmem_limit_bytes=VMEM_LIMIT),
    )(lx, lx, lg, cw, cb, wr, br, wi, bi, lam)


def _ret_tables():
    log_g = jnp.log1p(-jnp.exp2(-5.0 - jnp.arange(HEADS, dtype=F32)))
    idx = jnp.arange(CHUNK, dtype=F32)
    diff = idx[:, None] - idx[None, :]
    dmask = jnp.where(diff[None] >= 0.0, jnp.exp(jnp.maximum(diff, 0.0)[None] * log_g[:, None, None]), 0.0)
    kdec = jnp.repeat(jnp.exp((CHUNK - 1.0 - idx)[:, None] * log_g[None, :]), QK_DIM, axis=1)
    qdec = jnp.repeat(jnp.exp((idx + 1.0)[:, None] * log_g[None, :]), QK_DIM, axis=1)
    g_chunk = jnp.exp(CHUNK * log_g)
    g_rows = jnp.repeat(g_chunk, QK_DIM).reshape(HEADS // 2, 2 * QK_DIM, 1)
    g_state = jnp.broadcast_to(g_rows, (HEADS // 2, 2 * QK_DIM, 2 * LANES))
    r_head = jnp.arange(2 * QK_DIM)[:, None] // QK_DIM
    c_head = jnp.arange(2 * LANES)[None, :] // LANES
    block_diag = (r_head == c_head).astype(F32)
    return dmask, qdec, kdec, g_state, block_diag


def _head_norm(o_h):
    mu = jnp.mean(o_h, axis=-1, keepdims=True)
    oc = o_h - mu
    var = jnp.mean(oc * oc, axis=-1, keepdims=True)
    rstd = lax.rsqrt(var + EPS)
    return oc * rstd, rstd


def _ret_fwd(qb, kb, qd, kd, vb, rg, gain, tables, tm):
    tp, d_qk = qb.shape
    d_ret = vb.shape[1]
    n_ch = tp // CHUNK
    cps = tm // CHUNK
    n_pairs = HEADS // 2
    dmask, _, _, g_state, block_diag = tables

    def body(q_ref, k_ref, qd_ref, kd_ref, v_ref, rg_ref, gain_ref, dm_ref, gs_ref, bd_ref,
             o_ref, y_ref, rp_ref, state):
        n = pl.program_id(0)

        @pl.when(n == 0)
        def _():
            state[...] = jnp.zeros_like(state)

        lane = lax.broadcasted_iota(jnp.int32, (CHUNK, LANES), 1)
        for ci in range(cps):
            rs = slice(ci * CHUNK, (ci + 1) * CHUNK)
            for p in range(n_pairs):
                qs = slice(p * LANES, (p + 1) * LANES)
                vs = slice(p * 2 * LANES, (p + 1) * 2 * LANES)
                qp, kb = q_ref[rs, qs], k_ref[rs, qs]
                vb = v_ref[rs, vs]
                qd, kd = qd_ref[rs, qs], kd_ref[rs, qs]
                st = state[p]
                st_b = st.astype(BF16)
                rp_ref[ci, p] = st_b
                cross = _dot(qd, st_b)
                for e in range(2):
                    hd = 2 * p + e
                    hs = slice(hd * LANES, (hd + 1) * LANES)
                    es = slice(e * LANES, (e + 1) * LANES)
                    qm = jnp.where((lane // QK_DIM) == e, qp, jnp.zeros_like(qp))
                    s = _dot_nt(qm, kb) * dm_ref[hd]
                    o_h = _dot(s.astype(BF16), vb[:, es]) + cross[:, es]
                    o_ref[rs, hs] = o_h
                    xhat, _ = _head_norm(o_h)
                    g = rg_ref[rs, hs]
                    y_ref[rs, hs] = ((xhat * gain_ref[:, hs]) * (g * _sigmoid(g))).astype(BF16)
                state[p] = gs_ref[p] * st + bd_ref[...] * _dot_tn(kd, vb)

    ch = lambda w: pl.BlockSpec((tm, w), lambda n: (n, 0))
    const2 = lambda a: pl.BlockSpec(a.shape, lambda n: (0, 0))
    const3 = lambda a: pl.BlockSpec(a.shape, lambda n: (0, 0, 0))
    return pl.pallas_call(
        body,
        name="ret_fwd",
        grid=(n_ch // cps,),
        in_specs=[ch(d_qk)] * 4 + [ch(d_ret), ch(d_ret), const2(gain), const3(dmask), const3(g_state),
                                   const2(block_diag)],
        out_specs=[ch(d_ret), ch(d_ret),
                   pl.BlockSpec((cps, n_pairs, 2 * QK_DIM, 2 * LANES), lambda n: (n, 0, 0, 0))],
        out_shape=[jax.ShapeDtypeStruct((tp, d_ret), F32), jax.ShapeDtypeStruct((tp, d_ret), BF16),
                   jax.ShapeDtypeStruct((n_ch, n_pairs, 2 * QK_DIM, 2 * LANES), BF16)],
        scratch_shapes=[pltpu.VMEM((n_pairs, 2 * QK_DIM, 2 * LANES), F32)],
        compiler_params=pltpu.CompilerParams(dimension_semantics=("arbitrary",), vmem_limit_bytes=VMEM_LIMIT),
    )(qb, kb, qd, kd, vb, rg, gain, dmask, g_state, block_diag)


def _out_proj_loss(y_lru, y_ret, hp, tgt, wo, gain_f, tm):
    tp, d = hp.shape
    w_lru = y_lru.shape[1]
    w_mix = wo.shape[0]
    nt, nb = tp // tm, tm // CHUNK

    def body(*refs):
        yl_ref, yr_ref, hp_ref = refs[:3]
        tb = refs[3:3 + nb]
        wo_ref, gf_ref = refs[3 + nb:5 + nb]
        dh2_ref, dyl_ref, dyr_ref, dwo_ref, dgf_ref, loss_ref = refs[5 + nb:]
        i = pl.program_id(0)

        @pl.when(i == 0)
        def _():
            dwo_ref[...] = jnp.zeros_like(dwo_ref)
            dgf_ref[...] = jnp.zeros_like(dgf_ref)
            loss_ref[...] = jnp.zeros_like(loss_ref)

        yl, yr = yl_ref[...], yr_ref[...]
        h2 = hp_ref[...] + _dot(yl, wo_ref[0:w_lru, :]) + _dot(yr, wo_ref[w_lru:w_mix, :])
        rinv = lax.rsqrt(jnp.mean(h2 * h2, axis=-1, keepdims=True) + EPS)
        nrm = h2 * rinv
        gf = gf_ref[...]
        tgt_v = jnp.concatenate([r[...] for r in tb], axis=0)
        row = lax.broadcasted_iota(jnp.int32, (tm, 1), 0) + i * tm
        err = jnp.where(row >= CHUNK, nrm * gf - tgt_v, 0.0)
        loss_ref[...] += 0.5 * jnp.sum(jnp.mean(err * err, axis=-1, keepdims=True))
        dout = err * (1.0 / d)
        dgf_ref[...] += jnp.sum(dout * nrm, axis=0, keepdims=True)
        dn = dout * gf
        dh2 = rinv * (dn - nrm * jnp.mean(dn * nrm, axis=-1, keepdims=True))
        dh2_ref[...] = dh2
        dh2b = dh2.astype(BF16)
        dyl_ref[...] = _dot_nt(dh2b, wo_ref[0:w_lru, :])
        dyr_ref[...] = _dot_nt(dh2b, wo_ref[w_lru:w_mix, :])
        dwo_ref[0:w_lru, :] += _dot_tn(yl, dh2b)
        dwo_ref[w_lru:w_mix, :] += _dot_tn(yr, dh2b)

    tile = lambda w: pl.BlockSpec((tm, w), lambda i: (i, 0))
    t_specs = [pl.BlockSpec((CHUNK, d), functools.partial(lambda i, b: (jnp.maximum(i * nb + b - 1, 0), 0), b=b))
               for b in range(nb)]
    return pl.pallas_call(
        body,
        name="out_proj_loss",
        grid=(nt,),
        in_specs=[tile(w_lru), tile(w_mix - w_lru), tile(d)] + t_specs +
                 [pl.BlockSpec(wo.shape, lambda i: (0, 0)), pl.BlockSpec(gain_f.shape, lambda i: (0, 0))],
        out_specs=[tile(d), tile(w_lru), tile(w_mix - w_lru), pl.BlockSpec(wo.shape, lambda i: (0, 0)),
                   pl.BlockSpec((1, d), lambda i: (0, 0)), pl.BlockSpec((SUBLANES, LANES), lambda i: (0, 0))],
        out_shape=[jax.ShapeDtypeStruct((tp, d), F32), jax.ShapeDtypeStruct((tp, w_lru), F32),
                   jax.ShapeDtypeStruct((tp, w_mix - w_lru), F32), jax.ShapeDtypeStruct(wo.shape, F32),
                   jax.ShapeDtypeStruct((1, d), F32), jax.ShapeDtypeStruct((SUBLANES, LANES), F32)],
        compiler_params=pltpu.CompilerParams(dimension_semantics=("arbitrary",), vmem_limit_bytes=VMEM_LIMIT),
    )(y_lru, y_ret, hp, *([tgt] * nb), wo, gain_f)


def _ret_bwd(qb, kb, qd, kd, vb, rg, o, rprev, dy, gain, cos_t, sin_t, tables, tm, ride=None):
    tp, d_qk = qb.shape
    d_ret = vb.shape[1]
    n_ch = tp // CHUNK
    cps = tm // CHUNK
    n_pairs = HEADS // 2
    dmask, qdec, kdec, g_state, block_diag = tables

    dmask_t = jnp.swapaxes(dmask, 1, 2)

    def body(q_ref, k_ref, qdb_ref, kdb_ref, v_ref, rg_ref, o_ref, rp_ref, dy_ref, gain_ref, cos_ref, sin_ref,
             dm_ref, dmt_ref, qd_ref, kd_ref, gs_ref, bd_ref, dq_ref, dk_ref, dv_ref, drg_ref, dgain_ref, dstate):
        n = pl.program_id(0)

        @pl.when(n == 0)
        def _():
            dstate[...] = jnp.zeros_like(dstate)
            dgain_ref[...] = jnp.zeros_like(dgain_ref)

        lane = lax.broadcasted_iota(jnp.int32, (CHUNK, LANES), 1)
        for ci in reversed(range(cps)):
            rs = slice(ci * CHUNK, (ci + 1) * CHUNK)
            dq_parts, dk_parts = [], []
            for p in range(n_pairs):
                qs = slice(p * LANES, (p + 1) * LANES)
                vs = slice(p * 2 * LANES, (p + 1) * 2 * LANES)
                do_parts = []
                for e in range(2):
                    hd = 2 * p + e
                    hs = slice(hd * LANES, (hd + 1) * LANES)
                    xhat, rstd = _head_norm(o_ref[rs, hs])
                    g = rg_ref[rs, hs]
                    sg = _sigmoid(g)
                    dyh = dy_ref[rs, hs]
                    gn = gain_ref[:, hs]
                    d_on = dyh * (g * sg)
                    drg_ref[rs, hs] = (dyh * (xhat * gn) * (sg * (1.0 + g * (1.0 - sg)))).astype(BF16)
                    dgain_ref[:, hs] += jnp.sum(d_on * xhat, axis=0, keepdims=True)
                    dxh = d_on * gn
                    do_parts.append(rstd * (dxh - jnp.mean(dxh, axis=-1, keepdims=True)
                                            - xhat * jnp.mean(dxh * xhat, axis=-1, keepdims=True)))
                do_b = jnp.concatenate(do_parts, axis=1).astype(BF16)
                qp, kb = q_ref[rs, qs], k_ref[rs, qs]
                vb = v_ref[rs, vs]
                qd, kd = qdb_ref[rs, qs], kdb_ref[rs, qs]
                dst = dstate[p]
                dst_b = dst.astype(BF16)
                dqp = _dot_nt(do_b, rp_ref[ci, p]) * qd_ref[:, qs]
                dkp = _dot_nt(vb, dst_b) * kd_ref[:, qs]
                dvp = _dot(kd, dst_b)
                dv_parts = []
                for e in range(2):
                    hd = 2 * p + e
                    es = slice(e * LANES, (e + 1) * LANES)
                    mine = (lane // QK_DIM) == e
                    qm = jnp.where(mine, qp, jnp.zeros_like(qp))
                    km = jnp.where(mine, kb, jnp.zeros_like(kb))
                    ds = (_dot_nt(do_b[:, es], vb[:, es]) * dm_ref[hd]).astype(BF16)
                    s_t = (_dot_nt(kb, qm) * dmt_ref[hd]).astype(BF16)
                    ds_t = (_dot_nt(vb[:, es], do_b[:, es]) * dmt_ref[hd]).astype(BF16)
                    dv_parts.append(dvp[:, es] + _dot(s_t, do_b[:, es]))
                    dqp = dqp + _dot(ds, km)
                    dkp = dkp + _dot(ds_t, qm)
                dv_ref[rs, vs] = jnp.concatenate(dv_parts, axis=1).astype(BF16)
                dstate[p] = gs_ref[p] * dst + bd_ref[...] * _dot_tn(qd, do_b)
                dq_parts.append(dqp)
                dk_parts.append(dkp)
            cos = _tile_lanes(cos_ref[rs, :], d_qk // LANES)
            sin = _tile_lanes(sin_ref[rs, :], d_qk // LANES)
            dq = jnp.concatenate(dq_parts, axis=1)
            dk = jnp.concatenate(dk_parts, axis=1) * (QK_DIM ** -0.5)
            dq_ref[rs, :] = (dq * cos + _rot_partner(dq * sin)).astype(BF16)
            dk_ref[rs, :] = (dk * cos + _rot_partner(dk * sin)).astype(BF16)

    last = n_ch // cps - 1
    ch = lambda w: pl.BlockSpec((tm, w), lambda n: (last - n, 0))
    const2 = lambda a: pl.BlockSpec(a.shape, lambda n: (0, 0))
    const3 = lambda a: pl.BlockSpec(a.shape, lambda n: (0, 0, 0))
    return _hosted_call(
        body, ride, n_ch // cps,
        name="ret_bwd", barrier_id=1,
        in_specs=[ch(d_qk)] * 4 + [ch(d_ret), ch(d_ret), ch(d_ret),
                  pl.BlockSpec((cps, n_pairs, 2 * QK_DIM, 2 * LANES), lambda n: (last - n, 0, 0, 0)),
                  ch(d_ret), const2(gain), ch(LANES), ch(LANES),
                  const3(dmask), const3(dmask_t), const2(qdec), const2(kdec), const3(g_state), const2(block_diag)],
        out_specs=[ch(d_qk), ch(d_qk), ch(d_ret), ch(d_ret), pl.BlockSpec((1, d_ret), lambda n: (0, 0))],
        out_shape=[jax.ShapeDtypeStruct((tp, d_qk), BF16), jax.ShapeDtypeStruct((tp, d_qk), BF16),
                   jax.ShapeDtypeStruct((tp, d_ret), BF16), jax.ShapeDtypeStruct((tp, d_ret), BF16),
                   jax.ShapeDtypeStruct((1, d_ret), F32)],
        scratch_shapes=[pltpu.VMEM((n_pairs, 2 * QK_DIM, 2 * LANES), F32)],
        args=(qb, kb, qd, kd, vb, rg, o, rprev, dy, gain, cos_t, sin_t, dmask, dmask_t, qdec, kdec, g_state,
              block_diag),
    )


def _lru_bwd(lx, lg, hl, dy, saved, cw, wr, wi, lam, dgain, dgf, tm, ride=None):
    tp, w = lx.shape
    nt = tp // tm
    per8 = tm // SUBLANES
    n_heads = wr.shape[0]
    vec_row = {name: ROW_VEC + VEC_NAMES.index(name) for name in VEC_NAMES}

    def body(lx_ref, lg_ref, hl_ref, hlp_ref, dy_ref, xc_ref, r_ref, ig_ref, a_ref, beta_ref, w4_ref,
             cw_ref, wr_ref, wi_ref, lam_ref, dgain_ref, dgf_ref,
             dlx_ref, dlg_ref, pk_ref,
             g_s, b_s, p_s, carry, dxc_next, a_next):
        i = pl.program_id(0)
        first_tile = i == nt - 1
        heads = [slice(hd * LANES, (hd + 1) * LANES) for hd in range(n_heads)]

        def add_row(hd, row, value):
            pk_ref[hd, row:row + 1, :] += value

        @pl.when(i == 0)
        def _():
            carry[...] = jnp.zeros_like(carry)
            dxc_next[...] = jnp.zeros_like(dxc_next)
            a_next[...] = jnp.zeros_like(a_next)
            pk_ref[...] = jnp.zeros_like(pk_ref)
            for hd, hs in enumerate(heads):
                add_row(hd, vec_row["ret_norm_gain"], dgain_ref[:, hs])
                add_row(hd, vec_row["final_norm_gain"], dgf_ref[:, hs])

        for hd, hs in enumerate(heads):
            g = lg_ref[:, hs]
            sg = _sigmoid(g)
            dyv = dy_ref[:, hs]
            dlg_ref[:, hs] = (dyv * hl_ref[:, hs] * (sg * (1.0 + g * (1.0 - sg)))).astype(BF16)
            g_s[hd] = dyv * (g * sg)
            b_s[hd] = _shift_up(a_ref[:, hs], a_next[:, hs], 1)
        carry[0:1, :] = _segment_scan(b_s, g_s, g_s, p_s, carry[0:1, :], tm, reverse=True)
        a_next[...] = a_ref[0:SUBLANES, :]
        row = lax.broadcasted_iota(jnp.int32, (tm, 1), 0) + (nt - 1 - i) * tm
        lam_v = lam_ref[...]
        dlam_scale = LRU_C * _sigmoid(-lam_v)
        dr_scale = -LRU_C * _softplus_neg(lam_v)
        for hd, hs in enumerate(heads):
            a, beta, r, ig, xc = a_ref[:, hs], beta_ref[:, hs], r_ref[:, hs], ig_ref[:, hs], xc_ref[:, hs]
            dh = g_s[hd]
            hprev = _shift_down(hl_ref[:, hs], jnp.where(first_tile, 0.0, hlp_ref[:, hs]), 1)
            du = jnp.where(row >= PAD_ROWS, dh, 0.0)
            dbeta = du * ig * xc
            d_ig = du * beta * xc
            dxc = du * beta * ig
            dloga = (dh * hprev) * a - dbeta * w4_ref[:, hs]
            add_row(hd, vec_row["lru_lambda"], jnp.sum(dloga * r, axis=0, keepdims=True) * dlam_scale[:, hs])
            dpr = (dloga * dr_scale[:, hs]) * r * (1.0 - r)
            dpi = d_ig * ig * (1.0 - ig)
            add_row(hd, vec_row["b_rg"], jnp.sum(dpr, axis=0, keepdims=True))
            add_row(hd, vec_row["b_ig"], jnp.sum(dpi, axis=0, keepdims=True))
            xh, dprh, dpih = xc.astype(BF16), dpr.astype(BF16), dpi.astype(BF16)
            pk_ref[hd, ROW_WR:ROW_WR + LANES, :] += _dot_tn(xh, dprh)
            pk_ref[hd, ROW_WI:ROW_WI + LANES, :] += _dot_tn(xh, dpih)
            dxc = dxc + _dot_nt(dprh, wr_ref[hd].astype(BF16)) + _dot_nt(dpih, wi_ref[hd].astype(BF16))
            nxt = dxc_next[:, hs]
            up1, up2, up3 = _shift_up(dxc, nxt, 1), _shift_up(dxc, nxt, 2), _shift_up(dxc, nxt, 3)
            dlx = dxc * cw_ref[3:4, hs]
            dlx = dlx + up1 * cw_ref[2:3, hs]
            dlx = dlx + up2 * cw_ref[1:2, hs]
            dlx = dlx + up3 * cw_ref[0:1, hs]
            dlx_ref[:, hs] = dlx.astype(BF16)
            dxc_next[:, hs] = dxc[0:SUBLANES]
            lxv = lx_ref[:, hs]
            add_row(hd, vec_row["conv_b"], jnp.sum(dxc, axis=0, keepdims=True))
            for kk, shifted in enumerate((up3, up2, up1, dxc)):
                add_row(hd, ROW_CONV + kk, jnp.sum(shifted * lxv, axis=0, keepdims=True))

    last = nt - 1
    tile = pl.BlockSpec((tm, w), lambda i: (last - i, 0))
    prev = pl.BlockSpec((SUBLANES, w), lambda i: (jnp.maximum((last - i) * per8 - 1, 0), 0))
    vec = pl.BlockSpec((1, w), lambda i: (0, 0))
    mat = pl.BlockSpec(wr.shape, lambda i: (0, 0, 0))
    cwb = pl.BlockSpec(cw.shape, lambda i: (0, 0))
    packed = (n_heads, UNIT_ROWS, LANES)
    return _hosted_call(
        body, ride, nt,
        name="lru_bwd", barrier_id=2,
        in_specs=[tile, tile, tile, prev, tile] + [tile] * 6 + [cwb, mat, mat, vec, vec, vec],
        out_specs=[tile, tile, pl.BlockSpec(packed, lambda i: (0, 0, 0))],
        out_shape=[jax.ShapeDtypeStruct((tp, w), BF16), jax.ShapeDtypeStruct((tp, w), BF16),
                   jax.ShapeDtypeStruct(packed, F32)],
        scratch_shapes=[pltpu.VMEM((w // LANES, tm, LANES), F32)] * 3 + [pltpu.VMEM((SUBLANES, w), F32)] * 3,
        args=(lx, lg, hl, hl, dy, *saved, cw, wr, wi, lam, dgain, dgf),
    )


def _in_proj_dw(dparts, hp, gain, wg_shape, ride=None):
    tp, d = hp.shape
    n_ch = tp // CHUNK
    per = next(p for p in (4, 2, 5, 3, 1) if (n_ch - 1) % p == 0)
    n_steps = 1 + (n_ch - 1) // per
    widths = [p.shape[1] for p in dparts]
    segs = _proj_segments(widths[0], widths[2], widths[4], wg_shape[2])

    def body(*refs):
        dp = [refs[p * per:(p + 1) * per] for p in range(6)]
        hp_b = refs[6 * per:7 * per]
        g_ref, dwg_ref, acc, sem = refs[7 * per:]
        i = pl.program_id(0)

        def accumulate(blocks):
            h = jnp.concatenate([hp_b[b][...] for b in blocks], axis=0)
            rinv = lax.rsqrt(jnp.mean(h * h, axis=-1, keepdims=True) + EPS)
            u = ((h * rinv) * g_ref[...]).astype(BF16)
            for p_refs, parts in zip(dp, segs):
                for jj, inner, off, take in parts:
                    seg = jnp.concatenate([p_refs[b][:, off:off + take] for b in blocks], axis=0)
                    acc[jj, :, inner:inner + take] += _dot_tn(u, seg)

        @pl.when(i == 0)
        def _():
            acc[...] = jnp.zeros_like(acc)
            accumulate([0])

        @pl.when(i > 0)
        def _():
            accumulate(list(range(per)))

        @pl.when(i == n_steps - 1)
        def _():
            cp = pltpu.make_async_copy(acc, dwg_ref, sem)
            cp.start()
            cp.wait()

    def blocks(w):
        return [pl.BlockSpec((CHUNK, w), functools.partial(
            lambda i, b: (jnp.where(i == 0, b, per * (i - 1) + 1 + b), 0), b=b)) for b in range(per)]

    in_specs, args = [], []
    for a, w in list(zip(dparts, widths)) + [(hp, d)]:
        in_specs += blocks(w)
        args += [a] * per
    outs, rides = _hosted_call(
        body, ride, n_steps,
        name="in_proj_dw", barrier_id=3,
        in_specs=in_specs + [pl.BlockSpec(gain.shape, lambda i: (0, 0))],
        out_specs=[ANY_SPEC],
        out_shape=[jax.ShapeDtypeStruct(wg_shape, F32)],
        scratch_shapes=[pltpu.VMEM(wg_shape, F32), pltpu.SemaphoreType.DMA],
        args=(*args, gain),
    )
    return outs[0], rides


def _in_proj_dx(dparts, hp, dh2, gain, wg, s_len, tm, ride=None):
    tp, d = hp.shape
    nt = tp // tm
    widths = [p.shape[1] for p in dparts]
    segs = _proj_segments(widths[0], widths[2], widths[4], wg.shape[2])

    def body(*refs):
        dp = refs[:6]
        hp_ref, dh2_ref, g_ref, w_ref = refs[6:10]
        gx_ref, dmeta_ref, dg_ref = refs[10:13]
        stage, sems = refs[13:]
        i = pl.program_id(0)

        @pl.when(i == 0)
        def _():
            dg_ref[...] = jnp.zeros_like(dg_ref)

        h = hp_ref[...]
        rinv = lax.rsqrt(jnp.mean(h * h, axis=-1, keepdims=True) + EPS)
        nrm = h * rinv
        gv = g_ref[...]
        du = jnp.zeros((tm, d), F32)
        for p_ref, parts in zip(dp, segs):
            for jj, inner, off, take in parts:
                du = du + _dot_nt(p_ref[:, off:off + take], w_ref[jj, :, inner:inner + take])
        dg_ref[...] += jnp.sum(du * nrm, axis=0, keepdims=True)
        dn = du * gv
        dh = dh2_ref[...] + rinv * (dn - nrm * jnp.mean(dn * nrm, axis=-1, keepdims=True))

        def first_copy():
            return pltpu.make_async_copy(stage.at[0, pl.ds(CHUNK, tm - CHUNK), :],
                                         gx_ref.at[pl.ds(0, tm - CHUNK), :], sems.at[0])

        def tile_copy(slot, start):
            return pltpu.make_async_copy(stage.at[slot], gx_ref.at[pl.ds(start, tm), :], sems.at[slot])

        @pl.when(i == 0)
        def _():
            dmeta_ref[...] = dh[PAD_ROWS:CHUNK]
            stage[0] = dh
            first_copy().start()

        @pl.when(i > 0)
        def _():
            slot = 1 + i % 2

            @pl.when(i >= 3)
            def _():
                tile_copy(slot, 0).wait()

            stage[slot] = dh
            tile_copy(slot, pl.multiple_of(i * tm - CHUNK, CHUNK)).start()

        @pl.when(i == nt - 1)
        def _():
            first_copy().wait()
            for step in (nt - 2, nt - 1):
                if step >= 1:
                    tile_copy(1 + step % 2, 0).wait()

    tile = lambda w: pl.BlockSpec((tm, w), lambda i: (i, 0))
    return _hosted_call(
        body, ride, nt,
        name="in_proj_dx", barrier_id=4,
        in_specs=[tile(w) for w in widths] + [tile(d), tile(d), pl.BlockSpec(gain.shape, lambda i: (0, 0)),
                                              pl.BlockSpec(wg.shape, lambda i: (0, 0, 0))],
        out_specs=[ANY_SPEC, pl.BlockSpec((N_META, d), lambda i: (0, 0)), pl.BlockSpec((1, d), lambda i: (0, 0))],
        out_shape=[jax.ShapeDtypeStruct((s_len, d), F32), jax.ShapeDtypeStruct((N_META, d), F32),
                   jax.ShapeDtypeStruct((1, d), F32)],
        scratch_shapes=[pltpu.VMEM((3, tm, d), F32), pltpu.SemaphoreType.DMA((3,))],
        args=(*dparts, hp, dh2, gain, wg),
    )


def _pair_sum(buf, recv, c_arr, tr, name):
    _, rows, cols = buf.shape

    def body(c_ref, mine_ref, got_ref, out_ref):
        out_ref[...] = (mine_ref[...] + got_ref[...]).astype(BF16)

    grid_spec = pltpu.PrefetchScalarGridSpec(
        num_scalar_prefetch=1,
        grid=(N_CHIPS, rows // tr),
        in_specs=[pl.BlockSpec((1, tr, cols), lambda jj, r, c_ref: (2 * jj + c_ref[0], r, 0)),
                  pl.BlockSpec((1, tr, cols), lambda jj, r, c_ref: (jj, r, 0))],
        out_specs=pl.BlockSpec((1, tr, cols), lambda jj, r, c_ref: (jj, r, 0)),
    )
    return pl.pallas_call(
        body,
        name=name,
        grid_spec=grid_spec,
        out_shape=jax.ShapeDtypeStruct((N_CHIPS, rows, cols), BF16),
    )(c_arr, buf, recv)


def _pair_exchange_sum(buf, c_arr, tr, name):
    _, rows, cols = buf.shape
    per = rows // tr

    def body(c_ref, src_ref, mine_ref, out_ref, got, send_sems, recv_sems):
        jj, r = pl.program_id(0), pl.program_id(1)
        x, y, c, _ = _position()
        copies = [_remote(src_ref.at[2 * k + 1 - c], got.at[k], send_sems, recv_sems, k, (x, y, 1 - c))
                  for k in range(N_CHIPS)]

        @pl.when((jj == 0) & (r == 0))
        def _():
            _peer_barrier([(x, y, 1 - c)])
            for cp in copies:
                cp.start()

        for k in range(N_CHIPS):
            @pl.when((jj == k) & (r == 0))
            def _():
                copies[k].wait_recv()

        rows_r = pl.ds(pl.multiple_of(r * tr, tr), tr)
        out_ref[0] = (mine_ref[0] + got[jj, rows_r, :]).astype(BF16)

        @pl.when((jj == N_CHIPS - 1) & (r == per - 1))
        def _():
            for cp in copies:
                cp.wait_send()

    grid_spec = pltpu.PrefetchScalarGridSpec(
        num_scalar_prefetch=1,
        grid=(N_CHIPS, per),
        in_specs=[ANY_SPEC, pl.BlockSpec((1, tr, cols), lambda jj, r, c_ref: (2 * jj + c_ref[0], r, 0))],
        out_specs=pl.BlockSpec((1, tr, cols), lambda jj, r, c_ref: (jj, r, 0)),
        scratch_shapes=[pltpu.VMEM((N_CHIPS, rows, cols), F32), pltpu.SemaphoreType.DMA((N_CHIPS,)),
                        pltpu.SemaphoreType.DMA((N_CHIPS,))],
    )
    return pl.pallas_call(
        body,
        name=name,
        grid_spec=grid_spec,
        out_shape=jax.ShapeDtypeStruct((N_CHIPS, rows, cols), BF16),
        compiler_params=pltpu.CompilerParams(dimension_semantics=("arbitrary", "arbitrary"),
                                             vmem_limit_bytes=VMEM_LIMIT, collective_id=5),
    )(c_arr, buf, buf)


def _chip_sum(mine, got, j_arr, tr, name, loss_part=None):
    _, rows, cols = got.shape
    extra = [] if loss_part is None else [loss_part]

    def body(j_ref, mine_ref, got_ref, *rest):
        out_ref = rest[-1]
        j = j_ref[0]
        acc = None
        for jj in range(N_CHIPS):
            term = jnp.where(j == jj, mine_ref[0], got_ref[jj]).astype(F32)
            acc = term if acc is None else acc + term
        out_ref[...] = acc
        if loss_part is not None:
            out_ref[ROW_LOSS:ROW_LOSS + 1, :] = rest[0][0:1, :]

    grid_spec = pltpu.PrefetchScalarGridSpec(
        num_scalar_prefetch=1,
        grid=(rows // tr,),
        in_specs=[pl.BlockSpec((1, tr, cols), lambda r, j_ref: (j_ref[0], r, 0)),
                  pl.BlockSpec((N_CHIPS, tr, cols), lambda r, j_ref: (0, r, 0))] +
                 [pl.BlockSpec(e.shape, lambda r, j_ref: (0, 0)) for e in extra],
        out_specs=pl.BlockSpec((tr, cols), lambda r, j_ref: (r, 0)),
    )
    return pl.pallas_call(
        body,
        name=name,
        grid_spec=grid_spec,
        out_shape=jax.ShapeDtypeStruct((rows, cols), F32),
    )(j_arr, mine, got, *extra)


def _finish_exchange(f_in, f_small):
    def body(fin_ref, fs_ref, rin_ref, os_ref, send_sems, recv_sems, local_sem):
        x, y, c, chips = _position()
        j = 2 * x + y
        me = 2 * j + c
        sibling = (x, y, 1 - c)
        _peer_barrier([sibling] + [(cx, cy, c) for cx, cy in chips])
        local = pltpu.make_async_copy(fs_ref, os_ref.at[me], local_sem)
        local.start()

        def copy(k, src, dst, to):
            return _remote(src, dst, send_sems, recv_sems, k, to)

        first = [copy(0, fin_ref, rin_ref, sibling), copy(1, fs_ref, os_ref.at[me], sibling)]
        first += [copy(2 + k, fs_ref, os_ref.at[me], (cx, cy, c)) for k, (cx, cy) in enumerate(chips)]
        for cp in first:
            cp.start()
        passed = []
        for k, (cx, cy) in enumerate(chips):
            unit = 2 * (2 * cx + cy) + c
            copy(2 + k, fs_ref, os_ref.at[unit], sibling).wait_recv()
            fwd = copy(5 + k, os_ref.at[unit], os_ref.at[unit], sibling)
            fwd.start()
            passed.append(fwd)
        copy(0, fin_ref, rin_ref, sibling).wait_recv()
        copy(1, fs_ref, os_ref.at[2 * j + 1 - c], sibling).wait_recv()
        for k, (cx, cy) in enumerate(chips):
            unit = 2 * (2 * cx + cy) + 1 - c
            copy(5 + k, fs_ref, os_ref.at[unit], sibling).wait_recv()
        for cp in first + passed:
            cp.wait_send()
        local.wait()

    return pl.pallas_call(
        body,
        name="grad_finish_exchange",
        in_specs=[ANY_SPEC] * 2,
        out_specs=[ANY_SPEC] * 2,
        out_shape=[jax.ShapeDtypeStruct(f_in.shape, F32), jax.ShapeDtypeStruct((N_DEV,) + f_small.shape, F32)],
        scratch_shapes=[pltpu.SemaphoreType.DMA((8,)), pltpu.SemaphoreType.DMA((8,)), pltpu.SemaphoreType.DMA],
        compiler_params=pltpu.CompilerParams(collective_id=8),
    )(f_in, f_small)


def _adamw_math(w, g, m, v):
    m = ADAM_B1 * m + (1.0 - ADAM_B1) * g
    v = ADAM_B2 * v + (1.0 - ADAM_B2) * (g * g)
    m_hat = m / (1.0 - ADAM_B1 ** ADAM_STEP)
    v_hat = v / (1.0 - ADAM_B2 ** ADAM_STEP)
    delta = -ADAM_LR * (m_hat / (jnp.sqrt(v_hat) + ADAM_EPS) + ADAM_WD * w)
    return delta, m, v


def _adamw_big(w, g_mine, g_sib, m, v, c_arr, tr, name):
    rows, cols = w.shape
    half = rows // 2
    per = half // tr

    def body(c_ref, w_ref, gm_ref, gs_ref, m_ref, v_ref, g_ref, d_ref, mo_ref, vo_ref):
        g = jnp.where(pl.program_id(0) == c_ref[0], gm_ref[...], gs_ref[...])
        g_ref[...] = g
        d_ref[...], mo_ref[...], vo_ref[...] = _adamw_math(w_ref[...], g, m_ref[...], v_ref[...])

    full = pl.BlockSpec((tr, cols), lambda h, r, c_ref: (h * per + r, 0))
    unit = pl.BlockSpec((tr, cols), lambda h, r, c_ref: (r, 0))
    grid_spec = pltpu.PrefetchScalarGridSpec(
        num_scalar_prefetch=1,
        grid=(2, per),
        in_specs=[full, unit, unit, full, full],
        out_specs=[full] * 4,
    )
    return pl.pallas_call(
        body,
        name=name,
        grid_spec=grid_spec,
        out_shape=[jax.ShapeDtypeStruct(w.shape, F32)] * 4,
    )(c_arr, w, g_mine, g_sib, m, v)


def _adamw_small(j_arr, packed, params):
    names = list(params)
    n = len(names)

    def body(j_ref, pk_ref, *refs):
        ins = refs[:3 * n]
        outs = refs[3 * n:]
        j = j_ref[0]

        def shard(row, rows):
            return jnp.concatenate([pk_ref[2 * j, row:row + rows, :], pk_ref[2 * j + 1, row:row + rows, :]], axis=1)

        def tail_sum(unit, row, rows):
            start = pl.multiple_of(UNIT_ROWS + TAIL_ROWS * unit + row, SUBLANES)
            total = pk_ref[0, pl.ds(start, rows), :]
            for dev in range(1, N_DEV):
                total = total + pk_ref[dev, pl.ds(start, rows), :]
            return total

        for idx, name in enumerate(names):
            if name == "w_rg":
                g = pk_ref[:, ROW_WR:ROW_WR + LANES, :]
            elif name == "w_ig":
                g = pk_ref[:, ROW_WI:ROW_WI + LANES, :]
            elif name == "meta_tokens":
                g = jnp.concatenate([tail_sum(2 * j, 0, N_META), tail_sum(2 * j + 1, 0, N_META)], axis=1)
            elif name == "norm_gain":
                g = jnp.concatenate([tail_sum(u, N_META, SUBLANES)[0:1] for u in range(N_DEV)], axis=1)
            elif name == "conv_w":
                g = shard(ROW_CONV, 4)
            else:
                row = ROW_VEC + VEC_NAMES.index(name)
                g = jnp.concatenate([pk_ref[u, row:row + 1, :] for u in range(N_DEV)], axis=1)
            w_ref, m_ref, v_ref = ins[3 * idx:3 * idx + 3]
            delta, m, v = _adamw_math(w_ref[...], g, m_ref[...], v_ref[...])
            g_ref, d_ref, mo_ref, vo_ref = outs[4 * idx:4 * idx + 4]
            g_ref[...], d_ref[...], mo_ref[...], vo_ref[...] = g, delta, m, v
        total = pk_ref[0, ROW_LOSS:ROW_LOSS + 1, :]
        for u in range(1, N_DEV):
            total = total + pk_ref[u, ROW_LOSS:ROW_LOSS + 1, :]
        outs[4 * n][...] = jnp.broadcast_to(total, (SUBLANES, LANES))

    flat_in, out_shape = [], []
    for name in names:
        w, m, v = params[name]
        flat_in += [w, m, v]
        out_shape += [jax.ShapeDtypeStruct(w.shape, F32)] * 4
    out_shape.append(jax.ShapeDtypeStruct((SUBLANES, LANES), F32))
    res = pl.pallas_call(
        body,
        name="adamw_small",
        in_specs=[SMEM_SPEC, VMEM_SPEC] + [VMEM_SPEC] * (3 * n),
        out_specs=[VMEM_SPEC] * (4 * n + 1),
        out_shape=out_shape,
    )(j_arr, packed, *flat_in)
    return {name: tuple(res[4 * idx:4 * idx + 4]) for idx, name in enumerate(names)}, res[4 * n][0, 0]


def _units(a):
    rows = a.shape[0]
    return jnp.transpose(a.reshape(rows, N_DEV, LANES), (1, 0, 2))


def kernel(x, meta_tokens, norm_gain, w_in, conv_w, conv_b, w_rg, b_rg, w_ig, b_ig, lru_lambda, ret_norm_gain, w_out, final_norm_gain, loss_target, m_meta_tokens, m_norm_gain, m_w_in, m_conv_w, m_conv_b, m_w_rg, m_b_rg, m_w_ig, m_b_ig, m_lru_lambda, m_ret_norm_gain, m_w_out, m_final_norm_gain, v_meta_tokens, v_norm_gain, v_w_in, v_conv_w, v_conv_b, v_w_rg, v_b_rg, v_w_ig, v_b_ig, v_lru_lambda, v_ret_norm_gain, v_w_out, v_final_norm_gain):
    s_len, d = x.shape[1], x.shape[2]
    d_lru = w_rg.shape[1] * w_rg.shape[2]
    d_ret = ret_norm_gain.shape[1]
    d_qk = HEADS * QK_DIM
    tp = s_len + CHUNK
    tm = TOKEN_TILE
    assert tp % tm == 0 and d_lru == HEADS * LANES and d_ret == HEADS * LANES
    ax, ay, ac = lax.axis_index("x"), lax.axis_index("y"), lax.axis_index("c")
    c_arr = jnp.reshape(ac, (1,)).astype(jnp.int32)
    j_arr = jnp.reshape(2 * ax + ay, (1,)).astype(jnp.int32)

    small = jnp.concatenate([meta_tokens, conv_w[0], jnp.zeros((4, meta_tokens.shape[1]), F32)], axis=0)
    wg, sg = _gather_weights(w_in[0], small)
    cols = sg.shape[2]
    meta_full = jnp.transpose(sg[:, :N_META, :], (1, 0, 2)).reshape(N_META, N_CHIPS * cols)
    cw_full = jnp.transpose(sg[:, N_META:N_META + 4, :], (1, 0, 2)).reshape(4, N_CHIPS * cols)
    cw8 = jnp.concatenate([cw_full, jnp.zeros((4, cw_full.shape[1]), F32)], axis=0)

    half = QK_DIM // 2
    inv = ROPE_BASE ** (-jnp.arange(half, dtype=F32) / half)
    pos = (jnp.arange(tp) - PAD_ROWS).astype(F32)
    ang = pos[:, None] * inv[None, :]
    cos_t = jnp.tile(jnp.cos(ang), (1, LANES // half))
    sign = jnp.where((jnp.arange(LANES) % QK_DIM) < half, -1.0, 1.0).astype(F32)
    sin_t = jnp.tile(jnp.sin(ang), (1, LANES // half)) * sign[None, :]
    tables = _ret_tables()
    gain_f = final_norm_gain.reshape(1, d)

    hp, lx, lg, *qkv, rg, wo4 = _in_proj(x[0], meta_full, norm_gain, wg, cos_t, sin_t, w_out[0], tables[1], tables[2],
                                         tm, d_lru, d_qk, d_ret)
    wo = wo4.reshape(N_CHIPS * wo4.shape[1], wo4.shape[2])
    hl, y_lru, *lru_saved = _lru_fwd(lx, lg, cw8, conv_b, w_rg[0], b_rg, w_ig[0], b_ig, lru_lambda, tm)
    o, y_ret, rprev = _ret_fwd(*qkv, rg, ret_norm_gain, tables, tm)
    dh2, dy_lru, dy_ret, dwo, dgf, loss_acc = _out_proj_loss(y_lru, y_ret, hp, loss_target[0], wo, gain_f, tm)

    g_out = dwo.reshape(N_DEV, dwo.shape[0] // N_DEV, dwo.shape[1])
    (dq, dk, dv, drg, dgain), (r_out,) = _ret_bwd(*qkv, rg, o, rprev, dy_ret, ret_norm_gain, cos_t, sin_t, tables,
                                                 tm, ride=_pair_ride([g_out]))
    q_out = _pair_sum(g_out, r_out, c_arr, REDUCE_TILE, "grad_pair_sum_out")
    (dlx, dlg, g_small), (e_out,) = _lru_bwd(lx, lg, hl, dy_lru, lru_saved, cw8, w_rg[0], w_ig[0], lru_lambda,
                                            dgain, dgf, tm, ride=_chip_ride([q_out]))
    f_out = _chip_sum(q_out, e_out, j_arr, REDUCE_TILE, "grad_chip_sum_out")
    dparts = [dlx, dlg, dq, dk, dv, drg]
    dwg, (s_out, r_small) = _in_proj_dw(dparts, hp, norm_gain, wg.shape,
                                        ride=_join_rides(_sibling_ride([f_out]), _pair_ride([g_small])))
    g_in = dwg.reshape(N_DEV, dwg.shape[1] // 2, dwg.shape[2])
    q_in = _pair_exchange_sum(g_in, c_arr, REDUCE_TILE, "grad_pair_exchange_sum_in")
    q_small = _pair_sum(g_small, r_small, c_arr, UNIT_ROWS, "grad_pair_sum_small")
    (grad_x, dmeta, dg1), (e_in, e_small) = _in_proj_dx(dparts, hp, dh2, norm_gain, wg, s_len, tm,
                                                        ride=_chip_ride([q_in, q_small]))
    f_in = _chip_sum(q_in, e_in, j_arr, REDUCE_TILE, "grad_chip_sum_in")
    f_small = _chip_sum(q_small, e_small, j_arr, UNIT_ROWS, "grad_chip_sum_small", loss_part=loss_acc)
    tail = jnp.concatenate([_units(dmeta), _units(dg1), jnp.zeros((N_DEV, TAIL_ROWS - N_META - 1, LANES), F32)],
                           axis=1).reshape(N_DEV * TAIL_ROWS, LANES)
    s_in, o_small = _finish_exchange(f_in, jnp.concatenate([f_small, tail], axis=0))

    res_in = _adamw_big(w_in[0], f_in, s_in, m_w_in[0], v_w_in[0], c_arr, REDUCE_TILE, "adamw_w_in")
    res_out = _adamw_big(w_out[0], f_out, s_out, m_w_out[0], v_w_out[0], c_arr, REDUCE_TILE, "adamw_w_out")
    small_params = {
        "meta_tokens": (meta_tokens, m_meta_tokens, v_meta_tokens),
        "norm_gain": (norm_gain, m_norm_gain, v_norm_gain),
        "conv_w": (conv_w[0], m_conv_w[0], v_conv_w[0]),
        "conv_b": (conv_b, m_conv_b, v_conv_b),
        "w_rg": (w_rg[0], m_w_rg[0], v_w_rg[0]),
        "b_rg": (b_rg, m_b_rg, v_b_rg),
        "w_ig": (w_ig[0], m_w_ig[0], v_w_ig[0]),
        "b_ig": (b_ig, m_b_ig, v_b_ig),
        "lru_lambda": (lru_lambda, m_lru_lambda, v_lru_lambda),
        "ret_norm_gain": (ret_norm_gain, m_ret_norm_gain, v_ret_norm_gain),
        "final_norm_gain": (gain_f, m_final_norm_gain.reshape(1, d), v_final_norm_gain.reshape(1, d)),
    }
    res, loss = _adamw_small(j_arr, o_small, small_params)
    res["w_in"] = tuple(res_in)
    res["w_out"] = tuple(res_out)

    order = ["meta_tokens", "norm_gain", "w_in", "conv_w", "conv_b", "w_rg", "b_rg", "w_ig", "b_ig", "lru_lambda",
             "ret_norm_gain", "w_out", "final_norm_gain"]
    shapes = {"w_in": w_in.shape, "conv_w": conv_w.shape, "w_rg": w_rg.shape, "w_ig": w_ig.shape,
              "w_out": w_out.shape, "final_norm_gain": final_norm_gain.shape}
    outs = [loss, grad_x.reshape(x.shape)]
    for kind in range(4):
        for name in order:
            a = res[name][kind]
            outs.append(a.reshape(shapes[name]) if name in shapes else a)
    return tuple(outs)
```
